```python
import math
import jax, jax.numpy as jnp
from jax import lax
import numpy as np

D_MODEL = 1024
BATCH = 8
SEQ = 2048
DEPTH = 1
DEC_BATCH = 128
DEC_SEQ = 1
PAST_LEN = 16384
PAGE_SIZE = 128

LRU_W = D_MODEL
LRU_HEADS = 16
LRU_BLOCK = LRU_W // LRU_HEADS
LRU_C = 8.0
CONV_W = 4
SSD_INNER = D_MODEL
SSD_HEADS = 16
SSD_HEAD_DIM = SSD_INNER // SSD_HEADS
SSD_GROUPS = 2
SSD_STATE = 128
SSD_CHUNK = 128
SSD_CONV_DIM = SSD_INNER + 2 * SSD_GROUPS * SSD_STATE
MIX_W = LRU_W + SSD_INNER
IN_COLS = 2 * LRU_W + SSD_INNER + SSD_CONV_DIM + SSD_HEADS
SPLITS = (LRU_W, 2 * LRU_W, 2 * LRU_W + SSD_INNER, 2 * LRU_W + SSD_INNER + SSD_CONV_DIM)
N_EXPERT_GROUPS = 4
EXPERTS_PER_GROUP = 8
N_EXPERTS = N_EXPERT_GROUPS * EXPERTS_PER_GROUP
TOP_K = 2
D_FF_EXPERT = D_MODEL // 2
MOE_BLOCK = 128
N_MOD = 6
LN_EPS = 1e-5
RMS_EPS = 1e-6

kernel_name = 'hymba_rglru_ssd_hmoe_step'


def layer_norm(x, g=None, b=None):
    xf = x.astype(jnp.float32)
    xc = xf - xf.mean(-1, keepdims=True)
    y = xc * lax.rsqrt((xc * xc).mean(-1, keepdims=True) + LN_EPS)
    if g is not None:
        y = y * g.astype(jnp.float32) + b.astype(jnp.float32)
    return y.astype(x.dtype)


def rms_norm(x, g):
    xf = x.astype(jnp.float32)
    y = xf * lax.rsqrt((xf * xf).mean(-1, keepdims=True) + RMS_EPS) * g.astype(jnp.float32)
    return y.astype(x.dtype)


def causal_conv(x, buf, w, b):
    L = x.shape[1]
    xp = jnp.concatenate([buf.astype(x.dtype), x], axis=1)
    y = xp[:, 0:L] * w[0]
    for k in range(1, CONV_W):
        y = y + xp[:, k:k + L] * w[k]
    return y + b, xp[:, -(CONV_W - 1):]


def rglru(x, h0, wa, ba, wx, bx, lam):
    b, L, W = x.shape
    xf = x.astype(jnp.float32)
    xb = xf.reshape(b, L, LRU_HEADS, LRU_BLOCK)
    r = jax.nn.sigmoid(jnp.einsum('blhi,hij->blhj', xb, wa.astype(jnp.float32)) + ba).reshape(b, L, W)
    i = jax.nn.sigmoid(jnp.einsum('blhi,hij->blhj', xb, wx.astype(jnp.float32)) + bx).reshape(b, L, W)
    log_a = -LRU_C * r * jax.nn.softplus(-lam.astype(jnp.float32))
    a = jnp.exp(log_a)
    u = jnp.sqrt(-jnp.expm1(2.0 * log_a)) * (i * xf)
    u = u.at[:, 0].add(a[:, 0] * h0.astype(jnp.float32))

    def combine(left, right):
        a_l, u_l = left
        a_r, u_r = right
        return a_l * a_r, a_r * u_l + u_r

    _, h = lax.associative_scan(combine, (a, u), axis=1)
    return h, h[:, -1]


def ssd_chunked(x, dt, A, bm, cm, s0):
    b, L, H, P = x.shape
    G, N = bm.shape[2], bm.shape[3]
    HG = H // G
    Q = SSD_CHUNK if L >= SSD_CHUNK else L
    nc = -(-L // Q)
    pad = nc * Q - L
    if pad:
        def padl(t):
            return jnp.pad(t, [(0, 0), (0, pad)] + [(0, 0)] * (t.ndim - 2))
        x, dt, bm, cm = padl(x), padl(dt), padl(bm), padl(cm)
    xc = x.reshape(b, nc, Q, G, HG, P)
    dtc = dt.reshape(b, nc, Q, G, HG)
    bc = bm.reshape(b, nc, Q, G, N)
    cc = cm.reshape(b, nc, Q, G, N)
    acs = jnp.cumsum(dtc * A.reshape(G, HG), axis=2)
    xdt = xc * dtc[..., None]
    causal = jnp.tril(jnp.ones((Q, Q), bool))[None, None, :, :, None, None]
    seg = acs[:, :, :, None] - acs[:, :, None, :]
    decay = jnp.where(causal, jnp.exp(jnp.where(causal, seg, 0.0)), 0.0)
    cb = jnp.einsum('bclgn,bcsgn->bclsg', cc, bc)
    y_diag = jnp.einsum('bclsgj,bcsgjp->bclgjp', cb[..., None] * decay, xdt)
    decay_end = jnp.exp(acs[:, :, -1:] - acs)
    chunk_states = jnp.einsum('bclgn,bclgj,bclgjp->bcgjpn', bc, decay_end, xdt)
    chunk_decay = jnp.exp(acs[:, :, -1])

    def step(s, inp):
        cd, cs = inp
        return cd[..., None, None] * s + cs, s

    s_fin, s_in = lax.scan(step, s0.reshape(b, G, HG, P, N),
                           (jnp.moveaxis(chunk_decay, 1, 0), jnp.moveaxis(chunk_states, 1, 0)))
    s_in = jnp.moveaxis(s_in, 0, 1)
    y_off = jnp.einsum('bclgn,bcgjpn,bclgj->bclgjp', cc, s_in, jnp.exp(acs))
    y = (y_diag + y_off).reshape(b, nc * Q, H, P)[:, :L]
    return y, s_fin.reshape(b, H, P, N)


def mixer(u, lru_conv0, lru_h0, ssd_conv0, ssd_s0, w_in, lru_conv_w, lru_conv_b, lru_wa, lru_ba,
          lru_wx, lru_bx, lru_lambda, lru_norm_g, ssd_conv_w, ssd_conv_b, ssd_dt_bias, ssd_a_log,
          ssd_d, ssd_norm_g, w_out):
    b, L, _ = u.shape
    proj = u @ w_in
    x_l, g_l, z, xbc, dt_raw = jnp.split(proj, SPLITS, axis=-1)
    x_lc, lru_conv_new = causal_conv(x_l, lru_conv0, lru_conv_w, lru_conv_b)
    h, h_last = rglru(x_lc, lru_h0, lru_wa, lru_ba, lru_wx, lru_bx, lru_lambda)
    y_l = rms_norm(h.astype(u.dtype) * jax.nn.gelu(g_l), lru_norm_g)
    xbc_c, ssd_conv_new = causal_conv(xbc, ssd_conv0, ssd_conv_w, ssd_conv_b)
    xbc_c = jax.nn.silu(xbc_c).astype(jnp.float32)
    gn = SSD_GROUPS * SSD_STATE
    xh = xbc_c[..., :SSD_INNER].reshape(b, L, SSD_HEADS, SSD_HEAD_DIM)
    bm = xbc_c[..., SSD_INNER:SSD_INNER + gn].reshape(b, L, SSD_GROUPS, SSD_STATE)
    cm = xbc_c[..., SSD_INNER + gn:].reshape(b, L, SSD_GROUPS, SSD_STATE)
    dt = jax.nn.softplus(dt_raw.astype(jnp.float32) + ssd_dt_bias.astype(jnp.float32))
    a = -jnp.exp(ssd_a_log.astype(jnp.float32))
    y, s_last = ssd_chunked(xh, dt, a, bm, cm, ssd_s0.astype(jnp.float32))
    y = (y + ssd_d.astype(jnp.float32)[:, None] * xh).reshape(b, L, SSD_INNER).astype(u.dtype)
    y_s = rms_norm(y * jax.nn.silu(z), ssd_norm_g)
    out = jnp.concatenate([y_l, y_s], axis=-1) @ w_out
    return out, (lru_conv_new, h_last.astype(u.dtype), ssd_conv_new, s_last.astype(u.dtype))


def hmoe(v, w_rg, b_rg, w_re, b_re, w_gate, w_up, w_down):
    T, D = v.shape
    vf = v.astype(jnp.float32)
    lg = vf @ w_rg.astype(jnp.float32) + b_rg.astype(jnp.float32)
    pg = jax.nn.softmax(lg, axis=-1)
    g_sel = jnp.argmax(lg, axis=-1)
    le = (vf @ w_re.astype(jnp.float32) + b_re.astype(jnp.float32)).reshape(T, N_EXPERT_GROUPS, EXPERTS_PER_GROUP)
    le_sel = jnp.take_along_axis(le, jnp.broadcast_to(g_sel[:, None, None], (T, 1, EXPERTS_PER_GROUP)), axis=1)[:, 0]
    top_v, top_i = lax.top_k(le_sel, TOP_K)
    gate = jnp.take_along_axis(pg, g_sel[:, None], axis=1) * jax.nn.softmax(top_v, axis=-1)
    eid = (g_sel[:, None] * EXPERTS_PER_GROUP + top_i).reshape(-1).astype(jnp.int32)
    tok = jnp.repeat(jnp.arange(T, dtype=jnp.int32), TOP_K)
    gw = gate.reshape(-1)
    order = jnp.argsort(eid)
    eid_s, tok_s, gw_s = eid[order], tok[order], gw[order]
    A = T * TOP_K
    counts = jnp.zeros((N_EXPERTS,), jnp.int32).at[eid].add(1)
    padded = ((counts + MOE_BLOCK - 1) // MOE_BLOCK) * MOE_BLOCK
    off = jnp.cumsum(counts) - counts
    pend = jnp.cumsum(padded)
    poff = pend - padded
    dest = poff[eid_s] + (jnp.arange(A, dtype=jnp.int32) - off[eid_s])
    NB = (A + N_EXPERTS * (MOE_BLOCK - 1) + MOE_BLOCK - 1) // MOE_BLOCK
    P = NB * MOE_BLOCK
    xpad = jnp.zeros((P, D), v.dtype).at[dest].set(v[tok_s])
    blk_e = jnp.clip(jnp.searchsorted(pend, jnp.arange(NB, dtype=jnp.int32) * MOE_BLOCK, side='right'), 0, N_EXPERTS - 1)

    def expert_block(args):
        xb, e = args
        hb = jax.nn.silu(xb @ w_gate[e]) * (xb @ w_up[e])
        return hb @ w_down[e]

    ypad = lax.map(expert_block, (xpad.reshape(NB, MOE_BLOCK, D), blk_e)).reshape(P, D)
    ys = (ypad[dest] * gw_s[:, None]).astype(v.dtype)
    return jnp.zeros((T, D), v.dtype).at[tok_s].add(ys)


def ada_mod(c, w, b):
    m = jax.nn.silu(c) @ w + b
    return m.reshape(c.shape[0], N_MOD, 1, D_MODEL)


def modulate(x, shift, scale):
    return layer_norm(x) * (1.0 + scale) + shift


def setup_inputs(seed: int = 0) -> dict:
    key = jax.random.key(seed)
    ks = iter(jax.random.split(key, 64))
    f32 = jnp.float32

    def nrm(shape, scale):
        return jax.random.normal(next(ks), shape, f32) * scale

    def unif(shape, lo, hi):
        return jax.random.uniform(next(ks), shape, f32, lo, hi)

    beta = (8.0 * DEPTH) ** -0.25
    Dp = DEPTH
    lru_a = unif((Dp, LRU_W), 0.9, 0.999) ** (1.0 / LRU_C)
    lru_lambda = jnp.log(lru_a) - jnp.log1p(-lru_a)
    dt0 = jnp.exp(unif((Dp, SSD_HEADS), math.log(1e-3), math.log(1e-1)))
    dt_bias = dt0 + jnp.log(-jnp.expm1(-dt0))
    return {
        'x_prompt': nrm((BATCH, SEQ, D_MODEL), 1.0),
        'x_sample': nrm((DEC_BATCH, DEC_SEQ, D_MODEL), 1.0),
        'c_prompt': nrm((BATCH, D_MODEL), 1.0),
        'c_sample': nrm((DEC_BATCH, D_MODEL), 1.0),
        'state_lru_conv': nrm((Dp, DEC_BATCH, CONV_W - 1, LRU_W), 1.0),
        'state_lru_h': nrm((Dp, DEC_BATCH, LRU_W), 0.5),
        'state_ssd_conv': nrm((Dp, DEC_BATCH, CONV_W - 1, SSD_CONV_DIM), 1.0),
        'state_ssd': nrm((Dp, DEC_BATCH, SSD_HEADS, SSD_HEAD_DIM, SSD_STATE), 0.5),
        'w_ada': nrm((Dp, D_MODEL, N_MOD * D_MODEL), D_MODEL ** -0.5),
        'b_ada': nrm((Dp, N_MOD * D_MODEL), 0.01),
        'w_in': nrm((Dp, D_MODEL, IN_COLS), D_MODEL ** -0.5),
        'lru_conv_w': nrm((Dp, CONV_W, LRU_W), 0.5),
        'lru_conv_b': nrm((Dp, LRU_W), 0.01),
        'lru_wa': nrm((Dp, LRU_HEADS, LRU_BLOCK, LRU_BLOCK), LRU_BLOCK ** -0.5),
        'lru_ba': nrm((Dp, LRU_HEADS, LRU_BLOCK), 0.01),
        'lru_wx': nrm((Dp, LRU_HEADS, LRU_BLOCK, LRU_BLOCK), LRU_BLOCK ** -0.5),
        'lru_bx': nrm((Dp, LRU_HEADS, LRU_BLOCK), 0.01),
        'lru_lambda': lru_lambda,
        'lru_norm_g': 1.0 + nrm((Dp, LRU_W), 0.02),
        'ssd_conv_w': nrm((Dp, CONV_W, SSD_CONV_DIM), 0.5),
        'ssd_conv_b': nrm((Dp, SSD_CONV_DIM), 0.01),
        'ssd_dt_bias': dt_bias,
        'ssd_a_log': jnp.log(unif((Dp, SSD_HEADS), 1.0, 16.0)),
        'ssd_d': 1.0 + nrm((Dp, SSD_HEADS), 0.1),
        'ssd_norm_g': 1.0 + nrm((Dp, SSD_INNER), 0.02),
        'w_out': nrm((Dp, MIX_W, D_MODEL), MIX_W ** -0.5 * beta),
        'ln1_g': 1.0 + nrm((Dp, D_MODEL), 0.02),
        'ln1_b': nrm((Dp, D_MODEL), 0.01),
        'w_rg': nrm((Dp, D_MODEL, N_EXPERT_GROUPS), D_MODEL ** -0.5),
        'b_rg': nrm((Dp, N_EXPERT_GROUPS), 0.01),
        'w_re': nrm((Dp, D_MODEL, N_EXPERTS), D_MODEL ** -0.5),
        'b_re': nrm((Dp, N_EXPERTS), 0.01),
        'w_gate': nrm((Dp, N_EXPERTS, D_MODEL, D_FF_EXPERT), D_MODEL ** -0.5),
        'w_up': nrm((Dp, N_EXPERTS, D_MODEL, D_FF_EXPERT), D_MODEL ** -0.5),
        'w_down': nrm((Dp, N_EXPERTS, D_FF_EXPERT, D_MODEL), D_FF_EXPERT ** -0.5 * beta),
        'ln2_g': 1.0 + nrm((Dp, D_MODEL), 0.02),
        'ln2_b': nrm((Dp, D_MODEL), 0.01),
    }


def reference(x_prompt, x_sample, c_prompt, c_sample, state_lru_conv, state_lru_h, state_ssd_conv,
              state_ssd, w_ada, b_ada, w_in, lru_conv_w, lru_conv_b, lru_wa, lru_ba, lru_wx, lru_bx,
              lru_lambda, lru_norm_g, ssd_conv_w, ssd_conv_b, ssd_dt_bias, ssd_a_log, ssd_d,
              ssd_norm_g, w_out, ln1_g, ln1_b, w_rg, b_rg, w_re, b_re, w_gate, w_up, w_down,
              ln2_g, ln2_b):
    alpha = (2.0 * DEPTH) ** 0.25
    xp, xs = x_prompt, x_sample
    bp = xp.shape[0]
    dtype = xp.dtype
    p_lc, p_lh, p_sc, p_ss = [], [], [], []
    s_lc, s_lh, s_sc, s_ss = [], [], [], []
    for l in range(DEPTH):
        mp = ada_mod(c_prompt, w_ada[l], b_ada[l])
        ms = ada_mod(c_sample, w_ada[l], b_ada[l])
        mix_w = (w_in[l], lru_conv_w[l], lru_conv_b[l], lru_wa[l], lru_ba[l], lru_wx[l], lru_bx[l],
                 lru_lambda[l], lru_norm_g[l], ssd_conv_w[l], ssd_conv_b[l], ssd_dt_bias[l],
                 ssd_a_log[l], ssd_d[l], ssd_norm_g[l], w_out[l])
        op, stp = mixer(modulate(xp, mp[:, 0], mp[:, 1]),
                        jnp.zeros((bp, CONV_W - 1, LRU_W), dtype),
                        jnp.zeros((bp, LRU_W), dtype),
                        jnp.zeros((bp, CONV_W - 1, SSD_CONV_DIM), dtype),
                        jnp.zeros((bp, SSD_HEADS, SSD_HEAD_DIM, SSD_STATE), dtype),
                        *mix_w)
        os_, sts = mixer(modulate(xs, ms[:, 0], ms[:, 1]),
                         state_lru_conv[l], state_lru_h[l], state_ssd_conv[l], state_ssd[l], *mix_w)
        xp = layer_norm(alpha * xp + mp[:, 2] * op, ln1_g[l], ln1_b[l])
        xs = layer_norm(alpha * xs + ms[:, 2] * os_, ln1_g[l], ln1_b[l])
        vp = modulate(xp, mp[:, 3], mp[:, 4])
        vs = modulate(xs, ms[:, 3], ms[:, 4])
        n_p = vp.shape[0] * vp.shape[1]
        f = hmoe(jnp.concatenate([vp.reshape(-1, D_MODEL), vs.reshape(-1, D_MODEL)], axis=0),
                 w_rg[l], b_rg[l], w_re[l], b_re[l], w_gate[l], w_up[l], w_down[l])
        xp = layer_norm(alpha * xp + mp[:, 5] * f[:n_p].reshape(xp.shape), ln2_g[l], ln2_b[l])
        xs = layer_norm(alpha * xs + ms[:, 5] * f[n_p:].reshape(xs.shape), ln2_g[l], ln2_b[l])
        p_lc.append(stp[0]); p_lh.append(stp[1]); p_sc.append(stp[2]); p_ss.append(stp[3])
        s_lc.append(sts[0]); s_lh.append(sts[1]); s_sc.append(sts[2]); s_ss.append(sts[3])
    return (xp, xs,
            jnp.stack(p_lc), jnp.stack(p_lh), jnp.stack(p_sc), jnp.stack(p_ss),
            jnp.stack(s_lc), jnp.stack(s_lh), jnp.stack(s_sc), jnp.stack(s_ss))
```

```python
import functools
import math

import jax
import jax.numpy as jnp
from jax import lax
from jax.experimental import pallas as pl
from jax.experimental.pallas import tpu as pltpu

F32 = jnp.float32
BF16 = jnp.bfloat16
I32 = jnp.int32
HIGHEST = lax.Precision.HIGHEST

D = 1024
DEPTH = 1
CONV_W = 4
LRU_W = D
LRU_HEADS = 16
LRU_C = 8.0
LRU_PACK = 4
LRU_PACK_W = LRU_PACK * (LRU_W // LRU_HEADS)
SSD_INNER = D
SSD_HEADS = 16
SSD_P = SSD_INNER // SSD_HEADS
SSD_GROUPS = 2
SSD_N = 128
SSD_Q = 128
SSD_CONV_DIM = SSD_INNER + 2 * SSD_GROUPS * SSD_N
N_GROUPS = 4
GROUP_SIZE = 8
N_EXPERTS = N_GROUPS * GROUP_SIZE
D_FF = D // 2
N_MOD = 6
LN_EPS = 1e-5
RMS_EPS = 1e-6
ALPHA = (2.0 * DEPTH) ** 0.25

LANES = 128
SUBLANES = 8
VMEM_LIMIT = 56 * 1024 * 1024
TOK_TILE = 128
MOE_BM = 256
MOE_BM_SHIFT = 8
ROUTE_ROWS = 40

NT_DIMS = (((1,), (1,)), ((), ()))


def _cparams(sem):
    return pltpu.CompilerParams(dimension_semantics=sem, vmem_limit_bytes=VMEM_LIMIT)


def _sigmoid(x):
    return 1.0 / (1.0 + jnp.exp(-x))


def _silu(x):
    return x * _sigmoid(x)


def _softplus(x):
    return jnp.maximum(x, 0.0) + jnp.log1p(jnp.exp(-jnp.abs(x)))


def _gelu_tanh(x):
    return 0.5 * x * (1.0 + jnp.tanh(math.sqrt(2.0 / math.pi) * (x + 0.044715 * (x * x * x))))


def _layer_norm(x):
    mu = jnp.mean(x, axis=-1, keepdims=True)
    xc = x - mu
    var = jnp.mean(xc * xc, axis=-1, keepdims=True)
    return xc * lax.rsqrt(var + LN_EPS)


def _rms_norm(x, g):
    return x * lax.rsqrt(jnp.mean(x * x, axis=-1, keepdims=True) + RMS_EPS) * g


def _dot(a, b, **kw):
    return jnp.dot(a, b, preferred_element_type=F32, **kw)


def _dot_nt(a, b, **kw):
    return lax.dot_general(a, b, NT_DIMS, preferred_element_type=F32, **kw)


def _lru_gates(xc, xb, wa, wx, ba, bx, sp):
    r = _sigmoid(_dot(xb, wa) + ba)
    i = _sigmoid(_dot(xb, wx) + bx)
    log_a = (-LRU_C) * r * sp
    a = jnp.exp(log_a)
    mult = jnp.sqrt(jnp.tanh(-log_a) * (a * a + 1.0))
    return a, mult * (i * xc)


def _ada_kernel(c_ref, w_ref, b_ref, o_ref):
    s = _silu(c_ref[...]).astype(BF16)
    o_ref[...] = _dot(s, w_ref[...].astype(BF16)) + b_ref[...]


def _ada_call(c_all, w_ada, b_ada):
    rows = c_all.shape[0]
    n = w_ada.shape[1]
    tn = 512
    return pl.pallas_call(
        _ada_kernel,
        out_shape=jax.ShapeDtypeStruct((rows, n), F32),
        grid=(n // tn,),
        in_specs=[pl.BlockSpec((rows, D), lambda j: (0, 0)),
                  pl.BlockSpec((D, tn), lambda j: (0, j)),
                  pl.BlockSpec((1, tn), lambda j: (0, j))],
        out_specs=pl.BlockSpec((rows, tn), lambda j: (0, j)),
        compiler_params=_cparams(("arbitrary",)),
        name="ada",
    )(c_all, w_ada, b_ada)


def _proj_kernel(x_ref, sh_ref, sc_ref, w1_ref, w2_ref, w3_ref, w4_ref, o1_ref, o2_ref, o3_ref, o4_ref):
    u = _layer_norm(x_ref[...]) * (1.0 + sc_ref[...]) + sh_ref[...]
    ub = u.astype(BF16)
    o1_ref[...] = _dot(ub, w1_ref[...])
    o2_ref[...] = _dot(ub, w2_ref[...])
    o3_ref[...] = _dot(ub, w3_ref[...])
    o4_ref[...] = _dot(ub, w4_ref[...])


def _mod_spec(mod, tl):
    if mod.shape[1] == 1:
        return pl.BlockSpec((None, 1, D), lambda b, l: (b, 0, 0))
    return pl.BlockSpec((None, tl, D), lambda b, l: (b, l, 0))


def _proj_call(x3, sh3, sc3, ws, tl):
    nb, L, _ = x3.shape
    widths = [w.shape[1] for w in ws]
    row = lambda b, l: (b, l, 0)
    full = lambda b, l: (0, 0)
    return pl.pallas_call(
        _proj_kernel,
        out_shape=[jax.ShapeDtypeStruct((nb, L, n), F32) for n in widths],
        grid=(nb, L // tl),
        in_specs=[pl.BlockSpec((None, tl, D), row), _mod_spec(sh3, tl), _mod_spec(sc3, tl)]
                 + [pl.BlockSpec((D, n), full) for n in widths],
        out_specs=[pl.BlockSpec((None, tl, n), row) for n in widths],
        compiler_params=_cparams(("parallel", "arbitrary")),
        name="proj",
    )(x3, sh3, sc3, *ws)


def _scan_rows(a_scr, u_scr, h0, n_rows):
    sub = lax.broadcasted_iota(I32, (SUBLANES, D), 0)
    shifts = (1, 2, 4)
    masks = [sub >= d for d in shifts]

    def body(j, hp):
        r0 = pl.multiple_of(j * SUBLANES, SUBLANES)
        a = a_scr[pl.ds(r0, SUBLANES), :]
        u = u_scr[pl.ds(r0, SUBLANES), :]
        for d, m in zip(shifts, masks):
            ash = pltpu.roll(a, d, 0)
            ush = pltpu.roll(u, d, 0)
            u = jnp.where(m, u + a * ush, u)
            a = jnp.where(m, a * ash, a)
        h = u + a * hp
        u_scr[pl.ds(r0, SUBLANES), :] = h
        return h[SUBLANES - 1:SUBLANES, :]

    return lax.fori_loop(0, n_rows // SUBLANES, body, h0)


def _lru_kernel(x_ref, g_ref, cw_ref, cb_ref, wa_ref, wx_ref, ba_ref, bx_ref, lam_ref, ng_ref,
                y_ref, conv_ref, h_ref, xpad, xc_scr, a_scr, u_scr, hc_scr, *, tl):
    @pl.when(pl.program_id(1) == 0)
    def _():
        xpad[0:SUBLANES, :] = jnp.zeros((SUBLANES, D), F32)
        hc_scr[...] = jnp.zeros((1, D), F32)

    xpad[SUBLANES:SUBLANES + tl, :] = x_ref[...]
    off = SUBLANES - (CONV_W - 1)
    acc = xpad[off:off + tl, :] * cw_ref[0:1, :]
    for k in range(1, CONV_W):
        acc = acc + xpad[off + k:off + k + tl, :] * cw_ref[k:k + 1, :]
    xc_scr[...] = acc + cb_ref[...]
    conv_ref[...] = xpad[SUBLANES + tl - (CONV_W - 1):SUBLANES + tl, :]
    xpad[0:SUBLANES, :] = xpad[tl:tl + SUBLANES, :]

    sp = _softplus(-lam_ref[...])
    for j in range(LRU_W // LRU_PACK_W):
        cs = slice(j * LRU_PACK_W, (j + 1) * LRU_PACK_W)
        xc = xc_scr[:, cs]
        a, u = _lru_gates(xc, xc.astype(BF16), wa_ref[j], wx_ref[j], ba_ref[:, cs], bx_ref[:, cs], sp[:, cs])
        a_scr[:, cs] = a
        u_scr[:, cs] = u

    h_last = _scan_rows(a_scr, u_scr, hc_scr[...], tl)
    hc_scr[...] = h_last
    h_ref[...] = h_last
    y = u_scr[...] * _gelu_tanh(g_ref[...])
    y_ref[...] = _rms_norm(y, ng_ref[...]).astype(BF16)


def _lru_call(xg, p, tl):
    nb, L, _ = xg.shape
    vec = lambda b, l: (0, 0)
    blk = lambda b, l: (0, 0, 0)
    return pl.pallas_call(
        functools.partial(_lru_kernel, tl=tl),
        out_shape=[jax.ShapeDtypeStruct((nb, L, LRU_W), BF16),
                   jax.ShapeDtypeStruct((nb, CONV_W - 1, LRU_W), F32),
                   jax.ShapeDtypeStruct((nb, 1, LRU_W), F32)],
        grid=(nb, L // tl),
        in_specs=[pl.BlockSpec((None, tl, LRU_W), lambda b, l: (b, l, 0)),
                  pl.BlockSpec((None, tl, LRU_W), lambda b, l: (b, l, 1)),
                  pl.BlockSpec((CONV_W, LRU_W), vec), pl.BlockSpec((1, LRU_W), vec),
                  pl.BlockSpec((LRU_W // LRU_PACK_W, LRU_PACK_W, LRU_PACK_W), blk),
                  pl.BlockSpec((LRU_W // LRU_PACK_W, LRU_PACK_W, LRU_PACK_W), blk),
                  pl.BlockSpec((1, LRU_W), vec), pl.BlockSpec((1, LRU_W), vec),
                  pl.BlockSpec((1, LRU_W), vec), pl.BlockSpec((1, LRU_W), vec)],
        out_specs=[pl.BlockSpec((None, tl, LRU_W), lambda b, l: (b, l, 0)),
                   pl.BlockSpec((None, CONV_W - 1, LRU_W), lambda b, l: (b, 0, 0)),
                   pl.BlockSpec((None, 1, LRU_W), lambda b, l: (b, 0, 0))],
        scratch_shapes=[pltpu.VMEM((tl + SUBLANES, LRU_W), F32), pltpu.VMEM((tl, LRU_W), F32),
                        pltpu.VMEM((tl, LRU_W), F32), pltpu.VMEM((tl, LRU_W), F32),
                        pltpu.VMEM((1, LRU_W), F32)],
        compiler_params=_cparams(("parallel", "arbitrary")),
        name="lru",
    )(xg, xg, p["cw"], p["cb"], p["wa"], p["wx"], p["ba"], p["bx"], p["lam"], p["ng"])


def _ssd_kernel(xbc_ref, z_ref, dt_ref, cw_ref, cb_ref, dtb_ref, alog_ref, dexp_ref, ng_ref,
                y_ref, conv_ref, st_ref, xpad, xc_scr, st_scr, y_scr):
    q = SSD_Q

    @pl.when(pl.program_id(1) == 0)
    def _():
        xpad[0:SUBLANES, :] = jnp.zeros((SUBLANES, SSD_CONV_DIM), F32)
        st_scr[...] = jnp.zeros((SSD_N, SSD_INNER), F32)

    xpad[SUBLANES:SUBLANES + q, :] = xbc_ref[...]
    off = SUBLANES - (CONV_W - 1)
    acc = xpad[off:off + q, :] * cw_ref[0:1, :]
    for k in range(1, CONV_W):
        acc = acc + xpad[off + k:off + k + q, :] * cw_ref[k:k + 1, :]
    xc_scr[...] = _silu(acc + cb_ref[...])
    conv_ref[...] = xpad[SUBLANES + q - (CONV_W - 1):SUBLANES + q, :]
    xpad[0:SUBLANES, :] = xpad[q:q + SUBLANES, :]

    dt = _softplus(dt_ref[...] + dtb_ref[...])
    da = dt * (-jnp.exp(alog_ref[...]))
    ri = lax.broadcasted_iota(I32, (q, q), 0)
    ci = lax.broadcasted_iota(I32, (q, q), 1)
    causal = ri >= ci
    acs = _dot(causal.astype(F32), da, precision=HIGHEST)
    acs_t = acs.T
    tot = acs[q - 1:q, :]
    chunk_decay = jnp.exp(tot)
    e_acs = jnp.exp(acs)
    d_end = jnp.exp(tot - acs)

    for g in range(SSD_GROUPS):
        bg = xc_scr[:, SSD_INNER + g * SSD_N:SSD_INNER + (g + 1) * SSD_N]
        cg = xc_scr[:, SSD_INNER + (SSD_GROUPS + g) * SSD_N:SSD_INNER + (SSD_GROUPS + g + 1) * SSD_N]
        bgb = bg.astype(BF16)
        cgb = cg.astype(BF16)
        bgt = bg.T.astype(BF16)
        cb = _dot_nt(cgb, bgb)
        for j in range(SSD_HEADS // SSD_GROUPS):
            h = g * (SSD_HEADS // SSD_GROUPS) + j
            hs = slice(h * SSD_P, (h + 1) * SSD_P)
            seg = acs[:, h:h + 1] - acs_t[h:h + 1, :]
            decay = jnp.where(causal, jnp.exp(jnp.where(causal, seg, 0.0)), 0.0)
            xh = xc_scr[:, hs]
            xdt = xh * dt[:, h:h + 1]
            y_diag = _dot((cb * decay).astype(BF16), xdt.astype(BF16))
            s_in = st_scr[:, hs]
            y_off = _dot(cgb, s_in.astype(BF16)) * e_acs[:, h:h + 1]
            chunk_state = _dot(bgt, (xdt * d_end[:, h:h + 1]).astype(BF16))
            st_scr[:, hs] = chunk_decay[:, h:h + 1] * s_in + chunk_state
            y_scr[:, hs] = y_diag + y_off + dexp_ref[:, hs] * xh

    yz = y_scr[...] * _silu(z_ref[...])
    y_ref[...] = _rms_norm(yz, ng_ref[...]).astype(BF16)

    @pl.when(pl.program_id(1) == pl.num_programs(1) - 1)
    def _():
        st_ref[...] = st_scr[...].T


def _ssd_call(xbc, z, dtr, p):
    nb, L, _ = xbc.shape
    q = SSD_Q
    vec = lambda b, c: (0, 0)
    row = lambda b, c: (b, c, 0)
    return pl.pallas_call(
        _ssd_kernel,
        out_shape=[jax.ShapeDtypeStruct((nb, L, SSD_INNER), BF16),
                   jax.ShapeDtypeStruct((nb, CONV_W - 1, SSD_CONV_DIM), F32),
                   jax.ShapeDtypeStruct((nb, SSD_INNER, SSD_N), F32)],
        grid=(nb, L // q),
        in_specs=[pl.BlockSpec((None, q, SSD_CONV_DIM), row), pl.BlockSpec((None, q, SSD_INNER), row),
                  pl.BlockSpec((None, q, LANES), row),
                  pl.BlockSpec((CONV_W, SSD_CONV_DIM), vec), pl.BlockSpec((1, SSD_CONV_DIM), vec),
                  pl.BlockSpec((1, LANES), vec), pl.BlockSpec((1, LANES), vec),
                  pl.BlockSpec((1, SSD_INNER), vec), pl.BlockSpec((1, SSD_INNER), vec)],
        out_specs=[pl.BlockSpec((None, q, SSD_INNER), row),
                   pl.BlockSpec((None, CONV_W - 1, SSD_CONV_DIM), lambda b, c: (b, 0, 0)),
                   pl.BlockSpec((None, SSD_INNER, SSD_N), lambda b, c: (b, 0, 0))],
        scratch_shapes=[pltpu.VMEM((q + SUBLANES, SSD_CONV_DIM), F32), pltpu.VMEM((q, SSD_CONV_DIM), F32),
                        pltpu.VMEM((SSD_N, SSD_INNER), F32), pltpu.VMEM((q, SSD_INNER), F32)],
        compiler_params=_cparams(("parallel", "arbitrary")),
        name="ssd",
    )(xbc, z, dtr, p["cw"], p["cb"], p["dtb"], p["alog"], p["dexp"], p["ng"])


def _srow_kernel(xg_ref, xbc_ref, dte_ref, lconv_ref, h0_ref, sconv_ref,
                 lcw_ref, lcb_ref, wa_ref, wx_ref, ba_ref, bx_ref, lam_ref, lng_ref,
                 scw_ref, scb_ref, dtbe_ref, aloge_ref,
                 yl_ref, h_ref, xs_ref, bm_ref, cm_ref, xdtt_ref, dect_ref, xc_scr):
    xl = xg_ref[:, 0:LRU_W]
    acc = lcb_ref[...] + xl * lcw_ref[CONV_W - 1:CONV_W, :]
    for k in range(CONV_W - 1):
        acc = acc + lconv_ref[:, k * LRU_W:(k + 1) * LRU_W] * lcw_ref[k:k + 1, :]
    xc_scr[...] = acc
    sp = _softplus(-lam_ref[...])
    for j in range(LRU_W // LRU_PACK_W):
        cs = slice(j * LRU_PACK_W, (j + 1) * LRU_PACK_W)
        xc = xc_scr[:, cs]
        a, u = _lru_gates(xc, xc.astype(BF16), wa_ref[j], wx_ref[j], ba_ref[:, cs], bx_ref[:, cs], sp[:, cs])
        h_ref[:, cs] = a * h0_ref[:, cs] + u
    y = h_ref[...] * _gelu_tanh(xg_ref[:, LRU_W:2 * LRU_W])
    yl_ref[...] = _rms_norm(y, lng_ref[...]).astype(BF16)

    acc = scb_ref[...] + xbc_ref[...] * scw_ref[CONV_W - 1:CONV_W, :]
    for k in range(CONV_W - 1):
        acc = acc + sconv_ref[:, k * SSD_CONV_DIM:(k + 1) * SSD_CONV_DIM] * scw_ref[k:k + 1, :]
    xc = _silu(acc)
    xs = xc[:, 0:SSD_INNER]
    xs_ref[...] = xs
    bm_ref[...] = xc[:, SSD_INNER:SSD_INNER + SSD_GROUPS * SSD_N]
    cm_ref[...] = xc[:, SSD_INNER + SSD_GROUPS * SSD_N:]
    dt = _softplus(dte_ref[...] + dtbe_ref[...])
    dec = jnp.exp(dt * (-jnp.exp(aloge_ref[...])))
    xdtt_ref[...] = (xs * dt).T
    dect_ref[...] = dec.T


def _srow_call(xg, xbc, dte, lconv, h0, sconv, lp, sp):
    n = xg.shape[0]
    args = (xg, xbc, dte, lconv, h0, sconv, lp["cw"], lp["cb"], lp["wa"], lp["wx"], lp["ba"], lp["bx"],
            lp["lam"], lp["ng"], sp["cw"], sp["cb"], sp["dtbe"], sp["aloge"])
    full = lambda a: pl.BlockSpec(a.shape, lambda i, nd=a.ndim: (0,) * nd)
    outs = [jax.ShapeDtypeStruct((n, LRU_W), BF16), jax.ShapeDtypeStruct((n, LRU_W), F32),
            jax.ShapeDtypeStruct((n, SSD_INNER), F32), jax.ShapeDtypeStruct((n, SSD_GROUPS * SSD_N), F32),
            jax.ShapeDtypeStruct((n, SSD_GROUPS * SSD_N), F32),
            jax.ShapeDtypeStruct((SSD_INNER, n), F32), jax.ShapeDtypeStruct((SSD_INNER, n), F32)]
    return pl.pallas_call(
        _srow_kernel,
        out_shape=outs,
        grid=(1,),
        in_specs=[full(a) for a in args],
        out_specs=[pl.BlockSpec(o.shape, lambda i: (0, 0)) for o in outs],
        scratch_shapes=[pltpu.VMEM((n, LRU_W), F32)],
        compiler_params=_cparams(("arbitrary",)),
        name="srow",
    )(*args)


def _sstate_kernel(s0_ref, xq_ref, dq_ref, bm_ref, cm_ref, xs_ref, z_ref, dexp_ref, ng_ref,
                   sn_ref, ys_ref, yraw):
    nb = s0_ref.shape[0]
    half = SSD_INNER // SSD_GROUPS
    for bi in range(nb):
        brow = jnp.concatenate(
            [jnp.broadcast_to(bm_ref[bi:bi + 1, g * SSD_N:(g + 1) * SSD_N], (half, SSD_N)) for g in range(SSD_GROUPS)],
            axis=0)
        s = dq_ref[:, bi:bi + 1] * s0_ref[bi] + xq_ref[:, bi:bi + 1] * brow
        sn_ref[bi] = s
        sb = s.astype(BF16)
        for g in range(SSD_GROUPS):
            cg = cm_ref[:, g * SSD_N:(g + 1) * SSD_N].astype(BF16)
            res = _dot_nt(cg, sb[g * half:(g + 1) * half, :])
            yraw[bi:bi + 1, g * half:(g + 1) * half] = res[bi:bi + 1, :]
    y = yraw[...] + dexp_ref[...] * xs_ref[...]
    ys_ref[...] = _rms_norm(y * _silu(z_ref[...]), ng_ref[...]).astype(BF16)


def _sstate_call(s0, xq, dq, bm, cm, xs, z, dexp, ng, nb):
    n = s0.shape[0]
    row = lambda i: (i, 0)
    vec = lambda i: (0, 0)
    gn = SSD_GROUPS * SSD_N
    return pl.pallas_call(
        _sstate_kernel,
        out_shape=[jax.ShapeDtypeStruct(s0.shape, F32), jax.ShapeDtypeStruct((n, SSD_INNER), BF16)],
        grid=(n // nb,),
        in_specs=[pl.BlockSpec((nb, SSD_INNER, SSD_N), lambda i: (i, 0, 0)),
                  pl.BlockSpec((None, SSD_INNER, nb), lambda i: (i, 0, 0)),
                  pl.BlockSpec((None, SSD_INNER, nb), lambda i: (i, 0, 0)),
                  pl.BlockSpec((nb, gn), row), pl.BlockSpec((nb, gn), row),
                  pl.BlockSpec((nb, SSD_INNER), row), pl.BlockSpec((nb, SSD_INNER), row),
                  pl.BlockSpec((1, SSD_INNER), vec), pl.BlockSpec((1, SSD_INNER), vec)],
        out_specs=[pl.BlockSpec((nb, SSD_INNER, SSD_N), lambda i: (i, 0, 0)), pl.BlockSpec((nb, SSD_INNER), row)],
        scratch_shapes=[pltpu.VMEM((nb, SSD_INNER), F32)],
        compiler_params=_cparams(("parallel",)),
        name="sstate",
    )(s0, xq, dq, bm, cm, xs, z, dexp, ng)


def _post_kernel(yl_ref, ys_ref, x_ref, g1_ref, sh2_ref, sc2_ref, wo_ref, l1g_ref, l1b_ref, wrt_ref, brc_ref,
                 x1_ref, v_ref, eid_ref, gw_ref):
    o = _dot(yl_ref[...], wo_ref[0:LRU_W, :]) + _dot(ys_ref[...], wo_ref[LRU_W:LRU_W + SSD_INNER, :])
    x1 = _layer_norm(ALPHA * x_ref[...] + g1_ref[...] * o) * l1g_ref[...] + l1b_ref[...]
    x1_ref[...] = x1
    v = _layer_norm(x1) * (1.0 + sc2_ref[...]) + sh2_ref[...]
    v_ref[...] = v

    lt = _dot_nt(wrt_ref[...], v, precision=HIGHEST) + brc_ref[...]
    tl = lt.shape[1]
    row = lax.broadcasted_iota(I32, (GROUP_SIZE, tl), 0).astype(F32)
    big = float(GROUP_SIZE)
    neg = -jnp.inf
    lg = jnp.where(row < N_GROUPS, lt[0:GROUP_SIZE, :], neg)
    gmax = jnp.max(lg, axis=0, keepdims=True)
    gsel = jnp.min(jnp.where(lg == gmax, row, big), axis=0, keepdims=True)
    pg = 1.0 / jnp.sum(jnp.exp(lg - gmax), axis=0, keepdims=True)
    le = lt[GROUP_SIZE:2 * GROUP_SIZE, :]
    for j in range(1, N_GROUPS):
        le = jnp.where(gsel == j, lt[GROUP_SIZE * (j + 1):GROUP_SIZE * (j + 2), :], le)
    m1 = jnp.max(le, axis=0, keepdims=True)
    i1 = jnp.min(jnp.where(le == m1, row, big), axis=0, keepdims=True)
    rest = jnp.where(row == i1, neg, le)
    m2 = jnp.max(rest, axis=0, keepdims=True)
    i2 = jnp.min(jnp.where(rest == m2, row, big), axis=0, keepdims=True)
    e2 = jnp.exp(m2 - m1)
    den = 1.0 + e2
    eid = jnp.where(row == 0, gsel * GROUP_SIZE + i1, jnp.where(row == 1, gsel * GROUP_SIZE + i2, 0.0))
    eid_ref[...] = eid.astype(I32)
    gw_ref[...] = jnp.where(row == 0, pg * (1.0 / den), jnp.where(row == 1, pg * (e2 / den), 0.0))


def _post_call(yl, ys, x3, g1, sh2, sc2, p, tl):
    nb, L, _ = x3.shape
    nl = L // tl
    row = lambda b, l: (b, l, 0)
    vec = lambda b, l: (0, 0)
    tok = lambda b, l: (b * nl + l, 0)
    tokt = lambda b, l: (0, b * nl + l)
    return pl.pallas_call(
        _post_kernel,
        out_shape=[jax.ShapeDtypeStruct((nb, L, D), F32), jax.ShapeDtypeStruct((nb * L, D), F32),
                   jax.ShapeDtypeStruct((SUBLANES, nb * L), I32), jax.ShapeDtypeStruct((SUBLANES, nb * L), F32)],
        grid=(nb, nl),
        in_specs=[pl.BlockSpec((None, tl, LRU_W), row), pl.BlockSpec((None, tl, SSD_INNER), row),
                  pl.BlockSpec((None, tl, D), row), _mod_spec(g1, tl), _mod_spec(sh2, tl), _mod_spec(sc2, tl),
                  pl.BlockSpec((LRU_W + SSD_INNER, D), vec), pl.BlockSpec((1, D), vec), pl.BlockSpec((1, D), vec),
                  pl.BlockSpec((ROUTE_ROWS, D), vec), pl.BlockSpec((ROUTE_ROWS, 1), vec)],
        out_specs=[pl.BlockSpec((None, tl, D), row), pl.BlockSpec((tl, D), tok),
                   pl.BlockSpec((SUBLANES, tl), tokt), pl.BlockSpec((SUBLANES, tl), tokt)],
        compiler_params=_cparams(("parallel", "arbitrary")),
        name="post",
    )(yl, ys, x3, g1, sh2, sc2, p["wo"], p["l1g"], p["l1b"], p["wrt"], p["brc"])


def _route_kernel(eidp_ref, eids_ref, dest_ref, cnt_ref, run, poff, *, p_tiles):
    ph = pl.program_id(0)
    t = pl.program_id(1)
    n = TOK_TILE
    eid = jnp.where(t < p_tiles, eidp_ref[...], eids_ref[...])
    rowi = lax.broadcasted_iota(I32, (LANES, n), 0)
    oh0 = rowi == eid[0:1, :]
    oh1 = rowi == eid[1:2, :]
    oh = oh0.astype(F32) + oh1.astype(F32)
    tile_cnt = jnp.sum(oh, axis=1, keepdims=True)

    @pl.when((ph == 0) & (t == 0))
    def _():
        run[...] = jnp.zeros((LANES, n), F32)

    @pl.when(ph == 0)
    def _():
        run[...] = run[...] + tile_cnt

    @pl.when((ph == 1) & (t == 0))
    def _():
        counts = run[...]
        cnt_ref[...] = counts
        nblk = (counts.astype(I32) + (MOE_BM - 1)) >> MOE_BM_SHIFT
        r = lax.broadcasted_iota(I32, (LANES, LANES), 0)
        c = lax.broadcasted_iota(I32, (LANES, LANES), 1)
        lower = (r > c).astype(BF16)
        poff[...] = _dot(lower, nblk.astype(F32).astype(BF16)) * float(MOE_BM)
        run[...] = jnp.zeros((LANES, n), F32)

    @pl.when(ph == 1)
    def _():
        r = lax.broadcasted_iota(I32, (n, n), 0)
        c = lax.broadcasted_iota(I32, (n, n), 1)
        before = (r < c).astype(BF16)
        slot = _dot(oh.astype(BF16), before) + run[...] + poff[...]
        d0 = jnp.sum(jnp.where(oh0, slot, 0.0), axis=0, keepdims=True)
        d1 = jnp.sum(jnp.where(oh1, slot, 0.0), axis=0, keepdims=True)
        row = lax.broadcasted_iota(I32, (SUBLANES, n), 0)
        dest_ref[...] = jnp.where(row == 0, d0, jnp.where(row == 1, d1, 0.0)).astype(I32)
        run[...] = run[...] + tile_cnt


def _route_call(eid_p, eid_s):
    n = TOK_TILE
    p_tiles = eid_p.shape[1] // n
    s_tiles = eid_s.shape[1] // n
    t_all = (p_tiles + s_tiles) * n
    return pl.pallas_call(
        functools.partial(_route_kernel, p_tiles=p_tiles),
        out_shape=[jax.ShapeDtypeStruct((SUBLANES, t_all), I32), jax.ShapeDtypeStruct((LANES, n), F32)],
        grid=(2, p_tiles + s_tiles),
        in_specs=[pl.BlockSpec((SUBLANES, n), lambda ph, t: (0, jnp.minimum(t, p_tiles - 1))),
                  pl.BlockSpec((SUBLANES, n), lambda ph, t: (0, jnp.maximum(t - p_tiles, 0)))],
        out_specs=[pl.BlockSpec((SUBLANES, n), lambda ph, t: (0, t * ph)),
                   pl.BlockSpec((LANES, n), lambda ph, t: (0, 0))],
        scratch_shapes=[pltpu.VMEM((LANES, n), F32), pltpu.VMEM((LANES, n), F32)],
        compiler_params=_cparams(("arbitrary", "arbitrary")),
        name="route",
    )(eid_p, eid_s)


def _row_copy(src, src_row, dst, dst_row, sem):
    return pltpu.make_async_copy(src.at[pl.ds(src_row, 1), :], dst.at[pl.ds(dst_row, 1), :], sem)


def _scatter_rows(dest_ref, v_ref, xpad_ref, sem):
    def start(r, carry):
        for k in range(2):
            _row_copy(v_ref, r, xpad_ref, dest_ref[k, r], sem).start()
        return carry

    def wait(r, carry):
        for k in range(2):
            _row_copy(v_ref, r, xpad_ref, dest_ref[k, r], sem).wait()
        return carry

    lax.fori_loop(0, TOK_TILE, start, 0)
    lax.fori_loop(0, TOK_TILE, wait, 0)


def _dispatch_kernel(dest_ref, vp_ref, vs_ref, xpad_in, xpad_ref, sem, *, p_tiles):
    del xpad_in
    t = pl.program_id(0)

    @pl.when(t < p_tiles)
    def _():
        _scatter_rows(dest_ref, vp_ref, xpad_ref, sem)

    @pl.when(t >= p_tiles)
    def _():
        _scatter_rows(dest_ref, vs_ref, xpad_ref, sem)


def _dispatch_call(dest, v_p, v_s, xpad0):
    p_tiles = v_p.shape[0] // TOK_TILE
    s_tiles = v_s.shape[0] // TOK_TILE
    return pl.pallas_call(
        functools.partial(_dispatch_kernel, p_tiles=p_tiles),
        out_shape=jax.ShapeDtypeStruct(xpad0.shape, xpad0.dtype),
        grid=(p_tiles + s_tiles,),
        in_specs=[pl.BlockSpec((SUBLANES, TOK_TILE), lambda t: (0, t), memory_space=pltpu.SMEM),
                  pl.BlockSpec((TOK_TILE, D), lambda t: (jnp.minimum(t, p_tiles - 1), 0)),
                  pl.BlockSpec((TOK_TILE, D), lambda t: (jnp.maximum(t - p_tiles, 0), 0)),
                  pl.BlockSpec(memory_space=pl.ANY)],
        out_specs=pl.BlockSpec(memory_space=pl.ANY),
        scratch_shapes=[pltpu.SemaphoreType.DMA],
        input_output_aliases={3: 0},
        compiler_params=_cparams(("arbitrary",)),
        name="dispatch",
    )(dest, v_p, v_s, xpad0)


def _expert_kernel(be_ref, nb_ref, x_ref, wg_ref, wu_ref, wd_ref, o_ref, wgb, wub, wdb):
    i = pl.program_id(0)

    @pl.when(i < nb_ref[0])
    def _():
        @pl.when((i == 0) | (be_ref[i] != be_ref[jnp.maximum(i - 1, 0)]))
        def _():
            wgb[...] = wg_ref[...].astype(BF16)
            wub[...] = wu_ref[...].astype(BF16)
            wdb[...] = wd_ref[...].astype(BF16)

        x = x_ref[...].astype(BF16)
        h = _silu(_dot(x, wgb[...])) * _dot(x, wub[...])
        o_ref[...] = _dot(h.astype(BF16), wdb[...])

    @pl.when(i >= nb_ref[0])
    def _():
        o_ref[...] = jnp.zeros(o_ref.shape, F32)


def _expert_call(blk_e, nblk, xpad, w_gate, w_up, w_down):
    n_rows = xpad.shape[0]
    blk = lambda i, be, nb: (jnp.minimum(i, nb[0] - 1), 0)
    wsel = lambda i, be, nb: (be[jnp.minimum(i, nb[0] - 1)], 0, 0)
    return pl.pallas_call(
        _expert_kernel,
        out_shape=jax.ShapeDtypeStruct((n_rows, D), F32),
        grid_spec=pltpu.PrefetchScalarGridSpec(
            num_scalar_prefetch=2,
            grid=(n_rows // MOE_BM,),
            in_specs=[pl.BlockSpec((MOE_BM, D), blk),
                      pl.BlockSpec((None, D, D_FF), wsel), pl.BlockSpec((None, D, D_FF), wsel),
                      pl.BlockSpec((None, D_FF, D), wsel)],
            out_specs=pl.BlockSpec((MOE_BM, D), lambda i, be, nb: (i, 0)),
            scratch_shapes=[pltpu.VMEM((D, D_FF), BF16), pltpu.VMEM((D, D_FF), BF16), pltpu.VMEM((D_FF, D), BF16)]),
        compiler_params=_cparams(("arbitrary",)),
        name="expert",
    )(blk_e, nblk, xpad, w_gate, w_up, w_down)


def _combine_kernel(dest_ref, gw0_ref, gw1_ref, x1_ref, g2_ref, l2g_ref, l2b_ref, ypad_ref, y_ref, ybuf, sem):
    def start(r, carry):
        for k in range(2):
            _row_copy(ypad_ref, dest_ref[k, r], ybuf.at[k], r, sem).start()
        return carry

    def wait(r, carry):
        for k in range(2):
            _row_copy(ypad_ref, dest_ref[k, r], ybuf.at[k], r, sem).wait()
        return carry

    lax.fori_loop(0, TOK_TILE, start, 0)
    lax.fori_loop(0, TOK_TILE, wait, 0)
    f = ybuf[0] * gw0_ref[...] + ybuf[1] * gw1_ref[...]
    y_ref[...] = _layer_norm(ALPHA * x1_ref[...] + g2_ref[...] * f) * l2g_ref[...] + l2b_ref[...]


def _combine_call(dest, gw0, gw1, x1, g2, l2g, l2b, ypad, tok_off):
    nb, L, _ = x1.shape
    tl = TOK_TILE
    nl = L // tl
    blk_off = tok_off // tl
    row = lambda b, l: (b, l, 0)
    vec = lambda b, l: (0, 0)
    tok = lambda b, l: (b * nl + l, 0)
    return pl.pallas_call(
        _combine_kernel,
        out_shape=jax.ShapeDtypeStruct((nb, L, D), F32),
        grid=(nb, nl),
        in_specs=[pl.BlockSpec((SUBLANES, tl), lambda b, l: (0, blk_off + b * nl + l), memory_space=pltpu.SMEM),
                  pl.BlockSpec((tl, 1), tok), pl.BlockSpec((tl, 1), tok),
                  pl.BlockSpec((None, tl, D), row), _mod_spec(g2, tl),
                  pl.BlockSpec((1, D), vec), pl.BlockSpec((1, D), vec),
                  pl.BlockSpec(memory_space=pl.ANY)],
        out_specs=pl.BlockSpec((None, tl, D), row),
        scratch_shapes=[pltpu.VMEM((2, tl, D), F32), pltpu.SemaphoreType.DMA],
        compiler_params=_cparams(("parallel", "arbitrary")),
        name="combine",
    )(dest, gw0, gw1, x1, g2, l2g, l2b, ypad)


def _block_diag(w):
    nh, blk, _ = w.shape
    w4 = w.reshape(nh // LRU_PACK, LRU_PACK, blk, blk)
    eye = jnp.eye(LRU_PACK, dtype=w.dtype)
    out = jnp.einsum("gaij,ab->gaibj", w4, eye)
    return out.reshape(nh // LRU_PACK, LRU_PACK * blk, LRU_PACK * blk).astype(BF16)


def _row(v):
    return v.reshape(1, -1).astype(F32)


def _pad_lanes(v):
    return jnp.pad(v.reshape(1, -1).astype(F32), ((0, 0), (0, LANES - v.shape[-1])))


def kernel(x_prompt, x_sample, c_prompt, c_sample, state_lru_conv, state_lru_h, state_ssd_conv, state_ssd, w_ada, b_ada, w_in, lru_conv_w, lru_conv_b, lru_wa, lru_ba, lru_wx, lru_bx, lru_lambda, lru_norm_g, ssd_conv_w, ssd_conv_b, ssd_dt_bias, ssd_a_log, ssd_d, ssd_norm_g, w_out, ln1_g, ln1_b, w_rg, b_rg, w_re, b_re, w_gate, w_up, w_down, ln2_g, ln2_b):
    assert w_ada.shape[0] == DEPTH == 1
    nbp, seq, _ = x_prompt.shape
    nbs = x_sample.shape[0]
    t_p = nbp * seq
    t_all = t_p + nbs
    assert x_sample.shape[1] == 1 and t_p % TOK_TILE == 0 and nbs % TOK_TILE == 0

    c_rows = -(-(nbp + nbs) // 16) * 16
    c_all = jnp.pad(jnp.concatenate([c_prompt, c_sample], axis=0), ((0, c_rows - nbp - nbs), (0, 0)))
    w_in0 = w_in[0]
    o_z, o_xbc, o_dt = 2 * LRU_W, 2 * LRU_W + SSD_INNER, 2 * LRU_W + SSD_INNER + SSD_CONV_DIM
    w_dt = w_in0[:, o_dt:]
    ws_p = [w_in0[:, :o_z].astype(BF16), w_in0[:, o_z:o_xbc].astype(BF16), w_in0[:, o_xbc:o_dt].astype(BF16),
            jnp.pad(w_dt, ((0, 0), (0, LANES - SSD_HEADS))).astype(BF16)]
    ws_s = ws_p[:3] + [jnp.repeat(w_dt, SSD_P, axis=1).astype(BF16)]
    lp = dict(cw=lru_conv_w[0], cb=_row(lru_conv_b[0]), wa=_block_diag(lru_wa[0]), wx=_block_diag(lru_wx[0]),
              ba=_row(lru_ba[0]), bx=_row(lru_bx[0]), lam=_row(lru_lambda[0]), ng=_row(lru_norm_g[0]))
    sp = dict(cw=ssd_conv_w[0], cb=_row(ssd_conv_b[0]), dtb=_pad_lanes(ssd_dt_bias[0]), alog=_pad_lanes(ssd_a_log[0]),
              dexp=_row(jnp.repeat(ssd_d[0], SSD_P)), ng=_row(ssd_norm_g[0]),
              dtbe=_row(jnp.repeat(ssd_dt_bias[0], SSD_P)), aloge=_row(jnp.repeat(ssd_a_log[0], SSD_P)))
    wrt = jnp.zeros((ROUTE_ROWS, D), F32).at[0:N_GROUPS].set(w_rg[0].T).at[GROUP_SIZE:].set(w_re[0].T)
    brc = jnp.zeros((ROUTE_ROWS, 1), F32).at[0:N_GROUPS, 0].set(b_rg[0]).at[GROUP_SIZE:, 0].set(b_re[0])
    pp = dict(wo=w_out[0].astype(BF16), l1g=_row(ln1_g[0]), l1b=_row(ln1_b[0]), wrt=wrt, brc=brc)

    mod = _ada_call(c_all, w_ada[0], _row(b_ada[0])).reshape(c_rows, N_MOD, D)
    mp = [mod[:nbp, k].reshape(nbp, 1, D) for k in range(N_MOD)]
    ms = [mod[nbp:nbp + nbs, k].reshape(1, nbs, D) for k in range(N_MOD)]

    xg, z, xbc, dtr = _proj_call(x_prompt, mp[0], mp[1], ws_p, 256)
    yl, p_lru_conv, p_lru_h = _lru_call(xg, lp, 256)
    ys, p_ssd_conv, p_ssd = _ssd_call(xbc, z, dtr, sp)
    x1_p, v_p, eid_p, gw_p = _post_call(yl, ys, x_prompt, mp[2], mp[3], mp[4], pp, 256)

    xs3 = x_sample.reshape(1, nbs, D)
    xg_s, z_s, xbc_s, dte_s = _proj_call(xs3, ms[0], ms[1], ws_s, nbs)
    xg_s, z_s, xbc_s, dte_s = xg_s[0], z_s[0], xbc_s[0], dte_s[0]
    yl_s, h_s, xs_s, bm_s, cm_s, xdtt, dect = _srow_call(
        xg_s, xbc_s, dte_s, state_lru_conv[0].reshape(nbs, -1), state_lru_h[0],
        state_ssd_conv[0].reshape(nbs, -1), lp, sp)
    sb = SUBLANES
    regroup = lambda a: a.reshape(SSD_INNER, nbs // sb, sb).transpose(1, 0, 2)
    s_new, ys_s = _sstate_call(state_ssd[0].reshape(nbs, SSD_INNER, SSD_N), regroup(xdtt), regroup(dect),
                               bm_s, cm_s, xs_s, z_s, sp["dexp"], sp["ng"], sb)
    x1_s, v_s, eid_s, gw_s = _post_call(yl_s[None], ys_s[None], xs3, ms[2], ms[3], ms[4], pp, nbs)

    dest, counts = _route_call(eid_p, eid_s)
    counts = counts[:N_EXPERTS, 0].astype(I32)
    n_blocks = -(-(2 * t_all) // MOE_BM) + N_EXPERTS
    pend = jnp.cumsum(((counts + MOE_BM - 1) // MOE_BM) * MOE_BM)
    blk_e = jnp.clip(jnp.searchsorted(pend, jnp.arange(n_blocks, dtype=I32) * MOE_BM, side="right"),
                     0, N_EXPERTS - 1).astype(I32)
    nblk = (pend[-1:] // MOE_BM).astype(I32)
    xpad = _dispatch_call(dest, v_p, v_s, jnp.zeros((n_blocks * MOE_BM, D), F32))
    ypad = _expert_call(blk_e, nblk, xpad, w_gate[0], w_up[0], w_down[0])
    l2g, l2b = _row(ln2_g[0]), _row(ln2_b[0])
    y_p = _combine_call(dest, gw_p[0].reshape(t_p, 1), gw_p[1].reshape(t_p, 1), x1_p, mp[5], l2g, l2b, ypad, 0)
    y_s = _combine_call(dest, gw_s[0].reshape(nbs, 1), gw_s[1].reshape(nbs, 1), x1_s, ms[5], l2g, l2b, ypad, t_p)

    s_lru_conv = jnp.concatenate([state_lru_conv[0][:, 1:], xg_s[:, None, :LRU_W]], axis=1)
    s_ssd_conv = jnp.concatenate([state_ssd_conv[0][:, 1:], xbc_s[:, None, :]], axis=1)
    return (y_p, y_s.reshape(nbs, 1, D),
            p_lru_conv[None], p_lru_h.reshape(1, nbp, LRU_W), p_ssd_conv[None],
            p_ssd.reshape(1, nbp, SSD_HEADS, SSD_P, SSD_N),
            s_lru_conv[None], h_s[None], s_ssd_conv[None],
            s_new.reshape(1, nbs, SSD_HEADS, SSD_P, SSD_N))
```

```python
import functools
import math

import jax
import jax.numpy as jnp
from jax import lax
from jax.experimental import pallas as pl
from jax.experimental.pallas import tpu as pltpu

F32 = jnp.float32
BF16 = jnp.bfloat16
I32 = jnp.int32
HIGHEST = lax.Precision.HIGHEST

D = 1024
DEPTH = 1
CONV_W = 4
LRU_W = D
LRU_HEADS = 16
LRU_C = 8.0
LRU_PACK = 4
LRU_PACK_W = LRU_PACK * (LRU_W // LRU_HEADS)
SSD_INNER = D
SSD_HEADS = 16
SSD_P = SSD_INNER // SSD_HEADS
SSD_GROUPS = 2
SSD_N = 128
SSD_Q = 128
SSD_CONV_DIM = SSD_INNER + 2 * SSD_GROUPS * SSD_N
N_GROUPS = 4
GROUP_SIZE = 8
N_EXPERTS = N_GROUPS * GROUP_SIZE
D_FF = D // 2
N_MOD = 6
LN_EPS = 1e-5
RMS_EPS = 1e-6
ALPHA = (2.0 * DEPTH) ** 0.25

LANES = 128
SUBLANES = 8
VMEM_LIMIT = 56 * 1024 * 1024
TOK_TILE = 128
ROUTE_TILE = 512
ROW_TILE = (SUBLANES, LANES)
DMA_UNROLL = 8
MOE_BM = 256
MOE_BM_SHIFT = 8
ROUTE_ROWS = 40

NT_DIMS = (((1,), (1,)), ((), ()))


def _cparams(sem):
    return pltpu.CompilerParams(dimension_semantics=sem, vmem_limit_bytes=VMEM_LIMIT)


def _sigmoid(x):
    return 0.5 * (jnp.tanh(0.5 * x) + 1.0)


def _silu(x):
    return x * _sigmoid(x)


def _softplus(x):
    return jnp.maximum(x, 0.0) + jnp.log1p(jnp.exp(-jnp.abs(x)))


def _gelu_tanh(x):
    return 0.5 * x * (1.0 + jnp.tanh(math.sqrt(2.0 / math.pi) * (x + 0.044715 * (x * x * x))))


def _layer_norm(x):
    mu = jnp.mean(x, axis=-1, keepdims=True)
    xc = x - mu
    var = jnp.mean(xc * xc, axis=-1, keepdims=True)
    return xc * lax.rsqrt(var + LN_EPS)


def _rms_norm(x, g):
    return x * lax.rsqrt(jnp.mean(x * x, axis=-1, keepdims=True) + RMS_EPS) * g


def _dot(a, b, **kw):
    return jnp.dot(a, b, preferred_element_type=F32, **kw)


def _dot_nt(a, b, **kw):
    return lax.dot_general(a, b, NT_DIMS, preferred_element_type=F32, **kw)


def _store_row_tiles(ref, val):
    for j in range(SUBLANES):
        ref[:, j, :] = val[:, j * LANES:(j + 1) * LANES]


def _load_row_tiles(ref):
    return jnp.concatenate([ref[:, j, :] for j in range(SUBLANES)], axis=1)


def _lru_gates(xc, xb, wa, wx, ba, bx, sp):
    r = _sigmoid(_dot(xb, wa) + ba)
    i = _sigmoid(_dot(xb, wx) + bx)
    log_a = (-LRU_C) * r * sp
    a = jnp.exp(log_a)
    mult = jnp.sqrt(jnp.tanh(-log_a) * (a * a + 1.0))
    return a, mult * (i * xc)


def _ada_kernel(c_ref, w_ref, b_ref, o_ref):
    s = _silu(c_ref[...]).astype(BF16)
    o_ref[...] = _dot(s, w_ref[...].astype(BF16)) + b_ref[...]


def _ada_call(c_all, w_ada, b_ada):
    rows = c_all.shape[0]
    n = w_ada.shape[1]
    tn = 512
    return pl.pallas_call(
        _ada_kernel,
        out_shape=jax.ShapeDtypeStruct((rows, n), F32),
        grid=(n // tn,),
        in_specs=[pl.BlockSpec((rows, D), lambda j: (0, 0)),
                  pl.BlockSpec((D, tn), lambda j: (0, j)),
                  pl.BlockSpec((1, tn), lambda j: (0, j))],
        out_specs=pl.BlockSpec((rows, tn), lambda j: (0, j)),
        compiler_params=_cparams(("arbitrary",)),
        name="ada",
    )(c_all, w_ada, b_ada)


def _proj_kernel(x_ref, sh_ref, sc_ref, w1_ref, w2_ref, w3_ref, w4_ref, o1_ref, o2_ref, o3_ref, o4_ref):
    u = _layer_norm(x_ref[...]) * (1.0 + sc_ref[...]) + sh_ref[...]
    ub = u.astype(BF16)
    o1_ref[...] = _dot(ub, w1_ref[...])
    o2_ref[...] = _dot(ub, w2_ref[...])
    o3_ref[...] = _dot(ub, w3_ref[...])
    o4_ref[...] = _dot(ub, w4_ref[...])


def _mod_spec(mod, tl):
    if mod.shape[1] == 1:
        return pl.BlockSpec((None, 1, D), lambda b, l: (b, 0, 0))
    return pl.BlockSpec((None, tl, D), lambda b, l: (b, l, 0))


def _proj_call(x3, sh3, sc3, ws, tl):
    nb, L, _ = x3.shape
    widths = [w.shape[1] for w in ws]
    row = lambda b, l: (b, l, 0)
    full = lambda b, l: (0, 0)
    return pl.pallas_call(
        _proj_kernel,
        out_shape=[jax.ShapeDtypeStruct((nb, L, n), F32) for n in widths],
        grid=(nb, L // tl),
        in_specs=[pl.BlockSpec((None, tl, D), row), _mod_spec(sh3, tl), _mod_spec(sc3, tl)]
                 + [pl.BlockSpec((D, n), full) for n in widths],
        out_specs=[pl.BlockSpec((None, tl, n), row) for n in widths],
        compiler_params=_cparams(("parallel", "arbitrary")),
        name="proj",
    )(x3, sh3, sc3, *ws)


def _scan_rows(a_scr, u_scr, h0, n_rows):
    sub = lax.broadcasted_iota(I32, (SUBLANES, D), 0)
    shifts = (1, 2, 4)
    masks = [sub >= d for d in shifts]

    def body(j, hp):
        r0 = pl.multiple_of(j * SUBLANES, SUBLANES)
        a = a_scr[pl.ds(r0, SUBLANES), :]
        u = u_scr[pl.ds(r0, SUBLANES), :]
        for d, m in zip(shifts, masks):
            ash = pltpu.roll(a, d, 0)
            ush = pltpu.roll(u, d, 0)
            u = jnp.where(m, u + a * ush, u)
            a = jnp.where(m, a * ash, a)
        h = u + a * hp
        u_scr[pl.ds(r0, SUBLANES), :] = h
        return h[SUBLANES - 1:SUBLANES, :]

    return lax.fori_loop(0, n_rows // SUBLANES, body, h0)


def _lru_kernel(x_ref, g_ref, cw_ref, cb_ref, wa_ref, wx_ref, ba_ref, bx_ref, lam_ref, ng_ref,
                y_ref, conv_ref, h_ref, xpad, xc_scr, a_scr, u_scr, hc_scr, *, tl):
    @pl.when(pl.program_id(1) == 0)
    def _():
        xpad[0:SUBLANES, :] = jnp.zeros((SUBLANES, D), F32)
        hc_scr[...] = jnp.zeros((1, D), F32)

    xpad[SUBLANES:SUBLANES + tl, :] = x_ref[...]
    off = SUBLANES - (CONV_W - 1)
    acc = xpad[off:off + tl, :] * cw_ref[0:1, :]
    for k in range(1, CONV_W):
        acc = acc + xpad[off + k:off + k + tl, :] * cw_ref[k:k + 1, :]
    xc_scr[...] = acc + cb_ref[...]
    conv_ref[...] = xpad[SUBLANES + tl - (CONV_W - 1):SUBLANES + tl, :]
    xpad[0:SUBLANES, :] = xpad[tl:tl + SUBLANES, :]

    sp = _softplus(-lam_ref[...])
    for j in range(LRU_W // LRU_PACK_W):
        cs = slice(j * LRU_PACK_W, (j + 1) * LRU_PACK_W)
        xc = xc_scr[:, cs]
        a, u = _lru_gates(xc, xc.astype(BF16), wa_ref[j], wx_ref[j], ba_ref[:, cs], bx_ref[:, cs], sp[:, cs])
        a_scr[:, cs] = a
        u_scr[:, cs] = u

    h_last = _scan_rows(a_scr, u_scr, hc_scr[...], tl)
    hc_scr[...] = h_last
    h_ref[...] = h_last
    y = u_scr[...] * _gelu_tanh(g_ref[...])
    y_ref[...] = _rms_norm(y, ng_ref[...]).astype(BF16)


def _lru_call(xg, p, tl):
    nb, L, _ = xg.shape
    vec = lambda b, l: (0, 0)
    blk = lambda b, l: (0, 0, 0)
    return pl.pallas_call(
        functools.partial(_lru_kernel, tl=tl),
        out_shape=[jax.ShapeDtypeStruct((nb, L, LRU_W), BF16),
                   jax.ShapeDtypeStruct((nb, CONV_W - 1, LRU_W), F32),
                   jax.ShapeDtypeStruct((nb, 1, LRU_W), F32)],
        grid=(nb, L // tl),
        in_specs=[pl.BlockSpec((None, tl, LRU_W), lambda b, l: (b, l, 0)),
                  pl.BlockSpec((None, tl, LRU_W), lambda b, l: (b, l, 1)),
                  pl.BlockSpec((CONV_W, LRU_W), vec), pl.BlockSpec((1, LRU_W), vec),
                  pl.BlockSpec((LRU_W // LRU_PACK_W, LRU_PACK_W, LRU_PACK_W), blk),
                  pl.BlockSpec((LRU_W // LRU_PACK_W, LRU_PACK_W, LRU_PACK_W), blk),
                  pl.BlockSpec((1, LRU_W), vec), pl.BlockSpec((1, LRU_W), vec),
                  pl.BlockSpec((1, LRU_W), vec), pl.BlockSpec((1, LRU_W), vec)],
        out_specs=[pl.BlockSpec((None, tl, LRU_W), lambda b, l: (b, l, 0)),
                   pl.BlockSpec((None, CONV_W - 1, LRU_W), lambda b, l: (b, 0, 0)),
                   pl.BlockSpec((None, 1, LRU_W), lambda b, l: (b, 0, 0))],
        scratch_shapes=[pltpu.VMEM((tl + SUBLANES, LRU_W), F32), pltpu.VMEM((tl, LRU_W), F32),
                        pltpu.VMEM((tl, LRU_W), F32), pltpu.VMEM((tl, LRU_W), F32),
                        pltpu.VMEM((1, LRU_W), F32)],
        compiler_params=_cparams(("parallel", "arbitrary")),
        name="lru",
    )(xg, xg, p["cw"], p["cb"], p["wa"], p["wx"], p["ba"], p["bx"], p["lam"], p["ng"])


def _ssd_kernel(xbc_ref, z_ref, dt_ref, cw_ref, cb_ref, dtb_ref, alog_ref, dexp_ref, ng_ref,
                y_ref, conv_ref, st_ref, xpad, xc_scr, st_scr, y_scr):
    q = SSD_Q

    @pl.when(pl.program_id(1) == 0)
    def _():
        xpad[0:SUBLANES, :] = jnp.zeros((SUBLANES, SSD_CONV_DIM), F32)
        st_scr[...] = jnp.zeros((SSD_N, SSD_INNER), F32)

    xpad[SUBLANES:SUBLANES + q, :] = xbc_ref[...]
    off = SUBLANES - (CONV_W - 1)
    acc = xpad[off:off + q, :] * cw_ref[0:1, :]
    for k in range(1, CONV_W):
        acc = acc + xpad[off + k:off + k + q, :] * cw_ref[k:k + 1, :]
    xc_scr[...] = _silu(acc + cb_ref[...])
    conv_ref[...] = xpad[SUBLANES + q - (CONV_W - 1):SUBLANES + q, :]
    xpad[0:SUBLANES, :] = xpad[q:q + SUBLANES, :]

    dt = _softplus(dt_ref[...] + dtb_ref[...])
    da = dt * (-jnp.exp(alog_ref[...]))
    ri = lax.broadcasted_iota(I32, (q, q), 0)
    ci = lax.broadcasted_iota(I32, (q, q), 1)
    causal = ri >= ci
    acs = _dot(causal.astype(F32), da, precision=HIGHEST)
    acs_t = acs.T
    tot = acs[q - 1:q, :]
    chunk_decay = jnp.exp(tot)
    e_acs = jnp.exp(acs)
    d_end = jnp.exp(tot - acs)

    for g in range(SSD_GROUPS):
        bg = xc_scr[:, SSD_INNER + g * SSD_N:SSD_INNER + (g + 1) * SSD_N]
        cg = xc_scr[:, SSD_INNER + (SSD_GROUPS + g) * SSD_N:SSD_INNER + (SSD_GROUPS + g + 1) * SSD_N]
        bgb = bg.astype(BF16)
        cgb = cg.astype(BF16)
        bgt = bg.T.astype(BF16)
        cb = _dot_nt(cgb, bgb)
        for j in range(SSD_HEADS // SSD_GROUPS):
            h = g * (SSD_HEADS // SSD_GROUPS) + j
            hs = slice(h * SSD_P, (h + 1) * SSD_P)
            seg = acs[:, h:h + 1] - acs_t[h:h + 1, :]
            decay = jnp.where(causal, jnp.exp(jnp.where(causal, seg, 0.0)), 0.0)
            xh = xc_scr[:, hs]
            xdt = xh * dt[:, h:h + 1]
            y_diag = _dot((cb * decay).astype(BF16), xdt.astype(BF16))
            s_in = st_scr[:, hs]
            y_off = _dot(cgb, s_in.astype(BF16)) * e_acs[:, h:h + 1]
            chunk_state = _dot(bgt, (xdt * d_end[:, h:h + 1]).astype(BF16))
            st_scr[:, hs] = chunk_decay[:, h:h + 1] * s_in + chunk_state
            y_scr[:, hs] = y_diag + y_off + dexp_ref[:, hs] * xh

    yz = y_scr[...] * _silu(z_ref[...])
    y_ref[...] = _rms_norm(yz, ng_ref[...]).astype(BF16)

    @pl.when(pl.program_id(1) == pl.num_programs(1) - 1)
    def _():
        st_ref[...] = st_scr[...].T


def _ssd_call(xbc, z, dtr, p):
    nb, L, _ = xbc.shape
    q = SSD_Q
    vec = lambda b, c: (0, 0)
    row = lambda b, c: (b, c, 0)
    return pl.pallas_call(
        _ssd_kernel,
        out_shape=[jax.ShapeDtypeStruct((nb, L, SSD_INNER), BF16),
                   jax.ShapeDtypeStruct((nb, CONV_W - 1, SSD_CONV_DIM), F32),
                   jax.ShapeDtypeStruct((nb, SSD_INNER, SSD_N), F32)],
        grid=(nb, L // q),
        in_specs=[pl.BlockSpec((None, q, SSD_CONV_DIM), row), pl.BlockSpec((None, q, SSD_INNER), row),
                  pl.BlockSpec((None, q, LANES), row),
                  pl.BlockSpec((CONV_W, SSD_CONV_DIM), vec), pl.BlockSpec((1, SSD_CONV_DIM), vec),
                  pl.BlockSpec((1, LANES), vec), pl.BlockSpec((1, LANES), vec),
                  pl.BlockSpec((1, SSD_INNER), vec), pl.BlockSpec((1, SSD_INNER), vec)],
        out_specs=[pl.BlockSpec((None, q, SSD_INNER), row),
                   pl.BlockSpec((None, CONV_W - 1, SSD_CONV_DIM), lambda b, c: (b, 0, 0)),
                   pl.BlockSpec((None, SSD_INNER, SSD_N), lambda b, c: (b, 0, 0))],
        scratch_shapes=[pltpu.VMEM((q + SUBLANES, SSD_CONV_DIM), F32), pltpu.VMEM((q, SSD_CONV_DIM), F32),
                        pltpu.VMEM((SSD_N, SSD_INNER), F32), pltpu.VMEM((q, SSD_INNER), F32)],
        compiler_params=_cparams(("parallel", "arbitrary")),
        name="ssd",
    )(xbc, z, dtr, p["cw"], p["cb"], p["dtb"], p["alog"], p["dexp"], p["ng"])


def _srow_kernel(xg_ref, xbc_ref, dte_ref, lconv_ref, h0_ref, sconv_ref,
                 lcw_ref, lcb_ref, wa_ref, wx_ref, ba_ref, bx_ref, lam_ref, lng_ref,
                 scw_ref, scb_ref, dtbe_ref, aloge_ref,
                 yl_ref, h_ref, xs_ref, bm_ref, cm_ref, xdtt_ref, dect_ref, xc_scr):
    xl = xg_ref[:, 0:LRU_W]
    acc = lcb_ref[...] + xl * lcw_ref[CONV_W - 1:CONV_W, :]
    for k in range(CONV_W - 1):
        acc = acc + lconv_ref[:, k * LRU_W:(k + 1) * LRU_W] * lcw_ref[k:k + 1, :]
    xc_scr[...] = acc
    sp = _softplus(-lam_ref[...])
    for j in range(LRU_W // LRU_PACK_W):
        cs = slice(j * LRU_PACK_W, (j + 1) * LRU_PACK_W)
        xc = xc_scr[:, cs]
        a, u = _lru_gates(xc, xc.astype(BF16), wa_ref[j], wx_ref[j], ba_ref[:, cs], bx_ref[:, cs], sp[:, cs])
        h_ref[:, cs] = a * h0_ref[:, cs] + u
    y = h_ref[...] * _gelu_tanh(xg_ref[:, LRU_W:2 * LRU_W])
    yl_ref[...] = _rms_norm(y, lng_ref[...]).astype(BF16)

    acc = scb_ref[...] + xbc_ref[...] * scw_ref[CONV_W - 1:CONV_W, :]
    for k in range(CONV_W - 1):
        acc = acc + sconv_ref[:, k * SSD_CONV_DIM:(k + 1) * SSD_CONV_DIM] * scw_ref[k:k + 1, :]
    xc = _silu(acc)
    xs = xc[:, 0:SSD_INNER]
    xs_ref[...] = xs
    bm_ref[...] = xc[:, SSD_INNER:SSD_INNER + SSD_GROUPS * SSD_N]
    cm_ref[...] = xc[:, SSD_INNER + SSD_GROUPS * SSD_N:]
    dt = _softplus(dte_ref[...] + dtbe_ref[...])
    dec = jnp.exp(dt * (-jnp.exp(aloge_ref[...])))
    xdtt_ref[...] = (xs * dt).T
    dect_ref[...] = dec.T


def _srow_call(xg, xbc, dte, lconv, h0, sconv, lp, sp):
    n = xg.shape[0]
    args = (xg, xbc, dte, lconv, h0, sconv, lp["cw"], lp["cb"], lp["wa"], lp["wx"], lp["ba"], lp["bx"],
            lp["lam"], lp["ng"], sp["cw"], sp["cb"], sp["dtbe"], sp["aloge"])
    full = lambda a: pl.BlockSpec(a.shape, lambda i, nd=a.ndim: (0,) * nd)
    outs = [jax.ShapeDtypeStruct((n, LRU_W), BF16), jax.ShapeDtypeStruct((n, LRU_W), F32),
            jax.ShapeDtypeStruct((n, SSD_INNER), F32), jax.ShapeDtypeStruct((n, SSD_GROUPS * SSD_N), F32),
            jax.ShapeDtypeStruct((n, SSD_GROUPS * SSD_N), F32),
            jax.ShapeDtypeStruct((SSD_INNER, n), F32), jax.ShapeDtypeStruct((SSD_INNER, n), F32)]
    return pl.pallas_call(
        _srow_kernel,
        out_shape=outs,
        grid=(1,),
        in_specs=[full(a) for a in args],
        out_specs=[pl.BlockSpec(o.shape, lambda i: (0, 0)) for o in outs],
        scratch_shapes=[pltpu.VMEM((n, LRU_W), F32)],
        compiler_params=_cparams(("arbitrary",)),
        name="srow",
    )(*args)


def _sstate_kernel(s0_ref, xq_ref, dq_ref, bm_ref, cm_ref, xs_ref, z_ref, dexp_ref, ng_ref,
                   sn_ref, ys_ref, yraw):
    nb = s0_ref.shape[0]
    half = SSD_INNER // SSD_GROUPS
    for bi in range(nb):
        brow = jnp.concatenate(
            [jnp.broadcast_to(bm_ref[bi:bi + 1, g * SSD_N:(g + 1) * SSD_N], (half, SSD_N)) for g in range(SSD_GROUPS)],
            axis=0)
        s = dq_ref[:, bi:bi + 1] * s0_ref[bi] + xq_ref[:, bi:bi + 1] * brow
        sn_ref[bi] = s
        sb = s.astype(BF16)
        for g in range(SSD_GROUPS):
            cg = cm_ref[:, g * SSD_N:(g + 1) * SSD_N].astype(BF16)
            res = _dot_nt(cg, sb[g * half:(g + 1) * half, :])
            yraw[bi:bi + 1, g * half:(g + 1) * half] = res[bi:bi + 1, :]
    y = yraw[...] + dexp_ref[...] * xs_ref[...]
    ys_ref[...] = _rms_norm(y * _silu(z_ref[...]), ng_ref[...]).astype(BF16)


def _sstate_call(s0, xq, dq, bm, cm, xs, z, dexp, ng, nb):
    n = s0.shape[0]
    row = lambda i: (i, 0)
    vec = lambda i: (0, 0)
    gn = SSD_GROUPS * SSD_N
    return pl.pallas_call(
        _sstate_kernel,
        out_shape=[jax.ShapeDtypeStruct(s0.shape, F32), jax.ShapeDtypeStruct((n, SSD_INNER), BF16)],
        grid=(n // nb,),
        in_specs=[pl.BlockSpec((nb, SSD_INNER, SSD_N), lambda i: (i, 0, 0)),
                  pl.BlockSpec((None, SSD_INNER, nb), lambda i: (i, 0, 0)),
                  pl.BlockSpec((None, SSD_INNER, nb), lambda i: (i, 0, 0)),
                  pl.BlockSpec((nb, gn), row), pl.BlockSpec((nb, gn), row),
                  pl.BlockSpec((nb, SSD_INNER), row), pl.BlockSpec((nb, SSD_INNER), row),
                  pl.BlockSpec((1, SSD_INNER), vec), pl.BlockSpec((1, SSD_INNER), vec)],
        out_specs=[pl.BlockSpec((nb, SSD_INNER, SSD_N), lambda i: (i, 0, 0)), pl.BlockSpec((nb, SSD_INNER), row)],
        scratch_shapes=[pltpu.VMEM((nb, SSD_INNER), F32)],
        compiler_params=_cparams(("parallel",)),
        name="sstate",
    )(s0, xq, dq, bm, cm, xs, z, dexp, ng)


def _post_kernel(yl_ref, ys_ref, x_ref, g1_ref, sh2_ref, sc2_ref, wo_ref, l1g_ref, l1b_ref, wrt_ref, brc_ref,
                 x1_ref, v_ref, eid_ref, gw_ref):
    o = _dot(yl_ref[...], wo_ref[0:LRU_W, :]) + _dot(ys_ref[...], wo_ref[LRU_W:LRU_W + SSD_INNER, :])
    x1 = _layer_norm(ALPHA * x_ref[...] + g1_ref[...] * o) * l1g_ref[...] + l1b_ref[...]
    x1_ref[...] = x1
    v = _layer_norm(x1) * (1.0 + sc2_ref[...]) + sh2_ref[...]
    _store_row_tiles(v_ref, v)

    lt = _dot_nt(wrt_ref[...], v, precision=HIGHEST) + brc_ref[...]
    tl = lt.shape[1]
    row = lax.broadcasted_iota(I32, (GROUP_SIZE, tl), 0).astype(F32)
    big = float(GROUP_SIZE)
    neg = -jnp.inf
    lg = jnp.where(row < N_GROUPS, lt[0:GROUP_SIZE, :], neg)
    gmax = jnp.max(lg, axis=0, keepdims=True)
    gsel = jnp.min(jnp.where(lg == gmax, row, big), axis=0, keepdims=True)
    pg = 1.0 / jnp.sum(jnp.exp(lg - gmax), axis=0, keepdims=True)
    le = lt[GROUP_SIZE:2 * GROUP_SIZE, :]
    for j in range(1, N_GROUPS):
        le = jnp.where(gsel == j, lt[GROUP_SIZE * (j + 1):GROUP_SIZE * (j + 2), :], le)
    m1 = jnp.max(le, axis=0, keepdims=True)
    i1 = jnp.min(jnp.where(le == m1, row, big), axis=0, keepdims=True)
    rest = jnp.where(row == i1, neg, le)
    m2 = jnp.max(rest, axis=0, keepdims=True)
    i2 = jnp.min(jnp.where(rest == m2, row, big), axis=0, keepdims=True)
    e2 = jnp.exp(m2 - m1)
    den = 1.0 + e2
    eid = jnp.where(row == 0, gsel * GROUP_SIZE + i1, jnp.where(row == 1, gsel * GROUP_SIZE + i2, 0.0))
    eid_ref[...] = eid.astype(I32)
    gw_ref[...] = jnp.where(row == 0, pg * (1.0 / den), jnp.where(row == 1, pg * (e2 / den), 0.0))


def _post_call(yl, ys, x3, g1, sh2, sc2, p, tl):
    nb, L, _ = x3.shape
    nl = L // tl
    row = lambda b, l: (b, l, 0)
    vec = lambda b, l: (0, 0)
    tok = lambda b, l: (b * nl + l, 0, 0)
    tokt = lambda b, l: (0, b * nl + l)
    return pl.pallas_call(
        _post_kernel,
        out_shape=[jax.ShapeDtypeStruct((nb, L, D), F32), jax.ShapeDtypeStruct((nb * L,) + ROW_TILE, F32),
                   jax.ShapeDtypeStruct((SUBLANES, nb * L), I32), jax.ShapeDtypeStruct((SUBLANES, nb * L), F32)],
        grid=(nb, nl),
        in_specs=[pl.BlockSpec((None, tl, LRU_W), row), pl.BlockSpec((None, tl, SSD_INNER), row),
                  pl.BlockSpec((None, tl, D), row), _mod_spec(g1, tl), _mod_spec(sh2, tl), _mod_spec(sc2, tl),
                  pl.BlockSpec((LRU_W + SSD_INNER, D), vec), pl.BlockSpec((1, D), vec), pl.BlockSpec((1, D), vec),
                  pl.BlockSpec((ROUTE_ROWS, D), vec), pl.BlockSpec((ROUTE_ROWS, 1), vec)],
        out_specs=[pl.BlockSpec((None, tl, D), row), pl.BlockSpec((tl,) + ROW_TILE, tok),
                   pl.BlockSpec((SUBLANES, tl), tokt), pl.BlockSpec((SUBLANES, tl), tokt)],
        compiler_params=_cparams(("parallel", "arbitrary")),
        name="post",
    )(yl, ys, x3, g1, sh2, sc2, p["wo"], p["l1g"], p["l1b"], p["wrt"], p["brc"])


def _route_kernel(eidp_ref, eids_ref, dest_ref, cnt_ref, run, poff, *, p_tiles):
    ph = pl.program_id(0)
    t = pl.program_id(1)
    n = ROUTE_TILE
    eid = jnp.where(t < p_tiles, eidp_ref[...], eids_ref[...])
    rowi = lax.broadcasted_iota(I32, (LANES, n), 0)
    oh0 = rowi == eid[0:1, :]
    oh1 = rowi == eid[1:2, :]
    oh = oh0.astype(F32) + oh1.astype(F32)
    tile_cnt = jnp.sum(oh, axis=1, keepdims=True)

    @pl.when((ph == 0) & (t == 0))
    def _():
        run[...] = jnp.zeros((LANES, n), F32)

    @pl.when(ph == 0)
    def _():
        run[...] = run[...] + tile_cnt

    @pl.when((ph == 1) & (t == 0))
    def _():
        counts = run[...]
        cnt_ref[...] = counts
        nblk = (counts.astype(I32) + (MOE_BM - 1)) >> MOE_BM_SHIFT
        r = lax.broadcasted_iota(I32, (LANES, LANES), 0)
        c = lax.broadcasted_iota(I32, (LANES, LANES), 1)
        lower = (r > c).astype(BF16)
        poff[...] = _dot(lower, nblk.astype(F32).astype(BF16)) * float(MOE_BM)
        run[...] = jnp.zeros((LANES, n), F32)

    @pl.when(ph == 1)
    def _():
        r = lax.broadcasted_iota(I32, (n, n), 0)
        c = lax.broadcasted_iota(I32, (n, n), 1)
        before = (r < c).astype(BF16)
        slot = _dot(oh.astype(BF16), before) + run[...] + poff[...]
        d0 = jnp.sum(jnp.where(oh0, slot, 0.0), axis=0, keepdims=True)
        d1 = jnp.sum(jnp.where(oh1, slot, 0.0), axis=0, keepdims=True)
        row = lax.broadcasted_iota(I32, (SUBLANES, n), 0)
        dest_ref[...] = jnp.where(row == 0, d0, jnp.where(row == 1, d1, 0.0)).astype(I32)
        run[...] = run[...] + tile_cnt


def _route_call(eid_p, eid_s):
    n = ROUTE_TILE
    p_tiles = eid_p.shape[1] // n
    s_tiles = eid_s.shape[1] // n
    t_all = (p_tiles + s_tiles) * n
    return pl.pallas_call(
        functools.partial(_route_kernel, p_tiles=p_tiles),
        out_shape=[jax.ShapeDtypeStruct((SUBLANES, t_all), I32), jax.ShapeDtypeStruct((LANES, n), F32)],
        grid=(2, p_tiles + s_tiles),
        in_specs=[pl.BlockSpec((SUBLANES, n), lambda ph, t: (0, jnp.minimum(t, p_tiles - 1))),
                  pl.BlockSpec((SUBLANES, n), lambda ph, t: (0, jnp.maximum(t - p_tiles, 0)))],
        out_specs=[pl.BlockSpec((SUBLANES, n), lambda ph, t: (0, t * ph)),
                   pl.BlockSpec((LANES, n), lambda ph, t: (0, 0))],
        scratch_shapes=[pltpu.VMEM((LANES, n), F32), pltpu.VMEM((LANES, n), F32)],
        compiler_params=_cparams(("arbitrary", "arbitrary")),
        name="route",
    )(eid_p, eid_s)


def _row_copy(src, src_row, dst, dst_row, sem):
    return pltpu.make_async_copy(src.at[src_row], dst.at[dst_row], sem)


def _scatter_rows(dest_ref, v_ref, xpad_ref, sem):
    def start(r, carry):
        for k in range(2):
            _row_copy(v_ref, r, xpad_ref, dest_ref[k, r], sem).start(priority=k)
        return carry

    def wait(r, carry):
        for k in range(2):
            _row_copy(v_ref, r, xpad_ref, dest_ref[k, r], sem).wait()
        return carry

    lax.fori_loop(0, TOK_TILE, start, 0, unroll=DMA_UNROLL)
    lax.fori_loop(0, TOK_TILE, wait, 0, unroll=DMA_UNROLL)


def _dispatch_kernel(dest_ref, vp_ref, vs_ref, xpad_in, xpad_ref, sem, *, p_tiles):
    del xpad_in
    t = pl.program_id(0)

    @pl.when(t < p_tiles)
    def _():
        _scatter_rows(dest_ref, vp_ref, xpad_ref, sem)

    @pl.when(t >= p_tiles)
    def _():
        _scatter_rows(dest_ref, vs_ref, xpad_ref, sem)


def _dispatch_call(dest, v_p, v_s, xpad0):
    p_tiles = v_p.shape[0] // TOK_TILE
    s_tiles = v_s.shape[0] // TOK_TILE
    return pl.pallas_call(
        functools.partial(_dispatch_kernel, p_tiles=p_tiles),
        out_shape=jax.ShapeDtypeStruct(xpad0.shape, xpad0.dtype),
        grid=(p_tiles + s_tiles,),
        in_specs=[pl.BlockSpec((SUBLANES, TOK_TILE), lambda t: (0, t), memory_space=pltpu.SMEM),
                  pl.BlockSpec((TOK_TILE,) + ROW_TILE, lambda t: (jnp.minimum(t, p_tiles - 1), 0, 0)),
                  pl.BlockSpec((TOK_TILE,) + ROW_TILE, lambda t: (jnp.maximum(t - p_tiles, 0), 0, 0)),
                  pl.BlockSpec(memory_space=pl.ANY)],
        out_specs=pl.BlockSpec(memory_space=pl.ANY),
        scratch_shapes=[pltpu.SemaphoreType.DMA],
        input_output_aliases={3: 0},
        compiler_params=_cparams(("arbitrary",)),
        name="dispatch",
    )(dest, v_p, v_s, xpad0)


def _expert_kernel(be_ref, nb_ref, x_ref, wg_ref, wu_ref, wd_ref, o_ref, wgb, wub, wdb):
    i = pl.program_id(0)

    @pl.when(i < nb_ref[0])
    def _():
        @pl.when((i == 0) | (be_ref[i] != be_ref[jnp.maximum(i - 1, 0)]))
        def _():
            wgb[...] = wg_ref[...].astype(BF16)
            wub[...] = wu_ref[...].astype(BF16)
            wdb[...] = wd_ref[...].astype(BF16)

        x = _load_row_tiles(x_ref).astype(BF16)
        h = _silu(_dot(x, wgb[...])) * _dot(x, wub[...])
        _store_row_tiles(o_ref, _dot(h.astype(BF16), wdb[...]))

    @pl.when(i >= nb_ref[0])
    def _():
        o_ref[...] = jnp.zeros(o_ref.shape, F32)


def _expert_call(blk_e, nblk, xpad, w_gate, w_up, w_down):
    n_rows = xpad.shape[0]
    blk = lambda i, be, nb: (jnp.minimum(i, nb[0] - 1), 0, 0)
    wsel = lambda i, be, nb: (be[jnp.minimum(i, nb[0] - 1)], 0, 0)
    return pl.pallas_call(
        _expert_kernel,
        out_shape=jax.ShapeDtypeStruct((n_rows,) + ROW_TILE, F32),
        grid_spec=pltpu.PrefetchScalarGridSpec(
            num_scalar_prefetch=2,
            grid=(n_rows // MOE_BM,),
            in_specs=[pl.BlockSpec((MOE_BM,) + ROW_TILE, blk),
                      pl.BlockSpec((None, D, D_FF), wsel), pl.BlockSpec((None, D, D_FF), wsel),
                      pl.BlockSpec((None, D_FF, D), wsel)],
            out_specs=pl.BlockSpec((MOE_BM,) + ROW_TILE, lambda i, be, nb: (i, 0, 0)),
            scratch_shapes=[pltpu.VMEM((D, D_FF), BF16), pltpu.VMEM((D, D_FF), BF16), pltpu.VMEM((D_FF, D), BF16)]),
        compiler_params=_cparams(("arbitrary",)),
        name="expert",
    )(blk_e, nblk, xpad, w_gate, w_up, w_down)


def _combine_kernel(dest_ref, gw0_ref, gw1_ref, x1_ref, g2_ref, l2g_ref, l2b_ref, ypad_ref, y_ref, ybuf, sem):
    def start(r, carry):
        for k in range(2):
            _row_copy(ypad_ref, dest_ref[k, r], ybuf.at[k], r, sem).start(priority=k)
        return carry

    def wait(r, carry):
        for k in range(2):
            _row_copy(ypad_ref, dest_ref[k, r], ybuf.at[k], r, sem).wait()
        return carry

    lax.fori_loop(0, TOK_TILE, start, 0, unroll=DMA_UNROLL)
    lax.fori_loop(0, TOK_TILE, wait, 0, unroll=DMA_UNROLL)
    f = _load_row_tiles(ybuf.at[0]) * gw0_ref[...] + _load_row_tiles(ybuf.at[1]) * gw1_ref[...]
    y_ref[...] = _layer_norm(ALPHA * x1_ref[...] + g2_ref[...] * f) * l2g_ref[...] + l2b_ref[...]


def _combine_call(dest, gw0, gw1, x1, g2, l2g, l2b, ypad, tok_off):
    nb, L, _ = x1.shape
    tl = TOK_TILE
    nl = L // tl
    blk_off = tok_off // tl
    row = lambda b, l: (b, l, 0)
    vec = lambda b, l: (0, 0)
    tok = lambda b, l: (b * nl + l, 0)
    return pl.pallas_call(
        _combine_kernel,
        out_shape=jax.ShapeDtypeStruct((nb, L, D), F32),
        grid=(nb, nl),
        in_specs=[pl.BlockSpec((SUBLANES, tl), lambda b, l: (0, blk_off + b * nl + l), memory_space=pltpu.SMEM),
                  pl.BlockSpec((tl, 1), tok), pl.BlockSpec((tl, 1), tok),
                  pl.BlockSpec((None, tl, D), row), _mod_spec(g2, tl),
                  pl.BlockSpec((1, D), vec), pl.BlockSpec((1, D), vec),
                  pl.BlockSpec(memory_space=pl.ANY)],
        out_specs=pl.BlockSpec((None, tl, D), row),
        scratch_shapes=[pltpu.VMEM((2, tl) + ROW_TILE, F32), pltpu.SemaphoreType.DMA],
        compiler_params=_cparams(("parallel", "arbitrary")),
        name="combine",
    )(dest, gw0, gw1, x1, g2, l2g, l2b, ypad)


def _block_diag(w):
    nh, blk, _ = w.shape
    w4 = w.reshape(nh // LRU_PACK, LRU_PACK, blk, blk)
    eye = jnp.eye(LRU_PACK, dtype=w.dtype)
    out = jnp.einsum("gaij,ab->gaibj", w4, eye)
    return out.reshape(nh // LRU_PACK, LRU_PACK * blk, LRU_PACK * blk).astype(BF16)


def _row(v):
    return v.reshape(1, -1).astype(F32)


def _pad_lanes(v):
    return jnp.pad(v.reshape(1, -1).astype(F32), ((0, 0), (0, LANES - v.shape[-1])))


def kernel(x_prompt, x_sample, c_prompt, c_sample, state_lru_conv, state_lru_h, state_ssd_conv, state_ssd, w_ada, b_ada, w_in, lru_conv_w, lru_conv_b, lru_wa, lru_ba, lru_wx, lru_bx, lru_lambda, lru_norm_g, ssd_conv_w, ssd_conv_b, ssd_dt_bias, ssd_a_log, ssd_d, ssd_norm_g, w_out, ln1_g, ln1_b, w_rg, b_rg, w_re, b_re, w_gate, w_up, w_down, ln2_g, ln2_b):
    assert w_ada.shape[0] == DEPTH == 1
    nbp, seq, _ = x_prompt.shape
    nbs = x_sample.shape[0]
    t_p = nbp * seq
    t_all = t_p + nbs
    assert x_sample.shape[1] == 1 and t_p % ROUTE_TILE == 0 and nbs % TOK_TILE == 0

    c_rows = -(-(nbp + nbs) // 16) * 16
    c_all = jnp.pad(jnp.concatenate([c_prompt, c_sample], axis=0), ((0, c_rows - nbp - nbs), (0, 0)))
    w_in0 = w_in[0]
    o_z, o_xbc, o_dt = 2 * LRU_W, 2 * LRU_W + SSD_INNER, 2 * LRU_W + SSD_INNER + SSD_CONV_DIM
    w_dt = w_in0[:, o_dt:]
    ws_p = [w_in0[:, :o_z].astype(BF16), w_in0[:, o_z:o_xbc].astype(BF16), w_in0[:, o_xbc:o_dt].astype(BF16),
            jnp.pad(w_dt, ((0, 0), (0, LANES - SSD_HEADS))).astype(BF16)]
    ws_s = ws_p[:3] + [jnp.repeat(w_dt, SSD_P, axis=1).astype(BF16)]
    lp = dict(cw=lru_conv_w[0], cb=_row(lru_conv_b[0]), wa=_block_diag(lru_wa[0]), wx=_block_diag(lru_wx[0]),
              ba=_row(lru_ba[0]), bx=_row(lru_bx[0]), lam=_row(lru_lambda[0]), ng=_row(lru_norm_g[0]))
    sp = dict(cw=ssd_conv_w[0], cb=_row(ssd_conv_b[0]), dtb=_pad_lanes(ssd_dt_bias[0]), alog=_pad_lanes(ssd_a_log[0]),
              dexp=_row(jnp.repeat(ssd_d[0], SSD_P)), ng=_row(ssd_norm_g[0]),
              dtbe=_row(jnp.repeat(ssd_dt_bias[0], SSD_P)), aloge=_row(jnp.repeat(ssd_a_log[0], SSD_P)))
    wrt = jnp.zeros((ROUTE_ROWS, D), F32).at[0:N_GROUPS].set(w_rg[0].T).at[GROUP_SIZE:].set(w_re[0].T)
    brc = jnp.zeros((ROUTE_ROWS, 1), F32).at[0:N_GROUPS, 0].set(b_rg[0]).at[GROUP_SIZE:, 0].set(b_re[0])
    pp = dict(wo=w_out[0].astype(BF16), l1g=_row(ln1_g[0]), l1b=_row(ln1_b[0]), wrt=wrt, brc=brc)

    mod = _ada_call(c_all, w_ada[0], _row(b_ada[0])).reshape(c_rows, N_MOD, D)
    mp = [mod[:nbp, k].reshape(nbp, 1, D) for k in range(N_MOD)]
    ms = [mod[nbp:nbp + nbs, k].reshape(1, nbs, D) for k in range(N_MOD)]

    xg, z, xbc, dtr = _proj_call(x_prompt, mp[0], mp[1], ws_p, 256)
    yl, p_lru_conv, p_lru_h = _lru_call(xg, lp, 256)
    ys, p_ssd_conv, p_ssd = _ssd_call(xbc, z, dtr, sp)
    x1_p, v_p, eid_p, gw_p = _post_call(yl, ys, x_prompt, mp[2], mp[3], mp[4], pp, 256)

    xs3 = x_sample.reshape(1, nbs, D)
    xg_s, z_s, xbc_s, dte_s = _proj_call(xs3, ms[0], ms[1], ws_s, nbs)
    xg_s, z_s, xbc_s, dte_s = xg_s[0], z_s[0], xbc_s[0], dte_s[0]
    yl_s, h_s, xs_s, bm_s, cm_s, xdtt, dect = _srow_call(
        xg_s, xbc_s, dte_s, state_lru_conv[0].reshape(nbs, -1), state_lru_h[0],
        state_ssd_conv[0].reshape(nbs, -1), lp, sp)
    sb = SUBLANES
    regroup = lambda a: a.reshape(SSD_INNER, nbs // sb, sb).transpose(1, 0, 2)
    s_new, ys_s = _sstate_call(state_ssd[0].reshape(nbs, SSD_INNER, SSD_N), regroup(xdtt), regroup(dect),
                               bm_s, cm_s, xs_s, z_s, sp["dexp"], sp["ng"], sb)
    x1_s, v_s, eid_s, gw_s = _post_call(yl_s[None], ys_s[None], xs3, ms[2], ms[3], ms[4], pp, nbs)

    s_pad = -nbs % ROUTE_TILE
    dest, counts = _route_call(eid_p, jnp.pad(eid_s, ((0, 0), (0, s_pad)), constant_values=-1))
    counts = counts[:N_EXPERTS, 0].astype(I32)
    n_blocks = -(-(2 * t_all) // MOE_BM) + N_EXPERTS
    pend = jnp.cumsum(((counts + MOE_BM - 1) // MOE_BM) * MOE_BM)
    blk_start = jnp.arange(n_blocks, dtype=I32) * MOE_BM
    blk_e = jnp.minimum(jnp.sum((pend[None, :] <= blk_start[:, None]).astype(I32), axis=1), N_EXPERTS - 1)
    nblk = (pend[-1:] // MOE_BM).astype(I32)
    xpad = _dispatch_call(dest, v_p, v_s, jnp.zeros((n_blocks * MOE_BM,) + ROW_TILE, F32))
    ypad = _expert_call(blk_e, nblk, xpad, w_gate[0], w_up[0], w_down[0])
    l2g, l2b = _row(ln2_g[0]), _row(ln2_b[0])
    y_p = _combine_call(dest, gw_p[0].reshape(t_p, 1), gw_p[1].reshape(t_p, 1), x1_p, mp[5], l2g, l2b, ypad, 0)
    y_s = _combine_call(dest, gw_s[0].reshape(nbs, 1), gw_s[1].reshape(nbs, 1), x1_s, ms[5], l2g, l2b, ypad, t_p)

    s_lru_conv = jnp.concatenate([state_lru_conv[0][:, 1:], xg_s[:, None, :LRU_W]], axis=1)
    s_ssd_conv = jnp.concatenate([state_ssd_conv[0][:, 1:], xbc_s[:, None, :]], axis=1)
    return (y_p, y_s.reshape(nbs, 1, D),
            p_lru_conv[None], p_lru_h.reshape(1, nbp, LRU_W), p_ssd_conv[None],
            p_ssd.reshape(1, nbp, SSD_HEADS, SSD_P, SSD_N),
            s_lru_conv[None], h_s[None], s_ssd_conv[None],
            s_new.reshape(1, nbs, SSD_HEADS, SSD_P, SSD_N))
```

```python
import functools
import math

import jax
import jax.numpy as jnp
from jax import lax
from jax.experimental import pallas as pl
from jax.experimental.pallas import tpu as pltpu

F32 = jnp.float32
BF16 = jnp.bfloat16
I32 = jnp.int32
HIGHEST = lax.Precision.HIGHEST

D = 1024
DEPTH = 1
CONV_W = 4
LRU_W = D
LRU_HEADS = 16
LRU_C = 8.0
LRU_PACK = 4
LRU_PACK_W = LRU_PACK * (LRU_W // LRU_HEADS)
SSD_INNER = D
SSD_HEADS = 16
SSD_P = SSD_INNER // SSD_HEADS
SSD_GROUPS = 2
SSD_N = 128
SSD_Q = 128
SSD_CONV_DIM = SSD_INNER + 2 * SSD_GROUPS * SSD_N
N_GROUPS = 4
GROUP_SIZE = 8
N_EXPERTS = N_GROUPS * GROUP_SIZE
D_FF = D // 2
N_MOD = 6
LN_EPS = 1e-5
RMS_EPS = 1e-6
ALPHA = (2.0 * DEPTH) ** 0.25

LANES = 128
SUBLANES = 8
VMEM_LIMIT = 56 * 1024 * 1024
TOK_TILE = 128
ROUTE_TILE = 512
DMA_UNROLL = 8
MOE_BM = 256
MOE_BM_SHIFT = 8
ROUTE_ROWS = 40

NT_DIMS = (((1,), (1,)), ((), ()))


def _tile_rows(rows):
    return (rows * SUBLANES, LANES)


def _cparams(sem):
    return pltpu.CompilerParams(dimension_semantics=sem, vmem_limit_bytes=VMEM_LIMIT)


def _sigmoid(x):
    return 0.5 * (jnp.tanh(0.5 * x) + 1.0)


def _silu(x):
    return x * _sigmoid(x)


def _softplus(x):
    return jnp.maximum(x, 0.0) + jnp.log1p(jnp.exp(-jnp.abs(x)))


def _gelu_tanh(x):
    return 0.5 * x * (1.0 + jnp.tanh(math.sqrt(2.0 / math.pi) * (x + 0.044715 * (x * x * x))))


def _layer_norm(x):
    mu = jnp.mean(x, axis=-1, keepdims=True)
    xc = x - mu
    var = jnp.mean(xc * xc, axis=-1, keepdims=True)
    return xc * lax.rsqrt(var + LN_EPS)


def _rms_norm(x, g):
    return x * lax.rsqrt(jnp.mean(x * x, axis=-1, keepdims=True) + RMS_EPS) * g


def _dot(a, b, **kw):
    return jnp.dot(a, b, preferred_element_type=F32, **kw)


def _dot_nt(a, b, **kw):
    return lax.dot_general(a, b, NT_DIMS, preferred_element_type=F32, **kw)


def _store_row_tiles(ref, val):
    rows = val.shape[0]
    for j in range(SUBLANES):
        ref[pl.ds(j, rows, stride=SUBLANES), :] = val[:, j * LANES:(j + 1) * LANES]


def _load_row_tiles(ref):
    rows = ref.shape[0] // SUBLANES
    return jnp.concatenate([ref[pl.ds(j, rows, stride=SUBLANES), :] for j in range(SUBLANES)], axis=1)


def _lru_gates(xc, xb, wa, wx, ba, bx, sp):
    r = _sigmoid(_dot(xb, wa) + ba)
    i = _sigmoid(_dot(xb, wx) + bx)
    log_a = (-LRU_C) * r * sp
    a = jnp.exp(log_a)
    mult = jnp.sqrt(jnp.tanh(-log_a) * (a * a + 1.0))
    return a, mult * (i * xc)


def _ada_kernel(c_ref, w_ref, b_ref, o_ref):
    s = _silu(c_ref[...]).astype(BF16)
    o_ref[...] = _dot(s, w_ref[...].astype(BF16)) + b_ref[...]


def _ada_call(c_all, w_ada, b_ada):
    rows = c_all.shape[0]
    n = w_ada.shape[1]
    tn = 512
    return pl.pallas_call(
        _ada_kernel,
        out_shape=jax.ShapeDtypeStruct((rows, n), F32),
        grid=(n // tn,),
        in_specs=[pl.BlockSpec((rows, D), lambda j: (0, 0)),
                  pl.BlockSpec((D, tn), lambda j: (0, j)),
                  pl.BlockSpec((1, tn), lambda j: (0, j))],
        out_specs=pl.BlockSpec((rows, tn), lambda j: (0, j)),
        compiler_params=_cparams(("arbitrary",)),
        name="ada",
    )(c_all, w_ada, b_ada)


def _proj_kernel(x_ref, sh_ref, sc_ref, w1_ref, w2_ref, w3_ref, w4_ref, o1_ref, o2_ref, o3_ref, o4_ref):
    u = _layer_norm(x_ref[...]) * (1.0 + sc_ref[...]) + sh_ref[...]
    ub = u.astype(BF16)
    o1_ref[...] = _dot(ub, w1_ref[...])
    o2_ref[...] = _dot(ub, w2_ref[...])
    o3_ref[...] = _dot(ub, w3_ref[...])
    o4_ref[...] = _dot(ub, w4_ref[...])


def _mod_spec(mod, tl):
    if mod.shape[1] == 1:
        return pl.BlockSpec((None, 1, D), lambda b, l: (b, 0, 0))
    return pl.BlockSpec((None, tl, D), lambda b, l: (b, l, 0))


def _proj_call(x3, sh3, sc3, ws, tl):
    nb, L, _ = x3.shape
    widths = [w.shape[1] for w in ws]
    row = lambda b, l: (b, l, 0)
    full = lambda b, l: (0, 0)
    return pl.pallas_call(
        _proj_kernel,
        out_shape=[jax.ShapeDtypeStruct((nb, L, n), F32) for n in widths],
        grid=(nb, L // tl),
        in_specs=[pl.BlockSpec((None, tl, D), row), _mod_spec(sh3, tl), _mod_spec(sc3, tl)]
                 + [pl.BlockSpec((D, n), full) for n in widths],
        out_specs=[pl.BlockSpec((None, tl, n), row) for n in widths],
        compiler_params=_cparams(("parallel", "arbitrary")),
        name="proj",
    )(x3, sh3, sc3, *ws)


def _scan_rows(a_scr, u_scr, h0, n_rows):
    sub = lax.broadcasted_iota(I32, (SUBLANES, D), 0)
    shifts = (1, 2, 4)
    masks = [sub >= d for d in shifts]

    def body(j, hp):
        r0 = pl.multiple_of(j * SUBLANES, SUBLANES)
        a = a_scr[pl.ds(r0, SUBLANES), :]
        u = u_scr[pl.ds(r0, SUBLANES), :]
        for d, m in zip(shifts, masks):
            ash = pltpu.roll(a, d, 0)
            ush = pltpu.roll(u, d, 0)
            u = jnp.where(m, u + a * ush, u)
            a = jnp.where(m, a * ash, a)
        h = u + a * hp
        u_scr[pl.ds(r0, SUBLANES), :] = h
        return h[SUBLANES - 1:SUBLANES, :]

    return lax.fori_loop(0, n_rows // SUBLANES, body, h0)


def _lru_kernel(x_ref, g_ref, cw_ref, cb_ref, wa_ref, wx_ref, ba_ref, bx_ref, lam_ref, ng_ref,
                y_ref, conv_ref, h_ref, xpad, xc_scr, a_scr, u_scr, hc_scr, *, tl):
    @pl.when(pl.program_id(1) == 0)
    def _():
        xpad[0:SUBLANES, :] = jnp.zeros((SUBLANES, D), F32)
        hc_scr[...] = jnp.zeros((1, D), F32)

    xpad[SUBLANES:SUBLANES + tl, :] = x_ref[...]
    off = SUBLANES - (CONV_W - 1)
    acc = xpad[off:off + tl, :] * cw_ref[0:1, :]
    for k in range(1, CONV_W):
        acc = acc + xpad[off + k:off + k + tl, :] * cw_ref[k:k + 1, :]
    xc_scr[...] = acc + cb_ref[...]
    conv_ref[...] = xpad[SUBLANES + tl - (CONV_W - 1):SUBLANES + tl, :]
    xpad[0:SUBLANES, :] = xpad[tl:tl + SUBLANES, :]

    sp = _softplus(-lam_ref[...])
    for j in range(LRU_W // LRU_PACK_W):
        cs = slice(j * LRU_PACK_W, (j + 1) * LRU_PACK_W)
        xc = xc_scr[:, cs]
        a, u = _lru_gates(xc, xc.astype(BF16), wa_ref[j], wx_ref[j], ba_ref[:, cs], bx_ref[:, cs], sp[:, cs])
        a_scr[:, cs] = a
        u_scr[:, cs] = u

    h_last = _scan_rows(a_scr, u_scr, hc_scr[...], tl)
    hc_scr[...] = h_last
    h_ref[...] = h_last
    y = u_scr[...] * _gelu_tanh(g_ref[...])
    y_ref[...] = _rms_norm(y, ng_ref[...]).astype(BF16)


def _lru_call(xg, p, tl):
    nb, L, _ = xg.shape
    vec = lambda b, l: (0, 0)
    blk = lambda b, l: (0, 0, 0)
    return pl.pallas_call(
        functools.partial(_lru_kernel, tl=tl),
        out_shape=[jax.ShapeDtypeStruct((nb, L, LRU_W), BF16),
                   jax.ShapeDtypeStruct((nb, CONV_W - 1, LRU_W), F32),
                   jax.ShapeDtypeStruct((nb, 1, LRU_W), F32)],
        grid=(nb, L // tl),
        in_specs=[pl.BlockSpec((None, tl, LRU_W), lambda b, l: (b, l, 0)),
                  pl.BlockSpec((None, tl, LRU_W), lambda b, l: (b, l, 1)),
                  pl.BlockSpec((CONV_W, LRU_W), vec), pl.BlockSpec((1, LRU_W), vec),
                  pl.BlockSpec((LRU_W // LRU_PACK_W, LRU_PACK_W, LRU_PACK_W), blk),
                  pl.BlockSpec((LRU_W // LRU_PACK_W, LRU_PACK_W, LRU_PACK_W), blk),
                  pl.BlockSpec((1, LRU_W), vec), pl.BlockSpec((1, LRU_W), vec),
                  pl.BlockSpec((1, LRU_W), vec), pl.BlockSpec((1, LRU_W), vec)],
        out_specs=[pl.BlockSpec((None, tl, LRU_W), lambda b, l: (b, l, 0)),
                   pl.BlockSpec((None, CONV_W - 1, LRU_W), lambda b, l: (b, 0, 0)),
                   pl.BlockSpec((None, 1, LRU_W), lambda b, l: (b, 0, 0))],
        scratch_shapes=[pltpu.VMEM((tl + SUBLANES, LRU_W), F32), pltpu.VMEM((tl, LRU_W), F32),
                        pltpu.VMEM((tl, LRU_W), F32), pltpu.VMEM((tl, LRU_W), F32),
                        pltpu.VMEM((1, LRU_W), F32)],
        compiler_params=_cparams(("parallel", "arbitrary")),
        name="lru",
    )(xg, xg, p["cw"], p["cb"], p["wa"], p["wx"], p["ba"], p["bx"], p["lam"], p["ng"])


def _ssd_kernel(xbc_ref, z_ref, dt_ref, cw_ref, cb_ref, dtb_ref, alog_ref, dexp_ref, ng_ref,
                y_ref, conv_ref, st_ref, xpad, xc_scr, st_scr, y_scr, ea_scr, ew_scr):
    q = SSD_Q

    @pl.when(pl.program_id(1) == 0)
    def _():
        xpad[0:SUBLANES, :] = jnp.zeros((SUBLANES, SSD_CONV_DIM), F32)
        st_scr[...] = jnp.zeros((SSD_N, SSD_INNER), F32)

    xpad[SUBLANES:SUBLANES + q, :] = xbc_ref[...]
    off = SUBLANES - (CONV_W - 1)
    acc = xpad[off:off + q, :] * cw_ref[0:1, :]
    for k in range(1, CONV_W):
        acc = acc + xpad[off + k:off + k + q, :] * cw_ref[k:k + 1, :]
    xc_scr[...] = _silu(acc + cb_ref[...])
    conv_ref[...] = xpad[SUBLANES + q - (CONV_W - 1):SUBLANES + q, :]
    xpad[0:SUBLANES, :] = xpad[q:q + SUBLANES, :]

    dt = _softplus(dt_ref[...] + dtb_ref[...])
    da = dt * (-jnp.exp(alog_ref[...]))
    ri = lax.broadcasted_iota(I32, (q, q), 0)
    ci = lax.broadcasted_iota(I32, (q, q), 1)
    causal = ri >= ci
    acs = _dot(causal.astype(F32), da, precision=HIGHEST)
    acs_t = acs.T
    dt_t = dt.T
    w_end = dt * jnp.exp(acs[q - 1:q, :] - acs)

    half = SSD_INNER // SSD_GROUPS
    hpg = SSD_HEADS // SSD_GROUPS
    cgb, cbs = [], []
    for g in range(SSD_GROUPS):
        bgb = xc_scr[:, SSD_INNER + g * SSD_N:SSD_INNER + (g + 1) * SSD_N].astype(BF16)
        cgb.append(xc_scr[:, SSD_INNER + (SSD_GROUPS + g) * SSD_N:SSD_INNER + (SSD_GROUPS + g + 1) * SSD_N].astype(BF16))
        cbs.append(_dot_nt(cgb[g], bgb))

    low = lax.broadcasted_iota(I32, (q, LANES), 1) < SSD_P
    for k in range(SSD_HEADS // 2):
        cs = slice(k * LANES, (k + 1) * LANES)
        cb = cbs[(2 * k) // hpg]
        m, colb, wb = [], [], []
        for h in (2 * k, 2 * k + 1):
            colb.append(jnp.broadcast_to(acs[:, h:h + 1], (q, LANES)))
            wb.append(jnp.broadcast_to(w_end[:, h:h + 1], (q, LANES)))
            seg = colb[-1] - acs_t[h:h + 1, :]
            decay = jnp.where(causal, jnp.exp(jnp.where(causal, seg, 0.0)), 0.0)
            m.append((cb * (decay * dt_t[h:h + 1, :])).astype(BF16))
        ea_scr[:, cs] = jnp.where(low, colb[0], colb[1])
        ew_scr[:, cs] = jnp.where(low, wb[0], wb[1])
        xk = xc_scr[:, cs]
        zero = jnp.zeros_like(xk)
        rhs = jnp.concatenate([jnp.where(low, xk, zero), jnp.where(low, zero, xk)], axis=0).astype(BF16)
        y_scr[:, cs] = _dot(jnp.concatenate(m, axis=1), rhs)

    x = xc_scr[:, 0:SSD_INNER]
    ea = ea_scr[...]
    w = (x * ew_scr[...]).astype(BF16)
    s_in = st_scr[...]
    sb = s_in.astype(BF16)
    y_off, c_state = [], []
    for g in range(SSD_GROUPS):
        gs = slice(g * half, (g + 1) * half)
        bgt = xc_scr[:, SSD_INNER + g * SSD_N:SSD_INNER + (g + 1) * SSD_N].T.astype(BF16)
        y_off.append(_dot(cgb[g], sb[:, gs]))
        c_state.append(_dot(bgt, w[:, gs]))
    st_scr[...] = jnp.exp(ea[q - 1:q, :]) * s_in + jnp.concatenate(c_state, axis=1)
    y = y_scr[...] + jnp.concatenate(y_off, axis=1) * jnp.exp(ea) + dexp_ref[...] * x

    yz = y * _silu(z_ref[...])
    y_ref[...] = _rms_norm(yz, ng_ref[...]).astype(BF16)

    @pl.when(pl.program_id(1) == pl.num_programs(1) - 1)
    def _():
        st_ref[...] = st_scr[...].T


def _ssd_call(xbc, z, dtr, p):
    nb, L, _ = xbc.shape
    q = SSD_Q
    vec = lambda b, c: (0, 0)
    row = lambda b, c: (b, c, 0)
    return pl.pallas_call(
        _ssd_kernel,
        out_shape=[jax.ShapeDtypeStruct((nb, L, SSD_INNER), BF16),
                   jax.ShapeDtypeStruct((nb, CONV_W - 1, SSD_CONV_DIM), F32),
                   jax.ShapeDtypeStruct((nb, SSD_INNER, SSD_N), F32)],
        grid=(nb, L // q),
        in_specs=[pl.BlockSpec((None, q, SSD_CONV_DIM), row), pl.BlockSpec((None, q, SSD_INNER), row),
                  pl.BlockSpec((None, q, LANES), row),
                  pl.BlockSpec((CONV_W, SSD_CONV_DIM), vec), pl.BlockSpec((1, SSD_CONV_DIM), vec),
                  pl.BlockSpec((1, LANES), vec), pl.BlockSpec((1, LANES), vec),
                  pl.BlockSpec((1, SSD_INNER), vec), pl.BlockSpec((1, SSD_INNER), vec)],
        out_specs=[pl.BlockSpec((None, q, SSD_INNER), row),
                   pl.BlockSpec((None, CONV_W - 1, SSD_CONV_DIM), lambda b, c: (b, 0, 0)),
                   pl.BlockSpec((None, SSD_INNER, SSD_N), lambda b, c: (b, 0, 0))],
        scratch_shapes=[pltpu.VMEM((q + SUBLANES, SSD_CONV_DIM), F32), pltpu.VMEM((q, SSD_CONV_DIM), F32),
                        pltpu.VMEM((SSD_N, SSD_INNER), F32), pltpu.VMEM((q, SSD_INNER), F32),
                        pltpu.VMEM((q, SSD_INNER), F32), pltpu.VMEM((q, SSD_INNER), F32)],
        compiler_params=_cparams(("parallel", "arbitrary")),
        name="ssd",
    )(xbc, z, dtr, p["cw"], p["cb"], p["dtb"], p["alog"], p["dexp"], p["ng"])


def _srow_kernel(xg_ref, xbc_ref, dte_ref, lconv_ref, h0_ref, sconv_ref,
                 lcw_ref, lcb_ref, wa_ref, wx_ref, ba_ref, bx_ref, lam_ref, lng_ref,
                 scw_ref, scb_ref, dtbe_ref, aloge_ref,
                 yl_ref, h_ref, xs_ref, bm_ref, cm_ref, xdtt_ref, dect_ref, xc_scr):
    xl = xg_ref[:, 0:LRU_W]
    acc = lcb_ref[...] + xl * lcw_ref[CONV_W - 1:CONV_W, :]
    for k in range(CONV_W - 1):
        acc = acc + lconv_ref[:, k * LRU_W:(k + 1) * LRU_W] * lcw_ref[k:k + 1, :]
    xc_scr[...] = acc
    sp = _softplus(-lam_ref[...])
    for j in range(LRU_W // LRU_PACK_W):
        cs = slice(j * LRU_PACK_W, (j + 1) * LRU_PACK_W)
        xc = xc_scr[:, cs]
        a, u = _lru_gates(xc, xc.astype(BF16), wa_ref[j], wx_ref[j], ba_ref[:, cs], bx_ref[:, cs], sp[:, cs])
        h_ref[:, cs] = a * h0_ref[:, cs] + u
    y = h_ref[...] * _gelu_tanh(xg_ref[:, LRU_W:2 * LRU_W])
    yl_ref[...] = _rms_norm(y, lng_ref[...]).astype(BF16)

    acc = scb_ref[...] + xbc_ref[...] * scw_ref[CONV_W - 1:CONV_W, :]
    for k in range(CONV_W - 1):
        acc = acc + sconv_ref[:, k * SSD_CONV_DIM:(k + 1) * SSD_CONV_DIM] * scw_ref[k:k + 1, :]
    xc = _silu(acc)
    xs = xc[:, 0:SSD_INNER]
    xs_ref[...] = xs
    bm_ref[...] = xc[:, SSD_INNER:SSD_INNER + SSD_GROUPS * SSD_N]
    cm_ref[...] = xc[:, SSD_INNER + SSD_GROUPS * SSD_N:]
    dt = _softplus(dte_ref[...] + dtbe_ref[...])
    dec = jnp.exp(dt * (-jnp.exp(aloge_ref[...])))
    xdtt_ref[...] = (xs * dt).T
    dect_ref[...] = dec.T


def _srow_call(xg, xbc, dte, lconv, h0, sconv, lp, sp):
    n = xg.shape[0]
    args = (xg, xbc, dte, lconv, h0, sconv, lp["cw"], lp["cb"], lp["wa"], lp["wx"], lp["ba"], lp["bx"],
            lp["lam"], lp["ng"], sp["cw"], sp["cb"], sp["dtbe"], sp["aloge"])
    full = lambda a: pl.BlockSpec(a.shape, lambda i, nd=a.ndim: (0,) * nd)
    outs = [jax.ShapeDtypeStruct((n, LRU_W), BF16), jax.ShapeDtypeStruct((n, LRU_W), F32),
            jax.ShapeDtypeStruct((n, SSD_INNER), F32), jax.ShapeDtypeStruct((n, SSD_GROUPS * SSD_N), F32),
            jax.ShapeDtypeStruct((n, SSD_GROUPS * SSD_N), F32),
            jax.ShapeDtypeStruct((SSD_INNER, n), F32), jax.ShapeDtypeStruct((SSD_INNER, n), F32)]
    return pl.pallas_call(
        _srow_kernel,
        out_shape=outs,
        grid=(1,),
        in_specs=[full(a) for a in args],
        out_specs=[pl.BlockSpec(o.shape, lambda i: (0, 0)) for o in outs],
        scratch_shapes=[pltpu.VMEM((n, LRU_W), F32)],
        compiler_params=_cparams(("arbitrary",)),
        name="srow",
    )(*args)


def _sstate_kernel(s0_ref, xq_ref, dq_ref, bm_ref, cm_ref, xs_ref, z_ref, dexp_ref, ng_ref,
                   sn_ref, ys_ref, yraw):
    nb = s0_ref.shape[0]
    half = SSD_INNER // SSD_GROUPS
    for bi in range(nb):
        brow = jnp.concatenate(
            [jnp.broadcast_to(bm_ref[bi:bi + 1, g * SSD_N:(g + 1) * SSD_N], (half, SSD_N)) for g in range(SSD_GROUPS)],
            axis=0)
        s = dq_ref[:, bi:bi + 1] * s0_ref[bi] + xq_ref[:, bi:bi + 1] * brow
        sn_ref[bi] = s
        sb = s.astype(BF16)
        for g in range(SSD_GROUPS):
            cg = cm_ref[:, g * SSD_N:(g + 1) * SSD_N].astype(BF16)
            res = _dot_nt(cg, sb[g * half:(g + 1) * half, :])
            yraw[bi:bi + 1, g * half:(g + 1) * half] = res[bi:bi + 1, :]
    y = yraw[...] + dexp_ref[...] * xs_ref[...]
    ys_ref[...] = _rms_norm(y * _silu(z_ref[...]), ng_ref[...]).astype(BF16)


def _sstate_call(s0, xq, dq, bm, cm, xs, z, dexp, ng, nb):
    n = s0.shape[0]
    row = lambda i: (i, 0)
    vec = lambda i: (0, 0)
    gn = SSD_GROUPS * SSD_N
    return pl.pallas_call(
        _sstate_kernel,
        out_shape=[jax.ShapeDtypeStruct(s0.shape, F32), jax.ShapeDtypeStruct((n, SSD_INNER), BF16)],
        grid=(n // nb,),
        in_specs=[pl.BlockSpec((nb, SSD_INNER, SSD_N), lambda i: (i, 0, 0)),
                  pl.BlockSpec((None, SSD_INNER, nb), lambda i: (i, 0, 0)),
                  pl.BlockSpec((None, SSD_INNER, nb), lambda i: (i, 0, 0)),
                  pl.BlockSpec((nb, gn), row), pl.BlockSpec((nb, gn), row),
                  pl.BlockSpec((nb, SSD_INNER), row), pl.BlockSpec((nb, SSD_INNER), row),
                  pl.BlockSpec((1, SSD_INNER), vec), pl.BlockSpec((1, SSD_INNER), vec)],
        out_specs=[pl.BlockSpec((nb, SSD_INNER, SSD_N), lambda i: (i, 0, 0)), pl.BlockSpec((nb, SSD_INNER), row)],
        scratch_shapes=[pltpu.VMEM((nb, SSD_INNER), F32)],
        compiler_params=_cparams(("parallel",)),
        name="sstate",
    )(s0, xq, dq, bm, cm, xs, z, dexp, ng)


def _post_kernel(yl_ref, ys_ref, x_ref, g1_ref, sh2_ref, sc2_ref, wo_ref, l1g_ref, l1b_ref, wrt_ref, brc_ref,
                 x1_ref, v_ref, eid_ref, gw_ref):
    o = _dot(yl_ref[...], wo_ref[0:LRU_W, :]) + _dot(ys_ref[...], wo_ref[LRU_W:LRU_W + SSD_INNER, :])
    x1 = _layer_norm(ALPHA * x_ref[...] + g1_ref[...] * o) * l1g_ref[...] + l1b_ref[...]
    x1_ref[...] = x1
    v = _layer_norm(x1) * (1.0 + sc2_ref[...]) + sh2_ref[...]
    _store_row_tiles(v_ref, v)

    lt = _dot(v.astype(BF16), wrt_ref[...]).T + brc_ref[...]
    tl = lt.shape[1]
    row = lax.broadcasted_iota(I32, (GROUP_SIZE, tl), 0).astype(F32)
    big = float(GROUP_SIZE)
    neg = -jnp.inf
    lg = jnp.where(row < N_GROUPS, lt[0:GROUP_SIZE, :], neg)
    gmax = jnp.max(lg, axis=0, keepdims=True)
    gsel = jnp.min(jnp.where(lg == gmax, row, big), axis=0, keepdims=True)
    pg = 1.0 / jnp.sum(jnp.exp(lg - gmax), axis=0, keepdims=True)
    le = lt[GROUP_SIZE:2 * GROUP_SIZE, :]
    for j in range(1, N_GROUPS):
        le = jnp.where(gsel == j, lt[GROUP_SIZE * (j + 1):GROUP_SIZE * (j + 2), :], le)
    m1 = jnp.max(le, axis=0, keepdims=True)
    i1 = jnp.min(jnp.where(le == m1, row, big), axis=0, keepdims=True)
    rest = jnp.where(row == i1, neg, le)
    m2 = jnp.max(rest, axis=0, keepdims=True)
    i2 = jnp.min(jnp.where(rest == m2, row, big), axis=0, keepdims=True)
    e2 = jnp.exp(m2 - m1)
    den = 1.0 + e2
    eid = jnp.where(row == 0, gsel * GROUP_SIZE + i1, jnp.where(row == 1, gsel * GROUP_SIZE + i2, 0.0))
    eid_ref[...] = eid.astype(I32)
    gw_ref[...] = jnp.where(row == 0, pg * (1.0 / den), jnp.where(row == 1, pg * (e2 / den), 0.0))


def _post_call(yl, ys, x3, g1, sh2, sc2, p, tl):
    nb, L, _ = x3.shape
    nl = L // tl
    row = lambda b, l: (b, l, 0)
    vec = lambda b, l: (0, 0)
    tok = lambda b, l: (b * nl + l, 0)
    tokt = lambda b, l: (0, b * nl + l)
    return pl.pallas_call(
        _post_kernel,
        out_shape=[jax.ShapeDtypeStruct((nb, L, D), F32), jax.ShapeDtypeStruct(_tile_rows(nb * L), F32),
                   jax.ShapeDtypeStruct((SUBLANES, nb * L), I32), jax.ShapeDtypeStruct((SUBLANES, nb * L), F32)],
        grid=(nb, nl),
        in_specs=[pl.BlockSpec((None, tl, LRU_W), row), pl.BlockSpec((None, tl, SSD_INNER), row),
                  pl.BlockSpec((None, tl, D), row), _mod_spec(g1, tl), _mod_spec(sh2, tl), _mod_spec(sc2, tl),
                  pl.BlockSpec((LRU_W + SSD_INNER, D), vec), pl.BlockSpec((1, D), vec), pl.BlockSpec((1, D), vec),
                  pl.BlockSpec((D, LANES), vec), pl.BlockSpec((LANES, 1), vec)],
        out_specs=[pl.BlockSpec((None, tl, D), row), pl.BlockSpec(_tile_rows(tl), tok),
                   pl.BlockSpec((SUBLANES, tl), tokt), pl.BlockSpec((SUBLANES, tl), tokt)],
        compiler_params=_cparams(("parallel", "arbitrary")),
        name="post",
    )(yl, ys, x3, g1, sh2, sc2, p["wo"], p["l1g"], p["l1b"], p["wrt"], p["brc"])


def _route_kernel(eidp_ref, eids_ref, dest_ref, cnt_ref, run, poff, *, p_tiles):
    ph = pl.program_id(0)
    t = pl.program_id(1)
    n = ROUTE_TILE
    eid = jnp.where(t < p_tiles, eidp_ref[...], eids_ref[...])
    rowi = lax.broadcasted_iota(I32, (LANES, n), 0)
    oh0 = rowi == eid[0:1, :]
    oh1 = rowi == eid[1:2, :]
    oh = oh0.astype(F32) + oh1.astype(F32)
    tile_cnt = jnp.sum(oh, axis=1, keepdims=True)

    @pl.when((ph == 0) & (t == 0))
    def _():
        run[...] = jnp.zeros((LANES, n), F32)

    @pl.when(ph == 0)
    def _():
        run[...] = run[...] + tile_cnt

    @pl.when((ph == 1) & (t == 0))
    def _():
        counts = run[...]
        cnt_ref[...] = counts
        nblk = (counts.astype(I32) + (MOE_BM - 1)) >> MOE_BM_SHIFT
        r = lax.broadcasted_iota(I32, (LANES, LANES), 0)
        c = lax.broadcasted_iota(I32, (LANES, LANES), 1)
        lower = (r > c).astype(BF16)
        poff[...] = _dot(lower, nblk.astype(F32).astype(BF16)) * float(MOE_BM)
        run[...] = jnp.zeros((LANES, n), F32)

    @pl.when(ph == 1)
    def _():
        r = lax.broadcasted_iota(I32, (n, n), 0)
        c = lax.broadcasted_iota(I32, (n, n), 1)
        before = (r < c).astype(BF16)
        slot = _dot(oh.astype(BF16), before) + run[...] + poff[...]
        d0 = jnp.sum(jnp.where(oh0, slot, 0.0), axis=0, keepdims=True)
        d1 = jnp.sum(jnp.where(oh1, slot, 0.0), axis=0, keepdims=True)
        row = lax.broadcasted_iota(I32, (SUBLANES, n), 0)
        dest_ref[...] = jnp.where(row == 0, d0, jnp.where(row == 1, d1, 0.0)).astype(I32)
        run[...] = run[...] + tile_cnt


def _route_call(eid_p, eid_s):
    n = ROUTE_TILE
    p_tiles = eid_p.shape[1] // n
    s_tiles = eid_s.shape[1] // n
    t_all = (p_tiles + s_tiles) * n
    return pl.pallas_call(
        functools.partial(_route_kernel, p_tiles=p_tiles),
        out_shape=[jax.ShapeDtypeStruct((SUBLANES, t_all), I32), jax.ShapeDtypeStruct((LANES, n), F32)],
        grid=(2, p_tiles + s_tiles),
        in_specs=[pl.BlockSpec((SUBLANES, n), lambda ph, t: (0, jnp.minimum(t, p_tiles - 1))),
                  pl.BlockSpec((SUBLANES, n), lambda ph, t: (0, jnp.maximum(t - p_tiles, 0)))],
        out_specs=[pl.BlockSpec((SUBLANES, n), lambda ph, t: (0, t * ph)),
                   pl.BlockSpec((LANES, n), lambda ph, t: (0, 0))],
        scratch_shapes=[pltpu.VMEM((LANES, n), F32), pltpu.VMEM((LANES, n), F32)],
        compiler_params=_cparams(("arbitrary", "arbitrary")),
        name="route",
    )(eid_p, eid_s)


def _row_copy(src, src_row, dst, dst_row, sem):
    s0 = pl.multiple_of(src_row * SUBLANES, SUBLANES)
    d0 = pl.multiple_of(dst_row * SUBLANES, SUBLANES)
    return pltpu.make_async_copy(src.at[pl.ds(s0, SUBLANES), :], dst.at[pl.ds(d0, SUBLANES), :], sem)


def _scatter_rows(dest_ref, v_ref, xpad_ref, sem):
    def start(r, carry):
        for k in range(2):
            _row_copy(v_ref, r, xpad_ref, dest_ref[k, r], sem).start(priority=k)
        return carry

    def wait(r, carry):
        for k in range(2):
            _row_copy(v_ref, r, xpad_ref, dest_ref[k, r], sem).wait()
        return carry

    lax.fori_loop(0, TOK_TILE, start, 0, unroll=DMA_UNROLL)
    lax.fori_loop(0, TOK_TILE, wait, 0, unroll=DMA_UNROLL)


def _dispatch_kernel(dest_ref, vp_ref, vs_ref, xpad_in, xpad_ref, sem, *, p_tiles):
    del xpad_in
    t = pl.program_id(0)

    @pl.when(t < p_tiles)
    def _():
        _scatter_rows(dest_ref, vp_ref, xpad_ref, sem)

    @pl.when(t >= p_tiles)
    def _():
        _scatter_rows(dest_ref, vs_ref, xpad_ref, sem)


def _dispatch_call(dest, v_p, v_s, xpad0):
    p_tiles = v_p.shape[0] // _tile_rows(TOK_TILE)[0]
    s_tiles = v_s.shape[0] // _tile_rows(TOK_TILE)[0]
    return pl.pallas_call(
        functools.partial(_dispatch_kernel, p_tiles=p_tiles),
        out_shape=jax.ShapeDtypeStruct(xpad0.shape, xpad0.dtype),
        grid=(p_tiles + s_tiles,),
        in_specs=[pl.BlockSpec((SUBLANES, TOK_TILE), lambda t: (0, t), memory_space=pltpu.SMEM),
                  pl.BlockSpec(_tile_rows(TOK_TILE), lambda t: (jnp.minimum(t, p_tiles - 1), 0)),
                  pl.BlockSpec(_tile_rows(TOK_TILE), lambda t: (jnp.maximum(t - p_tiles, 0), 0)),
                  pl.BlockSpec(memory_space=pl.ANY)],
        out_specs=pl.BlockSpec(memory_space=pl.ANY),
        scratch_shapes=[pltpu.SemaphoreType.DMA],
        input_output_aliases={3: 0},
        compiler_params=_cparams(("arbitrary",)),
        name="dispatch",
    )(dest, v_p, v_s, xpad0)


def _expert_kernel(be_ref, nb_ref, x_ref, wg_ref, wu_ref, wd_ref, o_ref, wgb, wub, wdb):
    i = pl.program_id(0)

    @pl.when(i < nb_ref[0])
    def _():
        @pl.when((i == 0) | (be_ref[i] != be_ref[jnp.maximum(i - 1, 0)]))
        def _():
            wgb[...] = wg_ref[...].astype(BF16)
            wub[...] = wu_ref[...].astype(BF16)
            wdb[...] = wd_ref[...].astype(BF16)

        x = _load_row_tiles(x_ref).astype(BF16)
        h = _silu(_dot(x, wgb[...])) * _dot(x, wub[...])
        _store_row_tiles(o_ref, _dot(h.astype(BF16), wdb[...]))

    @pl.when(i >= nb_ref[0])
    def _():
        o_ref[...] = jnp.zeros(o_ref.shape, F32)


def _expert_call(blk_e, nblk, xpad, w_gate, w_up, w_down):
    n_rows = xpad.shape[0] // SUBLANES
    blk = lambda i, be, nb: (jnp.minimum(i, nb[0] - 1), 0)
    wsel = lambda i, be, nb: (be[jnp.minimum(i, nb[0] - 1)], 0, 0)
    return pl.pallas_call(
        _expert_kernel,
        out_shape=jax.ShapeDtypeStruct(_tile_rows(n_rows), F32),
        grid_spec=pltpu.PrefetchScalarGridSpec(
            num_scalar_prefetch=2,
            grid=(n_rows // MOE_BM,),
            in_specs=[pl.BlockSpec(_tile_rows(MOE_BM), blk),
                      pl.BlockSpec((None, D, D_FF), wsel), pl.BlockSpec((None, D, D_FF), wsel),
                      pl.BlockSpec((None, D_FF, D), wsel)],
            out_specs=pl.BlockSpec(_tile_rows(MOE_BM), lambda i, be, nb: (i, 0)),
            scratch_shapes=[pltpu.VMEM((D, D_FF), BF16), pltpu.VMEM((D, D_FF), BF16), pltpu.VMEM((D_FF, D), BF16)]),
        compiler_params=_cparams(("arbitrary",)),
        name="expert",
    )(blk_e, nblk, xpad, w_gate, w_up, w_down)


def _combine_kernel(dest_ref, gw0_ref, gw1_ref, x1_ref, g2_ref, l2g_ref, l2b_ref, ypad_ref, y_ref, ybuf, sems,
                    *, tok_off, k_stride):
    i = pl.program_id(0)

    def gather(tile, slot, go):
        base = tok_off + tile * TOK_TILE

        def body(r, carry):
            for k in range(2):
                cp = _row_copy(ypad_ref, dest_ref[k * k_stride + base + r], ybuf.at[slot, k], r, sems.at[slot])
                if go:
                    cp.start(priority=k)
                else:
                    cp.wait()
            return carry

        lax.fori_loop(0, TOK_TILE, body, 0, unroll=DMA_UNROLL)

    @pl.when(i == 0)
    def _():
        gather(0, 0, True)

    @pl.when(i + 1 < pl.num_programs(0))
    def _():
        gather(i + 1, (i + 1) % 2, True)

    slot = i % 2
    gather(i, slot, False)
    f = _load_row_tiles(ybuf.at[slot, 0]) * gw0_ref[...] + _load_row_tiles(ybuf.at[slot, 1]) * gw1_ref[...]
    y_ref[...] = _layer_norm(ALPHA * x1_ref[...] + g2_ref[...] * f) * l2g_ref[...] + l2b_ref[...]


def _combine_call(dest, gw0, gw1, x1, g2, l2g, l2b, ypad, tok_off):
    nb, L, _ = x1.shape
    tl = TOK_TILE
    nl = L // tl
    row = lambda i, d: (i // nl, i % nl, 0)
    vec = lambda i, d: (0, 0)
    tok = lambda i, d: (i, 0)
    if g2.shape[1] == 1:
        g2_spec = pl.BlockSpec((None, 1, D), lambda i, d: (i // nl, 0, 0))
    else:
        g2_spec = pl.BlockSpec((None, tl, D), row)
    return pl.pallas_call(
        functools.partial(_combine_kernel, tok_off=tok_off, k_stride=dest.shape[1]),
        out_shape=jax.ShapeDtypeStruct((nb, L, D), F32),
        grid_spec=pltpu.PrefetchScalarGridSpec(
            num_scalar_prefetch=1,
            grid=(nb * nl,),
            in_specs=[pl.BlockSpec((tl, 1), tok), pl.BlockSpec((tl, 1), tok),
                      pl.BlockSpec((None, tl, D), row), g2_spec,
                      pl.BlockSpec((1, D), vec), pl.BlockSpec((1, D), vec),
                      pl.BlockSpec(memory_space=pl.ANY)],
            out_specs=pl.BlockSpec((None, tl, D), row),
            scratch_shapes=[pltpu.VMEM((2, 2) + _tile_rows(tl), F32), pltpu.SemaphoreType.DMA((2,))]),
        compiler_params=_cparams(("arbitrary",)),
        name="combine",
    )(dest[:2].reshape(-1), gw0, gw1, x1, g2, l2g, l2b, ypad)


def _block_diag(w):
    nh, blk, _ = w.shape
    w4 = w.reshape(nh // LRU_PACK, LRU_PACK, blk, blk)
    eye = jnp.eye(LRU_PACK, dtype=w.dtype)
    out = jnp.einsum("gaij,ab->gaibj", w4, eye)
    return out.reshape(nh // LRU_PACK, LRU_PACK * blk, LRU_PACK * blk).astype(BF16)


def _row(v):
    return v.reshape(1, -1).astype(F32)


def _pad_lanes(v):
    return jnp.pad(v.reshape(1, -1).astype(F32), ((0, 0), (0, LANES - v.shape[-1])))


def kernel(x_prompt, x_sample, c_prompt, c_sample, state_lru_conv, state_lru_h, state_ssd_conv, state_ssd, w_ada, b_ada, w_in, lru_conv_w, lru_conv_b, lru_wa, lru_ba, lru_wx, lru_bx, lru_lambda, lru_norm_g, ssd_conv_w, ssd_conv_b, ssd_dt_bias, ssd_a_log, ssd_d, ssd_norm_g, w_out, ln1_g, ln1_b, w_rg, b_rg, w_re, b_re, w_gate, w_up, w_down, ln2_g, ln2_b):
    assert w_ada.shape[0] == DEPTH == 1
    nbp, seq, _ = x_prompt.shape
    nbs = x_sample.shape[0]
    t_p = nbp * seq
    t_all = t_p + nbs
    assert x_sample.shape[1] == 1 and t_p % ROUTE_TILE == 0 and nbs % TOK_TILE == 0

    c_rows = -(-(nbp + nbs) // 16) * 16
    c_all = jnp.pad(jnp.concatenate([c_prompt, c_sample], axis=0), ((0, c_rows - nbp - nbs), (0, 0)))
    w_in0 = w_in[0]
    o_z, o_xbc, o_dt = 2 * LRU_W, 2 * LRU_W + SSD_INNER, 2 * LRU_W + SSD_INNER + SSD_CONV_DIM
    w_dt = w_in0[:, o_dt:]
    ws_p = [w_in0[:, :o_z].astype(BF16), w_in0[:, o_z:o_xbc].astype(BF16), w_in0[:, o_xbc:o_dt].astype(BF16),
            jnp.pad(w_dt, ((0, 0), (0, LANES - SSD_HEADS))).astype(BF16)]
    ws_s = ws_p[:3] + [jnp.repeat(w_dt, SSD_P, axis=1).astype(BF16)]
    lp = dict(cw=lru_conv_w[0], cb=_row(lru_conv_b[0]), wa=_block_diag(lru_wa[0]), wx=_block_diag(lru_wx[0]),
              ba=_row(lru_ba[0]), bx=_row(lru_bx[0]), lam=_row(lru_lambda[0]), ng=_row(lru_norm_g[0]))
    sp = dict(cw=ssd_conv_w[0], cb=_row(ssd_conv_b[0]), dtb=_pad_lanes(ssd_dt_bias[0]), alog=_pad_lanes(ssd_a_log[0]),
              dexp=_row(jnp.repeat(ssd_d[0], SSD_P)), ng=_row(ssd_norm_g[0]),
              dtbe=_row(jnp.repeat(ssd_dt_bias[0], SSD_P)), aloge=_row(jnp.repeat(ssd_a_log[0], SSD_P)))
    wrt = jnp.zeros((D, LANES), F32).at[:, 0:N_GROUPS].set(w_rg[0]).at[:, GROUP_SIZE:ROUTE_ROWS].set(w_re[0]).astype(BF16)
    brc = jnp.zeros((LANES, 1), F32).at[0:N_GROUPS, 0].set(b_rg[0]).at[GROUP_SIZE:ROUTE_ROWS, 0].set(b_re[0])
    pp = dict(wo=w_out[0].astype(BF16), l1g=_row(ln1_g[0]), l1b=_row(ln1_b[0]), wrt=wrt, brc=brc)

    mod = _ada_call(c_all, w_ada[0], _row(b_ada[0])).reshape(c_rows, N_MOD, D)
    mp = [mod[:nbp, k].reshape(nbp, 1, D) for k in range(N_MOD)]
    ms = [mod[nbp:nbp + nbs, k].reshape(1, nbs, D) for k in range(N_MOD)]

    xg, z, xbc, dtr = _proj_call(x_prompt, mp[0], mp[1], ws_p, 256)
    yl, p_lru_conv, p_lru_h = _lru_call(xg, lp, 256)
    ys, p_ssd_conv, p_ssd = _ssd_call(xbc, z, dtr, sp)
    x1_p, v_p, eid_p, gw_p = _post_call(yl, ys, x_prompt, mp[2], mp[3], mp[4], pp, 256)

    xs3 = x_sample.reshape(1, nbs, D)
    xg_s, z_s, xbc_s, dte_s = _proj_call(xs3, ms[0], ms[1], ws_s, nbs)
    xg_s, z_s, xbc_s, dte_s = xg_s[0], z_s[0], xbc_s[0], dte_s[0]
    yl_s, h_s, xs_s, bm_s, cm_s, xdtt, dect = _srow_call(
        xg_s, xbc_s, dte_s, state_lru_conv[0].reshape(nbs, -1), state_lru_h[0],
        state_ssd_conv[0].reshape(nbs, -1), lp, sp)
    sb = SUBLANES
    regroup = lambda a: a.reshape(SSD_INNER, nbs // sb, sb).transpose(1, 0, 2)
    s_new, ys_s = _sstate_call(state_ssd[0].reshape(nbs, SSD_INNER, SSD_N), regroup(xdtt), regroup(dect),
                               bm_s, cm_s, xs_s, z_s, sp["dexp"], sp["ng"], sb)
    x1_s, v_s, eid_s, gw_s = _post_call(yl_s[None], ys_s[None], xs3, ms[2], ms[3], ms[4], pp, nbs)

    s_pad = -nbs % ROUTE_TILE
    dest, counts = _route_call(eid_p, jnp.pad(eid_s, ((0, 0), (0, s_pad)), constant_values=-1))
    counts = counts[:N_EXPERTS, 0].astype(I32)
    n_blocks = -(-(2 * t_all) // MOE_BM) + N_EXPERTS
    pend = jnp.cumsum(((counts + MOE_BM - 1) // MOE_BM) * MOE_BM)
    blk_start = jnp.arange(n_blocks, dtype=I32) * MOE_BM
    blk_e = jnp.minimum(jnp.sum((pend[None, :] <= blk_start[:, None]).astype(I32), axis=1), N_EXPERTS - 1)
    nblk = (pend[-1:] // MOE_BM).astype(I32)
    xpad = _dispatch_call(dest, v_p, v_s, jnp.zeros(_tile_rows(n_blocks * MOE_BM), F32))
    ypad = _expert_call(blk_e, nblk, xpad, w_gate[0], w_up[0], w_down[0])
    l2g, l2b = _row(ln2_g[0]), _row(ln2_b[0])
    y_p = _combine_call(dest, gw_p[0].reshape(t_p, 1), gw_p[1].reshape(t_p, 1), x1_p, mp[5], l2g, l2b, ypad, 0)
    y_s = _combine_call(dest, gw_s[0].reshape(nbs, 1), gw_s[1].reshape(nbs, 1), x1_s, ms[5], l2g, l2b, ypad, t_p)

    s_lru_conv = jnp.concatenate([state_lru_conv[0][:, 1:], xg_s[:, None, :LRU_W]], axis=1)
    s_ssd_conv = jnp.concatenate([state_ssd_conv[0][:, 1:], xbc_s[:, None, :]], axis=1)
    return (y_p, y_s.reshape(nbs, 1, D),
            p_lru_conv[None], p_lru_h.reshape(1, nbp, LRU_W), p_ssd_conv[None],
            p_ssd.reshape(1, nbp, SSD_HEADS, SSD_P, SSD_N),
            s_lru_conv[None], h_s[None], s_ssd_conv[None],
            s_new.reshape(1, nbs, SSD_HEADS, SSD_P, SSD_N))
```

```python
import functools
import math

import jax
import jax.numpy as jnp
from jax import lax
from jax.experimental import pallas as pl
from jax.experimental.pallas import tpu as pltpu

F32 = jnp.float32
BF16 = jnp.bfloat16
I32 = jnp.int32
HIGHEST = lax.Precision.HIGHEST

D = 1024
DEPTH = 1
CONV_W = 4
LRU_W = D
LRU_HEADS = 16
LRU_C = 8.0
LRU_PACK = 4
LRU_PACK_W = LRU_PACK * (LRU_W // LRU_HEADS)
SSD_INNER = D
SSD_HEADS = 16
SSD_P = SSD_INNER // SSD_HEADS
SSD_GROUPS = 2
SSD_N = 128
SSD_Q = 128
SSD_CONV_DIM = SSD_INNER + 2 * SSD_GROUPS * SSD_N
N_GROUPS = 4
GROUP_SIZE = 8
N_EXPERTS = N_GROUPS * GROUP_SIZE
D_FF = D // 2
N_MOD = 6
LN_EPS = 1e-5
RMS_EPS = 1e-6
ALPHA = (2.0 * DEPTH) ** 0.25

LANES = 128
SUBLANES = 8
VMEM_LIMIT = 56 * 1024 * 1024
SEQ_TILE = 512
TOK_TILE = 128
ROUTE_TILE = 512
DMA_UNROLL = 8
MOE_BM = 384
ROUTE_ROWS = 40

NT_DIMS = (((1,), (1,)), ((), ()))


def _tile_rows(rows):
    return (rows * SUBLANES, LANES)


def _cparams(sem):
    return pltpu.CompilerParams(dimension_semantics=sem, vmem_limit_bytes=VMEM_LIMIT)


def _sigmoid(x):
    return 0.5 * (jnp.tanh(0.5 * x) + 1.0)


def _silu(x):
    return x * _sigmoid(x)


def _softplus(x):
    return jnp.maximum(x, 0.0) + jnp.log1p(jnp.exp(-jnp.abs(x)))


def _gelu_tanh(x):
    return 0.5 * x * (1.0 + jnp.tanh(math.sqrt(2.0 / math.pi) * (x + 0.044715 * (x * x * x))))


def _layer_norm(x):
    mu = jnp.mean(x, axis=-1, keepdims=True)
    xc = x - mu
    var = jnp.mean(xc * xc, axis=-1, keepdims=True)
    return xc * lax.rsqrt(var + LN_EPS)


def _rms_norm(x, g):
    return x * lax.rsqrt(jnp.mean(x * x, axis=-1, keepdims=True) + RMS_EPS) * g


def _dot(a, b, **kw):
    return jnp.dot(a, b, preferred_element_type=F32, **kw)


def _dot_nt(a, b, **kw):
    return lax.dot_general(a, b, NT_DIMS, preferred_element_type=F32, **kw)


def _store_row_tiles(ref, val):
    rows = val.shape[0]
    for j in range(SUBLANES):
        ref[pl.ds(j, rows, stride=SUBLANES), :] = val[:, j * LANES:(j + 1) * LANES]


def _load_row_tiles(ref):
    rows = ref.shape[0] // SUBLANES
    return jnp.concatenate([ref[pl.ds(j, rows, stride=SUBLANES), :] for j in range(SUBLANES)], axis=1)


def _lru_gates(xc, xb, wa, wx, ba, bx, sp):
    r = _sigmoid(_dot(xb, wa) + ba)
    i = _sigmoid(_dot(xb, wx) + bx)
    log_a = (-LRU_C) * r * sp
    a = jnp.exp(log_a)
    mult = jnp.sqrt(jnp.tanh(-log_a) * (a * a + 1.0))
    return a, mult * (i * xc)


def _ada_kernel(c_ref, w_ref, b_ref, o_ref):
    s = _silu(c_ref[...]).astype(BF16)
    o_ref[...] = _dot(s, w_ref[...].astype(BF16)) + b_ref[...]


def _ada_call(c_all, w_ada, b_ada):
    rows = c_all.shape[0]
    n = w_ada.shape[1]
    tn = 512
    return pl.pallas_call(
        _ada_kernel,
        out_shape=jax.ShapeDtypeStruct((rows, n), F32),
        grid=(n // tn,),
        in_specs=[pl.BlockSpec((rows, D), lambda j: (0, 0)),
                  pl.BlockSpec((D, tn), lambda j: (0, j)),
                  pl.BlockSpec((1, tn), lambda j: (0, j))],
        out_specs=pl.BlockSpec((rows, tn), lambda j: (0, j)),
        compiler_params=_cparams(("arbitrary",)),
        name="ada",
    )(c_all, w_ada, b_ada)


def _proj_kernel(x_ref, sh_ref, sc_ref, w1_ref, w2_ref, w3_ref, w4_ref, o1_ref, o2_ref, o3_ref, o4_ref):
    u = _layer_norm(x_ref[...]) * (1.0 + sc_ref[...]) + sh_ref[...]
    ub = u.astype(BF16)
    o1_ref[...] = _dot(ub, w1_ref[...])
    o2_ref[...] = _dot(ub, w2_ref[...])
    o3_ref[...] = _dot(ub, w3_ref[...])
    o4_ref[...] = _dot(ub, w4_ref[...])


def _mod_spec(mod, tl):
    if mod.shape[1] == 1:
        return pl.BlockSpec((None, 1, D), lambda b, l: (b, 0, 0))
    return pl.BlockSpec((None, tl, D), lambda b, l: (b, l, 0))


def _proj_call(x3, sh3, sc3, ws, tl):
    nb, L, _ = x3.shape
    widths = [w.shape[1] for w in ws]
    row = lambda b, l: (b, l, 0)
    full = lambda b, l: (0, 0)
    return pl.pallas_call(
        _proj_kernel,
        out_shape=[jax.ShapeDtypeStruct((nb, L, n), F32) for n in widths],
        grid=(nb, L // tl),
        in_specs=[pl.BlockSpec((None, tl, D), row), _mod_spec(sh3, tl), _mod_spec(sc3, tl)]
                 + [pl.BlockSpec((D, n), full) for n in widths],
        out_specs=[pl.BlockSpec((None, tl, n), row) for n in widths],
        compiler_params=_cparams(("parallel", "arbitrary")),
        name="proj",
    )(x3, sh3, sc3, *ws)


def _scan_rows(a_scr, u_scr, h0, n_rows):
    sub = lax.broadcasted_iota(I32, (SUBLANES, D), 0)
    shifts = (1, 2, 4)
    masks = [sub >= d for d in shifts]

    def body(j, hp):
        r0 = pl.multiple_of(j * SUBLANES, SUBLANES)
        a = a_scr[pl.ds(r0, SUBLANES), :]
        u = u_scr[pl.ds(r0, SUBLANES), :]
        for d, m in zip(shifts, masks):
            ash = pltpu.roll(a, d, 0)
            ush = pltpu.roll(u, d, 0)
            u = jnp.where(m, u + a * ush, u)
            a = jnp.where(m, a * ash, a)
        h = u + a * hp
        u_scr[pl.ds(r0, SUBLANES), :] = h
        return h[SUBLANES - 1:SUBLANES, :]

    return lax.fori_loop(0, n_rows // SUBLANES, body, h0)


def _lru_kernel(x_ref, g_ref, cw_ref, cb_ref, wa_ref, wx_ref, ba_ref, bx_ref, lam_ref, ng_ref,
                y_ref, conv_ref, h_ref, xpad, xc_scr, a_scr, u_scr, hc_scr, *, tl):
    @pl.when(pl.program_id(1) == 0)
    def _():
        xpad[0:SUBLANES, :] = jnp.zeros((SUBLANES, D), F32)
        hc_scr[...] = jnp.zeros((1, D), F32)

    xpad[SUBLANES:SUBLANES + tl, :] = x_ref[...]
    off = SUBLANES - (CONV_W - 1)
    acc = xpad[off:off + tl, :] * cw_ref[0:1, :]
    for k in range(1, CONV_W):
        acc = acc + xpad[off + k:off + k + tl, :] * cw_ref[k:k + 1, :]
    xc_scr[...] = acc + cb_ref[...]
    conv_ref[...] = xpad[SUBLANES + tl - (CONV_W - 1):SUBLANES + tl, :]
    xpad[0:SUBLANES, :] = xpad[tl:tl + SUBLANES, :]

    sp = _softplus(-lam_ref[...])
    for j in range(LRU_W // LRU_PACK_W):
        cs = slice(j * LRU_PACK_W, (j + 1) * LRU_PACK_W)
        xc = xc_scr[:, cs]
        a, u = _lru_gates(xc, xc.astype(BF16), wa_ref[j], wx_ref[j], ba_ref[:, cs], bx_ref[:, cs], sp[:, cs])
        a_scr[:, cs] = a
        u_scr[:, cs] = u

    h_last = _scan_rows(a_scr, u_scr, hc_scr[...], tl)
    hc_scr[...] = h_last
    h_ref[...] = h_last
    y = u_scr[...] * _gelu_tanh(g_ref[...])
    y_ref[...] = _rms_norm(y, ng_ref[...]).astype(BF16)


def _lru_call(xg, p, tl):
    nb, L, _ = xg.shape
    vec = lambda b, l: (0, 0)
    blk = lambda b, l: (0, 0, 0)
    return pl.pallas_call(
        functools.partial(_lru_kernel, tl=tl),
        out_shape=[jax.ShapeDtypeStruct((nb, L, LRU_W), BF16),
                   jax.ShapeDtypeStruct((nb, CONV_W - 1, LRU_W), F32),
                   jax.ShapeDtypeStruct((nb, 1, LRU_W), F32)],
        grid=(nb, L // tl),
        in_specs=[pl.BlockSpec((None, tl, LRU_W), lambda b, l: (b, l, 0)),
                  pl.BlockSpec((None, tl, LRU_W), lambda b, l: (b, l, 1)),
                  pl.BlockSpec((CONV_W, LRU_W), vec), pl.BlockSpec((1, LRU_W), vec),
                  pl.BlockSpec((LRU_W // LRU_PACK_W, LRU_PACK_W, LRU_PACK_W), blk),
                  pl.BlockSpec((LRU_W // LRU_PACK_W, LRU_PACK_W, LRU_PACK_W), blk),
                  pl.BlockSpec((1, LRU_W), vec), pl.BlockSpec((1, LRU_W), vec),
                  pl.BlockSpec((1, LRU_W), vec), pl.BlockSpec((1, LRU_W), vec)],
        out_specs=[pl.BlockSpec((None, tl, LRU_W), lambda b, l: (b, l, 0)),
                   pl.BlockSpec((None, CONV_W - 1, LRU_W), lambda b, l: (b, 0, 0)),
                   pl.BlockSpec((None, 1, LRU_W), lambda b, l: (b, 0, 0))],
        scratch_shapes=[pltpu.VMEM((tl + SUBLANES, LRU_W), F32), pltpu.VMEM((tl, LRU_W), F32),
                        pltpu.VMEM((tl, LRU_W), F32), pltpu.VMEM((tl, LRU_W), F32),
                        pltpu.VMEM((1, LRU_W), F32)],
        compiler_params=_cparams(("parallel", "arbitrary")),
        name="lru",
    )(xg, xg, p["cw"], p["cb"], p["wa"], p["wx"], p["ba"], p["bx"], p["lam"], p["ng"])


def _ssd_kernel(xbc_ref, z_ref, dt_ref, cw_ref, cb_ref, dtb_ref, alog_ref, dexp_ref, ng_ref,
                y_ref, conv_ref, st_ref, xpad, xc_scr, st_scr, y_scr, ea_scr, ew_scr):
    q = SSD_Q

    @pl.when(pl.program_id(1) == 0)
    def _():
        xpad[0:SUBLANES, :] = jnp.zeros((SUBLANES, SSD_CONV_DIM), F32)
        st_scr[...] = jnp.zeros((SSD_N, SSD_INNER), F32)

    xpad[SUBLANES:SUBLANES + q, :] = xbc_ref[...]
    off = SUBLANES - (CONV_W - 1)
    acc = xpad[off:off + q, :] * cw_ref[0:1, :]
    for k in range(1, CONV_W):
        acc = acc + xpad[off + k:off + k + q, :] * cw_ref[k:k + 1, :]
    xc_scr[...] = _silu(acc + cb_ref[...])
    conv_ref[...] = xpad[SUBLANES + q - (CONV_W - 1):SUBLANES + q, :]
    xpad[0:SUBLANES, :] = xpad[q:q + SUBLANES, :]

    dt = _softplus(dt_ref[...] + dtb_ref[...])
    da = dt * (-jnp.exp(alog_ref[...]))
    ri = lax.broadcasted_iota(I32, (q, q), 0)
    ci = lax.broadcasted_iota(I32, (q, q), 1)
    causal = ri >= ci
    acs = _dot(causal.astype(F32), da, precision=HIGHEST)
    acs_t = acs.T
    dt_t = dt.T
    w_end = dt * jnp.exp(acs[q - 1:q, :] - acs)

    half = SSD_INNER // SSD_GROUPS
    hpg = SSD_HEADS // SSD_GROUPS
    cgb, cbs = [], []
    for g in range(SSD_GROUPS):
        bgb = xc_scr[:, SSD_INNER + g * SSD_N:SSD_INNER + (g + 1) * SSD_N].astype(BF16)
        cgb.append(xc_scr[:, SSD_INNER + (SSD_GROUPS + g) * SSD_N:SSD_INNER + (SSD_GROUPS + g + 1) * SSD_N].astype(BF16))
        cbs.append(_dot_nt(cgb[g], bgb))

    low = lax.broadcasted_iota(I32, (q, LANES), 1) < SSD_P
    for k in range(SSD_HEADS // 2):
        cs = slice(k * LANES, (k + 1) * LANES)
        cb = cbs[(2 * k) // hpg]
        m, colb, wb = [], [], []
        for h in (2 * k, 2 * k + 1):
            colb.append(jnp.broadcast_to(acs[:, h:h + 1], (q, LANES)))
            wb.append(jnp.broadcast_to(w_end[:, h:h + 1], (q, LANES)))
            seg = colb[-1] - acs_t[h:h + 1, :]
            decay = jnp.where(causal, jnp.exp(jnp.where(causal, seg, 0.0)), 0.0)
            m.append((cb * (decay * dt_t[h:h + 1, :])).astype(BF16))
        ea_scr[:, cs] = jnp.where(low, colb[0], colb[1])
        ew_scr[:, cs] = jnp.where(low, wb[0], wb[1])
        xk = xc_scr[:, cs]
        zero = jnp.zeros_like(xk)
        rhs = jnp.concatenate([jnp.where(low, xk, zero), jnp.where(low, zero, xk)], axis=0).astype(BF16)
        y_scr[:, cs] = _dot(jnp.concatenate(m, axis=1), rhs)

    x = xc_scr[:, 0:SSD_INNER]
    ea = ea_scr[...]
    w = (x * ew_scr[...]).astype(BF16)
    s_in = st_scr[...]
    sb = s_in.astype(BF16)
    y_off, c_state = [], []
    for g in range(SSD_GROUPS):
        gs = slice(g * half, (g + 1) * half)
        bgt = xc_scr[:, SSD_INNER + g * SSD_N:SSD_INNER + (g + 1) * SSD_N].T.astype(BF16)
        y_off.append(_dot(cgb[g], sb[:, gs]))
        c_state.append(_dot(bgt, w[:, gs]))
    st_scr[...] = jnp.exp(ea[q - 1:q, :]) * s_in + jnp.concatenate(c_state, axis=1)
    y = y_scr[...] + jnp.concatenate(y_off, axis=1) * jnp.exp(ea) + dexp_ref[...] * x

    yz = y * _silu(z_ref[...])
    y_ref[...] = _rms_norm(yz, ng_ref[...]).astype(BF16)

    @pl.when(pl.program_id(1) == pl.num_programs(1) - 1)
    def _():
        st_ref[...] = st_scr[...].T


def _ssd_call(xbc, z, dtr, p):
    nb, L, _ = xbc.shape
    q = SSD_Q
    vec = lambda b, c: (0, 0)
    row = lambda b, c: (b, c, 0)
    return pl.pallas_call(
        _ssd_kernel,
        out_shape=[jax.ShapeDtypeStruct((nb, L, SSD_INNER), BF16),
                   jax.ShapeDtypeStruct((nb, CONV_W - 1, SSD_CONV_DIM), F32),
                   jax.ShapeDtypeStruct((nb, SSD_INNER, SSD_N), F32)],
        grid=(nb, L // q),
        in_specs=[pl.BlockSpec((None, q, SSD_CONV_DIM), row), pl.BlockSpec((None, q, SSD_INNER), row),
                  pl.BlockSpec((None, q, LANES), row),
                  pl.BlockSpec((CONV_W, SSD_CONV_DIM), vec), pl.BlockSpec((1, SSD_CONV_DIM), vec),
                  pl.BlockSpec((1, LANES), vec), pl.BlockSpec((1, LANES), vec),
                  pl.BlockSpec((1, SSD_INNER), vec), pl.BlockSpec((1, SSD_INNER), vec)],
        out_specs=[pl.BlockSpec((None, q, SSD_INNER), row),
                   pl.BlockSpec((None, CONV_W - 1, SSD_CONV_DIM), lambda b, c: (b, 0, 0)),
                   pl.BlockSpec((None, SSD_INNER, SSD_N), lambda b, c: (b, 0, 0))],
        scratch_shapes=[pltpu.VMEM((q + SUBLANES, SSD_CONV_DIM), F32), pltpu.VMEM((q, SSD_CONV_DIM), F32),
                        pltpu.VMEM((SSD_N, SSD_INNER), F32), pltpu.VMEM((q, SSD_INNER), F32),
                        pltpu.VMEM((q, SSD_INNER), F32), pltpu.VMEM((q, SSD_INNER), F32)],
        compiler_params=_cparams(("parallel", "arbitrary")),
        name="ssd",
    )(xbc, z, dtr, p["cw"], p["cb"], p["dtb"], p["alog"], p["dexp"], p["ng"])


def _srow_kernel(xg_ref, xbc_ref, dte_ref, lconv_ref, h0_ref, sconv_ref,
                 lcw_ref, lcb_ref, wa_ref, wx_ref, ba_ref, bx_ref, lam_ref, lng_ref,
                 scw_ref, scb_ref, dtbe_ref, aloge_ref,
                 yl_ref, h_ref, xs_ref, bm_ref, cm_ref, xdtt_ref, dect_ref, xc_scr):
    xl = xg_ref[:, 0:LRU_W]
    acc = lcb_ref[...] + xl * lcw_ref[CONV_W - 1:CONV_W, :]
    for k in range(CONV_W - 1):
        acc = acc + lconv_ref[:, k * LRU_W:(k + 1) * LRU_W] * lcw_ref[k:k + 1, :]
    xc_scr[...] = acc
    sp = _softplus(-lam_ref[...])
    for j in range(LRU_W // LRU_PACK_W):
        cs = slice(j * LRU_PACK_W, (j + 1) * LRU_PACK_W)
        xc = xc_scr[:, cs]
        a, u = _lru_gates(xc, xc.astype(BF16), wa_ref[j], wx_ref[j], ba_ref[:, cs], bx_ref[:, cs], sp[:, cs])
        h_ref[:, cs] = a * h0_ref[:, cs] + u
    y = h_ref[...] * _gelu_tanh(xg_ref[:, LRU_W:2 * LRU_W])
    yl_ref[...] = _rms_norm(y, lng_ref[...]).astype(BF16)

    acc = scb_ref[...] + xbc_ref[...] * scw_ref[CONV_W - 1:CONV_W, :]
    for k in range(CONV_W - 1):
        acc = acc + sconv_ref[:, k * SSD_CONV_DIM:(k + 1) * SSD_CONV_DIM] * scw_ref[k:k + 1, :]
    xc = _silu(acc)
    xs = xc[:, 0:SSD_INNER]
    xs_ref[...] = xs
    bm_ref[...] = xc[:, SSD_INNER:SSD_INNER + SSD_GROUPS * SSD_N]
    cm_ref[...] = xc[:, SSD_INNER + SSD_GROUPS * SSD_N:]
    dt = _softplus(dte_ref[...] + dtbe_ref[...])
    dec = jnp.exp(dt * (-jnp.exp(aloge_ref[...])))
    xdtt_ref[...] = (xs * dt).T
    dect_ref[...] = dec.T


def _srow_call(xg, xbc, dte, lconv, h0, sconv, lp, sp):
    n = xg.shape[0]
    args = (xg, xbc, dte, lconv, h0, sconv, lp["cw"], lp["cb"], lp["wa"], lp["wx"], lp["ba"], lp["bx"],
            lp["lam"], lp["ng"], sp["cw"], sp["cb"], sp["dtbe"], sp["aloge"])
    full = lambda a: pl.BlockSpec(a.shape, lambda i, nd=a.ndim: (0,) * nd)
    outs = [jax.ShapeDtypeStruct((n, LRU_W), BF16), jax.ShapeDtypeStruct((n, LRU_W), F32),
            jax.ShapeDtypeStruct((n, SSD_INNER), F32), jax.ShapeDtypeStruct((n, SSD_GROUPS * SSD_N), F32),
            jax.ShapeDtypeStruct((n, SSD_GROUPS * SSD_N), F32),
            jax.ShapeDtypeStruct((SSD_INNER, n), F32), jax.ShapeDtypeStruct((SSD_INNER, n), F32)]
    return pl.pallas_call(
        _srow_kernel,
        out_shape=outs,
        grid=(1,),
        in_specs=[full(a) for a in args],
        out_specs=[pl.BlockSpec(o.shape, lambda i: (0, 0)) for o in outs],
        scratch_shapes=[pltpu.VMEM((n, LRU_W), F32)],
        compiler_params=_cparams(("arbitrary",)),
        name="srow",
    )(*args)


def _sstate_kernel(s0_ref, xq_ref, dq_ref, bm_ref, cm_ref, xs_ref, z_ref, dexp_ref, ng_ref,
                   sn_ref, ys_ref, yraw):
    nb = s0_ref.shape[0]
    half = SSD_INNER // SSD_GROUPS
    for bi in range(nb):
        brow = jnp.concatenate(
            [jnp.broadcast_to(bm_ref[bi:bi + 1, g * SSD_N:(g + 1) * SSD_N], (half, SSD_N)) for g in range(SSD_GROUPS)],
            axis=0)
        s = dq_ref[:, bi:bi + 1] * s0_ref[bi] + xq_ref[:, bi:bi + 1] * brow
        sn_ref[bi] = s
        sb = s.astype(BF16)
        for g in range(SSD_GROUPS):
            cg = cm_ref[:, g * SSD_N:(g + 1) * SSD_N].astype(BF16)
            res = _dot_nt(cg, sb[g * half:(g + 1) * half, :])
            yraw[bi:bi + 1, g * half:(g + 1) * half] = res[bi:bi + 1, :]
    y = yraw[...] + dexp_ref[...] * xs_ref[...]
    ys_ref[...] = _rms_norm(y * _silu(z_ref[...]), ng_ref[...]).astype(BF16)


def _sstate_call(s0, xq, dq, bm, cm, xs, z, dexp, ng, nb):
    n = s0.shape[0]
    row = lambda i: (i, 0)
    vec = lambda i: (0, 0)
    gn = SSD_GROUPS * SSD_N
    return pl.pallas_call(
        _sstate_kernel,
        out_shape=[jax.ShapeDtypeStruct(s0.shape, F32), jax.ShapeDtypeStruct((n, SSD_INNER), BF16)],
        grid=(n // nb,),
        in_specs=[pl.BlockSpec((nb, SSD_INNER, SSD_N), lambda i: (i, 0, 0)),
                  pl.BlockSpec((None, SSD_INNER, nb), lambda i: (i, 0, 0)),
                  pl.BlockSpec((None, SSD_INNER, nb), lambda i: (i, 0, 0)),
                  pl.BlockSpec((nb, gn), row), pl.BlockSpec((nb, gn), row),
                  pl.BlockSpec((nb, SSD_INNER), row), pl.BlockSpec((nb, SSD_INNER), row),
                  pl.BlockSpec((1, SSD_INNER), vec), pl.BlockSpec((1, SSD_INNER), vec)],
        out_specs=[pl.BlockSpec((nb, SSD_INNER, SSD_N), lambda i: (i, 0, 0)), pl.BlockSpec((nb, SSD_INNER), row)],
        scratch_shapes=[pltpu.VMEM((nb, SSD_INNER), F32)],
        compiler_params=_cparams(("parallel",)),
        name="sstate",
    )(s0, xq, dq, bm, cm, xs, z, dexp, ng)


def _post_kernel(yl_ref, ys_ref, x_ref, g1_ref, sh2_ref, sc2_ref, wo_ref, l1g_ref, l1b_ref, wrt_ref, brc_ref,
                 x1_ref, v_ref, eid_ref, gw_ref):
    o = _dot(yl_ref[...], wo_ref[0:LRU_W, :]) + _dot(ys_ref[...], wo_ref[LRU_W:LRU_W + SSD_INNER, :])
    x1 = _layer_norm(ALPHA * x_ref[...] + g1_ref[...] * o) * l1g_ref[...] + l1b_ref[...]
    x1_ref[...] = x1
    v = _layer_norm(x1) * (1.0 + sc2_ref[...]) + sh2_ref[...]
    _store_row_tiles(v_ref, v)

    lt = _dot(v.astype(BF16), wrt_ref[...]).T + brc_ref[...]
    tl = lt.shape[1]
    row = lax.broadcasted_iota(I32, (GROUP_SIZE, tl), 0).astype(F32)
    big = float(GROUP_SIZE)
    neg = -jnp.inf
    lg = jnp.where(row < N_GROUPS, lt[0:GROUP_SIZE, :], neg)
    gmax = jnp.max(lg, axis=0, keepdims=True)
    gsel = jnp.min(jnp.where(lg == gmax, row, big), axis=0, keepdims=True)
    pg = 1.0 / jnp.sum(jnp.exp(lg - gmax), axis=0, keepdims=True)
    le = lt[GROUP_SIZE:2 * GROUP_SIZE, :]
    for j in range(1, N_GROUPS):
        le = jnp.where(gsel == j, lt[GROUP_SIZE * (j + 1):GROUP_SIZE * (j + 2), :], le)
    m1 = jnp.max(le, axis=0, keepdims=True)
    i1 = jnp.min(jnp.where(le == m1, row, big), axis=0, keepdims=True)
    rest = jnp.where(row == i1, neg, le)
    m2 = jnp.max(rest, axis=0, keepdims=True)
    i2 = jnp.min(jnp.where(rest == m2, row, big), axis=0, keepdims=True)
    e2 = jnp.exp(m2 - m1)
    den = 1.0 + e2
    eid = jnp.where(row == 0, gsel * GROUP_SIZE + i1, jnp.where(row == 1, gsel * GROUP_SIZE + i2, 0.0))
    eid_ref[...] = eid.astype(I32)
    gw_ref[...] = jnp.where(row == 0, pg * (1.0 / den), jnp.where(row == 1, pg * (e2 / den), 0.0))


def _post_call(yl, ys, x3, g1, sh2, sc2, p, tl):
    nb, L, _ = x3.shape
    nl = L // tl
    row = lambda b, l: (b, l, 0)
    vec = lambda b, l: (0, 0)
    tok = lambda b, l: (b * nl + l, 0)
    tokt = lambda b, l: (0, b * nl + l)
    return pl.pallas_call(
        _post_kernel,
        out_shape=[jax.ShapeDtypeStruct((nb, L, D), F32), jax.ShapeDtypeStruct(_tile_rows(nb * L), F32),
                   jax.ShapeDtypeStruct((SUBLANES, nb * L), I32), jax.ShapeDtypeStruct((SUBLANES, nb * L), F32)],
        grid=(nb, nl),
        in_specs=[pl.BlockSpec((None, tl, LRU_W), row), pl.BlockSpec((None, tl, SSD_INNER), row),
                  pl.BlockSpec((None, tl, D), row), _mod_spec(g1, tl), _mod_spec(sh2, tl), _mod_spec(sc2, tl),
                  pl.BlockSpec((LRU_W + SSD_INNER, D), vec), pl.BlockSpec((1, D), vec), pl.BlockSpec((1, D), vec),
                  pl.BlockSpec((D, LANES), vec), pl.BlockSpec((LANES, 1), vec)],
        out_specs=[pl.BlockSpec((None, tl, D), row), pl.BlockSpec(_tile_rows(tl), tok),
                   pl.BlockSpec((SUBLANES, tl), tokt), pl.BlockSpec((SUBLANES, tl), tokt)],
        compiler_params=_cparams(("parallel", "arbitrary")),
        name="post",
    )(yl, ys, x3, g1, sh2, sc2, p["wo"], p["l1g"], p["l1b"], p["wrt"], p["brc"])


def _route_kernel(eidp_ref, eids_ref, dest_ref, cnt_ref, run, poff, *, p_tiles):
    ph = pl.program_id(0)
    t = pl.program_id(1)
    n = ROUTE_TILE
    eid = jnp.where(t < p_tiles, eidp_ref[...], eids_ref[...])
    rowi = lax.broadcasted_iota(I32, (LANES, n), 0)
    oh0 = rowi == eid[0:1, :]
    oh1 = rowi == eid[1:2, :]
    oh = oh0.astype(F32) + oh1.astype(F32)
    tile_cnt = jnp.sum(oh, axis=1, keepdims=True)

    @pl.when((ph == 0) & (t == 0))
    def _():
        run[...] = jnp.zeros((LANES, n), F32)

    @pl.when(ph == 0)
    def _():
        run[...] = run[...] + tile_cnt

    @pl.when((ph == 1) & (t == 0))
    def _():
        counts = run[...]
        cnt_ref[...] = counts
        ci = counts.astype(I32)
        q = jnp.floor(counts * (1.0 / MOE_BM)).astype(I32)
        rem = ci - q * MOE_BM
        q = q + jnp.where(rem >= MOE_BM, 1, 0) - jnp.where(rem < 0, 1, 0)
        nblk = q + jnp.where(ci - q * MOE_BM > 0, 1, 0)
        r = lax.broadcasted_iota(I32, (LANES, LANES), 0)
        c = lax.broadcasted_iota(I32, (LANES, LANES), 1)
        lower = (r > c).astype(BF16)
        poff[...] = _dot(lower, nblk.astype(F32).astype(BF16)) * float(MOE_BM)
        run[...] = jnp.zeros((LANES, n), F32)

    @pl.when(ph == 1)
    def _():
        r = lax.broadcasted_iota(I32, (n, n), 0)
        c = lax.broadcasted_iota(I32, (n, n), 1)
        before = (r < c).astype(BF16)
        slot = _dot(oh.astype(BF16), before) + run[...] + poff[...]
        d0 = jnp.sum(jnp.where(oh0, slot, 0.0), axis=0, keepdims=True)
        d1 = jnp.sum(jnp.where(oh1, slot, 0.0), axis=0, keepdims=True)
        row = lax.broadcasted_iota(I32, (SUBLANES, n), 0)
        dest_ref[...] = jnp.where(row == 0, d0, jnp.where(row == 1, d1, 0.0)).astype(I32)
        run[...] = run[...] + tile_cnt


def _route_call(eid_p, eid_s):
    n = ROUTE_TILE
    p_tiles = eid_p.shape[1] // n
    s_tiles = eid_s.shape[1] // n
    t_all = (p_tiles + s_tiles) * n
    return pl.pallas_call(
        functools.partial(_route_kernel, p_tiles=p_tiles),
        out_shape=[jax.ShapeDtypeStruct((SUBLANES, t_all), I32), jax.ShapeDtypeStruct((LANES, n), F32)],
        grid=(2, p_tiles + s_tiles),
        in_specs=[pl.BlockSpec((SUBLANES, n), lambda ph, t: (0, jnp.minimum(t, p_tiles - 1))),
                  pl.BlockSpec((SUBLANES, n), lambda ph, t: (0, jnp.maximum(t - p_tiles, 0)))],
        out_specs=[pl.BlockSpec((SUBLANES, n), lambda ph, t: (0, t * ph)),
                   pl.BlockSpec((LANES, n), lambda ph, t: (0, 0))],
        scratch_shapes=[pltpu.VMEM((LANES, n), F32), pltpu.VMEM((LANES, n), F32)],
        compiler_params=_cparams(("arbitrary", "arbitrary")),
        name="route",
    )(eid_p, eid_s)


def _row_copy(src, src_row, dst, dst_row, sem):
    s0 = pl.multiple_of(src_row * SUBLANES, SUBLANES)
    d0 = pl.multiple_of(dst_row * SUBLANES, SUBLANES)
    return pltpu.make_async_copy(src.at[pl.ds(s0, SUBLANES), :], dst.at[pl.ds(d0, SUBLANES), :], sem)


def _scatter_rows(dest_ref, v_ref, xpad_ref, sem):
    def start(r, carry):
        for k in range(2):
            _row_copy(v_ref, r, xpad_ref, dest_ref[k, r], sem).start(priority=k)
        return carry

    def wait(r, carry):
        for k in range(2):
            _row_copy(v_ref, r, xpad_ref, dest_ref[k, r], sem).wait()
        return carry

    lax.fori_loop(0, TOK_TILE, start, 0, unroll=DMA_UNROLL)
    lax.fori_loop(0, TOK_TILE, wait, 0, unroll=DMA_UNROLL)


def _dispatch_kernel(dest_ref, vp_ref, vs_ref, xpad_in, xpad_ref, sem, *, p_tiles):
    del xpad_in
    t = pl.program_id(0)

    @pl.when(t < p_tiles)
    def _():
        _scatter_rows(dest_ref, vp_ref, xpad_ref, sem)

    @pl.when(t >= p_tiles)
    def _():
        _scatter_rows(dest_ref, vs_ref, xpad_ref, sem)


def _dispatch_call(dest, v_p, v_s, xpad0):
    p_tiles = v_p.shape[0] // _tile_rows(TOK_TILE)[0]
    s_tiles = v_s.shape[0] // _tile_rows(TOK_TILE)[0]
    return pl.pallas_call(
        functools.partial(_dispatch_kernel, p_tiles=p_tiles),
        out_shape=jax.ShapeDtypeStruct(xpad0.shape, xpad0.dtype),
        grid=(p_tiles + s_tiles,),
        in_specs=[pl.BlockSpec((SUBLANES, TOK_TILE), lambda t: (0, t), memory_space=pltpu.SMEM),
                  pl.BlockSpec(_tile_rows(TOK_TILE), lambda t: (jnp.minimum(t, p_tiles - 1), 0)),
                  pl.BlockSpec(_tile_rows(TOK_TILE), lambda t: (jnp.maximum(t - p_tiles, 0), 0)),
                  pl.BlockSpec(memory_space=pl.ANY)],
        out_specs=pl.BlockSpec(memory_space=pl.ANY),
        scratch_shapes=[pltpu.SemaphoreType.DMA],
        input_output_aliases={3: 0},
        compiler_params=_cparams(("arbitrary",)),
        name="dispatch",
    )(dest, v_p, v_s, xpad0)


def _expert_kernel(be_ref, nb_ref, x_ref, wg_ref, wu_ref, wd_ref, o_ref, wgb, wub, wdb):
    i = pl.program_id(0)

    @pl.when(i < nb_ref[0])
    def _():
        @pl.when((i == 0) | (be_ref[i] != be_ref[jnp.maximum(i - 1, 0)]))
        def _():
            wgb[...] = wg_ref[...].astype(BF16)
            wub[...] = wu_ref[...].astype(BF16)
            wdb[...] = wd_ref[...].astype(BF16)

        x = _load_row_tiles(x_ref).astype(BF16)
        h = _silu(_dot(x, wgb[...])) * _dot(x, wub[...])
        _store_row_tiles(o_ref, _dot(h.astype(BF16), wdb[...]))

    @pl.when(i >= nb_ref[0])
    def _():
        o_ref[...] = jnp.zeros(o_ref.shape, F32)


def _expert_call(blk_e, nblk, xpad, w_gate, w_up, w_down):
    n_rows = xpad.shape[0] // SUBLANES
    blk = lambda i, be, nb: (jnp.minimum(i, nb[0] - 1), 0)
    wsel = lambda i, be, nb: (be[jnp.minimum(i, nb[0] - 1)], 0, 0)
    return pl.pallas_call(
        _expert_kernel,
        out_shape=jax.ShapeDtypeStruct(_tile_rows(n_rows), F32),
        grid_spec=pltpu.PrefetchScalarGridSpec(
            num_scalar_prefetch=2,
            grid=(n_rows // MOE_BM,),
            in_specs=[pl.BlockSpec(_tile_rows(MOE_BM), blk),
                      pl.BlockSpec((None, D, D_FF), wsel), pl.BlockSpec((None, D, D_FF), wsel),
                      pl.BlockSpec((None, D_FF, D), wsel)],
            out_specs=pl.BlockSpec(_tile_rows(MOE_BM), lambda i, be, nb: (i, 0)),
            scratch_shapes=[pltpu.VMEM((D, D_FF), BF16), pltpu.VMEM((D, D_FF), BF16), pltpu.VMEM((D_FF, D), BF16)]),
        compiler_params=_cparams(("arbitrary",)),
        name="expert",
    )(blk_e, nblk, xpad, w_gate, w_up, w_down)


def _combine_kernel(dest_ref, gw0_ref, gw1_ref, x1_ref, g2_ref, l2g_ref, l2b_ref, ypad_ref, y_ref, ybuf, sems,
                    *, tok_off, k_stride):
    i = pl.program_id(0)

    def gather(tile, slot, go):
        base = tok_off + tile * TOK_TILE

        def body(r, carry):
            for k in range(2):
                cp = _row_copy(ypad_ref, dest_ref[k * k_stride + base + r], ybuf.at[slot, k], r, sems.at[slot])
                if go:
                    cp.start(priority=k)
                else:
                    cp.wait()
            return carry

        lax.fori_loop(0, TOK_TILE, body, 0, unroll=DMA_UNROLL)

    @pl.when(i == 0)
    def _():
        gather(0, 0, True)

    @pl.when(i + 1 < pl.num_programs(0))
    def _():
        gather(i + 1, (i + 1) % 2, True)

    slot = i % 2
    gather(i, slot, False)
    f = _load_row_tiles(ybuf.at[slot, 0]) * gw0_ref[...] + _load_row_tiles(ybuf.at[slot, 1]) * gw1_ref[...]
    y_ref[...] = _layer_norm(ALPHA * x1_ref[...] + g2_ref[...] * f) * l2g_ref[...] + l2b_ref[...]


def _combine_call(dest, gw0, gw1, x1, g2, l2g, l2b, ypad, tok_off):
    nb, L, _ = x1.shape
    tl = TOK_TILE
    nl = L // tl
    row = lambda i, d: (i // nl, i % nl, 0)
    vec = lambda i, d: (0, 0)
    tok = lambda i, d: (i, 0)
    if g2.shape[1] == 1:
        g2_spec = pl.BlockSpec((None, 1, D), lambda i, d: (i // nl, 0, 0))
    else:
        g2_spec = pl.BlockSpec((None, tl, D), row)
    return pl.pallas_call(
        functools.partial(_combine_kernel, tok_off=tok_off, k_stride=dest.shape[1]),
        out_shape=jax.ShapeDtypeStruct((nb, L, D), F32),
        grid_spec=pltpu.PrefetchScalarGridSpec(
            num_scalar_prefetch=1,
            grid=(nb * nl,),
            in_specs=[pl.BlockSpec((tl, 1), tok), pl.BlockSpec((tl, 1), tok),
                      pl.BlockSpec((None, tl, D), row), g2_spec,
                      pl.BlockSpec((1, D), vec), pl.BlockSpec((1, D), vec),
                      pl.BlockSpec(memory_space=pl.ANY)],
            out_specs=pl.BlockSpec((None, tl, D), row),
            scratch_shapes=[pltpu.VMEM((2, 2) + _tile_rows(tl), F32), pltpu.SemaphoreType.DMA((2,))]),
        compiler_params=_cparams(("arbitrary",)),
        name="combine",
    )(dest[:2].reshape(-1), gw0, gw1, x1, g2, l2g, l2b, ypad)


def _block_diag(w):
    nh, blk, _ = w.shape
    w4 = w.reshape(nh // LRU_PACK, LRU_PACK, blk, blk)
    eye = jnp.eye(LRU_PACK, dtype=w.dtype)
    out = jnp.einsum("gaij,ab->gaibj", w4, eye)
    return out.reshape(nh // LRU_PACK, LRU_PACK * blk, LRU_PACK * blk).astype(BF16)


def _row(v):
    return v.reshape(1, -1).astype(F32)


def _pad_lanes(v):
    return jnp.pad(v.reshape(1, -1).astype(F32), ((0, 0), (0, LANES - v.shape[-1])))


def kernel(x_prompt, x_sample, c_prompt, c_sample, state_lru_conv, state_lru_h, state_ssd_conv, state_ssd, w_ada, b_ada, w_in, lru_conv_w, lru_conv_b, lru_wa, lru_ba, lru_wx, lru_bx, lru_lambda, lru_norm_g, ssd_conv_w, ssd_conv_b, ssd_dt_bias, ssd_a_log, ssd_d, ssd_norm_g, w_out, ln1_g, ln1_b, w_rg, b_rg, w_re, b_re, w_gate, w_up, w_down, ln2_g, ln2_b):
    assert w_ada.shape[0] == DEPTH == 1
    nbp, seq, _ = x_prompt.shape
    nbs = x_sample.shape[0]
    t_p = nbp * seq
    t_all = t_p + nbs
    assert x_sample.shape[1] == 1 and t_p % ROUTE_TILE == 0 and nbs % TOK_TILE == 0

    c_rows = -(-(nbp + nbs) // 16) * 16
    c_all = jnp.pad(jnp.concatenate([c_prompt, c_sample], axis=0), ((0, c_rows - nbp - nbs), (0, 0)))
    w_in0 = w_in[0]
    o_z, o_xbc, o_dt = 2 * LRU_W, 2 * LRU_W + SSD_INNER, 2 * LRU_W + SSD_INNER + SSD_CONV_DIM
    w_dt = w_in0[:, o_dt:]
    ws_p = [w_in0[:, :o_z].astype(BF16), w_in0[:, o_z:o_xbc].astype(BF16), w_in0[:, o_xbc:o_dt].astype(BF16),
            jnp.pad(w_dt, ((0, 0), (0, LANES - SSD_HEADS))).astype(BF16)]
    ws_s = ws_p[:3] + [jnp.repeat(w_dt, SSD_P, axis=1).astype(BF16)]
    lp = dict(cw=lru_conv_w[0], cb=_row(lru_conv_b[0]), wa=_block_diag(lru_wa[0]), wx=_block_diag(lru_wx[0]),
              ba=_row(lru_ba[0]), bx=_row(lru_bx[0]), lam=_row(lru_lambda[0]), ng=_row(lru_norm_g[0]))
    sp = dict(cw=ssd_conv_w[0], cb=_row(ssd_conv_b[0]), dtb=_pad_lanes(ssd_dt_bias[0]), alog=_pad_lanes(ssd_a_log[0]),
              dexp=_row(jnp.repeat(ssd_d[0], SSD_P)), ng=_row(ssd_norm_g[0]),
              dtbe=_row(jnp.repeat(ssd_dt_bias[0], SSD_P)), aloge=_row(jnp.repeat(ssd_a_log[0], SSD_P)))
    wrt = jnp.zeros((D, LANES), F32).at[:, 0:N_GROUPS].set(w_rg[0]).at[:, GROUP_SIZE:ROUTE_ROWS].set(w_re[0]).astype(BF16)
    brc = jnp.zeros((LANES, 1), F32).at[0:N_GROUPS, 0].set(b_rg[0]).at[GROUP_SIZE:ROUTE_ROWS, 0].set(b_re[0])
    pp = dict(wo=w_out[0].astype(BF16), l1g=_row(ln1_g[0]), l1b=_row(ln1_b[0]), wrt=wrt, brc=brc)

    mod = _ada_call(c_all, w_ada[0], _row(b_ada[0])).reshape(c_rows, N_MOD, D)
    mp = [mod[:nbp, k].reshape(nbp, 1, D) for k in range(N_MOD)]
    ms = [mod[nbp:nbp + nbs, k].reshape(1, nbs, D) for k in range(N_MOD)]

    xg, z, xbc, dtr = _proj_call(x_prompt, mp[0], mp[1], ws_p, SEQ_TILE)
    yl, p_lru_conv, p_lru_h = _lru_call(xg, lp, SEQ_TILE)
    ys, p_ssd_conv, p_ssd = _ssd_call(xbc, z, dtr, sp)
    x1_p, v_p, eid_p, gw_p = _post_call(yl, ys, x_prompt, mp[2], mp[3], mp[4], pp, SEQ_TILE)

    xs3 = x_sample.reshape(1, nbs, D)
    xg_s, z_s, xbc_s, dte_s = _proj_call(xs3, ms[0], ms[1], ws_s, nbs)
    xg_s, z_s, xbc_s, dte_s = xg_s[0], z_s[0], xbc_s[0], dte_s[0]
    yl_s, h_s, xs_s, bm_s, cm_s, xdtt, dect = _srow_call(
        xg_s, xbc_s, dte_s, state_lru_conv[0].reshape(nbs, -1), state_lru_h[0],
        state_ssd_conv[0].reshape(nbs, -1), lp, sp)
    sb = SUBLANES
    regroup = lambda a: a.reshape(SSD_INNER, nbs // sb, sb).transpose(1, 0, 2)
    s_new, ys_s = _sstate_call(state_ssd[0].reshape(nbs, SSD_INNER, SSD_N), regroup(xdtt), regroup(dect),
                               bm_s, cm_s, xs_s, z_s, sp["dexp"], sp["ng"], sb)
    x1_s, v_s, eid_s, gw_s = _post_call(yl_s[None], ys_s[None], xs3, ms[2], ms[3], ms[4], pp, nbs)

    s_pad = -nbs % ROUTE_TILE
    dest, counts = _route_call(eid_p, jnp.pad(eid_s, ((0, 0), (0, s_pad)), constant_values=-1))
    counts = counts[:N_EXPERTS, 0].astype(I32)
    n_blocks = -(-(2 * t_all) // MOE_BM) + N_EXPERTS
    pend = jnp.cumsum(((counts + MOE_BM - 1) // MOE_BM) * MOE_BM)
    blk_start = jnp.arange(n_blocks, dtype=I32) * MOE_BM
    blk_e = jnp.minimum(jnp.sum((pend[None, :] <= blk_start[:, None]).astype(I32), axis=1), N_EXPERTS - 1)
    nblk = (pend[-1:] // MOE_BM).astype(I32)
    xpad = _dispatch_call(dest, v_p, v_s, jnp.zeros(_tile_rows(n_blocks * MOE_BM), F32))
    ypad = _expert_call(blk_e, nblk, xpad, w_gate[0], w_up[0], w_down[0])
    l2g, l2b = _row(ln2_g[0]), _row(ln2_b[0])
    y_p = _combine_call(dest, gw_p[0].reshape(t_p, 1), gw_p[1].reshape(t_p, 1), x1_p, mp[5], l2g, l2b, ypad, 0)
    y_s = _combine_call(dest, gw_s[0].reshape(nbs, 1), gw_s[1].reshape(nbs, 1), x1_s, ms[5], l2g, l2b, ypad, t_p)

    s_lru_conv = jnp.concatenate([state_lru_conv[0][:, 1:], xg_s[:, None, :LRU_W]], axis=1)
    s_ssd_conv = jnp.concatenate([state_ssd_conv[0][:, 1:], xbc_s[:, None, :]], axis=1)
    return (y_p, y_s.reshape(nbs, 1, D),
            p_lru_conv[None], p_lru_h.reshape(1, nbp, LRU_W), p_ssd_conv[None],
            p_ssd.reshape(1, nbp, SSD_HEADS, SSD_P, SSD_N),
            s_lru_conv[None], h_s[None], s_ssd_conv[None],
            s_new.reshape(1, nbs, SSD_HEADS, SSD_P, SSD_N))
```

```python
import functools
import math

import jax
import jax.numpy as jnp
from jax import lax
from jax.experimental import pallas as pl
from jax.experimental.pallas import tpu as pltpu

F32 = jnp.float32
BF16 = jnp.bfloat16
I32 = jnp.int32
HIGHEST = lax.Precision.HIGHEST

D = 1024
DEPTH = 1
CONV_W = 4
LRU_W = D
LRU_HEADS = 16
LRU_C = 8.0
LRU_PACK = 4
LRU_PACK_W = LRU_PACK * (LRU_W // LRU_HEADS)
SSD_INNER = D
SSD_HEADS = 16
SSD_P = SSD_INNER // SSD_HEADS
SSD_GROUPS = 2
SSD_N = 128
SSD_Q = 128
SSD_CONV_DIM = SSD_INNER + 2 * SSD_GROUPS * SSD_N
N_GROUPS = 4
GROUP_SIZE = 8
N_EXPERTS = N_GROUPS * GROUP_SIZE
D_FF = D // 2
N_MOD = 6
LN_EPS = 1e-5
RMS_EPS = 1e-6
ALPHA = (2.0 * DEPTH) ** 0.25

LANES = 128
SUBLANES = 8
VMEM_LIMIT = 56 * 1024 * 1024
SEQ_TILE = 512
TOK_TILE = 128
ROUTE_TILE = 512
DMA_UNROLL = 8
MOE_BM = 384
ROUTE_ROWS = 40

NT_DIMS = (((1,), (1,)), ((), ()))


def _tile_rows(rows):
    return (rows * SUBLANES, LANES)


def _cparams(sem):
    return pltpu.CompilerParams(dimension_semantics=sem, vmem_limit_bytes=VMEM_LIMIT)


def _sigmoid(x):
    return 0.5 * (jnp.tanh(0.5 * x) + 1.0)


def _silu(x):
    return x * _sigmoid(x)


def _softplus(x):
    return jnp.maximum(x, 0.0) + jnp.log1p(jnp.exp(-jnp.abs(x)))


def _gelu_tanh(x):
    return 0.5 * x * (1.0 + jnp.tanh(math.sqrt(2.0 / math.pi) * (x + 0.044715 * (x * x * x))))


def _layer_norm(x):
    mu = jnp.mean(x, axis=-1, keepdims=True)
    xc = x - mu
    var = jnp.mean(xc * xc, axis=-1, keepdims=True)
    return xc * lax.rsqrt(var + LN_EPS)


def _rms_norm(x, g):
    return x * lax.rsqrt(jnp.mean(x * x, axis=-1, keepdims=True) + RMS_EPS) * g


def _dot(a, b, **kw):
    return jnp.dot(a, b, preferred_element_type=F32, **kw)


def _dot_nt(a, b, **kw):
    return lax.dot_general(a, b, NT_DIMS, preferred_element_type=F32, **kw)


def _store_row_tiles(ref, val):
    rows = val.shape[0]
    for j in range(SUBLANES):
        ref[pl.ds(j, rows, stride=SUBLANES), :] = val[:, j * LANES:(j + 1) * LANES]


def _load_row_tiles(ref):
    rows = ref.shape[0] // SUBLANES
    return jnp.concatenate([ref[pl.ds(j, rows, stride=SUBLANES), :] for j in range(SUBLANES)], axis=1)


def _lru_gates(xc, xb, wa, wx, ba, bx, sp):
    r = _sigmoid(_dot(xb, wa) + ba)
    i = _sigmoid(_dot(xb, wx) + bx)
    log_a = (-LRU_C) * r * sp
    a = jnp.exp(log_a)
    mult = jnp.sqrt(jnp.tanh(-log_a) * (a * a + 1.0))
    return a, mult * (i * xc)


def _ada_kernel(c_ref, w_ref, b_ref, o_ref):
    s = _silu(c_ref[...]).astype(BF16)
    o_ref[...] = _dot(s, w_ref[...].astype(BF16)) + b_ref[...]


def _ada_call(c_all, w_ada, b_ada):
    rows = c_all.shape[0]
    n = w_ada.shape[1]
    tn = 512
    return pl.pallas_call(
        _ada_kernel,
        out_shape=jax.ShapeDtypeStruct((rows, n), F32),
        grid=(n // tn,),
        in_specs=[pl.BlockSpec((rows, D), lambda j: (0, 0)),
                  pl.BlockSpec((D, tn), lambda j: (0, j)),
                  pl.BlockSpec((1, tn), lambda j: (0, j))],
        out_specs=pl.BlockSpec((rows, tn), lambda j: (0, j)),
        compiler_params=_cparams(("arbitrary",)),
        name="ada",
    )(c_all, w_ada, b_ada)


def _proj_kernel(x_ref, sh_ref, sc_ref, w1_ref, w2_ref, w3_ref, w4_ref, o1_ref, o2_ref, o3_ref, o4_ref):
    u = _layer_norm(x_ref[...]) * (1.0 + sc_ref[...]) + sh_ref[...]
    ub = u.astype(BF16)
    o1_ref[...] = _dot(ub, w1_ref[...])
    o2_ref[...] = _dot(ub, w2_ref[...])
    o3_ref[...] = _dot(ub, w3_ref[...])
    o4_ref[...] = _dot(ub, w4_ref[...])


def _mod_spec(mod, tl):
    if mod.shape[1] == 1:
        return pl.BlockSpec((None, 1, D), lambda b, l: (b, 0, 0))
    return pl.BlockSpec((None, tl, D), lambda b, l: (b, l, 0))


def _proj_call(x3, sh3, sc3, ws, tl):
    nb, L, _ = x3.shape
    widths = [w.shape[1] for w in ws]
    row = lambda b, l: (b, l, 0)
    full = lambda b, l: (0, 0)
    return pl.pallas_call(
        _proj_kernel,
        out_shape=[jax.ShapeDtypeStruct((nb, L, n), F32) for n in widths],
        grid=(nb, L // tl),
        in_specs=[pl.BlockSpec((None, tl, D), row), _mod_spec(sh3, tl), _mod_spec(sc3, tl)]
                 + [pl.BlockSpec((D, n), full) for n in widths],
        out_specs=[pl.BlockSpec((None, tl, n), row) for n in widths],
        compiler_params=_cparams(("parallel", "arbitrary")),
        name="proj",
    )(x3, sh3, sc3, *ws)


def _scan_rows(a_scr, u_scr, h0, n_rows):
    sub = lax.broadcasted_iota(I32, (SUBLANES, D), 0)
    shifts = (1, 2, 4)
    masks = [sub >= d for d in shifts]

    def body(j, hp):
        r0 = pl.multiple_of(j * SUBLANES, SUBLANES)
        a = a_scr[pl.ds(r0, SUBLANES), :]
        u = u_scr[pl.ds(r0, SUBLANES), :]
        for d, m in zip(shifts, masks):
            ash = pltpu.roll(a, d, 0)
            ush = pltpu.roll(u, d, 0)
            u = jnp.where(m, u + a * ush, u)
            a = jnp.where(m, a * ash, a)
        h = u + a * hp
        u_scr[pl.ds(r0, SUBLANES), :] = h
        return h[SUBLANES - 1:SUBLANES, :]

    return lax.fori_loop(0, n_rows // SUBLANES, body, h0)


def _lru_kernel(x_ref, g_ref, cw_ref, cb_ref, wa_ref, wx_ref, ba_ref, bx_ref, lam_ref, ng_ref,
                y_ref, conv_ref, h_ref, xpad, xc_scr, a_scr, u_scr, hc_scr, *, tl):
    @pl.when(pl.program_id(1) == 0)
    def _():
        xpad[0:SUBLANES, :] = jnp.zeros((SUBLANES, D), F32)
        hc_scr[...] = jnp.zeros((1, D), F32)

    xpad[SUBLANES:SUBLANES + tl, :] = x_ref[...]
    off = SUBLANES - (CONV_W - 1)
    acc = xpad[off:off + tl, :] * cw_ref[0:1, :]
    for k in range(1, CONV_W):
        acc = acc + xpad[off + k:off + k + tl, :] * cw_ref[k:k + 1, :]
    xc_scr[...] = acc + cb_ref[...]
    conv_ref[...] = xpad[SUBLANES + tl - (CONV_W - 1):SUBLANES + tl, :]
    xpad[0:SUBLANES, :] = xpad[tl:tl + SUBLANES, :]

    sp = _softplus(-lam_ref[...])
    for j in range(LRU_W // LRU_PACK_W):
        cs = slice(j * LRU_PACK_W, (j + 1) * LRU_PACK_W)
        xc = xc_scr[:, cs]
        a, u = _lru_gates(xc, xc.astype(BF16), wa_ref[j], wx_ref[j], ba_ref[:, cs], bx_ref[:, cs], sp[:, cs])
        a_scr[:, cs] = a
        u_scr[:, cs] = u

    h_last = _scan_rows(a_scr, u_scr, hc_scr[...], tl)
    hc_scr[...] = h_last
    h_ref[...] = h_last
    y = u_scr[...] * _gelu_tanh(g_ref[...])
    y_ref[...] = _rms_norm(y, ng_ref[...]).astype(BF16)


def _lru_call(xg, p, tl):
    nb, L, _ = xg.shape
    vec = lambda b, l: (0, 0)
    blk = lambda b, l: (0, 0, 0)
    return pl.pallas_call(
        functools.partial(_lru_kernel, tl=tl),
        out_shape=[jax.ShapeDtypeStruct((nb, L, LRU_W), BF16),
                   jax.ShapeDtypeStruct((nb, CONV_W - 1, LRU_W), F32),
                   jax.ShapeDtypeStruct((nb, 1, LRU_W), F32)],
        grid=(nb, L // tl),
        in_specs=[pl.BlockSpec((None, tl, LRU_W), lambda b, l: (b, l, 0)),
                  pl.BlockSpec((None, tl, LRU_W), lambda b, l: (b, l, 1)),
                  pl.BlockSpec((CONV_W, LRU_W), vec), pl.BlockSpec((1, LRU_W), vec),
                  pl.BlockSpec((LRU_W // LRU_PACK_W, LRU_PACK_W, LRU_PACK_W), blk),
                  pl.BlockSpec((LRU_W // LRU_PACK_W, LRU_PACK_W, LRU_PACK_W), blk),
                  pl.BlockSpec((1, LRU_W), vec), pl.BlockSpec((1, LRU_W), vec),
                  pl.BlockSpec((1, LRU_W), vec), pl.BlockSpec((1, LRU_W), vec)],
        out_specs=[pl.BlockSpec((None, tl, LRU_W), lambda b, l: (b, l, 0)),
                   pl.BlockSpec((None, CONV_W - 1, LRU_W), lambda b, l: (b, 0, 0)),
                   pl.BlockSpec((None, 1, LRU_W), lambda b, l: (b, 0, 0))],
        scratch_shapes=[pltpu.VMEM((tl + SUBLANES, LRU_W), F32), pltpu.VMEM((tl, LRU_W), F32),
                        pltpu.VMEM((tl, LRU_W), F32), pltpu.VMEM((tl, LRU_W), F32),
                        pltpu.VMEM((1, LRU_W), F32)],
        compiler_params=_cparams(("parallel", "arbitrary")),
        name="lru",
    )(xg, xg, p["cw"], p["cb"], p["wa"], p["wx"], p["ba"], p["bx"], p["lam"], p["ng"])


def _ssd_kernel(xbc_ref, z_ref, dt_ref, cw_ref, cb_ref, dtb_ref, alog_ref, dexp_ref, ng_ref,
                y_ref, conv_ref, st_ref, xpad, xc_scr, st_scr, y_scr, ea_scr, ew_scr):
    q = SSD_Q

    @pl.when(pl.program_id(1) == 0)
    def _():
        xpad[0:SUBLANES, :] = jnp.zeros((SUBLANES, SSD_CONV_DIM), F32)
        st_scr[...] = jnp.zeros((SSD_N, SSD_INNER), F32)

    xpad[SUBLANES:SUBLANES + q, :] = xbc_ref[...]
    off = SUBLANES - (CONV_W - 1)
    acc = xpad[off:off + q, :] * cw_ref[0:1, :]
    for k in range(1, CONV_W):
        acc = acc + xpad[off + k:off + k + q, :] * cw_ref[k:k + 1, :]
    xc_scr[...] = _silu(acc + cb_ref[...])
    conv_ref[...] = xpad[SUBLANES + q - (CONV_W - 1):SUBLANES + q, :]
    xpad[0:SUBLANES, :] = xpad[q:q + SUBLANES, :]

    dt = _softplus(dt_ref[...] + dtb_ref[...])
    da = dt * (-jnp.exp(alog_ref[...]))
    ri = lax.broadcasted_iota(I32, (q, q), 0)
    ci = lax.broadcasted_iota(I32, (q, q), 1)
    causal = ri >= ci
    acs = _dot(causal.astype(F32), da, precision=HIGHEST)
    acs_t = acs.T
    dt_t = dt.T
    w_end = dt * jnp.exp(acs[q - 1:q, :] - acs)

    half = SSD_INNER // SSD_GROUPS
    hpg = SSD_HEADS // SSD_GROUPS
    cgb, cbs = [], []
    for g in range(SSD_GROUPS):
        bgb = xc_scr[:, SSD_INNER + g * SSD_N:SSD_INNER + (g + 1) * SSD_N].astype(BF16)
        cgb.append(xc_scr[:, SSD_INNER + (SSD_GROUPS + g) * SSD_N:SSD_INNER + (SSD_GROUPS + g + 1) * SSD_N].astype(BF16))
        cbs.append(_dot_nt(cgb[g], bgb))

    low = lax.broadcasted_iota(I32, (q, LANES), 1) < SSD_P
    for k in range(SSD_HEADS // 2):
        cs = slice(k * LANES, (k + 1) * LANES)
        cb = cbs[(2 * k) // hpg]
        m, colb, wb = [], [], []
        for h in (2 * k, 2 * k + 1):
            colb.append(jnp.broadcast_to(acs[:, h:h + 1], (q, LANES)))
            wb.append(jnp.broadcast_to(w_end[:, h:h + 1], (q, LANES)))
            seg = colb[-1] - acs_t[h:h + 1, :]
            decay = jnp.where(causal, jnp.exp(jnp.where(causal, seg, 0.0)), 0.0)
            m.append((cb * (decay * dt_t[h:h + 1, :])).astype(BF16))
        ea_scr[:, cs] = jnp.where(low, colb[0], colb[1])
        ew_scr[:, cs] = jnp.where(low, wb[0], wb[1])
        xk = xc_scr[:, cs]
        zero = jnp.zeros_like(xk)
        rhs = jnp.concatenate([jnp.where(low, xk, zero), jnp.where(low, zero, xk)], axis=0).astype(BF16)
        y_scr[:, cs] = _dot(jnp.concatenate(m, axis=1), rhs)

    x = xc_scr[:, 0:SSD_INNER]
    ea = ea_scr[...]
    w = (x * ew_scr[...]).astype(BF16)
    s_in = st_scr[...]
    sb = s_in.astype(BF16)
    y_off, c_state = [], []
    for g in range(SSD_GROUPS):
        gs = slice(g * half, (g + 1) * half)
        bgt = xc_scr[:, SSD_INNER + g * SSD_N:SSD_INNER + (g + 1) * SSD_N].T.astype(BF16)
        y_off.append(_dot(cgb[g], sb[:, gs]))
        c_state.append(_dot(bgt, w[:, gs]))
    st_scr[...] = jnp.exp(ea[q - 1:q, :]) * s_in + jnp.concatenate(c_state, axis=1)
    y = y_scr[...] + jnp.concatenate(y_off, axis=1) * jnp.exp(ea) + dexp_ref[...] * x

    yz = y * _silu(z_ref[...])
    y_ref[...] = _rms_norm(yz, ng_ref[...]).astype(BF16)

    @pl.when(pl.program_id(1) == pl.num_programs(1) - 1)
    def _():
        st_ref[...] = st_scr[...].T


def _ssd_call(xbc, z, dtr, p):
    nb, L, _ = xbc.shape
    q = SSD_Q
    vec = lambda b, c: (0, 0)
    row = lambda b, c: (b, c, 0)
    return pl.pallas_call(
        _ssd_kernel,
        out_shape=[jax.ShapeDtypeStruct((nb, L, SSD_INNER), BF16),
                   jax.ShapeDtypeStruct((nb, CONV_W - 1, SSD_CONV_DIM), F32),
                   jax.ShapeDtypeStruct((nb, SSD_INNER, SSD_N), F32)],
        grid=(nb, L // q),
        in_specs=[pl.BlockSpec((None, q, SSD_CONV_DIM), row), pl.BlockSpec((None, q, SSD_INNER), row),
                  pl.BlockSpec((None, q, LANES), row),
                  pl.BlockSpec((CONV_W, SSD_CONV_DIM), vec), pl.BlockSpec((1, SSD_CONV_DIM), vec),
                  pl.BlockSpec((1, LANES), vec), pl.BlockSpec((1, LANES), vec),
                  pl.BlockSpec((1, SSD_INNER), vec), pl.BlockSpec((1, SSD_INNER), vec)],
        out_specs=[pl.BlockSpec((None, q, SSD_INNER), row),
                   pl.BlockSpec((None, CONV_W - 1, SSD_CONV_DIM), lambda b, c: (b, 0, 0)),
                   pl.BlockSpec((None, SSD_INNER, SSD_N), lambda b, c: (b, 0, 0))],
        scratch_shapes=[pltpu.VMEM((q + SUBLANES, SSD_CONV_DIM), F32), pltpu.VMEM((q, SSD_CONV_DIM), F32),
                        pltpu.VMEM((SSD_N, SSD_INNER), F32), pltpu.VMEM((q, SSD_INNER), F32),
                        pltpu.VMEM((q, SSD_INNER), F32), pltpu.VMEM((q, SSD_INNER), F32)],
        compiler_params=_cparams(("parallel", "arbitrary")),
        name="ssd",
    )(xbc, z, dtr, p["cw"], p["cb"], p["dtb"], p["alog"], p["dexp"], p["ng"])


def _srow_kernel(xg_ref, xbc_ref, dte_ref, lconv_ref, h0_ref, sconv_ref,
                 lcw_ref, lcb_ref, wa_ref, wx_ref, ba_ref, bx_ref, lam_ref, lng_ref,
                 scw_ref, scb_ref, dtbe_ref, aloge_ref,
                 yl_ref, h_ref, xs_ref, bm_ref, cm_ref, xdtt_ref, dect_ref, xc_scr):
    xl = xg_ref[:, 0:LRU_W]
    acc = lcb_ref[...] + xl * lcw_ref[CONV_W - 1:CONV_W, :]
    for k in range(CONV_W - 1):
        acc = acc + lconv_ref[:, k * LRU_W:(k + 1) * LRU_W] * lcw_ref[k:k + 1, :]
    xc_scr[...] = acc
    sp = _softplus(-lam_ref[...])
    for j in range(LRU_W // LRU_PACK_W):
        cs = slice(j * LRU_PACK_W, (j + 1) * LRU_PACK_W)
        xc = xc_scr[:, cs]
        a, u = _lru_gates(xc, xc.astype(BF16), wa_ref[j], wx_ref[j], ba_ref[:, cs], bx_ref[:, cs], sp[:, cs])
        h_ref[:, cs] = a * h0_ref[:, cs] + u
    y = h_ref[...] * _gelu_tanh(xg_ref[:, LRU_W:2 * LRU_W])
    yl_ref[...] = _rms_norm(y, lng_ref[...]).astype(BF16)

    acc = scb_ref[...] + xbc_ref[...] * scw_ref[CONV_W - 1:CONV_W, :]
    for k in range(CONV_W - 1):
        acc = acc + sconv_ref[:, k * SSD_CONV_DIM:(k + 1) * SSD_CONV_DIM] * scw_ref[k:k + 1, :]
    xc = _silu(acc)
    xs = xc[:, 0:SSD_INNER]
    xs_ref[...] = xs
    bm_ref[...] = xc[:, SSD_INNER:SSD_INNER + SSD_GROUPS * SSD_N]
    cm_ref[...] = xc[:, SSD_INNER + SSD_GROUPS * SSD_N:]
    dt = _softplus(dte_ref[...] + dtbe_ref[...])
    dec = jnp.exp(dt * (-jnp.exp(aloge_ref[...])))
    xdtt_ref[...] = (xs * dt).T
    dect_ref[...] = dec.T


def _srow_call(xg, xbc, dte, lconv, h0, sconv, lp, sp):
    n = xg.shape[0]
    args = (xg, xbc, dte, lconv, h0, sconv, lp["cw"], lp["cb"], lp["wa"], lp["wx"], lp["ba"], lp["bx"],
            lp["lam"], lp["ng"], sp["cw"], sp["cb"], sp["dtbe"], sp["aloge"])
    full = lambda a: pl.BlockSpec(a.shape, lambda i, nd=a.ndim: (0,) * nd)
    outs = [jax.ShapeDtypeStruct((n, LRU_W), BF16), jax.ShapeDtypeStruct((n, LRU_W), F32),
            jax.ShapeDtypeStruct((n, SSD_INNER), F32), jax.ShapeDtypeStruct((n, SSD_GROUPS * SSD_N), F32),
            jax.ShapeDtypeStruct((n, SSD_GROUPS * SSD_N), F32),
            jax.ShapeDtypeStruct((SSD_INNER, n), F32), jax.ShapeDtypeStruct((SSD_INNER, n), F32)]
    return pl.pallas_call(
        _srow_kernel,
        out_shape=outs,
        grid=(1,),
        in_specs=[full(a) for a in args],
        out_specs=[pl.BlockSpec(o.shape, lambda i: (0, 0)) for o in outs],
        scratch_shapes=[pltpu.VMEM((n, LRU_W), F32)],
        compiler_params=_cparams(("arbitrary",)),
        name="srow",
    )(*args)


def _sstate_kernel(s0_ref, xq_ref, dq_ref, bm_ref, cm_ref, xs_ref, z_ref, dexp_ref, ng_ref,
                   sn_ref, ys_ref, yraw):
    nb = s0_ref.shape[0]
    half = SSD_INNER // SSD_GROUPS
    for bi in range(nb):
        brow = jnp.concatenate(
            [jnp.broadcast_to(bm_ref[bi:bi + 1, g * SSD_N:(g + 1) * SSD_N], (half, SSD_N)) for g in range(SSD_GROUPS)],
            axis=0)
        s = dq_ref[:, bi:bi + 1] * s0_ref[bi] + xq_ref[:, bi:bi + 1] * brow
        sn_ref[bi] = s
        sb = s.astype(BF16)
        for g in range(SSD_GROUPS):
            cg = cm_ref[:, g * SSD_N:(g + 1) * SSD_N].astype(BF16)
            res = _dot_nt(cg, sb[g * half:(g + 1) * half, :])
            yraw[bi:bi + 1, g * half:(g + 1) * half] = res[bi:bi + 1, :]
    y = yraw[...] + dexp_ref[...] * xs_ref[...]
    ys_ref[...] = _rms_norm(y * _silu(z_ref[...]), ng_ref[...]).astype(BF16)


def _sstate_call(s0, xq, dq, bm, cm, xs, z, dexp, ng, nb):
    n = s0.shape[0]
    row = lambda i: (i, 0)
    vec = lambda i: (0, 0)
    gn = SSD_GROUPS * SSD_N
    return pl.pallas_call(
        _sstate_kernel,
        out_shape=[jax.ShapeDtypeStruct(s0.shape, F32), jax.ShapeDtypeStruct((n, SSD_INNER), BF16)],
        grid=(n // nb,),
        in_specs=[pl.BlockSpec((nb, SSD_INNER, SSD_N), lambda i: (i, 0, 0)),
                  pl.BlockSpec((None, SSD_INNER, nb), lambda i: (i, 0, 0)),
                  pl.BlockSpec((None, SSD_INNER, nb), lambda i: (i, 0, 0)),
                  pl.BlockSpec((nb, gn), row), pl.BlockSpec((nb, gn), row),
                  pl.BlockSpec((nb, SSD_INNER), row), pl.BlockSpec((nb, SSD_INNER), row),
                  pl.BlockSpec((1, SSD_INNER), vec), pl.BlockSpec((1, SSD_INNER), vec)],
        out_specs=[pl.BlockSpec((nb, SSD_INNER, SSD_N), lambda i: (i, 0, 0)), pl.BlockSpec((nb, SSD_INNER), row)],
        scratch_shapes=[pltpu.VMEM((nb, SSD_INNER), F32)],
        compiler_params=_cparams(("parallel",)),
        name="sstate",
    )(s0, xq, dq, bm, cm, xs, z, dexp, ng)


def _post_kernel(yl_ref, ys_ref, x_ref, g1_ref, sh2_ref, sc2_ref, wo_ref, l1g_ref, l1b_ref, wrt_ref, brc_ref,
                 x1_ref, v_ref, eid_ref, gw_ref):
    o = _dot(yl_ref[...], wo_ref[0:LRU_W, :]) + _dot(ys_ref[...], wo_ref[LRU_W:LRU_W + SSD_INNER, :])
    x1 = _layer_norm(ALPHA * x_ref[...] + g1_ref[...] * o) * l1g_ref[...] + l1b_ref[...]
    x1_ref[...] = x1
    v = _layer_norm(x1) * (1.0 + sc2_ref[...]) + sh2_ref[...]
    _store_row_tiles(v_ref, v)

    lt = _dot(v.astype(BF16), wrt_ref[...]).T + brc_ref[...]
    tl = lt.shape[1]
    row = lax.broadcasted_iota(I32, (GROUP_SIZE, tl), 0).astype(F32)
    big = float(GROUP_SIZE)
    neg = -jnp.inf
    lg = jnp.where(row < N_GROUPS, lt[0:GROUP_SIZE, :], neg)
    gmax = jnp.max(lg, axis=0, keepdims=True)
    gsel = jnp.min(jnp.where(lg == gmax, row, big), axis=0, keepdims=True)
    pg = 1.0 / jnp.sum(jnp.exp(lg - gmax), axis=0, keepdims=True)
    le = lt[GROUP_SIZE:2 * GROUP_SIZE, :]
    for j in range(1, N_GROUPS):
        le = jnp.where(gsel == j, lt[GROUP_SIZE * (j + 1):GROUP_SIZE * (j + 2), :], le)
    m1 = jnp.max(le, axis=0, keepdims=True)
    i1 = jnp.min(jnp.where(le == m1, row, big), axis=0, keepdims=True)
    rest = jnp.where(row == i1, neg, le)
    m2 = jnp.max(rest, axis=0, keepdims=True)
    i2 = jnp.min(jnp.where(rest == m2, row, big), axis=0, keepdims=True)
    e2 = jnp.exp(m2 - m1)
    den = 1.0 + e2
    eid = jnp.where(row == 0, gsel * GROUP_SIZE + i1, jnp.where(row == 1, gsel * GROUP_SIZE + i2, 0.0))
    eid_ref[...] = eid.astype(I32)
    gw_ref[...] = jnp.where(row == 0, pg * (1.0 / den), jnp.where(row == 1, pg * (e2 / den), 0.0))


def _post_call(yl, ys, x3, g1, sh2, sc2, p, tl):
    nb, L, _ = x3.shape
    nl = L // tl
    row = lambda b, l: (b, l, 0)
    vec = lambda b, l: (0, 0)
    tok = lambda b, l: (b * nl + l, 0)
    tokt = lambda b, l: (0, b * nl + l)
    return pl.pallas_call(
        _post_kernel,
        out_shape=[jax.ShapeDtypeStruct((nb, L, D), F32), jax.ShapeDtypeStruct(_tile_rows(nb * L), F32),
                   jax.ShapeDtypeStruct((SUBLANES, nb * L), I32), jax.ShapeDtypeStruct((SUBLANES, nb * L), F32)],
        grid=(nb, nl),
        in_specs=[pl.BlockSpec((None, tl, LRU_W), row), pl.BlockSpec((None, tl, SSD_INNER), row),
                  pl.BlockSpec((None, tl, D), row), _mod_spec(g1, tl), _mod_spec(sh2, tl), _mod_spec(sc2, tl),
                  pl.BlockSpec((LRU_W + SSD_INNER, D), vec), pl.BlockSpec((1, D), vec), pl.BlockSpec((1, D), vec),
                  pl.BlockSpec((D, LANES), vec), pl.BlockSpec((LANES, 1), vec)],
        out_specs=[pl.BlockSpec((None, tl, D), row), pl.BlockSpec(_tile_rows(tl), tok),
                   pl.BlockSpec((SUBLANES, tl), tokt), pl.BlockSpec((SUBLANES, tl), tokt)],
        compiler_params=_cparams(("parallel", "arbitrary")),
        name="post",
    )(yl, ys, x3, g1, sh2, sc2, p["wo"], p["l1g"], p["l1b"], p["wrt"], p["brc"])


def _route_kernel(eidp_ref, eids_ref, dest_ref, cnt_ref, run, poff, *, p_tiles):
    ph = pl.program_id(0)
    t = pl.program_id(1)
    n = ROUTE_TILE
    eid = jnp.where(t < p_tiles, eidp_ref[...], eids_ref[...])
    rowi = lax.broadcasted_iota(I32, (LANES, n), 0)
    oh0 = rowi == eid[0:1, :]
    oh1 = rowi == eid[1:2, :]
    oh = oh0.astype(F32) + oh1.astype(F32)
    tile_cnt = jnp.sum(oh, axis=1, keepdims=True)

    @pl.when((ph == 0) & (t == 0))
    def _():
        run[...] = jnp.zeros((LANES, n), F32)

    @pl.when(ph == 0)
    def _():
        run[...] = run[...] + tile_cnt

    @pl.when((ph == 1) & (t == 0))
    def _():
        counts = run[...]
        cnt_ref[...] = counts
        ci = counts.astype(I32)
        q = jnp.floor(counts * (1.0 / MOE_BM)).astype(I32)
        rem = ci - q * MOE_BM
        q = q + jnp.where(rem >= MOE_BM, 1, 0) - jnp.where(rem < 0, 1, 0)
        nblk = q + jnp.where(ci - q * MOE_BM > 0, 1, 0)
        r = lax.broadcasted_iota(I32, (LANES, LANES), 0)
        c = lax.broadcasted_iota(I32, (LANES, LANES), 1)
        lower = (r > c).astype(BF16)
        poff[...] = _dot(lower, nblk.astype(F32).astype(BF16)) * float(MOE_BM)
        run[...] = jnp.zeros((LANES, n), F32)

    @pl.when(ph == 1)
    def _():
        r = lax.broadcasted_iota(I32, (n, n), 0)
        c = lax.broadcasted_iota(I32, (n, n), 1)
        before = (r < c).astype(BF16)
        slot = _dot(oh.astype(BF16), before) + run[...] + poff[...]
        d0 = jnp.sum(jnp.where(oh0, slot, 0.0), axis=0, keepdims=True)
        d1 = jnp.sum(jnp.where(oh1, slot, 0.0), axis=0, keepdims=True)
        row = lax.broadcasted_iota(I32, (SUBLANES, n), 0)
        dest_ref[...] = jnp.where(row == 0, d0, jnp.where(row == 1, d1, 0.0)).astype(I32)
        run[...] = run[...] + tile_cnt


def _route_call(eid_p, eid_s):
    n = ROUTE_TILE
    p_tiles = eid_p.shape[1] // n
    s_tiles = eid_s.shape[1] // n
    t_all = (p_tiles + s_tiles) * n
    return pl.pallas_call(
        functools.partial(_route_kernel, p_tiles=p_tiles),
        out_shape=[jax.ShapeDtypeStruct((SUBLANES, t_all), I32), jax.ShapeDtypeStruct((LANES, n), F32)],
        grid=(2, p_tiles + s_tiles),
        in_specs=[pl.BlockSpec((SUBLANES, n), lambda ph, t: (0, jnp.minimum(t, p_tiles - 1))),
                  pl.BlockSpec((SUBLANES, n), lambda ph, t: (0, jnp.maximum(t - p_tiles, 0)))],
        out_specs=[pl.BlockSpec((SUBLANES, n), lambda ph, t: (0, t * ph)),
                   pl.BlockSpec((LANES, n), lambda ph, t: (0, 0))],
        scratch_shapes=[pltpu.VMEM((LANES, n), F32), pltpu.VMEM((LANES, n), F32)],
        compiler_params=_cparams(("arbitrary", "arbitrary")),
        name="route",
    )(eid_p, eid_s)


def _row_copy(src, src_row, dst, dst_row, sem):
    s0 = pl.multiple_of(src_row * SUBLANES, SUBLANES)
    d0 = pl.multiple_of(dst_row * SUBLANES, SUBLANES)
    return pltpu.make_async_copy(src.at[pl.ds(s0, SUBLANES), :], dst.at[pl.ds(d0, SUBLANES), :], sem)


def _scatter_rows(dest_ref, v_ref, xpad_ref, sem):
    def start(r, carry):
        for k in range(2):
            _row_copy(v_ref, r, xpad_ref, dest_ref[k, r], sem).start(priority=k)
        return carry

    def wait(r, carry):
        for k in range(2):
            _row_copy(v_ref, r, xpad_ref, dest_ref[k, r], sem).wait()
        return carry

    lax.fori_loop(0, TOK_TILE, start, 0, unroll=DMA_UNROLL)
    lax.fori_loop(0, TOK_TILE, wait, 0, unroll=DMA_UNROLL)


def _dispatch_kernel(zflag_ref, dest_ref, vp_ref, vs_ref, xpad_ref, zbuf, sem, zsem, *, p_tiles, n_blocks):
    t = pl.program_id(0)
    blk_rows = _tile_rows(MOE_BM)[0]

    @pl.when(t == 0)
    def _():
        zbuf[...] = jnp.zeros(zbuf.shape, F32)

        def zero_block(go):
            def body(b, carry):
                @pl.when(zflag_ref[b] != 0)
                def _():
                    r0 = pl.multiple_of(b * blk_rows, blk_rows)
                    cp = pltpu.make_async_copy(zbuf, xpad_ref.at[pl.ds(r0, blk_rows), :], zsem)
                    if go:
                        cp.start()
                    else:
                        cp.wait()
                return carry
            lax.fori_loop(0, n_blocks, body, 0)

        zero_block(True)
        zero_block(False)

    @pl.when(t < p_tiles)
    def _():
        _scatter_rows(dest_ref, vp_ref, xpad_ref, sem)

    @pl.when(t >= p_tiles)
    def _():
        _scatter_rows(dest_ref, vs_ref, xpad_ref, sem)


def _dispatch_call(zflag, dest, v_p, v_s, n_blocks):
    p_tiles = v_p.shape[0] // _tile_rows(TOK_TILE)[0]
    s_tiles = v_s.shape[0] // _tile_rows(TOK_TILE)[0]
    return pl.pallas_call(
        functools.partial(_dispatch_kernel, p_tiles=p_tiles, n_blocks=n_blocks),
        out_shape=jax.ShapeDtypeStruct(_tile_rows(n_blocks * MOE_BM), F32),
        grid_spec=pltpu.PrefetchScalarGridSpec(
            num_scalar_prefetch=1,
            grid=(p_tiles + s_tiles,),
            in_specs=[pl.BlockSpec((SUBLANES, TOK_TILE), lambda t, z: (0, t), memory_space=pltpu.SMEM),
                      pl.BlockSpec(_tile_rows(TOK_TILE), lambda t, z: (jnp.minimum(t, p_tiles - 1), 0)),
                      pl.BlockSpec(_tile_rows(TOK_TILE), lambda t, z: (jnp.maximum(t - p_tiles, 0), 0))],
            out_specs=pl.BlockSpec(memory_space=pl.ANY),
            scratch_shapes=[pltpu.VMEM(_tile_rows(MOE_BM), F32), pltpu.SemaphoreType.DMA, pltpu.SemaphoreType.DMA]),
        compiler_params=_cparams(("arbitrary",)),
        name="dispatch",
    )(zflag, dest, v_p, v_s)


def _expert_kernel(be_ref, nb_ref, nxt_ref, slot_ref, x_ref, wg_hbm, wu_hbm, wd_hbm, o_ref,
                   wgf, wuf, wdf, wgb, wub, wdb, sems):
    i = pl.program_id(0)

    def weights(e, slot, go):
        for m, (hbm, buf) in enumerate(((wg_hbm, wgf), (wu_hbm, wuf), (wd_hbm, wdf))):
            cp = pltpu.make_async_copy(hbm.at[e], buf.at[slot], sems.at[slot, m])
            if go:
                cp.start()
            else:
                cp.wait()

    @pl.when(i < nb_ref[0])
    def _():
        e = be_ref[i]
        slot = slot_ref[e]

        @pl.when(i == 0)
        def _():
            weights(e, slot, True)

        @pl.when((i == 0) | (e != be_ref[jnp.maximum(i - 1, 0)]))
        def _():
            weights(e, slot, False)
            nxt = nxt_ref[e]

            @pl.when(nxt >= 0)
            def _():
                weights(nxt, 1 - slot, True)

            wgb[...] = wgf[slot].astype(BF16)
            wub[...] = wuf[slot].astype(BF16)
            wdb[...] = wdf[slot].astype(BF16)

        x = _load_row_tiles(x_ref).astype(BF16)
        h = _silu(_dot(x, wgb[...])) * _dot(x, wub[...])
        _store_row_tiles(o_ref, _dot(h.astype(BF16), wdb[...]))

    @pl.when(i >= nb_ref[0])
    def _():
        o_ref[...] = jnp.zeros(o_ref.shape, F32)


def _expert_call(blk_e, nblk, nxt_e, slot_e, xpad, w_gate, w_up, w_down):
    n_rows = xpad.shape[0] // SUBLANES
    blk = lambda i, be, nb, nx, sl: (jnp.minimum(i, nb[0] - 1), 0)
    hbm = pl.BlockSpec(memory_space=pl.ANY)
    return pl.pallas_call(
        _expert_kernel,
        out_shape=jax.ShapeDtypeStruct(_tile_rows(n_rows), F32),
        grid_spec=pltpu.PrefetchScalarGridSpec(
            num_scalar_prefetch=4,
            grid=(n_rows // MOE_BM,),
            in_specs=[pl.BlockSpec(_tile_rows(MOE_BM), blk), hbm, hbm, hbm],
            out_specs=pl.BlockSpec(_tile_rows(MOE_BM), lambda i, be, nb, nx, sl: (i, 0)),
            scratch_shapes=[pltpu.VMEM((2, D, D_FF), F32), pltpu.VMEM((2, D, D_FF), F32), pltpu.VMEM((2, D_FF, D), F32),
                            pltpu.VMEM((D, D_FF), BF16), pltpu.VMEM((D, D_FF), BF16), pltpu.VMEM((D_FF, D), BF16),
                            pltpu.SemaphoreType.DMA((2, 3))]),
        compiler_params=_cparams(("arbitrary",)),
        name="expert",
    )(blk_e, nblk, nxt_e, slot_e, xpad, w_gate, w_up, w_down)


def _combine_kernel(dest_ref, gw0_ref, gw1_ref, x1_ref, g2_ref, l2g_ref, l2b_ref, ypad_ref, y_ref, ybuf, sems,
                    *, tok_off, k_stride, tl):
    i = pl.program_id(0)

    def gather(tile, slot, go):
        base = tok_off + tile * tl

        def body(r, carry):
            for k in range(2):
                cp = _row_copy(ypad_ref, dest_ref[k * k_stride + base + r], ybuf.at[slot, k], r, sems.at[slot])
                if go:
                    cp.start(priority=k)
                else:
                    cp.wait()
            return carry

        lax.fori_loop(0, tl, body, 0, unroll=DMA_UNROLL)

    @pl.when(i == 0)
    def _():
        gather(0, 0, True)

    @pl.when(i + 1 < pl.num_programs(0))
    def _():
        gather(i + 1, (i + 1) % 2, True)

    slot = i % 2
    gather(i, slot, False)
    f = _load_row_tiles(ybuf.at[slot, 0]) * gw0_ref[...] + _load_row_tiles(ybuf.at[slot, 1]) * gw1_ref[...]
    y_ref[...] = _layer_norm(ALPHA * x1_ref[...] + g2_ref[...] * f) * l2g_ref[...] + l2b_ref[...]


def _combine_call(dest, gw0, gw1, x1, g2, l2g, l2b, ypad, tok_off, tl):
    nb, L, _ = x1.shape
    nl = L // tl
    row = lambda i, d: (i // nl, i % nl, 0)
    vec = lambda i, d: (0, 0)
    tok = lambda i, d: (i, 0)
    if g2.shape[1] == 1:
        g2_spec = pl.BlockSpec((None, 1, D), lambda i, d: (i // nl, 0, 0))
    else:
        g2_spec = pl.BlockSpec((None, tl, D), row)
    return pl.pallas_call(
        functools.partial(_combine_kernel, tok_off=tok_off, k_stride=dest.shape[1], tl=tl),
        out_shape=jax.ShapeDtypeStruct((nb, L, D), F32),
        grid_spec=pltpu.PrefetchScalarGridSpec(
            num_scalar_prefetch=1,
            grid=(nb * nl,),
            in_specs=[pl.BlockSpec((tl, 1), tok), pl.BlockSpec((tl, 1), tok),
                      pl.BlockSpec((None, tl, D), row), g2_spec,
                      pl.BlockSpec((1, D), vec), pl.BlockSpec((1, D), vec),
                      pl.BlockSpec(memory_space=pl.ANY)],
            out_specs=pl.BlockSpec((None, tl, D), row),
            scratch_shapes=[pltpu.VMEM((2, 2) + _tile_rows(tl), F32), pltpu.SemaphoreType.DMA((2,))]),
        compiler_params=_cparams(("arbitrary",)),
        name="combine",
    )(dest[:2].reshape(-1), gw0, gw1, x1, g2, l2g, l2b, ypad)


def _block_diag(w):
    nh, blk, _ = w.shape
    w4 = w.reshape(nh // LRU_PACK, LRU_PACK, blk, blk)
    eye = jnp.eye(LRU_PACK, dtype=w.dtype)
    out = jnp.einsum("gaij,ab->gaibj", w4, eye)
    return out.reshape(nh // LRU_PACK, LRU_PACK * blk, LRU_PACK * blk).astype(BF16)


def _row(v):
    return v.reshape(1, -1).astype(F32)


def _pad_lanes(v):
    return jnp.pad(v.reshape(1, -1).astype(F32), ((0, 0), (0, LANES - v.shape[-1])))


def kernel(x_prompt, x_sample, c_prompt, c_sample, state_lru_conv, state_lru_h, state_ssd_conv, state_ssd, w_ada, b_ada, w_in, lru_conv_w, lru_conv_b, lru_wa, lru_ba, lru_wx, lru_bx, lru_lambda, lru_norm_g, ssd_conv_w, ssd_conv_b, ssd_dt_bias, ssd_a_log, ssd_d, ssd_norm_g, w_out, ln1_g, ln1_b, w_rg, b_rg, w_re, b_re, w_gate, w_up, w_down, ln2_g, ln2_b):
    assert w_ada.shape[0] == DEPTH == 1
    nbp, seq, _ = x_prompt.shape
    nbs = x_sample.shape[0]
    t_p = nbp * seq
    t_all = t_p + nbs
    assert x_sample.shape[1] == 1 and t_p % ROUTE_TILE == 0 and nbs % TOK_TILE == 0

    c_rows = -(-(nbp + nbs) // 16) * 16
    c_all = jnp.pad(jnp.concatenate([c_prompt, c_sample], axis=0), ((0, c_rows - nbp - nbs), (0, 0)))
    w_in0 = w_in[0]
    o_z, o_xbc, o_dt = 2 * LRU_W, 2 * LRU_W + SSD_INNER, 2 * LRU_W + SSD_INNER + SSD_CONV_DIM
    w_dt = w_in0[:, o_dt:]
    ws_p = [w_in0[:, :o_z].astype(BF16), w_in0[:, o_z:o_xbc].astype(BF16), w_in0[:, o_xbc:o_dt].astype(BF16),
            jnp.pad(w_dt, ((0, 0), (0, LANES - SSD_HEADS))).astype(BF16)]
    ws_s = ws_p[:3] + [jnp.repeat(w_dt, SSD_P, axis=1).astype(BF16)]
    lp = dict(cw=lru_conv_w[0], cb=_row(lru_conv_b[0]), wa=_block_diag(lru_wa[0]), wx=_block_diag(lru_wx[0]),
              ba=_row(lru_ba[0]), bx=_row(lru_bx[0]), lam=_row(lru_lambda[0]), ng=_row(lru_norm_g[0]))
    sp = dict(cw=ssd_conv_w[0], cb=_row(ssd_conv_b[0]), dtb=_pad_lanes(ssd_dt_bias[0]), alog=_pad_lanes(ssd_a_log[0]),
              dexp=_row(jnp.repeat(ssd_d[0], SSD_P)), ng=_row(ssd_norm_g[0]),
              dtbe=_row(jnp.repeat(ssd_dt_bias[0], SSD_P)), aloge=_row(jnp.repeat(ssd_a_log[0], SSD_P)))
    wrt = jnp.zeros((D, LANES), F32).at[:, 0:N_GROUPS].set(w_rg[0]).at[:, GROUP_SIZE:ROUTE_ROWS].set(w_re[0]).astype(BF16)
    brc = jnp.zeros((LANES, 1), F32).at[0:N_GROUPS, 0].set(b_rg[0]).at[GROUP_SIZE:ROUTE_ROWS, 0].set(b_re[0])
    pp = dict(wo=w_out[0].astype(BF16), l1g=_row(ln1_g[0]), l1b=_row(ln1_b[0]), wrt=wrt, brc=brc)

    mod = _ada_call(c_all, w_ada[0], _row(b_ada[0])).reshape(c_rows, N_MOD, D)
    mp = [mod[:nbp, k].reshape(nbp, 1, D) for k in range(N_MOD)]
    ms = [mod[nbp:nbp + nbs, k].reshape(1, nbs, D) for k in range(N_MOD)]

    xg, z, xbc, dtr = _proj_call(x_prompt, mp[0], mp[1], ws_p, SEQ_TILE)
    yl, p_lru_conv, p_lru_h = _lru_call(xg, lp, SEQ_TILE)
    ys, p_ssd_conv, p_ssd = _ssd_call(xbc, z, dtr, sp)
    x1_p, v_p, eid_p, gw_p = _post_call(yl, ys, x_prompt, mp[2], mp[3], mp[4], pp, SEQ_TILE)

    xs3 = x_sample.reshape(1, nbs, D)
    xg_s, z_s, xbc_s, dte_s = _proj_call(xs3, ms[0], ms[1], ws_s, nbs)
    xg_s, z_s, xbc_s, dte_s = xg_s[0], z_s[0], xbc_s[0], dte_s[0]
    yl_s, h_s, xs_s, bm_s, cm_s, xdtt, dect = _srow_call(
        xg_s, xbc_s, dte_s, state_lru_conv[0].reshape(nbs, -1), state_lru_h[0],
        state_ssd_conv[0].reshape(nbs, -1), lp, sp)
    sb = SUBLANES
    regroup = lambda a: a.reshape(SSD_INNER, nbs // sb, sb).transpose(1, 0, 2)
    s_new, ys_s = _sstate_call(state_ssd[0].reshape(nbs, SSD_INNER, SSD_N), regroup(xdtt), regroup(dect),
                               bm_s, cm_s, xs_s, z_s, sp["dexp"], sp["ng"], sb)
    x1_s, v_s, eid_s, gw_s = _post_call(yl_s[None], ys_s[None], xs3, ms[2], ms[3], ms[4], pp, nbs)

    s_pad = -nbs % ROUTE_TILE
    dest, counts = _route_call(eid_p, jnp.pad(eid_s, ((0, 0), (0, s_pad)), constant_values=-1))
    counts = counts[:N_EXPERTS, 0].astype(I32)
    n_blocks = -(-(2 * t_all) // MOE_BM) + N_EXPERTS
    pend = jnp.cumsum(((counts + MOE_BM - 1) // MOE_BM) * MOE_BM)
    blk_start = jnp.arange(n_blocks, dtype=I32) * MOE_BM
    blk_e = jnp.minimum(jnp.sum((pend[None, :] <= blk_start[:, None]).astype(I32), axis=1), N_EXPERTS - 1)
    nblk = (pend[-1:] // MOE_BM).astype(I32)
    blk_ids = jnp.arange(n_blocks, dtype=I32)
    last_of_expert = jnp.any(((pend // MOE_BM - 1)[None, :] == blk_ids[:, None]) & (counts > 0)[None, :], axis=1)
    zflag = (last_of_expert | (blk_ids >= nblk[0])).astype(I32)
    used = counts > 0
    eids = jnp.arange(N_EXPERTS, dtype=I32)
    later_used = (eids[None, :] > eids[:, None]) & used[None, :]
    nxt_e = jnp.min(jnp.where(later_used, eids[None, :], N_EXPERTS), axis=1)
    nxt_e = jnp.where(nxt_e == N_EXPERTS, -1, nxt_e).astype(I32)
    slot_e = ((jnp.cumsum(used.astype(I32)) - used.astype(I32)) % 2).astype(I32)
    xpad = _dispatch_call(zflag, dest, v_p, v_s, n_blocks)
    ypad = _expert_call(blk_e, nblk, nxt_e, slot_e, xpad, w_gate[0], w_up[0], w_down[0])
    l2g, l2b = _row(ln2_g[0]), _row(ln2_b[0])
    y_p = _combine_call(dest, gw_p[0].reshape(t_p, 1), gw_p[1].reshape(t_p, 1), x1_p, mp[5], l2g, l2b, ypad, 0,
                        2 * TOK_TILE)
    y_s = _combine_call(dest, gw_s[0].reshape(nbs, 1), gw_s[1].reshape(nbs, 1), x1_s, ms[5], l2g, l2b, ypad, t_p,
                        TOK_TILE)

    s_lru_conv = jnp.concatenate([state_lru_conv[0][:, 1:], xg_s[:, None, :LRU_W]], axis=1)
    s_ssd_conv = jnp.concatenate([state_ssd_conv[0][:, 1:], xbc_s[:, None, :]], axis=1)
    return (y_p, y_s.reshape(nbs, 1, D),
            p_lru_conv[None], p_lru_h.reshape(1, nbp, LRU_W), p_ssd_conv[None],
            p_ssd.reshape(1, nbp, SSD_HEADS, SSD_P, SSD_N),
            s_lru_conv[None], h_s[None], s_ssd_conv[None],
            s_new.reshape(1, nbs, SSD_HEADS, SSD_P, SSD_N))
```

```python
import functools
import math

import jax
import jax.numpy as jnp
from jax import lax
from jax.experimental import pallas as pl
from jax.experimental.pallas import tpu as pltpu

F32 = jnp.float32
BF16 = jnp.bfloat16
I32 = jnp.int32
HIGHEST = lax.Precision.HIGHEST

D = 1024
DEPTH = 1
CONV_W = 4
LRU_W = D
LRU_HEADS = 16
LRU_C = 8.0
LRU_PACK = 4
LRU_PACK_W = LRU_PACK * (LRU_W // LRU_HEADS)
SSD_INNER = D
SSD_HEADS = 16
SSD_P = SSD_INNER // SSD_HEADS
SSD_GROUPS = 2
SSD_N = 128
SSD_Q = 128
SSD_CONV_DIM = SSD_INNER + 2 * SSD_GROUPS * SSD_N
N_GROUPS = 4
GROUP_SIZE = 8
N_EXPERTS = N_GROUPS * GROUP_SIZE
D_FF = D // 2
N_MOD = 6
LN_EPS = 1e-5
RMS_EPS = 1e-6
ALPHA = (2.0 * DEPTH) ** 0.25

LANES = 128
SUBLANES = 8
VMEM_LIMIT = 56 * 1024 * 1024
SEQ_TILE = 512
TOK_TILE = 128
ROUTE_TILE = 512
DMA_UNROLL = 8
MOE_BM = 384
ROUTE_ROWS = 40

NT_DIMS = (((1,), (1,)), ((), ()))


def _tile_rows(rows):
    return (rows * SUBLANES, LANES)


def _cparams(sem):
    return pltpu.CompilerParams(dimension_semantics=sem, vmem_limit_bytes=VMEM_LIMIT)


def _sigmoid(x):
    return 0.5 * (jnp.tanh(0.5 * x) + 1.0)


def _silu(x):
    return x * _sigmoid(x)


def _softplus(x):
    return jnp.maximum(x, 0.0) + jnp.log1p(jnp.exp(-jnp.abs(x)))


def _gelu_tanh(x):
    return 0.5 * x * (1.0 + jnp.tanh(math.sqrt(2.0 / math.pi) * (x + 0.044715 * (x * x * x))))


def _layer_norm(x):
    mu = jnp.mean(x, axis=-1, keepdims=True)
    xc = x - mu
    var = jnp.mean(xc * xc, axis=-1, keepdims=True)
    return xc * lax.rsqrt(var + LN_EPS)


def _rms_norm(x, g):
    return x * lax.rsqrt(jnp.mean(x * x, axis=-1, keepdims=True) + RMS_EPS) * g


def _dot(a, b, **kw):
    return jnp.dot(a, b, preferred_element_type=F32, **kw)


def _dot_nt(a, b, **kw):
    return lax.dot_general(a, b, NT_DIMS, preferred_element_type=F32, **kw)


def _store_row_tiles(ref, val):
    rows = val.shape[0]
    for j in range(SUBLANES):
        ref[pl.ds(j, rows, stride=SUBLANES), :] = val[:, j * LANES:(j + 1) * LANES]


def _load_row_tiles(ref):
    rows = ref.shape[0] // SUBLANES
    return jnp.concatenate([ref[pl.ds(j, rows, stride=SUBLANES), :] for j in range(SUBLANES)], axis=1)


def _lru_gates(xc, xb, wa, wx, ba, bx, sp):
    r = _sigmoid(_dot(xb, wa) + ba)
    i = _sigmoid(_dot(xb, wx) + bx)
    log_a = (-LRU_C) * r * sp
    a = jnp.exp(log_a)
    mult = jnp.sqrt(jnp.tanh(-log_a) * (a * a + 1.0))
    return a, mult * (i * xc)


def _ada_kernel(c_ref, w_ref, b_ref, o_ref):
    s = _silu(c_ref[...]).astype(BF16)
    o_ref[...] = _dot(s, w_ref[...].astype(BF16)) + b_ref[...]


def _ada_call(c_all, w_ada, b_ada):
    rows = c_all.shape[0]
    tn = 512
    per_mod = D // tn
    return pl.pallas_call(
        _ada_kernel,
        out_shape=jax.ShapeDtypeStruct((N_MOD, rows, D), F32),
        grid=(N_MOD * per_mod,),
        in_specs=[pl.BlockSpec((rows, D), lambda j: (0, 0)),
                  pl.BlockSpec((D, tn), lambda j: (0, j)),
                  pl.BlockSpec((1, tn), lambda j: (0, j))],
        out_specs=pl.BlockSpec((None, rows, tn), lambda j: (j // per_mod, 0, j % per_mod)),
        compiler_params=_cparams(("arbitrary",)),
        name="ada",
    )(c_all, w_ada, b_ada)


def _wcast_kernel(w_ref, *o_refs, bounds):
    for o_ref, (lo, hi) in zip(o_refs, bounds):
        o_ref[...] = w_ref[:, lo:hi].astype(BF16)


def _wcast_call(w, bounds, tr):
    rows, cols = w.shape
    assert all(lo % LANES == 0 for lo, _ in bounds)
    return pl.pallas_call(
        functools.partial(_wcast_kernel, bounds=bounds),
        out_shape=[jax.ShapeDtypeStruct((rows, hi - lo), BF16) for lo, hi in bounds],
        grid=(rows // tr,),
        in_specs=[pl.BlockSpec((tr, cols), lambda i: (i, 0))],
        out_specs=[pl.BlockSpec((tr, hi - lo), lambda i: (i, 0)) for lo, hi in bounds],
        compiler_params=_cparams(("arbitrary",)),
        name="wcast",
    )(w)


class _Mod:
    def __init__(self, table, row0, per_batch):
        self.table, self.row0, self.per_batch = table, row0, per_batch

    def spec(self, k, tl, batch_of, tile_of):
        if self.per_batch:
            blk = self.row0 // SUBLANES
            return pl.BlockSpec((None, SUBLANES, D), lambda *g: (k, blk, 0))
        blk = self.row0 // tl
        return pl.BlockSpec((None, tl, D), lambda *g: (k, blk + tile_of(*g), 0))


def _mod_rows(ref, per_batch, b):
    return ref[pl.ds(b, 1), :] if per_batch else ref[...]


def _proj_kernel(x_ref, sh_ref, sc_ref, w1_ref, w2_ref, w3_ref, w4_ref, o1_ref, o2_ref, o3_ref, o4_ref, *, per_batch):
    b = pl.program_id(0)
    u = _layer_norm(x_ref[...]) * (1.0 + _mod_rows(sc_ref, per_batch, b)) + _mod_rows(sh_ref, per_batch, b)
    ub = u.astype(BF16)
    o1_ref[...] = _dot(ub, w1_ref[...])
    o2_ref[...] = _dot(ub, w2_ref[...])
    o3_ref[...] = _dot(ub, w3_ref[...])
    o4_ref[...] = _dot(ub, w4_ref[...])


def _proj_call(x3, mod, ws, tl):
    nb, L, _ = x3.shape
    widths = [w.shape[1] for w in ws]
    row = lambda b, l: (b, l, 0)
    full = lambda b, l: (0, 0)
    bof, tof = (lambda b, l: b), (lambda b, l: l)
    return pl.pallas_call(
        functools.partial(_proj_kernel, per_batch=mod.per_batch),
        out_shape=[jax.ShapeDtypeStruct((nb, L, n), F32) for n in widths],
        grid=(nb, L // tl),
        in_specs=[pl.BlockSpec((None, tl, D), row), mod.spec(0, tl, bof, tof), mod.spec(1, tl, bof, tof)]
                 + [pl.BlockSpec((D, n), full) for n in widths],
        out_specs=[pl.BlockSpec((None, tl, n), row) for n in widths],
        compiler_params=_cparams(("parallel", "arbitrary")),
        name="proj",
    )(x3, mod.table, mod.table, *ws)


def _scan_rows(a_scr, u_scr, h0, n_rows):
    sub = lax.broadcasted_iota(I32, (SUBLANES, D), 0)
    shifts = (1, 2, 4)
    masks = [sub >= d for d in shifts]

    def body(j, hp):
        r0 = pl.multiple_of(j * SUBLANES, SUBLANES)
        a = a_scr[pl.ds(r0, SUBLANES), :]
        u = u_scr[pl.ds(r0, SUBLANES), :]
        for d, m in zip(shifts, masks):
            ash = pltpu.roll(a, d, 0)
            ush = pltpu.roll(u, d, 0)
            u = jnp.where(m, u + a * ush, u)
            a = jnp.where(m, a * ash, a)
        h = u + a * hp
        u_scr[pl.ds(r0, SUBLANES), :] = h
        return h[SUBLANES - 1:SUBLANES, :]

    return lax.fori_loop(0, n_rows // SUBLANES, body, h0)


def _lru_kernel(x_ref, g_ref, cw_ref, cb_ref, wa_ref, wx_ref, ba_ref, bx_ref, lam_ref, ng_ref,
                y_ref, conv_ref, h_ref, xpad, xc_scr, a_scr, u_scr, hc_scr, *, tl):
    @pl.when(pl.program_id(1) == 0)
    def _():
        xpad[0:SUBLANES, :] = jnp.zeros((SUBLANES, D), F32)
        hc_scr[...] = jnp.zeros((1, D), F32)

    xpad[SUBLANES:SUBLANES + tl, :] = x_ref[...]
    off = SUBLANES - (CONV_W - 1)
    acc = xpad[off:off + tl, :] * cw_ref[0:1, :]
    for k in range(1, CONV_W):
        acc = acc + xpad[off + k:off + k + tl, :] * cw_ref[k:k + 1, :]
    xc_scr[...] = acc + cb_ref[...]
    conv_ref[...] = xpad[SUBLANES + tl - (CONV_W - 1):SUBLANES + tl, :]
    xpad[0:SUBLANES, :] = xpad[tl:tl + SUBLANES, :]

    sp = _softplus(-lam_ref[...])
    for j in range(LRU_W // LRU_PACK_W):
        cs = slice(j * LRU_PACK_W, (j + 1) * LRU_PACK_W)
        xc = xc_scr[:, cs]
        a, u = _lru_gates(xc, xc.astype(BF16), wa_ref[j], wx_ref[j], ba_ref[:, cs], bx_ref[:, cs], sp[:, cs])
        a_scr[:, cs] = a
        u_scr[:, cs] = u

    h_last = _scan_rows(a_scr, u_scr, hc_scr[...], tl)
    hc_scr[...] = h_last
    h_ref[...] = h_last
    y = u_scr[...] * _gelu_tanh(g_ref[...])
    y_ref[...] = _rms_norm(y, ng_ref[...]).astype(BF16)


def _lru_call(xg, p, tl):
    nb, L, _ = xg.shape
    vec = lambda b, l: (0, 0)
    blk = lambda b, l: (0, 0, 0)
    return pl.pallas_call(
        functools.partial(_lru_kernel, tl=tl),
        out_shape=[jax.ShapeDtypeStruct((nb, L, LRU_W), BF16),
                   jax.ShapeDtypeStruct((nb, CONV_W - 1, LRU_W), F32),
                   jax.ShapeDtypeStruct((nb, 1, LRU_W), F32)],
        grid=(nb, L // tl),
        in_specs=[pl.BlockSpec((None, tl, LRU_W), lambda b, l: (b, l, 0)),
                  pl.BlockSpec((None, tl, LRU_W), lambda b, l: (b, l, 1)),
                  pl.BlockSpec((CONV_W, LRU_W), vec), pl.BlockSpec((1, LRU_W), vec),
                  pl.BlockSpec((LRU_W // LRU_PACK_W, LRU_PACK_W, LRU_PACK_W), blk),
                  pl.BlockSpec((LRU_W // LRU_PACK_W, LRU_PACK_W, LRU_PACK_W), blk),
                  pl.BlockSpec((1, LRU_W), vec), pl.BlockSpec((1, LRU_W), vec),
                  pl.BlockSpec((1, LRU_W), vec), pl.BlockSpec((1, LRU_W), vec)],
        out_specs=[pl.BlockSpec((None, tl, LRU_W), lambda b, l: (b, l, 0)),
                   pl.BlockSpec((None, CONV_W - 1, LRU_W), lambda b, l: (b, 0, 0)),
                   pl.BlockSpec((None, 1, LRU_W), lambda b, l: (b, 0, 0))],
        scratch_shapes=[pltpu.VMEM((tl + SUBLANES, LRU_W), F32), pltpu.VMEM((tl, LRU_W), F32),
                        pltpu.VMEM((tl, LRU_W), F32), pltpu.VMEM((tl, LRU_W), F32),
                        pltpu.VMEM((1, LRU_W), F32)],
        compiler_params=_cparams(("parallel", "arbitrary")),
        name="lru",
    )(xg, xg, p["cw"], p["cb"], p["wa"], p["wx"], p["ba"], p["bx"], p["lam"], p["ng"])


def _ssd_kernel(xbc_ref, z_ref, dt_ref, cw_ref, cb_ref, dtb_ref, alog_ref, dexp_ref, ng_ref,
                y_ref, conv_ref, st_ref, xpad, xc_scr, st_scr, y_scr, ea_scr, ew_scr):
    q = SSD_Q

    @pl.when(pl.program_id(1) == 0)
    def _():
        xpad[0:SUBLANES, :] = jnp.zeros((SUBLANES, SSD_CONV_DIM), F32)
        st_scr[...] = jnp.zeros((SSD_N, SSD_INNER), F32)

    xpad[SUBLANES:SUBLANES + q, :] = xbc_ref[...]
    off = SUBLANES - (CONV_W - 1)
    acc = xpad[off:off + q, :] * cw_ref[0:1, :]
    for k in range(1, CONV_W):
        acc = acc + xpad[off + k:off + k + q, :] * cw_ref[k:k + 1, :]
    xc_scr[...] = _silu(acc + cb_ref[...])
    conv_ref[...] = xpad[SUBLANES + q - (CONV_W - 1):SUBLANES + q, :]
    xpad[0:SUBLANES, :] = xpad[q:q + SUBLANES, :]

    dt = _softplus(dt_ref[...] + dtb_ref[...])
    da = dt * (-jnp.exp(alog_ref[...]))
    ri = lax.broadcasted_iota(I32, (q, q), 0)
    ci = lax.broadcasted_iota(I32, (q, q), 1)
    causal = ri >= ci
    acs = _dot(causal.astype(F32), da, precision=HIGHEST)
    acs_t = acs.T
    dt_t = dt.T
    w_end = dt * jnp.exp(acs[q - 1:q, :] - acs)

    half = SSD_INNER // SSD_GROUPS
    hpg = SSD_HEADS // SSD_GROUPS
    cgb, cbs = [], []
    for g in range(SSD_GROUPS):
        bgb = xc_scr[:, SSD_INNER + g * SSD_N:SSD_INNER + (g + 1) * SSD_N].astype(BF16)
        cgb.append(xc_scr[:, SSD_INNER + (SSD_GROUPS + g) * SSD_N:SSD_INNER + (SSD_GROUPS + g + 1) * SSD_N].astype(BF16))
        cbs.append(_dot_nt(cgb[g], bgb))

    low = lax.broadcasted_iota(I32, (q, LANES), 1) < SSD_P
    for k in range(SSD_HEADS // 2):
        cs = slice(k * LANES, (k + 1) * LANES)
        cb = cbs[(2 * k) // hpg]
        m, colb, wb = [], [], []
        for h in (2 * k, 2 * k + 1):
            colb.append(jnp.broadcast_to(acs[:, h:h + 1], (q, LANES)))
            wb.append(jnp.broadcast_to(w_end[:, h:h + 1], (q, LANES)))
            seg = colb[-1] - acs_t[h:h + 1, :]
            decay = jnp.where(causal, jnp.exp(jnp.where(causal, seg, 0.0)), 0.0)
            m.append((cb * (decay * dt_t[h:h + 1, :])).astype(BF16))
        ea_scr[:, cs] = jnp.where(low, colb[0], colb[1])
        ew_scr[:, cs] = jnp.where(low, wb[0], wb[1])
        xk = xc_scr[:, cs]
        zero = jnp.zeros_like(xk)
        rhs = jnp.concatenate([jnp.where(low, xk, zero), jnp.where(low, zero, xk)], axis=0).astype(BF16)
        y_scr[:, cs] = _dot(jnp.concatenate(m, axis=1), rhs)

    x = xc_scr[:, 0:SSD_INNER]
    ea = ea_scr[...]
    w = (x * ew_scr[...]).astype(BF16)
    s_in = st_scr[...]
    sb = s_in.astype(BF16)
    y_off, c_state = [], []
    for g in range(SSD_GROUPS):
        gs = slice(g * half, (g + 1) * half)
        bgt = xc_scr[:, SSD_INNER + g * SSD_N:SSD_INNER + (g + 1) * SSD_N].T.astype(BF16)
        y_off.append(_dot(cgb[g], sb[:, gs]))
        c_state.append(_dot(bgt, w[:, gs]))
    st_scr[...] = jnp.exp(ea[q - 1:q, :]) * s_in + jnp.concatenate(c_state, axis=1)
    y = y_scr[...] + jnp.concatenate(y_off, axis=1) * jnp.exp(ea) + dexp_ref[...] * x

    yz = y * _silu(z_ref[...])
    y_ref[...] = _rms_norm(yz, ng_ref[...]).astype(BF16)

    @pl.when(pl.program_id(1) == pl.num_programs(1) - 1)
    def _():
        st_ref[...] = st_scr[...].T


def _ssd_call(xbc, z, dtr, p):
    nb, L, _ = xbc.shape
    q = SSD_Q
    vec = lambda b, c: (0, 0)
    row = lambda b, c: (b, c, 0)
    return pl.pallas_call(
        _ssd_kernel,
        out_shape=[jax.ShapeDtypeStruct((nb, L, SSD_INNER), BF16),
                   jax.ShapeDtypeStruct((nb, CONV_W - 1, SSD_CONV_DIM), F32),
                   jax.ShapeDtypeStruct((nb, SSD_INNER, SSD_N), F32)],
        grid=(nb, L // q),
        in_specs=[pl.BlockSpec((None, q, SSD_CONV_DIM), row), pl.BlockSpec((None, q, SSD_INNER), row),
                  pl.BlockSpec((None, q, LANES), row),
                  pl.BlockSpec((CONV_W, SSD_CONV_DIM), vec), pl.BlockSpec((1, SSD_CONV_DIM), vec),
                  pl.BlockSpec((1, LANES), vec), pl.BlockSpec((1, LANES), vec),
                  pl.BlockSpec((1, SSD_INNER), vec), pl.BlockSpec((1, SSD_INNER), vec)],
        out_specs=[pl.BlockSpec((None, q, SSD_INNER), row),
                   pl.BlockSpec((None, CONV_W - 1, SSD_CONV_DIM), lambda b, c: (b, 0, 0)),
                   pl.BlockSpec((None, SSD_INNER, SSD_N), lambda b, c: (b, 0, 0))],
        scratch_shapes=[pltpu.VMEM((q + SUBLANES, SSD_CONV_DIM), F32), pltpu.VMEM((q, SSD_CONV_DIM), F32),
                        pltpu.VMEM((SSD_N, SSD_INNER), F32), pltpu.VMEM((q, SSD_INNER), F32),
                        pltpu.VMEM((q, SSD_INNER), F32), pltpu.VMEM((q, SSD_INNER), F32)],
        compiler_params=_cparams(("parallel", "arbitrary")),
        name="ssd",
    )(xbc, z, dtr, p["cw"], p["cb"], p["dtb"], p["alog"], p["dexp"], p["ng"])


def _srow_kernel(xg_ref, xbc_ref, dte_ref, lconv_ref, h0_ref, sconv_ref,
                 lcw_ref, lcb_ref, wa_ref, wx_ref, ba_ref, bx_ref, lam_ref, lng_ref,
                 scw_ref, scb_ref, dtbe_ref, aloge_ref,
                 yl_ref, h_ref, xs_ref, bm_ref, cm_ref, xdtt_ref, dect_ref, lcn_ref, scn_ref, xc_scr):
    xl = xg_ref[:, 0:LRU_W]
    acc = lcb_ref[...] + xl * lcw_ref[CONV_W - 1:CONV_W, :]
    for k in range(CONV_W - 1):
        acc = acc + lconv_ref[:, k * LRU_W:(k + 1) * LRU_W] * lcw_ref[k:k + 1, :]
    xc_scr[...] = acc
    for k in range(CONV_W - 2):
        lcn_ref[:, k, :] = lconv_ref[:, (k + 1) * LRU_W:(k + 2) * LRU_W]
        scn_ref[:, k, :] = sconv_ref[:, (k + 1) * SSD_CONV_DIM:(k + 2) * SSD_CONV_DIM]
    lcn_ref[:, CONV_W - 2, :] = xl
    scn_ref[:, CONV_W - 2, :] = xbc_ref[...]
    sp = _softplus(-lam_ref[...])
    for j in range(LRU_W // LRU_PACK_W):
        cs = slice(j * LRU_PACK_W, (j + 1) * LRU_PACK_W)
        xc = xc_scr[:, cs]
        a, u = _lru_gates(xc, xc.astype(BF16), wa_ref[j], wx_ref[j], ba_ref[:, cs], bx_ref[:, cs], sp[:, cs])
        h_ref[:, cs] = a * h0_ref[:, cs] + u
    y = h_ref[...] * _gelu_tanh(xg_ref[:, LRU_W:2 * LRU_W])
    yl_ref[...] = _rms_norm(y, lng_ref[...]).astype(BF16)

    acc = scb_ref[...] + xbc_ref[...] * scw_ref[CONV_W - 1:CONV_W, :]
    for k in range(CONV_W - 1):
        acc = acc + sconv_ref[:, k * SSD_CONV_DIM:(k + 1) * SSD_CONV_DIM] * scw_ref[k:k + 1, :]
    xc = _silu(acc)
    xs = xc[:, 0:SSD_INNER]
    xs_ref[...] = xs
    bm_ref[...] = xc[:, SSD_INNER:SSD_INNER + SSD_GROUPS * SSD_N]
    cm_ref[...] = xc[:, SSD_INNER + SSD_GROUPS * SSD_N:]
    dt = _softplus(dte_ref[...] + dtbe_ref[...])
    dec = jnp.exp(dt * (-jnp.exp(aloge_ref[...])))
    xdtt_ref[...] = (xs * dt).T
    dect_ref[...] = dec.T


def _srow_call(xg, xbc, dte, lconv, h0, sconv, lp, sp):
    n = xg.shape[0]
    args = (xg, xbc, dte, lconv, h0, sconv, lp["cw"], lp["cb"], lp["wa"], lp["wx"], lp["ba"], lp["bx"],
            lp["lam"], lp["ng"], sp["cw"], sp["cb"], sp["dtbe"], sp["aloge"])
    full = lambda a: pl.BlockSpec(a.shape, lambda i, nd=a.ndim: (0,) * nd)
    outs = [jax.ShapeDtypeStruct((n, LRU_W), BF16), jax.ShapeDtypeStruct((n, LRU_W), F32),
            jax.ShapeDtypeStruct((n, SSD_INNER), F32), jax.ShapeDtypeStruct((n, SSD_GROUPS * SSD_N), F32),
            jax.ShapeDtypeStruct((n, SSD_GROUPS * SSD_N), F32),
            jax.ShapeDtypeStruct((SSD_INNER, n), F32), jax.ShapeDtypeStruct((SSD_INNER, n), F32),
            jax.ShapeDtypeStruct((n, CONV_W - 1, LRU_W), F32), jax.ShapeDtypeStruct((n, CONV_W - 1, SSD_CONV_DIM), F32)]
    return pl.pallas_call(
        _srow_kernel,
        out_shape=outs,
        grid=(1,),
        in_specs=[full(a) for a in args],
        out_specs=[pl.BlockSpec(o.shape, lambda i, nd=len(o.shape): (0,) * nd) for o in outs],
        scratch_shapes=[pltpu.VMEM((n, LRU_W), F32)],
        compiler_params=_cparams(("arbitrary",)),
        name="srow",
    )(*args)


def _sstate_kernel(s0_ref, xq_ref, dq_ref, bm_ref, cm_ref, xs_ref, z_ref, dexp_ref, ng_ref,
                   sn_ref, ys_ref, yraw):
    nb = s0_ref.shape[0]
    half = SSD_INNER // SSD_GROUPS
    for bi in range(nb):
        brow = jnp.concatenate(
            [jnp.broadcast_to(bm_ref[bi:bi + 1, g * SSD_N:(g + 1) * SSD_N], (half, SSD_N)) for g in range(SSD_GROUPS)],
            axis=0)
        s = dq_ref[:, bi:bi + 1] * s0_ref[bi] + xq_ref[:, bi:bi + 1] * brow
        sn_ref[bi] = s
        sb = s.astype(BF16)
        for g in range(SSD_GROUPS):
            cg = cm_ref[:, g * SSD_N:(g + 1) * SSD_N].astype(BF16)
            res = _dot_nt(cg, sb[g * half:(g + 1) * half, :])
            yraw[bi:bi + 1, g * half:(g + 1) * half] = res[bi:bi + 1, :]
    y = yraw[...] + dexp_ref[...] * xs_ref[...]
    ys_ref[...] = _rms_norm(y * _silu(z_ref[...]), ng_ref[...]).astype(BF16)


def _sstate_call(s0, xq, dq, bm, cm, xs, z, dexp, ng, nb):
    n = s0.shape[0]
    row = lambda i: (i, 0)
    vec = lambda i: (0, 0)
    gn = SSD_GROUPS * SSD_N
    return pl.pallas_call(
        _sstate_kernel,
        out_shape=[jax.ShapeDtypeStruct(s0.shape, F32), jax.ShapeDtypeStruct((n, SSD_INNER), BF16)],
        grid=(n // nb,),
        in_specs=[pl.BlockSpec((nb, SSD_INNER, SSD_N), lambda i: (i, 0, 0)),
                  pl.BlockSpec((None, SSD_INNER, nb), lambda i: (i, 0, 0)),
                  pl.BlockSpec((None, SSD_INNER, nb), lambda i: (i, 0, 0)),
                  pl.BlockSpec((nb, gn), row), pl.BlockSpec((nb, gn), row),
                  pl.BlockSpec((nb, SSD_INNER), row), pl.BlockSpec((nb, SSD_INNER), row),
                  pl.BlockSpec((1, SSD_INNER), vec), pl.BlockSpec((1, SSD_INNER), vec)],
        out_specs=[pl.BlockSpec((nb, SSD_INNER, SSD_N), lambda i: (i, 0, 0)), pl.BlockSpec((nb, SSD_INNER), row)],
        scratch_shapes=[pltpu.VMEM((nb, SSD_INNER), F32)],
        compiler_params=_cparams(("parallel",)),
        name="sstate",
    )(s0, xq, dq, bm, cm, xs, z, dexp, ng)


def _post_kernel(yl_ref, ys_ref, x_ref, g1_ref, sh2_ref, sc2_ref, wo_ref, l1g_ref, l1b_ref, wrt_ref, brc_ref,
                 x1_ref, v_ref, eid_ref, gw_ref, *, per_batch):
    b = pl.program_id(0)
    o = _dot(yl_ref[...], wo_ref[0:LRU_W, :]) + _dot(ys_ref[...], wo_ref[LRU_W:LRU_W + SSD_INNER, :])
    x1 = _layer_norm(ALPHA * x_ref[...] + _mod_rows(g1_ref, per_batch, b) * o) * l1g_ref[...] + l1b_ref[...]
    x1_ref[...] = x1
    v = _layer_norm(x1) * (1.0 + _mod_rows(sc2_ref, per_batch, b)) + _mod_rows(sh2_ref, per_batch, b)
    _store_row_tiles(v_ref, v)

    lt = _dot(v.astype(BF16), wrt_ref[...]).T + brc_ref[...]
    tl = lt.shape[1]
    row = lax.broadcasted_iota(I32, (GROUP_SIZE, tl), 0).astype(F32)
    big = float(GROUP_SIZE)
    neg = -jnp.inf
    lg = jnp.where(row < N_GROUPS, lt[0:GROUP_SIZE, :], neg)
    gmax = jnp.max(lg, axis=0, keepdims=True)
    gsel = jnp.min(jnp.where(lg == gmax, row, big), axis=0, keepdims=True)
    pg = 1.0 / jnp.sum(jnp.exp(lg - gmax), axis=0, keepdims=True)
    le = lt[GROUP_SIZE:2 * GROUP_SIZE, :]
    for j in range(1, N_GROUPS):
        le = jnp.where(gsel == j, lt[GROUP_SIZE * (j + 1):GROUP_SIZE * (j + 2), :], le)
    m1 = jnp.max(le, axis=0, keepdims=True)
    i1 = jnp.min(jnp.where(le == m1, row, big), axis=0, keepdims=True)
    rest = jnp.where(row == i1, neg, le)
    m2 = jnp.max(rest, axis=0, keepdims=True)
    i2 = jnp.min(jnp.where(rest == m2, row, big), axis=0, keepdims=True)
    e2 = jnp.exp(m2 - m1)
    den = 1.0 + e2
    eid = jnp.where(row == 0, gsel * GROUP_SIZE + i1, jnp.where(row == 1, gsel * GROUP_SIZE + i2, 0.0))
    eid_ref[...] = eid.astype(I32)
    gw_ref[...] = jnp.where(row == 0, pg * (1.0 / den), jnp.where(row == 1, pg * (e2 / den), 0.0))


def _post_call(yl, ys, x3, mod, p, tl):
    nb, L, _ = x3.shape
    nl = L // tl
    row = lambda b, l: (b, l, 0)
    vec = lambda b, l: (0, 0)
    tok = lambda b, l: (b * nl + l, 0)
    tokt = lambda b, l: (0, b * nl + l)
    bof, tof = (lambda b, l: b), (lambda b, l: l)
    return pl.pallas_call(
        functools.partial(_post_kernel, per_batch=mod.per_batch),
        out_shape=[jax.ShapeDtypeStruct((nb, L, D), F32), jax.ShapeDtypeStruct(_tile_rows(nb * L), F32),
                   jax.ShapeDtypeStruct((SUBLANES, nb * L), I32), jax.ShapeDtypeStruct((SUBLANES, nb * L), F32)],
        grid=(nb, nl),
        in_specs=[pl.BlockSpec((None, tl, LRU_W), row), pl.BlockSpec((None, tl, SSD_INNER), row),
                  pl.BlockSpec((None, tl, D), row),
                  mod.spec(2, tl, bof, tof), mod.spec(3, tl, bof, tof), mod.spec(4, tl, bof, tof),
                  pl.BlockSpec((LRU_W + SSD_INNER, D), vec), pl.BlockSpec((1, D), vec), pl.BlockSpec((1, D), vec),
                  pl.BlockSpec((D, LANES), vec), pl.BlockSpec((LANES, 1), vec)],
        out_specs=[pl.BlockSpec((None, tl, D), row), pl.BlockSpec(_tile_rows(tl), tok),
                   pl.BlockSpec((SUBLANES, tl), tokt), pl.BlockSpec((SUBLANES, tl), tokt)],
        compiler_params=_cparams(("parallel", "arbitrary")),
        name="post",
    )(yl, ys, x3, mod.table, mod.table, mod.table, p["wo"], p["l1g"], p["l1b"], p["wrt"], p["brc"])


def _route_kernel(eidp_ref, eids_ref, dest_ref, cnt_ref, run, poff, *, p_tiles):
    ph = pl.program_id(0)
    t = pl.program_id(1)
    n = ROUTE_TILE
    eid = jnp.where(t < p_tiles, eidp_ref[...], eids_ref[...])
    rowi = lax.broadcasted_iota(I32, (LANES, n), 0)
    oh0 = rowi == eid[0:1, :]
    oh1 = rowi == eid[1:2, :]
    oh = oh0.astype(F32) + oh1.astype(F32)
    tile_cnt = jnp.sum(oh, axis=1, keepdims=True)

    @pl.when((ph == 0) & (t == 0))
    def _():
        run[...] = jnp.zeros((LANES, n), F32)

    @pl.when(ph == 0)
    def _():
        run[...] = run[...] + tile_cnt

    @pl.when((ph == 1) & (t == 0))
    def _():
        counts = run[...]
        cnt_ref[...] = counts
        ci = counts.astype(I32)
        q = jnp.floor(counts * (1.0 / MOE_BM)).astype(I32)
        rem = ci - q * MOE_BM
        q = q + jnp.where(rem >= MOE_BM, 1, 0) - jnp.where(rem < 0, 1, 0)
        nblk = q + jnp.where(ci - q * MOE_BM > 0, 1, 0)
        r = lax.broadcasted_iota(I32, (LANES, LANES), 0)
        c = lax.broadcasted_iota(I32, (LANES, LANES), 1)
        lower = (r > c).astype(BF16)
        poff[...] = _dot(lower, nblk.astype(F32).astype(BF16)) * float(MOE_BM)
        run[...] = jnp.zeros((LANES, n), F32)

    @pl.when(ph == 1)
    def _():
        r = lax.broadcasted_iota(I32, (n, n), 0)
        c = lax.broadcasted_iota(I32, (n, n), 1)
        before = (r < c).astype(BF16)
        slot = _dot(oh.astype(BF16), before) + run[...] + poff[...]
        d0 = jnp.sum(jnp.where(oh0, slot, 0.0), axis=0, keepdims=True)
        d1 = jnp.sum(jnp.where(oh1, slot, 0.0), axis=0, keepdims=True)
        row = lax.broadcasted_iota(I32, (SUBLANES, n), 0)
        dest_ref[...] = jnp.where(row == 0, d0, jnp.where(row == 1, d1, 0.0)).astype(I32)
        run[...] = run[...] + tile_cnt


def _route_call(eid_p, eid_s):
    n = ROUTE_TILE
    p_tiles = eid_p.shape[1] // n
    s_tiles = eid_s.shape[1] // n
    t_all = (p_tiles + s_tiles) * n
    return pl.pallas_call(
        functools.partial(_route_kernel, p_tiles=p_tiles),
        out_shape=[jax.ShapeDtypeStruct((SUBLANES, t_all), I32), jax.ShapeDtypeStruct((LANES, n), F32)],
        grid=(2, p_tiles + s_tiles),
        in_specs=[pl.BlockSpec((SUBLANES, n), lambda ph, t: (0, jnp.minimum(t, p_tiles - 1))),
                  pl.BlockSpec((SUBLANES, n), lambda ph, t: (0, jnp.maximum(t - p_tiles, 0)))],
        out_specs=[pl.BlockSpec((SUBLANES, n), lambda ph, t: (0, t * ph)),
                   pl.BlockSpec((LANES, n), lambda ph, t: (0, 0))],
        scratch_shapes=[pltpu.VMEM((LANES, n), F32), pltpu.VMEM((LANES, n), F32)],
        compiler_params=_cparams(("arbitrary", "arbitrary")),
        name="route",
    )(eid_p, eid_s)


def _row_copy(src, src_row, dst, dst_row, sem):
    s0 = pl.multiple_of(src_row * SUBLANES, SUBLANES)
    d0 = pl.multiple_of(dst_row * SUBLANES, SUBLANES)
    return pltpu.make_async_copy(src.at[pl.ds(s0, SUBLANES), :], dst.at[pl.ds(d0, SUBLANES), :], sem)


def _scatter_rows(dest_ref, v_ref, xpad_ref, sem):
    def start(r, carry):
        for k in range(2):
            _row_copy(v_ref, r, xpad_ref, dest_ref[k, r], sem).start(priority=k)
        return carry

    def wait(r, carry):
        for k in range(2):
            _row_copy(v_ref, r, xpad_ref, dest_ref[k, r], sem).wait()
        return carry

    lax.fori_loop(0, TOK_TILE, start, 0, unroll=DMA_UNROLL)
    lax.fori_loop(0, TOK_TILE, wait, 0, unroll=DMA_UNROLL)


def _dispatch_kernel(zflag_ref, dest_ref, vp_ref, vs_ref, xpad_ref, zbuf, sem, zsem, *, p_tiles, n_blocks):
    t = pl.program_id(0)
    blk_rows = _tile_rows(MOE_BM)[0]

    @pl.when(t == 0)
    def _():
        zbuf[...] = jnp.zeros(zbuf.shape, F32)

        def zero_block(go):
            def body(b, carry):
                @pl.when(zflag_ref[b] != 0)
                def _():
                    r0 = pl.multiple_of(b * blk_rows, blk_rows)
                    cp = pltpu.make_async_copy(zbuf, xpad_ref.at[pl.ds(r0, blk_rows), :], zsem)
                    if go:
                        cp.start()
                    else:
                        cp.wait()
                return carry
            lax.fori_loop(0, n_blocks, body, 0)

        zero_block(True)
        zero_block(False)

    @pl.when(t < p_tiles)
    def _():
        _scatter_rows(dest_ref, vp_ref, xpad_ref, sem)

    @pl.when(t >= p_tiles)
    def _():
        _scatter_rows(dest_ref, vs_ref, xpad_ref, sem)


def _dispatch_call(zflag, dest, v_p, v_s, n_blocks):
    p_tiles = v_p.shape[0] // _tile_rows(TOK_TILE)[0]
    s_tiles = v_s.shape[0] // _tile_rows(TOK_TILE)[0]
    return pl.pallas_call(
        functools.partial(_dispatch_kernel, p_tiles=p_tiles, n_blocks=n_blocks),
        out_shape=jax.ShapeDtypeStruct(_tile_rows(n_blocks * MOE_BM), F32),
        grid_spec=pltpu.PrefetchScalarGridSpec(
            num_scalar_prefetch=1,
            grid=(p_tiles + s_tiles,),
            in_specs=[pl.BlockSpec((SUBLANES, TOK_TILE), lambda t, z: (0, t), memory_space=pltpu.SMEM),
                      pl.BlockSpec(_tile_rows(TOK_TILE), lambda t, z: (jnp.minimum(t, p_tiles - 1), 0)),
                      pl.BlockSpec(_tile_rows(TOK_TILE), lambda t, z: (jnp.maximum(t - p_tiles, 0), 0))],
            out_specs=pl.BlockSpec(memory_space=pl.ANY),
            scratch_shapes=[pltpu.VMEM(_tile_rows(MOE_BM), F32), pltpu.SemaphoreType.DMA, pltpu.SemaphoreType.DMA]),
        compiler_params=_cparams(("arbitrary",)),
        name="dispatch",
    )(zflag, dest, v_p, v_s)


def _expert_kernel(be_ref, nb_ref, nxt_ref, slot_ref, x_ref, wg_hbm, wu_hbm, wd_hbm, o_ref,
                   wgf, wuf, wdf, wgb, wub, wdb, sems):
    i = pl.program_id(0)

    def weights(e, slot, go):
        for m, (hbm, buf) in enumerate(((wg_hbm, wgf), (wu_hbm, wuf), (wd_hbm, wdf))):
            cp = pltpu.make_async_copy(hbm.at[e], buf.at[slot], sems.at[slot, m])
            if go:
                cp.start()
            else:
                cp.wait()

    @pl.when(i < nb_ref[0])
    def _():
        e = be_ref[i]
        slot = slot_ref[e]

        @pl.when(i == 0)
        def _():
            weights(e, slot, True)

        @pl.when((i == 0) | (e != be_ref[jnp.maximum(i - 1, 0)]))
        def _():
            weights(e, slot, False)
            nxt = nxt_ref[e]

            @pl.when(nxt >= 0)
            def _():
                weights(nxt, 1 - slot, True)

            wgb[...] = wgf[slot].astype(BF16)
            wub[...] = wuf[slot].astype(BF16)
            wdb[...] = wdf[slot].astype(BF16)

        x = _load_row_tiles(x_ref).astype(BF16)
        h = _silu(_dot(x, wgb[...])) * _dot(x, wub[...])
        _store_row_tiles(o_ref, _dot(h.astype(BF16), wdb[...]))

    @pl.when(i >= nb_ref[0])
    def _():
        o_ref[...] = jnp.zeros(o_ref.shape, F32)


def _expert_call(blk_e, nblk, nxt_e, slot_e, xpad, w_gate, w_up, w_down):
    n_rows = xpad.shape[0] // SUBLANES
    blk = lambda i, be, nb, nx, sl: (jnp.minimum(i, nb[0] - 1), 0)
    hbm = pl.BlockSpec(memory_space=pl.ANY)
    return pl.pallas_call(
        _expert_kernel,
        out_shape=jax.ShapeDtypeStruct(_tile_rows(n_rows), F32),
        grid_spec=pltpu.PrefetchScalarGridSpec(
            num_scalar_prefetch=4,
            grid=(n_rows // MOE_BM,),
            in_specs=[pl.BlockSpec(_tile_rows(MOE_BM), blk), hbm, hbm, hbm],
            out_specs=pl.BlockSpec(_tile_rows(MOE_BM), lambda i, be, nb, nx, sl: (i, 0)),
            scratch_shapes=[pltpu.VMEM((2, D, D_FF), F32), pltpu.VMEM((2, D, D_FF), F32), pltpu.VMEM((2, D_FF, D), F32),
                            pltpu.VMEM((D, D_FF), BF16), pltpu.VMEM((D, D_FF), BF16), pltpu.VMEM((D_FF, D), BF16),
                            pltpu.SemaphoreType.DMA((2, 3))]),
        compiler_params=_cparams(("arbitrary",)),
        name="expert",
    )(blk_e, nblk, nxt_e, slot_e, xpad, w_gate, w_up, w_down)


def _combine_kernel(dest_ref, gw0_ref, gw1_ref, x1_ref, g2_ref, l2g_ref, l2b_ref, ypad_ref, y_ref, ybuf, sems,
                    *, tok_off, k_stride, tl, nl, per_batch):
    i = pl.program_id(0)

    def gather(tile, slot, go):
        base = tok_off + tile * tl

        def body(r, carry):
            for k in range(2):
                cp = _row_copy(ypad_ref, dest_ref[k * k_stride + base + r], ybuf.at[slot, k], r, sems.at[slot])
                if go:
                    cp.start(priority=k)
                else:
                    cp.wait()
            return carry

        lax.fori_loop(0, tl, body, 0, unroll=DMA_UNROLL)

    @pl.when(i == 0)
    def _():
        gather(0, 0, True)

    @pl.when(i + 1 < pl.num_programs(0))
    def _():
        gather(i + 1, (i + 1) % 2, True)

    slot = i % 2
    gather(i, slot, False)
    f = _load_row_tiles(ybuf.at[slot, 0]) * gw0_ref[...] + _load_row_tiles(ybuf.at[slot, 1]) * gw1_ref[...]
    g2 = _mod_rows(g2_ref, per_batch, i // nl)
    y_ref[...] = _layer_norm(ALPHA * x1_ref[...] + g2 * f) * l2g_ref[...] + l2b_ref[...]


def _combine_call(dest, gw0, gw1, x1, mod, l2g, l2b, ypad, tok_off, tl):
    nb, L, _ = x1.shape
    nl = L // tl
    row = lambda i, d: (i // nl, i % nl, 0)
    vec = lambda i, d: (0, 0)
    tok = lambda i, d: (i, 0)
    g2_spec = mod.spec(5, tl, lambda i, d: i // nl, lambda i, d: i % nl)
    return pl.pallas_call(
        functools.partial(_combine_kernel, tok_off=tok_off, k_stride=dest.shape[1], tl=tl, nl=nl,
                          per_batch=mod.per_batch),
        out_shape=jax.ShapeDtypeStruct((nb, L, D), F32),
        grid_spec=pltpu.PrefetchScalarGridSpec(
            num_scalar_prefetch=1,
            grid=(nb * nl,),
            in_specs=[pl.BlockSpec((tl, 1), tok), pl.BlockSpec((tl, 1), tok),
                      pl.BlockSpec((None, tl, D), row), g2_spec,
                      pl.BlockSpec((1, D), vec), pl.BlockSpec((1, D), vec),
                      pl.BlockSpec(memory_space=pl.ANY)],
            out_specs=pl.BlockSpec((None, tl, D), row),
            scratch_shapes=[pltpu.VMEM((2, 2) + _tile_rows(tl), F32), pltpu.SemaphoreType.DMA((2,))]),
        compiler_params=_cparams(("arbitrary",)),
        name="combine",
    )(dest[:2].reshape(-1), gw0, gw1, x1, mod.table, l2g, l2b, ypad)


def _block_diag(w):
    nh, blk, _ = w.shape
    w4 = w.reshape(nh // LRU_PACK, LRU_PACK, blk, blk)
    eye = jnp.eye(LRU_PACK, dtype=w.dtype)
    out = jnp.einsum("gaij,ab->gaibj", w4, eye)
    return out.reshape(nh // LRU_PACK, LRU_PACK * blk, LRU_PACK * blk).astype(BF16)


def _row(v):
    return v.reshape(1, -1).astype(F32)


def _pad_lanes(v):
    return jnp.pad(v.reshape(1, -1).astype(F32), ((0, 0), (0, LANES - v.shape[-1])))


def kernel(x_prompt, x_sample, c_prompt, c_sample, state_lru_conv, state_lru_h, state_ssd_conv, state_ssd, w_ada, b_ada, w_in, lru_conv_w, lru_conv_b, lru_wa, lru_ba, lru_wx, lru_bx, lru_lambda, lru_norm_g, ssd_conv_w, ssd_conv_b, ssd_dt_bias, ssd_a_log, ssd_d, ssd_norm_g, w_out, ln1_g, ln1_b, w_rg, b_rg, w_re, b_re, w_gate, w_up, w_down, ln2_g, ln2_b):
    assert w_ada.shape[0] == DEPTH == 1
    nbp, seq, _ = x_prompt.shape
    nbs = x_sample.shape[0]
    t_p = nbp * seq
    t_all = t_p + nbs
    assert x_sample.shape[1] == 1 and t_p % ROUTE_TILE == 0 and nbs % TOK_TILE == 0 and nbp == SUBLANES

    c_rows = -(-(nbp + nbs) // 16) * 16
    c_all = jnp.pad(jnp.concatenate([c_sample, c_prompt], axis=0), ((0, c_rows - nbp - nbs), (0, 0)))
    w_in0 = w_in[0]
    o_z, o_xbc, o_dt = 2 * LRU_W, 2 * LRU_W + SSD_INNER, 2 * LRU_W + SSD_INNER + SSD_CONV_DIM
    w_dt = w_in0[:, o_dt:]
    ws_p = list(_wcast_call(w_in0, ((0, o_z), (o_z, o_xbc), (o_xbc, o_dt)), 256))
    ws_s = ws_p + [jnp.repeat(w_dt, SSD_P, axis=1).astype(BF16)]
    ws_p = ws_p + [jnp.pad(w_dt, ((0, 0), (0, LANES - SSD_HEADS))).astype(BF16)]
    lp = dict(cw=lru_conv_w[0], cb=_row(lru_conv_b[0]), wa=_block_diag(lru_wa[0]), wx=_block_diag(lru_wx[0]),
              ba=_row(lru_ba[0]), bx=_row(lru_bx[0]), lam=_row(lru_lambda[0]), ng=_row(lru_norm_g[0]))
    sp = dict(cw=ssd_conv_w[0], cb=_row(ssd_conv_b[0]), dtb=_pad_lanes(ssd_dt_bias[0]), alog=_pad_lanes(ssd_a_log[0]),
              dexp=_row(jnp.repeat(ssd_d[0], SSD_P)), ng=_row(ssd_norm_g[0]),
              dtbe=_row(jnp.repeat(ssd_dt_bias[0], SSD_P)), aloge=_row(jnp.repeat(ssd_a_log[0], SSD_P)))
    wrt = jnp.zeros((D, LANES), F32).at[:, 0:N_GROUPS].set(w_rg[0]).at[:, GROUP_SIZE:ROUTE_ROWS].set(w_re[0]).astype(BF16)
    brc = jnp.zeros((LANES, 1), F32).at[0:N_GROUPS, 0].set(b_rg[0]).at[GROUP_SIZE:ROUTE_ROWS, 0].set(b_re[0])
    pp = dict(wo=w_out[0].astype(BF16), l1g=_row(ln1_g[0]), l1b=_row(ln1_b[0]), wrt=wrt, brc=brc)

    table = _ada_call(c_all, w_ada[0], _row(b_ada[0]))
    mod_p = _Mod(table, nbs, True)
    mod_s = _Mod(table, 0, False)

    xg, z, xbc, dtr = _proj_call(x_prompt, mod_p, ws_p, SEQ_TILE)
    yl, p_lru_conv, p_lru_h = _lru_call(xg, lp, SEQ_TILE)
    ys, p_ssd_conv, p_ssd = _ssd_call(xbc, z, dtr, sp)
    x1_p, v_p, eid_p, gw_p = _post_call(yl, ys, x_prompt, mod_p, pp, SEQ_TILE)

    xs3 = x_sample.reshape(1, nbs, D)
    xg_s, z_s, xbc_s, dte_s = _proj_call(xs3, mod_s, ws_s, nbs)
    xg_s, z_s, xbc_s, dte_s = xg_s[0], z_s[0], xbc_s[0], dte_s[0]
    yl_s, h_s, xs_s, bm_s, cm_s, xdtt, dect, s_lru_conv, s_ssd_conv = _srow_call(
        xg_s, xbc_s, dte_s, state_lru_conv[0].reshape(nbs, -1), state_lru_h[0],
        state_ssd_conv[0].reshape(nbs, -1), lp, sp)
    sb = SUBLANES
    regroup = lambda a: a.reshape(SSD_INNER, nbs // sb, sb).transpose(1, 0, 2)
    s_new, ys_s = _sstate_call(state_ssd[0].reshape(nbs, SSD_INNER, SSD_N), regroup(xdtt), regroup(dect),
                               bm_s, cm_s, xs_s, z_s, sp["dexp"], sp["ng"], sb)
    x1_s, v_s, eid_s, gw_s = _post_call(yl_s[None], ys_s[None], xs3, mod_s, pp, nbs)

    s_pad = -nbs % ROUTE_TILE
    dest, counts = _route_call(eid_p, jnp.pad(eid_s, ((0, 0), (0, s_pad)), constant_values=-1))
    counts = counts[:N_EXPERTS, 0].astype(I32)
    n_blocks = -(-(2 * t_all) // MOE_BM) + N_EXPERTS
    pend = jnp.cumsum(((counts + MOE_BM - 1) // MOE_BM) * MOE_BM)
    blk_start = jnp.arange(n_blocks, dtype=I32) * MOE_BM
    blk_e = jnp.minimum(jnp.sum((pend[None, :] <= blk_start[:, None]).astype(I32), axis=1), N_EXPERTS - 1)
    nblk = (pend[-1:] // MOE_BM).astype(I32)
    blk_ids = jnp.arange(n_blocks, dtype=I32)
    last_of_expert = jnp.any(((pend // MOE_BM - 1)[None, :] == blk_ids[:, None]) & (counts > 0)[None, :], axis=1)
    zflag = (last_of_expert | (blk_ids >= nblk[0])).astype(I32)
    used = counts > 0
    eids = jnp.arange(N_EXPERTS, dtype=I32)
    later_used = (eids[None, :] > eids[:, None]) & used[None, :]
    nxt_e = jnp.min(jnp.where(later_used, eids[None, :], N_EXPERTS), axis=1)
    nxt_e = jnp.where(nxt_e == N_EXPERTS, -1, nxt_e).astype(I32)
    slot_e = ((jnp.cumsum(used.astype(I32)) - used.astype(I32)) % 2).astype(I32)
    xpad = _dispatch_call(zflag, dest, v_p, v_s, n_blocks)
    ypad = _expert_call(blk_e, nblk, nxt_e, slot_e, xpad, w_gate[0], w_up[0], w_down[0])
    l2g, l2b = _row(ln2_g[0]), _row(ln2_b[0])
    y_p = _combine_call(dest, gw_p[0].reshape(t_p, 1), gw_p[1].reshape(t_p, 1), x1_p, mod_p, l2g, l2b, ypad, 0,
                        2 * TOK_TILE)
    y_s = _combine_call(dest, gw_s[0].reshape(nbs, 1), gw_s[1].reshape(nbs, 1), x1_s, mod_s, l2g, l2b, ypad, t_p,
                        TOK_TILE)

    return (y_p, y_s.reshape(nbs, 1, D),
            p_lru_conv[None], p_lru_h.reshape(1, nbp, LRU_W), p_ssd_conv[None],
            p_ssd.reshape(1, nbp, SSD_HEADS, SSD_P, SSD_N),
            s_lru_conv[None], h_s[None], s_ssd_conv[None],
            s_new.reshape(1, nbs, SSD_HEADS, SSD_P, SSD_N))
```

```python
import functools
import math

import jax
import jax.numpy as jnp
from jax import lax
from jax.experimental import pallas as pl
from jax.experimental.pallas import tpu as pltpu

F32 = jnp.float32
BF16 = jnp.bfloat16
I32 = jnp.int32
HIGHEST = lax.Precision.HIGHEST

D = 1024
DEPTH = 1
CONV_W = 4
LRU_W = D
LRU_HEADS = 16
LRU_C = 8.0
LRU_PACK = 4
LRU_PACK_W = LRU_PACK * (LRU_W // LRU_HEADS)
SSD_INNER = D
SSD_HEADS = 16
SSD_P = SSD_INNER // SSD_HEADS
SSD_GROUPS = 2
SSD_N = 128
SSD_Q = 128
SSD_CONV_DIM = SSD_INNER + 2 * SSD_GROUPS * SSD_N
N_GROUPS = 4
GROUP_SIZE = 8
N_EXPERTS = N_GROUPS * GROUP_SIZE
D_FF = D // 2
N_MOD = 6
LN_EPS = 1e-5
RMS_EPS = 1e-6
ALPHA = (2.0 * DEPTH) ** 0.25

LANES = 128
SUBLANES = 8
VMEM_LIMIT = 56 * 1024 * 1024
SEQ_TILE = 512
TOK_TILE = 128
ROUTE_TILE = 1024
DMA_UNROLL = 8
MOE_BM = 384
ROUTE_ROWS = 40

NT_DIMS = (((1,), (1,)), ((), ()))


def _tile_rows(rows):
    return (rows * SUBLANES, LANES)


def _cparams(sem):
    return pltpu.CompilerParams(dimension_semantics=sem, vmem_limit_bytes=VMEM_LIMIT)


def _sigmoid(x):
    return 0.5 * (jnp.tanh(0.5 * x) + 1.0)


def _silu(x):
    return x * _sigmoid(x)


def _softplus(x):
    return jnp.maximum(x, 0.0) + jnp.log1p(jnp.exp(-jnp.abs(x)))


def _gelu_tanh(x):
    return 0.5 * x * (1.0 + jnp.tanh(math.sqrt(2.0 / math.pi) * (x + 0.044715 * (x * x * x))))


def _layer_norm(x):
    mu = jnp.mean(x, axis=-1, keepdims=True)
    xc = x - mu
    var = jnp.mean(xc * xc, axis=-1, keepdims=True)
    return xc * lax.rsqrt(var + LN_EPS)


def _rms_norm(x, g):
    return x * lax.rsqrt(jnp.mean(x * x, axis=-1, keepdims=True) + RMS_EPS) * g


def _dot(a, b, **kw):
    return jnp.dot(a, b, preferred_element_type=F32, **kw)


def _dot_nt(a, b, **kw):
    return lax.dot_general(a, b, NT_DIMS, preferred_element_type=F32, **kw)


def _store_row_tiles(ref, val):
    rows = val.shape[0]
    for j in range(SUBLANES):
        ref[pl.ds(j, rows, stride=SUBLANES), :] = val[:, j * LANES:(j + 1) * LANES]


def _load_row_tiles(ref):
    rows = ref.shape[0] // SUBLANES
    return jnp.concatenate([ref[pl.ds(j, rows, stride=SUBLANES), :] for j in range(SUBLANES)], axis=1)


def _lru_gates(xc, xb, wa, wx, ba, bx, sp):
    r = _sigmoid(_dot(xb, wa) + ba)
    i = _sigmoid(_dot(xb, wx) + bx)
    log_a = (-LRU_C) * r * sp
    a = jnp.exp(log_a)
    mult = jnp.sqrt(jnp.tanh(-log_a) * (a * a + 1.0))
    return a, mult * (i * xc)


def _ada_kernel(c_ref, w_ref, b_ref, o_ref):
    s = _silu(c_ref[...]).astype(BF16)
    o_ref[...] = _dot(s, w_ref[...].astype(BF16)) + b_ref[...]


def _ada_call(c_all, w_ada, b_ada):
    rows = c_all.shape[0]
    tn = 512
    per_mod = D // tn
    return pl.pallas_call(
        _ada_kernel,
        out_shape=jax.ShapeDtypeStruct((N_MOD, rows, D), F32),
        grid=(N_MOD * per_mod,),
        in_specs=[pl.BlockSpec((rows, D), lambda j: (0, 0)),
                  pl.BlockSpec((D, tn), lambda j: (0, j)),
                  pl.BlockSpec((1, tn), lambda j: (0, j))],
        out_specs=pl.BlockSpec((None, rows, tn), lambda j: (j // per_mod, 0, j % per_mod)),
        compiler_params=_cparams(("arbitrary",)),
        name="ada",
    )(c_all, w_ada, b_ada)


def _wcast_kernel(w_ref, *o_refs, bounds):
    for o_ref, (lo, hi) in zip(o_refs, bounds):
        o_ref[...] = w_ref[:, lo:hi].astype(BF16)


def _wcast_call(w, bounds, tr):
    rows, cols = w.shape
    assert all(lo % LANES == 0 for lo, _ in bounds)
    return pl.pallas_call(
        functools.partial(_wcast_kernel, bounds=bounds),
        out_shape=[jax.ShapeDtypeStruct((rows, hi - lo), BF16) for lo, hi in bounds],
        grid=(rows // tr,),
        in_specs=[pl.BlockSpec((tr, cols), lambda i: (i, 0))],
        out_specs=[pl.BlockSpec((tr, hi - lo), lambda i: (i, 0)) for lo, hi in bounds],
        compiler_params=_cparams(("arbitrary",)),
        name="wcast",
    )(w)


class _Mod:
    def __init__(self, table, row0, per_batch):
        self.table, self.row0, self.per_batch = table, row0, per_batch

    def spec(self, k, tl, batch_of, tile_of):
        if self.per_batch:
            blk = self.row0 // SUBLANES
            return pl.BlockSpec((None, SUBLANES, D), lambda *g: (k, blk, 0))
        blk = self.row0 // tl
        return pl.BlockSpec((None, tl, D), lambda *g: (k, blk + tile_of(*g), 0))


def _mod_rows(ref, per_batch, b):
    return ref[pl.ds(b, 1), :] if per_batch else ref[...]


def _proj_kernel(x_ref, sh_ref, sc_ref, w1_ref, w2_ref, w3_ref, w4_ref, o1_ref, o2_ref, o3_ref, o4_ref, *, per_batch):
    b = pl.program_id(0)
    u = _layer_norm(x_ref[...]) * (1.0 + _mod_rows(sc_ref, per_batch, b)) + _mod_rows(sh_ref, per_batch, b)
    ub = u.astype(BF16)
    o1_ref[...] = _dot(ub, w1_ref[...])
    o2_ref[...] = _dot(ub, w2_ref[...])
    o3_ref[...] = _dot(ub, w3_ref[...])
    o4_ref[...] = _dot(ub, w4_ref[...])


def _causal_conv(xpad, x_new, cw_ref, cb_ref, conv_ref, tl):
    xpad[SUBLANES:SUBLANES + tl, :] = x_new
    off = SUBLANES - (CONV_W - 1)
    acc = xpad[off:off + tl, :] * cw_ref[0:1, :]
    for k in range(1, CONV_W):
        acc = acc + xpad[off + k:off + k + tl, :] * cw_ref[k:k + 1, :]
    conv_ref[...] = xpad[SUBLANES + tl - (CONV_W - 1):SUBLANES + tl, :]
    xpad[0:SUBLANES, :] = xpad[tl:tl + SUBLANES, :]
    return acc + cb_ref[...]


def _proj_conv_kernel(x_ref, sh_ref, sc_ref, w1_ref, w2_ref, w3_ref, w4_ref, lcw_ref, lcb_ref, scw_ref, scb_ref,
                      o1_ref, o2_ref, o3_ref, o4_ref, lconv_ref, sconv_ref, lpad, spad, *, tl):
    b = pl.program_id(0)

    @pl.when(pl.program_id(1) == 0)
    def _():
        lpad[0:SUBLANES, :] = jnp.zeros((SUBLANES, LRU_W), F32)
        spad[0:SUBLANES, :] = jnp.zeros((SUBLANES, SSD_CONV_DIM), F32)

    u = _layer_norm(x_ref[...]) * (1.0 + _mod_rows(sc_ref, True, b)) + _mod_rows(sh_ref, True, b)
    ub = u.astype(BF16)
    o1_ref[:, 0:LRU_W] = _causal_conv(lpad, _dot(ub, w1_ref[:, 0:LRU_W]), lcw_ref, lcb_ref, lconv_ref, tl)
    o1_ref[:, LRU_W:2 * LRU_W] = _dot(ub, w1_ref[:, LRU_W:2 * LRU_W])
    o2_ref[...] = _dot(ub, w2_ref[...])
    o3_ref[...] = _silu(_causal_conv(spad, _dot(ub, w3_ref[...]), scw_ref, scb_ref, sconv_ref, tl))
    o4_ref[...] = _dot(ub, w4_ref[...])


def _proj_conv_call(x3, mod, ws, lp, sp, tl):
    nb, L, _ = x3.shape
    widths = [w.shape[1] for w in ws]
    row = lambda b, l: (b, l, 0)
    full = lambda b, l: (0, 0)
    st = lambda b, l: (b, 0, 0)
    bof, tof = (lambda b, l: b), (lambda b, l: l)
    return pl.pallas_call(
        functools.partial(_proj_conv_kernel, tl=tl),
        out_shape=[jax.ShapeDtypeStruct((nb, L, n), F32) for n in widths]
                  + [jax.ShapeDtypeStruct((nb, CONV_W - 1, LRU_W), F32),
                     jax.ShapeDtypeStruct((nb, CONV_W - 1, SSD_CONV_DIM), F32)],
        grid=(nb, L // tl),
        in_specs=[pl.BlockSpec((None, tl, D), row), mod.spec(0, tl, bof, tof), mod.spec(1, tl, bof, tof)]
                 + [pl.BlockSpec((D, n), full) for n in widths]
                 + [pl.BlockSpec((CONV_W, LRU_W), full), pl.BlockSpec((1, LRU_W), full),
                    pl.BlockSpec((CONV_W, SSD_CONV_DIM), full), pl.BlockSpec((1, SSD_CONV_DIM), full)],
        out_specs=[pl.BlockSpec((None, tl, n), row) for n in widths]
                  + [pl.BlockSpec((None, CONV_W - 1, LRU_W), st), pl.BlockSpec((None, CONV_W - 1, SSD_CONV_DIM), st)],
        scratch_shapes=[pltpu.VMEM((tl + SUBLANES, LRU_W), F32), pltpu.VMEM((tl + SUBLANES, SSD_CONV_DIM), F32)],
        compiler_params=_cparams(("parallel", "arbitrary")),
        name="projc",
    )(x3, mod.table, mod.table, *ws, lp["cw"], lp["cb"], sp["cw"], sp["cb"])


def _proj_call(x3, mod, ws, tl):
    nb, L, _ = x3.shape
    widths = [w.shape[1] for w in ws]
    row = lambda b, l: (b, l, 0)
    full = lambda b, l: (0, 0)
    bof, tof = (lambda b, l: b), (lambda b, l: l)
    return pl.pallas_call(
        functools.partial(_proj_kernel, per_batch=mod.per_batch),
        out_shape=[jax.ShapeDtypeStruct((nb, L, n), F32) for n in widths],
        grid=(nb, L // tl),
        in_specs=[pl.BlockSpec((None, tl, D), row), mod.spec(0, tl, bof, tof), mod.spec(1, tl, bof, tof)]
                 + [pl.BlockSpec((D, n), full) for n in widths],
        out_specs=[pl.BlockSpec((None, tl, n), row) for n in widths],
        compiler_params=_cparams(("parallel", "arbitrary")),
        name="proj",
    )(x3, mod.table, mod.table, *ws)


def _scan_rows(a_scr, u_scr, h0, n_rows):
    sub = lax.broadcasted_iota(I32, (SUBLANES, D), 0)
    shifts = (1, 2, 4)
    masks = [sub >= d for d in shifts]

    def body(j, hp):
        r0 = pl.multiple_of(j * SUBLANES, SUBLANES)
        a = a_scr[pl.ds(r0, SUBLANES), :]
        u = u_scr[pl.ds(r0, SUBLANES), :]
        for d, m in zip(shifts, masks):
            ash = pltpu.roll(a, d, 0)
            ush = pltpu.roll(u, d, 0)
            u = jnp.where(m, u + a * ush, u)
            a = jnp.where(m, a * ash, a)
        h = u + a * hp
        u_scr[pl.ds(r0, SUBLANES), :] = h
        return h[SUBLANES - 1:SUBLANES, :]

    return lax.fori_loop(0, n_rows // SUBLANES, body, h0)


def _lru_kernel(x_ref, g_ref, wa_ref, wx_ref, ba_ref, bx_ref, lam_ref, ng_ref,
                y_ref, h_ref, a_scr, u_scr, hc_scr, *, tl):
    @pl.when(pl.program_id(1) == 0)
    def _():
        hc_scr[...] = jnp.zeros((1, D), F32)

    sp = _softplus(-lam_ref[...])
    for j in range(LRU_W // LRU_PACK_W):
        cs = slice(j * LRU_PACK_W, (j + 1) * LRU_PACK_W)
        xc = x_ref[:, cs]
        a, u = _lru_gates(xc, xc.astype(BF16), wa_ref[j], wx_ref[j], ba_ref[:, cs], bx_ref[:, cs], sp[:, cs])
        a_scr[:, cs] = a
        u_scr[:, cs] = u

    h_last = _scan_rows(a_scr, u_scr, hc_scr[...], tl)
    hc_scr[...] = h_last
    h_ref[...] = h_last
    y = u_scr[...] * _gelu_tanh(g_ref[...])
    y_ref[...] = _rms_norm(y, ng_ref[...]).astype(BF16)


def _lru_call(xg, p, tl):
    nb, L, _ = xg.shape
    vec = lambda b, l: (0, 0)
    blk = lambda b, l: (0, 0, 0)
    return pl.pallas_call(
        functools.partial(_lru_kernel, tl=tl),
        out_shape=[jax.ShapeDtypeStruct((nb, L, LRU_W), BF16),
                   jax.ShapeDtypeStruct((nb, 1, LRU_W), F32)],
        grid=(nb, L // tl),
        in_specs=[pl.BlockSpec((None, tl, LRU_W), lambda b, l: (b, l, 0)),
                  pl.BlockSpec((None, tl, LRU_W), lambda b, l: (b, l, 1)),
                  pl.BlockSpec((LRU_W // LRU_PACK_W, LRU_PACK_W, LRU_PACK_W), blk),
                  pl.BlockSpec((LRU_W // LRU_PACK_W, LRU_PACK_W, LRU_PACK_W), blk),
                  pl.BlockSpec((1, LRU_W), vec), pl.BlockSpec((1, LRU_W), vec),
                  pl.BlockSpec((1, LRU_W), vec), pl.BlockSpec((1, LRU_W), vec)],
        out_specs=[pl.BlockSpec((None, tl, LRU_W), lambda b, l: (b, l, 0)),
                   pl.BlockSpec((None, 1, LRU_W), lambda b, l: (b, 0, 0))],
        scratch_shapes=[pltpu.VMEM((tl, LRU_W), F32), pltpu.VMEM((tl, LRU_W), F32),
                        pltpu.VMEM((1, LRU_W), F32)],
        compiler_params=_cparams(("parallel", "arbitrary")),
        name="lru",
    )(xg, xg, p["wa"], p["wx"], p["ba"], p["bx"], p["lam"], p["ng"])


def _ssd_kernel(xc_scr, z_ref, dt_ref, dtb_ref, alog_ref, dexp_ref, ng_ref,
                y_ref, st_ref, st_scr, y_scr, ea_scr, ew_scr):
    q = SSD_Q

    @pl.when(pl.program_id(1) == 0)
    def _():
        st_scr[...] = jnp.zeros((SSD_N, SSD_INNER), F32)

    dt = _softplus(dt_ref[...] + dtb_ref[...])
    da = dt * (-jnp.exp(alog_ref[...]))
    ri = lax.broadcasted_iota(I32, (q, q), 0)
    ci = lax.broadcasted_iota(I32, (q, q), 1)
    causal = ri >= ci
    acs = _dot(causal.astype(F32), da, precision=HIGHEST)
    acs_t = acs.T
    dt_t = dt.T
    w_end = dt * jnp.exp(acs[q - 1:q, :] - acs)

    half = SSD_INNER // SSD_GROUPS
    hpg = SSD_HEADS // SSD_GROUPS
    cgb, cbs = [], []
    for g in range(SSD_GROUPS):
        bgb = xc_scr[:, SSD_INNER + g * SSD_N:SSD_INNER + (g + 1) * SSD_N].astype(BF16)
        cgb.append(xc_scr[:, SSD_INNER + (SSD_GROUPS + g) * SSD_N:SSD_INNER + (SSD_GROUPS + g + 1) * SSD_N].astype(BF16))
        cbs.append(_dot_nt(cgb[g], bgb))

    low = lax.broadcasted_iota(I32, (q, LANES), 1) < SSD_P
    for k in range(SSD_HEADS // 2):
        cs = slice(k * LANES, (k + 1) * LANES)
        cb = cbs[(2 * k) // hpg]
        m, colb, wb = [], [], []
        for h in (2 * k, 2 * k + 1):
            colb.append(jnp.broadcast_to(acs[:, h:h + 1], (q, LANES)))
            wb.append(jnp.broadcast_to(w_end[:, h:h + 1], (q, LANES)))
            seg = colb[-1] - acs_t[h:h + 1, :]
            decay = jnp.where(causal, jnp.exp(jnp.where(causal, seg, 0.0)), 0.0)
            m.append((cb * (decay * dt_t[h:h + 1, :])).astype(BF16))
        ea_scr[:, cs] = jnp.where(low, colb[0], colb[1])
        ew_scr[:, cs] = jnp.where(low, wb[0], wb[1])
        xk = xc_scr[:, cs]
        zero = jnp.zeros_like(xk)
        rhs = jnp.concatenate([jnp.where(low, xk, zero), jnp.where(low, zero, xk)], axis=0).astype(BF16)
        y_scr[:, cs] = _dot(jnp.concatenate(m, axis=1), rhs)

    x = xc_scr[:, 0:SSD_INNER]
    ea = ea_scr[...]
    w = (x * ew_scr[...]).astype(BF16)
    s_in = st_scr[...]
    sb = s_in.astype(BF16)
    y_off, c_state = [], []
    for g in range(SSD_GROUPS):
        gs = slice(g * half, (g + 1) * half)
        bgt = xc_scr[:, SSD_INNER + g * SSD_N:SSD_INNER + (g + 1) * SSD_N].T.astype(BF16)
        y_off.append(_dot(cgb[g], sb[:, gs]))
        c_state.append(_dot(bgt, w[:, gs]))
    st_scr[...] = jnp.exp(ea[q - 1:q, :]) * s_in + jnp.concatenate(c_state, axis=1)
    y = y_scr[...] + jnp.concatenate(y_off, axis=1) * jnp.exp(ea) + dexp_ref[...] * x

    yz = y * _silu(z_ref[...])
    y_ref[...] = _rms_norm(yz, ng_ref[...]).astype(BF16)

    @pl.when(pl.program_id(1) == pl.num_programs(1) - 1)
    def _():
        st_ref[...] = st_scr[...].T


def _ssd_call(xbc, z, dtr, p):
    nb, L, _ = xbc.shape
    q = SSD_Q
    vec = lambda b, c: (0, 0)
    row = lambda b, c: (b, c, 0)
    return pl.pallas_call(
        _ssd_kernel,
        out_shape=[jax.ShapeDtypeStruct((nb, L, SSD_INNER), BF16),
                   jax.ShapeDtypeStruct((nb, SSD_INNER, SSD_N), F32)],
        grid=(nb, L // q),
        in_specs=[pl.BlockSpec((None, q, SSD_CONV_DIM), row), pl.BlockSpec((None, q, SSD_INNER), row),
                  pl.BlockSpec((None, q, LANES), row),
                  pl.BlockSpec((1, LANES), vec), pl.BlockSpec((1, LANES), vec),
                  pl.BlockSpec((1, SSD_INNER), vec), pl.BlockSpec((1, SSD_INNER), vec)],
        out_specs=[pl.BlockSpec((None, q, SSD_INNER), row),
                   pl.BlockSpec((None, SSD_INNER, SSD_N), lambda b, c: (b, 0, 0))],
        scratch_shapes=[pltpu.VMEM((SSD_N, SSD_INNER), F32), pltpu.VMEM((q, SSD_INNER), F32),
                        pltpu.VMEM((q, SSD_INNER), F32), pltpu.VMEM((q, SSD_INNER), F32)],
        compiler_params=_cparams(("parallel", "arbitrary")),
        name="ssd",
    )(xbc, z, dtr, p["dtb"], p["alog"], p["dexp"], p["ng"])


def _srow_kernel(xg_ref, xbc_ref, dte_ref, lconv_ref, h0_ref, sconv_ref,
                 lcw_ref, lcb_ref, wa_ref, wx_ref, ba_ref, bx_ref, lam_ref, lng_ref,
                 scw_ref, scb_ref, dtbe_ref, aloge_ref,
                 yl_ref, h_ref, xs_ref, bm_ref, cm_ref, xdtt_ref, dect_ref, lcn_ref, scn_ref, xc_scr):
    xl = xg_ref[:, 0:LRU_W]
    acc = lcb_ref[...] + xl * lcw_ref[CONV_W - 1:CONV_W, :]
    for k in range(CONV_W - 1):
        acc = acc + lconv_ref[:, k * LRU_W:(k + 1) * LRU_W] * lcw_ref[k:k + 1, :]
    xc_scr[...] = acc
    for k in range(CONV_W - 2):
        lcn_ref[:, k, :] = lconv_ref[:, (k + 1) * LRU_W:(k + 2) * LRU_W]
        scn_ref[:, k, :] = sconv_ref[:, (k + 1) * SSD_CONV_DIM:(k + 2) * SSD_CONV_DIM]
    lcn_ref[:, CONV_W - 2, :] = xl
    scn_ref[:, CONV_W - 2, :] = xbc_ref[...]
    sp = _softplus(-lam_ref[...])
    for j in range(LRU_W // LRU_PACK_W):
        cs = slice(j * LRU_PACK_W, (j + 1) * LRU_PACK_W)
        xc = xc_scr[:, cs]
        a, u = _lru_gates(xc, xc.astype(BF16), wa_ref[j], wx_ref[j], ba_ref[:, cs], bx_ref[:, cs], sp[:, cs])
        h_ref[:, cs] = a * h0_ref[:, cs] + u
    y = h_ref[...] * _gelu_tanh(xg_ref[:, LRU_W:2 * LRU_W])
    yl_ref[...] = _rms_norm(y, lng_ref[...]).astype(BF16)

    acc = scb_ref[...] + xbc_ref[...] * scw_ref[CONV_W - 1:CONV_W, :]
    for k in range(CONV_W - 1):
        acc = acc + sconv_ref[:, k * SSD_CONV_DIM:(k + 1) * SSD_CONV_DIM] * scw_ref[k:k + 1, :]
    xc = _silu(acc)
    xs = xc[:, 0:SSD_INNER]
    xs_ref[...] = xs
    bm_ref[...] = xc[:, SSD_INNER:SSD_INNER + SSD_GROUPS * SSD_N]
    cm_ref[...] = xc[:, SSD_INNER + SSD_GROUPS * SSD_N:]
    dt = _softplus(dte_ref[...] + dtbe_ref[...])
    dec = jnp.exp(dt * (-jnp.exp(aloge_ref[...])))
    xdtt = (xs * dt).T
    dect = dec.T
    for j in range(xdtt_ref.shape[0]):
        xdtt_ref[j] = xdtt[:, j * SUBLANES:(j + 1) * SUBLANES]
        dect_ref[j] = dect[:, j * SUBLANES:(j + 1) * SUBLANES]


def _srow_call(xg, xbc, dte, lconv, h0, sconv, lp, sp):
    n = xg.shape[0]
    args = (xg, xbc, dte, lconv, h0, sconv, lp["cw"], lp["cb"], lp["wa"], lp["wx"], lp["ba"], lp["bx"],
            lp["lam"], lp["ng"], sp["cw"], sp["cb"], sp["dtbe"], sp["aloge"])
    full = lambda a: pl.BlockSpec(a.shape, lambda i, nd=a.ndim: (0,) * nd)
    outs = [jax.ShapeDtypeStruct((n, LRU_W), BF16), jax.ShapeDtypeStruct((n, LRU_W), F32),
            jax.ShapeDtypeStruct((n, SSD_INNER), F32), jax.ShapeDtypeStruct((n, SSD_GROUPS * SSD_N), F32),
            jax.ShapeDtypeStruct((n, SSD_GROUPS * SSD_N), F32),
            jax.ShapeDtypeStruct((n // SUBLANES, SSD_INNER, SUBLANES), F32),
            jax.ShapeDtypeStruct((n // SUBLANES, SSD_INNER, SUBLANES), F32),
            jax.ShapeDtypeStruct((n, CONV_W - 1, LRU_W), F32), jax.ShapeDtypeStruct((n, CONV_W - 1, SSD_CONV_DIM), F32)]
    return pl.pallas_call(
        _srow_kernel,
        out_shape=outs,
        grid=(1,),
        in_specs=[full(a) for a in args],
        out_specs=[pl.BlockSpec(o.shape, lambda i, nd=len(o.shape): (0,) * nd) for o in outs],
        scratch_shapes=[pltpu.VMEM((n, LRU_W), F32)],
        compiler_params=_cparams(("arbitrary",)),
        name="srow",
    )(*args)


def _sstate_kernel(s0_ref, xq_ref, dq_ref, bm_ref, cm_ref, xs_ref, z_ref, dexp_ref, ng_ref,
                   sn_ref, ys_ref, yraw):
    nb = s0_ref.shape[0]
    half = SSD_INNER // SSD_GROUPS
    for bi in range(nb):
        brow = jnp.concatenate(
            [jnp.broadcast_to(bm_ref[bi:bi + 1, g * SSD_N:(g + 1) * SSD_N], (half, SSD_N)) for g in range(SSD_GROUPS)],
            axis=0)
        s = dq_ref[:, bi:bi + 1] * s0_ref[bi] + xq_ref[:, bi:bi + 1] * brow
        sn_ref[bi] = s
        sb = s.astype(BF16)
        for g in range(SSD_GROUPS):
            cg = cm_ref[:, g * SSD_N:(g + 1) * SSD_N].astype(BF16)
            res = _dot_nt(cg, sb[g * half:(g + 1) * half, :])
            yraw[bi:bi + 1, g * half:(g + 1) * half] = res[bi:bi + 1, :]
    y = yraw[...] + dexp_ref[...] * xs_ref[...]
    ys_ref[...] = _rms_norm(y * _silu(z_ref[...]), ng_ref[...]).astype(BF16)


def _sstate_call(s0, xq, dq, bm, cm, xs, z, dexp, ng, nb):
    n = s0.shape[0]
    row = lambda i: (i, 0)
    vec = lambda i: (0, 0)
    gn = SSD_GROUPS * SSD_N
    return pl.pallas_call(
        _sstate_kernel,
        out_shape=[jax.ShapeDtypeStruct(s0.shape, F32), jax.ShapeDtypeStruct((n, SSD_INNER), BF16)],
        grid=(n // nb,),
        in_specs=[pl.BlockSpec((nb, SSD_INNER, SSD_N), lambda i: (i, 0, 0)),
                  pl.BlockSpec((None, SSD_INNER, nb), lambda i: (i, 0, 0)),
                  pl.BlockSpec((None, SSD_INNER, nb), lambda i: (i, 0, 0)),
                  pl.BlockSpec((nb, gn), row), pl.BlockSpec((nb, gn), row),
                  pl.BlockSpec((nb, SSD_INNER), row), pl.BlockSpec((nb, SSD_INNER), row),
                  pl.BlockSpec((1, SSD_INNER), vec), pl.BlockSpec((1, SSD_INNER), vec)],
        out_specs=[pl.BlockSpec((nb, SSD_INNER, SSD_N), lambda i: (i, 0, 0)), pl.BlockSpec((nb, SSD_INNER), row)],
        scratch_shapes=[pltpu.VMEM((nb, SSD_INNER), F32)],
        compiler_params=_cparams(("parallel",)),
        name="sstate",
    )(s0, xq, dq, bm, cm, xs, z, dexp, ng)


def _post_kernel(yl_ref, ys_ref, x_ref, g1_ref, sh2_ref, sc2_ref, wo_ref, l1g_ref, l1b_ref, wrt_ref, brc_ref,
                 x1_ref, v_ref, eid_ref, gw_ref, *, per_batch):
    b = pl.program_id(0)
    o = _dot(yl_ref[...], wo_ref[0:LRU_W, :]) + _dot(ys_ref[...], wo_ref[LRU_W:LRU_W + SSD_INNER, :])
    x1 = _layer_norm(ALPHA * x_ref[...] + _mod_rows(g1_ref, per_batch, b) * o) * l1g_ref[...] + l1b_ref[...]
    x1_ref[...] = x1
    v = _layer_norm(x1) * (1.0 + _mod_rows(sc2_ref, per_batch, b)) + _mod_rows(sh2_ref, per_batch, b)
    _store_row_tiles(v_ref, v)

    lt = _dot(v.astype(BF16), wrt_ref[...]).T + brc_ref[...]
    tl = lt.shape[1]
    row = lax.broadcasted_iota(I32, (GROUP_SIZE, tl), 0).astype(F32)
    big = float(GROUP_SIZE)
    neg = -jnp.inf
    lg = jnp.where(row < N_GROUPS, lt[0:GROUP_SIZE, :], neg)
    gmax = jnp.max(lg, axis=0, keepdims=True)
    gsel = jnp.min(jnp.where(lg == gmax, row, big), axis=0, keepdims=True)
    pg = 1.0 / jnp.sum(jnp.exp(lg - gmax), axis=0, keepdims=True)
    le = lt[GROUP_SIZE:2 * GROUP_SIZE, :]
    for j in range(1, N_GROUPS):
        le = jnp.where(gsel == j, lt[GROUP_SIZE * (j + 1):GROUP_SIZE * (j + 2), :], le)
    m1 = jnp.max(le, axis=0, keepdims=True)
    i1 = jnp.min(jnp.where(le == m1, row, big), axis=0, keepdims=True)
    rest = jnp.where(row == i1, neg, le)
    m2 = jnp.max(rest, axis=0, keepdims=True)
    i2 = jnp.min(jnp.where(rest == m2, row, big), axis=0, keepdims=True)
    e2 = jnp.exp(m2 - m1)
    den = 1.0 + e2
    eid = jnp.where(row == 0, gsel * GROUP_SIZE + i1, jnp.where(row == 1, gsel * GROUP_SIZE + i2, 0.0))
    eid_ref[...] = eid.astype(I32)
    gw_ref[...] = jnp.where(row == 0, pg * (1.0 / den), jnp.where(row == 1, pg * (e2 / den), 0.0))


def _post_call(yl, ys, x3, mod, p, tl):
    nb, L, _ = x3.shape
    nl = L // tl
    row = lambda b, l: (b, l, 0)
    vec = lambda b, l: (0, 0)
    tok = lambda b, l: (b * nl + l, 0)
    tokt = lambda b, l: (0, b * nl + l)
    bof, tof = (lambda b, l: b), (lambda b, l: l)
    return pl.pallas_call(
        functools.partial(_post_kernel, per_batch=mod.per_batch),
        out_shape=[jax.ShapeDtypeStruct((nb, L, D), F32), jax.ShapeDtypeStruct(_tile_rows(nb * L), F32),
                   jax.ShapeDtypeStruct((SUBLANES, nb * L), I32), jax.ShapeDtypeStruct((SUBLANES, nb * L), F32)],
        grid=(nb, nl),
        in_specs=[pl.BlockSpec((None, tl, LRU_W), row), pl.BlockSpec((None, tl, SSD_INNER), row),
                  pl.BlockSpec((None, tl, D), row),
                  mod.spec(2, tl, bof, tof), mod.spec(3, tl, bof, tof), mod.spec(4, tl, bof, tof),
                  pl.BlockSpec((LRU_W + SSD_INNER, D), vec), pl.BlockSpec((1, D), vec), pl.BlockSpec((1, D), vec),
                  pl.BlockSpec((D, LANES), vec), pl.BlockSpec((LANES, 1), vec)],
        out_specs=[pl.BlockSpec((None, tl, D), row), pl.BlockSpec(_tile_rows(tl), tok),
                   pl.BlockSpec((SUBLANES, tl), tokt), pl.BlockSpec((SUBLANES, tl), tokt)],
        compiler_params=_cparams(("parallel", "arbitrary")),
        name="post",
    )(yl, ys, x3, mod.table, mod.table, mod.table, p["wo"], p["l1g"], p["l1b"], p["wrt"], p["brc"])


def _route_kernel(eidp_ref, eids_ref, dest_ref, cnt_ref, run, poff, *, p_tiles):
    ph = pl.program_id(0)
    t = pl.program_id(1)
    n = ROUTE_TILE
    eid = jnp.where(t < p_tiles, eidp_ref[...], eids_ref[...])
    rowi = lax.broadcasted_iota(I32, (LANES, n), 0)
    oh0 = rowi == eid[0:1, :]
    oh1 = rowi == eid[1:2, :]
    oh = oh0.astype(F32) + oh1.astype(F32)
    tile_cnt = jnp.sum(oh, axis=1, keepdims=True)

    @pl.when((ph == 0) & (t == 0))
    def _():
        run[...] = jnp.zeros((LANES, n), F32)

    @pl.when(ph == 0)
    def _():
        run[...] = run[...] + tile_cnt

    @pl.when((ph == 1) & (t == 0))
    def _():
        counts = run[...]
        cnt_ref[...] = counts
        ci = counts.astype(I32)
        q = jnp.floor(counts * (1.0 / MOE_BM)).astype(I32)
        rem = ci - q * MOE_BM
        q = q + jnp.where(rem >= MOE_BM, 1, 0) - jnp.where(rem < 0, 1, 0)
        nblk = q + jnp.where(ci - q * MOE_BM > 0, 1, 0)
        r = lax.broadcasted_iota(I32, (LANES, LANES), 0)
        c = lax.broadcasted_iota(I32, (LANES, LANES), 1)
        lower = (r > c).astype(BF16)
        poff[...] = _dot(lower, nblk.astype(F32).astype(BF16)) * float(MOE_BM)
        run[...] = jnp.zeros((LANES, n), F32)

    @pl.when(ph == 1)
    def _():
        r = lax.broadcasted_iota(I32, (n, n), 0)
        c = lax.broadcasted_iota(I32, (n, n), 1)
        before = (r < c).astype(BF16)
        slot = _dot(oh.astype(BF16), before) + run[...] + poff[...]
        d0 = jnp.sum(jnp.where(oh0, slot, 0.0), axis=0, keepdims=True)
        d1 = jnp.sum(jnp.where(oh1, slot, 0.0), axis=0, keepdims=True)
        row = lax.broadcasted_iota(I32, (SUBLANES, n), 0)
        dest_ref[...] = jnp.where(row == 0, d0, jnp.where(row == 1, d1, 0.0)).astype(I32)
        run[...] = run[...] + tile_cnt


def _route_call(eid_p, eid_s):
    n = ROUTE_TILE
    p_tiles = eid_p.shape[1] // n
    s_tiles = eid_s.shape[1] // n
    t_all = (p_tiles + s_tiles) * n
    return pl.pallas_call(
        functools.partial(_route_kernel, p_tiles=p_tiles),
        out_shape=[jax.ShapeDtypeStruct((SUBLANES, t_all), I32), jax.ShapeDtypeStruct((LANES, n), F32)],
        grid=(2, p_tiles + s_tiles),
        in_specs=[pl.BlockSpec((SUBLANES, n), lambda ph, t: (0, jnp.minimum(t, p_tiles - 1))),
                  pl.BlockSpec((SUBLANES, n), lambda ph, t: (0, jnp.maximum(t - p_tiles, 0)))],
        out_specs=[pl.BlockSpec((SUBLANES, n), lambda ph, t: (0, t * ph)),
                   pl.BlockSpec((LANES, n), lambda ph, t: (0, 0))],
        scratch_shapes=[pltpu.VMEM((LANES, n), F32), pltpu.VMEM((LANES, n), F32)],
        compiler_params=_cparams(("arbitrary", "arbitrary")),
        name="route",
    )(eid_p, eid_s)


def _row_copy(src, src_row, dst, dst_row, sem):
    s0 = pl.multiple_of(src_row * SUBLANES, SUBLANES)
    d0 = pl.multiple_of(dst_row * SUBLANES, SUBLANES)
    return pltpu.make_async_copy(src.at[pl.ds(s0, SUBLANES), :], dst.at[pl.ds(d0, SUBLANES), :], sem)


def _scatter_rows(dest_ref, v_ref, xpad_ref, sem):
    def start(r, carry):
        for k in range(2):
            _row_copy(v_ref, r, xpad_ref, dest_ref[k, r], sem).start(priority=k)
        return carry

    def wait(r, carry):
        for k in range(2):
            _row_copy(v_ref, r, xpad_ref, dest_ref[k, r], sem).wait()
        return carry

    lax.fori_loop(0, TOK_TILE, start, 0, unroll=DMA_UNROLL)
    lax.fori_loop(0, TOK_TILE, wait, 0, unroll=DMA_UNROLL)


def _dispatch_kernel(zflag_ref, dest_ref, vp_ref, vs_ref, xpad_ref, zbuf, sem, zsem, *, p_tiles, n_blocks):
    t = pl.program_id(0)
    blk_rows = _tile_rows(MOE_BM)[0]

    @pl.when(t == 0)
    def _():
        zbuf[...] = jnp.zeros(zbuf.shape, F32)

        def zero_block(go):
            def body(b, carry):
                @pl.when(zflag_ref[b] != 0)
                def _():
                    r0 = pl.multiple_of(b * blk_rows, blk_rows)
                    cp = pltpu.make_async_copy(zbuf, xpad_ref.at[pl.ds(r0, blk_rows), :], zsem)
                    if go:
                        cp.start()
                    else:
                        cp.wait()
                return carry
            lax.fori_loop(0, n_blocks, body, 0)

        zero_block(True)
        zero_block(False)

    @pl.when(t < p_tiles)
    def _():
        _scatter_rows(dest_ref, vp_ref, xpad_ref, sem)

    @pl.when(t >= p_tiles)
    def _():
        _scatter_rows(dest_ref, vs_ref, xpad_ref, sem)


def _dispatch_call(zflag, dest, v_p, v_s, n_blocks):
    p_tiles = v_p.shape[0] // _tile_rows(TOK_TILE)[0]
    s_tiles = v_s.shape[0] // _tile_rows(TOK_TILE)[0]
    return pl.pallas_call(
        functools.partial(_dispatch_kernel, p_tiles=p_tiles, n_blocks=n_blocks),
        out_shape=jax.ShapeDtypeStruct(_tile_rows(n_blocks * MOE_BM), F32),
        grid_spec=pltpu.PrefetchScalarGridSpec(
            num_scalar_prefetch=1,
            grid=(p_tiles + s_tiles,),
            in_specs=[pl.BlockSpec((SUBLANES, TOK_TILE), lambda t, z: (0, t), memory_space=pltpu.SMEM),
                      pl.BlockSpec(_tile_rows(TOK_TILE), lambda t, z: (jnp.minimum(t, p_tiles - 1), 0)),
                      pl.BlockSpec(_tile_rows(TOK_TILE), lambda t, z: (jnp.maximum(t - p_tiles, 0), 0))],
            out_specs=pl.BlockSpec(memory_space=pl.ANY),
            scratch_shapes=[pltpu.VMEM(_tile_rows(MOE_BM), F32), pltpu.SemaphoreType.DMA, pltpu.SemaphoreType.DMA]),
        compiler_params=_cparams(("arbitrary",)),
        name="dispatch",
    )(zflag, dest, v_p, v_s)


def _expert_kernel(be_ref, nb_ref, nxt_ref, slot_ref, x_ref, wg_hbm, wu_hbm, wd_hbm, o_ref,
                   wgf, wuf, wdf, wgb, wub, wdb, sems):
    i = pl.program_id(0)

    def weights(e, slot, go):
        for m, (hbm, buf) in enumerate(((wg_hbm, wgf), (wu_hbm, wuf), (wd_hbm, wdf))):
            cp = pltpu.make_async_copy(hbm.at[e], buf.at[slot], sems.at[slot, m])
            if go:
                cp.start()
            else:
                cp.wait()

    @pl.when(i < nb_ref[0])
    def _():
        e = be_ref[i]
        slot = slot_ref[e]

        @pl.when(i == 0)
        def _():
            weights(e, slot, True)

        @pl.when((i == 0) | (e != be_ref[jnp.maximum(i - 1, 0)]))
        def _():
            weights(e, slot, False)
            nxt = nxt_ref[e]

            @pl.when(nxt >= 0)
            def _():
                weights(nxt, 1 - slot, True)

            wgb[...] = wgf[slot].astype(BF16)
            wub[...] = wuf[slot].astype(BF16)
            wdb[...] = wdf[slot].astype(BF16)

        x = _load_row_tiles(x_ref).astype(BF16)
        h = _silu(_dot(x, wgb[...])) * _dot(x, wub[...])
        _store_row_tiles(o_ref, _dot(h.astype(BF16), wdb[...]))

    @pl.when(i >= nb_ref[0])
    def _():
        o_ref[...] = jnp.zeros(o_ref.shape, F32)


def _expert_call(blk_e, nblk, nxt_e, slot_e, xpad, w_gate, w_up, w_down):
    n_rows = xpad.shape[0] // SUBLANES
    blk = lambda i, be, nb, nx, sl: (jnp.minimum(i, nb[0] - 1), 0)
    hbm = pl.BlockSpec(memory_space=pl.ANY)
    return pl.pallas_call(
        _expert_kernel,
        out_shape=jax.ShapeDtypeStruct(_tile_rows(n_rows), F32),
        grid_spec=pltpu.PrefetchScalarGridSpec(
            num_scalar_prefetch=4,
            grid=(n_rows // MOE_BM,),
            in_specs=[pl.BlockSpec(_tile_rows(MOE_BM), blk), hbm, hbm, hbm],
            out_specs=pl.BlockSpec(_tile_rows(MOE_BM), lambda i, be, nb, nx, sl: (i, 0)),
            scratch_shapes=[pltpu.VMEM((2, D, D_FF), F32), pltpu.VMEM((2, D, D_FF), F32), pltpu.VMEM((2, D_FF, D), F32),
                            pltpu.VMEM((D, D_FF), BF16), pltpu.VMEM((D, D_FF), BF16), pltpu.VMEM((D_FF, D), BF16),
                            pltpu.SemaphoreType.DMA((2, 3))]),
        compiler_params=_cparams(("arbitrary",)),
        name="expert",
    )(blk_e, nblk, nxt_e, slot_e, xpad, w_gate, w_up, w_down)


def _combine_kernel(dest_ref, gw0_ref, gw1_ref, x1_ref, g2_ref, l2g_ref, l2b_ref, ypad_ref, y_ref, ybuf, sems,
                    *, tok_off, k_stride, tl, nl, per_batch):
    i = pl.program_id(0)

    def gather(tile, slot, go):
        base = tok_off + tile * tl

        def body(r, carry):
            for k in range(2):
                cp = _row_copy(ypad_ref, dest_ref[k * k_stride + base + r], ybuf.at[slot, k], r, sems.at[slot])
                if go:
                    cp.start(priority=k)
                else:
                    cp.wait()
            return carry

        lax.fori_loop(0, tl, body, 0, unroll=DMA_UNROLL)

    @pl.when(i == 0)
    def _():
        gather(0, 0, True)

    @pl.when(i + 1 < pl.num_programs(0))
    def _():
        gather(i + 1, (i + 1) % 2, True)

    slot = i % 2
    gather(i, slot, False)
    f = _load_row_tiles(ybuf.at[slot, 0]) * gw0_ref[...] + _load_row_tiles(ybuf.at[slot, 1]) * gw1_ref[...]
    g2 = _mod_rows(g2_ref, per_batch, i // nl)
    y_ref[...] = _layer_norm(ALPHA * x1_ref[...] + g2 * f) * l2g_ref[...] + l2b_ref[...]


def _combine_call(dest, gw0, gw1, x1, mod, l2g, l2b, ypad, tok_off, tl):
    nb, L, _ = x1.shape
    nl = L // tl
    row = lambda i, d: (i // nl, i % nl, 0)
    vec = lambda i, d: (0, 0)
    tok = lambda i, d: (i, 0)
    g2_spec = mod.spec(5, tl, lambda i, d: i // nl, lambda i, d: i % nl)
    return pl.pallas_call(
        functools.partial(_combine_kernel, tok_off=tok_off, k_stride=dest.shape[1], tl=tl, nl=nl,
                          per_batch=mod.per_batch),
        out_shape=jax.ShapeDtypeStruct((nb, L, D), F32),
        grid_spec=pltpu.PrefetchScalarGridSpec(
            num_scalar_prefetch=1,
            grid=(nb * nl,),
            in_specs=[pl.BlockSpec((tl, 1), tok), pl.BlockSpec((tl, 1), tok),
                      pl.BlockSpec((None, tl, D), row), g2_spec,
                      pl.BlockSpec((1, D), vec), pl.BlockSpec((1, D), vec),
                      pl.BlockSpec(memory_space=pl.ANY)],
            out_specs=pl.BlockSpec((None, tl, D), row),
            scratch_shapes=[pltpu.VMEM((2, 2) + _tile_rows(tl), F32), pltpu.SemaphoreType.DMA((2,))]),
        compiler_params=_cparams(("arbitrary",)),
        name="combine",
    )(dest[:2].reshape(-1), gw0, gw1, x1, mod.table, l2g, l2b, ypad)


def _block_diag(w):
    nh, blk, _ = w.shape
    w4 = w.reshape(nh // LRU_PACK, LRU_PACK, blk, blk)
    eye = jnp.eye(LRU_PACK, dtype=w.dtype)
    out = jnp.einsum("gaij,ab->gaibj", w4, eye)
    return out.reshape(nh // LRU_PACK, LRU_PACK * blk, LRU_PACK * blk).astype(BF16)


def _row(v):
    return v.reshape(1, -1).astype(F32)


def _pad_lanes(v):
    return jnp.pad(v.reshape(1, -1).astype(F32), ((0, 0), (0, LANES - v.shape[-1])))


def kernel(x_prompt, x_sample, c_prompt, c_sample, state_lru_conv, state_lru_h, state_ssd_conv, state_ssd, w_ada, b_ada, w_in, lru_conv_w, lru_conv_b, lru_wa, lru_ba, lru_wx, lru_bx, lru_lambda, lru_norm_g, ssd_conv_w, ssd_conv_b, ssd_dt_bias, ssd_a_log, ssd_d, ssd_norm_g, w_out, ln1_g, ln1_b, w_rg, b_rg, w_re, b_re, w_gate, w_up, w_down, ln2_g, ln2_b):
    assert w_ada.shape[0] == DEPTH == 1
    nbp, seq, _ = x_prompt.shape
    nbs = x_sample.shape[0]
    t_p = nbp * seq
    t_all = t_p + nbs
    assert x_sample.shape[1] == 1 and t_p % ROUTE_TILE == 0 and nbs % TOK_TILE == 0 and nbp == SUBLANES

    c_rows = -(-(nbp + nbs) // 16) * 16
    c_all = jnp.pad(jnp.concatenate([c_sample, c_prompt], axis=0), ((0, c_rows - nbp - nbs), (0, 0)))
    w_in0 = w_in[0]
    o_z, o_xbc, o_dt = 2 * LRU_W, 2 * LRU_W + SSD_INNER, 2 * LRU_W + SSD_INNER + SSD_CONV_DIM
    w_dt = w_in0[:, o_dt:]
    ws_p = list(_wcast_call(w_in0, ((0, o_z), (o_z, o_xbc), (o_xbc, o_dt)), 256))
    ws_s = ws_p + [jnp.repeat(w_dt, SSD_P, axis=1).astype(BF16)]
    ws_p = ws_p + [jnp.pad(w_dt, ((0, 0), (0, LANES - SSD_HEADS))).astype(BF16)]
    lp = dict(cw=lru_conv_w[0], cb=_row(lru_conv_b[0]), wa=_block_diag(lru_wa[0]), wx=_block_diag(lru_wx[0]),
              ba=_row(lru_ba[0]), bx=_row(lru_bx[0]), lam=_row(lru_lambda[0]), ng=_row(lru_norm_g[0]))
    sp = dict(cw=ssd_conv_w[0], cb=_row(ssd_conv_b[0]), dtb=_pad_lanes(ssd_dt_bias[0]), alog=_pad_lanes(ssd_a_log[0]),
              dexp=_row(jnp.repeat(ssd_d[0], SSD_P)), ng=_row(ssd_norm_g[0]),
              dtbe=_row(jnp.repeat(ssd_dt_bias[0], SSD_P)), aloge=_row(jnp.repeat(ssd_a_log[0], SSD_P)))
    wrt = jnp.zeros((D, LANES), F32).at[:, 0:N_GROUPS].set(w_rg[0]).at[:, GROUP_SIZE:ROUTE_ROWS].set(w_re[0]).astype(BF16)
    brc = jnp.zeros((LANES, 1), F32).at[0:N_GROUPS, 0].set(b_rg[0]).at[GROUP_SIZE:ROUTE_ROWS, 0].set(b_re[0])
    pp = dict(wo=w_out[0].astype(BF16), l1g=_row(ln1_g[0]), l1b=_row(ln1_b[0]), wrt=wrt, brc=brc)

    table = _ada_call(c_all, w_ada[0], _row(b_ada[0]))
    mod_p = _Mod(table, nbs, True)
    mod_s = _Mod(table, 0, False)

    xg, z, xbc, dtr, p_lru_conv, p_ssd_conv = _proj_conv_call(x_prompt, mod_p, ws_p, lp, sp, SEQ_TILE)
    yl, p_lru_h = _lru_call(xg, lp, SEQ_TILE)
    ys, p_ssd = _ssd_call(xbc, z, dtr, sp)
    x1_p, v_p, eid_p, gw_p = _post_call(yl, ys, x_prompt, mod_p, pp, SEQ_TILE)

    xs3 = x_sample.reshape(1, nbs, D)
    xg_s, z_s, xbc_s, dte_s = _proj_call(xs3, mod_s, ws_s, nbs)
    xg_s, z_s, xbc_s, dte_s = xg_s[0], z_s[0], xbc_s[0], dte_s[0]
    yl_s, h_s, xs_s, bm_s, cm_s, xdtt, dect, s_lru_conv, s_ssd_conv = _srow_call(
        xg_s, xbc_s, dte_s, state_lru_conv[0].reshape(nbs, -1), state_lru_h[0],
        state_ssd_conv[0].reshape(nbs, -1), lp, sp)
    s_new, ys_s = _sstate_call(state_ssd[0].reshape(nbs, SSD_INNER, SSD_N), xdtt, dect,
                               bm_s, cm_s, xs_s, z_s, sp["dexp"], sp["ng"], SUBLANES)
    x1_s, v_s, eid_s, gw_s = _post_call(yl_s[None], ys_s[None], xs3, mod_s, pp, nbs)

    s_pad = -nbs % ROUTE_TILE
    dest, counts = _route_call(eid_p, jnp.pad(eid_s, ((0, 0), (0, s_pad)), constant_values=-1))
    counts = counts[:N_EXPERTS, 0].astype(I32)
    n_blocks = -(-(2 * t_all) // MOE_BM) + N_EXPERTS
    pend = jnp.cumsum(((counts + MOE_BM - 1) // MOE_BM) * MOE_BM)
    blk_start = jnp.arange(n_blocks, dtype=I32) * MOE_BM
    blk_e = jnp.minimum(jnp.sum((pend[None, :] <= blk_start[:, None]).astype(I32), axis=1), N_EXPERTS - 1)
    nblk = (pend[-1:] // MOE_BM).astype(I32)
    blk_ids = jnp.arange(n_blocks, dtype=I32)
    last_of_expert = jnp.any(((pend // MOE_BM - 1)[None, :] == blk_ids[:, None]) & (counts > 0)[None, :], axis=1)
    zflag = (last_of_expert | (blk_ids >= nblk[0])).astype(I32)
    used = counts > 0
    eids = jnp.arange(N_EXPERTS, dtype=I32)
    later_used = (eids[None, :] > eids[:, None]) & used[None, :]
    nxt_e = jnp.min(jnp.where(later_used, eids[None, :], N_EXPERTS), axis=1)
    nxt_e = jnp.where(nxt_e == N_EXPERTS, -1, nxt_e).astype(I32)
    slot_e = ((jnp.cumsum(used.astype(I32)) - used.astype(I32)) % 2).astype(I32)
    xpad = _dispatch_call(zflag, dest, v_p, v_s, n_blocks)
    ypad = _expert_call(blk_e, nblk, nxt_e, slot_e, xpad, w_gate[0], w_up[0], w_down[0])
    l2g, l2b = _row(ln2_g[0]), _row(ln2_b[0])
    y_p = _combine_call(dest, gw_p[0].reshape(t_p, 1), gw_p[1].reshape(t_p, 1), x1_p, mod_p, l2g, l2b, ypad, 0,
                        2 * TOK_TILE)
    y_s = _combine_call(dest, gw_s[0].reshape(nbs, 1), gw_s[1].reshape(nbs, 1), x1_s, mod_s, l2g, l2b, ypad, t_p,
                        TOK_TILE)

    return (y_p, y_s.reshape(nbs, 1, D),
            p_lru_conv[None], p_lru_h.reshape(1, nbp, LRU_W), p_ssd_conv[None],
            p_ssd.reshape(1, nbp, SSD_HEADS, SSD_P, SSD_N),
            s_lru_conv[None], h_s[None], s_ssd_conv[None],
            s_new.reshape(1, nbs, SSD_HEADS, SSD_P, SSD_N))
```

```python
import functools
import math

import jax
import jax.numpy as jnp
from jax import lax
from jax.experimental import pallas as pl
from jax.experimental.pallas import tpu as pltpu

F32 = jnp.float32
BF16 = jnp.bfloat16
I32 = jnp.int32
HIGHEST = lax.Precision.HIGHEST

D = 1024
DEPTH = 1
CONV_W = 4
LRU_W = D
LRU_HEADS = 16
LRU_C = 8.0
LRU_PACK = 4
LRU_PACK_W = LRU_PACK * (LRU_W // LRU_HEADS)
SSD_INNER = D
SSD_HEADS = 16
SSD_P = SSD_INNER // SSD_HEADS
SSD_GROUPS = 2
SSD_N = 128
SSD_Q = 128
SSD_CONV_DIM = SSD_INNER + 2 * SSD_GROUPS * SSD_N
N_GROUPS = 4
GROUP_SIZE = 8
N_EXPERTS = N_GROUPS * GROUP_SIZE
D_FF = D // 2
N_MOD = 6
LN_EPS = 1e-5
RMS_EPS = 1e-6
ALPHA = (2.0 * DEPTH) ** 0.25

LANES = 128
SUBLANES = 8
VMEM_LIMIT = 56 * 1024 * 1024
SEQ_TILE = 512
TOK_TILE = 128
ROUTE_TILE = 1024
DMA_UNROLL = 8
MOE_BM = 384
ROUTE_ROWS = 40

NT_DIMS = (((1,), (1,)), ((), ()))


def _tile_rows(rows):
    return (rows * SUBLANES, LANES)


def _cparams(sem):
    return pltpu.CompilerParams(dimension_semantics=sem, vmem_limit_bytes=VMEM_LIMIT)


def _sigmoid(x):
    return 0.5 * (jnp.tanh(0.5 * x) + 1.0)


def _silu(x):
    return x * _sigmoid(x)


def _softplus(x):
    return jnp.maximum(x, 0.0) + jnp.log1p(jnp.exp(-jnp.abs(x)))


def _gelu_tanh(x):
    return 0.5 * x * (1.0 + jnp.tanh(math.sqrt(2.0 / math.pi) * (x + 0.044715 * (x * x * x))))


def _layer_norm(x):
    mu = jnp.mean(x, axis=-1, keepdims=True)
    xc = x - mu
    var = jnp.mean(xc * xc, axis=-1, keepdims=True)
    return xc * lax.rsqrt(var + LN_EPS)


def _rms_norm(x, g):
    return x * lax.rsqrt(jnp.mean(x * x, axis=-1, keepdims=True) + RMS_EPS) * g


def _dot(a, b, **kw):
    return jnp.dot(a, b, preferred_element_type=F32, **kw)


def _dot_nt(a, b, **kw):
    return lax.dot_general(a, b, NT_DIMS, preferred_element_type=F32, **kw)


def _store_row_tiles(ref, val):
    rows = val.shape[0]
    for j in range(SUBLANES):
        ref[pl.ds(j, rows, stride=SUBLANES), :] = val[:, j * LANES:(j + 1) * LANES]


def _load_row_tiles(ref):
    rows = ref.shape[0] // SUBLANES
    return jnp.concatenate([ref[pl.ds(j, rows, stride=SUBLANES), :] for j in range(SUBLANES)], axis=1)


def _lru_gates(xc, xb, wa, wx, ba, bx, sp):
    r = _sigmoid(_dot(xb, wa) + ba)
    i = _sigmoid(_dot(xb, wx) + bx)
    log_a = (-LRU_C) * r * sp
    a = jnp.exp(log_a)
    mult = jnp.sqrt(jnp.tanh(-log_a) * (a * a + 1.0))
    return a, mult * (i * xc)


def _ada_kernel(c_ref, w_ref, b_ref, o_ref):
    s = _silu(c_ref[...]).astype(BF16)
    o_ref[...] = _dot(s, w_ref[...].astype(BF16)) + b_ref[...]


def _ada_call(c_all, w_ada, b_ada):
    rows = c_all.shape[0]
    tn = 512
    per_mod = D // tn
    return pl.pallas_call(
        _ada_kernel,
        out_shape=jax.ShapeDtypeStruct((N_MOD, rows, D), F32),
        grid=(N_MOD * per_mod,),
        in_specs=[pl.BlockSpec((rows, D), lambda j: (0, 0)),
                  pl.BlockSpec((D, tn), lambda j: (0, j)),
                  pl.BlockSpec((1, tn), lambda j: (0, j))],
        out_specs=pl.BlockSpec((None, rows, tn), lambda j: (j // per_mod, 0, j % per_mod)),
        compiler_params=_cparams(("arbitrary",)),
        name="ada",
    )(c_all, w_ada, b_ada)


def _wcast_kernel(w_ref, *o_refs, bounds):
    for o_ref, (lo, hi) in zip(o_refs, bounds):
        o_ref[...] = w_ref[:, lo:hi].astype(BF16)


def _wcast_call(w, bounds, tr):
    rows, cols = w.shape
    assert all(lo % LANES == 0 for lo, _ in bounds)
    return pl.pallas_call(
        functools.partial(_wcast_kernel, bounds=bounds),
        out_shape=[jax.ShapeDtypeStruct((rows, hi - lo), BF16) for lo, hi in bounds],
        grid=(rows // tr,),
        in_specs=[pl.BlockSpec((tr, cols), lambda i: (i, 0))],
        out_specs=[pl.BlockSpec((tr, hi - lo), lambda i: (i, 0)) for lo, hi in bounds],
        compiler_params=_cparams(("arbitrary",)),
        name="wcast",
    )(w)


class _Mod:
    def __init__(self, table, row0, per_batch):
        self.table, self.row0, self.per_batch = table, row0, per_batch

    def spec(self, k, tl, batch_of, tile_of):
        if self.per_batch:
            blk = self.row0 // SUBLANES
            return pl.BlockSpec((None, SUBLANES, D), lambda *g: (k, blk, 0))
        blk = self.row0 // tl
        return pl.BlockSpec((None, tl, D), lambda *g: (k, blk + tile_of(*g), 0))


def _mod_rows(ref, per_batch, b):
    return ref[pl.ds(b, 1), :] if per_batch else ref[...]


def _proj_kernel(x_ref, sh_ref, sc_ref, w1_ref, w2_ref, w3_ref, w4_ref, o1_ref, o2_ref, o3_ref, o4_ref, *, per_batch):
    b = pl.program_id(0)
    u = _layer_norm(x_ref[...]) * (1.0 + _mod_rows(sc_ref, per_batch, b)) + _mod_rows(sh_ref, per_batch, b)
    ub = u.astype(BF16)
    o1_ref[...] = _dot(ub, w1_ref[...])
    o2_ref[...] = _dot(ub, w2_ref[...])
    o3_ref[...] = _dot(ub, w3_ref[...])
    o4_ref[...] = _dot(ub, w4_ref[...])


def _causal_conv(xpad, x_new, cw_ref, cb_ref, conv_ref, tl):
    xpad[SUBLANES:SUBLANES + tl, :] = x_new
    off = SUBLANES - (CONV_W - 1)
    acc = xpad[off:off + tl, :] * cw_ref[0:1, :]
    for k in range(1, CONV_W):
        acc = acc + xpad[off + k:off + k + tl, :] * cw_ref[k:k + 1, :]
    conv_ref[...] = xpad[SUBLANES + tl - (CONV_W - 1):SUBLANES + tl, :]
    xpad[0:SUBLANES, :] = xpad[tl:tl + SUBLANES, :]
    return acc + cb_ref[...]


def _proj_conv_kernel(x_ref, sh_ref, sc_ref, w1_ref, w2_ref, w3_ref, w4_ref, lcw_ref, lcb_ref, scw_ref, scb_ref,
                      o1_ref, o2_ref, o3_ref, o4_ref, lconv_ref, sconv_ref, lpad, spad, *, tl):
    b = pl.program_id(0)

    @pl.when(pl.program_id(1) == 0)
    def _():
        lpad[0:SUBLANES, :] = jnp.zeros((SUBLANES, LRU_W), F32)
        spad[0:SUBLANES, :] = jnp.zeros((SUBLANES, SSD_CONV_DIM), F32)

    u = _layer_norm(x_ref[...]) * (1.0 + _mod_rows(sc_ref, True, b)) + _mod_rows(sh_ref, True, b)
    ub = u.astype(BF16)
    o1_ref[:, 0:LRU_W] = _causal_conv(lpad, _dot(ub, w1_ref[:, 0:LRU_W]), lcw_ref, lcb_ref, lconv_ref, tl)
    o1_ref[:, LRU_W:2 * LRU_W] = _dot(ub, w1_ref[:, LRU_W:2 * LRU_W])
    o2_ref[...] = _dot(ub, w2_ref[...])
    o3_ref[...] = _silu(_causal_conv(spad, _dot(ub, w3_ref[...]), scw_ref, scb_ref, sconv_ref, tl))
    o4_ref[...] = _dot(ub, w4_ref[...])


def _proj_conv_call(x3, mod, ws, lp, sp, tl):
    nb, L, _ = x3.shape
    widths = [w.shape[1] for w in ws]
    row = lambda b, l: (b, l, 0)
    full = lambda b, l: (0, 0)
    st = lambda b, l: (b, 0, 0)
    bof, tof = (lambda b, l: b), (lambda b, l: l)
    return pl.pallas_call(
        functools.partial(_proj_conv_kernel, tl=tl),
        out_shape=[jax.ShapeDtypeStruct((nb, L, n), F32) for n in widths]
                  + [jax.ShapeDtypeStruct((nb, CONV_W - 1, LRU_W), F32),
                     jax.ShapeDtypeStruct((nb, CONV_W - 1, SSD_CONV_DIM), F32)],
        grid=(nb, L // tl),
        in_specs=[pl.BlockSpec((None, tl, D), row), mod.spec(0, tl, bof, tof), mod.spec(1, tl, bof, tof)]
                 + [pl.BlockSpec((D, n), full) for n in widths]
                 + [pl.BlockSpec((CONV_W, LRU_W), full), pl.BlockSpec((1, LRU_W), full),
                    pl.BlockSpec((CONV_W, SSD_CONV_DIM), full), pl.BlockSpec((1, SSD_CONV_DIM), full)],
        out_specs=[pl.BlockSpec((None, tl, n), row) for n in widths]
                  + [pl.BlockSpec((None, CONV_W - 1, LRU_W), st), pl.BlockSpec((None, CONV_W - 1, SSD_CONV_DIM), st)],
        scratch_shapes=[pltpu.VMEM((tl + SUBLANES, LRU_W), F32), pltpu.VMEM((tl + SUBLANES, SSD_CONV_DIM), F32)],
        compiler_params=_cparams(("parallel", "arbitrary")),
        name="projc",
    )(x3, mod.table, mod.table, *ws, lp["cw"], lp["cb"], sp["cw"], sp["cb"])


def _proj_call(x3, mod, ws, tl):
    nb, L, _ = x3.shape
    widths = [w.shape[1] for w in ws]
    row = lambda b, l: (b, l, 0)
    full = lambda b, l: (0, 0)
    bof, tof = (lambda b, l: b), (lambda b, l: l)
    return pl.pallas_call(
        functools.partial(_proj_kernel, per_batch=mod.per_batch),
        out_shape=[jax.ShapeDtypeStruct((nb, L, n), F32) for n in widths],
        grid=(nb, L // tl),
        in_specs=[pl.BlockSpec((None, tl, D), row), mod.spec(0, tl, bof, tof), mod.spec(1, tl, bof, tof)]
                 + [pl.BlockSpec((D, n), full) for n in widths],
        out_specs=[pl.BlockSpec((None, tl, n), row) for n in widths],
        compiler_params=_cparams(("parallel", "arbitrary")),
        name="proj",
    )(x3, mod.table, mod.table, *ws)


def _scan_rows(a_scr, u_scr, h0, n_rows):
    sub = lax.broadcasted_iota(I32, (SUBLANES, D), 0)
    shifts = (1, 2, 4)
    masks = [sub >= d for d in shifts]

    def body(j, hp):
        r0 = pl.multiple_of(j * SUBLANES, SUBLANES)
        a = a_scr[pl.ds(r0, SUBLANES), :]
        u = u_scr[pl.ds(r0, SUBLANES), :]
        for d, m in zip(shifts, masks):
            ash = pltpu.roll(a, d, 0)
            ush = pltpu.roll(u, d, 0)
            u = jnp.where(m, u + a * ush, u)
            a = jnp.where(m, a * ash, a)
        h = u + a * hp
        u_scr[pl.ds(r0, SUBLANES), :] = h
        return h[SUBLANES - 1:SUBLANES, :]

    return lax.fori_loop(0, n_rows // SUBLANES, body, h0)


def _lru_kernel(x_ref, g_ref, wa_ref, wx_ref, ba_ref, bx_ref, lam_ref, ng_ref,
                y_ref, h_ref, a_scr, u_scr, hc_scr, *, tl):
    @pl.when(pl.program_id(1) == 0)
    def _():
        hc_scr[...] = jnp.zeros((1, D), F32)

    sp = _softplus(-lam_ref[...])
    for j in range(LRU_W // LRU_PACK_W):
        cs = slice(j * LRU_PACK_W, (j + 1) * LRU_PACK_W)
        xc = x_ref[:, cs]
        a, u = _lru_gates(xc, xc.astype(BF16), wa_ref[j], wx_ref[j], ba_ref[:, cs], bx_ref[:, cs], sp[:, cs])
        a_scr[:, cs] = a
        u_scr[:, cs] = u

    h_last = _scan_rows(a_scr, u_scr, hc_scr[...], tl)
    hc_scr[...] = h_last
    h_ref[...] = h_last
    y = u_scr[...] * _gelu_tanh(g_ref[...])
    y_ref[...] = _rms_norm(y, ng_ref[...]).astype(BF16)


def _lru_call(xg, p, tl):
    nb, L, _ = xg.shape
    vec = lambda b, l: (0, 0)
    blk = lambda b, l: (0, 0, 0)
    return pl.pallas_call(
        functools.partial(_lru_kernel, tl=tl),
        out_shape=[jax.ShapeDtypeStruct((nb, L, LRU_W), BF16),
                   jax.ShapeDtypeStruct((nb, 1, LRU_W), F32)],
        grid=(nb, L // tl),
        in_specs=[pl.BlockSpec((None, tl, LRU_W), lambda b, l: (b, l, 0)),
                  pl.BlockSpec((None, tl, LRU_W), lambda b, l: (b, l, 1)),
                  pl.BlockSpec((LRU_W // LRU_PACK_W, LRU_PACK_W, LRU_PACK_W), blk),
                  pl.BlockSpec((LRU_W // LRU_PACK_W, LRU_PACK_W, LRU_PACK_W), blk),
                  pl.BlockSpec((1, LRU_W), vec), pl.BlockSpec((1, LRU_W), vec),
                  pl.BlockSpec((1, LRU_W), vec), pl.BlockSpec((1, LRU_W), vec)],
        out_specs=[pl.BlockSpec((None, tl, LRU_W), lambda b, l: (b, l, 0)),
                   pl.BlockSpec((None, 1, LRU_W), lambda b, l: (b, 0, 0))],
        scratch_shapes=[pltpu.VMEM((tl, LRU_W), F32), pltpu.VMEM((tl, LRU_W), F32),
                        pltpu.VMEM((1, LRU_W), F32)],
        compiler_params=_cparams(("parallel", "arbitrary")),
        name="lru",
    )(xg, xg, p["wa"], p["wx"], p["ba"], p["bx"], p["lam"], p["ng"])


def _ssd_kernel(xc_ref, z_blk, dt_blk, dtb_ref, alog_ref, dexp_ref, ng_ref,
                y_blk, st_ref, st_scr, y_scr, ea_scr, ew_scr, *, chunks):
    @pl.when(pl.program_id(1) == 0)
    def _():
        st_scr[...] = jnp.zeros((SSD_N, SSD_INNER), F32)

    def one_chunk(c, carry):
        rows = pl.ds(pl.multiple_of(c * SSD_Q, SSD_Q), SSD_Q)
        _ssd_chunk(xc_ref.at[rows, :], z_blk.at[rows, :], dt_blk.at[rows, :], dtb_ref, alog_ref, dexp_ref, ng_ref,
                   y_blk.at[rows, :], st_scr, y_scr, ea_scr, ew_scr)
        return carry

    lax.fori_loop(0, chunks, one_chunk, 0)

    @pl.when(pl.program_id(1) == pl.num_programs(1) - 1)
    def _():
        st_ref[...] = st_scr[...].T


def _ssd_chunk(xc_scr, z_ref, dt_ref, dtb_ref, alog_ref, dexp_ref, ng_ref, y_ref, st_scr, y_scr, ea_scr, ew_scr):
    q = SSD_Q
    dt = _softplus(dt_ref[...] + dtb_ref[...])
    da = dt * (-jnp.exp(alog_ref[...]))
    ri = lax.broadcasted_iota(I32, (q, q), 0)
    ci = lax.broadcasted_iota(I32, (q, q), 1)
    causal = ri >= ci
    acs = _dot(causal.astype(F32), da, precision=HIGHEST)
    acs_t = acs.T
    dt_t = dt.T
    w_end = dt * jnp.exp(acs[q - 1:q, :] - acs)

    half = SSD_INNER // SSD_GROUPS
    hpg = SSD_HEADS // SSD_GROUPS
    cgb, cbs = [], []
    for g in range(SSD_GROUPS):
        bgb = xc_scr[:, SSD_INNER + g * SSD_N:SSD_INNER + (g + 1) * SSD_N].astype(BF16)
        cgb.append(xc_scr[:, SSD_INNER + (SSD_GROUPS + g) * SSD_N:SSD_INNER + (SSD_GROUPS + g + 1) * SSD_N].astype(BF16))
        cbs.append(_dot_nt(cgb[g], bgb))

    low = lax.broadcasted_iota(I32, (q, LANES), 1) < SSD_P
    for k in range(SSD_HEADS // 2):
        cs = slice(k * LANES, (k + 1) * LANES)
        cb = cbs[(2 * k) // hpg]
        m, colb, wb = [], [], []
        for h in (2 * k, 2 * k + 1):
            colb.append(jnp.broadcast_to(acs[:, h:h + 1], (q, LANES)))
            wb.append(jnp.broadcast_to(w_end[:, h:h + 1], (q, LANES)))
            seg = colb[-1] - acs_t[h:h + 1, :]
            decay = jnp.where(causal, jnp.exp(jnp.where(causal, seg, 0.0)), 0.0)
            m.append((cb * (decay * dt_t[h:h + 1, :])).astype(BF16))
        ea_scr[:, cs] = jnp.where(low, colb[0], colb[1])
        ew_scr[:, cs] = jnp.where(low, wb[0], wb[1])
        xk = xc_scr[:, cs]
        zero = jnp.zeros_like(xk)
        rhs = jnp.concatenate([jnp.where(low, xk, zero), jnp.where(low, zero, xk)], axis=0).astype(BF16)
        y_scr[:, cs] = _dot(jnp.concatenate(m, axis=1), rhs)

    x = xc_scr[:, 0:SSD_INNER]
    ea = ea_scr[...]
    w = (x * ew_scr[...]).astype(BF16)
    s_in = st_scr[...]
    sb = s_in.astype(BF16)
    y_off, c_state = [], []
    for g in range(SSD_GROUPS):
        gs = slice(g * half, (g + 1) * half)
        bgt = xc_scr[:, SSD_INNER + g * SSD_N:SSD_INNER + (g + 1) * SSD_N].T.astype(BF16)
        y_off.append(_dot(cgb[g], sb[:, gs]))
        c_state.append(_dot(bgt, w[:, gs]))
    st_scr[...] = jnp.exp(ea[q - 1:q, :]) * s_in + jnp.concatenate(c_state, axis=1)
    y = y_scr[...] + jnp.concatenate(y_off, axis=1) * jnp.exp(ea) + dexp_ref[...] * x

    yz = y * _silu(z_ref[...])
    y_ref[...] = _rms_norm(yz, ng_ref[...]).astype(BF16)


def _ssd_call(xbc, z, dtr, p, chunks):
    nb, L, _ = xbc.shape
    q = SSD_Q
    tl = q * chunks
    vec = lambda b, c: (0, 0)
    row = lambda b, c: (b, c, 0)
    return pl.pallas_call(
        functools.partial(_ssd_kernel, chunks=chunks),
        out_shape=[jax.ShapeDtypeStruct((nb, L, SSD_INNER), BF16),
                   jax.ShapeDtypeStruct((nb, SSD_INNER, SSD_N), F32)],
        grid=(nb, L // tl),
        in_specs=[pl.BlockSpec((None, tl, SSD_CONV_DIM), row), pl.BlockSpec((None, tl, SSD_INNER), row),
                  pl.BlockSpec((None, tl, LANES), row),
                  pl.BlockSpec((1, LANES), vec), pl.BlockSpec((1, LANES), vec),
                  pl.BlockSpec((1, SSD_INNER), vec), pl.BlockSpec((1, SSD_INNER), vec)],
        out_specs=[pl.BlockSpec((None, tl, SSD_INNER), row),
                   pl.BlockSpec((None, SSD_INNER, SSD_N), lambda b, c: (b, 0, 0))],
        scratch_shapes=[pltpu.VMEM((SSD_N, SSD_INNER), F32), pltpu.VMEM((q, SSD_INNER), F32),
                        pltpu.VMEM((q, SSD_INNER), F32), pltpu.VMEM((q, SSD_INNER), F32)],
        compiler_params=_cparams(("parallel", "arbitrary")),
        name="ssd",
    )(xbc, z, dtr, p["dtb"], p["alog"], p["dexp"], p["ng"])


def _srow_kernel(xg_ref, xbc_ref, dte_ref, lconv_ref, h0_ref, sconv_ref,
                 lcw_ref, lcb_ref, wa_ref, wx_ref, ba_ref, bx_ref, lam_ref, lng_ref,
                 scw_ref, scb_ref, dtbe_ref, aloge_ref,
                 yl_ref, h_ref, xs_ref, bm_ref, cm_ref, xdtt_ref, dect_ref, lcn_ref, scn_ref, xc_scr):
    xl = xg_ref[:, 0:LRU_W]
    acc = lcb_ref[...] + xl * lcw_ref[CONV_W - 1:CONV_W, :]
    for k in range(CONV_W - 1):
        acc = acc + lconv_ref[:, k * LRU_W:(k + 1) * LRU_W] * lcw_ref[k:k + 1, :]
    xc_scr[...] = acc
    for k in range(CONV_W - 2):
        lcn_ref[:, k, :] = lconv_ref[:, (k + 1) * LRU_W:(k + 2) * LRU_W]
        scn_ref[:, k, :] = sconv_ref[:, (k + 1) * SSD_CONV_DIM:(k + 2) * SSD_CONV_DIM]
    lcn_ref[:, CONV_W - 2, :] = xl
    scn_ref[:, CONV_W - 2, :] = xbc_ref[...]
    sp = _softplus(-lam_ref[...])
    for j in range(LRU_W // LRU_PACK_W):
        cs = slice(j * LRU_PACK_W, (j + 1) * LRU_PACK_W)
        xc = xc_scr[:, cs]
        a, u = _lru_gates(xc, xc.astype(BF16), wa_ref[j], wx_ref[j], ba_ref[:, cs], bx_ref[:, cs], sp[:, cs])
        h_ref[:, cs] = a * h0_ref[:, cs] + u
    y = h_ref[...] * _gelu_tanh(xg_ref[:, LRU_W:2 * LRU_W])
    yl_ref[...] = _rms_norm(y, lng_ref[...]).astype(BF16)

    acc = scb_ref[...] + xbc_ref[...] * scw_ref[CONV_W - 1:CONV_W, :]
    for k in range(CONV_W - 1):
        acc = acc + sconv_ref[:, k * SSD_CONV_DIM:(k + 1) * SSD_CONV_DIM] * scw_ref[k:k + 1, :]
    xc = _silu(acc)
    xs = xc[:, 0:SSD_INNER]
    xs_ref[...] = xs
    bm_ref[...] = xc[:, SSD_INNER:SSD_INNER + SSD_GROUPS * SSD_N]
    cm_ref[...] = xc[:, SSD_INNER + SSD_GROUPS * SSD_N:]
    dt = _softplus(dte_ref[...] + dtbe_ref[...])
    dec = jnp.exp(dt * (-jnp.exp(aloge_ref[...])))
    xdtt = (xs * dt).T
    dect = dec.T
    for j in range(xdtt_ref.shape[0]):
        xdtt_ref[j] = xdtt[:, j * SUBLANES:(j + 1) * SUBLANES]
        dect_ref[j] = dect[:, j * SUBLANES:(j + 1) * SUBLANES]


def _srow_call(xg, xbc, dte, lconv, h0, sconv, lp, sp):
    n = xg.shape[0]
    args = (xg, xbc, dte, lconv, h0, sconv, lp["cw"], lp["cb"], lp["wa"], lp["wx"], lp["ba"], lp["bx"],
            lp["lam"], lp["ng"], sp["cw"], sp["cb"], sp["dtbe"], sp["aloge"])
    full = lambda a: pl.BlockSpec(a.shape, lambda i, nd=a.ndim: (0,) * nd)
    outs = [jax.ShapeDtypeStruct((n, LRU_W), BF16), jax.ShapeDtypeStruct((n, LRU_W), F32),
            jax.ShapeDtypeStruct((n, SSD_INNER), F32), jax.ShapeDtypeStruct((n, SSD_GROUPS * SSD_N), F32),
            jax.ShapeDtypeStruct((n, SSD_GROUPS * SSD_N), F32),
            jax.ShapeDtypeStruct((n // SUBLANES, SSD_INNER, SUBLANES), F32),
            jax.ShapeDtypeStruct((n // SUBLANES, SSD_INNER, SUBLANES), F32),
            jax.ShapeDtypeStruct((n, CONV_W - 1, LRU_W), F32), jax.ShapeDtypeStruct((n, CONV_W - 1, SSD_CONV_DIM), F32)]
    return pl.pallas_call(
        _srow_kernel,
        out_shape=outs,
        grid=(1,),
        in_specs=[full(a) for a in args],
        out_specs=[pl.BlockSpec(o.shape, lambda i, nd=len(o.shape): (0,) * nd) for o in outs],
        scratch_shapes=[pltpu.VMEM((n, LRU_W), F32)],
        compiler_params=_cparams(("arbitrary",)),
        name="srow",
    )(*args)


def _sstate_kernel(s0_ref, xq_ref, dq_ref, bm_ref, cm_ref, xs_ref, z_ref, dexp_ref, ng_ref,
                   sn_ref, ys_ref, yraw):
    nb = s0_ref.shape[0]
    half = SSD_INNER // SSD_GROUPS
    for bi in range(nb):
        brow = jnp.concatenate(
            [jnp.broadcast_to(bm_ref[bi:bi + 1, g * SSD_N:(g + 1) * SSD_N], (half, SSD_N)) for g in range(SSD_GROUPS)],
            axis=0)
        s = dq_ref[:, bi:bi + 1] * s0_ref[bi] + xq_ref[:, bi:bi + 1] * brow
        sn_ref[bi] = s
        sb = s.astype(BF16)
        for g in range(SSD_GROUPS):
            cg = cm_ref[:, g * SSD_N:(g + 1) * SSD_N].astype(BF16)
            res = _dot_nt(cg, sb[g * half:(g + 1) * half, :])
            yraw[bi:bi + 1, g * half:(g + 1) * half] = res[bi:bi + 1, :]
    y = yraw[...] + dexp_ref[...] * xs_ref[...]
    ys_ref[...] = _rms_norm(y * _silu(z_ref[...]), ng_ref[...]).astype(BF16)


def _sstate_call(s0, xq, dq, bm, cm, xs, z, dexp, ng, nb):
    n = s0.shape[0]
    row = lambda i: (i, 0)
    vec = lambda i: (0, 0)
    gn = SSD_GROUPS * SSD_N
    return pl.pallas_call(
        _sstate_kernel,
        out_shape=[jax.ShapeDtypeStruct(s0.shape, F32), jax.ShapeDtypeStruct((n, SSD_INNER), BF16)],
        grid=(n // nb,),
        in_specs=[pl.BlockSpec((nb, SSD_INNER, SSD_N), lambda i: (i, 0, 0)),
                  pl.BlockSpec((None, SSD_INNER, nb), lambda i: (i, 0, 0)),
                  pl.BlockSpec((None, SSD_INNER, nb), lambda i: (i, 0, 0)),
                  pl.BlockSpec((nb, gn), row), pl.BlockSpec((nb, gn), row),
                  pl.BlockSpec((nb, SSD_INNER), row), pl.BlockSpec((nb, SSD_INNER), row),
                  pl.BlockSpec((1, SSD_INNER), vec), pl.BlockSpec((1, SSD_INNER), vec)],
        out_specs=[pl.BlockSpec((nb, SSD_INNER, SSD_N), lambda i: (i, 0, 0)), pl.BlockSpec((nb, SSD_INNER), row)],
        scratch_shapes=[pltpu.VMEM((nb, SSD_INNER), F32)],
        compiler_params=_cparams(("parallel",)),
        name="sstate",
    )(s0, xq, dq, bm, cm, xs, z, dexp, ng)


def _post_kernel(yl_ref, ys_ref, x_ref, g1_ref, sh2_ref, sc2_ref, wo_ref, l1g_ref, l1b_ref, wrt_ref, brc_ref,
                 x1_ref, v_ref, eid_ref, gw_ref, *, per_batch):
    b = pl.program_id(0)
    o = _dot(yl_ref[...], wo_ref[0:LRU_W, :]) + _dot(ys_ref[...], wo_ref[LRU_W:LRU_W + SSD_INNER, :])
    x1 = _layer_norm(ALPHA * x_ref[...] + _mod_rows(g1_ref, per_batch, b) * o) * l1g_ref[...] + l1b_ref[...]
    x1_ref[...] = x1
    v = _layer_norm(x1) * (1.0 + _mod_rows(sc2_ref, per_batch, b)) + _mod_rows(sh2_ref, per_batch, b)
    _store_row_tiles(v_ref, v)

    lt = _dot(v.astype(BF16), wrt_ref[...]).T + brc_ref[...]
    tl = lt.shape[1]
    row = lax.broadcasted_iota(I32, (GROUP_SIZE, tl), 0).astype(F32)
    big = float(GROUP_SIZE)
    neg = -jnp.inf
    lg = jnp.where(row < N_GROUPS, lt[0:GROUP_SIZE, :], neg)
    gmax = jnp.max(lg, axis=0, keepdims=True)
    gsel = jnp.min(jnp.where(lg == gmax, row, big), axis=0, keepdims=True)
    pg = 1.0 / jnp.sum(jnp.exp(lg - gmax), axis=0, keepdims=True)
    le = lt[GROUP_SIZE:2 * GROUP_SIZE, :]
    for j in range(1, N_GROUPS):
        le = jnp.where(gsel == j, lt[GROUP_SIZE * (j + 1):GROUP_SIZE * (j + 2), :], le)
    m1 = jnp.max(le, axis=0, keepdims=True)
    i1 = jnp.min(jnp.where(le == m1, row, big), axis=0, keepdims=True)
    rest = jnp.where(row == i1, neg, le)
    m2 = jnp.max(rest, axis=0, keepdims=True)
    i2 = jnp.min(jnp.where(rest == m2, row, big), axis=0, keepdims=True)
    e2 = jnp.exp(m2 - m1)
    den = 1.0 + e2
    eid = jnp.where(row == 0, gsel * GROUP_SIZE + i1, jnp.where(row == 1, gsel * GROUP_SIZE + i2, 0.0))
    eid_ref[...] = eid.astype(I32)
    gw_ref[...] = jnp.where(row == 0, pg * (1.0 / den), jnp.where(row == 1, pg * (e2 / den), 0.0))


def _post_call(yl, ys, x3, mod, p, tl):
    nb, L, _ = x3.shape
    nl = L // tl
    row = lambda b, l: (b, l, 0)
    vec = lambda b, l: (0, 0)
    tok = lambda b, l: (b * nl + l, 0)
    tokt = lambda b, l: (0, b * nl + l)
    bof, tof = (lambda b, l: b), (lambda b, l: l)
    return pl.pallas_call(
        functools.partial(_post_kernel, per_batch=mod.per_batch),
        out_shape=[jax.ShapeDtypeStruct((nb, L, D), F32), jax.ShapeDtypeStruct(_tile_rows(nb * L), F32),
                   jax.ShapeDtypeStruct((SUBLANES, nb * L), I32), jax.ShapeDtypeStruct((SUBLANES, nb * L), F32)],
        grid=(nb, nl),
        in_specs=[pl.BlockSpec((None, tl, LRU_W), row), pl.BlockSpec((None, tl, SSD_INNER), row),
                  pl.BlockSpec((None, tl, D), row),
                  mod.spec(2, tl, bof, tof), mod.spec(3, tl, bof, tof), mod.spec(4, tl, bof, tof),
                  pl.BlockSpec((LRU_W + SSD_INNER, D), vec), pl.BlockSpec((1, D), vec), pl.BlockSpec((1, D), vec),
                  pl.BlockSpec((D, LANES), vec), pl.BlockSpec((LANES, 1), vec)],
        out_specs=[pl.BlockSpec((None, tl, D), row), pl.BlockSpec(_tile_rows(tl), tok),
                   pl.BlockSpec((SUBLANES, tl), tokt), pl.BlockSpec((SUBLANES, tl), tokt)],
        compiler_params=_cparams(("parallel", "arbitrary")),
        name="post",
    )(yl, ys, x3, mod.table, mod.table, mod.table, p["wo"], p["l1g"], p["l1b"], p["wrt"], p["brc"])


def _route_kernel(eidp_ref, eids_ref, dest_ref, cnt_ref, run, poff, *, p_tiles):
    ph = pl.program_id(0)
    t = pl.program_id(1)
    n = ROUTE_TILE
    eid = jnp.where(t < p_tiles, eidp_ref[...], eids_ref[...])
    rowi = lax.broadcasted_iota(I32, (LANES, n), 0)
    oh0 = rowi == eid[0:1, :]
    oh1 = rowi == eid[1:2, :]
    oh = oh0.astype(F32) + oh1.astype(F32)
    tile_cnt = jnp.sum(oh, axis=1, keepdims=True)

    @pl.when((ph == 0) & (t == 0))
    def _():
        run[...] = jnp.zeros((LANES, n), F32)

    @pl.when(ph == 0)
    def _():
        run[...] = run[...] + tile_cnt

    @pl.when((ph == 1) & (t == 0))
    def _():
        counts = run[...]
        cnt_ref[...] = counts
        ci = counts.astype(I32)
        q = jnp.floor(counts * (1.0 / MOE_BM)).astype(I32)
        rem = ci - q * MOE_BM
        q = q + jnp.where(rem >= MOE_BM, 1, 0) - jnp.where(rem < 0, 1, 0)
        nblk = q + jnp.where(ci - q * MOE_BM > 0, 1, 0)
        r = lax.broadcasted_iota(I32, (LANES, LANES), 0)
        c = lax.broadcasted_iota(I32, (LANES, LANES), 1)
        lower = (r > c).astype(BF16)
        poff[...] = _dot(lower, nblk.astype(F32).astype(BF16)) * float(MOE_BM)
        run[...] = jnp.zeros((LANES, n), F32)

    @pl.when(ph == 1)
    def _():
        r = lax.broadcasted_iota(I32, (n, n), 0)
        c = lax.broadcasted_iota(I32, (n, n), 1)
        before = (r < c).astype(BF16)
        slot = _dot(oh.astype(BF16), before) + run[...] + poff[...]
        d0 = jnp.sum(jnp.where(oh0, slot, 0.0), axis=0, keepdims=True)
        d1 = jnp.sum(jnp.where(oh1, slot, 0.0), axis=0, keepdims=True)
        row = lax.broadcasted_iota(I32, (SUBLANES, n), 0)
        dest_ref[...] = jnp.where(row == 0, d0, jnp.where(row == 1, d1, 0.0)).astype(I32)
        run[...] = run[...] + tile_cnt


def _route_call(eid_p, eid_s):
    n = ROUTE_TILE
    p_tiles = eid_p.shape[1] // n
    s_tiles = eid_s.shape[1] // n
    t_all = (p_tiles + s_tiles) * n
    return pl.pallas_call(
        functools.partial(_route_kernel, p_tiles=p_tiles),
        out_shape=[jax.ShapeDtypeStruct((SUBLANES, t_all), I32), jax.ShapeDtypeStruct((LANES, n), F32)],
        grid=(2, p_tiles + s_tiles),
        in_specs=[pl.BlockSpec((SUBLANES, n), lambda ph, t: (0, jnp.minimum(t, p_tiles - 1))),
                  pl.BlockSpec((SUBLANES, n), lambda ph, t: (0, jnp.maximum(t - p_tiles, 0)))],
        out_specs=[pl.BlockSpec((SUBLANES, n), lambda ph, t: (0, t * ph)),
                   pl.BlockSpec((LANES, n), lambda ph, t: (0, 0))],
        scratch_shapes=[pltpu.VMEM((LANES, n), F32), pltpu.VMEM((LANES, n), F32)],
        compiler_params=_cparams(("arbitrary", "arbitrary")),
        name="route",
    )(eid_p, eid_s)


def _row_copy(src, src_row, dst, dst_row, sem):
    s0 = pl.multiple_of(src_row * SUBLANES, SUBLANES)
    d0 = pl.multiple_of(dst_row * SUBLANES, SUBLANES)
    return pltpu.make_async_copy(src.at[pl.ds(s0, SUBLANES), :], dst.at[pl.ds(d0, SUBLANES), :], sem)


def _scatter_rows(dest_ref, v_ref, xpad_ref, sem, tl):
    def start(r, carry):
        for k in range(2):
            _row_copy(v_ref, r, xpad_ref, dest_ref[k, r], sem).start(priority=k)
        return carry

    def wait(r, carry):
        for k in range(2):
            _row_copy(v_ref, r, xpad_ref, dest_ref[k, r], sem).wait()
        return carry

    lax.fori_loop(0, tl, start, 0, unroll=DMA_UNROLL)
    lax.fori_loop(0, tl, wait, 0, unroll=DMA_UNROLL)


def _dispatch_kernel(zflag_ref, destp_ref, dests_ref, vp_ref, vs_ref, xpad_ref, zbuf, sem, zsem,
                     *, tl_p, tl_s, p_tiles, n_blocks):
    blk_rows = _tile_rows(MOE_BM)[0]

    @pl.when(pl.program_id(0) == 0)
    def _():
        zbuf[...] = jnp.zeros(zbuf.shape, F32)

        def zero_block(go):
            def body(b, carry):
                @pl.when(zflag_ref[b] != 0)
                def _():
                    r0 = pl.multiple_of(b * blk_rows, blk_rows)
                    cp = pltpu.make_async_copy(zbuf, xpad_ref.at[pl.ds(r0, blk_rows), :], zsem)
                    if go:
                        cp.start()
                    else:
                        cp.wait()
                return carry
            lax.fori_loop(0, n_blocks, body, 0)

        zero_block(True)
        zero_block(False)

    @pl.when(pl.program_id(0) < p_tiles)
    def _():
        _scatter_rows(destp_ref, vp_ref, xpad_ref, sem, tl_p)

    @pl.when(pl.program_id(0) >= p_tiles)
    def _():
        _scatter_rows(dests_ref, vs_ref, xpad_ref, sem, tl_s)


def _dispatch_call(zflag, dest, v_p, v_s, n_blocks, tl_p, tl_s):
    p_tiles = v_p.shape[0] // _tile_rows(tl_p)[0]
    s_tiles = v_s.shape[0] // _tile_rows(tl_s)[0]
    s_off = p_tiles * tl_p // tl_s
    return pl.pallas_call(
        functools.partial(_dispatch_kernel, tl_p=tl_p, tl_s=tl_s, p_tiles=p_tiles, n_blocks=n_blocks),
        out_shape=jax.ShapeDtypeStruct(_tile_rows(n_blocks * MOE_BM), F32),
        grid_spec=pltpu.PrefetchScalarGridSpec(
            num_scalar_prefetch=1,
            grid=(p_tiles + s_tiles,),
            in_specs=[pl.BlockSpec((SUBLANES, tl_p), lambda t, z: (0, jnp.minimum(t, p_tiles - 1)),
                                   memory_space=pltpu.SMEM),
                      pl.BlockSpec((SUBLANES, tl_s), lambda t, z: (0, s_off + jnp.maximum(t - p_tiles, 0)),
                                   memory_space=pltpu.SMEM),
                      pl.BlockSpec(_tile_rows(tl_p), lambda t, z: (jnp.minimum(t, p_tiles - 1), 0)),
                      pl.BlockSpec(_tile_rows(tl_s), lambda t, z: (jnp.maximum(t - p_tiles, 0), 0))],
            out_specs=pl.BlockSpec(memory_space=pl.ANY),
            scratch_shapes=[pltpu.VMEM(_tile_rows(MOE_BM), F32), pltpu.SemaphoreType.DMA, pltpu.SemaphoreType.DMA]),
        compiler_params=_cparams(("arbitrary",)),
        name="dispatch",
    )(zflag, dest, dest, v_p, v_s)


def _expert_kernel(be_ref, nb_ref, nxt_ref, slot_ref, x_ref, wg_hbm, wu_hbm, wd_hbm, o_ref,
                   wgf, wuf, wdf, wgb, wub, wdb, sems):
    i = pl.program_id(0)

    def weights(e, slot, go):
        for m, (hbm, buf) in enumerate(((wg_hbm, wgf), (wu_hbm, wuf), (wd_hbm, wdf))):
            cp = pltpu.make_async_copy(hbm.at[e], buf.at[slot], sems.at[slot, m])
            if go:
                cp.start()
            else:
                cp.wait()

    @pl.when(i < nb_ref[0])
    def _():
        e = be_ref[i]
        slot = slot_ref[e]

        @pl.when(i == 0)
        def _():
            weights(e, slot, True)

        @pl.when((i == 0) | (e != be_ref[jnp.maximum(i - 1, 0)]))
        def _():
            weights(e, slot, False)
            nxt = nxt_ref[e]

            @pl.when(nxt >= 0)
            def _():
                weights(nxt, 1 - slot, True)

            wgb[...] = wgf[slot].astype(BF16)
            wub[...] = wuf[slot].astype(BF16)
            wdb[...] = wdf[slot].astype(BF16)

        x = _load_row_tiles(x_ref).astype(BF16)
        h = _silu(_dot(x, wgb[...])) * _dot(x, wub[...])
        _store_row_tiles(o_ref, _dot(h.astype(BF16), wdb[...]))

    @pl.when(i >= nb_ref[0])
    def _():
        o_ref[...] = jnp.zeros(o_ref.shape, F32)


def _expert_call(blk_e, nblk, nxt_e, slot_e, xpad, w_gate, w_up, w_down):
    n_rows = xpad.shape[0] // SUBLANES
    blk = lambda i, be, nb, nx, sl: (jnp.minimum(i, nb[0] - 1), 0)
    hbm = pl.BlockSpec(memory_space=pl.ANY)
    return pl.pallas_call(
        _expert_kernel,
        out_shape=jax.ShapeDtypeStruct(_tile_rows(n_rows), F32),
        grid_spec=pltpu.PrefetchScalarGridSpec(
            num_scalar_prefetch=4,
            grid=(n_rows // MOE_BM,),
            in_specs=[pl.BlockSpec(_tile_rows(MOE_BM), blk), hbm, hbm, hbm],
            out_specs=pl.BlockSpec(_tile_rows(MOE_BM), lambda i, be, nb, nx, sl: (i, 0)),
            scratch_shapes=[pltpu.VMEM((2, D, D_FF), F32), pltpu.VMEM((2, D, D_FF), F32), pltpu.VMEM((2, D_FF, D), F32),
                            pltpu.VMEM((D, D_FF), BF16), pltpu.VMEM((D, D_FF), BF16), pltpu.VMEM((D_FF, D), BF16),
                            pltpu.SemaphoreType.DMA((2, 3))]),
        compiler_params=_cparams(("arbitrary",)),
        name="expert",
    )(blk_e, nblk, nxt_e, slot_e, xpad, w_gate, w_up, w_down)


def _combine_kernel(dest_ref, gw0_ref, gw1_ref, x1_ref, g2_ref, l2g_ref, l2b_ref, ypad_ref, y_ref, ybuf, sems,
                    *, tok_off, k_stride, tl, nl, per_batch):
    i = pl.program_id(0)

    def gather(tile, slot, go):
        base = tok_off + tile * tl

        def body(r, carry):
            for k in range(2):
                cp = _row_copy(ypad_ref, dest_ref[k * k_stride + base + r], ybuf.at[slot, k], r, sems.at[slot])
                if go:
                    cp.start(priority=k)
                else:
                    cp.wait()
            return carry

        lax.fori_loop(0, tl, body, 0, unroll=DMA_UNROLL)

    @pl.when(i == 0)
    def _():
        gather(0, 0, True)

    @pl.when(i + 1 < pl.num_programs(0))
    def _():
        gather(i + 1, (i + 1) % 2, True)

    slot = i % 2
    gather(i, slot, False)
    f = _load_row_tiles(ybuf.at[slot, 0]) * gw0_ref[...] + _load_row_tiles(ybuf.at[slot, 1]) * gw1_ref[...]
    g2 = _mod_rows(g2_ref, per_batch, i // nl)
    y_ref[...] = _layer_norm(ALPHA * x1_ref[...] + g2 * f) * l2g_ref[...] + l2b_ref[...]


def _combine_call(dest, gw0, gw1, x1, mod, l2g, l2b, ypad, tok_off, tl):
    nb, L, _ = x1.shape
    nl = L // tl
    row = lambda i, d: (i // nl, i % nl, 0)
    vec = lambda i, d: (0, 0)
    tok = lambda i, d: (i, 0)
    g2_spec = mod.spec(5, tl, lambda i, d: i // nl, lambda i, d: i % nl)
    return pl.pallas_call(
        functools.partial(_combine_kernel, tok_off=tok_off, k_stride=dest.shape[1], tl=tl, nl=nl,
                          per_batch=mod.per_batch),
        out_shape=jax.ShapeDtypeStruct((nb, L, D), F32),
        grid_spec=pltpu.PrefetchScalarGridSpec(
            num_scalar_prefetch=1,
            grid=(nb * nl,),
            in_specs=[pl.BlockSpec((tl, 1), tok), pl.BlockSpec((tl, 1), tok),
                      pl.BlockSpec((None, tl, D), row), g2_spec,
                      pl.BlockSpec((1, D), vec), pl.BlockSpec((1, D), vec),
                      pl.BlockSpec(memory_space=pl.ANY)],
            out_specs=pl.BlockSpec((None, tl, D), row),
            scratch_shapes=[pltpu.VMEM((2, 2) + _tile_rows(tl), F32), pltpu.SemaphoreType.DMA((2,))]),
        compiler_params=_cparams(("arbitrary",)),
        name="combine",
    )(dest[:2].reshape(-1), gw0, gw1, x1, mod.table, l2g, l2b, ypad)


def _block_diag(w):
    nh, blk, _ = w.shape
    w4 = w.reshape(nh // LRU_PACK, LRU_PACK, blk, blk)
    eye = jnp.eye(LRU_PACK, dtype=w.dtype)
    out = jnp.einsum("gaij,ab->gaibj", w4, eye)
    return out.reshape(nh // LRU_PACK, LRU_PACK * blk, LRU_PACK * blk).astype(BF16)


def _row(v):
    return v.reshape(1, -1).astype(F32)


def _pad_lanes(v):
    return jnp.pad(v.reshape(1, -1).astype(F32), ((0, 0), (0, LANES - v.shape[-1])))


def kernel(x_prompt, x_sample, c_prompt, c_sample, state_lru_conv, state_lru_h, state_ssd_conv, state_ssd, w_ada, b_ada, w_in, lru_conv_w, lru_conv_b, lru_wa, lru_ba, lru_wx, lru_bx, lru_lambda, lru_norm_g, ssd_conv_w, ssd_conv_b, ssd_dt_bias, ssd_a_log, ssd_d, ssd_norm_g, w_out, ln1_g, ln1_b, w_rg, b_rg, w_re, b_re, w_gate, w_up, w_down, ln2_g, ln2_b):
    assert w_ada.shape[0] == DEPTH == 1
    nbp, seq, _ = x_prompt.shape
    nbs = x_sample.shape[0]
    t_p = nbp * seq
    t_all = t_p + nbs
    assert x_sample.shape[1] == 1 and t_p % ROUTE_TILE == 0 and nbs % TOK_TILE == 0 and nbp == SUBLANES

    c_rows = -(-(nbp + nbs) // 16) * 16
    c_all = jnp.pad(jnp.concatenate([c_sample, c_prompt], axis=0), ((0, c_rows - nbp - nbs), (0, 0)))
    w_in0 = w_in[0]
    o_z, o_xbc, o_dt = 2 * LRU_W, 2 * LRU_W + SSD_INNER, 2 * LRU_W + SSD_INNER + SSD_CONV_DIM
    w_dt = w_in0[:, o_dt:]
    ws_p = list(_wcast_call(w_in0, ((0, o_z), (o_z, o_xbc), (o_xbc, o_dt)), 256))
    ws_s = ws_p + [jnp.repeat(w_dt, SSD_P, axis=1).astype(BF16)]
    ws_p = ws_p + [jnp.pad(w_dt, ((0, 0), (0, LANES - SSD_HEADS))).astype(BF16)]
    lp = dict(cw=lru_conv_w[0], cb=_row(lru_conv_b[0]), wa=_block_diag(lru_wa[0]), wx=_block_diag(lru_wx[0]),
              ba=_row(lru_ba[0]), bx=_row(lru_bx[0]), lam=_row(lru_lambda[0]), ng=_row(lru_norm_g[0]))
    sp = dict(cw=ssd_conv_w[0], cb=_row(ssd_conv_b[0]), dtb=_pad_lanes(ssd_dt_bias[0]), alog=_pad_lanes(ssd_a_log[0]),
              dexp=_row(jnp.repeat(ssd_d[0], SSD_P)), ng=_row(ssd_norm_g[0]),
              dtbe=_row(jnp.repeat(ssd_dt_bias[0], SSD_P)), aloge=_row(jnp.repeat(ssd_a_log[0], SSD_P)))
    wrt = jnp.zeros((D, LANES), F32).at[:, 0:N_GROUPS].set(w_rg[0]).at[:, GROUP_SIZE:ROUTE_ROWS].set(w_re[0]).astype(BF16)
    brc = jnp.zeros((LANES, 1), F32).at[0:N_GROUPS, 0].set(b_rg[0]).at[GROUP_SIZE:ROUTE_ROWS, 0].set(b_re[0])
    pp = dict(wo=w_out[0].astype(BF16), l1g=_row(ln1_g[0]), l1b=_row(ln1_b[0]), wrt=wrt, brc=brc)

    table = _ada_call(c_all, w_ada[0], _row(b_ada[0]))
    mod_p = _Mod(table, nbs, True)
    mod_s = _Mod(table, 0, False)

    xg, z, xbc, dtr, p_lru_conv, p_ssd_conv = _proj_conv_call(x_prompt, mod_p, ws_p, lp, sp, SEQ_TILE)
    yl, p_lru_h = _lru_call(xg, lp, SEQ_TILE)
    ys, p_ssd = _ssd_call(xbc, z, dtr, sp, SEQ_TILE // SSD_Q)
    x1_p, v_p, eid_p, gw_p = _post_call(yl, ys, x_prompt, mod_p, pp, SEQ_TILE)

    xs3 = x_sample.reshape(1, nbs, D)
    xg_s, z_s, xbc_s, dte_s = _proj_call(xs3, mod_s, ws_s, nbs)
    xg_s, z_s, xbc_s, dte_s = xg_s[0], z_s[0], xbc_s[0], dte_s[0]
    yl_s, h_s, xs_s, bm_s, cm_s, xdtt, dect, s_lru_conv, s_ssd_conv = _srow_call(
        xg_s, xbc_s, dte_s, state_lru_conv[0].reshape(nbs, -1), state_lru_h[0],
        state_ssd_conv[0].reshape(nbs, -1), lp, sp)
    s_new, ys_s = _sstate_call(state_ssd[0].reshape(nbs, SSD_INNER, SSD_N), xdtt, dect,
                               bm_s, cm_s, xs_s, z_s, sp["dexp"], sp["ng"], SUBLANES)
    x1_s, v_s, eid_s, gw_s = _post_call(yl_s[None], ys_s[None], xs3, mod_s, pp, nbs)

    s_pad = -nbs % ROUTE_TILE
    dest, counts = _route_call(eid_p, jnp.pad(eid_s, ((0, 0), (0, s_pad)), constant_values=-1))
    counts = counts[:N_EXPERTS, 0].astype(I32)
    n_blocks = -(-(2 * t_all) // MOE_BM) + N_EXPERTS
    pend = jnp.cumsum(((counts + MOE_BM - 1) // MOE_BM) * MOE_BM)
    blk_start = jnp.arange(n_blocks, dtype=I32) * MOE_BM
    blk_e = jnp.minimum(jnp.sum((pend[None, :] <= blk_start[:, None]).astype(I32), axis=1), N_EXPERTS - 1)
    nblk = (pend[-1:] // MOE_BM).astype(I32)
    blk_ids = jnp.arange(n_blocks, dtype=I32)
    last_of_expert = jnp.any(((pend // MOE_BM - 1)[None, :] == blk_ids[:, None]) & (counts > 0)[None, :], axis=1)
    zflag = (last_of_expert | (blk_ids >= nblk[0])).astype(I32)
    used = counts > 0
    eids = jnp.arange(N_EXPERTS, dtype=I32)
    later_used = (eids[None, :] > eids[:, None]) & used[None, :]
    nxt_e = jnp.min(jnp.where(later_used, eids[None, :], N_EXPERTS), axis=1)
    nxt_e = jnp.where(nxt_e == N_EXPERTS, -1, nxt_e).astype(I32)
    slot_e = ((jnp.cumsum(used.astype(I32)) - used.astype(I32)) % 2).astype(I32)
    xpad = _dispatch_call(zflag, dest, v_p, v_s, n_blocks, 2 * TOK_TILE, TOK_TILE)
    ypad = _expert_call(blk_e, nblk, nxt_e, slot_e, xpad, w_gate[0], w_up[0], w_down[0])
    l2g, l2b = _row(ln2_g[0]), _row(ln2_b[0])
    y_p = _combine_call(dest, gw_p[0].reshape(t_p, 1), gw_p[1].reshape(t_p, 1), x1_p, mod_p, l2g, l2b, ypad, 0,
                        4 * TOK_TILE)
    y_s = _combine_call(dest, gw_s[0].reshape(nbs, 1), gw_s[1].reshape(nbs, 1), x1_s, mod_s, l2g, l2b, ypad, t_p,
                        TOK_TILE)

    return (y_p, y_s.reshape(nbs, 1, D),
            p_lru_conv[None], p_lru_h.reshape(1, nbp, LRU_W), p_ssd_conv[None],
            p_ssd.reshape(1, nbp, SSD_HEADS, SSD_P, SSD_N),
            s_lru_conv[None], h_s[None], s_ssd_conv[None],
            s_new.reshape(1, nbs, SSD_HEADS, SSD_P, SSD_N))
```

```python
import functools
import math

import jax
import jax.numpy as jnp
from jax import lax
from jax.experimental import pallas as pl
from jax.experimental.pallas import tpu as pltpu

F32 = jnp.float32
BF16 = jnp.bfloat16
I32 = jnp.int32
HIGHEST = lax.Precision.HIGHEST

D = 1024
DEPTH = 1
CONV_W = 4
LRU_W = D
LRU_HEADS = 16
LRU_C = 8.0
LRU_PACK = 4
LRU_PACK_W = LRU_PACK * (LRU_W // LRU_HEADS)
SSD_INNER = D
SSD_HEADS = 16
SSD_P = SSD_INNER // SSD_HEADS
SSD_GROUPS = 2
SSD_N = 128
SSD_Q = 128
SSD_CONV_DIM = SSD_INNER + 2 * SSD_GROUPS * SSD_N
N_GROUPS = 4
GROUP_SIZE = 8
N_EXPERTS = N_GROUPS * GROUP_SIZE
D_FF = D // 2
N_MOD = 6
LN_EPS = 1e-5
RMS_EPS = 1e-6
ALPHA = (2.0 * DEPTH) ** 0.25

LANES = 128
SUBLANES = 8
VMEM_LIMIT = 56 * 1024 * 1024
SEQ_TILE = 512
TOK_TILE = 128
ROUTE_TILE = 1024
DMA_UNROLL = 8
MOE_BM = 384
ROUTE_ROWS = 40

NT_DIMS = (((1,), (1,)), ((), ()))


def _tile_rows(rows):
    return (rows * SUBLANES, LANES)


def _cparams(sem):
    return pltpu.CompilerParams(dimension_semantics=sem, vmem_limit_bytes=VMEM_LIMIT)


def _sigmoid(x):
    return 0.5 * (jnp.tanh(0.5 * x) + 1.0)


def _silu(x):
    return x * _sigmoid(x)


def _softplus(x):
    return jnp.maximum(x, 0.0) + jnp.log1p(jnp.exp(-jnp.abs(x)))


def _gelu_tanh(x):
    return 0.5 * x * (1.0 + jnp.tanh(math.sqrt(2.0 / math.pi) * (x + 0.044715 * (x * x * x))))


def _layer_norm(x):
    mu = jnp.mean(x, axis=-1, keepdims=True)
    xc = x - mu
    var = jnp.mean(xc * xc, axis=-1, keepdims=True)
    return xc * lax.rsqrt(var + LN_EPS)


def _rms_norm(x, g):
    return x * lax.rsqrt(jnp.mean(x * x, axis=-1, keepdims=True) + RMS_EPS) * g


def _dot(a, b, **kw):
    return jnp.dot(a, b, preferred_element_type=F32, **kw)


def _dot_nt(a, b, **kw):
    return lax.dot_general(a, b, NT_DIMS, preferred_element_type=F32, **kw)


def _store_row_tiles(ref, val):
    rows = val.shape[0]
    for j in range(SUBLANES):
        ref[pl.ds(j, rows, stride=SUBLANES), :] = val[:, j * LANES:(j + 1) * LANES]


def _load_row_tiles(ref):
    rows = ref.shape[0] // SUBLANES
    return jnp.concatenate([ref[pl.ds(j, rows, stride=SUBLANES), :] for j in range(SUBLANES)], axis=1)


def _lru_gates(xc, xb, wa, wx, ba, bx, sp):
    r = _sigmoid(_dot(xb, wa) + ba)
    i = _sigmoid(_dot(xb, wx) + bx)
    log_a = (-LRU_C) * r * sp
    a = jnp.exp(log_a)
    mult = jnp.sqrt(jnp.tanh(-log_a) * (a * a + 1.0))
    return a, mult * (i * xc)


def _ada_kernel(c_ref, w_ref, b_ref, o_ref):
    s = _silu(c_ref[...]).astype(BF16)
    o_ref[...] = _dot(s, w_ref[...].astype(BF16)) + b_ref[...]


def _ada_call(c_all, w_ada, b_ada):
    rows = c_all.shape[0]
    tn = 512
    per_mod = D // tn
    return pl.pallas_call(
        _ada_kernel,
        out_shape=jax.ShapeDtypeStruct((N_MOD, rows, D), F32),
        grid=(N_MOD * per_mod,),
        in_specs=[pl.BlockSpec((rows, D), lambda j: (0, 0)),
                  pl.BlockSpec((D, tn), lambda j: (0, j)),
                  pl.BlockSpec((1, tn), lambda j: (0, j))],
        out_specs=pl.BlockSpec((None, rows, tn), lambda j: (j // per_mod, 0, j % per_mod)),
        compiler_params=_cparams(("arbitrary",)),
        name="ada",
    )(c_all, w_ada, b_ada)


def _wcast_kernel(w_ref, *o_refs, bounds):
    for o_ref, (lo, hi) in zip(o_refs, bounds):
        o_ref[...] = w_ref[:, lo:hi].astype(BF16)


def _wcast_call(w, bounds, tr):
    rows, cols = w.shape
    assert all(lo % LANES == 0 for lo, _ in bounds)
    return pl.pallas_call(
        functools.partial(_wcast_kernel, bounds=bounds),
        out_shape=[jax.ShapeDtypeStruct((rows, hi - lo), BF16) for lo, hi in bounds],
        grid=(rows // tr,),
        in_specs=[pl.BlockSpec((tr, cols), lambda i: (i, 0))],
        out_specs=[pl.BlockSpec((tr, hi - lo), lambda i: (i, 0)) for lo, hi in bounds],
        compiler_params=_cparams(("arbitrary",)),
        name="wcast",
    )(w)


class _Mod:
    def __init__(self, table, row0, per_batch):
        self.table, self.row0, self.per_batch = table, row0, per_batch

    def spec(self, k, tl, batch_of, tile_of):
        if self.per_batch:
            blk = self.row0 // SUBLANES
            return pl.BlockSpec((None, SUBLANES, D), lambda *g: (k, blk, 0))
        blk = self.row0 // tl
        return pl.BlockSpec((None, tl, D), lambda *g: (k, blk + tile_of(*g), 0))


def _mod_rows(ref, per_batch, b):
    return ref[pl.ds(b, 1), :] if per_batch else ref[...]


def _proj_kernel(x_ref, sh_ref, sc_ref, w1_ref, w2_ref, w3_ref, w4_ref, o1_ref, o2_ref, o3_ref, o4_ref, *, per_batch):
    b = pl.program_id(0)
    u = _layer_norm(x_ref[...]) * (1.0 + _mod_rows(sc_ref, per_batch, b)) + _mod_rows(sh_ref, per_batch, b)
    ub = u.astype(BF16)
    o1_ref[...] = _dot(ub, w1_ref[...])
    o2_ref[...] = _dot(ub, w2_ref[...])
    o3_ref[...] = _dot(ub, w3_ref[...])
    o4_ref[...] = _dot(ub, w4_ref[...])


def _causal_conv(xpad, x_new, cw_ref, cb_ref, conv_ref, tl):
    xpad[SUBLANES:SUBLANES + tl, :] = x_new
    off = SUBLANES - (CONV_W - 1)
    acc = xpad[off:off + tl, :] * cw_ref[0:1, :]
    for k in range(1, CONV_W):
        acc = acc + xpad[off + k:off + k + tl, :] * cw_ref[k:k + 1, :]
    conv_ref[...] = xpad[SUBLANES + tl - (CONV_W - 1):SUBLANES + tl, :]
    xpad[0:SUBLANES, :] = xpad[tl:tl + SUBLANES, :]
    return acc + cb_ref[...]


def _proj_conv_kernel(x_ref, sh_ref, sc_ref, w1_ref, w2_ref, w3_ref, w4_ref, lcw_ref, lcb_ref, scw_ref, scb_ref,
                      o1_ref, o2_ref, o3_ref, o4_ref, lconv_ref, sconv_ref, lpad, spad, *, tl):
    b = pl.program_id(0)

    @pl.when(pl.program_id(1) == 0)
    def _():
        lpad[0:SUBLANES, :] = jnp.zeros((SUBLANES, LRU_W), F32)
        spad[0:SUBLANES, :] = jnp.zeros((SUBLANES, SSD_CONV_DIM), F32)

    u = _layer_norm(x_ref[...]) * (1.0 + _mod_rows(sc_ref, True, b)) + _mod_rows(sh_ref, True, b)
    ub = u.astype(BF16)
    o1_ref[:, 0:LRU_W] = _causal_conv(lpad, _dot(ub, w1_ref[:, 0:LRU_W]), lcw_ref, lcb_ref, lconv_ref, tl)
    o1_ref[:, LRU_W:2 * LRU_W] = _dot(ub, w1_ref[:, LRU_W:2 * LRU_W])
    o2_ref[...] = _dot(ub, w2_ref[...])
    o3_ref[...] = _silu(_causal_conv(spad, _dot(ub, w3_ref[...]), scw_ref, scb_ref, sconv_ref, tl))
    o4_ref[...] = _dot(ub, w4_ref[...])


def _proj_conv_call(x3, mod, ws, lp, sp, tl):
    nb, L, _ = x3.shape
    widths = [w.shape[1] for w in ws]
    row = lambda b, l: (b, l, 0)
    full = lambda b, l: (0, 0)
    st = lambda b, l: (b, 0, 0)
    bof, tof = (lambda b, l: b), (lambda b, l: l)
    return pl.pallas_call(
        functools.partial(_proj_conv_kernel, tl=tl),
        out_shape=[jax.ShapeDtypeStruct((nb, L, n), F32) for n in widths]
                  + [jax.ShapeDtypeStruct((nb, CONV_W - 1, LRU_W), F32),
                     jax.ShapeDtypeStruct((nb, CONV_W - 1, SSD_CONV_DIM), F32)],
        grid=(nb, L // tl),
        in_specs=[pl.BlockSpec((None, tl, D), row), mod.spec(0, tl, bof, tof), mod.spec(1, tl, bof, tof)]
                 + [pl.BlockSpec((D, n), full) for n in widths]
                 + [pl.BlockSpec((CONV_W, LRU_W), full), pl.BlockSpec((1, LRU_W), full),
                    pl.BlockSpec((CONV_W, SSD_CONV_DIM), full), pl.BlockSpec((1, SSD_CONV_DIM), full)],
        out_specs=[pl.BlockSpec((None, tl, n), row) for n in widths]
                  + [pl.BlockSpec((None, CONV_W - 1, LRU_W), st), pl.BlockSpec((None, CONV_W - 1, SSD_CONV_DIM), st)],
        scratch_shapes=[pltpu.VMEM((tl + SUBLANES, LRU_W), F32), pltpu.VMEM((tl + SUBLANES, SSD_CONV_DIM), F32)],
        compiler_params=_cparams(("parallel", "arbitrary")),
        name="projc",
    )(x3, mod.table, mod.table, *ws, lp["cw"], lp["cb"], sp["cw"], sp["cb"])


def _proj_call(x3, mod, ws, tl):
    nb, L, _ = x3.shape
    widths = [w.shape[1] for w in ws]
    row = lambda b, l: (b, l, 0)
    full = lambda b, l: (0, 0)
    bof, tof = (lambda b, l: b), (lambda b, l: l)
    return pl.pallas_call(
        functools.partial(_proj_kernel, per_batch=mod.per_batch),
        out_shape=[jax.ShapeDtypeStruct((nb, L, n), F32) for n in widths],
        grid=(nb, L // tl),
        in_specs=[pl.BlockSpec((None, tl, D), row), mod.spec(0, tl, bof, tof), mod.spec(1, tl, bof, tof)]
                 + [pl.BlockSpec((D, n), full) for n in widths],
        out_specs=[pl.BlockSpec((None, tl, n), row) for n in widths],
        compiler_params=_cparams(("parallel", "arbitrary")),
        name="proj",
    )(x3, mod.table, mod.table, *ws)


def _scan_segments(a_sk, u_sk, h0, seg):
    stride = seg + 1
    nt = D // LANES
    rows = lambda g: pl.ds(g, SUBLANES, stride=stride)
    shape = (nt, SUBLANES, LANES)

    def local(g, carry):
        h, p = carry
        a = a_sk[:, rows(g), :]
        h = a * h + u_sk[:, rows(g), :]
        u_sk[:, rows(g), :] = h
        return h, a * p

    h_fin, p_fin = lax.fori_loop(0, seg, local, (jnp.zeros(shape, F32), jnp.ones(shape, F32)), unroll=2)
    starts = [jnp.stack([h0[:, c * LANES:(c + 1) * LANES] for c in range(nt)], axis=0)]
    for s in range(SUBLANES):
        starts.append(p_fin[:, s:s + 1, :] * starts[-1] + h_fin[:, s:s + 1, :])
    start = jnp.concatenate(starts[:SUBLANES], axis=1)

    def fixup(g, p):
        p = a_sk[:, rows(g), :] * p
        u_sk[:, rows(g), :] = u_sk[:, rows(g), :] + p * start
        return p

    lax.fori_loop(0, seg, fixup, jnp.ones(shape, F32), unroll=2)
    return jnp.concatenate([starts[SUBLANES][c] for c in range(nt)], axis=1)


def _lru_kernel(x_ref, g_ref, wa_ref, wx_ref, ba_ref, bx_ref, lam_ref, ng_ref,
                y_ref, h_ref, a_sk, u_sk, hc_scr, *, tl):
    @pl.when(pl.program_id(1) == 0)
    def _():
        hc_scr[...] = jnp.zeros((1, D), F32)

    seg = tl // SUBLANES
    sp = _softplus(-lam_ref[...])
    for j in range(LRU_W // LRU_PACK_W):
        cs = slice(j * LRU_PACK_W, (j + 1) * LRU_PACK_W)
        xc = x_ref[:, cs]
        a, u = _lru_gates(xc, xc.astype(BF16), wa_ref[j], wx_ref[j], ba_ref[:, cs], bx_ref[:, cs], sp[:, cs])
        for s in range(SUBLANES):
            for t in range(LRU_PACK_W // LANES):
                c = j * (LRU_PACK_W // LANES) + t
                a_sk[c, s * (seg + 1):s * (seg + 1) + seg, :] = a[s * seg:(s + 1) * seg, t * LANES:(t + 1) * LANES]
                u_sk[c, s * (seg + 1):s * (seg + 1) + seg, :] = u[s * seg:(s + 1) * seg, t * LANES:(t + 1) * LANES]

    h_last = _scan_segments(a_sk, u_sk, hc_scr[...], seg)
    hc_scr[...] = h_last
    h_ref[...] = h_last
    for s in range(SUBLANES):
        h = jnp.concatenate([u_sk[c, s * (seg + 1):s * (seg + 1) + seg, :] for c in range(D // LANES)], axis=1)
        y = h * _gelu_tanh(g_ref[s * seg:(s + 1) * seg, :])
        y_ref[s * seg:(s + 1) * seg, :] = _rms_norm(y, ng_ref[...]).astype(BF16)


def _lru_call(xg, p, tl):
    nb, L, _ = xg.shape
    vec = lambda b, l: (0, 0)
    blk = lambda b, l: (0, 0, 0)
    return pl.pallas_call(
        functools.partial(_lru_kernel, tl=tl),
        out_shape=[jax.ShapeDtypeStruct((nb, L, LRU_W), BF16),
                   jax.ShapeDtypeStruct((nb, 1, LRU_W), F32)],
        grid=(nb, L // tl),
        in_specs=[pl.BlockSpec((None, tl, LRU_W), lambda b, l: (b, l, 0)),
                  pl.BlockSpec((None, tl, LRU_W), lambda b, l: (b, l, 1)),
                  pl.BlockSpec((LRU_W // LRU_PACK_W, LRU_PACK_W, LRU_PACK_W), blk),
                  pl.BlockSpec((LRU_W // LRU_PACK_W, LRU_PACK_W, LRU_PACK_W), blk),
                  pl.BlockSpec((1, LRU_W), vec), pl.BlockSpec((1, LRU_W), vec),
                  pl.BlockSpec((1, LRU_W), vec), pl.BlockSpec((1, LRU_W), vec)],
        out_specs=[pl.BlockSpec((None, tl, LRU_W), lambda b, l: (b, l, 0)),
                   pl.BlockSpec((None, 1, LRU_W), lambda b, l: (b, 0, 0))],
        scratch_shapes=[pltpu.VMEM((LRU_W // LANES, tl + SUBLANES, LANES), F32),
                        pltpu.VMEM((LRU_W // LANES, tl + SUBLANES, LANES), F32),
                        pltpu.VMEM((1, LRU_W), F32)],
        compiler_params=_cparams(("parallel", "arbitrary")),
        name="lru",
    )(xg, xg, p["wa"], p["wx"], p["ba"], p["bx"], p["lam"], p["ng"])


def _ssd_kernel(xc_ref, z_blk, dt_blk, dtb_ref, alog_ref, dexp_ref, ng_ref,
                y_blk, st_ref, st_scr, y_scr, ea_scr, ew_scr, *, chunks):
    @pl.when(pl.program_id(1) == 0)
    def _():
        st_scr[...] = jnp.zeros((SSD_N, SSD_INNER), F32)

    def one_chunk(c, carry):
        rows = pl.ds(pl.multiple_of(c * SSD_Q, SSD_Q), SSD_Q)
        _ssd_chunk(xc_ref.at[rows, :], z_blk.at[rows, :], dt_blk.at[rows, :], dtb_ref, alog_ref, dexp_ref, ng_ref,
                   y_blk.at[rows, :], st_scr, y_scr, ea_scr, ew_scr)
        return carry

    lax.fori_loop(0, chunks, one_chunk, 0)

    @pl.when(pl.program_id(1) == pl.num_programs(1) - 1)
    def _():
        st_ref[...] = st_scr[...].T


def _ssd_chunk(xc_scr, z_ref, dt_ref, dtb_ref, alog_ref, dexp_ref, ng_ref, y_ref, st_scr, y_scr, ea_scr, ew_scr):
    q = SSD_Q
    dt = _softplus(dt_ref[...] + dtb_ref[...])
    da = dt * (-jnp.exp(alog_ref[...]))
    ri = lax.broadcasted_iota(I32, (q, q), 0)
    ci = lax.broadcasted_iota(I32, (q, q), 1)
    causal = ri >= ci
    acs = _dot(causal.astype(F32), da, precision=HIGHEST)
    acs_t = acs.T
    dt_t = dt.T
    w_end = dt * jnp.exp(acs[q - 1:q, :] - acs)

    half = SSD_INNER // SSD_GROUPS
    hpg = SSD_HEADS // SSD_GROUPS
    cgb, cbs = [], []
    for g in range(SSD_GROUPS):
        bgb = xc_scr[:, SSD_INNER + g * SSD_N:SSD_INNER + (g + 1) * SSD_N].astype(BF16)
        cgb.append(xc_scr[:, SSD_INNER + (SSD_GROUPS + g) * SSD_N:SSD_INNER + (SSD_GROUPS + g + 1) * SSD_N].astype(BF16))
        cbs.append(_dot_nt(cgb[g], bgb))

    low = lax.broadcasted_iota(I32, (q, LANES), 1) < SSD_P
    for k in range(SSD_HEADS // 2):
        cs = slice(k * LANES, (k + 1) * LANES)
        cb = cbs[(2 * k) // hpg]
        m, colb, wb = [], [], []
        for h in (2 * k, 2 * k + 1):
            colb.append(jnp.broadcast_to(acs[:, h:h + 1], (q, LANES)))
            wb.append(jnp.broadcast_to(w_end[:, h:h + 1], (q, LANES)))
            seg = colb[-1] - acs_t[h:h + 1, :]
            decay = jnp.where(causal, jnp.exp(jnp.where(causal, seg, 0.0)), 0.0)
            m.append((cb * (decay * dt_t[h:h + 1, :])).astype(BF16))
        ea_scr[:, cs] = jnp.where(low, colb[0], colb[1])
        ew_scr[:, cs] = jnp.where(low, wb[0], wb[1])
        xk = xc_scr[:, cs]
        zero = jnp.zeros_like(xk)
        rhs = jnp.concatenate([jnp.where(low, xk, zero), jnp.where(low, zero, xk)], axis=0).astype(BF16)
        y_scr[:, cs] = _dot(jnp.concatenate(m, axis=1), rhs)

    x = xc_scr[:, 0:SSD_INNER]
    ea = ea_scr[...]
    w = (x * ew_scr[...]).astype(BF16)
    s_in = st_scr[...]
    sb = s_in.astype(BF16)
    y_off, c_state = [], []
    for g in range(SSD_GROUPS):
        gs = slice(g * half, (g + 1) * half)
        bgt = xc_scr[:, SSD_INNER + g * SSD_N:SSD_INNER + (g + 1) * SSD_N].T.astype(BF16)
        y_off.append(_dot(cgb[g], sb[:, gs]))
        c_state.append(_dot(bgt, w[:, gs]))
    st_scr[...] = jnp.exp(ea[q - 1:q, :]) * s_in + jnp.concatenate(c_state, axis=1)
    y = y_scr[...] + jnp.concatenate(y_off, axis=1) * jnp.exp(ea) + dexp_ref[...] * x

    yz = y * _silu(z_ref[...])
    y_ref[...] = _rms_norm(yz, ng_ref[...]).astype(BF16)


def _ssd_call(xbc, z, dtr, p, chunks):
    nb, L, _ = xbc.shape
    q = SSD_Q
    tl = q * chunks
    vec = lambda b, c: (0, 0)
    row = lambda b, c: (b, c, 0)
    return pl.pallas_call(
        functools.partial(_ssd_kernel, chunks=chunks),
        out_shape=[jax.ShapeDtypeStruct((nb, L, SSD_INNER), BF16),
                   jax.ShapeDtypeStruct((nb, SSD_INNER, SSD_N), F32)],
        grid=(nb, L // tl),
        in_specs=[pl.BlockSpec((None, tl, SSD_CONV_DIM), row), pl.BlockSpec((None, tl, SSD_INNER), row),
                  pl.BlockSpec((None, tl, LANES), row),
                  pl.BlockSpec((1, LANES), vec), pl.BlockSpec((1, LANES), vec),
                  pl.BlockSpec((1, SSD_INNER), vec), pl.BlockSpec((1, SSD_INNER), vec)],
        out_specs=[pl.BlockSpec((None, tl, SSD_INNER), row),
                   pl.BlockSpec((None, SSD_INNER, SSD_N), lambda b, c: (b, 0, 0))],
        scratch_shapes=[pltpu.VMEM((SSD_N, SSD_INNER), F32), pltpu.VMEM((q, SSD_INNER), F32),
                        pltpu.VMEM((q, SSD_INNER), F32), pltpu.VMEM((q, SSD_INNER), F32)],
        compiler_params=_cparams(("parallel", "arbitrary")),
        name="ssd",
    )(xbc, z, dtr, p["dtb"], p["alog"], p["dexp"], p["ng"])


def _srow_kernel(xg_ref, xbc_ref, dte_ref, lconv_ref, h0_ref, sconv_ref,
                 lcw_ref, lcb_ref, wa_ref, wx_ref, ba_ref, bx_ref, lam_ref, lng_ref,
                 scw_ref, scb_ref, dtbe_ref, aloge_ref,
                 yl_ref, h_ref, xs_ref, bm_ref, cm_ref, xdtt_ref, dect_ref, lcn_ref, scn_ref, xc_scr):
    xl = xg_ref[:, 0:LRU_W]
    acc = lcb_ref[...] + xl * lcw_ref[CONV_W - 1:CONV_W, :]
    for k in range(CONV_W - 1):
        acc = acc + lconv_ref[:, k * LRU_W:(k + 1) * LRU_W] * lcw_ref[k:k + 1, :]
    xc_scr[...] = acc
    for k in range(CONV_W - 2):
        lcn_ref[:, k, :] = lconv_ref[:, (k + 1) * LRU_W:(k + 2) * LRU_W]
        scn_ref[:, k, :] = sconv_ref[:, (k + 1) * SSD_CONV_DIM:(k + 2) * SSD_CONV_DIM]
    lcn_ref[:, CONV_W - 2, :] = xl
    scn_ref[:, CONV_W - 2, :] = xbc_ref[...]
    sp = _softplus(-lam_ref[...])
    for j in range(LRU_W // LRU_PACK_W):
        cs = slice(j * LRU_PACK_W, (j + 1) * LRU_PACK_W)
        xc = xc_scr[:, cs]
        a, u = _lru_gates(xc, xc.astype(BF16), wa_ref[j], wx_ref[j], ba_ref[:, cs], bx_ref[:, cs], sp[:, cs])
        h_ref[:, cs] = a * h0_ref[:, cs] + u
    y = h_ref[...] * _gelu_tanh(xg_ref[:, LRU_W:2 * LRU_W])
    yl_ref[...] = _rms_norm(y, lng_ref[...]).astype(BF16)

    acc = scb_ref[...] + xbc_ref[...] * scw_ref[CONV_W - 1:CONV_W, :]
    for k in range(CONV_W - 1):
        acc = acc + sconv_ref[:, k * SSD_CONV_DIM:(k + 1) * SSD_CONV_DIM] * scw_ref[k:k + 1, :]
    xc = _silu(acc)
    xs = xc[:, 0:SSD_INNER]
    xs_ref[...] = xs
    bm_ref[...] = xc[:, SSD_INNER:SSD_INNER + SSD_GROUPS * SSD_N]
    cm_ref[...] = xc[:, SSD_INNER + SSD_GROUPS * SSD_N:]
    dt = _softplus(dte_ref[...] + dtbe_ref[...])
    dec = jnp.exp(dt * (-jnp.exp(aloge_ref[...])))
    xdtt = (xs * dt).T
    dect = dec.T
    for j in range(xdtt_ref.shape[0]):
        xdtt_ref[j] = xdtt[:, j * SUBLANES:(j + 1) * SUBLANES]
        dect_ref[j] = dect[:, j * SUBLANES:(j + 1) * SUBLANES]


def _srow_call(xg, xbc, dte, lconv, h0, sconv, lp, sp):
    n = xg.shape[0]
    args = (xg, xbc, dte, lconv, h0, sconv, lp["cw"], lp["cb"], lp["wa"], lp["wx"], lp["ba"], lp["bx"],
            lp["lam"], lp["ng"], sp["cw"], sp["cb"], sp["dtbe"], sp["aloge"])
    full = lambda a: pl.BlockSpec(a.shape, lambda i, nd=a.ndim: (0,) * nd)
    outs = [jax.ShapeDtypeStruct((n, LRU_W), BF16), jax.ShapeDtypeStruct((n, LRU_W), F32),
            jax.ShapeDtypeStruct((n, SSD_INNER), F32), jax.ShapeDtypeStruct((n, SSD_GROUPS * SSD_N), F32),
            jax.ShapeDtypeStruct((n, SSD_GROUPS * SSD_N), F32),
            jax.ShapeDtypeStruct((n // SUBLANES, SSD_INNER, SUBLANES), F32),
            jax.ShapeDtypeStruct((n // SUBLANES, SSD_INNER, SUBLANES), F32),
            jax.ShapeDtypeStruct((n, CONV_W - 1, LRU_W), F32), jax.ShapeDtypeStruct((n, CONV_W - 1, SSD_CONV_DIM), F32)]
    return pl.pallas_call(
        _srow_kernel,
        out_shape=outs,
        grid=(1,),
        in_specs=[full(a) for a in args],
        out_specs=[pl.BlockSpec(o.shape, lambda i, nd=len(o.shape): (0,) * nd) for o in outs],
        scratch_shapes=[pltpu.VMEM((n, LRU_W), F32)],
        compiler_params=_cparams(("arbitrary",)),
        name="srow",
    )(*args)


def _sstate_kernel(s0_ref, xq_ref, dq_ref, bm_ref, cm_ref, xs_ref, z_ref, dexp_ref, ng_ref,
                   sn_ref, ys_ref, yraw):
    nb = s0_ref.shape[0]
    half = SSD_INNER // SSD_GROUPS
    for bi in range(nb):
        brow = jnp.concatenate(
            [jnp.broadcast_to(bm_ref[bi:bi + 1, g * SSD_N:(g + 1) * SSD_N], (half, SSD_N)) for g in range(SSD_GROUPS)],
            axis=0)
        s = dq_ref[:, bi:bi + 1] * s0_ref[bi] + xq_ref[:, bi:bi + 1] * brow
        sn_ref[bi] = s
        sb = s.astype(BF16)
        for g in range(SSD_GROUPS):
            cg = cm_ref[:, g * SSD_N:(g + 1) * SSD_N].astype(BF16)
            res = _dot_nt(cg, sb[g * half:(g + 1) * half, :])
            yraw[bi:bi + 1, g * half:(g + 1) * half] = res[bi:bi + 1, :]
    y = yraw[...] + dexp_ref[...] * xs_ref[...]
    ys_ref[...] = _rms_norm(y * _silu(z_ref[...]), ng_ref[...]).astype(BF16)


def _sstate_call(s0, xq, dq, bm, cm, xs, z, dexp, ng, nb):
    n = s0.shape[0]
    row = lambda i: (i, 0)
    vec = lambda i: (0, 0)
    gn = SSD_GROUPS * SSD_N
    return pl.pallas_call(
        _sstate_kernel,
        out_shape=[jax.ShapeDtypeStruct(s0.shape, F32), jax.ShapeDtypeStruct((n, SSD_INNER), BF16)],
        grid=(n // nb,),
        in_specs=[pl.BlockSpec((nb, SSD_INNER, SSD_N), lambda i: (i, 0, 0)),
                  pl.BlockSpec((None, SSD_INNER, nb), lambda i: (i, 0, 0)),
                  pl.BlockSpec((None, SSD_INNER, nb), lambda i: (i, 0, 0)),
                  pl.BlockSpec((nb, gn), row), pl.BlockSpec((nb, gn), row),
                  pl.BlockSpec((nb, SSD_INNER), row), pl.BlockSpec((nb, SSD_INNER), row),
                  pl.BlockSpec((1, SSD_INNER), vec), pl.BlockSpec((1, SSD_INNER), vec)],
        out_specs=[pl.BlockSpec((nb, SSD_INNER, SSD_N), lambda i: (i, 0, 0)), pl.BlockSpec((nb, SSD_INNER), row)],
        scratch_shapes=[pltpu.VMEM((nb, SSD_INNER), F32)],
        compiler_params=_cparams(("parallel",)),
        name="sstate",
    )(s0, xq, dq, bm, cm, xs, z, dexp, ng)


def _post_kernel(yl_ref, ys_ref, x_ref, g1_ref, sh2_ref, sc2_ref, wo_ref, l1g_ref, l1b_ref, wrt_ref, brc_ref,
                 x1_ref, v_ref, eid_ref, gw_ref, *, per_batch):
    b = pl.program_id(0)
    o = _dot(yl_ref[...], wo_ref[0:LRU_W, :]) + _dot(ys_ref[...], wo_ref[LRU_W:LRU_W + SSD_INNER, :])
    x1 = _layer_norm(ALPHA * x_ref[...] + _mod_rows(g1_ref, per_batch, b) * o) * l1g_ref[...] + l1b_ref[...]
    x1_ref[...] = x1
    v = _layer_norm(x1) * (1.0 + _mod_rows(sc2_ref, per_batch, b)) + _mod_rows(sh2_ref, per_batch, b)
    _store_row_tiles(v_ref, v)

    lt = _dot(v.astype(BF16), wrt_ref[...]).T + brc_ref[...]
    tl = lt.shape[1]
    row = lax.broadcasted_iota(I32, (GROUP_SIZE, tl), 0).astype(F32)
    big = float(GROUP_SIZE)
    neg = -jnp.inf
    lg = jnp.where(row < N_GROUPS, lt[0:GROUP_SIZE, :], neg)
    gmax = jnp.max(lg, axis=0, keepdims=True)
    gsel = jnp.min(jnp.where(lg == gmax, row, big), axis=0, keepdims=True)
    pg = 1.0 / jnp.sum(jnp.exp(lg - gmax), axis=0, keepdims=True)
    le = lt[GROUP_SIZE:2 * GROUP_SIZE, :]
    for j in range(1, N_GROUPS):
        le = jnp.where(gsel == j, lt[GROUP_SIZE * (j + 1):GROUP_SIZE * (j + 2), :], le)
    m1 = jnp.max(le, axis=0, keepdims=True)
    i1 = jnp.min(jnp.where(le == m1, row, big), axis=0, keepdims=True)
    rest = jnp.where(row == i1, neg, le)
    m2 = jnp.max(rest, axis=0, keepdims=True)
    i2 = jnp.min(jnp.where(rest == m2, row, big), axis=0, keepdims=True)
    e2 = jnp.exp(m2 - m1)
    den = 1.0 + e2
    eid = jnp.where(row == 0, gsel * GROUP_SIZE + i1, jnp.where(row == 1, gsel * GROUP_SIZE + i2, 0.0))
    eid_ref[...] = eid.astype(I32)
    gw_ref[...] = jnp.where(row == 0, pg * (1.0 / den), jnp.where(row == 1, pg * (e2 / den), 0.0))


def _post_call(yl, ys, x3, mod, p, tl):
    nb, L, _ = x3.shape
    nl = L // tl
    row = lambda b, l: (b, l, 0)
    vec = lambda b, l: (0, 0)
    tok = lambda b, l: (b * nl + l, 0)
    tokt = lambda b, l: (0, b * nl + l)
    bof, tof = (lambda b, l: b), (lambda b, l: l)
    return pl.pallas_call(
        functools.partial(_post_kernel, per_batch=mod.per_batch),
        out_shape=[jax.ShapeDtypeStruct((nb, L, D), F32), jax.ShapeDtypeStruct(_tile_rows(nb * L), F32),
                   jax.ShapeDtypeStruct((SUBLANES, nb * L), I32), jax.ShapeDtypeStruct((SUBLANES, nb * L), F32)],
        grid=(nb, nl),
        in_specs=[pl.BlockSpec((None, tl, LRU_W), row), pl.BlockSpec((None, tl, SSD_INNER), row),
                  pl.BlockSpec((None, tl, D), row),
                  mod.spec(2, tl, bof, tof), mod.spec(3, tl, bof, tof), mod.spec(4, tl, bof, tof),
                  pl.BlockSpec((LRU_W + SSD_INNER, D), vec), pl.BlockSpec((1, D), vec), pl.BlockSpec((1, D), vec),
                  pl.BlockSpec((D, LANES), vec), pl.BlockSpec((LANES, 1), vec)],
        out_specs=[pl.BlockSpec((None, tl, D), row), pl.BlockSpec(_tile_rows(tl), tok),
                   pl.BlockSpec((SUBLANES, tl), tokt), pl.BlockSpec((SUBLANES, tl), tokt)],
        compiler_params=_cparams(("parallel", "arbitrary")),
        name="post",
    )(yl, ys, x3, mod.table, mod.table, mod.table, p["wo"], p["l1g"], p["l1b"], p["wrt"], p["brc"])


def _route_kernel(eidp_ref, eids_ref, dest_ref, cnt_ref, run, poff, *, p_tiles):
    ph = pl.program_id(0)
    t = pl.program_id(1)
    n = ROUTE_TILE
    eid = jnp.where(t < p_tiles, eidp_ref[...], eids_ref[...])
    rowi = lax.broadcasted_iota(I32, (LANES, n), 0)
    oh0 = rowi == eid[0:1, :]
    oh1 = rowi == eid[1:2, :]
    oh = oh0.astype(F32) + oh1.astype(F32)
    tile_cnt = jnp.sum(oh, axis=1, keepdims=True)

    @pl.when((ph == 0) & (t == 0))
    def _():
        run[...] = jnp.zeros((LANES, n), F32)

    @pl.when(ph == 0)
    def _():
        run[...] = run[...] + tile_cnt

    @pl.when((ph == 1) & (t == 0))
    def _():
        counts = run[...]
        cnt_ref[...] = counts
        ci = counts.astype(I32)
        q = jnp.floor(counts * (1.0 / MOE_BM)).astype(I32)
        rem = ci - q * MOE_BM
        q = q + jnp.where(rem >= MOE_BM, 1, 0) - jnp.where(rem < 0, 1, 0)
        nblk = q + jnp.where(ci - q * MOE_BM > 0, 1, 0)
        r = lax.broadcasted_iota(I32, (LANES, LANES), 0)
        c = lax.broadcasted_iota(I32, (LANES, LANES), 1)
        lower = (r > c).astype(BF16)
        poff[...] = _dot(lower, nblk.astype(F32).astype(BF16)) * float(MOE_BM)
        run[...] = jnp.zeros((LANES, n), F32)

    @pl.when(ph == 1)
    def _():
        r = lax.broadcasted_iota(I32, (n, n), 0)
        c = lax.broadcasted_iota(I32, (n, n), 1)
        before = (r < c).astype(BF16)
        slot = _dot(oh.astype(BF16), before) + run[...] + poff[...]
        d0 = jnp.sum(jnp.where(oh0, slot, 0.0), axis=0, keepdims=True)
        d1 = jnp.sum(jnp.where(oh1, slot, 0.0), axis=0, keepdims=True)
        row = lax.broadcasted_iota(I32, (SUBLANES, n), 0)
        dest_ref[...] = jnp.where(row == 0, d0, jnp.where(row == 1, d1, 0.0)).astype(I32)
        run[...] = run[...] + tile_cnt


def _route_call(eid_p, eid_s):
    n = ROUTE_TILE
    p_tiles = eid_p.shape[1] // n
    s_tiles = eid_s.shape[1] // n
    t_all = (p_tiles + s_tiles) * n
    return pl.pallas_call(
        functools.partial(_route_kernel, p_tiles=p_tiles),
        out_shape=[jax.ShapeDtypeStruct((SUBLANES, t_all), I32), jax.ShapeDtypeStruct((LANES, n), F32)],
        grid=(2, p_tiles + s_tiles),
        in_specs=[pl.BlockSpec((SUBLANES, n), lambda ph, t: (0, jnp.minimum(t, p_tiles - 1))),
                  pl.BlockSpec((SUBLANES, n), lambda ph, t: (0, jnp.maximum(t - p_tiles, 0)))],
        out_specs=[pl.BlockSpec((SUBLANES, n), lambda ph, t: (0, t * ph)),
                   pl.BlockSpec((LANES, n), lambda ph, t: (0, 0))],
        scratch_shapes=[pltpu.VMEM((LANES, n), F32), pltpu.VMEM((LANES, n), F32)],
        compiler_params=_cparams(("arbitrary", "arbitrary")),
        name="route",
    )(eid_p, eid_s)


def _row_copy(src, src_row, dst, dst_row, sem):
    s0 = pl.multiple_of(src_row * SUBLANES, SUBLANES)
    d0 = pl.multiple_of(dst_row * SUBLANES, SUBLANES)
    return pltpu.make_async_copy(src.at[pl.ds(s0, SUBLANES), :], dst.at[pl.ds(d0, SUBLANES), :], sem)


def _scatter_rows(dest_ref, v_ref, xpad_ref, sem, tl):
    def start(r, carry):
        for k in range(2):
            _row_copy(v_ref, r, xpad_ref, dest_ref[k, r], sem).start(priority=k)
        return carry

    def wait(r, carry):
        for k in range(2):
            _row_copy(v_ref, r, xpad_ref, dest_ref[k, r], sem).wait()
        return carry

    lax.fori_loop(0, tl, start, 0, unroll=DMA_UNROLL)
    lax.fori_loop(0, tl, wait, 0, unroll=DMA_UNROLL)


def _dispatch_kernel(zflag_ref, destp_ref, dests_ref, vp_ref, vs_ref, xpad_ref, zbuf, sem, zsem,
                     *, tl_p, tl_s, p_tiles, n_blocks):
    blk_rows = _tile_rows(MOE_BM)[0]

    @pl.when(pl.program_id(0) == 0)
    def _():
        zbuf[...] = jnp.zeros(zbuf.shape, F32)

        def zero_block(go):
            def body(b, carry):
                @pl.when(zflag_ref[b] != 0)
                def _():
                    r0 = pl.multiple_of(b * blk_rows, blk_rows)
                    cp = pltpu.make_async_copy(zbuf, xpad_ref.at[pl.ds(r0, blk_rows), :], zsem)
                    if go:
                        cp.start()
                    else:
                        cp.wait()
                return carry
            lax.fori_loop(0, n_blocks, body, 0)

        zero_block(True)
        zero_block(False)

    @pl.when(pl.program_id(0) < p_tiles)
    def _():
        _scatter_rows(destp_ref, vp_ref, xpad_ref, sem, tl_p)

    @pl.when(pl.program_id(0) >= p_tiles)
    def _():
        _scatter_rows(dests_ref, vs_ref, xpad_ref, sem, tl_s)


def _dispatch_call(zflag, dest, v_p, v_s, n_blocks, tl_p, tl_s):
    p_tiles = v_p.shape[0] // _tile_rows(tl_p)[0]
    s_tiles = v_s.shape[0] // _tile_rows(tl_s)[0]
    s_off = p_tiles * tl_p // tl_s
    return pl.pallas_call(
        functools.partial(_dispatch_kernel, tl_p=tl_p, tl_s=tl_s, p_tiles=p_tiles, n_blocks=n_blocks),
        out_shape=jax.ShapeDtypeStruct(_tile_rows(n_blocks * MOE_BM), F32),
        grid_spec=pltpu.PrefetchScalarGridSpec(
            num_scalar_prefetch=1,
            grid=(p_tiles + s_tiles,),
            in_specs=[pl.BlockSpec((SUBLANES, tl_p), lambda t, z: (0, jnp.minimum(t, p_tiles - 1)),
                                   memory_space=pltpu.SMEM),
                      pl.BlockSpec((SUBLANES, tl_s), lambda t, z: (0, s_off + jnp.maximum(t - p_tiles, 0)),
                                   memory_space=pltpu.SMEM),
                      pl.BlockSpec(_tile_rows(tl_p), lambda t, z: (jnp.minimum(t, p_tiles - 1), 0)),
                      pl.BlockSpec(_tile_rows(tl_s), lambda t, z: (jnp.maximum(t - p_tiles, 0), 0))],
            out_specs=pl.BlockSpec(memory_space=pl.ANY),
            scratch_shapes=[pltpu.VMEM(_tile_rows(MOE_BM), F32), pltpu.SemaphoreType.DMA, pltpu.SemaphoreType.DMA]),
        compiler_params=_cparams(("arbitrary",)),
        name="dispatch",
    )(zflag, dest, dest, v_p, v_s)


def _expert_kernel(be_ref, nb_ref, nxt_ref, slot_ref, x_ref, wg_hbm, wu_hbm, wd_hbm, o_ref,
                   wgf, wuf, wdf, wgb, wub, wdb, sems):
    i = pl.program_id(0)

    def weights(e, slot, go):
        for m, (hbm, buf) in enumerate(((wg_hbm, wgf), (wu_hbm, wuf), (wd_hbm, wdf))):
            cp = pltpu.make_async_copy(hbm.at[e], buf.at[slot], sems.at[slot, m])
            if go:
                cp.start()
            else:
                cp.wait()

    @pl.when(i < nb_ref[0])
    def _():
        e = be_ref[i]
        slot = slot_ref[e]

        @pl.when(i == 0)
        def _():
            weights(e, slot, True)

        @pl.when((i == 0) | (e != be_ref[jnp.maximum(i - 1, 0)]))
        def _():
            weights(e, slot, False)
            nxt = nxt_ref[e]

            @pl.when(nxt >= 0)
            def _():
                weights(nxt, 1 - slot, True)

            wgb[...] = wgf[slot].astype(BF16)
            wub[...] = wuf[slot].astype(BF16)
            wdb[...] = wdf[slot].astype(BF16)

        x = _load_row_tiles(x_ref).astype(BF16)
        h = _silu(_dot(x, wgb[...])) * _dot(x, wub[...])
        _store_row_tiles(o_ref, _dot(h.astype(BF16), wdb[...]))

    @pl.when(i >= nb_ref[0])
    def _():
        o_ref[...] = jnp.zeros(o_ref.shape, F32)


def _expert_call(blk_e, nblk, nxt_e, slot_e, xpad, w_gate, w_up, w_down):
    n_rows = xpad.shape[0] // SUBLANES
    blk = lambda i, be, nb, nx, sl: (jnp.minimum(i, nb[0] - 1), 0)
    hbm = pl.BlockSpec(memory_space=pl.ANY)
    return pl.pallas_call(
        _expert_kernel,
        out_shape=jax.ShapeDtypeStruct(_tile_rows(n_rows), F32),
        grid_spec=pltpu.PrefetchScalarGridSpec(
            num_scalar_prefetch=4,
            grid=(n_rows // MOE_BM,),
            in_specs=[pl.BlockSpec(_tile_rows(MOE_BM), blk), hbm, hbm, hbm],
            out_specs=pl.BlockSpec(_tile_rows(MOE_BM), lambda i, be, nb, nx, sl: (i, 0)),
            scratch_shapes=[pltpu.VMEM((2, D, D_FF), F32), pltpu.VMEM((2, D, D_FF), F32), pltpu.VMEM((2, D_FF, D), F32),
                            pltpu.VMEM((D, D_FF), BF16), pltpu.VMEM((D, D_FF), BF16), pltpu.VMEM((D_FF, D), BF16),
                            pltpu.SemaphoreType.DMA((2, 3))]),
        compiler_params=_cparams(("arbitrary",)),
        name="expert",
    )(blk_e, nblk, nxt_e, slot_e, xpad, w_gate, w_up, w_down)


def _combine_kernel(dest_ref, gw0_ref, gw1_ref, x1_ref, g2_ref, l2g_ref, l2b_ref, ypad_ref, y_ref, ybuf, sems,
                    *, tok_off, k_stride, tl, nl, per_batch):
    i = pl.program_id(0)

    def gather(tile, slot, go):
        base = tok_off + tile * tl

        def body(r, carry):
            for k in range(2):
                cp = _row_copy(ypad_ref, dest_ref[k * k_stride + base + r], ybuf.at[slot, k], r, sems.at[slot])
                if go:
                    cp.start(priority=k)
                else:
                    cp.wait()
            return carry

        lax.fori_loop(0, tl, body, 0, unroll=DMA_UNROLL)

    @pl.when(i == 0)
    def _():
        gather(0, 0, True)

    @pl.when(i + 1 < pl.num_programs(0))
    def _():
        gather(i + 1, (i + 1) % 2, True)

    slot = i % 2
    gather(i, slot, False)
    f = _load_row_tiles(ybuf.at[slot, 0]) * gw0_ref[...] + _load_row_tiles(ybuf.at[slot, 1]) * gw1_ref[...]
    g2 = _mod_rows(g2_ref, per_batch, i // nl)
    y_ref[...] = _layer_norm(ALPHA * x1_ref[...] + g2 * f) * l2g_ref[...] + l2b_ref[...]


def _combine_call(dest, gw0, gw1, x1, mod, l2g, l2b, ypad, tok_off, tl):
    nb, L, _ = x1.shape
    nl = L // tl
    row = lambda i, d: (i // nl, i % nl, 0)
    vec = lambda i, d: (0, 0)
    tok = lambda i, d: (i, 0)
    g2_spec = mod.spec(5, tl, lambda i, d: i // nl, lambda i, d: i % nl)
    return pl.pallas_call(
        functools.partial(_combine_kernel, tok_off=tok_off, k_stride=dest.shape[1], tl=tl, nl=nl,
                          per_batch=mod.per_batch),
        out_shape=jax.ShapeDtypeStruct((nb, L, D), F32),
        grid_spec=pltpu.PrefetchScalarGridSpec(
            num_scalar_prefetch=1,
            grid=(nb * nl,),
            in_specs=[pl.BlockSpec((tl, 1), tok), pl.BlockSpec((tl, 1), tok),
                      pl.BlockSpec((None, tl, D), row), g2_spec,
                      pl.BlockSpec((1, D), vec), pl.BlockSpec((1, D), vec),
                      pl.BlockSpec(memory_space=pl.ANY)],
            out_specs=pl.BlockSpec((None, tl, D), row),
            scratch_shapes=[pltpu.VMEM((2, 2) + _tile_rows(tl), F32), pltpu.SemaphoreType.DMA((2,))]),
        compiler_params=_cparams(("arbitrary",)),
        name="combine",
    )(dest[:2].reshape(-1), gw0, gw1, x1, mod.table, l2g, l2b, ypad)


def _block_diag(w):
    nh, blk, _ = w.shape
    w4 = w.reshape(nh // LRU_PACK, LRU_PACK, blk, blk)
    eye = jnp.eye(LRU_PACK, dtype=w.dtype)
    out = jnp.einsum("gaij,ab->gaibj", w4, eye)
    return out.reshape(nh // LRU_PACK, LRU_PACK * blk, LRU_PACK * blk).astype(BF16)


def _row(v):
    return v.reshape(1, -1).astype(F32)


def _pad_lanes(v):
    return jnp.pad(v.reshape(1, -1).astype(F32), ((0, 0), (0, LANES - v.shape[-1])))


def kernel(x_prompt, x_sample, c_prompt, c_sample, state_lru_conv, state_lru_h, state_ssd_conv, state_ssd, w_ada, b_ada, w_in, lru_conv_w, lru_conv_b, lru_wa, lru_ba, lru_wx, lru_bx, lru_lambda, lru_norm_g, ssd_conv_w, ssd_conv_b, ssd_dt_bias, ssd_a_log, ssd_d, ssd_norm_g, w_out, ln1_g, ln1_b, w_rg, b_rg, w_re, b_re, w_gate, w_up, w_down, ln2_g, ln2_b):
    assert w_ada.shape[0] == DEPTH == 1
    nbp, seq, _ = x_prompt.shape
    nbs = x_sample.shape[0]
    t_p = nbp * seq
    t_all = t_p + nbs
    assert x_sample.shape[1] == 1 and t_p % ROUTE_TILE == 0 and nbs % TOK_TILE == 0 and nbp == SUBLANES

    c_rows = -(-(nbp + nbs) // 16) * 16
    c_all = jnp.pad(jnp.concatenate([c_sample, c_prompt], axis=0), ((0, c_rows - nbp - nbs), (0, 0)))
    w_in0 = w_in[0]
    o_z, o_xbc, o_dt = 2 * LRU_W, 2 * LRU_W + SSD_INNER, 2 * LRU_W + SSD_INNER + SSD_CONV_DIM
    w_dt = w_in0[:, o_dt:]
    ws_p = list(_wcast_call(w_in0, ((0, o_z), (o_z, o_xbc), (o_xbc, o_dt)), 256))
    ws_s = ws_p + [jnp.repeat(w_dt, SSD_P, axis=1).astype(BF16)]
    ws_p = ws_p + [jnp.pad(w_dt, ((0, 0), (0, LANES - SSD_HEADS))).astype(BF16)]
    lp = dict(cw=lru_conv_w[0], cb=_row(lru_conv_b[0]), wa=_block_diag(lru_wa[0]), wx=_block_diag(lru_wx[0]),
              ba=_row(lru_ba[0]), bx=_row(lru_bx[0]), lam=_row(lru_lambda[0]), ng=_row(lru_norm_g[0]))
    sp = dict(cw=ssd_conv_w[0], cb=_row(ssd_conv_b[0]), dtb=_pad_lanes(ssd_dt_bias[0]), alog=_pad_lanes(ssd_a_log[0]),
              dexp=_row(jnp.repeat(ssd_d[0], SSD_P)), ng=_row(ssd_norm_g[0]),
              dtbe=_row(jnp.repeat(ssd_dt_bias[0], SSD_P)), aloge=_row(jnp.repeat(ssd_a_log[0], SSD_P)))
    wrt = jnp.zeros((D, LANES), F32).at[:, 0:N_GROUPS].set(w_rg[0]).at[:, GROUP_SIZE:ROUTE_ROWS].set(w_re[0]).astype(BF16)
    brc = jnp.zeros((LANES, 1), F32).at[0:N_GROUPS, 0].set(b_rg[0]).at[GROUP_SIZE:ROUTE_ROWS, 0].set(b_re[0])
    pp = dict(wo=w_out[0].astype(BF16), l1g=_row(ln1_g[0]), l1b=_row(ln1_b[0]), wrt=wrt, brc=brc)

    table = _ada_call(c_all, w_ada[0], _row(b_ada[0]))
    mod_p = _Mod(table, nbs, True)
    mod_s = _Mod(table, 0, False)

    xg, z, xbc, dtr, p_lru_conv, p_ssd_conv = _proj_conv_call(x_prompt, mod_p, ws_p, lp, sp, SEQ_TILE)
    yl, p_lru_h = _lru_call(xg, lp, SEQ_TILE)
    ys, p_ssd = _ssd_call(xbc, z, dtr, sp, SEQ_TILE // SSD_Q)
    x1_p, v_p, eid_p, gw_p = _post_call(yl, ys, x_prompt, mod_p, pp, SEQ_TILE)

    xs3 = x_sample.reshape(1, nbs, D)
    xg_s, z_s, xbc_s, dte_s = _proj_call(xs3, mod_s, ws_s, nbs)
    xg_s, z_s, xbc_s, dte_s = xg_s[0], z_s[0], xbc_s[0], dte_s[0]
    yl_s, h_s, xs_s, bm_s, cm_s, xdtt, dect, s_lru_conv, s_ssd_conv = _srow_call(
        xg_s, xbc_s, dte_s, state_lru_conv[0].reshape(nbs, -1), state_lru_h[0],
        state_ssd_conv[0].reshape(nbs, -1), lp, sp)
    s_new, ys_s = _sstate_call(state_ssd[0].reshape(nbs, SSD_INNER, SSD_N), xdtt, dect,
                               bm_s, cm_s, xs_s, z_s, sp["dexp"], sp["ng"], SUBLANES)
    x1_s, v_s, eid_s, gw_s = _post_call(yl_s[None], ys_s[None], xs3, mod_s, pp, nbs)

    s_pad = -nbs % ROUTE_TILE
    dest, counts = _route_call(eid_p, jnp.pad(eid_s, ((0, 0), (0, s_pad)), constant_values=-1))
    counts = counts[:N_EXPERTS, 0].astype(I32)
    n_blocks = -(-(2 * t_all) // MOE_BM) + N_EXPERTS
    pend = jnp.cumsum(((counts + MOE_BM - 1) // MOE_BM) * MOE_BM)
    blk_start = jnp.arange(n_blocks, dtype=I32) * MOE_BM
    blk_e = jnp.minimum(jnp.sum((pend[None, :] <= blk_start[:, None]).astype(I32), axis=1), N_EXPERTS - 1)
    nblk = (pend[-1:] // MOE_BM).astype(I32)
    blk_ids = jnp.arange(n_blocks, dtype=I32)
    last_of_expert = jnp.any(((pend // MOE_BM - 1)[None, :] == blk_ids[:, None]) & (counts > 0)[None, :], axis=1)
    zflag = (last_of_expert | (blk_ids >= nblk[0])).astype(I32)
    used = counts > 0
    eids = jnp.arange(N_EXPERTS, dtype=I32)
    later_used = (eids[None, :] > eids[:, None]) & used[None, :]
    nxt_e = jnp.min(jnp.where(later_used, eids[None, :], N_EXPERTS), axis=1)
    nxt_e = jnp.where(nxt_e == N_EXPERTS, -1, nxt_e).astype(I32)
    slot_e = ((jnp.cumsum(used.astype(I32)) - used.astype(I32)) % 2).astype(I32)
    xpad = _dispatch_call(zflag, dest, v_p, v_s, n_blocks, 4 * TOK_TILE, TOK_TILE)
    ypad = _expert_call(blk_e, nblk, nxt_e, slot_e, xpad, w_gate[0], w_up[0], w_down[0])
    l2g, l2b = _row(ln2_g[0]), _row(ln2_b[0])
    y_p = _combine_call(dest, gw_p[0].reshape(t_p, 1), gw_p[1].reshape(t_p, 1), x1_p, mod_p, l2g, l2b, ypad, 0,
                        4 * TOK_TILE)
    y_s = _combine_call(dest, gw_s[0].reshape(nbs, 1), gw_s[1].reshape(nbs, 1), x1_s, mod_s, l2g, l2b, ypad, t_p,
                        TOK_TILE)

    return (y_p, y_s.reshape(nbs, 1, D),
            p_lru_conv[None], p_lru_h.reshape(1, nbp, LRU_W), p_ssd_conv[None],
            p_ssd.reshape(1, nbp, SSD_HEADS, SSD_P, SSD_N),
            s_lru_conv[None], h_s[None], s_ssd_conv[None],
            s_new.reshape(1, nbs, SSD_HEADS, SSD_P, SSD_N))
```

```python
import functools
import math

import jax
import jax.numpy as jnp
from jax import lax
from jax.experimental import pallas as pl
from jax.experimental.pallas import tpu as pltpu

F32 = jnp.float32
BF16 = jnp.bfloat16
I32 = jnp.int32
HIGHEST = lax.Precision.HIGHEST

D = 1024
DEPTH = 1
CONV_W = 4
LRU_W = D
LRU_HEADS = 16
LRU_C = 8.0
LRU_PACK = 4
LRU_PACK_W = LRU_PACK * (LRU_W // LRU_HEADS)
SSD_INNER = D
SSD_HEADS = 16
SSD_P = SSD_INNER // SSD_HEADS
SSD_GROUPS = 2
SSD_N = 128
SSD_Q = 128
SSD_CONV_DIM = SSD_INNER + 2 * SSD_GROUPS * SSD_N
N_GROUPS = 4
GROUP_SIZE = 8
N_EXPERTS = N_GROUPS * GROUP_SIZE
D_FF = D // 2
N_MOD = 6
LN_EPS = 1e-5
RMS_EPS = 1e-6
ALPHA = (2.0 * DEPTH) ** 0.25

LANES = 128
SUBLANES = 8
VMEM_LIMIT = 56 * 1024 * 1024
SEQ_TILE = 512
TOK_TILE = 128
ROUTE_TILE = 1024
DMA_UNROLL = 8
MOE_BM = 384
ROUTE_ROWS = 40

NT_DIMS = (((1,), (1,)), ((), ()))


def _tile_rows(rows):
    return (rows * SUBLANES, LANES)


def _cparams(sem):
    return pltpu.CompilerParams(dimension_semantics=sem, vmem_limit_bytes=VMEM_LIMIT)


def _sigmoid(x):
    return 0.5 * (jnp.tanh(0.5 * x) + 1.0)


def _silu(x):
    return x * _sigmoid(x)


def _softplus(x):
    return jnp.maximum(x, 0.0) + jnp.log1p(jnp.exp(-jnp.abs(x)))


def _gelu_tanh(x):
    return 0.5 * x * (1.0 + jnp.tanh(math.sqrt(2.0 / math.pi) * (x + 0.044715 * (x * x * x))))


def _layer_norm(x):
    mu = jnp.mean(x, axis=-1, keepdims=True)
    xc = x - mu
    var = jnp.mean(xc * xc, axis=-1, keepdims=True)
    return xc * lax.rsqrt(var + LN_EPS)


def _rms_norm(x, g):
    return x * lax.rsqrt(jnp.mean(x * x, axis=-1, keepdims=True) + RMS_EPS) * g


def _dot(a, b, **kw):
    return jnp.dot(a, b, preferred_element_type=F32, **kw)


def _dot_nt(a, b, **kw):
    return lax.dot_general(a, b, NT_DIMS, preferred_element_type=F32, **kw)


def _store_row_tiles(ref, val):
    rows = val.shape[0]
    for j in range(SUBLANES):
        ref[pl.ds(j, rows, stride=SUBLANES), :] = val[:, j * LANES:(j + 1) * LANES]


def _load_row_tiles(ref):
    rows = ref.shape[0] // SUBLANES
    return jnp.concatenate([ref[pl.ds(j, rows, stride=SUBLANES), :] for j in range(SUBLANES)], axis=1)


def _lru_gates(xc, xb, wa, wx, ba, bx, sp):
    r = _sigmoid(_dot(xb, wa) + ba)
    i = _sigmoid(_dot(xb, wx) + bx)
    log_a = (-LRU_C) * r * sp
    a = jnp.exp(log_a)
    mult = jnp.sqrt(jnp.tanh(-log_a) * (a * a + 1.0))
    return a, mult * (i * xc)


def _ada_kernel(c_ref, w_ref, b_ref, o_ref):
    s = _silu(c_ref[...]).astype(BF16)
    o_ref[...] = _dot(s, w_ref[...].astype(BF16)) + b_ref[...]


def _ada_call(c_all, w_ada, b_ada):
    rows = c_all.shape[0]
    tn = 512
    per_mod = D // tn
    return pl.pallas_call(
        _ada_kernel,
        out_shape=jax.ShapeDtypeStruct((N_MOD, rows, D), F32),
        grid=(N_MOD * per_mod,),
        in_specs=[pl.BlockSpec((rows, D), lambda j: (0, 0)),
                  pl.BlockSpec((D, tn), lambda j: (0, j)),
                  pl.BlockSpec((1, tn), lambda j: (0, j))],
        out_specs=pl.BlockSpec((None, rows, tn), lambda j: (j // per_mod, 0, j % per_mod)),
        compiler_params=_cparams(("arbitrary",)),
        name="ada",
    )(c_all, w_ada, b_ada)


def _wcast_kernel(wt_ref, tail_ref, *o_refs, firsts):
    j = pl.program_id(0)
    *o_refs, tail_out = o_refs
    tail_out[...] = tail_ref[...].astype(BF16)
    wb = wt_ref[...].T.astype(BF16)
    for k, o_ref in enumerate(o_refs):
        last = firsts[k + 1] if k + 1 < len(firsts) else pl.num_programs(0)

        @pl.when((j >= firsts[k]) & (j < last))
        def _():
            o_ref[...] = wb


def _wcast_call(wt, bounds, tc):
    n, k_dim = wt.shape
    assert all(lo % tc == 0 and hi % tc == 0 for lo, hi in bounds)
    assert all(a[1] == b[0] for a, b in zip(bounds, bounds[1:]))
    firsts = tuple((lo - bounds[0][0]) // tc for lo, _ in bounds)
    n_steps = (bounds[-1][1] - bounds[0][0]) // tc
    base = bounds[0][0] // tc
    tail = n - bounds[-1][1]
    assert tail > 0 and bounds[-1][1] % tail == 0

    def out_spec(first, nblk):
        return pl.BlockSpec((k_dim, tc), lambda j: (0, jnp.clip(j - first, 0, nblk - 1)))

    return pl.pallas_call(
        functools.partial(_wcast_kernel, firsts=firsts),
        out_shape=[jax.ShapeDtypeStruct((k_dim, hi - lo), BF16) for lo, hi in bounds]
                  + [jax.ShapeDtypeStruct((tail, k_dim), BF16)],
        grid=(n_steps,),
        in_specs=[pl.BlockSpec((tc, k_dim), lambda j: (base + j, 0)),
                  pl.BlockSpec((tail, k_dim), lambda j: (bounds[-1][1] // tail, 0))],
        out_specs=[out_spec(f, (hi - lo) // tc) for f, (lo, hi) in zip(firsts, bounds)]
                  + [pl.BlockSpec((tail, k_dim), lambda j: (0, 0))],
        compiler_params=_cparams(("arbitrary",)),
        name="wcast",
    )(wt, wt)


class _Mod:
    def __init__(self, table, row0, per_batch):
        self.table, self.row0, self.per_batch = table, row0, per_batch

    def spec(self, k, tl, batch_of, tile_of):
        if self.per_batch:
            blk = self.row0 // SUBLANES
            return pl.BlockSpec((None, SUBLANES, D), lambda *g: (k, blk, 0))
        blk = self.row0 // tl
        return pl.BlockSpec((None, tl, D), lambda *g: (k, blk + tile_of(*g), 0))


def _mod_rows(ref, per_batch, b):
    return ref[pl.ds(b, 1), :] if per_batch else ref[...]


def _proj_kernel(x_ref, sh_ref, sc_ref, w1_ref, w2_ref, w3_ref, w4_ref, o1_ref, o2_ref, o3_ref, o4_ref, *, per_batch):
    b = pl.program_id(0)
    u = _layer_norm(x_ref[...]) * (1.0 + _mod_rows(sc_ref, per_batch, b)) + _mod_rows(sh_ref, per_batch, b)
    ub = u.astype(BF16)
    o1_ref[...] = _dot(ub, w1_ref[...])
    o2_ref[...] = _dot(ub, w2_ref[...])
    o3_ref[...] = _dot(ub, w3_ref[...])
    o4_ref[...] = _dot(ub, w4_ref[...])


def _causal_conv(xpad, x_new, cw_ref, cb_ref, conv_ref, tl):
    xpad[SUBLANES:SUBLANES + tl, :] = x_new
    off = SUBLANES - (CONV_W - 1)
    acc = xpad[off:off + tl, :] * cw_ref[0:1, :]
    for k in range(1, CONV_W):
        acc = acc + xpad[off + k:off + k + tl, :] * cw_ref[k:k + 1, :]
    conv_ref[...] = xpad[SUBLANES + tl - (CONV_W - 1):SUBLANES + tl, :]
    xpad[0:SUBLANES, :] = xpad[tl:tl + SUBLANES, :]
    return acc + cb_ref[...]


def _proj_conv_kernel(x_ref, sh_ref, sc_ref, w1_ref, w2_ref, w3_ref, w4_ref, lcw_ref, lcb_ref, scw_ref, scb_ref,
                      o1_ref, o2_ref, o3_ref, o4_ref, lconv_ref, sconv_ref, lpad, spad, *, tl):
    b = pl.program_id(0)

    @pl.when(pl.program_id(1) == 0)
    def _():
        lpad[0:SUBLANES, :] = jnp.zeros((SUBLANES, LRU_W), F32)
        spad[0:SUBLANES, :] = jnp.zeros((SUBLANES, SSD_CONV_DIM), F32)

    u = _layer_norm(x_ref[...]) * (1.0 + _mod_rows(sc_ref, True, b)) + _mod_rows(sh_ref, True, b)
    ub = u.astype(BF16)
    o1_ref[:, 0:LRU_W] = _causal_conv(lpad, _dot(ub, w1_ref[:, 0:LRU_W]), lcw_ref, lcb_ref, lconv_ref, tl)
    o1_ref[:, LRU_W:2 * LRU_W] = _dot(ub, w1_ref[:, LRU_W:2 * LRU_W])
    o2_ref[...] = _dot(ub, w2_ref[...])
    o3_ref[...] = _silu(_causal_conv(spad, _dot(ub, w3_ref[...]), scw_ref, scb_ref, sconv_ref, tl))
    o4_ref[...] = _dot(ub, w4_ref[...])


def _proj_conv_call(x3, mod, ws, lp, sp, tl):
    nb, L, _ = x3.shape
    widths = [w.shape[1] for w in ws]
    row = lambda b, l: (b, l, 0)
    full = lambda b, l: (0, 0)
    st = lambda b, l: (b, 0, 0)
    bof, tof = (lambda b, l: b), (lambda b, l: l)
    return pl.pallas_call(
        functools.partial(_proj_conv_kernel, tl=tl),
        out_shape=[jax.ShapeDtypeStruct((nb, L, n), F32) for n in widths]
                  + [jax.ShapeDtypeStruct((nb, CONV_W - 1, LRU_W), F32),
                     jax.ShapeDtypeStruct((nb, CONV_W - 1, SSD_CONV_DIM), F32)],
        grid=(nb, L // tl),
        in_specs=[pl.BlockSpec((None, tl, D), row), mod.spec(0, tl, bof, tof), mod.spec(1, tl, bof, tof)]
                 + [pl.BlockSpec((D, n), full) for n in widths]
                 + [pl.BlockSpec((CONV_W, LRU_W), full), pl.BlockSpec((1, LRU_W), full),
                    pl.BlockSpec((CONV_W, SSD_CONV_DIM), full), pl.BlockSpec((1, SSD_CONV_DIM), full)],
        out_specs=[pl.BlockSpec((None, tl, n), row) for n in widths]
                  + [pl.BlockSpec((None, CONV_W - 1, LRU_W), st), pl.BlockSpec((None, CONV_W - 1, SSD_CONV_DIM), st)],
        scratch_shapes=[pltpu.VMEM((tl + SUBLANES, LRU_W), F32), pltpu.VMEM((tl + SUBLANES, SSD_CONV_DIM), F32)],
        compiler_params=_cparams(("parallel", "arbitrary")),
        name="projc",
    )(x3, mod.table, mod.table, *ws, lp["cw"], lp["cb"], sp["cw"], sp["cb"])


def _proj_call(x3, mod, ws, tl):
    nb, L, _ = x3.shape
    widths = [w.shape[1] for w in ws]
    row = lambda b, l: (b, l, 0)
    full = lambda b, l: (0, 0)
    bof, tof = (lambda b, l: b), (lambda b, l: l)
    return pl.pallas_call(
        functools.partial(_proj_kernel, per_batch=mod.per_batch),
        out_shape=[jax.ShapeDtypeStruct((nb, L, n), F32) for n in widths],
        grid=(nb, L // tl),
        in_specs=[pl.BlockSpec((None, tl, D), row), mod.spec(0, tl, bof, tof), mod.spec(1, tl, bof, tof)]
                 + [pl.BlockSpec((D, n), full) for n in widths],
        out_specs=[pl.BlockSpec((None, tl, n), row) for n in widths],
        compiler_params=_cparams(("parallel", "arbitrary")),
        name="proj",
    )(x3, mod.table, mod.table, *ws)


def _scan_segments(a_sk, u_sk, h0, seg):
    stride = seg + 1
    nt = D // LANES
    rows = lambda g: pl.ds(g, SUBLANES, stride=stride)
    shape = (nt, SUBLANES, LANES)

    def local(g, carry):
        h, p = carry
        a = a_sk[:, rows(g), :]
        h = a * h + u_sk[:, rows(g), :]
        u_sk[:, rows(g), :] = h
        return h, a * p

    h_fin, p_fin = lax.fori_loop(0, seg, local, (jnp.zeros(shape, F32), jnp.ones(shape, F32)), unroll=2)
    starts = [jnp.stack([h0[:, c * LANES:(c + 1) * LANES] for c in range(nt)], axis=0)]
    for s in range(SUBLANES):
        starts.append(p_fin[:, s:s + 1, :] * starts[-1] + h_fin[:, s:s + 1, :])
    start = jnp.concatenate(starts[:SUBLANES], axis=1)

    def fixup(g, p):
        p = a_sk[:, rows(g), :] * p
        u_sk[:, rows(g), :] = u_sk[:, rows(g), :] + p * start
        return p

    lax.fori_loop(0, seg, fixup, jnp.ones(shape, F32), unroll=2)
    return jnp.concatenate([starts[SUBLANES][c] for c in range(nt)], axis=1)


def _lru_kernel(x_ref, g_ref, wa_ref, wx_ref, ba_ref, bx_ref, lam_ref, ng_ref,
                y_ref, h_ref, a_sk, u_sk, hc_scr, *, tl):
    @pl.when(pl.program_id(1) == 0)
    def _():
        hc_scr[...] = jnp.zeros((1, D), F32)

    seg = tl // SUBLANES
    sp = _softplus(-lam_ref[...])
    for j in range(LRU_W // LRU_PACK_W):
        cs = slice(j * LRU_PACK_W, (j + 1) * LRU_PACK_W)
        xc = x_ref[:, cs]
        a, u = _lru_gates(xc, xc.astype(BF16), wa_ref[j], wx_ref[j], ba_ref[:, cs], bx_ref[:, cs], sp[:, cs])
        for s in range(SUBLANES):
            for t in range(LRU_PACK_W // LANES):
                c = j * (LRU_PACK_W // LANES) + t
                a_sk[c, s * (seg + 1):s * (seg + 1) + seg, :] = a[s * seg:(s + 1) * seg, t * LANES:(t + 1) * LANES]
                u_sk[c, s * (seg + 1):s * (seg + 1) + seg, :] = u[s * seg:(s + 1) * seg, t * LANES:(t + 1) * LANES]

    h_last = _scan_segments(a_sk, u_sk, hc_scr[...], seg)
    hc_scr[...] = h_last
    h_ref[...] = h_last
    for s in range(SUBLANES):
        h = jnp.concatenate([u_sk[c, s * (seg + 1):s * (seg + 1) + seg, :] for c in range(D // LANES)], axis=1)
        y = h * _gelu_tanh(g_ref[s * seg:(s + 1) * seg, :])
        y_ref[s * seg:(s + 1) * seg, :] = _rms_norm(y, ng_ref[...]).astype(BF16)


def _lru_call(xg, p, tl):
    nb, L, _ = xg.shape
    vec = lambda b, l: (0, 0)
    blk = lambda b, l: (0, 0, 0)
    return pl.pallas_call(
        functools.partial(_lru_kernel, tl=tl),
        out_shape=[jax.ShapeDtypeStruct((nb, L, LRU_W), BF16),
                   jax.ShapeDtypeStruct((nb, 1, LRU_W), F32)],
        grid=(nb, L // tl),
        in_specs=[pl.BlockSpec((None, tl, LRU_W), lambda b, l: (b, l, 0)),
                  pl.BlockSpec((None, tl, LRU_W), lambda b, l: (b, l, 1)),
                  pl.BlockSpec((LRU_W // LRU_PACK_W, LRU_PACK_W, LRU_PACK_W), blk),
                  pl.BlockSpec((LRU_W // LRU_PACK_W, LRU_PACK_W, LRU_PACK_W), blk),
                  pl.BlockSpec((1, LRU_W), vec), pl.BlockSpec((1, LRU_W), vec),
                  pl.BlockSpec((1, LRU_W), vec), pl.BlockSpec((1, LRU_W), vec)],
        out_specs=[pl.BlockSpec((None, tl, LRU_W), lambda b, l: (b, l, 0)),
                   pl.BlockSpec((None, 1, LRU_W), lambda b, l: (b, 0, 0))],
        scratch_shapes=[pltpu.VMEM((LRU_W // LANES, tl + SUBLANES, LANES), F32),
                        pltpu.VMEM((LRU_W // LANES, tl + SUBLANES, LANES), F32),
                        pltpu.VMEM((1, LRU_W), F32)],
        compiler_params=_cparams(("parallel", "arbitrary")),
        name="lru",
    )(xg, xg, p["wa"], p["wx"], p["ba"], p["bx"], p["lam"], p["ng"])


def _ssd_kernel(xc_ref, z_blk, dt_blk, dtb_ref, alog_ref, dexp_ref, ng_ref,
                y_blk, st_ref, st_scr, *tmp, chunks):
    @pl.when(pl.program_id(1) == 0)
    def _():
        st_scr[...] = jnp.zeros((SSD_N, SSD_INNER), F32)

    sets = (tmp[0:3], tmp[3:6])
    for c in range(chunks):
        rows = pl.ds(c * SSD_Q, SSD_Q)
        _ssd_chunk(xc_ref.at[rows, :], z_blk.at[rows, :], dt_blk.at[rows, :], dtb_ref, alog_ref, dexp_ref, ng_ref,
                   y_blk.at[rows, :], st_scr, *sets[c % 2])

    @pl.when(pl.program_id(1) == pl.num_programs(1) - 1)
    def _():
        st_ref[...] = st_scr[...].T


def _ssd_chunk(xc_scr, z_ref, dt_ref, dtb_ref, alog_ref, dexp_ref, ng_ref, y_ref, st_scr, y_scr, ea_scr, ew_scr):
    q = SSD_Q
    dt = _softplus(dt_ref[...] + dtb_ref[...])
    da = dt * (-jnp.exp(alog_ref[...]))
    ri = lax.broadcasted_iota(I32, (q, q), 0)
    ci = lax.broadcasted_iota(I32, (q, q), 1)
    causal = ri >= ci
    acs = _dot(causal.astype(F32), da, precision=HIGHEST)
    acs_t = acs.T
    dt_t = dt.T
    w_end = dt * jnp.exp(acs[q - 1:q, :] - acs)

    half = SSD_INNER // SSD_GROUPS
    hpg = SSD_HEADS // SSD_GROUPS
    cgb, cbs = [], []
    for g in range(SSD_GROUPS):
        bgb = xc_scr[:, SSD_INNER + g * SSD_N:SSD_INNER + (g + 1) * SSD_N].astype(BF16)
        cgb.append(xc_scr[:, SSD_INNER + (SSD_GROUPS + g) * SSD_N:SSD_INNER + (SSD_GROUPS + g + 1) * SSD_N].astype(BF16))
        cbs.append(_dot_nt(cgb[g], bgb))

    low = lax.broadcasted_iota(I32, (q, LANES), 1) < SSD_P
    for k in range(SSD_HEADS // 2):
        cs = slice(k * LANES, (k + 1) * LANES)
        cb = cbs[(2 * k) // hpg]
        m, colb, wb = [], [], []
        for h in (2 * k, 2 * k + 1):
            colb.append(jnp.broadcast_to(acs[:, h:h + 1], (q, LANES)))
            wb.append(jnp.broadcast_to(w_end[:, h:h + 1], (q, LANES)))
            seg = colb[-1] - acs_t[h:h + 1, :]
            decay = jnp.where(causal, jnp.exp(jnp.where(causal, seg, 0.0)), 0.0)
            m.append((cb * (decay * dt_t[h:h + 1, :])).astype(BF16))
        ea_scr[:, cs] = jnp.where(low, colb[0], colb[1])
        ew_scr[:, cs] = jnp.where(low, wb[0], wb[1])
        xk = xc_scr[:, cs]
        zero = jnp.zeros_like(xk)
        rhs = jnp.concatenate([jnp.where(low, xk, zero), jnp.where(low, zero, xk)], axis=0).astype(BF16)
        y_scr[:, cs] = _dot(jnp.concatenate(m, axis=1), rhs)

    x = xc_scr[:, 0:SSD_INNER]
    ea = ea_scr[...]
    w = (x * ew_scr[...]).astype(BF16)
    s_in = st_scr[...]
    sb = s_in.astype(BF16)
    y_off, c_state = [], []
    for g in range(SSD_GROUPS):
        gs = slice(g * half, (g + 1) * half)
        bgt = xc_scr[:, SSD_INNER + g * SSD_N:SSD_INNER + (g + 1) * SSD_N].T.astype(BF16)
        y_off.append(_dot(cgb[g], sb[:, gs]))
        c_state.append(_dot(bgt, w[:, gs]))
    st_scr[...] = jnp.exp(ea[q - 1:q, :]) * s_in + jnp.concatenate(c_state, axis=1)
    y = y_scr[...] + jnp.concatenate(y_off, axis=1) * jnp.exp(ea) + dexp_ref[...] * x

    yz = y * _silu(z_ref[...])
    y_ref[...] = _rms_norm(yz, ng_ref[...]).astype(BF16)


def _ssd_call(xbc, z, dtr, p, chunks):
    nb, L, _ = xbc.shape
    q = SSD_Q
    tl = q * chunks
    vec = lambda b, c: (0, 0)
    row = lambda b, c: (b, c, 0)
    return pl.pallas_call(
        functools.partial(_ssd_kernel, chunks=chunks),
        out_shape=[jax.ShapeDtypeStruct((nb, L, SSD_INNER), BF16),
                   jax.ShapeDtypeStruct((nb, SSD_INNER, SSD_N), F32)],
        grid=(nb, L // tl),
        in_specs=[pl.BlockSpec((None, tl, SSD_CONV_DIM), row), pl.BlockSpec((None, tl, SSD_INNER), row),
                  pl.BlockSpec((None, tl, LANES), row),
                  pl.BlockSpec((1, LANES), vec), pl.BlockSpec((1, LANES), vec),
                  pl.BlockSpec((1, SSD_INNER), vec), pl.BlockSpec((1, SSD_INNER), vec)],
        out_specs=[pl.BlockSpec((None, tl, SSD_INNER), row),
                   pl.BlockSpec((None, SSD_INNER, SSD_N), lambda b, c: (b, 0, 0))],
        scratch_shapes=[pltpu.VMEM((SSD_N, SSD_INNER), F32)] + [pltpu.VMEM((q, SSD_INNER), F32)] * 6,
        compiler_params=_cparams(("parallel", "arbitrary")),
        name="ssd",
    )(xbc, z, dtr, p["dtb"], p["alog"], p["dexp"], p["ng"])


def _srow_kernel(xg_ref, xbc_ref, dte_ref, lconv_ref, h0_ref, sconv_ref,
                 lcw_ref, lcb_ref, wa_ref, wx_ref, ba_ref, bx_ref, lam_ref, lng_ref,
                 scw_ref, scb_ref, dtbe_ref, aloge_ref,
                 yl_ref, h_ref, xs_ref, bm_ref, cm_ref, xdtt_ref, dect_ref, lcn_ref, scn_ref, xc_scr):
    xl = xg_ref[:, 0:LRU_W]
    acc = lcb_ref[...] + xl * lcw_ref[CONV_W - 1:CONV_W, :]
    for k in range(CONV_W - 1):
        acc = acc + lconv_ref[k] * lcw_ref[k:k + 1, :]
    xc_scr[...] = acc
    for k in range(CONV_W - 2):
        lcn_ref[k] = lconv_ref[k + 1]
        scn_ref[k] = sconv_ref[k + 1]
    lcn_ref[CONV_W - 2] = xl
    scn_ref[CONV_W - 2] = xbc_ref[...]
    sp = _softplus(-lam_ref[...])
    for j in range(LRU_W // LRU_PACK_W):
        cs = slice(j * LRU_PACK_W, (j + 1) * LRU_PACK_W)
        xc = xc_scr[:, cs]
        a, u = _lru_gates(xc, xc.astype(BF16), wa_ref[j], wx_ref[j], ba_ref[:, cs], bx_ref[:, cs], sp[:, cs])
        h_ref[:, cs] = a * h0_ref[:, cs] + u
    y = h_ref[...] * _gelu_tanh(xg_ref[:, LRU_W:2 * LRU_W])
    yl_ref[...] = _rms_norm(y, lng_ref[...]).astype(BF16)

    acc = scb_ref[...] + xbc_ref[...] * scw_ref[CONV_W - 1:CONV_W, :]
    for k in range(CONV_W - 1):
        acc = acc + sconv_ref[k] * scw_ref[k:k + 1, :]
    xc = _silu(acc)
    xs = xc[:, 0:SSD_INNER]
    xs_ref[...] = xs
    bm_ref[...] = xc[:, SSD_INNER:SSD_INNER + SSD_GROUPS * SSD_N]
    cm_ref[...] = xc[:, SSD_INNER + SSD_GROUPS * SSD_N:]
    dt = _softplus(dte_ref[...] + dtbe_ref[...])
    dec = jnp.exp(dt * (-jnp.exp(aloge_ref[...])))
    xdtt = (xs * dt).T
    dect = dec.T
    for j in range(xdtt_ref.shape[0]):
        xdtt_ref[j] = xdtt[:, j * SUBLANES:(j + 1) * SUBLANES]
        dect_ref[j] = dect[:, j * SUBLANES:(j + 1) * SUBLANES]


def _srow_call(xg, xbc, dte, lconv, h0, sconv, lp, sp):
    n = xg.shape[0]
    args = (xg, xbc, dte, lconv, h0, sconv, lp["cw"], lp["cb"], lp["wa"], lp["wx"], lp["ba"], lp["bx"],
            lp["lam"], lp["ng"], sp["cw"], sp["cb"], sp["dtbe"], sp["aloge"])
    full = lambda a: pl.BlockSpec(a.shape, lambda i, nd=a.ndim: (0,) * nd)
    outs = [jax.ShapeDtypeStruct((n, LRU_W), BF16), jax.ShapeDtypeStruct((n, LRU_W), F32),
            jax.ShapeDtypeStruct((n, SSD_INNER), F32), jax.ShapeDtypeStruct((n, SSD_GROUPS * SSD_N), F32),
            jax.ShapeDtypeStruct((n, SSD_GROUPS * SSD_N), F32),
            jax.ShapeDtypeStruct((n // SUBLANES, SSD_INNER, SUBLANES), F32),
            jax.ShapeDtypeStruct((n // SUBLANES, SSD_INNER, SUBLANES), F32),
            jax.ShapeDtypeStruct((CONV_W - 1, n, LRU_W), F32), jax.ShapeDtypeStruct((CONV_W - 1, n, SSD_CONV_DIM), F32)]
    return pl.pallas_call(
        _srow_kernel,
        out_shape=outs,
        grid=(1,),
        in_specs=[full(a) for a in args],
        out_specs=[pl.BlockSpec(o.shape, lambda i, nd=len(o.shape): (0,) * nd) for o in outs],
        scratch_shapes=[pltpu.VMEM((n, LRU_W), F32)],
        compiler_params=_cparams(("arbitrary",)),
        name="srow",
    )(*args)


def _sstate_kernel(s0_ref, xq_ref, dq_ref, bm_ref, cm_ref, xs_ref, z_ref, dexp_ref, ng_ref,
                   sn_ref, ys_ref, yraw):
    nb = s0_ref.shape[0]
    half = SSD_INNER // SSD_GROUPS
    for bi in range(nb):
        brow = jnp.concatenate(
            [jnp.broadcast_to(bm_ref[bi:bi + 1, g * SSD_N:(g + 1) * SSD_N], (half, SSD_N)) for g in range(SSD_GROUPS)],
            axis=0)
        s = dq_ref[:, bi:bi + 1] * s0_ref[bi] + xq_ref[:, bi:bi + 1] * brow
        sn_ref[bi] = s
        sb = s.astype(BF16)
        for g in range(SSD_GROUPS):
            cg = cm_ref[:, g * SSD_N:(g + 1) * SSD_N].astype(BF16)
            res = _dot_nt(cg, sb[g * half:(g + 1) * half, :])
            yraw[bi:bi + 1, g * half:(g + 1) * half] = res[bi:bi + 1, :]
    y = yraw[...] + dexp_ref[...] * xs_ref[...]
    ys_ref[...] = _rms_norm(y * _silu(z_ref[...]), ng_ref[...]).astype(BF16)


def _sstate_call(s0, xq, dq, bm, cm, xs, z, dexp, ng, nb):
    n = s0.shape[0]
    row = lambda i: (i, 0)
    vec = lambda i: (0, 0)
    gn = SSD_GROUPS * SSD_N
    return pl.pallas_call(
        _sstate_kernel,
        out_shape=[jax.ShapeDtypeStruct(s0.shape, F32), jax.ShapeDtypeStruct((n, SSD_INNER), BF16)],
        grid=(n // nb,),
        in_specs=[pl.BlockSpec((nb, SSD_INNER, SSD_N), lambda i: (i, 0, 0)),
                  pl.BlockSpec((None, SSD_INNER, nb), lambda i: (i, 0, 0)),
                  pl.BlockSpec((None, SSD_INNER, nb), lambda i: (i, 0, 0)),
                  pl.BlockSpec((nb, gn), row), pl.BlockSpec((nb, gn), row),
                  pl.BlockSpec((nb, SSD_INNER), row), pl.BlockSpec((nb, SSD_INNER), row),
                  pl.BlockSpec((1, SSD_INNER), vec), pl.BlockSpec((1, SSD_INNER), vec)],
        out_specs=[pl.BlockSpec((nb, SSD_INNER, SSD_N), lambda i: (i, 0, 0)), pl.BlockSpec((nb, SSD_INNER), row)],
        scratch_shapes=[pltpu.VMEM((nb, SSD_INNER), F32)],
        compiler_params=_cparams(("parallel",)),
        name="sstate",
    )(s0, xq, dq, bm, cm, xs, z, dexp, ng)


def _post_kernel(yl_ref, ys_ref, x_ref, g1_ref, sh2_ref, sc2_ref, wo_ref, l1g_ref, l1b_ref, wrt_ref, brc_ref,
                 x1_ref, v_ref, eid_ref, gw_ref, *, per_batch):
    b = pl.program_id(0)
    o = _dot(yl_ref[...], wo_ref[0:LRU_W, :]) + _dot(ys_ref[...], wo_ref[LRU_W:LRU_W + SSD_INNER, :])
    x1 = _layer_norm(ALPHA * x_ref[...] + _mod_rows(g1_ref, per_batch, b) * o) * l1g_ref[...] + l1b_ref[...]
    x1_ref[...] = x1
    v = _layer_norm(x1) * (1.0 + _mod_rows(sc2_ref, per_batch, b)) + _mod_rows(sh2_ref, per_batch, b)
    _store_row_tiles(v_ref, v)

    lt = _dot(v.astype(BF16), wrt_ref[...]).T + brc_ref[...]
    tl = lt.shape[1]
    row = lax.broadcasted_iota(I32, (GROUP_SIZE, tl), 0).astype(F32)
    big = float(GROUP_SIZE)
    neg = -jnp.inf
    lg = jnp.where(row < N_GROUPS, lt[0:GROUP_SIZE, :], neg)
    gmax = jnp.max(lg, axis=0, keepdims=True)
    gsel = jnp.min(jnp.where(lg == gmax, row, big), axis=0, keepdims=True)
    pg = 1.0 / jnp.sum(jnp.exp(lg - gmax), axis=0, keepdims=True)
    le = lt[GROUP_SIZE:2 * GROUP_SIZE, :]
    for j in range(1, N_GROUPS):
        le = jnp.where(gsel == j, lt[GROUP_SIZE * (j + 1):GROUP_SIZE * (j + 2), :], le)
    m1 = jnp.max(le, axis=0, keepdims=True)
    i1 = jnp.min(jnp.where(le == m1, row, big), axis=0, keepdims=True)
    rest = jnp.where(row == i1, neg, le)
    m2 = jnp.max(rest, axis=0, keepdims=True)
    i2 = jnp.min(jnp.where(rest == m2, row, big), axis=0, keepdims=True)
    e2 = jnp.exp(m2 - m1)
    den = 1.0 + e2
    eid = jnp.where(row == 0, gsel * GROUP_SIZE + i1, jnp.where(row == 1, gsel * GROUP_SIZE + i2, 0.0))
    eid_ref[...] = eid.astype(I32)
    gw_ref[...] = jnp.where(row == 0, pg * (1.0 / den), jnp.where(row == 1, pg * (e2 / den), 0.0))


def _post_call(yl, ys, x3, mod, p, tl):
    nb, L, _ = x3.shape
    nl = L // tl
    row = lambda b, l: (b, l, 0)
    vec = lambda b, l: (0, 0)
    tok = lambda b, l: (b * nl + l, 0)
    tokt = lambda b, l: (0, b * nl + l)
    bof, tof = (lambda b, l: b), (lambda b, l: l)
    return pl.pallas_call(
        functools.partial(_post_kernel, per_batch=mod.per_batch),
        out_shape=[jax.ShapeDtypeStruct((nb, L, D), F32), jax.ShapeDtypeStruct(_tile_rows(nb * L), F32),
                   jax.ShapeDtypeStruct((SUBLANES, nb * L), I32), jax.ShapeDtypeStruct((SUBLANES, nb * L), F32)],
        grid=(nb, nl),
        in_specs=[pl.BlockSpec((None, tl, LRU_W), row), pl.BlockSpec((None, tl, SSD_INNER), row),
                  pl.BlockSpec((None, tl, D), row),
                  mod.spec(2, tl, bof, tof), mod.spec(3, tl, bof, tof), mod.spec(4, tl, bof, tof),
                  pl.BlockSpec((LRU_W + SSD_INNER, D), vec), pl.BlockSpec((1, D), vec), pl.BlockSpec((1, D), vec),
                  pl.BlockSpec((D, LANES), vec), pl.BlockSpec((LANES, 1), vec)],
        out_specs=[pl.BlockSpec((None, tl, D), row), pl.BlockSpec(_tile_rows(tl), tok),
                   pl.BlockSpec((SUBLANES, tl), tokt), pl.BlockSpec((SUBLANES, tl), tokt)],
        compiler_params=_cparams(("parallel", "arbitrary")),
        name="post",
    )(yl, ys, x3, mod.table, mod.table, mod.table, p["wo"], p["l1g"], p["l1b"], p["wrt"], p["brc"])


def _route_kernel(eidp_ref, eids_ref, dest_ref, cnt_ref, run, poff, *, p_tiles):
    ph = pl.program_id(0)
    t = pl.program_id(1)
    n = ROUTE_TILE
    eid = jnp.where(t < p_tiles, eidp_ref[...], eids_ref[...])
    rowi = lax.broadcasted_iota(I32, (LANES, n), 0)
    oh0 = rowi == eid[0:1, :]
    oh1 = rowi == eid[1:2, :]
    oh = oh0.astype(F32) + oh1.astype(F32)
    tile_cnt = jnp.sum(oh, axis=1, keepdims=True)

    @pl.when((ph == 0) & (t == 0))
    def _():
        run[...] = jnp.zeros((LANES, n), F32)

    @pl.when(ph == 0)
    def _():
        run[...] = run[...] + tile_cnt

    @pl.when((ph == 1) & (t == 0))
    def _():
        counts = run[...]
        cnt_ref[...] = counts
        ci = counts.astype(I32)
        q = jnp.floor(counts * (1.0 / MOE_BM)).astype(I32)
        rem = ci - q * MOE_BM
        q = q + jnp.where(rem >= MOE_BM, 1, 0) - jnp.where(rem < 0, 1, 0)
        nblk = q + jnp.where(ci - q * MOE_BM > 0, 1, 0)
        r = lax.broadcasted_iota(I32, (LANES, LANES), 0)
        c = lax.broadcasted_iota(I32, (LANES, LANES), 1)
        lower = (r > c).astype(BF16)
        poff[...] = _dot(lower, nblk.astype(F32).astype(BF16)) * float(MOE_BM)
        run[...] = jnp.zeros((LANES, n), F32)

    @pl.when(ph == 1)
    def _():
        r = lax.broadcasted_iota(I32, (n, n), 0)
        c = lax.broadcasted_iota(I32, (n, n), 1)
        before = (r < c).astype(BF16)
        slot = _dot(oh.astype(BF16), before) + run[...] + poff[...]
        d0 = jnp.sum(jnp.where(oh0, slot, 0.0), axis=0, keepdims=True)
        d1 = jnp.sum(jnp.where(oh1, slot, 0.0), axis=0, keepdims=True)
        row = lax.broadcasted_iota(I32, (SUBLANES, n), 0)
        dest_ref[...] = jnp.where(row == 0, d0, jnp.where(row == 1, d1, 0.0)).astype(I32)
        run[...] = run[...] + tile_cnt


def _route_call(eid_p, eid_s):
    n = ROUTE_TILE
    p_tiles = eid_p.shape[1] // n
    s_tiles = eid_s.shape[1] // n
    t_all = (p_tiles + s_tiles) * n
    return pl.pallas_call(
        functools.partial(_route_kernel, p_tiles=p_tiles),
        out_shape=[jax.ShapeDtypeStruct((SUBLANES, t_all), I32), jax.ShapeDtypeStruct((LANES, n), F32)],
        grid=(2, p_tiles + s_tiles),
        in_specs=[pl.BlockSpec((SUBLANES, n), lambda ph, t: (0, jnp.minimum(t, p_tiles - 1))),
                  pl.BlockSpec((SUBLANES, n), lambda ph, t: (0, jnp.maximum(t - p_tiles, 0)))],
        out_specs=[pl.BlockSpec((SUBLANES, n), lambda ph, t: (0, t * ph)),
                   pl.BlockSpec((LANES, n), lambda ph, t: (0, 0))],
        scratch_shapes=[pltpu.VMEM((LANES, n), F32), pltpu.VMEM((LANES, n), F32)],
        compiler_params=_cparams(("arbitrary", "arbitrary")),
        name="route",
    )(eid_p, eid_s)


def _row_copy(src, src_row, dst, dst_row, sem):
    s0 = pl.multiple_of(src_row * SUBLANES, SUBLANES)
    d0 = pl.multiple_of(dst_row * SUBLANES, SUBLANES)
    return pltpu.make_async_copy(src.at[pl.ds(s0, SUBLANES), :], dst.at[pl.ds(d0, SUBLANES), :], sem)


def _scatter_rows(dest_ref, v_ref, xpad_ref, sem, tl):
    def start(r, carry):
        for k in range(2):
            _row_copy(v_ref, r, xpad_ref, dest_ref[k, r], sem).start(priority=k)
        return carry

    def wait(r, carry):
        for k in range(2):
            _row_copy(v_ref, r, xpad_ref, dest_ref[k, r], sem).wait()
        return carry

    lax.fori_loop(0, tl, start, 0, unroll=DMA_UNROLL)
    lax.fori_loop(0, tl, wait, 0, unroll=DMA_UNROLL)


def _dispatch_kernel(zflag_ref, destp_ref, dests_ref, vp_ref, vs_ref, xpad_ref, zbuf, sem, zsem,
                     *, tl_p, tl_s, p_tiles, n_blocks):
    blk_rows = _tile_rows(MOE_BM)[0]

    @pl.when(pl.program_id(0) == 0)
    def _():
        zbuf[...] = jnp.zeros(zbuf.shape, F32)

        def zero_block(go):
            def body(b, carry):
                @pl.when(zflag_ref[b] != 0)
                def _():
                    r0 = pl.multiple_of(b * blk_rows, blk_rows)
                    cp = pltpu.make_async_copy(zbuf, xpad_ref.at[pl.ds(r0, blk_rows), :], zsem)
                    if go:
                        cp.start()
                    else:
                        cp.wait()
                return carry
            lax.fori_loop(0, n_blocks, body, 0)

        zero_block(True)
        zero_block(False)

    @pl.when(pl.program_id(0) < p_tiles)
    def _():
        _scatter_rows(destp_ref, vp_ref, xpad_ref, sem, tl_p)

    @pl.when(pl.program_id(0) >= p_tiles)
    def _():
        _scatter_rows(dests_ref, vs_ref, xpad_ref, sem, tl_s)


def _dispatch_call(zflag, dest, v_p, v_s, n_blocks, tl_p, tl_s):
    p_tiles = v_p.shape[0] // _tile_rows(tl_p)[0]
    s_tiles = v_s.shape[0] // _tile_rows(tl_s)[0]
    s_off = p_tiles * tl_p // tl_s
    return pl.pallas_call(
        functools.partial(_dispatch_kernel, tl_p=tl_p, tl_s=tl_s, p_tiles=p_tiles, n_blocks=n_blocks),
        out_shape=jax.ShapeDtypeStruct(_tile_rows(n_blocks * MOE_BM), F32),
        grid_spec=pltpu.PrefetchScalarGridSpec(
            num_scalar_prefetch=1,
            grid=(p_tiles + s_tiles,),
            in_specs=[pl.BlockSpec((SUBLANES, tl_p), lambda t, z: (0, jnp.minimum(t, p_tiles - 1)),
                                   memory_space=pltpu.SMEM),
                      pl.BlockSpec((SUBLANES, tl_s), lambda t, z: (0, s_off + jnp.maximum(t - p_tiles, 0)),
                                   memory_space=pltpu.SMEM),
                      pl.BlockSpec(_tile_rows(tl_p), lambda t, z: (jnp.minimum(t, p_tiles - 1), 0)),
                      pl.BlockSpec(_tile_rows(tl_s), lambda t, z: (jnp.maximum(t - p_tiles, 0), 0))],
            out_specs=pl.BlockSpec(memory_space=pl.ANY),
            scratch_shapes=[pltpu.VMEM(_tile_rows(MOE_BM), F32), pltpu.SemaphoreType.DMA, pltpu.SemaphoreType.DMA]),
        compiler_params=_cparams(("arbitrary",)),
        name="dispatch",
    )(zflag, dest, dest, v_p, v_s)


def _expert_kernel(be_ref, nb_ref, nxt_ref, slot_ref, x_ref, wg_hbm, wu_hbm, wd_hbm, o_ref,
                   wgf, wuf, wdf, wgb, wub, wdb, sems):
    i = pl.program_id(0)

    def weights(e, slot, go):
        for m, (hbm, buf) in enumerate(((wg_hbm, wgf), (wu_hbm, wuf), (wd_hbm, wdf))):
            cp = pltpu.make_async_copy(hbm.at[e], buf.at[slot], sems.at[slot, m])
            if go:
                cp.start()
            else:
                cp.wait()

    @pl.when(i < nb_ref[0])
    def _():
        e = be_ref[i]
        slot = slot_ref[e]

        @pl.when(i == 0)
        def _():
            weights(e, slot, True)

        @pl.when((i == 0) | (e != be_ref[jnp.maximum(i - 1, 0)]))
        def _():
            weights(e, slot, False)
            nxt = nxt_ref[e]

            @pl.when(nxt >= 0)
            def _():
                weights(nxt, 1 - slot, True)

            wgb[...] = wgf[slot].astype(BF16)
            wub[...] = wuf[slot].astype(BF16)
            wdb[...] = wdf[slot].astype(BF16)

        x = _load_row_tiles(x_ref).astype(BF16)
        h = _silu(_dot(x, wgb[...])) * _dot(x, wub[...])
        _store_row_tiles(o_ref, _dot(h.astype(BF16), wdb[...]))

    @pl.when(i >= nb_ref[0])
    def _():
        o_ref[...] = jnp.zeros(o_ref.shape, F32)


def _expert_call(blk_e, nblk, nxt_e, slot_e, xpad, w_gate, w_up, w_down):
    n_rows = xpad.shape[0] // SUBLANES
    blk = lambda i, be, nb, nx, sl: (jnp.minimum(i, nb[0] - 1), 0)
    hbm = pl.BlockSpec(memory_space=pl.ANY)
    return pl.pallas_call(
        _expert_kernel,
        out_shape=jax.ShapeDtypeStruct(_tile_rows(n_rows), F32),
        grid_spec=pltpu.PrefetchScalarGridSpec(
            num_scalar_prefetch=4,
            grid=(n_rows // MOE_BM,),
            in_specs=[pl.BlockSpec(_tile_rows(MOE_BM), blk), hbm, hbm, hbm],
            out_specs=pl.BlockSpec(_tile_rows(MOE_BM), lambda i, be, nb, nx, sl: (i, 0)),
            scratch_shapes=[pltpu.VMEM((2, D, D_FF), F32), pltpu.VMEM((2, D, D_FF), F32), pltpu.VMEM((2, D_FF, D), F32),
                            pltpu.VMEM((D, D_FF), BF16), pltpu.VMEM((D, D_FF), BF16), pltpu.VMEM((D_FF, D), BF16),
                            pltpu.SemaphoreType.DMA((2, 3))]),
        compiler_params=_cparams(("arbitrary",)),
        name="expert",
    )(blk_e, nblk, nxt_e, slot_e, xpad, w_gate, w_up, w_down)


def _combine_kernel(dest_ref, gw0_ref, gw1_ref, x1_ref, g2_ref, l2g_ref, l2b_ref, ypad_ref, y_ref, ybuf, sems,
                    *, tok_off, k_stride, tl, nl, per_batch):
    i = pl.program_id(0)

    def gather(tile, slot, go):
        base = tok_off + tile * tl

        def body(r, carry):
            for k in range(2):
                cp = _row_copy(ypad_ref, dest_ref[k * k_stride + base + r], ybuf.at[slot, k], r, sems.at[slot])
                if go:
                    cp.start(priority=k)
                else:
                    cp.wait()
            return carry

        lax.fori_loop(0, tl, body, 0, unroll=DMA_UNROLL)

    @pl.when(i == 0)
    def _():
        gather(0, 0, True)

    @pl.when(i + 1 < pl.num_programs(0))
    def _():
        gather(i + 1, (i + 1) % 2, True)

    slot = i % 2
    gather(i, slot, False)
    f = _load_row_tiles(ybuf.at[slot, 0]) * gw0_ref[...] + _load_row_tiles(ybuf.at[slot, 1]) * gw1_ref[...]
    g2 = _mod_rows(g2_ref, per_batch, i // nl)
    y_ref[...] = _layer_norm(ALPHA * x1_ref[...] + g2 * f) * l2g_ref[...] + l2b_ref[...]


def _combine_call(dest, gw0, gw1, x1, mod, l2g, l2b, ypad, tok_off, tl):
    nb, L, _ = x1.shape
    nl = L // tl
    row = lambda i, d: (i // nl, i % nl, 0)
    vec = lambda i, d: (0, 0)
    tok = lambda i, d: (i, 0)
    g2_spec = mod.spec(5, tl, lambda i, d: i // nl, lambda i, d: i % nl)
    return pl.pallas_call(
        functools.partial(_combine_kernel, tok_off=tok_off, k_stride=dest.shape[1], tl=tl, nl=nl,
                          per_batch=mod.per_batch),
        out_shape=jax.ShapeDtypeStruct((nb, L, D), F32),
        grid_spec=pltpu.PrefetchScalarGridSpec(
            num_scalar_prefetch=1,
            grid=(nb * nl,),
            in_specs=[pl.BlockSpec((tl, 1), tok), pl.BlockSpec((tl, 1), tok),
                      pl.BlockSpec((None, tl, D), row), g2_spec,
                      pl.BlockSpec((1, D), vec), pl.BlockSpec((1, D), vec),
                      pl.BlockSpec(memory_space=pl.ANY)],
            out_specs=pl.BlockSpec((None, tl, D), row),
            scratch_shapes=[pltpu.VMEM((2, 2) + _tile_rows(tl), F32), pltpu.SemaphoreType.DMA((2,))]),
        compiler_params=_cparams(("arbitrary",)),
        name="combine",
    )(dest[:2].reshape(-1), gw0, gw1, x1, mod.table, l2g, l2b, ypad)


def _block_diag(w):
    nh, blk, _ = w.shape
    w4 = w.reshape(nh // LRU_PACK, LRU_PACK, blk, blk)
    eye = jnp.eye(LRU_PACK, dtype=w.dtype)
    out = jnp.einsum("gaij,ab->gaibj", w4, eye)
    return out.reshape(nh // LRU_PACK, LRU_PACK * blk, LRU_PACK * blk).astype(BF16)


def _row(v):
    return v.reshape(1, -1).astype(F32)


def _pad_lanes(v):
    return jnp.pad(v.reshape(1, -1).astype(F32), ((0, 0), (0, LANES - v.shape[-1])))


def kernel(x_prompt, x_sample, c_prompt, c_sample, state_lru_conv, state_lru_h, state_ssd_conv, state_ssd, w_ada, b_ada, w_in, lru_conv_w, lru_conv_b, lru_wa, lru_ba, lru_wx, lru_bx, lru_lambda, lru_norm_g, ssd_conv_w, ssd_conv_b, ssd_dt_bias, ssd_a_log, ssd_d, ssd_norm_g, w_out, ln1_g, ln1_b, w_rg, b_rg, w_re, b_re, w_gate, w_up, w_down, ln2_g, ln2_b):
    assert w_ada.shape[0] == DEPTH == 1
    nbp, seq, _ = x_prompt.shape
    nbs = x_sample.shape[0]
    t_p = nbp * seq
    t_all = t_p + nbs
    assert x_sample.shape[1] == 1 and t_p % ROUTE_TILE == 0 and nbs % TOK_TILE == 0 and nbp == SUBLANES

    c_rows = -(-(nbp + nbs) // 16) * 16
    c_all = jnp.pad(jnp.concatenate([c_sample, c_prompt], axis=0), ((0, c_rows - nbp - nbs), (0, 0)))
    w_in0 = w_in[0]
    o_z, o_xbc, o_dt = 2 * LRU_W, 2 * LRU_W + SSD_INNER, 2 * LRU_W + SSD_INNER + SSD_CONV_DIM
    w_in_t = jnp.swapaxes(w_in0, 0, 1)
    *ws_p, w_dt_t = _wcast_call(w_in_t, ((0, o_z), (o_z, o_xbc), (o_xbc, o_dt)), 512)
    w_dt = jnp.swapaxes(w_dt_t, 0, 1)
    ws_s = ws_p + [jnp.repeat(w_dt, SSD_P, axis=1)]
    ws_p = ws_p + [jnp.pad(w_dt, ((0, 0), (0, LANES - SSD_HEADS)))]
    lp = dict(cw=lru_conv_w[0], cb=_row(lru_conv_b[0]), wa=_block_diag(lru_wa[0]), wx=_block_diag(lru_wx[0]),
              ba=_row(lru_ba[0]), bx=_row(lru_bx[0]), lam=_row(lru_lambda[0]), ng=_row(lru_norm_g[0]))
    sp = dict(cw=ssd_conv_w[0], cb=_row(ssd_conv_b[0]), dtb=_pad_lanes(ssd_dt_bias[0]), alog=_pad_lanes(ssd_a_log[0]),
              dexp=_row(jnp.repeat(ssd_d[0], SSD_P)), ng=_row(ssd_norm_g[0]),
              dtbe=_row(jnp.repeat(ssd_dt_bias[0], SSD_P)), aloge=_row(jnp.repeat(ssd_a_log[0], SSD_P)))
    wrt = jnp.zeros((D, LANES), F32).at[:, 0:N_GROUPS].set(w_rg[0]).at[:, GROUP_SIZE:ROUTE_ROWS].set(w_re[0]).astype(BF16)
    brc = jnp.zeros((LANES, 1), F32).at[0:N_GROUPS, 0].set(b_rg[0]).at[GROUP_SIZE:ROUTE_ROWS, 0].set(b_re[0])
    pp = dict(wo=w_out[0].astype(BF16), l1g=_row(ln1_g[0]), l1b=_row(ln1_b[0]), wrt=wrt, brc=brc)

    table = _ada_call(c_all, w_ada[0], _row(b_ada[0]))
    mod_p = _Mod(table, nbs, True)
    mod_s = _Mod(table, 0, False)

    xg, z, xbc, dtr, p_lru_conv, p_ssd_conv = _proj_conv_call(x_prompt, mod_p, ws_p, lp, sp, SEQ_TILE)
    yl, p_lru_h = _lru_call(xg, lp, SEQ_TILE)
    ys, p_ssd = _ssd_call(xbc, z, dtr, sp, SEQ_TILE // SSD_Q)
    x1_p, v_p, eid_p, gw_p = _post_call(yl, ys, x_prompt, mod_p, pp, SEQ_TILE)

    xs3 = x_sample.reshape(1, nbs, D)
    xg_s, z_s, xbc_s, dte_s = _proj_call(xs3, mod_s, ws_s, nbs)
    xg_s, z_s, xbc_s, dte_s = xg_s[0], z_s[0], xbc_s[0], dte_s[0]
    yl_s, h_s, xs_s, bm_s, cm_s, xdtt, dect, s_lru_conv, s_ssd_conv = _srow_call(
        xg_s, xbc_s, dte_s, jnp.swapaxes(state_lru_conv[0], 0, 1), state_lru_h[0],
        jnp.swapaxes(state_ssd_conv[0], 0, 1), lp, sp)
    s_new, ys_s = _sstate_call(state_ssd[0].reshape(nbs, SSD_INNER, SSD_N), xdtt, dect,
                               bm_s, cm_s, xs_s, z_s, sp["dexp"], sp["ng"], SUBLANES)
    x1_s, v_s, eid_s, gw_s = _post_call(yl_s[None], ys_s[None], xs3, mod_s, pp, nbs)

    s_pad = -nbs % ROUTE_TILE
    dest, counts = _route_call(eid_p, jnp.pad(eid_s, ((0, 0), (0, s_pad)), constant_values=-1))
    counts = counts[:N_EXPERTS, 0].astype(I32)
    n_blocks = -(-(2 * t_all) // MOE_BM) + N_EXPERTS
    pend = jnp.cumsum(((counts + MOE_BM - 1) // MOE_BM) * MOE_BM)
    blk_start = jnp.arange(n_blocks, dtype=I32) * MOE_BM
    blk_e = jnp.minimum(jnp.sum((pend[None, :] <= blk_start[:, None]).astype(I32), axis=1), N_EXPERTS - 1)
    nblk = (pend[-1:] // MOE_BM).astype(I32)
    blk_ids = jnp.arange(n_blocks, dtype=I32)
    last_of_expert = jnp.any(((pend // MOE_BM - 1)[None, :] == blk_ids[:, None]) & (counts > 0)[None, :], axis=1)
    zflag = (last_of_expert | (blk_ids >= nblk[0])).astype(I32)
    used = counts > 0
    eids = jnp.arange(N_EXPERTS, dtype=I32)
    later_used = (eids[None, :] > eids[:, None]) & used[None, :]
    nxt_e = jnp.min(jnp.where(later_used, eids[None, :], N_EXPERTS), axis=1)
    nxt_e = jnp.where(nxt_e == N_EXPERTS, -1, nxt_e).astype(I32)
    slot_e = ((jnp.cumsum(used.astype(I32)) - used.astype(I32)) % 2).astype(I32)
    xpad = _dispatch_call(zflag, dest, v_p, v_s, n_blocks, 4 * TOK_TILE, TOK_TILE)
    ypad = _expert_call(blk_e, nblk, nxt_e, slot_e, xpad, w_gate[0], w_up[0], w_down[0])
    l2g, l2b = _row(ln2_g[0]), _row(ln2_b[0])
    y_p = _combine_call(dest, gw_p[0].reshape(t_p, 1), gw_p[1].reshape(t_p, 1), x1_p, mod_p, l2g, l2b, ypad, 0,
                        4 * TOK_TILE)
    y_s = _combine_call(dest, gw_s[0].reshape(nbs, 1), gw_s[1].reshape(nbs, 1), x1_s, mod_s, l2g, l2b, ypad, t_p,
                        TOK_TILE)

    return (y_p, y_s.reshape(nbs, 1, D),
            p_lru_conv[None], p_lru_h.reshape(1, nbp, LRU_W), p_ssd_conv[None],
            p_ssd.reshape(1, nbp, SSD_HEADS, SSD_P, SSD_N),
            jnp.swapaxes(s_lru_conv, 0, 1)[None], h_s[None], jnp.swapaxes(s_ssd_conv, 0, 1)[None],
            s_new.reshape(1, nbs, SSD_HEADS, SSD_P, SSD_N))
```

```python
import functools
import math

import jax
import jax.numpy as jnp
from jax import lax
from jax.experimental import pallas as pl
from jax.experimental.pallas import tpu as pltpu

F32 = jnp.float32
BF16 = jnp.bfloat16
I32 = jnp.int32
HIGHEST = lax.Precision.HIGHEST

D = 1024
DEPTH = 1
CONV_W = 4
LRU_W = D
LRU_HEADS = 16
LRU_C = 8.0
LRU_PACK = 4
LRU_PACK_W = LRU_PACK * (LRU_W // LRU_HEADS)
SSD_INNER = D
SSD_HEADS = 16
SSD_P = SSD_INNER // SSD_HEADS
SSD_GROUPS = 2
SSD_N = 128
SSD_Q = 128
SSD_CONV_DIM = SSD_INNER + 2 * SSD_GROUPS * SSD_N
N_GROUPS = 4
GROUP_SIZE = 8
N_EXPERTS = N_GROUPS * GROUP_SIZE
D_FF = D // 2
N_MOD = 6
LN_EPS = 1e-5
RMS_EPS = 1e-6
ALPHA = (2.0 * DEPTH) ** 0.25

LANES = 128
SUBLANES = 8
VMEM_LIMIT = 56 * 1024 * 1024
SEQ_TILE = 512
TOK_TILE = 128
ROUTE_TILE = 1024
DMA_UNROLL = 8
MOE_BM = 384
ROUTE_ROWS = 40

NT_DIMS = (((1,), (1,)), ((), ()))


def _tile_rows(rows):
    return (rows * SUBLANES, LANES)


def _cparams(sem):
    return pltpu.CompilerParams(dimension_semantics=sem, vmem_limit_bytes=VMEM_LIMIT)


def _sigmoid(x):
    return 0.5 * (jnp.tanh(0.5 * x) + 1.0)


def _silu(x):
    return x * _sigmoid(x)


def _softplus(x):
    return jnp.maximum(x, 0.0) + jnp.log1p(jnp.exp(-jnp.abs(x)))


def _gelu_tanh(x):
    return 0.5 * x * (1.0 + jnp.tanh(math.sqrt(2.0 / math.pi) * (x + 0.044715 * (x * x * x))))


def _layer_norm(x):
    mu = jnp.mean(x, axis=-1, keepdims=True)
    xc = x - mu
    var = jnp.mean(xc * xc, axis=-1, keepdims=True)
    return xc * lax.rsqrt(var + LN_EPS)


def _rms_norm(x, g):
    return x * lax.rsqrt(jnp.mean(x * x, axis=-1, keepdims=True) + RMS_EPS) * g


def _dot(a, b, **kw):
    return jnp.dot(a, b, preferred_element_type=F32, **kw)


def _dot_nt(a, b, **kw):
    return lax.dot_general(a, b, NT_DIMS, preferred_element_type=F32, **kw)


def _store_row_tiles(ref, val):
    rows = val.shape[0]
    for j in range(SUBLANES):
        ref[pl.ds(j, rows, stride=SUBLANES), :] = val[:, j * LANES:(j + 1) * LANES]


def _load_row_tiles(ref):
    rows = ref.shape[0] // SUBLANES
    return jnp.concatenate([ref[pl.ds(j, rows, stride=SUBLANES), :] for j in range(SUBLANES)], axis=1)


def _lru_gates(xc, xb, wa, wx, ba, bx, sp):
    r = _sigmoid(_dot(xb, wa) + ba)
    i = _sigmoid(_dot(xb, wx) + bx)
    log_a = (-LRU_C) * r * sp
    a = jnp.exp(log_a)
    mult = jnp.sqrt(jnp.tanh(-log_a) * (a * a + 1.0))
    return a, mult * (i * xc)


def _ada_kernel(c_ref, w_ref, b_ref, o_ref):
    s = _silu(c_ref[...]).astype(BF16)
    o_ref[...] = _dot(s, w_ref[...].astype(BF16)) + b_ref[...]


def _ada_call(c_all, w_ada, b_ada):
    rows = c_all.shape[0]
    tn = 512
    per_mod = D // tn
    return pl.pallas_call(
        _ada_kernel,
        out_shape=jax.ShapeDtypeStruct((N_MOD, rows, D), F32),
        grid=(N_MOD * per_mod,),
        in_specs=[pl.BlockSpec((rows, D), lambda j: (0, 0)),
                  pl.BlockSpec((D, tn), lambda j: (0, j)),
                  pl.BlockSpec((1, tn), lambda j: (0, j))],
        out_specs=pl.BlockSpec((None, rows, tn), lambda j: (j // per_mod, 0, j % per_mod)),
        compiler_params=_cparams(("arbitrary",)),
        name="ada",
    )(c_all, w_ada, b_ada)


def _wcast_kernel(wt_ref, tail_ref, *o_refs, firsts):
    j = pl.program_id(0)
    *o_refs, tail_out = o_refs
    tail_out[...] = tail_ref[...].astype(BF16)
    wb = wt_ref[...].T.astype(BF16)
    for k, o_ref in enumerate(o_refs):
        last = firsts[k + 1] if k + 1 < len(firsts) else pl.num_programs(0)

        @pl.when((j >= firsts[k]) & (j < last))
        def _():
            o_ref[...] = wb


def _wcast_call(wt, bounds, tc):
    n, k_dim = wt.shape
    assert all(lo % tc == 0 and hi % tc == 0 for lo, hi in bounds)
    assert all(a[1] == b[0] for a, b in zip(bounds, bounds[1:]))
    firsts = tuple((lo - bounds[0][0]) // tc for lo, _ in bounds)
    n_steps = (bounds[-1][1] - bounds[0][0]) // tc
    base = bounds[0][0] // tc
    tail = n - bounds[-1][1]
    assert tail > 0 and bounds[-1][1] % tail == 0

    def out_spec(first, nblk):
        return pl.BlockSpec((k_dim, tc), lambda j: (0, jnp.clip(j - first, 0, nblk - 1)))

    return pl.pallas_call(
        functools.partial(_wcast_kernel, firsts=firsts),
        out_shape=[jax.ShapeDtypeStruct((k_dim, hi - lo), BF16) for lo, hi in bounds]
                  + [jax.ShapeDtypeStruct((tail, k_dim), BF16)],
        grid=(n_steps,),
        in_specs=[pl.BlockSpec((tc, k_dim), lambda j: (base + j, 0)),
                  pl.BlockSpec((tail, k_dim), lambda j: (bounds[-1][1] // tail, 0))],
        out_specs=[out_spec(f, (hi - lo) // tc) for f, (lo, hi) in zip(firsts, bounds)]
                  + [pl.BlockSpec((tail, k_dim), lambda j: (0, 0))],
        compiler_params=_cparams(("arbitrary",)),
        name="wcast",
    )(wt, wt)


class _Mod:
    def __init__(self, table, row0, per_batch):
        self.table, self.row0, self.per_batch = table, row0, per_batch

    def spec(self, k, tl, batch_of, tile_of):
        if self.per_batch:
            blk = self.row0 // SUBLANES
            return pl.BlockSpec((None, SUBLANES, D), lambda *g: (k, blk, 0))
        blk = self.row0 // tl
        return pl.BlockSpec((None, tl, D), lambda *g: (k, blk + tile_of(*g), 0))


def _mod_rows(ref, per_batch, b):
    return ref[pl.ds(b, 1), :] if per_batch else ref[...]


def _proj_kernel(x_ref, sh_ref, sc_ref, w1_ref, w2_ref, w3_ref, w4_ref, o1_ref, o2_ref, o3_ref, o4_ref, *, per_batch):
    b = pl.program_id(0)
    u = _layer_norm(x_ref[...]) * (1.0 + _mod_rows(sc_ref, per_batch, b)) + _mod_rows(sh_ref, per_batch, b)
    ub = u.astype(BF16)
    o1_ref[...] = _dot(ub, w1_ref[...])
    o2_ref[...] = _dot(ub, w2_ref[...])
    o3_ref[...] = _dot(ub, w3_ref[...])
    o4_ref[...] = _dot(ub, w4_ref[...])


def _causal_conv(xpad, x_new, cw_ref, cb_ref, conv_ref, tl):
    xpad[SUBLANES:SUBLANES + tl, :] = x_new
    off = SUBLANES - (CONV_W - 1)
    acc = xpad[off:off + tl, :] * cw_ref[0:1, :]
    for k in range(1, CONV_W):
        acc = acc + xpad[off + k:off + k + tl, :] * cw_ref[k:k + 1, :]
    conv_ref[...] = xpad[SUBLANES + tl - (CONV_W - 1):SUBLANES + tl, :]
    xpad[0:SUBLANES, :] = xpad[tl:tl + SUBLANES, :]
    return acc + cb_ref[...]


def _proj_conv_kernel(x_ref, sh_ref, sc_ref, w1_ref, w2_ref, w3_ref, w4_ref, lcw_ref, lcb_ref, scw_ref, scb_ref,
                      o1_ref, o2_ref, o3_ref, o4_ref, lconv_ref, sconv_ref, lpad, spad, *, tl):
    b = pl.program_id(0)

    @pl.when(pl.program_id(1) == 0)
    def _():
        lpad[0:SUBLANES, :] = jnp.zeros((SUBLANES, LRU_W), F32)
        spad[0:SUBLANES, :] = jnp.zeros((SUBLANES, SSD_CONV_DIM), F32)

    u = _layer_norm(x_ref[...]) * (1.0 + _mod_rows(sc_ref, True, b)) + _mod_rows(sh_ref, True, b)
    ub = u.astype(BF16)
    o1_ref[:, 0:LRU_W] = _causal_conv(lpad, _dot(ub, w1_ref[:, 0:LRU_W]), lcw_ref, lcb_ref, lconv_ref, tl)
    o1_ref[:, LRU_W:2 * LRU_W] = _dot(ub, w1_ref[:, LRU_W:2 * LRU_W])
    o2_ref[...] = _dot(ub, w2_ref[...])
    o3_ref[...] = _silu(_causal_conv(spad, _dot(ub, w3_ref[...]), scw_ref, scb_ref, sconv_ref, tl))
    o4_ref[...] = _dot(ub, w4_ref[...])


def _proj_conv_call(x3, mod, ws, lp, sp, tl):
    nb, L, _ = x3.shape
    widths = [w.shape[1] for w in ws]
    row = lambda b, l: (b, l, 0)
    full = lambda b, l: (0, 0)
    st = lambda b, l: (b, 0, 0)
    bof, tof = (lambda b, l: b), (lambda b, l: l)
    return pl.pallas_call(
        functools.partial(_proj_conv_kernel, tl=tl),
        out_shape=[jax.ShapeDtypeStruct((nb, L, n), F32) for n in widths]
                  + [jax.ShapeDtypeStruct((nb, CONV_W - 1, LRU_W), F32),
                     jax.ShapeDtypeStruct((nb, CONV_W - 1, SSD_CONV_DIM), F32)],
        grid=(nb, L // tl),
        in_specs=[pl.BlockSpec((None, tl, D), row), mod.spec(0, tl, bof, tof), mod.spec(1, tl, bof, tof)]
                 + [pl.BlockSpec((D, n), full) for n in widths]
                 + [pl.BlockSpec((CONV_W, LRU_W), full), pl.BlockSpec((1, LRU_W), full),
                    pl.BlockSpec((CONV_W, SSD_CONV_DIM), full), pl.BlockSpec((1, SSD_CONV_DIM), full)],
        out_specs=[pl.BlockSpec((None, tl, n), row) for n in widths]
                  + [pl.BlockSpec((None, CONV_W - 1, LRU_W), st), pl.BlockSpec((None, CONV_W - 1, SSD_CONV_DIM), st)],
        scratch_shapes=[pltpu.VMEM((tl + SUBLANES, LRU_W), F32), pltpu.VMEM((tl + SUBLANES, SSD_CONV_DIM), F32)],
        compiler_params=_cparams(("parallel", "arbitrary")),
        name="projc",
    )(x3, mod.table, mod.table, *ws, lp["cw"], lp["cb"], sp["cw"], sp["cb"])


def _proj_call(x3, mod, ws, tl):
    nb, L, _ = x3.shape
    widths = [w.shape[1] for w in ws]
    row = lambda b, l: (b, l, 0)
    full = lambda b, l: (0, 0)
    bof, tof = (lambda b, l: b), (lambda b, l: l)
    return pl.pallas_call(
        functools.partial(_proj_kernel, per_batch=mod.per_batch),
        out_shape=[jax.ShapeDtypeStruct((nb, L, n), F32) for n in widths],
        grid=(nb, L // tl),
        in_specs=[pl.BlockSpec((None, tl, D), row), mod.spec(0, tl, bof, tof), mod.spec(1, tl, bof, tof)]
                 + [pl.BlockSpec((D, n), full) for n in widths],
        out_specs=[pl.BlockSpec((None, tl, n), row) for n in widths],
        compiler_params=_cparams(("parallel", "arbitrary")),
        name="proj",
    )(x3, mod.table, mod.table, *ws)


def _scan_segments(a_sk, u_sk, h0, seg):
    stride = seg + 1
    nt = D // LANES
    rows = lambda g: pl.ds(g, SUBLANES, stride=stride)
    shape = (nt, SUBLANES, LANES)

    def local(g, carry):
        h, p = carry
        a = a_sk[:, rows(g), :]
        h = a * h + u_sk[:, rows(g), :]
        u_sk[:, rows(g), :] = h
        return h, a * p

    h_fin, p_fin = lax.fori_loop(0, seg, local, (jnp.zeros(shape, F32), jnp.ones(shape, F32)), unroll=2)
    starts = [jnp.stack([h0[:, c * LANES:(c + 1) * LANES] for c in range(nt)], axis=0)]
    for s in range(SUBLANES):
        starts.append(p_fin[:, s:s + 1, :] * starts[-1] + h_fin[:, s:s + 1, :])
    start = jnp.concatenate(starts[:SUBLANES], axis=1)

    def fixup(g, p):
        p = a_sk[:, rows(g), :] * p
        u_sk[:, rows(g), :] = u_sk[:, rows(g), :] + p * start
        return p

    lax.fori_loop(0, seg, fixup, jnp.ones(shape, F32), unroll=2)
    return jnp.concatenate([starts[SUBLANES][c] for c in range(nt)], axis=1)


def _lru_kernel(x_ref, g_ref, wa_ref, wx_ref, ba_ref, bx_ref, lam_ref, ng_ref,
                y_ref, h_ref, a_sk, u_sk, hc_scr, *, tl):
    @pl.when(pl.program_id(1) == 0)
    def _():
        hc_scr[...] = jnp.zeros((1, D), F32)

    seg = tl // SUBLANES
    sp = _softplus(-lam_ref[...])
    for j in range(LRU_W // LRU_PACK_W):
        cs = slice(j * LRU_PACK_W, (j + 1) * LRU_PACK_W)
        xc = x_ref[:, cs]
        a, u = _lru_gates(xc, xc.astype(BF16), wa_ref[j], wx_ref[j], ba_ref[:, cs], bx_ref[:, cs], sp[:, cs])
        for s in range(SUBLANES):
            for t in range(LRU_PACK_W // LANES):
                c = j * (LRU_PACK_W // LANES) + t
                a_sk[c, s * (seg + 1):s * (seg + 1) + seg, :] = a[s * seg:(s + 1) * seg, t * LANES:(t + 1) * LANES]
                u_sk[c, s * (seg + 1):s * (seg + 1) + seg, :] = u[s * seg:(s + 1) * seg, t * LANES:(t + 1) * LANES]

    h_last = _scan_segments(a_sk, u_sk, hc_scr[...], seg)
    hc_scr[...] = h_last
    h_ref[...] = h_last
    for s in range(SUBLANES):
        h = jnp.concatenate([u_sk[c, s * (seg + 1):s * (seg + 1) + seg, :] for c in range(D // LANES)], axis=1)
        y = h * _gelu_tanh(g_ref[s * seg:(s + 1) * seg, :])
        y_ref[s * seg:(s + 1) * seg, :] = _rms_norm(y, ng_ref[...]).astype(BF16)


def _lru_call(xg, p, tl):
    nb, L, _ = xg.shape
    vec = lambda b, l: (0, 0)
    blk = lambda b, l: (0, 0, 0)
    return pl.pallas_call(
        functools.partial(_lru_kernel, tl=tl),
        out_shape=[jax.ShapeDtypeStruct((nb, L, LRU_W), BF16),
                   jax.ShapeDtypeStruct((nb, 1, LRU_W), F32)],
        grid=(nb, L // tl),
        in_specs=[pl.BlockSpec((None, tl, LRU_W), lambda b, l: (b, l, 0)),
                  pl.BlockSpec((None, tl, LRU_W), lambda b, l: (b, l, 1)),
                  pl.BlockSpec((LRU_W // LRU_PACK_W, LRU_PACK_W, LRU_PACK_W), blk),
                  pl.BlockSpec((LRU_W // LRU_PACK_W, LRU_PACK_W, LRU_PACK_W), blk),
                  pl.BlockSpec((1, LRU_W), vec), pl.BlockSpec((1, LRU_W), vec),
                  pl.BlockSpec((1, LRU_W), vec), pl.BlockSpec((1, LRU_W), vec)],
        out_specs=[pl.BlockSpec((None, tl, LRU_W), lambda b, l: (b, l, 0)),
                   pl.BlockSpec((None, 1, LRU_W), lambda b, l: (b, 0, 0))],
        scratch_shapes=[pltpu.VMEM((LRU_W // LANES, tl + SUBLANES, LANES), F32),
                        pltpu.VMEM((LRU_W // LANES, tl + SUBLANES, LANES), F32),
                        pltpu.VMEM((1, LRU_W), F32)],
        compiler_params=_cparams(("parallel", "arbitrary")),
        name="lru",
    )(xg, xg, p["wa"], p["wx"], p["ba"], p["bx"], p["lam"], p["ng"])


def _ssd_kernel(xc_ref, z_blk, dt_blk, dtb_ref, alog_ref, dexp_ref, ng_ref,
                y_blk, st_ref, st_scr, *tmp, chunks):
    @pl.when(pl.program_id(1) == 0)
    def _():
        st_scr[...] = jnp.zeros((SSD_N, SSD_INNER), F32)

    sets = (tmp[0:3], tmp[3:6])
    for c in range(chunks):
        rows = pl.ds(c * SSD_Q, SSD_Q)
        _ssd_chunk(xc_ref.at[rows, :], z_blk.at[rows, :], dt_blk.at[rows, :], dtb_ref, alog_ref, dexp_ref, ng_ref,
                   y_blk.at[rows, :], st_scr, *sets[c % 2])

    @pl.when(pl.program_id(1) == pl.num_programs(1) - 1)
    def _():
        st_ref[...] = st_scr[...].T


def _ssd_chunk(xc_scr, z_ref, dt_ref, dtb_ref, alog_ref, dexp_ref, ng_ref, y_ref, st_scr, y_scr, ea_scr, ew_scr):
    q = SSD_Q
    dt = _softplus(dt_ref[...] + dtb_ref[...])
    da = dt * (-jnp.exp(alog_ref[...]))
    ri = lax.broadcasted_iota(I32, (q, q), 0)
    ci = lax.broadcasted_iota(I32, (q, q), 1)
    causal = ri >= ci
    acs = _dot(causal.astype(F32), da, precision=HIGHEST)
    acs_t = acs.T
    dt_t = dt.T
    w_end = dt * jnp.exp(acs[q - 1:q, :] - acs)

    half = SSD_INNER // SSD_GROUPS
    hpg = SSD_HEADS // SSD_GROUPS
    cgb, cbs = [], []
    for g in range(SSD_GROUPS):
        bgb = xc_scr[:, SSD_INNER + g * SSD_N:SSD_INNER + (g + 1) * SSD_N].astype(BF16)
        cgb.append(xc_scr[:, SSD_INNER + (SSD_GROUPS + g) * SSD_N:SSD_INNER + (SSD_GROUPS + g + 1) * SSD_N].astype(BF16))
        cbs.append(jnp.where(causal, _dot_nt(cgb[g], bgb), 0.0))

    low = lax.broadcasted_iota(I32, (q, LANES), 1) < SSD_P
    for k in range(SSD_HEADS // 2):
        cs = slice(k * LANES, (k + 1) * LANES)
        cb = cbs[(2 * k) // hpg]
        m, colb, wb = [], [], []
        for h in (2 * k, 2 * k + 1):
            colb.append(jnp.broadcast_to(acs[:, h:h + 1], (q, LANES)))
            wb.append(jnp.broadcast_to(w_end[:, h:h + 1], (q, LANES)))
            seg = colb[-1] - acs_t[h:h + 1, :]
            decay = jnp.exp(jnp.minimum(seg, 0.0))
            m.append((cb * (decay * dt_t[h:h + 1, :])).astype(BF16))
        ea_scr[:, cs] = jnp.where(low, colb[0], colb[1])
        ew_scr[:, cs] = jnp.where(low, wb[0], wb[1])
        xk = xc_scr[:, cs]
        zero = jnp.zeros_like(xk)
        rhs = jnp.concatenate([jnp.where(low, xk, zero), jnp.where(low, zero, xk)], axis=0).astype(BF16)
        y_scr[:, cs] = _dot(jnp.concatenate(m, axis=1), rhs)

    x = xc_scr[:, 0:SSD_INNER]
    ea = ea_scr[...]
    w = (x * ew_scr[...]).astype(BF16)
    s_in = st_scr[...]
    sb = s_in.astype(BF16)
    y_off, c_state = [], []
    for g in range(SSD_GROUPS):
        gs = slice(g * half, (g + 1) * half)
        bgt = xc_scr[:, SSD_INNER + g * SSD_N:SSD_INNER + (g + 1) * SSD_N].T.astype(BF16)
        y_off.append(_dot(cgb[g], sb[:, gs]))
        c_state.append(_dot(bgt, w[:, gs]))
    st_scr[...] = jnp.exp(ea[q - 1:q, :]) * s_in + jnp.concatenate(c_state, axis=1)
    y = y_scr[...] + jnp.concatenate(y_off, axis=1) * jnp.exp(ea) + dexp_ref[...] * x

    yz = y * _silu(z_ref[...])
    y_ref[...] = _rms_norm(yz, ng_ref[...]).astype(BF16)


def _ssd_call(xbc, z, dtr, p, chunks):
    nb, L, _ = xbc.shape
    q = SSD_Q
    tl = q * chunks
    vec = lambda b, c: (0, 0)
    row = lambda b, c: (b, c, 0)
    return pl.pallas_call(
        functools.partial(_ssd_kernel, chunks=chunks),
        out_shape=[jax.ShapeDtypeStruct((nb, L, SSD_INNER), BF16),
                   jax.ShapeDtypeStruct((nb, SSD_INNER, SSD_N), F32)],
        grid=(nb, L // tl),
        in_specs=[pl.BlockSpec((None, tl, SSD_CONV_DIM), row), pl.BlockSpec((None, tl, SSD_INNER), row),
                  pl.BlockSpec((None, tl, LANES), row),
                  pl.BlockSpec((1, LANES), vec), pl.BlockSpec((1, LANES), vec),
                  pl.BlockSpec((1, SSD_INNER), vec), pl.BlockSpec((1, SSD_INNER), vec)],
        out_specs=[pl.BlockSpec((None, tl, SSD_INNER), row),
                   pl.BlockSpec((None, SSD_INNER, SSD_N), lambda b, c: (b, 0, 0))],
        scratch_shapes=[pltpu.VMEM((SSD_N, SSD_INNER), F32)] + [pltpu.VMEM((q, SSD_INNER), F32)] * 6,
        compiler_params=_cparams(("parallel", "arbitrary")),
        name="ssd",
    )(xbc, z, dtr, p["dtb"], p["alog"], p["dexp"], p["ng"])


def _srow_kernel(xg_ref, xbc_ref, dte_ref, lconv_ref, h0_ref, sconv_ref,
                 lcw_ref, lcb_ref, wa_ref, wx_ref, ba_ref, bx_ref, lam_ref, lng_ref,
                 scw_ref, scb_ref, dtbe_ref, aloge_ref,
                 yl_ref, h_ref, xs_ref, bm_ref, cm_ref, xdtt_ref, dect_ref, lcn_ref, scn_ref, xc_scr):
    xl = xg_ref[:, 0:LRU_W]
    acc = lcb_ref[...] + xl * lcw_ref[CONV_W - 1:CONV_W, :]
    for k in range(CONV_W - 1):
        acc = acc + lconv_ref[k] * lcw_ref[k:k + 1, :]
    xc_scr[...] = acc
    for k in range(CONV_W - 2):
        lcn_ref[k] = lconv_ref[k + 1]
        scn_ref[k] = sconv_ref[k + 1]
    lcn_ref[CONV_W - 2] = xl
    scn_ref[CONV_W - 2] = xbc_ref[...]
    sp = _softplus(-lam_ref[...])
    for j in range(LRU_W // LRU_PACK_W):
        cs = slice(j * LRU_PACK_W, (j + 1) * LRU_PACK_W)
        xc = xc_scr[:, cs]
        a, u = _lru_gates(xc, xc.astype(BF16), wa_ref[j], wx_ref[j], ba_ref[:, cs], bx_ref[:, cs], sp[:, cs])
        h_ref[:, cs] = a * h0_ref[:, cs] + u
    y = h_ref[...] * _gelu_tanh(xg_ref[:, LRU_W:2 * LRU_W])
    yl_ref[...] = _rms_norm(y, lng_ref[...]).astype(BF16)

    acc = scb_ref[...] + xbc_ref[...] * scw_ref[CONV_W - 1:CONV_W, :]
    for k in range(CONV_W - 1):
        acc = acc + sconv_ref[k] * scw_ref[k:k + 1, :]
    xc = _silu(acc)
    xs = xc[:, 0:SSD_INNER]
    xs_ref[...] = xs
    bm_ref[...] = xc[:, SSD_INNER:SSD_INNER + SSD_GROUPS * SSD_N]
    cm_ref[...] = xc[:, SSD_INNER + SSD_GROUPS * SSD_N:]
    dt = _softplus(dte_ref[...] + dtbe_ref[...])
    dec = jnp.exp(dt * (-jnp.exp(aloge_ref[...])))
    xdtt = (xs * dt).T
    dect = dec.T
    for j in range(xdtt_ref.shape[0]):
        xdtt_ref[j] = xdtt[:, j * SUBLANES:(j + 1) * SUBLANES]
        dect_ref[j] = dect[:, j * SUBLANES:(j + 1) * SUBLANES]


def _srow_call(xg, xbc, dte, lconv, h0, sconv, lp, sp):
    n = xg.shape[0]
    args = (xg, xbc, dte, lconv, h0, sconv, lp["cw"], lp["cb"], lp["wa"], lp["wx"], lp["ba"], lp["bx"],
            lp["lam"], lp["ng"], sp["cw"], sp["cb"], sp["dtbe"], sp["aloge"])
    full = lambda a: pl.BlockSpec(a.shape, lambda i, nd=a.ndim: (0,) * nd)
    outs = [jax.ShapeDtypeStruct((n, LRU_W), BF16), jax.ShapeDtypeStruct((n, LRU_W), F32),
            jax.ShapeDtypeStruct((n, SSD_INNER), F32), jax.ShapeDtypeStruct((n, SSD_GROUPS * SSD_N), F32),
            jax.ShapeDtypeStruct((n, SSD_GROUPS * SSD_N), F32),
            jax.ShapeDtypeStruct((n // SUBLANES, SSD_INNER, SUBLANES), F32),
            jax.ShapeDtypeStruct((n // SUBLANES, SSD_INNER, SUBLANES), F32),
            jax.ShapeDtypeStruct((CONV_W - 1, n, LRU_W), F32), jax.ShapeDtypeStruct((CONV_W - 1, n, SSD_CONV_DIM), F32)]
    return pl.pallas_call(
        _srow_kernel,
        out_shape=outs,
        grid=(1,),
        in_specs=[full(a) for a in args],
        out_specs=[pl.BlockSpec(o.shape, lambda i, nd=len(o.shape): (0,) * nd) for o in outs],
        scratch_shapes=[pltpu.VMEM((n, LRU_W), F32)],
        compiler_params=_cparams(("arbitrary",)),
        name="srow",
    )(*args)


def _sstate_kernel(s0_ref, xq_ref, dq_ref, bm_ref, cm_ref, xs_ref, z_ref, dexp_ref, ng_ref,
                   sn_ref, ys_ref, yraw):
    nb = s0_ref.shape[0]
    half = SSD_INNER // SSD_GROUPS
    for bi in range(nb):
        brow = jnp.concatenate(
            [jnp.broadcast_to(bm_ref[bi:bi + 1, g * SSD_N:(g + 1) * SSD_N], (half, SSD_N)) for g in range(SSD_GROUPS)],
            axis=0)
        s = dq_ref[:, bi:bi + 1] * s0_ref[bi] + xq_ref[:, bi:bi + 1] * brow
        sn_ref[bi] = s
        sb = s.astype(BF16)
        for g in range(SSD_GROUPS):
            cg = cm_ref[:, g * SSD_N:(g + 1) * SSD_N].astype(BF16)
            res = _dot_nt(cg, sb[g * half:(g + 1) * half, :])
            yraw[bi:bi + 1, g * half:(g + 1) * half] = res[bi:bi + 1, :]
    y = yraw[...] + dexp_ref[...] * xs_ref[...]
    ys_ref[...] = _rms_norm(y * _silu(z_ref[...]), ng_ref[...]).astype(BF16)


def _sstate_call(s0, xq, dq, bm, cm, xs, z, dexp, ng, nb):
    n = s0.shape[0]
    row = lambda i: (i, 0)
    vec = lambda i: (0, 0)
    gn = SSD_GROUPS * SSD_N
    return pl.pallas_call(
        _sstate_kernel,
        out_shape=[jax.ShapeDtypeStruct(s0.shape, F32), jax.ShapeDtypeStruct((n, SSD_INNER), BF16)],
        grid=(n // nb,),
        in_specs=[pl.BlockSpec((nb, SSD_INNER, SSD_N), lambda i: (i, 0, 0)),
                  pl.BlockSpec((None, SSD_INNER, nb), lambda i: (i, 0, 0)),
                  pl.BlockSpec((None, SSD_INNER, nb), lambda i: (i, 0, 0)),
                  pl.BlockSpec((nb, gn), row), pl.BlockSpec((nb, gn), row),
                  pl.BlockSpec((nb, SSD_INNER), row), pl.BlockSpec((nb, SSD_INNER), row),
                  pl.BlockSpec((1, SSD_INNER), vec), pl.BlockSpec((1, SSD_INNER), vec)],
        out_specs=[pl.BlockSpec((nb, SSD_INNER, SSD_N), lambda i: (i, 0, 0)), pl.BlockSpec((nb, SSD_INNER), row)],
        scratch_shapes=[pltpu.VMEM((nb, SSD_INNER), F32)],
        compiler_params=_cparams(("parallel",)),
        name="sstate",
    )(s0, xq, dq, bm, cm, xs, z, dexp, ng)


def _post_kernel(yl_ref, ys_ref, x_ref, g1_ref, sh2_ref, sc2_ref, wo_ref, l1g_ref, l1b_ref, wrt_ref, brc_ref,
                 x1_ref, v_ref, eid_ref, gw_ref, *, per_batch):
    b = pl.program_id(0)
    o = _dot(yl_ref[...], wo_ref[0:LRU_W, :]) + _dot(ys_ref[...], wo_ref[LRU_W:LRU_W + SSD_INNER, :])
    x1 = _layer_norm(ALPHA * x_ref[...] + _mod_rows(g1_ref, per_batch, b) * o) * l1g_ref[...] + l1b_ref[...]
    x1_ref[...] = x1
    v = _layer_norm(x1) * (1.0 + _mod_rows(sc2_ref, per_batch, b)) + _mod_rows(sh2_ref, per_batch, b)
    _store_row_tiles(v_ref, v)

    lt = _dot(v.astype(BF16), wrt_ref[...]).T + brc_ref[...]
    tl = lt.shape[1]
    row = lax.broadcasted_iota(I32, (GROUP_SIZE, tl), 0).astype(F32)
    big = float(GROUP_SIZE)
    neg = -jnp.inf
    lg = jnp.where(row < N_GROUPS, lt[0:GROUP_SIZE, :], neg)
    gmax = jnp.max(lg, axis=0, keepdims=True)
    gsel = jnp.min(jnp.where(lg == gmax, row, big), axis=0, keepdims=True)
    pg = 1.0 / jnp.sum(jnp.exp(lg - gmax), axis=0, keepdims=True)
    le = lt[GROUP_SIZE:2 * GROUP_SIZE, :]
    for j in range(1, N_GROUPS):
        le = jnp.where(gsel == j, lt[GROUP_SIZE * (j + 1):GROUP_SIZE * (j + 2), :], le)
    m1 = jnp.max(le, axis=0, keepdims=True)
    i1 = jnp.min(jnp.where(le == m1, row, big), axis=0, keepdims=True)
    rest = jnp.where(row == i1, neg, le)
    m2 = jnp.max(rest, axis=0, keepdims=True)
    i2 = jnp.min(jnp.where(rest == m2, row, big), axis=0, keepdims=True)
    e2 = jnp.exp(m2 - m1)
    den = 1.0 + e2
    eid = jnp.where(row == 0, gsel * GROUP_SIZE + i1, jnp.where(row == 1, gsel * GROUP_SIZE + i2, 0.0))
    eid_ref[...] = eid.astype(I32)
    gw_ref[...] = jnp.where(row == 0, pg * (1.0 / den), jnp.where(row == 1, pg * (e2 / den), 0.0))


def _post_call(yl, ys, x3, mod, p, tl):
    nb, L, _ = x3.shape
    nl = L // tl
    row = lambda b, l: (b, l, 0)
    vec = lambda b, l: (0, 0)
    tok = lambda b, l: (b * nl + l, 0)
    tokt = lambda b, l: (0, b * nl + l)
    bof, tof = (lambda b, l: b), (lambda b, l: l)
    return pl.pallas_call(
        functools.partial(_post_kernel, per_batch=mod.per_batch),
        out_shape=[jax.ShapeDtypeStruct((nb, L, D), F32), jax.ShapeDtypeStruct(_tile_rows(nb * L), F32),
                   jax.ShapeDtypeStruct((SUBLANES, nb * L), I32), jax.ShapeDtypeStruct((SUBLANES, nb * L), F32)],
        grid=(nb, nl),
        in_specs=[pl.BlockSpec((None, tl, LRU_W), row), pl.BlockSpec((None, tl, SSD_INNER), row),
                  pl.BlockSpec((None, tl, D), row),
                  mod.spec(2, tl, bof, tof), mod.spec(3, tl, bof, tof), mod.spec(4, tl, bof, tof),
                  pl.BlockSpec((LRU_W + SSD_INNER, D), vec), pl.BlockSpec((1, D), vec), pl.BlockSpec((1, D), vec),
                  pl.BlockSpec((D, LANES), vec), pl.BlockSpec((LANES, 1), vec)],
        out_specs=[pl.BlockSpec((None, tl, D), row), pl.BlockSpec(_tile_rows(tl), tok),
                   pl.BlockSpec((SUBLANES, tl), tokt), pl.BlockSpec((SUBLANES, tl), tokt)],
        compiler_params=_cparams(("parallel", "arbitrary")),
        name="post",
    )(yl, ys, x3, mod.table, mod.table, mod.table, p["wo"], p["l1g"], p["l1b"], p["wrt"], p["brc"])


def _route_kernel(eidp_ref, eids_ref, dest_ref, cnt_ref, run, poff, *, p_tiles):
    ph = pl.program_id(0)
    t = pl.program_id(1)
    n = ROUTE_TILE
    eid = jnp.where(t < p_tiles, eidp_ref[...], eids_ref[...])
    rowi = lax.broadcasted_iota(I32, (LANES, n), 0)
    oh0 = rowi == eid[0:1, :]
    oh1 = rowi == eid[1:2, :]
    oh = oh0.astype(F32) + oh1.astype(F32)
    tile_cnt = jnp.sum(oh, axis=1, keepdims=True)

    @pl.when((ph == 0) & (t == 0))
    def _():
        run[...] = jnp.zeros((LANES, n), F32)

    @pl.when(ph == 0)
    def _():
        run[...] = run[...] + tile_cnt

    @pl.when((ph == 1) & (t == 0))
    def _():
        counts = run[...]
        cnt_ref[...] = counts
        ci = counts.astype(I32)
        q = jnp.floor(counts * (1.0 / MOE_BM)).astype(I32)
        rem = ci - q * MOE_BM
        q = q + jnp.where(rem >= MOE_BM, 1, 0) - jnp.where(rem < 0, 1, 0)
        nblk = q + jnp.where(ci - q * MOE_BM > 0, 1, 0)
        r = lax.broadcasted_iota(I32, (LANES, LANES), 0)
        c = lax.broadcasted_iota(I32, (LANES, LANES), 1)
        lower = (r > c).astype(BF16)
        poff[...] = _dot(lower, nblk.astype(F32).astype(BF16)) * float(MOE_BM)
        run[...] = jnp.zeros((LANES, n), F32)

    @pl.when(ph == 1)
    def _():
        r = lax.broadcasted_iota(I32, (n, n), 0)
        c = lax.broadcasted_iota(I32, (n, n), 1)
        before = (r < c).astype(BF16)
        slot = _dot(oh.astype(BF16), before) + run[...] + poff[...]
        d0 = jnp.sum(jnp.where(oh0, slot, 0.0), axis=0, keepdims=True)
        d1 = jnp.sum(jnp.where(oh1, slot, 0.0), axis=0, keepdims=True)
        row = lax.broadcasted_iota(I32, (SUBLANES, n), 0)
        dest_ref[...] = jnp.where(row == 0, d0, jnp.where(row == 1, d1, 0.0)).astype(I32)
        run[...] = run[...] + tile_cnt


def _route_call(eid_p, eid_s):
    n = ROUTE_TILE
    p_tiles = eid_p.shape[1] // n
    s_tiles = eid_s.shape[1] // n
    t_all = (p_tiles + s_tiles) * n
    return pl.pallas_call(
        functools.partial(_route_kernel, p_tiles=p_tiles),
        out_shape=[jax.ShapeDtypeStruct((SUBLANES, t_all), I32), jax.ShapeDtypeStruct((LANES, n), F32)],
        grid=(2, p_tiles + s_tiles),
        in_specs=[pl.BlockSpec((SUBLANES, n), lambda ph, t: (0, jnp.minimum(t, p_tiles - 1))),
                  pl.BlockSpec((SUBLANES, n), lambda ph, t: (0, jnp.maximum(t - p_tiles, 0)))],
        out_specs=[pl.BlockSpec((SUBLANES, n), lambda ph, t: (0, t * ph)),
                   pl.BlockSpec((LANES, n), lambda ph, t: (0, 0))],
        scratch_shapes=[pltpu.VMEM((LANES, n), F32), pltpu.VMEM((LANES, n), F32)],
        compiler_params=_cparams(("arbitrary", "arbitrary")),
        name="route",
    )(eid_p, eid_s)


def _row_copy(src, src_row, dst, dst_row, sem):
    s0 = pl.multiple_of(src_row * SUBLANES, SUBLANES)
    d0 = pl.multiple_of(dst_row * SUBLANES, SUBLANES)
    return pltpu.make_async_copy(src.at[pl.ds(s0, SUBLANES), :], dst.at[pl.ds(d0, SUBLANES), :], sem)


def _scatter_rows(dest_ref, v_ref, xpad_ref, sem, tl):
    def start(r, carry):
        for k in range(2):
            _row_copy(v_ref, r, xpad_ref, dest_ref[k, r], sem).start(priority=k)
        return carry

    def wait(r, carry):
        for k in range(2):
            _row_copy(v_ref, r, xpad_ref, dest_ref[k, r], sem).wait()
        return carry

    lax.fori_loop(0, tl, start, 0, unroll=DMA_UNROLL)
    lax.fori_loop(0, tl, wait, 0, unroll=DMA_UNROLL)


def _dispatch_kernel(zflag_ref, destp_ref, dests_ref, vp_ref, vs_ref, xpad_ref, zbuf, sem, zsem,
                     *, tl_p, tl_s, p_tiles, n_blocks):
    blk_rows = _tile_rows(MOE_BM)[0]

    @pl.when(pl.program_id(0) == 0)
    def _():
        zbuf[...] = jnp.zeros(zbuf.shape, F32)

        def zero_block(go):
            def body(b, carry):
                @pl.when(zflag_ref[b] != 0)
                def _():
                    r0 = pl.multiple_of(b * blk_rows, blk_rows)
                    cp = pltpu.make_async_copy(zbuf, xpad_ref.at[pl.ds(r0, blk_rows), :], zsem)
                    if go:
                        cp.start()
                    else:
                        cp.wait()
                return carry
            lax.fori_loop(0, n_blocks, body, 0)

        zero_block(True)
        zero_block(False)

    @pl.when(pl.program_id(0) < p_tiles)
    def _():
        _scatter_rows(destp_ref, vp_ref, xpad_ref, sem, tl_p)

    @pl.when(pl.program_id(0) >= p_tiles)
    def _():
        _scatter_rows(dests_ref, vs_ref, xpad_ref, sem, tl_s)


def _dispatch_call(zflag, dest, v_p, v_s, n_blocks, tl_p, tl_s):
    p_tiles = v_p.shape[0] // _tile_rows(tl_p)[0]
    s_tiles = v_s.shape[0] // _tile_rows(tl_s)[0]
    s_off = p_tiles * tl_p // tl_s
    return pl.pallas_call(
        functools.partial(_dispatch_kernel, tl_p=tl_p, tl_s=tl_s, p_tiles=p_tiles, n_blocks=n_blocks),
        out_shape=jax.ShapeDtypeStruct(_tile_rows(n_blocks * MOE_BM), F32),
        grid_spec=pltpu.PrefetchScalarGridSpec(
            num_scalar_prefetch=1,
            grid=(p_tiles + s_tiles,),
            in_specs=[pl.BlockSpec((SUBLANES, tl_p), lambda t, z: (0, jnp.minimum(t, p_tiles - 1)),
                                   memory_space=pltpu.SMEM),
                      pl.BlockSpec((SUBLANES, tl_s), lambda t, z: (0, s_off + jnp.maximum(t - p_tiles, 0)),
                                   memory_space=pltpu.SMEM),
                      pl.BlockSpec(_tile_rows(tl_p), lambda t, z: (jnp.minimum(t, p_tiles - 1), 0)),
                      pl.BlockSpec(_tile_rows(tl_s), lambda t, z: (jnp.maximum(t - p_tiles, 0), 0))],
            out_specs=pl.BlockSpec(memory_space=pl.ANY),
            scratch_shapes=[pltpu.VMEM(_tile_rows(MOE_BM), F32), pltpu.SemaphoreType.DMA, pltpu.SemaphoreType.DMA]),
        compiler_params=_cparams(("arbitrary",)),
        name="dispatch",
    )(zflag, dest, dest, v_p, v_s)


def _expert_kernel(be_ref, nb_ref, nxt_ref, slot_ref, x_ref, wg_hbm, wu_hbm, wd_hbm, o_ref,
                   wgf, wuf, wdf, wgb, wub, wdb, sems):
    i = pl.program_id(0)

    def weights(e, slot, go):
        for m, (hbm, buf) in enumerate(((wg_hbm, wgf), (wu_hbm, wuf), (wd_hbm, wdf))):
            cp = pltpu.make_async_copy(hbm.at[e], buf.at[slot], sems.at[slot, m])
            if go:
                cp.start()
            else:
                cp.wait()

    @pl.when(i < nb_ref[0])
    def _():
        e = be_ref[i]
        slot = slot_ref[e]

        @pl.when(i == 0)
        def _():
            weights(e, slot, True)

        @pl.when((i == 0) | (e != be_ref[jnp.maximum(i - 1, 0)]))
        def _():
            weights(e, slot, False)
            nxt = nxt_ref[e]

            @pl.when(nxt >= 0)
            def _():
                weights(nxt, 1 - slot, True)

            wgb[...] = wgf[slot].astype(BF16)
            wub[...] = wuf[slot].astype(BF16)
            wdb[...] = wdf[slot].astype(BF16)

        x = _load_row_tiles(x_ref).astype(BF16)
        h = _silu(_dot(x, wgb[...])) * _dot(x, wub[...])
        _store_row_tiles(o_ref, _dot(h.astype(BF16), wdb[...]))


def _expert_call(blk_e, nblk, nxt_e, slot_e, xpad, w_gate, w_up, w_down):
    n_rows = xpad.shape[0] // SUBLANES
    blk = lambda i, be, nb, nx, sl: (jnp.minimum(i, nb[0] - 1), 0)
    hbm = pl.BlockSpec(memory_space=pl.ANY)
    return pl.pallas_call(
        _expert_kernel,
        out_shape=jax.ShapeDtypeStruct(_tile_rows(n_rows), F32),
        grid_spec=pltpu.PrefetchScalarGridSpec(
            num_scalar_prefetch=4,
            grid=(n_rows // MOE_BM,),
            in_specs=[pl.BlockSpec(_tile_rows(MOE_BM), blk), hbm, hbm, hbm],
            out_specs=pl.BlockSpec(_tile_rows(MOE_BM), blk),
            scratch_shapes=[pltpu.VMEM((2, D, D_FF), F32), pltpu.VMEM((2, D, D_FF), F32), pltpu.VMEM((2, D_FF, D), F32),
                            pltpu.VMEM((D, D_FF), BF16), pltpu.VMEM((D, D_FF), BF16), pltpu.VMEM((D_FF, D), BF16),
                            pltpu.SemaphoreType.DMA((2, 3))]),
        input_output_aliases={4: 0},
        compiler_params=_cparams(("arbitrary",)),
        name="expert",
    )(blk_e, nblk, nxt_e, slot_e, xpad, w_gate, w_up, w_down)


def _combine_kernel(dest_ref, gw_ref, x1_ref, g2_ref, l2g_ref, l2b_ref, ypad_ref, y_ref, ybuf, sems,
                    *, tok_off, k_stride, tl, nl, per_batch):
    i = pl.program_id(0)

    def gather(tile, slot, go):
        base = tok_off + tile * tl

        def body(r, carry):
            for k in range(2):
                cp = _row_copy(ypad_ref, dest_ref[k * k_stride + base + r], ybuf.at[slot, k], r, sems.at[slot])
                if go:
                    cp.start(priority=k)
                else:
                    cp.wait()
            return carry

        lax.fori_loop(0, tl, body, 0, unroll=DMA_UNROLL)

    @pl.when(i == 0)
    def _():
        gather(0, 0, True)

    @pl.when(i + 1 < pl.num_programs(0))
    def _():
        gather(i + 1, (i + 1) % 2, True)

    slot = i % 2
    gather(i, slot, False)
    gw_t = jnp.concatenate([gw_ref[...], jnp.zeros((LANES - SUBLANES, tl), F32)], axis=0).T
    f = _load_row_tiles(ybuf.at[slot, 0]) * gw_t[:, 0:1] + _load_row_tiles(ybuf.at[slot, 1]) * gw_t[:, 1:2]
    g2 = _mod_rows(g2_ref, per_batch, i // nl)
    y_ref[...] = _layer_norm(ALPHA * x1_ref[...] + g2 * f) * l2g_ref[...] + l2b_ref[...]


def _combine_call(dest, gw, x1, mod, l2g, l2b, ypad, tok_off, tl):
    nb, L, _ = x1.shape
    nl = L // tl
    row = lambda i, d: (i // nl, i % nl, 0)
    vec = lambda i, d: (0, 0)
    g2_spec = mod.spec(5, tl, lambda i, d: i // nl, lambda i, d: i % nl)
    return pl.pallas_call(
        functools.partial(_combine_kernel, tok_off=tok_off, k_stride=dest.shape[1], tl=tl, nl=nl,
                          per_batch=mod.per_batch),
        out_shape=jax.ShapeDtypeStruct((nb, L, D), F32),
        grid_spec=pltpu.PrefetchScalarGridSpec(
            num_scalar_prefetch=1,
            grid=(nb * nl,),
            in_specs=[pl.BlockSpec((SUBLANES, tl), lambda i, d: (0, i)),
                      pl.BlockSpec((None, tl, D), row), g2_spec,
                      pl.BlockSpec((1, D), vec), pl.BlockSpec((1, D), vec),
                      pl.BlockSpec(memory_space=pl.ANY)],
            out_specs=pl.BlockSpec((None, tl, D), row),
            scratch_shapes=[pltpu.VMEM((2, 2) + _tile_rows(tl), F32), pltpu.SemaphoreType.DMA((2,))]),
        compiler_params=_cparams(("arbitrary",)),
        name="combine",
    )(dest[:2].reshape(-1), gw, x1, mod.table, l2g, l2b, ypad)


def _block_diag(w):
    nh, blk, _ = w.shape
    w4 = w.reshape(nh // LRU_PACK, LRU_PACK, blk, blk)
    eye = jnp.eye(LRU_PACK, dtype=w.dtype)
    out = jnp.einsum("gaij,ab->gaibj", w4, eye)
    return out.reshape(nh // LRU_PACK, LRU_PACK * blk, LRU_PACK * blk).astype(BF16)


def _row(v):
    return v.reshape(1, -1).astype(F32)


def _pad_lanes(v):
    return jnp.pad(v.reshape(1, -1).astype(F32), ((0, 0), (0, LANES - v.shape[-1])))


def kernel(x_prompt, x_sample, c_prompt, c_sample, state_lru_conv, state_lru_h, state_ssd_conv, state_ssd, w_ada, b_ada, w_in, lru_conv_w, lru_conv_b, lru_wa, lru_ba, lru_wx, lru_bx, lru_lambda, lru_norm_g, ssd_conv_w, ssd_conv_b, ssd_dt_bias, ssd_a_log, ssd_d, ssd_norm_g, w_out, ln1_g, ln1_b, w_rg, b_rg, w_re, b_re, w_gate, w_up, w_down, ln2_g, ln2_b):
    assert w_ada.shape[0] == DEPTH == 1
    nbp, seq, _ = x_prompt.shape
    nbs = x_sample.shape[0]
    t_p = nbp * seq
    t_all = t_p + nbs
    assert x_sample.shape[1] == 1 and t_p % ROUTE_TILE == 0 and nbs % TOK_TILE == 0 and nbp == SUBLANES

    c_rows = -(-(nbp + nbs) // 16) * 16
    c_all = jnp.pad(jnp.concatenate([c_sample, c_prompt], axis=0), ((0, c_rows - nbp - nbs), (0, 0)))
    w_in0 = w_in[0]
    o_z, o_xbc, o_dt = 2 * LRU_W, 2 * LRU_W + SSD_INNER, 2 * LRU_W + SSD_INNER + SSD_CONV_DIM
    w_in_t = jnp.swapaxes(w_in0, 0, 1)
    *ws_p, w_dt_t = _wcast_call(w_in_t, ((0, o_z), (o_z, o_xbc), (o_xbc, o_dt)), 512)
    w_dt = jnp.swapaxes(w_dt_t, 0, 1)
    ws_s = ws_p + [jnp.repeat(w_dt, SSD_P, axis=1)]
    ws_p = ws_p + [jnp.pad(w_dt, ((0, 0), (0, LANES - SSD_HEADS)))]
    lp = dict(cw=lru_conv_w[0], cb=_row(lru_conv_b[0]), wa=_block_diag(lru_wa[0]), wx=_block_diag(lru_wx[0]),
              ba=_row(lru_ba[0]), bx=_row(lru_bx[0]), lam=_row(lru_lambda[0]), ng=_row(lru_norm_g[0]))
    sp = dict(cw=ssd_conv_w[0], cb=_row(ssd_conv_b[0]), dtb=_pad_lanes(ssd_dt_bias[0]), alog=_pad_lanes(ssd_a_log[0]),
              dexp=_row(jnp.repeat(ssd_d[0], SSD_P)), ng=_row(ssd_norm_g[0]),
              dtbe=_row(jnp.repeat(ssd_dt_bias[0], SSD_P)), aloge=_row(jnp.repeat(ssd_a_log[0], SSD_P)))
    wrt = jnp.zeros((D, LANES), F32).at[:, 0:N_GROUPS].set(w_rg[0]).at[:, GROUP_SIZE:ROUTE_ROWS].set(w_re[0]).astype(BF16)
    brc = jnp.zeros((LANES, 1), F32).at[0:N_GROUPS, 0].set(b_rg[0]).at[GROUP_SIZE:ROUTE_ROWS, 0].set(b_re[0])
    pp = dict(wo=w_out[0].astype(BF16), l1g=_row(ln1_g[0]), l1b=_row(ln1_b[0]), wrt=wrt, brc=brc)

    table = _ada_call(c_all, w_ada[0], _row(b_ada[0]))
    mod_p = _Mod(table, nbs, True)
    mod_s = _Mod(table, 0, False)

    xg, z, xbc, dtr, p_lru_conv, p_ssd_conv = _proj_conv_call(x_prompt, mod_p, ws_p, lp, sp, SEQ_TILE)
    yl, p_lru_h = _lru_call(xg, lp, SEQ_TILE)
    ys, p_ssd = _ssd_call(xbc, z, dtr, sp, SEQ_TILE // SSD_Q)
    x1_p, v_p, eid_p, gw_p = _post_call(yl, ys, x_prompt, mod_p, pp, SEQ_TILE)

    xs3 = x_sample.reshape(1, nbs, D)
    xg_s, z_s, xbc_s, dte_s = _proj_call(xs3, mod_s, ws_s, nbs)
    xg_s, z_s, xbc_s, dte_s = xg_s[0], z_s[0], xbc_s[0], dte_s[0]
    yl_s, h_s, xs_s, bm_s, cm_s, xdtt, dect, s_lru_conv, s_ssd_conv = _srow_call(
        xg_s, xbc_s, dte_s, jnp.swapaxes(state_lru_conv[0], 0, 1), state_lru_h[0],
        jnp.swapaxes(state_ssd_conv[0], 0, 1), lp, sp)
    s_new, ys_s = _sstate_call(state_ssd[0].reshape(nbs, SSD_INNER, SSD_N), xdtt, dect,
                               bm_s, cm_s, xs_s, z_s, sp["dexp"], sp["ng"], SUBLANES)
    x1_s, v_s, eid_s, gw_s = _post_call(yl_s[None], ys_s[None], xs3, mod_s, pp, nbs)

    s_pad = -nbs % ROUTE_TILE
    dest, counts = _route_call(eid_p, jnp.pad(eid_s, ((0, 0), (0, s_pad)), constant_values=-1))
    counts = counts[:N_EXPERTS, 0].astype(I32)
    n_blocks = -(-(2 * t_all) // MOE_BM) + N_EXPERTS
    pend = jnp.cumsum(((counts + MOE_BM - 1) // MOE_BM) * MOE_BM)
    blk_start = jnp.arange(n_blocks, dtype=I32) * MOE_BM
    blk_e = jnp.minimum(jnp.sum((pend[None, :] <= blk_start[:, None]).astype(I32), axis=1), N_EXPERTS - 1)
    nblk = (pend[-1:] // MOE_BM).astype(I32)
    blk_ids = jnp.arange(n_blocks, dtype=I32)
    last_of_expert = jnp.any(((pend // MOE_BM - 1)[None, :] == blk_ids[:, None]) & (counts > 0)[None, :], axis=1)
    zflag = (last_of_expert | (blk_ids >= nblk[0])).astype(I32)
    used = counts > 0
    eids = jnp.arange(N_EXPERTS, dtype=I32)
    later_used = (eids[None, :] > eids[:, None]) & used[None, :]
    nxt_e = jnp.min(jnp.where(later_used, eids[None, :], N_EXPERTS), axis=1)
    nxt_e = jnp.where(nxt_e == N_EXPERTS, -1, nxt_e).astype(I32)
    slot_e = ((jnp.cumsum(used.astype(I32)) - used.astype(I32)) % 2).astype(I32)
    xpad = _dispatch_call(zflag, dest, v_p, v_s, n_blocks, 4 * TOK_TILE, TOK_TILE)
    ypad = _expert_call(blk_e, nblk, nxt_e, slot_e, xpad, w_gate[0], w_up[0], w_down[0])
    l2g, l2b = _row(ln2_g[0]), _row(ln2_b[0])
    y_p = _combine_call(dest, gw_p, x1_p, mod_p, l2g, l2b, ypad, 0, 4 * TOK_TILE)
    y_s = _combine_call(dest, gw_s, x1_s, mod_s, l2g, l2b, ypad, t_p, TOK_TILE)

    return (y_p, y_s.reshape(nbs, 1, D),
            p_lru_conv[None], p_lru_h.reshape(1, nbp, LRU_W), p_ssd_conv[None],
            p_ssd.reshape(1, nbp, SSD_HEADS, SSD_P, SSD_N),
            jnp.swapaxes(s_lru_conv, 0, 1)[None], h_s[None], jnp.swapaxes(s_ssd_conv, 0, 1)[None],
            s_new.reshape(1, nbs, SSD_HEADS, SSD_P, SSD_N))
```

```python
import functools
import math

import jax
import jax.numpy as jnp
from jax import lax
from jax.experimental import pallas as pl
from jax.experimental.pallas import tpu as pltpu

F32 = jnp.float32
BF16 = jnp.bfloat16
I32 = jnp.int32
HIGHEST = lax.Precision.HIGHEST

D = 1024
DEPTH = 1
CONV_W = 4
LRU_W = D
LRU_HEADS = 16
LRU_C = 8.0
LRU_PACK = 4
LRU_PACK_W = LRU_PACK * (LRU_W // LRU_HEADS)
SSD_INNER = D
SSD_HEADS = 16
SSD_P = SSD_INNER // SSD_HEADS
SSD_GROUPS = 2
SSD_N = 128
SSD_Q = 128
SSD_CONV_DIM = SSD_INNER + 2 * SSD_GROUPS * SSD_N
N_GROUPS = 4
GROUP_SIZE = 8
N_EXPERTS = N_GROUPS * GROUP_SIZE
D_FF = D // 2
N_MOD = 6
LN_EPS = 1e-5
RMS_EPS = 1e-6
ALPHA = (2.0 * DEPTH) ** 0.25

LANES = 128
SUBLANES = 8
VMEM_LIMIT = 56 * 1024 * 1024
SEQ_TILE = 512
TOK_TILE = 128
ROUTE_TILE = 1024
DMA_UNROLL = 8
MOE_BM = 384
ROUTE_ROWS = 40

NT_DIMS = (((1,), (1,)), ((), ()))


def _tile_rows(rows):
    return (rows * SUBLANES, LANES)


def _cparams(sem):
    return pltpu.CompilerParams(dimension_semantics=sem, vmem_limit_bytes=VMEM_LIMIT)


def _sigmoid(x):
    return 0.5 * (jnp.tanh(0.5 * x) + 1.0)


def _silu(x):
    return x * _sigmoid(x)


def _softplus(x):
    return jnp.maximum(x, 0.0) + jnp.log1p(jnp.exp(-jnp.abs(x)))


def _gelu_tanh(x):
    return 0.5 * x * (1.0 + jnp.tanh(math.sqrt(2.0 / math.pi) * (x + 0.044715 * (x * x * x))))


def _layer_norm(x):
    mu = jnp.mean(x, axis=-1, keepdims=True)
    xc = x - mu
    var = jnp.mean(xc * xc, axis=-1, keepdims=True)
    return xc * lax.rsqrt(var + LN_EPS)


def _rms_norm(x, g):
    return x * lax.rsqrt(jnp.mean(x * x, axis=-1, keepdims=True) + RMS_EPS) * g


def _dot(a, b, **kw):
    return jnp.dot(a, b, preferred_element_type=F32, **kw)


def _dot_nt(a, b, **kw):
    return lax.dot_general(a, b, NT_DIMS, preferred_element_type=F32, **kw)


def _store_row_tiles(ref, val):
    rows = val.shape[0]
    for j in range(SUBLANES):
        ref[pl.ds(j, rows, stride=SUBLANES), :] = val[:, j * LANES:(j + 1) * LANES]


def _load_row_tiles(ref):
    rows = ref.shape[0] // SUBLANES
    return jnp.concatenate([ref[pl.ds(j, rows, stride=SUBLANES), :] for j in range(SUBLANES)], axis=1)


def _lru_gates(xc, xb, wa, wx, ba, bx, sp):
    r = _sigmoid(_dot(xb, wa) + ba)
    i = _sigmoid(_dot(xb, wx) + bx)
    log_a = (-LRU_C) * r * sp
    a = jnp.exp(log_a)
    mult = jnp.sqrt(jnp.tanh(-log_a) * (a * a + 1.0))
    return a, mult * (i * xc)


def _ada_kernel(c_ref, w_ref, b_ref, o_ref):
    s = _silu(c_ref[...]).astype(BF16)
    o_ref[...] = _dot(s, w_ref[...].astype(BF16)) + b_ref[...]


def _ada_call(c_all, w_ada, b_ada):
    rows = c_all.shape[0]
    tn = 512
    per_mod = D // tn
    return pl.pallas_call(
        _ada_kernel,
        out_shape=jax.ShapeDtypeStruct((N_MOD, rows, D), F32),
        grid=(N_MOD * per_mod,),
        in_specs=[pl.BlockSpec((rows, D), lambda j: (0, 0)),
                  pl.BlockSpec((D, tn), lambda j: (0, j)),
                  pl.BlockSpec((1, tn), lambda j: (0, j))],
        out_specs=pl.BlockSpec((None, rows, tn), lambda j: (j // per_mod, 0, j % per_mod)),
        compiler_params=_cparams(("arbitrary",)),
        name="ada",
    )(c_all, w_ada, b_ada)


def _wcast_kernel(wt_ref, tail_ref, *o_refs, firsts):
    j = pl.program_id(0)
    *o_refs, tail_out = o_refs
    tail_out[...] = tail_ref[...].astype(BF16)
    wb = wt_ref[...].T.astype(BF16)
    for k, o_ref in enumerate(o_refs):
        last = firsts[k + 1] if k + 1 < len(firsts) else pl.num_programs(0)

        @pl.when((j >= firsts[k]) & (j < last))
        def _():
            o_ref[...] = wb


def _wcast_call(wt, bounds, tc):
    n, k_dim = wt.shape
    assert all(lo % tc == 0 and hi % tc == 0 for lo, hi in bounds)
    assert all(a[1] == b[0] for a, b in zip(bounds, bounds[1:]))
    firsts = tuple((lo - bounds[0][0]) // tc for lo, _ in bounds)
    n_steps = (bounds[-1][1] - bounds[0][0]) // tc
    base = bounds[0][0] // tc
    tail = n - bounds[-1][1]
    assert tail > 0 and bounds[-1][1] % tail == 0

    def out_spec(first, nblk):
        return pl.BlockSpec((k_dim, tc), lambda j: (0, jnp.clip(j - first, 0, nblk - 1)))

    return pl.pallas_call(
        functools.partial(_wcast_kernel, firsts=firsts),
        out_shape=[jax.ShapeDtypeStruct((k_dim, hi - lo), BF16) for lo, hi in bounds]
                  + [jax.ShapeDtypeStruct((tail, k_dim), BF16)],
        grid=(n_steps,),
        in_specs=[pl.BlockSpec((tc, k_dim), lambda j: (base + j, 0)),
                  pl.BlockSpec((tail, k_dim), lambda j: (bounds[-1][1] // tail, 0))],
        out_specs=[out_spec(f, (hi - lo) // tc) for f, (lo, hi) in zip(firsts, bounds)]
                  + [pl.BlockSpec((tail, k_dim), lambda j: (0, 0))],
        compiler_params=_cparams(("arbitrary",)),
        name="wcast",
    )(wt, wt)


class _Mod:
    def __init__(self, table, row0, per_batch):
        self.table, self.row0, self.per_batch = table, row0, per_batch

    def spec(self, k, tl, batch_of, tile_of):
        if self.per_batch:
            blk = self.row0 // SUBLANES
            return pl.BlockSpec((None, SUBLANES, D), lambda *g: (k, blk, 0))
        blk = self.row0 // tl
        return pl.BlockSpec((None, tl, D), lambda *g: (k, blk + tile_of(*g), 0))


def _mod_rows(ref, per_batch, b):
    return ref[pl.ds(b, 1), :] if per_batch else ref[...]


def _proj_kernel(x_ref, sh_ref, sc_ref, w1_ref, w2_ref, w3_ref, w4_ref, o1_ref, o2_ref, o3_ref, o4_ref, *, per_batch):
    b = pl.program_id(0)
    u = _layer_norm(x_ref[...]) * (1.0 + _mod_rows(sc_ref, per_batch, b)) + _mod_rows(sh_ref, per_batch, b)
    ub = u.astype(BF16)
    o1_ref[...] = _dot(ub, w1_ref[...])
    o2_ref[...] = _dot(ub, w2_ref[...])
    o3_ref[...] = _dot(ub, w3_ref[...])
    o4_ref[...] = _dot(ub, w4_ref[...])


def _causal_conv(xpad, x_new, cw_ref, cb_ref, conv_ref, tl):
    xpad[SUBLANES:SUBLANES + tl, :] = x_new
    off = SUBLANES - (CONV_W - 1)
    acc = xpad[off:off + tl, :] * cw_ref[0:1, :]
    for k in range(1, CONV_W):
        acc = acc + xpad[off + k:off + k + tl, :] * cw_ref[k:k + 1, :]
    conv_ref[...] = xpad[SUBLANES + tl - (CONV_W - 1):SUBLANES + tl, :]
    xpad[0:SUBLANES, :] = xpad[tl:tl + SUBLANES, :]
    return acc + cb_ref[...]


def _proj_conv_kernel(x_ref, sh_ref, sc_ref, w1_ref, w2_ref, w3_ref, w4_ref, lcw_ref, lcb_ref, scw_ref, scb_ref,
                      o1_ref, o2_ref, o3_ref, o4_ref, lconv_ref, sconv_ref, lpad, spad, *, tl):
    b = pl.program_id(0)

    @pl.when(pl.program_id(1) == 0)
    def _():
        lpad[0:SUBLANES, :] = jnp.zeros((SUBLANES, LRU_W), F32)
        spad[0:SUBLANES, :] = jnp.zeros((SUBLANES, SSD_CONV_DIM), F32)

    u = _layer_norm(x_ref[...]) * (1.0 + _mod_rows(sc_ref, True, b)) + _mod_rows(sh_ref, True, b)
    ub = u.astype(BF16)
    o1_ref[:, 0:LRU_W] = _causal_conv(lpad, _dot(ub, w1_ref[:, 0:LRU_W]), lcw_ref, lcb_ref, lconv_ref, tl)
    o1_ref[:, LRU_W:2 * LRU_W] = _dot(ub, w1_ref[:, LRU_W:2 * LRU_W])
    o2_ref[...] = _dot(ub, w2_ref[...])
    o3_ref[...] = _silu(_causal_conv(spad, _dot(ub, w3_ref[...]), scw_ref, scb_ref, sconv_ref, tl))
    o4_ref[...] = _dot(ub, w4_ref[...])


def _proj_conv_call(x3, mod, ws, lp, sp, tl):
    nb, L, _ = x3.shape
    widths = [w.shape[1] for w in ws]
    row = lambda b, l: (b, l, 0)
    full = lambda b, l: (0, 0)
    st = lambda b, l: (b, 0, 0)
    bof, tof = (lambda b, l: b), (lambda b, l: l)
    return pl.pallas_call(
        functools.partial(_proj_conv_kernel, tl=tl),
        out_shape=[jax.ShapeDtypeStruct((nb, L, n), F32) for n in widths]
                  + [jax.ShapeDtypeStruct((nb, CONV_W - 1, LRU_W), F32),
                     jax.ShapeDtypeStruct((nb, CONV_W - 1, SSD_CONV_DIM), F32)],
        grid=(nb, L // tl),
        in_specs=[pl.BlockSpec((None, tl, D), row), mod.spec(0, tl, bof, tof), mod.spec(1, tl, bof, tof)]
                 + [pl.BlockSpec((D, n), full) for n in widths]
                 + [pl.BlockSpec((CONV_W, LRU_W), full), pl.BlockSpec((1, LRU_W), full),
                    pl.BlockSpec((CONV_W, SSD_CONV_DIM), full), pl.BlockSpec((1, SSD_CONV_DIM), full)],
        out_specs=[pl.BlockSpec((None, tl, n), row) for n in widths]
                  + [pl.BlockSpec((None, CONV_W - 1, LRU_W), st), pl.BlockSpec((None, CONV_W - 1, SSD_CONV_DIM), st)],
        scratch_shapes=[pltpu.VMEM((tl + SUBLANES, LRU_W), F32), pltpu.VMEM((tl + SUBLANES, SSD_CONV_DIM), F32)],
        compiler_params=_cparams(("parallel", "arbitrary")),
        name="projc",
    )(x3, mod.table, mod.table, *ws, lp["cw"], lp["cb"], sp["cw"], sp["cb"])


def _proj_call(x3, mod, ws, tl):
    nb, L, _ = x3.shape
    widths = [w.shape[1] for w in ws]
    row = lambda b, l: (b, l, 0)
    full = lambda b, l: (0, 0)
    bof, tof = (lambda b, l: b), (lambda b, l: l)
    return pl.pallas_call(
        functools.partial(_proj_kernel, per_batch=mod.per_batch),
        out_shape=[jax.ShapeDtypeStruct((nb, L, n), F32) for n in widths],
        grid=(nb, L // tl),
        in_specs=[pl.BlockSpec((None, tl, D), row), mod.spec(0, tl, bof, tof), mod.spec(1, tl, bof, tof)]
                 + [pl.BlockSpec((D, n), full) for n in widths],
        out_specs=[pl.BlockSpec((None, tl, n), row) for n in widths],
        compiler_params=_cparams(("parallel", "arbitrary")),
        name="proj",
    )(x3, mod.table, mod.table, *ws)


def _scan_segments(a_sk, u_sk, h0, seg):
    stride = seg + 1
    nt = D // LANES
    rows = lambda g: pl.ds(g, SUBLANES, stride=stride)
    shape = (nt, SUBLANES, LANES)

    def local(g, carry):
        h, p = carry
        a = a_sk[:, rows(g), :]
        h = a * h + u_sk[:, rows(g), :]
        u_sk[:, rows(g), :] = h
        return h, a * p

    h_fin, p_fin = lax.fori_loop(0, seg, local, (jnp.zeros(shape, F32), jnp.ones(shape, F32)), unroll=2)
    starts = [jnp.stack([h0[:, c * LANES:(c + 1) * LANES] for c in range(nt)], axis=0)]
    for s in range(SUBLANES):
        starts.append(p_fin[:, s:s + 1, :] * starts[-1] + h_fin[:, s:s + 1, :])
    start = jnp.concatenate(starts[:SUBLANES], axis=1)

    def fixup(g, p):
        p = a_sk[:, rows(g), :] * p
        u_sk[:, rows(g), :] = u_sk[:, rows(g), :] + p * start
        return p

    lax.fori_loop(0, seg, fixup, jnp.ones(shape, F32), unroll=2)
    return jnp.concatenate([starts[SUBLANES][c] for c in range(nt)], axis=1)


def _lru_kernel(x_ref, g_ref, wa_ref, wx_ref, ba_ref, bx_ref, lam_ref, ng_ref,
                y_ref, h_ref, a_sk, u_sk, hc_scr, *, tl):
    @pl.when(pl.program_id(1) == 0)
    def _():
        hc_scr[...] = jnp.zeros((1, D), F32)

    seg = tl // SUBLANES
    sp = _softplus(-lam_ref[...])
    for j in range(LRU_W // LRU_PACK_W):
        cs = slice(j * LRU_PACK_W, (j + 1) * LRU_PACK_W)
        xc = x_ref[:, cs]
        a, u = _lru_gates(xc, xc.astype(BF16), wa_ref[j], wx_ref[j], ba_ref[:, cs], bx_ref[:, cs], sp[:, cs])
        for s in range(SUBLANES):
            for t in range(LRU_PACK_W // LANES):
                c = j * (LRU_PACK_W // LANES) + t
                a_sk[c, s * (seg + 1):s * (seg + 1) + seg, :] = a[s * seg:(s + 1) * seg, t * LANES:(t + 1) * LANES]
                u_sk[c, s * (seg + 1):s * (seg + 1) + seg, :] = u[s * seg:(s + 1) * seg, t * LANES:(t + 1) * LANES]

    h_last = _scan_segments(a_sk, u_sk, hc_scr[...], seg)
    hc_scr[...] = h_last
    h_ref[...] = h_last
    for s in range(SUBLANES):
        h = jnp.concatenate([u_sk[c, s * (seg + 1):s * (seg + 1) + seg, :] for c in range(D // LANES)], axis=1)
        y = h * _gelu_tanh(g_ref[s * seg:(s + 1) * seg, :])
        y_ref[s * seg:(s + 1) * seg, :] = _rms_norm(y, ng_ref[...]).astype(BF16)


def _lru_call(xg, p, tl):
    nb, L, _ = xg.shape
    vec = lambda b, l: (0, 0)
    blk = lambda b, l: (0, 0, 0)
    return pl.pallas_call(
        functools.partial(_lru_kernel, tl=tl),
        out_shape=[jax.ShapeDtypeStruct((nb, L, LRU_W), BF16),
                   jax.ShapeDtypeStruct((nb, 1, LRU_W), F32)],
        grid=(nb, L // tl),
        in_specs=[pl.BlockSpec((None, tl, LRU_W), lambda b, l: (b, l, 0)),
                  pl.BlockSpec((None, tl, LRU_W), lambda b, l: (b, l, 1)),
                  pl.BlockSpec((LRU_W // LRU_PACK_W, LRU_PACK_W, LRU_PACK_W), blk),
                  pl.BlockSpec((LRU_W // LRU_PACK_W, LRU_PACK_W, LRU_PACK_W), blk),
                  pl.BlockSpec((1, LRU_W), vec), pl.BlockSpec((1, LRU_W), vec),
                  pl.BlockSpec((1, LRU_W), vec), pl.BlockSpec((1, LRU_W), vec)],
        out_specs=[pl.BlockSpec((None, tl, LRU_W), lambda b, l: (b, l, 0)),
                   pl.BlockSpec((None, 1, LRU_W), lambda b, l: (b, 0, 0))],
        scratch_shapes=[pltpu.VMEM((LRU_W // LANES, tl + SUBLANES, LANES), F32),
                        pltpu.VMEM((LRU_W // LANES, tl + SUBLANES, LANES), F32),
                        pltpu.VMEM((1, LRU_W), F32)],
        compiler_params=_cparams(("parallel", "arbitrary")),
        name="lru",
    )(xg, xg, p["wa"], p["wx"], p["ba"], p["bx"], p["lam"], p["ng"])


def _ssd_kernel(xc_ref, z_blk, dt_blk, dtb_ref, alog_ref, dexp_ref, ng_ref,
                y_blk, st_ref, st_scr, *tmp, chunks):
    @pl.when(pl.program_id(1) == 0)
    def _():
        st_scr[...] = jnp.zeros((SSD_N, SSD_INNER), F32)

    sets = (tmp[0:3], tmp[3:6])
    for c in range(chunks):
        rows = pl.ds(c * SSD_Q, SSD_Q)
        _ssd_chunk(xc_ref.at[rows, :], z_blk.at[rows, :], dt_blk.at[rows, :], dtb_ref, alog_ref, dexp_ref, ng_ref,
                   y_blk.at[rows, :], st_scr, *sets[c % 2])

    @pl.when(pl.program_id(1) == pl.num_programs(1) - 1)
    def _():
        st_ref[...] = st_scr[...].T


def _ssd_chunk(xc_scr, z_ref, dt_ref, dtb_ref, alog_ref, dexp_ref, ng_ref, y_ref, st_scr, y_scr, ea_scr, ew_scr):
    q = SSD_Q
    dt = _softplus(dt_ref[...] + dtb_ref[...])
    da = dt * (-jnp.exp(alog_ref[...]))
    ri = lax.broadcasted_iota(I32, (q, q), 0)
    ci = lax.broadcasted_iota(I32, (q, q), 1)
    causal = ri >= ci
    acs = _dot(causal.astype(F32), da, precision=HIGHEST)
    acs_t = acs.T
    dt_t = dt.T
    w_end = dt * jnp.exp(acs[q - 1:q, :] - acs)

    half = SSD_INNER // SSD_GROUPS
    hpg = SSD_HEADS // SSD_GROUPS
    cgb, cbs = [], []
    for g in range(SSD_GROUPS):
        bgb = xc_scr[:, SSD_INNER + g * SSD_N:SSD_INNER + (g + 1) * SSD_N].astype(BF16)
        cgb.append(xc_scr[:, SSD_INNER + (SSD_GROUPS + g) * SSD_N:SSD_INNER + (SSD_GROUPS + g + 1) * SSD_N].astype(BF16))
        cbs.append(jnp.where(causal, _dot_nt(cgb[g], bgb), 0.0))

    low = lax.broadcasted_iota(I32, (q, LANES), 1) < SSD_P
    for k in range(SSD_HEADS // 2):
        cs = slice(k * LANES, (k + 1) * LANES)
        cb = cbs[(2 * k) // hpg]
        m, colb, wb = [], [], []
        for h in (2 * k, 2 * k + 1):
            colb.append(jnp.broadcast_to(acs[:, h:h + 1], (q, LANES)))
            wb.append(jnp.broadcast_to(w_end[:, h:h + 1], (q, LANES)))
            seg = colb[-1] - acs_t[h:h + 1, :]
            decay = jnp.exp(jnp.minimum(seg, 0.0))
            m.append((cb * (decay * dt_t[h:h + 1, :])).astype(BF16))
        ea_scr[:, cs] = jnp.where(low, colb[0], colb[1])
        ew_scr[:, cs] = jnp.where(low, wb[0], wb[1])
        xk = xc_scr[:, cs]
        zero = jnp.zeros_like(xk)
        rhs = jnp.concatenate([jnp.where(low, xk, zero), jnp.where(low, zero, xk)], axis=0).astype(BF16)
        y_scr[:, cs] = _dot(jnp.concatenate(m, axis=1), rhs)

    x = xc_scr[:, 0:SSD_INNER]
    ea = ea_scr[...]
    w = (x * ew_scr[...]).astype(BF16)
    s_in = st_scr[...]
    sb = s_in.astype(BF16)
    y_off, c_state = [], []
    for g in range(SSD_GROUPS):
        gs = slice(g * half, (g + 1) * half)
        bgt = xc_scr[:, SSD_INNER + g * SSD_N:SSD_INNER + (g + 1) * SSD_N].T.astype(BF16)
        y_off.append(_dot(cgb[g], sb[:, gs]))
        c_state.append(_dot(bgt, w[:, gs]))
    st_scr[...] = jnp.exp(ea[q - 1:q, :]) * s_in + jnp.concatenate(c_state, axis=1)
    y = y_scr[...] + jnp.concatenate(y_off, axis=1) * jnp.exp(ea) + dexp_ref[...] * x

    yz = y * _silu(z_ref[...])
    y_ref[...] = _rms_norm(yz, ng_ref[...]).astype(BF16)


def _ssd_call(xbc, z, dtr, p, chunks):
    nb, L, _ = xbc.shape
    q = SSD_Q
    tl = q * chunks
    vec = lambda b, c: (0, 0)
    row = lambda b, c: (b, c, 0)
    return pl.pallas_call(
        functools.partial(_ssd_kernel, chunks=chunks),
        out_shape=[jax.ShapeDtypeStruct((nb, L, SSD_INNER), BF16),
                   jax.ShapeDtypeStruct((nb, SSD_INNER, SSD_N), F32)],
        grid=(nb, L // tl),
        in_specs=[pl.BlockSpec((None, tl, SSD_CONV_DIM), row), pl.BlockSpec((None, tl, SSD_INNER), row),
                  pl.BlockSpec((None, tl, LANES), row),
                  pl.BlockSpec((1, LANES), vec), pl.BlockSpec((1, LANES), vec),
                  pl.BlockSpec((1, SSD_INNER), vec), pl.BlockSpec((1, SSD_INNER), vec)],
        out_specs=[pl.BlockSpec((None, tl, SSD_INNER), row),
                   pl.BlockSpec((None, SSD_INNER, SSD_N), lambda b, c: (b, 0, 0))],
        scratch_shapes=[pltpu.VMEM((SSD_N, SSD_INNER), F32)] + [pltpu.VMEM((q, SSD_INNER), F32)] * 6,
        compiler_params=_cparams(("parallel", "arbitrary")),
        name="ssd",
    )(xbc, z, dtr, p["dtb"], p["alog"], p["dexp"], p["ng"])


def _srow_kernel(xg_ref, xbc_ref, dte_ref, lconv_ref, h0_ref, sconv_ref,
                 lcw_ref, lcb_ref, wa_ref, wx_ref, ba_ref, bx_ref, lam_ref, lng_ref,
                 scw_ref, scb_ref, dtbe_ref, aloge_ref,
                 yl_ref, h_ref, xs_ref, bm_ref, cm_ref, xdtt_ref, dect_ref, lcn_ref, scn_ref, xc_scr):
    xl = xg_ref[:, 0:LRU_W]
    acc = lcb_ref[...] + xl * lcw_ref[CONV_W - 1:CONV_W, :]
    for k in range(CONV_W - 1):
        acc = acc + lconv_ref[k] * lcw_ref[k:k + 1, :]
    xc_scr[...] = acc
    for k in range(CONV_W - 2):
        lcn_ref[k] = lconv_ref[k + 1]
        scn_ref[k] = sconv_ref[k + 1]
    lcn_ref[CONV_W - 2] = xl
    scn_ref[CONV_W - 2] = xbc_ref[...]
    sp = _softplus(-lam_ref[...])
    for j in range(LRU_W // LRU_PACK_W):
        cs = slice(j * LRU_PACK_W, (j + 1) * LRU_PACK_W)
        xc = xc_scr[:, cs]
        a, u = _lru_gates(xc, xc.astype(BF16), wa_ref[j], wx_ref[j], ba_ref[:, cs], bx_ref[:, cs], sp[:, cs])
        h_ref[:, cs] = a * h0_ref[:, cs] + u
    y = h_ref[...] * _gelu_tanh(xg_ref[:, LRU_W:2 * LRU_W])
    yl_ref[...] = _rms_norm(y, lng_ref[...]).astype(BF16)

    acc = scb_ref[...] + xbc_ref[...] * scw_ref[CONV_W - 1:CONV_W, :]
    for k in range(CONV_W - 1):
        acc = acc + sconv_ref[k] * scw_ref[k:k + 1, :]
    xc = _silu(acc)
    xs = xc[:, 0:SSD_INNER]
    xs_ref[...] = xs
    bm_ref[...] = xc[:, SSD_INNER:SSD_INNER + SSD_GROUPS * SSD_N]
    cm_ref[...] = xc[:, SSD_INNER + SSD_GROUPS * SSD_N:]
    dt = _softplus(dte_ref[...] + dtbe_ref[...])
    dec = jnp.exp(dt * (-jnp.exp(aloge_ref[...])))
    xdtt = (xs * dt).T
    dect = dec.T
    for j in range(xdtt_ref.shape[0]):
        xdtt_ref[j] = xdtt[:, j * SUBLANES:(j + 1) * SUBLANES]
        dect_ref[j] = dect[:, j * SUBLANES:(j + 1) * SUBLANES]


def _srow_call(xg, xbc, dte, lconv, h0, sconv, lp, sp):
    n = xg.shape[0]
    args = (xg, xbc, dte, lconv, h0, sconv, lp["cw"], lp["cb"], lp["wa"], lp["wx"], lp["ba"], lp["bx"],
            lp["lam"], lp["ng"], sp["cw"], sp["cb"], sp["dtbe"], sp["aloge"])
    full = lambda a: pl.BlockSpec(a.shape, lambda i, nd=a.ndim: (0,) * nd)
    outs = [jax.ShapeDtypeStruct((n, LRU_W), BF16), jax.ShapeDtypeStruct((n, LRU_W), F32),
            jax.ShapeDtypeStruct((n, SSD_INNER), F32), jax.ShapeDtypeStruct((n, SSD_GROUPS * SSD_N), F32),
            jax.ShapeDtypeStruct((n, SSD_GROUPS * SSD_N), F32),
            jax.ShapeDtypeStruct((n // SUBLANES, SSD_INNER, SUBLANES), F32),
            jax.ShapeDtypeStruct((n // SUBLANES, SSD_INNER, SUBLANES), F32),
            jax.ShapeDtypeStruct((CONV_W - 1, n, LRU_W), F32), jax.ShapeDtypeStruct((CONV_W - 1, n, SSD_CONV_DIM), F32)]
    return pl.pallas_call(
        _srow_kernel,
        out_shape=outs,
        grid=(1,),
        in_specs=[full(a) for a in args],
        out_specs=[pl.BlockSpec(o.shape, lambda i, nd=len(o.shape): (0,) * nd) for o in outs],
        scratch_shapes=[pltpu.VMEM((n, LRU_W), F32)],
        compiler_params=_cparams(("arbitrary",)),
        name="srow",
    )(*args)


def _sstate_kernel(s0_ref, xq_ref, dq_ref, bm_ref, cm_ref, xs_ref, z_ref, dexp_ref, ng_ref,
                   sn_ref, ys_ref, yraw):
    nb = s0_ref.shape[0]
    half = SSD_INNER // SSD_GROUPS
    for bi in range(nb):
        brow = jnp.concatenate(
            [jnp.broadcast_to(bm_ref[bi:bi + 1, g * SSD_N:(g + 1) * SSD_N], (half, SSD_N)) for g in range(SSD_GROUPS)],
            axis=0)
        s = dq_ref[:, bi:bi + 1] * s0_ref[bi] + xq_ref[:, bi:bi + 1] * brow
        sn_ref[bi] = s
        sb = s.astype(BF16)
        for g in range(SSD_GROUPS):
            cg = cm_ref[:, g * SSD_N:(g + 1) * SSD_N].astype(BF16)
            res = _dot_nt(cg, sb[g * half:(g + 1) * half, :])
            yraw[bi:bi + 1, g * half:(g + 1) * half] = res[bi:bi + 1, :]
    y = yraw[...] + dexp_ref[...] * xs_ref[...]
    ys_ref[...] = _rms_norm(y * _silu(z_ref[...]), ng_ref[...]).astype(BF16)


def _sstate_call(s0, xq, dq, bm, cm, xs, z, dexp, ng, nb):
    n = s0.shape[0]
    row = lambda i: (i, 0)
    vec = lambda i: (0, 0)
    gn = SSD_GROUPS * SSD_N
    return pl.pallas_call(
        _sstate_kernel,
        out_shape=[jax.ShapeDtypeStruct(s0.shape, F32), jax.ShapeDtypeStruct((n, SSD_INNER), BF16)],
        grid=(n // nb,),
        in_specs=[pl.BlockSpec((nb, SSD_INNER, SSD_N), lambda i: (i, 0, 0)),
                  pl.BlockSpec((None, SSD_INNER, nb), lambda i: (i, 0, 0)),
                  pl.BlockSpec((None, SSD_INNER, nb), lambda i: (i, 0, 0)),
                  pl.BlockSpec((nb, gn), row), pl.BlockSpec((nb, gn), row),
                  pl.BlockSpec((nb, SSD_INNER), row), pl.BlockSpec((nb, SSD_INNER), row),
                  pl.BlockSpec((1, SSD_INNER), vec), pl.BlockSpec((1, SSD_INNER), vec)],
        out_specs=[pl.BlockSpec((nb, SSD_INNER, SSD_N), lambda i: (i, 0, 0)), pl.BlockSpec((nb, SSD_INNER), row)],
        scratch_shapes=[pltpu.VMEM((nb, SSD_INNER), F32)],
        compiler_params=_cparams(("parallel",)),
        name="sstate",
    )(s0, xq, dq, bm, cm, xs, z, dexp, ng)


def _out_proj(yl_ref, ys_ref, wo_ref):
    return _dot(yl_ref[...], wo_ref[0:LRU_W, :]) + _dot(ys_ref[...], wo_ref[LRU_W:LRU_W + SSD_INNER, :])


def _post_kernel(yl_ref, ys_ref, x_ref, g1_ref, sh2_ref, sc2_ref, wo_ref, l1g_ref, l1b_ref, wrt_ref, brc_ref,
                 x1_ref, v_ref, eid_ref, gw_ref, *, per_batch):
    _post_tail(_out_proj(yl_ref, ys_ref, wo_ref), pl.program_id(0), x_ref, g1_ref, sh2_ref, sc2_ref,
               l1g_ref, l1b_ref, wrt_ref, brc_ref, x1_ref, v_ref, eid_ref, gw_ref, per_batch)


def _post_lag_kernel(yl_ref, ys_ref, x_ref, g1_ref, sh2_ref, sc2_ref, wo_ref, l1g_ref, l1b_ref, wrt_ref, brc_ref,
                     x1_ref, v_ref, eid_ref, gw_ref, oa, ob, *, nl):
    s = pl.program_id(0)
    b_lag = jnp.maximum(s - 1, 0) // nl

    @pl.when(s == 0)
    def _():
        ob[...] = jnp.zeros(ob.shape, F32)

    def step(o_new, o_old):
        o_new[...] = _out_proj(yl_ref, ys_ref, wo_ref)
        _post_tail(o_old[...], b_lag, x_ref, g1_ref, sh2_ref, sc2_ref, l1g_ref, l1b_ref, wrt_ref, brc_ref,
                   x1_ref, v_ref, eid_ref, gw_ref, True)

    pl.when(s % 2 == 0)(functools.partial(step, oa, ob))
    pl.when(s % 2 == 1)(functools.partial(step, ob, oa))


def _post_tail(o, b, x_ref, g1_ref, sh2_ref, sc2_ref, l1g_ref, l1b_ref, wrt_ref, brc_ref,
               x1_ref, v_ref, eid_ref, gw_ref, per_batch):
    x1 = _layer_norm(ALPHA * x_ref[...] + _mod_rows(g1_ref, per_batch, b) * o) * l1g_ref[...] + l1b_ref[...]
    x1_ref[...] = x1
    v = _layer_norm(x1) * (1.0 + _mod_rows(sc2_ref, per_batch, b)) + _mod_rows(sh2_ref, per_batch, b)
    _store_row_tiles(v_ref, v)

    lt = _dot(v.astype(BF16), wrt_ref[...]).T + brc_ref[...]
    tl = lt.shape[1]
    row = lax.broadcasted_iota(I32, (GROUP_SIZE, tl), 0).astype(F32)
    big = float(GROUP_SIZE)
    neg = -jnp.inf
    lg = jnp.where(row < N_GROUPS, lt[0:GROUP_SIZE, :], neg)
    gmax = jnp.max(lg, axis=0, keepdims=True)
    gsel = jnp.min(jnp.where(lg == gmax, row, big), axis=0, keepdims=True)
    pg = 1.0 / jnp.sum(jnp.exp(lg - gmax), axis=0, keepdims=True)
    le = lt[GROUP_SIZE:2 * GROUP_SIZE, :]
    for j in range(1, N_GROUPS):
        le = jnp.where(gsel == j, lt[GROUP_SIZE * (j + 1):GROUP_SIZE * (j + 2), :], le)
    m1 = jnp.max(le, axis=0, keepdims=True)
    i1 = jnp.min(jnp.where(le == m1, row, big), axis=0, keepdims=True)
    rest = jnp.where(row == i1, neg, le)
    m2 = jnp.max(rest, axis=0, keepdims=True)
    i2 = jnp.min(jnp.where(rest == m2, row, big), axis=0, keepdims=True)
    e2 = jnp.exp(m2 - m1)
    den = 1.0 + e2
    eid = jnp.where(row == 0, gsel * GROUP_SIZE + i1, jnp.where(row == 1, gsel * GROUP_SIZE + i2, 0.0))
    eid_ref[...] = eid.astype(I32)
    gw_ref[...] = jnp.where(row == 0, pg * (1.0 / den), jnp.where(row == 1, pg * (e2 / den), 0.0))


def _post_call(yl, ys, x3, mod, p, tl):
    nb, L, _ = x3.shape
    nl = L // tl
    row = lambda b, l: (b, l, 0)
    vec = lambda b, l: (0, 0)
    tok = lambda b, l: (b * nl + l, 0)
    tokt = lambda b, l: (0, b * nl + l)
    bof, tof = (lambda b, l: b), (lambda b, l: l)
    return pl.pallas_call(
        functools.partial(_post_kernel, per_batch=mod.per_batch),
        out_shape=[jax.ShapeDtypeStruct((nb, L, D), F32), jax.ShapeDtypeStruct(_tile_rows(nb * L), F32),
                   jax.ShapeDtypeStruct((SUBLANES, nb * L), I32), jax.ShapeDtypeStruct((SUBLANES, nb * L), F32)],
        grid=(nb, nl),
        in_specs=[pl.BlockSpec((None, tl, LRU_W), row), pl.BlockSpec((None, tl, SSD_INNER), row),
                  pl.BlockSpec((None, tl, D), row),
                  mod.spec(2, tl, bof, tof), mod.spec(3, tl, bof, tof), mod.spec(4, tl, bof, tof),
                  pl.BlockSpec((LRU_W + SSD_INNER, D), vec), pl.BlockSpec((1, D), vec), pl.BlockSpec((1, D), vec),
                  pl.BlockSpec((D, LANES), vec), pl.BlockSpec((LANES, 1), vec)],
        out_specs=[pl.BlockSpec((None, tl, D), row), pl.BlockSpec(_tile_rows(tl), tok),
                   pl.BlockSpec((SUBLANES, tl), tokt), pl.BlockSpec((SUBLANES, tl), tokt)],
        compiler_params=_cparams(("parallel", "arbitrary")),
        name="post",
    )(yl, ys, x3, mod.table, mod.table, mod.table, p["wo"], p["l1g"], p["l1b"], p["wrt"], p["brc"])


def _post_lag_call(yl, ys, x3, mod, p, tl):
    nb, L, _ = x3.shape
    nl = L // tl
    n = nb * nl
    cur = lambda s: (jnp.minimum(s, n - 1) // nl, jnp.minimum(s, n - 1) % nl, 0)
    lag = lambda s: (jnp.maximum(s - 1, 0) // nl, jnp.maximum(s - 1, 0) % nl, 0)
    lag_tok = lambda s: (jnp.maximum(s - 1, 0), 0)
    lag_tokt = lambda s: (0, jnp.maximum(s - 1, 0))
    vec = lambda s: (0, 0)
    one = lambda *g: 0
    return pl.pallas_call(
        functools.partial(_post_lag_kernel, nl=nl),
        out_shape=[jax.ShapeDtypeStruct((nb, L, D), F32), jax.ShapeDtypeStruct(_tile_rows(nb * L), F32),
                   jax.ShapeDtypeStruct((SUBLANES, nb * L), I32), jax.ShapeDtypeStruct((SUBLANES, nb * L), F32)],
        grid=(n + 1,),
        in_specs=[pl.BlockSpec((None, tl, LRU_W), cur), pl.BlockSpec((None, tl, SSD_INNER), cur),
                  pl.BlockSpec((None, tl, D), lag),
                  mod.spec(2, tl, one, one), mod.spec(3, tl, one, one), mod.spec(4, tl, one, one),
                  pl.BlockSpec((LRU_W + SSD_INNER, D), vec), pl.BlockSpec((1, D), vec), pl.BlockSpec((1, D), vec),
                  pl.BlockSpec((D, LANES), vec), pl.BlockSpec((LANES, 1), vec)],
        out_specs=[pl.BlockSpec((None, tl, D), lag), pl.BlockSpec(_tile_rows(tl), lag_tok),
                   pl.BlockSpec((SUBLANES, tl), lag_tokt), pl.BlockSpec((SUBLANES, tl), lag_tokt)],
        scratch_shapes=[pltpu.VMEM((tl, D), F32), pltpu.VMEM((tl, D), F32)],
        compiler_params=_cparams(("arbitrary",)),
        name="postl",
    )(yl, ys, x3, mod.table, mod.table, mod.table, p["wo"], p["l1g"], p["l1b"], p["wrt"], p["brc"])


def _route_kernel(eidp_ref, eids_ref, dest_ref, cnt_ref, run, poff, *, p_tiles):
    ph = pl.program_id(0)
    t = pl.program_id(1)
    n = ROUTE_TILE
    eid = jnp.where(t < p_tiles, eidp_ref[...], eids_ref[...])
    rowi = lax.broadcasted_iota(I32, (LANES, n), 0)
    oh0 = rowi == eid[0:1, :]
    oh1 = rowi == eid[1:2, :]
    oh = oh0.astype(F32) + oh1.astype(F32)
    tile_cnt = jnp.sum(oh, axis=1, keepdims=True)

    @pl.when((ph == 0) & (t == 0))
    def _():
        run[...] = jnp.zeros((LANES, n), F32)

    @pl.when(ph == 0)
    def _():
        run[...] = run[...] + tile_cnt

    @pl.when((ph == 1) & (t == 0))
    def _():
        counts = run[...]
        cnt_ref[...] = counts
        ci = counts.astype(I32)
        q = jnp.floor(counts * (1.0 / MOE_BM)).astype(I32)
        rem = ci - q * MOE_BM
        q = q + jnp.where(rem >= MOE_BM, 1, 0) - jnp.where(rem < 0, 1, 0)
        nblk = q + jnp.where(ci - q * MOE_BM > 0, 1, 0)
        r = lax.broadcasted_iota(I32, (LANES, LANES), 0)
        c = lax.broadcasted_iota(I32, (LANES, LANES), 1)
        lower = (r > c).astype(BF16)
        poff[...] = _dot(lower, nblk.astype(F32).astype(BF16)) * float(MOE_BM)
        run[...] = jnp.zeros((LANES, n), F32)

    @pl.when(ph == 1)
    def _():
        r = lax.broadcasted_iota(I32, (n, n), 0)
        c = lax.broadcasted_iota(I32, (n, n), 1)
        before = (r < c).astype(BF16)
        slot = _dot(oh.astype(BF16), before) + run[...] + poff[...]
        d0 = jnp.sum(jnp.where(oh0, slot, 0.0), axis=0, keepdims=True)
        d1 = jnp.sum(jnp.where(oh1, slot, 0.0), axis=0, keepdims=True)
        row = lax.broadcasted_iota(I32, (SUBLANES, n), 0)
        dest_ref[...] = jnp.where(row == 0, d0, jnp.where(row == 1, d1, 0.0)).astype(I32)
        run[...] = run[...] + tile_cnt


def _route_call(eid_p, eid_s):
    n = ROUTE_TILE
    p_tiles = eid_p.shape[1] // n
    s_tiles = eid_s.shape[1] // n
    t_all = (p_tiles + s_tiles) * n
    return pl.pallas_call(
        functools.partial(_route_kernel, p_tiles=p_tiles),
        out_shape=[jax.ShapeDtypeStruct((SUBLANES, t_all), I32), jax.ShapeDtypeStruct((LANES, n), F32)],
        grid=(2, p_tiles + s_tiles),
        in_specs=[pl.BlockSpec((SUBLANES, n), lambda ph, t: (0, jnp.minimum(t, p_tiles - 1))),
                  pl.BlockSpec((SUBLANES, n), lambda ph, t: (0, jnp.maximum(t - p_tiles, 0)))],
        out_specs=[pl.BlockSpec((SUBLANES, n), lambda ph, t: (0, t * ph)),
                   pl.BlockSpec((LANES, n), lambda ph, t: (0, 0))],
        scratch_shapes=[pltpu.VMEM((LANES, n), F32), pltpu.VMEM((LANES, n), F32)],
        compiler_params=_cparams(("arbitrary", "arbitrary")),
        name="route",
    )(eid_p, eid_s)


def _row_copy(src, src_row, dst, dst_row, sem):
    s0 = pl.multiple_of(src_row * SUBLANES, SUBLANES)
    d0 = pl.multiple_of(dst_row * SUBLANES, SUBLANES)
    return pltpu.make_async_copy(src.at[pl.ds(s0, SUBLANES), :], dst.at[pl.ds(d0, SUBLANES), :], sem)


def _scatter_rows(dest_ref, v_ref, xpad_ref, sem, tl):
    def start(r, carry):
        for k in range(2):
            _row_copy(v_ref, r, xpad_ref, dest_ref[k, r], sem).start(priority=k)
        return carry

    def wait(r, carry):
        for k in range(2):
            _row_copy(v_ref, r, xpad_ref, dest_ref[k, r], sem).wait()
        return carry

    lax.fori_loop(0, tl, start, 0, unroll=DMA_UNROLL)
    lax.fori_loop(0, tl, wait, 0, unroll=DMA_UNROLL)


def _dispatch_kernel(zflag_ref, destp_ref, dests_ref, vp_ref, vs_ref, xpad_ref, zbuf, sem, zsem,
                     *, tl_p, tl_s, p_tiles, n_blocks):
    blk_rows = _tile_rows(MOE_BM)[0]

    @pl.when(pl.program_id(0) == 0)
    def _():
        zbuf[...] = jnp.zeros(zbuf.shape, F32)

        def zero_block(go):
            def body(b, carry):
                @pl.when(zflag_ref[b] != 0)
                def _():
                    r0 = pl.multiple_of(b * blk_rows, blk_rows)
                    cp = pltpu.make_async_copy(zbuf, xpad_ref.at[pl.ds(r0, blk_rows), :], zsem)
                    if go:
                        cp.start()
                    else:
                        cp.wait()
                return carry
            lax.fori_loop(0, n_blocks, body, 0)

        zero_block(True)
        zero_block(False)

    @pl.when(pl.program_id(0) < p_tiles)
    def _():
        _scatter_rows(destp_ref, vp_ref, xpad_ref, sem, tl_p)

    @pl.when(pl.program_id(0) >= p_tiles)
    def _():
        _scatter_rows(dests_ref, vs_ref, xpad_ref, sem, tl_s)


def _dispatch_call(zflag, dest, v_p, v_s, n_blocks, tl_p, tl_s):
    p_tiles = v_p.shape[0] // _tile_rows(tl_p)[0]
    s_tiles = v_s.shape[0] // _tile_rows(tl_s)[0]
    s_off = p_tiles * tl_p // tl_s
    return pl.pallas_call(
        functools.partial(_dispatch_kernel, tl_p=tl_p, tl_s=tl_s, p_tiles=p_tiles, n_blocks=n_blocks),
        out_shape=jax.ShapeDtypeStruct(_tile_rows(n_blocks * MOE_BM), F32),
        grid_spec=pltpu.PrefetchScalarGridSpec(
            num_scalar_prefetch=1,
            grid=(p_tiles + s_tiles,),
            in_specs=[pl.BlockSpec((SUBLANES, tl_p), lambda t, z: (0, jnp.minimum(t, p_tiles - 1)),
                                   memory_space=pltpu.SMEM),
                      pl.BlockSpec((SUBLANES, tl_s), lambda t, z: (0, s_off + jnp.maximum(t - p_tiles, 0)),
                                   memory_space=pltpu.SMEM),
                      pl.BlockSpec(_tile_rows(tl_p), lambda t, z: (jnp.minimum(t, p_tiles - 1), 0)),
                      pl.BlockSpec(_tile_rows(tl_s), lambda t, z: (jnp.maximum(t - p_tiles, 0), 0))],
            out_specs=pl.BlockSpec(memory_space=pl.ANY),
            scratch_shapes=[pltpu.VMEM(_tile_rows(MOE_BM), F32), pltpu.SemaphoreType.DMA, pltpu.SemaphoreType.DMA]),
        compiler_params=_cparams(("arbitrary",)),
        name="dispatch",
    )(zflag, dest, dest, v_p, v_s)


def _expert_kernel(be_ref, nb_ref, nxt_ref, slot_ref, x_ref, wg_hbm, wu_hbm, wd_hbm, o_ref,
                   wgf, wuf, wdf, wgb, wub, wdb, sems):
    i = pl.program_id(0)

    def weights(e, slot, go):
        for m, (hbm, buf) in enumerate(((wg_hbm, wgf), (wu_hbm, wuf), (wd_hbm, wdf))):
            cp = pltpu.make_async_copy(hbm.at[e], buf.at[slot], sems.at[slot, m])
            if go:
                cp.start()
            else:
                cp.wait()

    @pl.when(i < nb_ref[0])
    def _():
        e = be_ref[i]
        slot = slot_ref[e]

        @pl.when(i == 0)
        def _():
            weights(e, slot, True)

        @pl.when((i == 0) | (e != be_ref[jnp.maximum(i - 1, 0)]))
        def _():
            weights(e, slot, False)
            nxt = nxt_ref[e]

            @pl.when(nxt >= 0)
            def _():
                weights(nxt, 1 - slot, True)

            wgb[...] = wgf[slot].astype(BF16)
            wub[...] = wuf[slot].astype(BF16)
            wdb[...] = wdf[slot].astype(BF16)

        x = _load_row_tiles(x_ref).astype(BF16)
        h = _silu(_dot(x, wgb[...])) * _dot(x, wub[...])
        _store_row_tiles(o_ref, _dot(h.astype(BF16), wdb[...]))


def _expert_call(blk_e, nblk, nxt_e, slot_e, xpad, w_gate, w_up, w_down):
    n_rows = xpad.shape[0] // SUBLANES
    blk = lambda i, be, nb, nx, sl: (jnp.minimum(i, nb[0] - 1), 0)
    hbm = pl.BlockSpec(memory_space=pl.ANY)
    return pl.pallas_call(
        _expert_kernel,
        out_shape=jax.ShapeDtypeStruct(_tile_rows(n_rows), F32),
        grid_spec=pltpu.PrefetchScalarGridSpec(
            num_scalar_prefetch=4,
            grid=(n_rows // MOE_BM,),
            in_specs=[pl.BlockSpec(_tile_rows(MOE_BM), blk), hbm, hbm, hbm],
            out_specs=pl.BlockSpec(_tile_rows(MOE_BM), blk),
            scratch_shapes=[pltpu.VMEM((2, D, D_FF), F32), pltpu.VMEM((2, D, D_FF), F32), pltpu.VMEM((2, D_FF, D), F32),
                            pltpu.VMEM((D, D_FF), BF16), pltpu.VMEM((D, D_FF), BF16), pltpu.VMEM((D_FF, D), BF16),
                            pltpu.SemaphoreType.DMA((2, 3))]),
        input_output_aliases={4: 0},
        compiler_params=_cparams(("arbitrary",)),
        name="expert",
    )(blk_e, nblk, nxt_e, slot_e, xpad, w_gate, w_up, w_down)


def _combine_kernel(dest_ref, gw_ref, x1_ref, g2_ref, l2g_ref, l2b_ref, ypad_ref, y_ref, ybuf, sems,
                    *, tok_off, k_stride, tl, nl, per_batch):
    i = pl.program_id(0)

    def gather(tile, slot, go):
        base = tok_off + tile * tl

        def body(r, carry):
            for k in range(2):
                cp = _row_copy(ypad_ref, dest_ref[k * k_stride + base + r], ybuf.at[slot, k], r, sems.at[slot])
                if go:
                    cp.start(priority=k)
                else:
                    cp.wait()
            return carry

        lax.fori_loop(0, tl, body, 0, unroll=DMA_UNROLL)

    @pl.when(i == 0)
    def _():
        gather(0, 0, True)

    @pl.when(i + 1 < pl.num_programs(0))
    def _():
        gather(i + 1, (i + 1) % 2, True)

    slot = i % 2
    gather(i, slot, False)
    gw_t = jnp.concatenate([gw_ref[...], jnp.zeros((LANES - SUBLANES, tl), F32)], axis=0).T
    f = _load_row_tiles(ybuf.at[slot, 0]) * gw_t[:, 0:1] + _load_row_tiles(ybuf.at[slot, 1]) * gw_t[:, 1:2]
    g2 = _mod_rows(g2_ref, per_batch, i // nl)
    y_ref[...] = _layer_norm(ALPHA * x1_ref[...] + g2 * f) * l2g_ref[...] + l2b_ref[...]


def _combine_call(dest, gw, x1, mod, l2g, l2b, ypad, tok_off, tl):
    nb, L, _ = x1.shape
    nl = L // tl
    row = lambda i, d: (i // nl, i % nl, 0)
    vec = lambda i, d: (0, 0)
    g2_spec = mod.spec(5, tl, lambda i, d: i // nl, lambda i, d: i % nl)
    return pl.pallas_call(
        functools.partial(_combine_kernel, tok_off=tok_off, k_stride=dest.shape[1], tl=tl, nl=nl,
                          per_batch=mod.per_batch),
        out_shape=jax.ShapeDtypeStruct((nb, L, D), F32),
        grid_spec=pltpu.PrefetchScalarGridSpec(
            num_scalar_prefetch=1,
            grid=(nb * nl,),
            in_specs=[pl.BlockSpec((SUBLANES, tl), lambda i, d: (0, i)),
                      pl.BlockSpec((None, tl, D), row), g2_spec,
                      pl.BlockSpec((1, D), vec), pl.BlockSpec((1, D), vec),
                      pl.BlockSpec(memory_space=pl.ANY)],
            out_specs=pl.BlockSpec((None, tl, D), row),
            scratch_shapes=[pltpu.VMEM((2, 2) + _tile_rows(tl), F32), pltpu.SemaphoreType.DMA((2,))]),
        compiler_params=_cparams(("arbitrary",)),
        name="combine",
    )(dest[:2].reshape(-1), gw, x1, mod.table, l2g, l2b, ypad)


def _block_diag(w):
    nh, blk, _ = w.shape
    w4 = w.reshape(nh // LRU_PACK, LRU_PACK, blk, blk)
    eye = jnp.eye(LRU_PACK, dtype=w.dtype)
    out = jnp.einsum("gaij,ab->gaibj", w4, eye)
    return out.reshape(nh // LRU_PACK, LRU_PACK * blk, LRU_PACK * blk).astype(BF16)


def _row(v):
    return v.reshape(1, -1).astype(F32)


def _pad_lanes(v):
    return jnp.pad(v.reshape(1, -1).astype(F32), ((0, 0), (0, LANES - v.shape[-1])))


def kernel(x_prompt, x_sample, c_prompt, c_sample, state_lru_conv, state_lru_h, state_ssd_conv, state_ssd, w_ada, b_ada, w_in, lru_conv_w, lru_conv_b, lru_wa, lru_ba, lru_wx, lru_bx, lru_lambda, lru_norm_g, ssd_conv_w, ssd_conv_b, ssd_dt_bias, ssd_a_log, ssd_d, ssd_norm_g, w_out, ln1_g, ln1_b, w_rg, b_rg, w_re, b_re, w_gate, w_up, w_down, ln2_g, ln2_b):
    assert w_ada.shape[0] == DEPTH == 1
    nbp, seq, _ = x_prompt.shape
    nbs = x_sample.shape[0]
    t_p = nbp * seq
    t_all = t_p + nbs
    assert x_sample.shape[1] == 1 and t_p % ROUTE_TILE == 0 and nbs % TOK_TILE == 0 and nbp == SUBLANES

    c_rows = -(-(nbp + nbs) // 16) * 16
    c_all = jnp.pad(jnp.concatenate([c_sample, c_prompt], axis=0), ((0, c_rows - nbp - nbs), (0, 0)))
    w_in0 = w_in[0]
    o_z, o_xbc, o_dt = 2 * LRU_W, 2 * LRU_W + SSD_INNER, 2 * LRU_W + SSD_INNER + SSD_CONV_DIM
    w_in_t = jnp.swapaxes(w_in0, 0, 1)
    *ws_p, w_dt_t = _wcast_call(w_in_t, ((0, o_z), (o_z, o_xbc), (o_xbc, o_dt)), 512)
    w_dt = jnp.swapaxes(w_dt_t, 0, 1)
    ws_s = ws_p + [jnp.repeat(w_dt, SSD_P, axis=1)]
    ws_p = ws_p + [jnp.pad(w_dt, ((0, 0), (0, LANES - SSD_HEADS)))]
    lp = dict(cw=lru_conv_w[0], cb=_row(lru_conv_b[0]), wa=_block_diag(lru_wa[0]), wx=_block_diag(lru_wx[0]),
              ba=_row(lru_ba[0]), bx=_row(lru_bx[0]), lam=_row(lru_lambda[0]), ng=_row(lru_norm_g[0]))
    sp = dict(cw=ssd_conv_w[0], cb=_row(ssd_conv_b[0]), dtb=_pad_lanes(ssd_dt_bias[0]), alog=_pad_lanes(ssd_a_log[0]),
              dexp=_row(jnp.repeat(ssd_d[0], SSD_P)), ng=_row(ssd_norm_g[0]),
              dtbe=_row(jnp.repeat(ssd_dt_bias[0], SSD_P)), aloge=_row(jnp.repeat(ssd_a_log[0], SSD_P)))
    wrt = jnp.zeros((D, LANES), F32).at[:, 0:N_GROUPS].set(w_rg[0]).at[:, GROUP_SIZE:ROUTE_ROWS].set(w_re[0]).astype(BF16)
    brc = jnp.zeros((LANES, 1), F32).at[0:N_GROUPS, 0].set(b_rg[0]).at[GROUP_SIZE:ROUTE_ROWS, 0].set(b_re[0])
    pp = dict(wo=w_out[0].astype(BF16), l1g=_row(ln1_g[0]), l1b=_row(ln1_b[0]), wrt=wrt, brc=brc)

    table = _ada_call(c_all, w_ada[0], _row(b_ada[0]))
    mod_p = _Mod(table, nbs, True)
    mod_s = _Mod(table, 0, False)

    xg, z, xbc, dtr, p_lru_conv, p_ssd_conv = _proj_conv_call(x_prompt, mod_p, ws_p, lp, sp, SEQ_TILE)
    yl, p_lru_h = _lru_call(xg, lp, SEQ_TILE)
    ys, p_ssd = _ssd_call(xbc, z, dtr, sp, SEQ_TILE // SSD_Q)
    x1_p, v_p, eid_p, gw_p = _post_lag_call(yl, ys, x_prompt, mod_p, pp, SEQ_TILE)

    xs3 = x_sample.reshape(1, nbs, D)
    xg_s, z_s, xbc_s, dte_s = _proj_call(xs3, mod_s, ws_s, nbs)
    xg_s, z_s, xbc_s, dte_s = xg_s[0], z_s[0], xbc_s[0], dte_s[0]
    yl_s, h_s, xs_s, bm_s, cm_s, xdtt, dect, s_lru_conv, s_ssd_conv = _srow_call(
        xg_s, xbc_s, dte_s, jnp.swapaxes(state_lru_conv[0], 0, 1), state_lru_h[0],
        jnp.swapaxes(state_ssd_conv[0], 0, 1), lp, sp)
    s_new, ys_s = _sstate_call(state_ssd[0].reshape(nbs, SSD_INNER, SSD_N), xdtt, dect,
                               bm_s, cm_s, xs_s, z_s, sp["dexp"], sp["ng"], SUBLANES)
    x1_s, v_s, eid_s, gw_s = _post_call(yl_s[None], ys_s[None], xs3, mod_s, pp, nbs)

    s_pad = -nbs % ROUTE_TILE
    dest, counts = _route_call(eid_p, jnp.pad(eid_s, ((0, 0), (0, s_pad)), constant_values=-1))
    counts = counts[:N_EXPERTS, 0].astype(I32)
    n_blocks = -(-(2 * t_all) // MOE_BM) + N_EXPERTS
    pend = jnp.cumsum(((counts + MOE_BM - 1) // MOE_BM) * MOE_BM)
    blk_start = jnp.arange(n_blocks, dtype=I32) * MOE_BM
    blk_e = jnp.minimum(jnp.sum((pend[None, :] <= blk_start[:, None]).astype(I32), axis=1), N_EXPERTS - 1)
    nblk = (pend[-1:] // MOE_BM).astype(I32)
    blk_ids = jnp.arange(n_blocks, dtype=I32)
    last_of_expert = jnp.any(((pend // MOE_BM - 1)[None, :] == blk_ids[:, None]) & (counts > 0)[None, :], axis=1)
    zflag = (last_of_expert | (blk_ids >= nblk[0])).astype(I32)
    used = counts > 0
    eids = jnp.arange(N_EXPERTS, dtype=I32)
    later_used = (eids[None, :] > eids[:, None]) & used[None, :]
    nxt_e = jnp.min(jnp.where(later_used, eids[None, :], N_EXPERTS), axis=1)
    nxt_e = jnp.where(nxt_e == N_EXPERTS, -1, nxt_e).astype(I32)
    slot_e = ((jnp.cumsum(used.astype(I32)) - used.astype(I32)) % 2).astype(I32)
    xpad = _dispatch_call(zflag, dest, v_p, v_s, n_blocks, 4 * TOK_TILE, TOK_TILE)
    ypad = _expert_call(blk_e, nblk, nxt_e, slot_e, xpad, w_gate[0], w_up[0], w_down[0])
    l2g, l2b = _row(ln2_g[0]), _row(ln2_b[0])
    y_p = _combine_call(dest, gw_p, x1_p, mod_p, l2g, l2b, ypad, 0, 4 * TOK_TILE)
    y_s = _combine_call(dest, gw_s, x1_s, mod_s, l2g, l2b, ypad, t_p, TOK_TILE)

    return (y_p, y_s.reshape(nbs, 1, D),
            p_lru_conv[None], p_lru_h.reshape(1, nbp, LRU_W), p_ssd_conv[None],
            p_ssd.reshape(1, nbp, SSD_HEADS, SSD_P, SSD_N),
            jnp.swapaxes(s_lru_conv, 0, 1)[None], h_s[None], jnp.swapaxes(s_ssd_conv, 0, 1)[None],
            s_new.reshape(1, nbs, SSD_HEADS, SSD_P, SSD_N))
```

```python
import functools
import math

import jax
import jax.numpy as jnp
from jax import lax
from jax.experimental import pallas as pl
from jax.experimental.pallas import tpu as pltpu

F32 = jnp.float32
BF16 = jnp.bfloat16
I32 = jnp.int32
HIGHEST = lax.Precision.HIGHEST

D = 1024
DEPTH = 1
CONV_W = 4
LRU_W = D
LRU_HEADS = 16
LRU_C = 8.0
LRU_PACK = 4
LRU_PACK_W = LRU_PACK * (LRU_W // LRU_HEADS)
SSD_INNER = D
SSD_HEADS = 16
SSD_P = SSD_INNER // SSD_HEADS
SSD_GROUPS = 2
SSD_N = 128
SSD_Q = 128
SSD_CONV_DIM = SSD_INNER + 2 * SSD_GROUPS * SSD_N
N_GROUPS = 4
GROUP_SIZE = 8
N_EXPERTS = N_GROUPS * GROUP_SIZE
D_FF = D // 2
N_MOD = 6
LN_EPS = 1e-5
RMS_EPS = 1e-6
ALPHA = (2.0 * DEPTH) ** 0.25

LANES = 128
SUBLANES = 8
VMEM_LIMIT = 56 * 1024 * 1024
SEQ_TILE = 512
TOK_TILE = 128
ROUTE_TILE = 1024
DMA_UNROLL = 8
MOE_BM = 368
LRU_TILE = 1024
ROUTE_ROWS = 40

NT_DIMS = (((1,), (1,)), ((), ()))


def _tile_rows(rows):
    return (rows * SUBLANES, LANES)


def _cparams(sem):
    return pltpu.CompilerParams(dimension_semantics=sem, vmem_limit_bytes=VMEM_LIMIT)


def _sigmoid(x):
    return 0.5 * (jnp.tanh(0.5 * x) + 1.0)


def _silu(x):
    return x * _sigmoid(x)


def _softplus(x):
    return jnp.maximum(x, 0.0) + jnp.log1p(jnp.exp(-jnp.abs(x)))


def _gelu_tanh(x):
    return 0.5 * x * (1.0 + jnp.tanh(math.sqrt(2.0 / math.pi) * (x + 0.044715 * (x * x * x))))


def _layer_norm(x):
    mu = jnp.mean(x, axis=-1, keepdims=True)
    xc = x - mu
    var = jnp.mean(xc * xc, axis=-1, keepdims=True)
    return xc * lax.rsqrt(var + LN_EPS)


def _rms_norm(x, g):
    return x * lax.rsqrt(jnp.mean(x * x, axis=-1, keepdims=True) + RMS_EPS) * g


def _dot(a, b, **kw):
    return jnp.dot(a, b, preferred_element_type=F32, **kw)


def _dot_nt(a, b, **kw):
    return lax.dot_general(a, b, NT_DIMS, preferred_element_type=F32, **kw)


def _store_row_tiles(ref, val):
    rows = val.shape[0]
    for j in range(SUBLANES):
        ref[pl.ds(j, rows, stride=SUBLANES), :] = val[:, j * LANES:(j + 1) * LANES]


def _load_row_tiles(ref):
    rows = ref.shape[0] // SUBLANES
    return jnp.concatenate([ref[pl.ds(j, rows, stride=SUBLANES), :] for j in range(SUBLANES)], axis=1)


def _lru_gates(xc, xb, wa, wx, ba, bx, sp):
    r = _sigmoid(_dot(xb, wa) + ba)
    i = _sigmoid(_dot(xb, wx) + bx)
    log_a = (-LRU_C) * r * sp
    a = jnp.exp(log_a)
    mult = jnp.sqrt(jnp.tanh(-log_a) * (a * a + 1.0))
    return a, mult * (i * xc)


def _ada_kernel(c_ref, w_ref, b_ref, o_ref):
    s = _silu(c_ref[...]).astype(BF16)
    o_ref[...] = _dot(s, w_ref[...].astype(BF16)) + b_ref[...]


def _ada_call(c_all, w_ada, b_ada):
    rows = c_all.shape[0]
    tn = 512
    per_mod = D // tn
    return pl.pallas_call(
        _ada_kernel,
        out_shape=jax.ShapeDtypeStruct((N_MOD, rows, D), F32),
        grid=(N_MOD * per_mod,),
        in_specs=[pl.BlockSpec((rows, D), lambda j: (0, 0)),
                  pl.BlockSpec((D, tn), lambda j: (0, j)),
                  pl.BlockSpec((1, tn), lambda j: (0, j))],
        out_specs=pl.BlockSpec((None, rows, tn), lambda j: (j // per_mod, 0, j % per_mod)),
        compiler_params=_cparams(("arbitrary",)),
        name="ada",
    )(c_all, w_ada, b_ada)


def _wcast_kernel(wt_ref, tail_ref, *o_refs, firsts):
    j = pl.program_id(0)
    *o_refs, tail_out = o_refs
    tail_out[...] = tail_ref[...].astype(BF16)
    wb = wt_ref[...].T.astype(BF16)
    for k, o_ref in enumerate(o_refs):
        last = firsts[k + 1] if k + 1 < len(firsts) else pl.num_programs(0)

        @pl.when((j >= firsts[k]) & (j < last))
        def _():
            o_ref[...] = wb


def _wcast_call(wt, bounds, tc):
    n, k_dim = wt.shape
    assert all(lo % tc == 0 and hi % tc == 0 for lo, hi in bounds)
    assert all(a[1] == b[0] for a, b in zip(bounds, bounds[1:]))
    firsts = tuple((lo - bounds[0][0]) // tc for lo, _ in bounds)
    n_steps = (bounds[-1][1] - bounds[0][0]) // tc
    base = bounds[0][0] // tc
    tail = n - bounds[-1][1]
    assert tail > 0 and bounds[-1][1] % tail == 0

    def out_spec(first, nblk):
        return pl.BlockSpec((k_dim, tc), lambda j: (0, jnp.clip(j - first, 0, nblk - 1)))

    return pl.pallas_call(
        functools.partial(_wcast_kernel, firsts=firsts),
        out_shape=[jax.ShapeDtypeStruct((k_dim, hi - lo), BF16) for lo, hi in bounds]
                  + [jax.ShapeDtypeStruct((tail, k_dim), BF16)],
        grid=(n_steps,),
        in_specs=[pl.BlockSpec((tc, k_dim), lambda j: (base + j, 0)),
                  pl.BlockSpec((tail, k_dim), lambda j: (bounds[-1][1] // tail, 0))],
        out_specs=[out_spec(f, (hi - lo) // tc) for f, (lo, hi) in zip(firsts, bounds)]
                  + [pl.BlockSpec((tail, k_dim), lambda j: (0, 0))],
        compiler_params=_cparams(("arbitrary",)),
        name="wcast",
    )(wt, wt)


class _Mod:
    def __init__(self, table, row0, per_batch):
        self.table, self.row0, self.per_batch = table, row0, per_batch

    def spec(self, k, tl, batch_of, tile_of):
        if self.per_batch:
            blk = self.row0 // SUBLANES
            return pl.BlockSpec((None, SUBLANES, D), lambda *g: (k, blk, 0))
        blk = self.row0 // tl
        return pl.BlockSpec((None, tl, D), lambda *g: (k, blk + tile_of(*g), 0))


def _mod_rows(ref, per_batch, b):
    return ref[pl.ds(b, 1), :] if per_batch else ref[...]


def _proj_kernel(x_ref, sh_ref, sc_ref, w1_ref, w2_ref, w3_ref, w4_ref, o1_ref, o2_ref, o3_ref, o4_ref, *, per_batch):
    b = pl.program_id(0)
    u = _layer_norm(x_ref[...]) * (1.0 + _mod_rows(sc_ref, per_batch, b)) + _mod_rows(sh_ref, per_batch, b)
    ub = u.astype(BF16)
    o1_ref[...] = _dot(ub, w1_ref[...])
    o2_ref[...] = _dot(ub, w2_ref[...])
    o3_ref[...] = _dot(ub, w3_ref[...])
    o4_ref[...] = _dot(ub, w4_ref[...])


def _causal_conv(xpad, x_new, cw_ref, cb_ref, conv_ref, tl):
    xpad[SUBLANES:SUBLANES + tl, :] = x_new
    off = SUBLANES - (CONV_W - 1)
    acc = xpad[off:off + tl, :] * cw_ref[0:1, :]
    for k in range(1, CONV_W):
        acc = acc + xpad[off + k:off + k + tl, :] * cw_ref[k:k + 1, :]
    conv_ref[...] = xpad[SUBLANES + tl - (CONV_W - 1):SUBLANES + tl, :]
    xpad[0:SUBLANES, :] = xpad[tl:tl + SUBLANES, :]
    return acc + cb_ref[...]


def _proj_conv_kernel(x_ref, sh_ref, sc_ref, w1_ref, w2_ref, w3_ref, w4_ref, lcw_ref, lcb_ref, scw_ref, scb_ref,
                      o1_ref, o2_ref, o3_ref, o4_ref, lconv_ref, sconv_ref, lpad, spad, *, tl):
    b = pl.program_id(0)

    @pl.when(pl.program_id(1) == 0)
    def _():
        lpad[0:SUBLANES, :] = jnp.zeros((SUBLANES, LRU_W), F32)
        spad[0:SUBLANES, :] = jnp.zeros((SUBLANES, SSD_CONV_DIM), F32)

    u = _layer_norm(x_ref[...]) * (1.0 + _mod_rows(sc_ref, True, b)) + _mod_rows(sh_ref, True, b)
    ub = u.astype(BF16)
    o1_ref[:, 0:LRU_W] = _causal_conv(lpad, _dot(ub, w1_ref[:, 0:LRU_W]), lcw_ref, lcb_ref, lconv_ref, tl)
    o1_ref[:, LRU_W:2 * LRU_W] = _dot(ub, w1_ref[:, LRU_W:2 * LRU_W])
    o2_ref[...] = _dot(ub, w2_ref[...])
    o3_ref[...] = _silu(_causal_conv(spad, _dot(ub, w3_ref[...]), scw_ref, scb_ref, sconv_ref, tl))
    o4_ref[...] = _dot(ub, w4_ref[...])


def _proj_conv_call(x3, mod, ws, lp, sp, tl):
    nb, L, _ = x3.shape
    widths = [w.shape[1] for w in ws]
    row = lambda b, l: (b, l, 0)
    full = lambda b, l: (0, 0)
    st = lambda b, l: (b, 0, 0)
    bof, tof = (lambda b, l: b), (lambda b, l: l)
    return pl.pallas_call(
        functools.partial(_proj_conv_kernel, tl=tl),
        out_shape=[jax.ShapeDtypeStruct((nb, L, n), F32) for n in widths]
                  + [jax.ShapeDtypeStruct((nb, CONV_W - 1, LRU_W), F32),
                     jax.ShapeDtypeStruct((nb, CONV_W - 1, SSD_CONV_DIM), F32)],
        grid=(nb, L // tl),
        in_specs=[pl.BlockSpec((None, tl, D), row), mod.spec(0, tl, bof, tof), mod.spec(1, tl, bof, tof)]
                 + [pl.BlockSpec((D, n), full) for n in widths]
                 + [pl.BlockSpec((CONV_W, LRU_W), full), pl.BlockSpec((1, LRU_W), full),
                    pl.BlockSpec((CONV_W, SSD_CONV_DIM), full), pl.BlockSpec((1, SSD_CONV_DIM), full)],
        out_specs=[pl.BlockSpec((None, tl, n), row) for n in widths]
                  + [pl.BlockSpec((None, CONV_W - 1, LRU_W), st), pl.BlockSpec((None, CONV_W - 1, SSD_CONV_DIM), st)],
        scratch_shapes=[pltpu.VMEM((tl + SUBLANES, LRU_W), F32), pltpu.VMEM((tl + SUBLANES, SSD_CONV_DIM), F32)],
        compiler_params=_cparams(("parallel", "arbitrary")),
        name="projc",
    )(x3, mod.table, mod.table, *ws, lp["cw"], lp["cb"], sp["cw"], sp["cb"])


def _proj_call(x3, mod, ws, tl):
    nb, L, _ = x3.shape
    widths = [w.shape[1] for w in ws]
    row = lambda b, l: (b, l, 0)
    full = lambda b, l: (0, 0)
    bof, tof = (lambda b, l: b), (lambda b, l: l)
    return pl.pallas_call(
        functools.partial(_proj_kernel, per_batch=mod.per_batch),
        out_shape=[jax.ShapeDtypeStruct((nb, L, n), F32) for n in widths],
        grid=(nb, L // tl),
        in_specs=[pl.BlockSpec((None, tl, D), row), mod.spec(0, tl, bof, tof), mod.spec(1, tl, bof, tof)]
                 + [pl.BlockSpec((D, n), full) for n in widths],
        out_specs=[pl.BlockSpec((None, tl, n), row) for n in widths],
        compiler_params=_cparams(("parallel", "arbitrary")),
        name="proj",
    )(x3, mod.table, mod.table, *ws)


def _scan_segments(a_sk, u_sk, h0, seg):
    stride = seg + 1
    nt = D // LANES
    rows = lambda g: pl.ds(g, SUBLANES, stride=stride)
    shape = (nt, SUBLANES, LANES)

    def local(g, carry):
        h, p = carry
        a = a_sk[:, rows(g), :]
        h = a * h + u_sk[:, rows(g), :]
        u_sk[:, rows(g), :] = h
        return h, a * p

    h_fin, p_fin = lax.fori_loop(0, seg, local, (jnp.zeros(shape, F32), jnp.ones(shape, F32)), unroll=2)
    starts = [jnp.stack([h0[:, c * LANES:(c + 1) * LANES] for c in range(nt)], axis=0)]
    for s in range(SUBLANES):
        starts.append(p_fin[:, s:s + 1, :] * starts[-1] + h_fin[:, s:s + 1, :])
    start = jnp.concatenate(starts[:SUBLANES], axis=1)

    def fixup(g, p):
        p = a_sk[:, rows(g), :] * p
        u_sk[:, rows(g), :] = u_sk[:, rows(g), :] + p * start
        return p

    lax.fori_loop(0, seg, fixup, jnp.ones(shape, F32), unroll=2)
    return jnp.concatenate([starts[SUBLANES][c] for c in range(nt)], axis=1)


def _lru_kernel(x_ref, g_ref, wa_ref, wx_ref, ba_ref, bx_ref, lam_ref, ng_ref,
                y_ref, h_ref, a_sk, u_sk, hc_scr, *, tl):
    @pl.when(pl.program_id(1) == 0)
    def _():
        hc_scr[...] = jnp.zeros((1, D), F32)

    seg = tl // SUBLANES
    sp = _softplus(-lam_ref[...])
    for j in range(LRU_W // LRU_PACK_W):
        cs = slice(j * LRU_PACK_W, (j + 1) * LRU_PACK_W)
        xc = x_ref[:, cs]
        a, u = _lru_gates(xc, xc.astype(BF16), wa_ref[j], wx_ref[j], ba_ref[:, cs], bx_ref[:, cs], sp[:, cs])
        for s in range(SUBLANES):
            for t in range(LRU_PACK_W // LANES):
                c = j * (LRU_PACK_W // LANES) + t
                a_sk[c, s * (seg + 1):s * (seg + 1) + seg, :] = a[s * seg:(s + 1) * seg, t * LANES:(t + 1) * LANES]
                u_sk[c, s * (seg + 1):s * (seg + 1) + seg, :] = u[s * seg:(s + 1) * seg, t * LANES:(t + 1) * LANES]

    h_last = _scan_segments(a_sk, u_sk, hc_scr[...], seg)
    hc_scr[...] = h_last
    h_ref[...] = h_last
    for s in range(SUBLANES):
        h = jnp.concatenate([u_sk[c, s * (seg + 1):s * (seg + 1) + seg, :] for c in range(D // LANES)], axis=1)
        y = h * _gelu_tanh(g_ref[s * seg:(s + 1) * seg, :])
        y_ref[s * seg:(s + 1) * seg, :] = _rms_norm(y, ng_ref[...]).astype(BF16)


def _lru_call(xg, p, tl):
    nb, L, _ = xg.shape
    vec = lambda b, l: (0, 0)
    blk = lambda b, l: (0, 0, 0)
    return pl.pallas_call(
        functools.partial(_lru_kernel, tl=tl),
        out_shape=[jax.ShapeDtypeStruct((nb, L, LRU_W), BF16),
                   jax.ShapeDtypeStruct((nb, 1, LRU_W), F32)],
        grid=(nb, L // tl),
        in_specs=[pl.BlockSpec((None, tl, LRU_W), lambda b, l: (b, l, 0)),
                  pl.BlockSpec((None, tl, LRU_W), lambda b, l: (b, l, 1)),
                  pl.BlockSpec((LRU_W // LRU_PACK_W, LRU_PACK_W, LRU_PACK_W), blk),
                  pl.BlockSpec((LRU_W // LRU_PACK_W, LRU_PACK_W, LRU_PACK_W), blk),
                  pl.BlockSpec((1, LRU_W), vec), pl.BlockSpec((1, LRU_W), vec),
                  pl.BlockSpec((1, LRU_W), vec), pl.BlockSpec((1, LRU_W), vec)],
        out_specs=[pl.BlockSpec((None, tl, LRU_W), lambda b, l: (b, l, 0)),
                   pl.BlockSpec((None, 1, LRU_W), lambda b, l: (b, 0, 0))],
        scratch_shapes=[pltpu.VMEM((LRU_W // LANES, tl + SUBLANES, LANES), F32),
                        pltpu.VMEM((LRU_W // LANES, tl + SUBLANES, LANES), F32),
                        pltpu.VMEM((1, LRU_W), F32)],
        compiler_params=_cparams(("parallel", "arbitrary")),
        name="lru",
    )(xg, xg, p["wa"], p["wx"], p["ba"], p["bx"], p["lam"], p["ng"])


def _ssd_kernel(xc_ref, z_blk, dt_blk, dtb_ref, alog_ref, dexp_ref, ng_ref,
                y_blk, st_ref, st_scr, *tmp, chunks):
    @pl.when(pl.program_id(1) == 0)
    def _():
        st_scr[...] = jnp.zeros((SSD_N, SSD_INNER), F32)

    sets = (tmp[0:3], tmp[3:6])
    for c in range(chunks):
        rows = pl.ds(c * SSD_Q, SSD_Q)
        _ssd_chunk(xc_ref.at[rows, :], z_blk.at[rows, :], dt_blk.at[rows, :], dtb_ref, alog_ref, dexp_ref, ng_ref,
                   y_blk.at[rows, :], st_scr, *sets[c % 2])

    @pl.when(pl.program_id(1) == pl.num_programs(1) - 1)
    def _():
        st_ref[...] = st_scr[...].T


def _ssd_chunk(xc_scr, z_ref, dt_ref, dtb_ref, alog_ref, dexp_ref, ng_ref, y_ref, st_scr, y_scr, ea_scr, ew_scr):
    q = SSD_Q
    dt = _softplus(dt_ref[...] + dtb_ref[...])
    da = dt * (-jnp.exp(alog_ref[...]))
    ri = lax.broadcasted_iota(I32, (q, q), 0)
    ci = lax.broadcasted_iota(I32, (q, q), 1)
    causal = ri >= ci
    acs = _dot(causal.astype(F32), da, precision=HIGHEST)
    acs_t = acs.T
    dt_t = dt.T
    w_end = dt * jnp.exp(acs[q - 1:q, :] - acs)

    half = SSD_INNER // SSD_GROUPS
    hpg = SSD_HEADS // SSD_GROUPS
    cgb, cbs = [], []
    for g in range(SSD_GROUPS):
        bgb = xc_scr[:, SSD_INNER + g * SSD_N:SSD_INNER + (g + 1) * SSD_N].astype(BF16)
        cgb.append(xc_scr[:, SSD_INNER + (SSD_GROUPS + g) * SSD_N:SSD_INNER + (SSD_GROUPS + g + 1) * SSD_N].astype(BF16))
        cbs.append(jnp.where(causal, _dot_nt(cgb[g], bgb), 0.0))

    low = lax.broadcasted_iota(I32, (q, LANES), 1) < SSD_P
    for k in range(SSD_HEADS // 2):
        cs = slice(k * LANES, (k + 1) * LANES)
        cb = cbs[(2 * k) // hpg]
        m, colb, wb = [], [], []
        for h in (2 * k, 2 * k + 1):
            colb.append(jnp.broadcast_to(acs[:, h:h + 1], (q, LANES)))
            wb.append(jnp.broadcast_to(w_end[:, h:h + 1], (q, LANES)))
            seg = colb[-1] - acs_t[h:h + 1, :]
            decay = jnp.exp(jnp.minimum(seg, 0.0))
            m.append((cb * (decay * dt_t[h:h + 1, :])).astype(BF16))
        ea_scr[:, cs] = jnp.where(low, colb[0], colb[1])
        ew_scr[:, cs] = jnp.where(low, wb[0], wb[1])
        xk = xc_scr[:, cs]
        zero = jnp.zeros_like(xk)
        rhs = jnp.concatenate([jnp.where(low, xk, zero), jnp.where(low, zero, xk)], axis=0).astype(BF16)
        y_scr[:, cs] = _dot(jnp.concatenate(m, axis=1), rhs)

    x = xc_scr[:, 0:SSD_INNER]
    ea = ea_scr[...]
    w = (x * ew_scr[...]).astype(BF16)
    s_in = st_scr[...]
    sb = s_in.astype(BF16)
    y_off, c_state = [], []
    for g in range(SSD_GROUPS):
        gs = slice(g * half, (g + 1) * half)
        bgt = xc_scr[:, SSD_INNER + g * SSD_N:SSD_INNER + (g + 1) * SSD_N].T.astype(BF16)
        y_off.append(_dot(cgb[g], sb[:, gs]))
        c_state.append(_dot(bgt, w[:, gs]))
    st_scr[...] = jnp.exp(ea[q - 1:q, :]) * s_in + jnp.concatenate(c_state, axis=1)
    y = y_scr[...] + jnp.concatenate(y_off, axis=1) * jnp.exp(ea) + dexp_ref[...] * x

    yz = y * _silu(z_ref[...])
    y_ref[...] = _rms_norm(yz, ng_ref[...]).astype(BF16)


def _ssd_call(xbc, z, dtr, p, chunks):
    nb, L, _ = xbc.shape
    q = SSD_Q
    tl = q * chunks
    vec = lambda b, c: (0, 0)
    row = lambda b, c: (b, c, 0)
    return pl.pallas_call(
        functools.partial(_ssd_kernel, chunks=chunks),
        out_shape=[jax.ShapeDtypeStruct((nb, L, SSD_INNER), BF16),
                   jax.ShapeDtypeStruct((nb, SSD_INNER, SSD_N), F32)],
        grid=(nb, L // tl),
        in_specs=[pl.BlockSpec((None, tl, SSD_CONV_DIM), row), pl.BlockSpec((None, tl, SSD_INNER), row),
                  pl.BlockSpec((None, tl, LANES), row),
                  pl.BlockSpec((1, LANES), vec), pl.BlockSpec((1, LANES), vec),
                  pl.BlockSpec((1, SSD_INNER), vec), pl.BlockSpec((1, SSD_INNER), vec)],
        out_specs=[pl.BlockSpec((None, tl, SSD_INNER), row),
                   pl.BlockSpec((None, SSD_INNER, SSD_N), lambda b, c: (b, 0, 0))],
        scratch_shapes=[pltpu.VMEM((SSD_N, SSD_INNER), F32)] + [pltpu.VMEM((q, SSD_INNER), F32)] * 6,
        compiler_params=_cparams(("parallel", "arbitrary")),
        name="ssd",
    )(xbc, z, dtr, p["dtb"], p["alog"], p["dexp"], p["ng"])


def _srow_kernel(xg_ref, xbc_ref, dte_ref, lconv_ref, h0_ref, sconv_ref,
                 lcw_ref, lcb_ref, wa_ref, wx_ref, ba_ref, bx_ref, lam_ref, lng_ref,
                 scw_ref, scb_ref, dtbe_ref, aloge_ref,
                 yl_ref, h_ref, xs_ref, bm_ref, cm_ref, xdtt_ref, dect_ref, lcn_ref, scn_ref, xc_scr):
    xl = xg_ref[:, 0:LRU_W]
    acc = lcb_ref[...] + xl * lcw_ref[CONV_W - 1:CONV_W, :]
    for k in range(CONV_W - 1):
        acc = acc + lconv_ref[k] * lcw_ref[k:k + 1, :]
    xc_scr[...] = acc
    for k in range(CONV_W - 2):
        lcn_ref[k] = lconv_ref[k + 1]
        scn_ref[k] = sconv_ref[k + 1]
    lcn_ref[CONV_W - 2] = xl
    scn_ref[CONV_W - 2] = xbc_ref[...]
    sp = _softplus(-lam_ref[...])
    for j in range(LRU_W // LRU_PACK_W):
        cs = slice(j * LRU_PACK_W, (j + 1) * LRU_PACK_W)
        xc = xc_scr[:, cs]
        a, u = _lru_gates(xc, xc.astype(BF16), wa_ref[j], wx_ref[j], ba_ref[:, cs], bx_ref[:, cs], sp[:, cs])
        h_ref[:, cs] = a * h0_ref[:, cs] + u
    y = h_ref[...] * _gelu_tanh(xg_ref[:, LRU_W:2 * LRU_W])
    yl_ref[...] = _rms_norm(y, lng_ref[...]).astype(BF16)

    acc = scb_ref[...] + xbc_ref[...] * scw_ref[CONV_W - 1:CONV_W, :]
    for k in range(CONV_W - 1):
        acc = acc + sconv_ref[k] * scw_ref[k:k + 1, :]
    xc = _silu(acc)
    xs = xc[:, 0:SSD_INNER]
    xs_ref[...] = xs
    bm_ref[...] = xc[:, SSD_INNER:SSD_INNER + SSD_GROUPS * SSD_N]
    cm_ref[...] = xc[:, SSD_INNER + SSD_GROUPS * SSD_N:]
    dt = _softplus(dte_ref[...] + dtbe_ref[...])
    dec = jnp.exp(dt * (-jnp.exp(aloge_ref[...])))
    xdtt = (xs * dt).T
    dect = dec.T
    for j in range(xdtt_ref.shape[0]):
        xdtt_ref[j] = xdtt[:, j * SUBLANES:(j + 1) * SUBLANES]
        dect_ref[j] = dect[:, j * SUBLANES:(j + 1) * SUBLANES]


def _srow_call(xg, xbc, dte, lconv, h0, sconv, lp, sp):
    n = xg.shape[0]
    args = (xg, xbc, dte, lconv, h0, sconv, lp["cw"], lp["cb"], lp["wa"], lp["wx"], lp["ba"], lp["bx"],
            lp["lam"], lp["ng"], sp["cw"], sp["cb"], sp["dtbe"], sp["aloge"])
    full = lambda a: pl.BlockSpec(a.shape, lambda i, nd=a.ndim: (0,) * nd)
    outs = [jax.ShapeDtypeStruct((n, LRU_W), BF16), jax.ShapeDtypeStruct((n, LRU_W), F32),
            jax.ShapeDtypeStruct((n, SSD_INNER), F32), jax.ShapeDtypeStruct((n, SSD_GROUPS * SSD_N), F32),
            jax.ShapeDtypeStruct((n, SSD_GROUPS * SSD_N), F32),
            jax.ShapeDtypeStruct((n // SUBLANES, SSD_INNER, SUBLANES), F32),
            jax.ShapeDtypeStruct((n // SUBLANES, SSD_INNER, SUBLANES), F32),
            jax.ShapeDtypeStruct((CONV_W - 1, n, LRU_W), F32), jax.ShapeDtypeStruct((CONV_W - 1, n, SSD_CONV_DIM), F32)]
    return pl.pallas_call(
        _srow_kernel,
        out_shape=outs,
        grid=(1,),
        in_specs=[full(a) for a in args],
        out_specs=[pl.BlockSpec(o.shape, lambda i, nd=len(o.shape): (0,) * nd) for o in outs],
        scratch_shapes=[pltpu.VMEM((n, LRU_W), F32)],
        compiler_params=_cparams(("arbitrary",)),
        name="srow",
    )(*args)


def _sstate_kernel(s0_ref, xq_ref, dq_ref, bm_ref, cm_ref, xs_ref, z_ref, dexp_ref, ng_ref,
                   sn_ref, ys_ref, yraw):
    nb = s0_ref.shape[0]
    half = SSD_INNER // SSD_GROUPS
    for bi in range(nb):
        brow = jnp.concatenate(
            [jnp.broadcast_to(bm_ref[bi:bi + 1, g * SSD_N:(g + 1) * SSD_N], (half, SSD_N)) for g in range(SSD_GROUPS)],
            axis=0)
        dec = jnp.concatenate(
            [jnp.broadcast_to(dq_ref[h * SSD_P:h * SSD_P + 1, bi:bi + 1], (SSD_P, SSD_N)) for h in range(SSD_HEADS)],
            axis=0)
        s = dec * s0_ref[bi] + xq_ref[:, bi:bi + 1] * brow
        sn_ref[bi] = s
        sb = s.astype(BF16)
        for g in range(SSD_GROUPS):
            cg = cm_ref[:, g * SSD_N:(g + 1) * SSD_N].astype(BF16)
            res = _dot_nt(cg, sb[g * half:(g + 1) * half, :])
            yraw[bi:bi + 1, g * half:(g + 1) * half] = res[bi:bi + 1, :]
    y = yraw[...] + dexp_ref[...] * xs_ref[...]
    ys_ref[...] = _rms_norm(y * _silu(z_ref[...]), ng_ref[...]).astype(BF16)


def _sstate_call(s0, xq, dq, bm, cm, xs, z, dexp, ng, nb):
    n = s0.shape[0]
    row = lambda i: (i, 0)
    vec = lambda i: (0, 0)
    gn = SSD_GROUPS * SSD_N
    return pl.pallas_call(
        _sstate_kernel,
        out_shape=[jax.ShapeDtypeStruct(s0.shape, F32), jax.ShapeDtypeStruct((n, SSD_INNER), BF16)],
        grid=(n // nb,),
        in_specs=[pl.BlockSpec((nb, SSD_INNER, SSD_N), lambda i: (i, 0, 0)),
                  pl.BlockSpec((None, SSD_INNER, nb), lambda i: (i, 0, 0)),
                  pl.BlockSpec((None, SSD_INNER, nb), lambda i: (i, 0, 0)),
                  pl.BlockSpec((nb, gn), row), pl.BlockSpec((nb, gn), row),
                  pl.BlockSpec((nb, SSD_INNER), row), pl.BlockSpec((nb, SSD_INNER), row),
                  pl.BlockSpec((1, SSD_INNER), vec), pl.BlockSpec((1, SSD_INNER), vec)],
        out_specs=[pl.BlockSpec((nb, SSD_INNER, SSD_N), lambda i: (i, 0, 0)), pl.BlockSpec((nb, SSD_INNER), row)],
        scratch_shapes=[pltpu.VMEM((nb, SSD_INNER), F32)],
        compiler_params=_cparams(("parallel",)),
        name="sstate",
    )(s0, xq, dq, bm, cm, xs, z, dexp, ng)


def _out_proj(yl_ref, ys_ref, wo_ref):
    return _dot(yl_ref[...], wo_ref[0:LRU_W, :]) + _dot(ys_ref[...], wo_ref[LRU_W:LRU_W + SSD_INNER, :])


def _post_kernel(yl_ref, ys_ref, x_ref, g1_ref, sh2_ref, sc2_ref, wo_ref, l1g_ref, l1b_ref, wrt_ref, brc_ref,
                 x1_ref, v_ref, eid_ref, gw_ref, *, per_batch):
    _post_tail(_out_proj(yl_ref, ys_ref, wo_ref), pl.program_id(0), x_ref, g1_ref, sh2_ref, sc2_ref,
               l1g_ref, l1b_ref, wrt_ref, brc_ref, x1_ref, v_ref, eid_ref, gw_ref, per_batch)


def _post_lag_kernel(yl_ref, ys_ref, x_ref, g1_ref, sh2_ref, sc2_ref, wo_ref, l1g_ref, l1b_ref, wrt_ref, brc_ref,
                     x1_ref, v_ref, eid_ref, gw_ref, oa, ob, *, nl):
    s = pl.program_id(0)
    b_lag = jnp.maximum(s - 1, 0) // nl

    @pl.when(s == 0)
    def _():
        ob[...] = jnp.zeros(ob.shape, F32)

    def step(o_new, o_old):
        o_new[...] = _out_proj(yl_ref, ys_ref, wo_ref)
        _post_tail(o_old[...], b_lag, x_ref, g1_ref, sh2_ref, sc2_ref, l1g_ref, l1b_ref, wrt_ref, brc_ref,
                   x1_ref, v_ref, eid_ref, gw_ref, True)

    pl.when(s % 2 == 0)(functools.partial(step, oa, ob))
    pl.when(s % 2 == 1)(functools.partial(step, ob, oa))


def _post_tail(o, b, x_ref, g1_ref, sh2_ref, sc2_ref, l1g_ref, l1b_ref, wrt_ref, brc_ref,
               x1_ref, v_ref, eid_ref, gw_ref, per_batch):
    x1 = _layer_norm(ALPHA * x_ref[...] + _mod_rows(g1_ref, per_batch, b) * o) * l1g_ref[...] + l1b_ref[...]
    x1_ref[...] = x1
    v = _layer_norm(x1) * (1.0 + _mod_rows(sc2_ref, per_batch, b)) + _mod_rows(sh2_ref, per_batch, b)
    _store_row_tiles(v_ref, v)

    lt = _dot(v.astype(BF16), wrt_ref[...]).T + brc_ref[...]
    tl = lt.shape[1]
    row = lax.broadcasted_iota(I32, (GROUP_SIZE, tl), 0).astype(F32)
    big = float(GROUP_SIZE)
    neg = -jnp.inf
    lg = jnp.where(row < N_GROUPS, lt[0:GROUP_SIZE, :], neg)
    gmax = jnp.max(lg, axis=0, keepdims=True)
    gsel = jnp.min(jnp.where(lg == gmax, row, big), axis=0, keepdims=True)
    pg = 1.0 / jnp.sum(jnp.exp(lg - gmax), axis=0, keepdims=True)
    le = lt[GROUP_SIZE:2 * GROUP_SIZE, :]
    for j in range(1, N_GROUPS):
        le = jnp.where(gsel == j, lt[GROUP_SIZE * (j + 1):GROUP_SIZE * (j + 2), :], le)
    m1 = jnp.max(le, axis=0, keepdims=True)
    i1 = jnp.min(jnp.where(le == m1, row, big), axis=0, keepdims=True)
    rest = jnp.where(row == i1, neg, le)
    m2 = jnp.max(rest, axis=0, keepdims=True)
    i2 = jnp.min(jnp.where(rest == m2, row, big), axis=0, keepdims=True)
    e2 = jnp.exp(m2 - m1)
    den = 1.0 + e2
    eid = jnp.where(row == 0, gsel * GROUP_SIZE + i1, jnp.where(row == 1, gsel * GROUP_SIZE + i2, 0.0))
    eid_ref[...] = eid.astype(I32)
    gw_ref[...] = jnp.where(row == 0, pg * (1.0 / den), jnp.where(row == 1, pg * (e2 / den), 0.0))


def _post_call(yl, ys, x3, mod, p, tl):
    nb, L, _ = x3.shape
    nl = L // tl
    row = lambda b, l: (b, l, 0)
    vec = lambda b, l: (0, 0)
    tok = lambda b, l: (b * nl + l, 0)
    tokt = lambda b, l: (0, b * nl + l)
    bof, tof = (lambda b, l: b), (lambda b, l: l)
    return pl.pallas_call(
        functools.partial(_post_kernel, per_batch=mod.per_batch),
        out_shape=[jax.ShapeDtypeStruct((nb, L, D), F32), jax.ShapeDtypeStruct(_tile_rows(nb * L), F32),
                   jax.ShapeDtypeStruct((SUBLANES, nb * L), I32), jax.ShapeDtypeStruct((SUBLANES, nb * L), F32)],
        grid=(nb, nl),
        in_specs=[pl.BlockSpec((None, tl, LRU_W), row), pl.BlockSpec((None, tl, SSD_INNER), row),
                  pl.BlockSpec((None, tl, D), row),
                  mod.spec(2, tl, bof, tof), mod.spec(3, tl, bof, tof), mod.spec(4, tl, bof, tof),
                  pl.BlockSpec((LRU_W + SSD_INNER, D), vec), pl.BlockSpec((1, D), vec), pl.BlockSpec((1, D), vec),
                  pl.BlockSpec((D, LANES), vec), pl.BlockSpec((LANES, 1), vec)],
        out_specs=[pl.BlockSpec((None, tl, D), row), pl.BlockSpec(_tile_rows(tl), tok),
                   pl.BlockSpec((SUBLANES, tl), tokt), pl.BlockSpec((SUBLANES, tl), tokt)],
        compiler_params=_cparams(("parallel", "arbitrary")),
        name="post",
    )(yl, ys, x3, mod.table, mod.table, mod.table, p["wo"], p["l1g"], p["l1b"], p["wrt"], p["brc"])


def _post_lag_call(yl, ys, x3, mod, p, tl):
    nb, L, _ = x3.shape
    nl = L // tl
    n = nb * nl
    cur = lambda s: (jnp.minimum(s, n - 1) // nl, jnp.minimum(s, n - 1) % nl, 0)
    lag = lambda s: (jnp.maximum(s - 1, 0) // nl, jnp.maximum(s - 1, 0) % nl, 0)
    lag_tok = lambda s: (jnp.maximum(s - 1, 0), 0)
    lag_tokt = lambda s: (0, jnp.maximum(s - 1, 0))
    vec = lambda s: (0, 0)
    one = lambda *g: 0
    return pl.pallas_call(
        functools.partial(_post_lag_kernel, nl=nl),
        out_shape=[jax.ShapeDtypeStruct((nb, L, D), F32), jax.ShapeDtypeStruct(_tile_rows(nb * L), F32),
                   jax.ShapeDtypeStruct((SUBLANES, nb * L), I32), jax.ShapeDtypeStruct((SUBLANES, nb * L), F32)],
        grid=(n + 1,),
        in_specs=[pl.BlockSpec((None, tl, LRU_W), cur), pl.BlockSpec((None, tl, SSD_INNER), cur),
                  pl.BlockSpec((None, tl, D), lag),
                  mod.spec(2, tl, one, one), mod.spec(3, tl, one, one), mod.spec(4, tl, one, one),
                  pl.BlockSpec((LRU_W + SSD_INNER, D), vec), pl.BlockSpec((1, D), vec), pl.BlockSpec((1, D), vec),
                  pl.BlockSpec((D, LANES), vec), pl.BlockSpec((LANES, 1), vec)],
        out_specs=[pl.BlockSpec((None, tl, D), lag), pl.BlockSpec(_tile_rows(tl), lag_tok),
                   pl.BlockSpec((SUBLANES, tl), lag_tokt), pl.BlockSpec((SUBLANES, tl), lag_tokt)],
        scratch_shapes=[pltpu.VMEM((tl, D), F32), pltpu.VMEM((tl, D), F32)],
        compiler_params=_cparams(("arbitrary",)),
        name="postl",
    )(yl, ys, x3, mod.table, mod.table, mod.table, p["wo"], p["l1g"], p["l1b"], p["wrt"], p["brc"])


def _route_kernel(eidp_ref, eids_ref, dest_ref, cnt_ref, run, poff, *, p_tiles):
    ph = pl.program_id(0)
    t = pl.program_id(1)
    n = ROUTE_TILE
    eid = jnp.where(t < p_tiles, eidp_ref[...], eids_ref[...])
    rowi = lax.broadcasted_iota(I32, (LANES, n), 0)
    oh0 = rowi == eid[0:1, :]
    oh1 = rowi == eid[1:2, :]
    oh = oh0.astype(F32) + oh1.astype(F32)
    tile_cnt = jnp.sum(oh, axis=1, keepdims=True)

    @pl.when((ph == 0) & (t == 0))
    def _():
        run[...] = jnp.zeros((LANES, n), F32)

    @pl.when(ph == 0)
    def _():
        run[...] = run[...] + tile_cnt

    @pl.when((ph == 1) & (t == 0))
    def _():
        counts = run[...]
        cnt_ref[...] = counts
        ci = counts.astype(I32)
        q = jnp.floor(counts * (1.0 / MOE_BM)).astype(I32)
        rem = ci - q * MOE_BM
        q = q + jnp.where(rem >= MOE_BM, 1, 0) - jnp.where(rem < 0, 1, 0)
        nblk = q + jnp.where(ci - q * MOE_BM > 0, 1, 0)
        r = lax.broadcasted_iota(I32, (LANES, LANES), 0)
        c = lax.broadcasted_iota(I32, (LANES, LANES), 1)
        lower = (r > c).astype(BF16)
        poff[...] = _dot(lower, nblk.astype(F32).astype(BF16)) * float(MOE_BM)
        run[...] = jnp.zeros((LANES, n), F32)

    @pl.when(ph == 1)
    def _():
        r = lax.broadcasted_iota(I32, (n, n), 0)
        c = lax.broadcasted_iota(I32, (n, n), 1)
        before = (r < c).astype(BF16)
        slot = _dot(oh.astype(BF16), before) + run[...] + poff[...]
        d0 = jnp.sum(jnp.where(oh0, slot, 0.0), axis=0, keepdims=True)
        d1 = jnp.sum(jnp.where(oh1, slot, 0.0), axis=0, keepdims=True)
        row = lax.broadcasted_iota(I32, (SUBLANES, n), 0)
        dest_ref[...] = jnp.where(row == 0, d0, jnp.where(row == 1, d1, 0.0)).astype(I32)
        run[...] = run[...] + tile_cnt


def _route_call(eid_p, eid_s):
    n = ROUTE_TILE
    p_tiles = eid_p.shape[1] // n
    s_tiles = eid_s.shape[1] // n
    t_all = (p_tiles + s_tiles) * n
    return pl.pallas_call(
        functools.partial(_route_kernel, p_tiles=p_tiles),
        out_shape=[jax.ShapeDtypeStruct((SUBLANES, t_all), I32), jax.ShapeDtypeStruct((LANES, n), F32)],
        grid=(2, p_tiles + s_tiles),
        in_specs=[pl.BlockSpec((SUBLANES, n), lambda ph, t: (0, jnp.minimum(t, p_tiles - 1))),
                  pl.BlockSpec((SUBLANES, n), lambda ph, t: (0, jnp.maximum(t - p_tiles, 0)))],
        out_specs=[pl.BlockSpec((SUBLANES, n), lambda ph, t: (0, t * ph)),
                   pl.BlockSpec((LANES, n), lambda ph, t: (0, 0))],
        scratch_shapes=[pltpu.VMEM((LANES, n), F32), pltpu.VMEM((LANES, n), F32)],
        compiler_params=_cparams(("arbitrary", "arbitrary")),
        name="route",
    )(eid_p, eid_s)


def _row_copy(src, src_row, dst, dst_row, sem):
    s0 = pl.multiple_of(src_row * SUBLANES, SUBLANES)
    d0 = pl.multiple_of(dst_row * SUBLANES, SUBLANES)
    return pltpu.make_async_copy(src.at[pl.ds(s0, SUBLANES), :], dst.at[pl.ds(d0, SUBLANES), :], sem)


def _scatter_rows(dest_ref, v_ref, xpad_ref, sem, tl):
    def start(r, carry):
        for k in range(2):
            _row_copy(v_ref, r, xpad_ref, dest_ref[k, r], sem).start(priority=k)
        return carry

    def wait(r, carry):
        for k in range(2):
            _row_copy(v_ref, r, xpad_ref, dest_ref[k, r], sem).wait()
        return carry

    lax.fori_loop(0, tl, start, 0, unroll=DMA_UNROLL)
    lax.fori_loop(0, tl, wait, 0, unroll=DMA_UNROLL)


def _dispatch_kernel(zflag_ref, destp_ref, dests_ref, vp_ref, vs_ref, xpad_ref, zbuf, sem, zsem,
                     *, tl_p, tl_s, p_tiles, n_blocks):
    blk_rows = _tile_rows(MOE_BM)[0]

    @pl.when(pl.program_id(0) == 0)
    def _():
        zbuf[...] = jnp.zeros(zbuf.shape, F32)

        def zero_block(go):
            def body(b, carry):
                @pl.when(zflag_ref[b] != 0)
                def _():
                    r0 = pl.multiple_of(b * blk_rows, blk_rows)
                    cp = pltpu.make_async_copy(zbuf, xpad_ref.at[pl.ds(r0, blk_rows), :], zsem)
                    if go:
                        cp.start()
                    else:
                        cp.wait()
                return carry
            lax.fori_loop(0, n_blocks, body, 0)

        zero_block(True)
        zero_block(False)

    @pl.when(pl.program_id(0) < p_tiles)
    def _():
        _scatter_rows(destp_ref, vp_ref, xpad_ref, sem, tl_p)

    @pl.when(pl.program_id(0) >= p_tiles)
    def _():
        _scatter_rows(dests_ref, vs_ref, xpad_ref, sem, tl_s)


def _dispatch_call(zflag, dest, v_p, v_s, n_blocks, tl_p, tl_s):
    p_tiles = v_p.shape[0] // _tile_rows(tl_p)[0]
    s_tiles = v_s.shape[0] // _tile_rows(tl_s)[0]
    s_off = p_tiles * tl_p // tl_s
    return pl.pallas_call(
        functools.partial(_dispatch_kernel, tl_p=tl_p, tl_s=tl_s, p_tiles=p_tiles, n_blocks=n_blocks),
        out_shape=jax.ShapeDtypeStruct(_tile_rows(n_blocks * MOE_BM), F32),
        grid_spec=pltpu.PrefetchScalarGridSpec(
            num_scalar_prefetch=1,
            grid=(p_tiles + s_tiles,),
            in_specs=[pl.BlockSpec((SUBLANES, tl_p), lambda t, z: (0, jnp.minimum(t, p_tiles - 1)),
                                   memory_space=pltpu.SMEM),
                      pl.BlockSpec((SUBLANES, tl_s), lambda t, z: (0, s_off + jnp.maximum(t - p_tiles, 0)),
                                   memory_space=pltpu.SMEM),
                      pl.BlockSpec(_tile_rows(tl_p), lambda t, z: (jnp.minimum(t, p_tiles - 1), 0)),
                      pl.BlockSpec(_tile_rows(tl_s), lambda t, z: (jnp.maximum(t - p_tiles, 0), 0))],
            out_specs=pl.BlockSpec(memory_space=pl.ANY),
            scratch_shapes=[pltpu.VMEM(_tile_rows(MOE_BM), F32), pltpu.SemaphoreType.DMA, pltpu.SemaphoreType.DMA]),
        compiler_params=_cparams(("arbitrary",)),
        name="dispatch",
    )(zflag, dest, dest, v_p, v_s)


def _expert_kernel(be_ref, nb_ref, nxt_ref, slot_ref, x_ref, wg_hbm, wu_hbm, wd_hbm, o_ref,
                   wgf, wuf, wdf, wgb, wub, wdb, sems):
    i = pl.program_id(0)

    def weights(e, slot, go):
        for m, (hbm, buf) in enumerate(((wg_hbm, wgf), (wu_hbm, wuf), (wd_hbm, wdf))):
            cp = pltpu.make_async_copy(hbm.at[e], buf.at[slot], sems.at[slot, m])
            if go:
                cp.start()
            else:
                cp.wait()

    @pl.when(i < nb_ref[0])
    def _():
        e = be_ref[i]
        slot = slot_ref[e]

        @pl.when(i == 0)
        def _():
            weights(e, slot, True)

        @pl.when((i == 0) | (e != be_ref[jnp.maximum(i - 1, 0)]))
        def _():
            weights(e, slot, False)
            nxt = nxt_ref[e]

            @pl.when(nxt >= 0)
            def _():
                weights(nxt, 1 - slot, True)

            wgb[...] = wgf[slot].astype(BF16)
            wub[...] = wuf[slot].astype(BF16)
            wdb[...] = wdf[slot].astype(BF16)

        x = _load_row_tiles(x_ref).astype(BF16)
        h = _silu(_dot(x, wgb[...])) * _dot(x, wub[...])
        _store_row_tiles(o_ref, _dot(h.astype(BF16), wdb[...]))


def _expert_call(blk_e, nblk, nxt_e, slot_e, xpad, w_gate, w_up, w_down):
    n_rows = xpad.shape[0] // SUBLANES
    blk = lambda i, be, nb, nx, sl: (jnp.minimum(i, nb[0] - 1), 0)
    hbm = pl.BlockSpec(memory_space=pl.ANY)
    return pl.pallas_call(
        _expert_kernel,
        out_shape=jax.ShapeDtypeStruct(_tile_rows(n_rows), F32),
        grid_spec=pltpu.PrefetchScalarGridSpec(
            num_scalar_prefetch=4,
            grid=(n_rows // MOE_BM,),
            in_specs=[pl.BlockSpec(_tile_rows(MOE_BM), blk), hbm, hbm, hbm],
            out_specs=pl.BlockSpec(_tile_rows(MOE_BM), blk),
            scratch_shapes=[pltpu.VMEM((2, D, D_FF), F32), pltpu.VMEM((2, D, D_FF), F32), pltpu.VMEM((2, D_FF, D), F32),
                            pltpu.VMEM((D, D_FF), BF16), pltpu.VMEM((D, D_FF), BF16), pltpu.VMEM((D_FF, D), BF16),
                            pltpu.SemaphoreType.DMA((2, 3))]),
        input_output_aliases={4: 0},
        compiler_params=_cparams(("arbitrary",)),
        name="expert",
    )(blk_e, nblk, nxt_e, slot_e, xpad, w_gate, w_up, w_down)


def _combine_kernel(dest_ref, gw_ref, x1_ref, g2_ref, l2g_ref, l2b_ref, ypad_ref, y_ref, ybuf, sems,
                    *, tok_off, k_stride, tl, nl, per_batch):
    i = pl.program_id(0)

    def gather(tile, slot, go):
        base = tok_off + tile * tl

        def body(r, carry):
            for k in range(2):
                cp = _row_copy(ypad_ref, dest_ref[k * k_stride + base + r], ybuf.at[slot, k], r, sems.at[slot])
                if go:
                    cp.start(priority=k)
                else:
                    cp.wait()
            return carry

        lax.fori_loop(0, tl, body, 0, unroll=DMA_UNROLL)

    @pl.when(i == 0)
    def _():
        gather(0, 0, True)

    @pl.when(i + 1 < pl.num_programs(0))
    def _():
        gather(i + 1, (i + 1) % 2, True)

    slot = i % 2
    gather(i, slot, False)
    gw_t = jnp.concatenate([gw_ref[...], jnp.zeros((LANES - SUBLANES, tl), F32)], axis=0).T
    f = _load_row_tiles(ybuf.at[slot, 0]) * gw_t[:, 0:1] + _load_row_tiles(ybuf.at[slot, 1]) * gw_t[:, 1:2]
    g2 = _mod_rows(g2_ref, per_batch, i // nl)
    y_ref[...] = _layer_norm(ALPHA * x1_ref[...] + g2 * f) * l2g_ref[...] + l2b_ref[...]


def _combine_call(dest, gw, x1, mod, l2g, l2b, ypad, tok_off, tl):
    nb, L, _ = x1.shape
    nl = L // tl
    row = lambda i, d: (i // nl, i % nl, 0)
    vec = lambda i, d: (0, 0)
    g2_spec = mod.spec(5, tl, lambda i, d: i // nl, lambda i, d: i % nl)
    return pl.pallas_call(
        functools.partial(_combine_kernel, tok_off=tok_off, k_stride=dest.shape[1], tl=tl, nl=nl,
                          per_batch=mod.per_batch),
        out_shape=jax.ShapeDtypeStruct((nb, L, D), F32),
        grid_spec=pltpu.PrefetchScalarGridSpec(
            num_scalar_prefetch=1,
            grid=(nb * nl,),
            in_specs=[pl.BlockSpec((SUBLANES, tl), lambda i, d: (0, i)),
                      pl.BlockSpec((None, tl, D), row), g2_spec,
                      pl.BlockSpec((1, D), vec), pl.BlockSpec((1, D), vec),
                      pl.BlockSpec(memory_space=pl.ANY)],
            out_specs=pl.BlockSpec((None, tl, D), row),
            scratch_shapes=[pltpu.VMEM((2, 2) + _tile_rows(tl), F32), pltpu.SemaphoreType.DMA((2,))]),
        compiler_params=_cparams(("arbitrary",)),
        name="combine",
    )(dest[:2].reshape(-1), gw, x1, mod.table, l2g, l2b, ypad)


def _block_diag(w):
    nh, blk, _ = w.shape
    w4 = w.reshape(nh // LRU_PACK, LRU_PACK, blk, blk)
    eye = jnp.eye(LRU_PACK, dtype=w.dtype)
    out = jnp.einsum("gaij,ab->gaibj", w4, eye)
    return out.reshape(nh // LRU_PACK, LRU_PACK * blk, LRU_PACK * blk).astype(BF16)


def _row(v):
    return v.reshape(1, -1).astype(F32)


def _pad_lanes(v):
    return jnp.pad(v.reshape(1, -1).astype(F32), ((0, 0), (0, LANES - v.shape[-1])))


def kernel(x_prompt, x_sample, c_prompt, c_sample, state_lru_conv, state_lru_h, state_ssd_conv, state_ssd, w_ada, b_ada, w_in, lru_conv_w, lru_conv_b, lru_wa, lru_ba, lru_wx, lru_bx, lru_lambda, lru_norm_g, ssd_conv_w, ssd_conv_b, ssd_dt_bias, ssd_a_log, ssd_d, ssd_norm_g, w_out, ln1_g, ln1_b, w_rg, b_rg, w_re, b_re, w_gate, w_up, w_down, ln2_g, ln2_b):
    assert w_ada.shape[0] == DEPTH == 1
    nbp, seq, _ = x_prompt.shape
    nbs = x_sample.shape[0]
    t_p = nbp * seq
    t_all = t_p + nbs
    assert x_sample.shape[1] == 1 and t_p % ROUTE_TILE == 0 and nbs % TOK_TILE == 0 and nbp == SUBLANES

    c_rows = -(-(nbp + nbs) // 16) * 16
    c_all = jnp.pad(jnp.concatenate([c_sample, c_prompt], axis=0), ((0, c_rows - nbp - nbs), (0, 0)))
    w_in0 = w_in[0]
    o_z, o_xbc, o_dt = 2 * LRU_W, 2 * LRU_W + SSD_INNER, 2 * LRU_W + SSD_INNER + SSD_CONV_DIM
    w_in_t = jnp.swapaxes(w_in0, 0, 1)
    *ws_p, w_dt_t = _wcast_call(w_in_t, ((0, o_z), (o_z, o_xbc), (o_xbc, o_dt)), 512)
    w_dt = jnp.swapaxes(w_dt_t, 0, 1)
    ws_s = ws_p + [jnp.repeat(w_dt, SSD_P, axis=1)]
    ws_p = ws_p + [jnp.pad(w_dt, ((0, 0), (0, LANES - SSD_HEADS)))]
    lp = dict(cw=lru_conv_w[0], cb=_row(lru_conv_b[0]), wa=_block_diag(lru_wa[0]), wx=_block_diag(lru_wx[0]),
              ba=_row(lru_ba[0]), bx=_row(lru_bx[0]), lam=_row(lru_lambda[0]), ng=_row(lru_norm_g[0]))
    sp = dict(cw=ssd_conv_w[0], cb=_row(ssd_conv_b[0]), dtb=_pad_lanes(ssd_dt_bias[0]), alog=_pad_lanes(ssd_a_log[0]),
              dexp=_row(jnp.repeat(ssd_d[0], SSD_P)), ng=_row(ssd_norm_g[0]),
              dtbe=_row(jnp.repeat(ssd_dt_bias[0], SSD_P)), aloge=_row(jnp.repeat(ssd_a_log[0], SSD_P)))
    wrt = jnp.zeros((D, LANES), F32).at[:, 0:N_GROUPS].set(w_rg[0]).at[:, GROUP_SIZE:ROUTE_ROWS].set(w_re[0]).astype(BF16)
    brc = jnp.zeros((LANES, 1), F32).at[0:N_GROUPS, 0].set(b_rg[0]).at[GROUP_SIZE:ROUTE_ROWS, 0].set(b_re[0])
    pp = dict(wo=w_out[0].astype(BF16), l1g=_row(ln1_g[0]), l1b=_row(ln1_b[0]), wrt=wrt, brc=brc)

    table = _ada_call(c_all, w_ada[0], _row(b_ada[0]))
    mod_p = _Mod(table, nbs, True)
    mod_s = _Mod(table, 0, False)

    xg, z, xbc, dtr, p_lru_conv, p_ssd_conv = _proj_conv_call(x_prompt, mod_p, ws_p, lp, sp, SEQ_TILE)
    yl, p_lru_h = _lru_call(xg, lp, LRU_TILE)
    ys, p_ssd = _ssd_call(xbc, z, dtr, sp, SEQ_TILE // SSD_Q)
    x1_p, v_p, eid_p, gw_p = _post_lag_call(yl, ys, x_prompt, mod_p, pp, SEQ_TILE)

    xs3 = x_sample.reshape(1, nbs, D)
    xg_s, z_s, xbc_s, dte_s = _proj_call(xs3, mod_s, ws_s, nbs)
    xg_s, z_s, xbc_s, dte_s = xg_s[0], z_s[0], xbc_s[0], dte_s[0]
    yl_s, h_s, xs_s, bm_s, cm_s, xdtt, dect, s_lru_conv, s_ssd_conv = _srow_call(
        xg_s, xbc_s, dte_s, jnp.swapaxes(state_lru_conv[0], 0, 1), state_lru_h[0],
        jnp.swapaxes(state_ssd_conv[0], 0, 1), lp, sp)
    s_new, ys_s = _sstate_call(state_ssd[0].reshape(nbs, SSD_INNER, SSD_N), xdtt, dect,
                               bm_s, cm_s, xs_s, z_s, sp["dexp"], sp["ng"], SUBLANES)
    x1_s, v_s, eid_s, gw_s = _post_call(yl_s[None], ys_s[None], xs3, mod_s, pp, nbs)

    s_pad = -nbs % ROUTE_TILE
    dest, counts = _route_call(eid_p, jnp.pad(eid_s, ((0, 0), (0, s_pad)), constant_values=-1))
    counts = counts[:N_EXPERTS, 0].astype(I32)
    n_blocks = -(-(2 * t_all) // MOE_BM) + N_EXPERTS
    pend = jnp.cumsum(((counts + MOE_BM - 1) // MOE_BM) * MOE_BM)
    blk_start = jnp.arange(n_blocks, dtype=I32) * MOE_BM
    blk_e = jnp.minimum(jnp.sum((pend[None, :] <= blk_start[:, None]).astype(I32), axis=1), N_EXPERTS - 1)
    nblk = (pend[-1:] // MOE_BM).astype(I32)
    blk_ids = jnp.arange(n_blocks, dtype=I32)
    last_of_expert = jnp.any(((pend // MOE_BM - 1)[None, :] == blk_ids[:, None]) & (counts > 0)[None, :], axis=1)
    zflag = (last_of_expert | (blk_ids >= nblk[0])).astype(I32)
    used = counts > 0
    eids = jnp.arange(N_EXPERTS, dtype=I32)
    later_used = (eids[None, :] > eids[:, None]) & used[None, :]
    nxt_e = jnp.min(jnp.where(later_used, eids[None, :], N_EXPERTS), axis=1)
    nxt_e = jnp.where(nxt_e == N_EXPERTS, -1, nxt_e).astype(I32)
    slot_e = ((jnp.cumsum(used.astype(I32)) - used.astype(I32)) % 2).astype(I32)
    xpad = _dispatch_call(zflag, dest, v_p, v_s, n_blocks, 4 * TOK_TILE, TOK_TILE)
    ypad = _expert_call(blk_e, nblk, nxt_e, slot_e, xpad, w_gate[0], w_up[0], w_down[0])
    l2g, l2b = _row(ln2_g[0]), _row(ln2_b[0])
    y_p = _combine_call(dest, gw_p, x1_p, mod_p, l2g, l2b, ypad, 0, 4 * TOK_TILE)
    y_s = _combine_call(dest, gw_s, x1_s, mod_s, l2g, l2b, ypad, t_p, TOK_TILE)

    return (y_p, y_s.reshape(nbs, 1, D),
            p_lru_conv[None], p_lru_h.reshape(1, nbp, LRU_W), p_ssd_conv[None],
            p_ssd.reshape(1, nbp, SSD_HEADS, SSD_P, SSD_N),
            jnp.swapaxes(s_lru_conv, 0, 1)[None], h_s[None], jnp.swapaxes(s_ssd_conv, 0, 1)[None],
            s_new.reshape(1, nbs, SSD_HEADS, SSD_P, SSD_N))
```

```python
import functools
import math

import jax
import jax.numpy as jnp
from jax import lax
from jax.experimental import pallas as pl
from jax.experimental.pallas import tpu as pltpu

F32 = jnp.float32
BF16 = jnp.bfloat16
I32 = jnp.int32
HIGHEST = lax.Precision.HIGHEST

D = 1024
DEPTH = 1
CONV_W = 4
LRU_W = D
LRU_HEADS = 16
LRU_C = 8.0
LRU_PACK = 4
LRU_PACK_W = LRU_PACK * (LRU_W // LRU_HEADS)
SSD_INNER = D
SSD_HEADS = 16
SSD_P = SSD_INNER // SSD_HEADS
SSD_GROUPS = 2
SSD_N = 128
SSD_Q = 128
SSD_CONV_DIM = SSD_INNER + 2 * SSD_GROUPS * SSD_N
N_GROUPS = 4
GROUP_SIZE = 8
N_EXPERTS = N_GROUPS * GROUP_SIZE
D_FF = D // 2
N_MOD = 6
LN_EPS = 1e-5
RMS_EPS = 1e-6
ALPHA = (2.0 * DEPTH) ** 0.25

LANES = 128
SUBLANES = 8
VMEM_LIMIT = 56 * 1024 * 1024
SEQ_TILE = 512
TOK_TILE = 128
ROUTE_TILE = 1024
DMA_UNROLL = 8
MOE_BM = 384
LRU_TILE = 1024
SSD_CHUNKS_PER_STEP = 4
ROUTE_ROWS = 40

NT_DIMS = (((1,), (1,)), ((), ()))


def _tile_rows(rows):
    return (rows * SUBLANES, LANES)


def _cparams(sem):
    return pltpu.CompilerParams(dimension_semantics=sem, vmem_limit_bytes=VMEM_LIMIT)


def _sigmoid(x):
    return 0.5 * (jnp.tanh(0.5 * x) + 1.0)


def _silu(x):
    return x * _sigmoid(x)


def _softplus(x):
    return jnp.maximum(x, 0.0) + jnp.log1p(jnp.exp(-jnp.abs(x)))


def _gelu_tanh(x):
    return 0.5 * x * (1.0 + jnp.tanh(math.sqrt(2.0 / math.pi) * (x + 0.044715 * (x * x * x))))


def _layer_norm(x):
    mu = jnp.mean(x, axis=-1, keepdims=True)
    xc = x - mu
    var = jnp.mean(xc * xc, axis=-1, keepdims=True)
    return xc * lax.rsqrt(var + LN_EPS)


def _rms_norm(x, g):
    return x * lax.rsqrt(jnp.mean(x * x, axis=-1, keepdims=True) + RMS_EPS) * g


def _dot(a, b, **kw):
    return jnp.dot(a, b, preferred_element_type=F32, **kw)


def _dot_nt(a, b, **kw):
    return lax.dot_general(a, b, NT_DIMS, preferred_element_type=F32, **kw)


def _store_row_tiles(ref, val):
    rows = val.shape[0]
    for j in range(SUBLANES):
        ref[pl.ds(j, rows, stride=SUBLANES), :] = val[:, j * LANES:(j + 1) * LANES]


def _load_row_tiles(ref):
    rows = ref.shape[0] // SUBLANES
    return jnp.concatenate([ref[pl.ds(j, rows, stride=SUBLANES), :] for j in range(SUBLANES)], axis=1)


def _lru_gates(xc, xb, wa, wx, ba, bx, sp):
    r = _sigmoid(_dot(xb, wa) + ba)
    i = _sigmoid(_dot(xb, wx) + bx)
    log_a = (-LRU_C) * r * sp
    a = jnp.exp(log_a)
    mult = jnp.sqrt(jnp.tanh(-log_a) * (a * a + 1.0))
    return a, mult * (i * xc)


def _ada_kernel(c_ref, w_ref, b_ref, o_ref):
    s = _silu(c_ref[...]).astype(BF16)
    o_ref[...] = _dot(s, w_ref[...].astype(BF16)) + b_ref[...]


def _ada_call(c_all, w_ada, b_ada):
    rows = c_all.shape[0]
    tn = 512
    per_mod = D // tn
    return pl.pallas_call(
        _ada_kernel,
        out_shape=jax.ShapeDtypeStruct((N_MOD, rows, D), F32),
        grid=(N_MOD * per_mod,),
        in_specs=[pl.BlockSpec((rows, D), lambda j: (0, 0)),
                  pl.BlockSpec((D, tn), lambda j: (0, j)),
                  pl.BlockSpec((1, tn), lambda j: (0, j))],
        out_specs=pl.BlockSpec((None, rows, tn), lambda j: (j // per_mod, 0, j % per_mod)),
        compiler_params=_cparams(("arbitrary",)),
        name="ada",
    )(c_all, w_ada, b_ada)


def _wcast_kernel(wt_ref, tail_ref, *o_refs, firsts):
    j = pl.program_id(0)
    *o_refs, tail_out = o_refs
    tail_out[...] = tail_ref[...].astype(BF16)
    wb = wt_ref[...].T.astype(BF16)
    for k, o_ref in enumerate(o_refs):
        last = firsts[k + 1] if k + 1 < len(firsts) else pl.num_programs(0)

        @pl.when((j >= firsts[k]) & (j < last))
        def _():
            o_ref[...] = wb


def _wcast_call(wt, bounds, tc):
    n, k_dim = wt.shape
    assert all(lo % tc == 0 and hi % tc == 0 for lo, hi in bounds)
    assert all(a[1] == b[0] for a, b in zip(bounds, bounds[1:]))
    firsts = tuple((lo - bounds[0][0]) // tc for lo, _ in bounds)
    n_steps = (bounds[-1][1] - bounds[0][0]) // tc
    base = bounds[0][0] // tc
    tail = n - bounds[-1][1]
    assert tail > 0 and bounds[-1][1] % tail == 0

    def out_spec(first, nblk):
        return pl.BlockSpec((k_dim, tc), lambda j: (0, jnp.clip(j - first, 0, nblk - 1)))

    return pl.pallas_call(
        functools.partial(_wcast_kernel, firsts=firsts),
        out_shape=[jax.ShapeDtypeStruct((k_dim, hi - lo), BF16) for lo, hi in bounds]
                  + [jax.ShapeDtypeStruct((tail, k_dim), BF16)],
        grid=(n_steps,),
        in_specs=[pl.BlockSpec((tc, k_dim), lambda j: (base + j, 0)),
                  pl.BlockSpec((tail, k_dim), lambda j: (bounds[-1][1] // tail, 0))],
        out_specs=[out_spec(f, (hi - lo) // tc) for f, (lo, hi) in zip(firsts, bounds)]
                  + [pl.BlockSpec((tail, k_dim), lambda j: (0, 0))],
        compiler_params=_cparams(("arbitrary",)),
        name="wcast",
    )(wt, wt)


class _Mod:
    def __init__(self, table, row0, per_batch):
        self.table, self.row0, self.per_batch = table, row0, per_batch

    def spec(self, k, tl, batch_of, tile_of):
        if self.per_batch:
            blk = self.row0 // SUBLANES
            return pl.BlockSpec((None, SUBLANES, D), lambda *g: (k, blk, 0))
        blk = self.row0 // tl
        return pl.BlockSpec((None, tl, D), lambda *g: (k, blk + tile_of(*g), 0))


def _mod_rows(ref, per_batch, b):
    return ref[pl.ds(b, 1), :] if per_batch else ref[...]


def _proj_kernel(x_ref, sh_ref, sc_ref, w1_ref, w2_ref, w3_ref, w4_ref, o1_ref, o2_ref, o3_ref, o4_ref, *, per_batch):
    b = pl.program_id(0)
    u = _layer_norm(x_ref[...]) * (1.0 + _mod_rows(sc_ref, per_batch, b)) + _mod_rows(sh_ref, per_batch, b)
    ub = u.astype(BF16)
    o1_ref[...] = _dot(ub, w1_ref[...])
    o2_ref[...] = _dot(ub, w2_ref[...])
    o3_ref[...] = _dot(ub, w3_ref[...])
    o4_ref[...] = _dot(ub, w4_ref[...])


def _causal_conv(xpad, x_new, cw_ref, cb_ref, conv_ref, tl):
    xpad[SUBLANES:SUBLANES + tl, :] = x_new
    off = SUBLANES - (CONV_W - 1)
    acc = xpad[off:off + tl, :] * cw_ref[0:1, :]
    for k in range(1, CONV_W):
        acc = acc + xpad[off + k:off + k + tl, :] * cw_ref[k:k + 1, :]
    conv_ref[...] = xpad[SUBLANES + tl - (CONV_W - 1):SUBLANES + tl, :]
    xpad[0:SUBLANES, :] = xpad[tl:tl + SUBLANES, :]
    return acc + cb_ref[...]


def _proj_conv_kernel(x_ref, sh_ref, sc_ref, w1_ref, w2_ref, w3_ref, w4_ref, lcw_ref, lcb_ref, scw_ref, scb_ref,
                      o1_ref, o2_ref, o3_ref, o4_ref, lconv_ref, sconv_ref, lpad, spad, *, tl):
    b = pl.program_id(0)

    @pl.when(pl.program_id(1) == 0)
    def _():
        lpad[0:SUBLANES, :] = jnp.zeros((SUBLANES, LRU_W), F32)
        spad[0:SUBLANES, :] = jnp.zeros((SUBLANES, SSD_CONV_DIM), F32)

    u = _layer_norm(x_ref[...]) * (1.0 + _mod_rows(sc_ref, True, b)) + _mod_rows(sh_ref, True, b)
    ub = u.astype(BF16)
    o1_ref[:, 0:LRU_W] = _causal_conv(lpad, _dot(ub, w1_ref[:, 0:LRU_W]), lcw_ref, lcb_ref, lconv_ref, tl)
    o1_ref[:, LRU_W:2 * LRU_W] = _dot(ub, w1_ref[:, LRU_W:2 * LRU_W])
    o2_ref[...] = _dot(ub, w2_ref[...])
    o3_ref[...] = _silu(_causal_conv(spad, _dot(ub, w3_ref[...]), scw_ref, scb_ref, sconv_ref, tl))
    o4_ref[...] = _dot(ub, w4_ref[...])


def _proj_conv_call(x3, mod, ws, lp, sp, tl):
    nb, L, _ = x3.shape
    widths = [w.shape[1] for w in ws]
    row = lambda b, l: (b, l, 0)
    full = lambda b, l: (0, 0)
    st = lambda b, l: (b, 0, 0)
    bof, tof = (lambda b, l: b), (lambda b, l: l)
    return pl.pallas_call(
        functools.partial(_proj_conv_kernel, tl=tl),
        out_shape=[jax.ShapeDtypeStruct((nb, L, n), F32) for n in widths]
                  + [jax.ShapeDtypeStruct((nb, CONV_W - 1, LRU_W), F32),
                     jax.ShapeDtypeStruct((nb, CONV_W - 1, SSD_CONV_DIM), F32)],
        grid=(nb, L // tl),
        in_specs=[pl.BlockSpec((None, tl, D), row), mod.spec(0, tl, bof, tof), mod.spec(1, tl, bof, tof)]
                 + [pl.BlockSpec((D, n), full) for n in widths]
                 + [pl.BlockSpec((CONV_W, LRU_W), full), pl.BlockSpec((1, LRU_W), full),
                    pl.BlockSpec((CONV_W, SSD_CONV_DIM), full), pl.BlockSpec((1, SSD_CONV_DIM), full)],
        out_specs=[pl.BlockSpec((None, tl, n), row) for n in widths]
                  + [pl.BlockSpec((None, CONV_W - 1, LRU_W), st), pl.BlockSpec((None, CONV_W - 1, SSD_CONV_DIM), st)],
        scratch_shapes=[pltpu.VMEM((tl + SUBLANES, LRU_W), F32), pltpu.VMEM((tl + SUBLANES, SSD_CONV_DIM), F32)],
        compiler_params=_cparams(("parallel", "arbitrary")),
        name="projc",
    )(x3, mod.table, mod.table, *ws, lp["cw"], lp["cb"], sp["cw"], sp["cb"])


def _proj_call(x3, mod, ws, tl):
    nb, L, _ = x3.shape
    widths = [w.shape[1] for w in ws]
    row = lambda b, l: (b, l, 0)
    full = lambda b, l: (0, 0)
    bof, tof = (lambda b, l: b), (lambda b, l: l)
    return pl.pallas_call(
        functools.partial(_proj_kernel, per_batch=mod.per_batch),
        out_shape=[jax.ShapeDtypeStruct((nb, L, n), F32) for n in widths],
        grid=(nb, L // tl),
        in_specs=[pl.BlockSpec((None, tl, D), row), mod.spec(0, tl, bof, tof), mod.spec(1, tl, bof, tof)]
                 + [pl.BlockSpec((D, n), full) for n in widths],
        out_specs=[pl.BlockSpec((None, tl, n), row) for n in widths],
        compiler_params=_cparams(("parallel", "arbitrary")),
        name="proj",
    )(x3, mod.table, mod.table, *ws)


def _scan_segments(a_sk, u_sk, h0, seg):
    stride = seg + 1
    nt = D // LANES
    rows = lambda g: pl.ds(g, SUBLANES, stride=stride)
    shape = (nt, SUBLANES, LANES)

    def local(g, carry):
        h, p = carry
        a = a_sk[:, rows(g), :]
        h = a * h + u_sk[:, rows(g), :]
        u_sk[:, rows(g), :] = h
        return h, a * p

    h_fin, p_fin = lax.fori_loop(0, seg, local, (jnp.zeros(shape, F32), jnp.ones(shape, F32)), unroll=2)
    starts = [jnp.stack([h0[:, c * LANES:(c + 1) * LANES] for c in range(nt)], axis=0)]
    for s in range(SUBLANES):
        starts.append(p_fin[:, s:s + 1, :] * starts[-1] + h_fin[:, s:s + 1, :])
    start = jnp.concatenate(starts[:SUBLANES], axis=1)

    def fixup(g, p):
        p = a_sk[:, rows(g), :] * p
        u_sk[:, rows(g), :] = u_sk[:, rows(g), :] + p * start
        return p

    lax.fori_loop(0, seg, fixup, jnp.ones(shape, F32), unroll=2)
    return jnp.concatenate([starts[SUBLANES][c] for c in range(nt)], axis=1)


def _lru_kernel(x_ref, g_ref, wa_ref, wx_ref, ba_ref, bx_ref, lam_ref, ng_ref,
                y_ref, h_ref, a_sk, u_sk, hc_scr, *, tl):
    @pl.when(pl.program_id(1) == 0)
    def _():
        hc_scr[...] = jnp.zeros((1, D), F32)

    seg = tl // SUBLANES
    sp = _softplus(-lam_ref[...])
    for j in range(LRU_W // LRU_PACK_W):
        cs = slice(j * LRU_PACK_W, (j + 1) * LRU_PACK_W)
        xc = x_ref[:, cs]
        a, u = _lru_gates(xc, xc.astype(BF16), wa_ref[j], wx_ref[j], ba_ref[:, cs], bx_ref[:, cs], sp[:, cs])
        for s in range(SUBLANES):
            for t in range(LRU_PACK_W // LANES):
                c = j * (LRU_PACK_W // LANES) + t
                a_sk[c, s * (seg + 1):s * (seg + 1) + seg, :] = a[s * seg:(s + 1) * seg, t * LANES:(t + 1) * LANES]
                u_sk[c, s * (seg + 1):s * (seg + 1) + seg, :] = u[s * seg:(s + 1) * seg, t * LANES:(t + 1) * LANES]

    h_last = _scan_segments(a_sk, u_sk, hc_scr[...], seg)
    hc_scr[...] = h_last
    h_ref[...] = h_last
    for s in range(SUBLANES):
        h = jnp.concatenate([u_sk[c, s * (seg + 1):s * (seg + 1) + seg, :] for c in range(D // LANES)], axis=1)
        y = h * _gelu_tanh(g_ref[s * seg:(s + 1) * seg, :])
        y_ref[s * seg:(s + 1) * seg, :] = _rms_norm(y, ng_ref[...]).astype(BF16)


def _lru_call(xg, p, tl):
    nb, L, _ = xg.shape
    vec = lambda b, l: (0, 0)
    blk = lambda b, l: (0, 0, 0)
    return pl.pallas_call(
        functools.partial(_lru_kernel, tl=tl),
        out_shape=[jax.ShapeDtypeStruct((nb, L, LRU_W), BF16),
                   jax.ShapeDtypeStruct((nb, 1, LRU_W), F32)],
        grid=(nb, L // tl),
        in_specs=[pl.BlockSpec((None, tl, LRU_W), lambda b, l: (b, l, 0)),
                  pl.BlockSpec((None, tl, LRU_W), lambda b, l: (b, l, 1)),
                  pl.BlockSpec((LRU_W // LRU_PACK_W, LRU_PACK_W, LRU_PACK_W), blk),
                  pl.BlockSpec((LRU_W // LRU_PACK_W, LRU_PACK_W, LRU_PACK_W), blk),
                  pl.BlockSpec((1, LRU_W), vec), pl.BlockSpec((1, LRU_W), vec),
                  pl.BlockSpec((1, LRU_W), vec), pl.BlockSpec((1, LRU_W), vec)],
        out_specs=[pl.BlockSpec((None, tl, LRU_W), lambda b, l: (b, l, 0)),
                   pl.BlockSpec((None, 1, LRU_W), lambda b, l: (b, 0, 0))],
        scratch_shapes=[pltpu.VMEM((LRU_W // LANES, tl + SUBLANES, LANES), F32),
                        pltpu.VMEM((LRU_W // LANES, tl + SUBLANES, LANES), F32),
                        pltpu.VMEM((1, LRU_W), F32)],
        compiler_params=_cparams(("parallel", "arbitrary")),
        name="lru",
    )(xg, xg, p["wa"], p["wx"], p["ba"], p["bx"], p["lam"], p["ng"])


def _ssd_kernel(xc_ref, z_blk, dt_blk, dtb_ref, alog_ref, dexp_ref, ng_ref,
                y_blk, st_ref, st_scr, *tmp, chunks):
    @pl.when(pl.program_id(1) == 0)
    def _():
        st_scr[...] = jnp.zeros((SSD_N, SSD_INNER), F32)

    sets = (tmp[0:3], tmp[3:6])
    for c in range(chunks):
        rows = pl.ds(c * SSD_Q, SSD_Q)
        _ssd_chunk(xc_ref.at[rows, :], z_blk.at[rows, :], dt_blk.at[rows, :], dtb_ref, alog_ref, dexp_ref, ng_ref,
                   y_blk.at[rows, :], st_scr, *sets[c % 2])

    @pl.when(pl.program_id(1) == pl.num_programs(1) - 1)
    def _():
        st_ref[...] = st_scr[...].T


def _ssd_chunk(xc_scr, z_ref, dt_ref, dtb_ref, alog_ref, dexp_ref, ng_ref, y_ref, st_scr, y_scr, ea_scr, ew_scr):
    q = SSD_Q
    dt = _softplus(dt_ref[...] + dtb_ref[...])
    da = dt * (-jnp.exp(alog_ref[...]))
    ri = lax.broadcasted_iota(I32, (q, q), 0)
    ci = lax.broadcasted_iota(I32, (q, q), 1)
    causal = ri >= ci
    acs = _dot(causal.astype(F32), da, precision=HIGHEST)
    acs_t = acs.T
    dt_t = dt.T
    w_end = dt * jnp.exp(acs[q - 1:q, :] - acs)

    half = SSD_INNER // SSD_GROUPS
    hpg = SSD_HEADS // SSD_GROUPS
    cgb, cbs = [], []
    for g in range(SSD_GROUPS):
        bgb = xc_scr[:, SSD_INNER + g * SSD_N:SSD_INNER + (g + 1) * SSD_N].astype(BF16)
        cgb.append(xc_scr[:, SSD_INNER + (SSD_GROUPS + g) * SSD_N:SSD_INNER + (SSD_GROUPS + g + 1) * SSD_N].astype(BF16))
        cbs.append(jnp.where(causal, _dot_nt(cgb[g], bgb), 0.0))

    low = lax.broadcasted_iota(I32, (q, LANES), 1) < SSD_P
    for k in range(SSD_HEADS // 2):
        cs = slice(k * LANES, (k + 1) * LANES)
        cb = cbs[(2 * k) // hpg]
        m, colb, wb = [], [], []
        for h in (2 * k, 2 * k + 1):
            colb.append(jnp.broadcast_to(acs[:, h:h + 1], (q, LANES)))
            wb.append(jnp.broadcast_to(w_end[:, h:h + 1], (q, LANES)))
            seg = colb[-1] - acs_t[h:h + 1, :]
            decay = jnp.exp(jnp.minimum(seg, 0.0))
            m.append((cb * (decay * dt_t[h:h + 1, :])).astype(BF16))
        ea_scr[:, cs] = jnp.where(low, colb[0], colb[1])
        ew_scr[:, cs] = jnp.where(low, wb[0], wb[1])
        xk = xc_scr[:, cs]
        zero = jnp.zeros_like(xk)
        rhs = jnp.concatenate([jnp.where(low, xk, zero), jnp.where(low, zero, xk)], axis=0).astype(BF16)
        y_scr[:, cs] = _dot(jnp.concatenate(m, axis=1), rhs)

    x = xc_scr[:, 0:SSD_INNER]
    ea = ea_scr[...]
    w = (x * ew_scr[...]).astype(BF16)
    s_in = st_scr[...]
    sb = s_in.astype(BF16)
    y_off, c_state = [], []
    for g in range(SSD_GROUPS):
        gs = slice(g * half, (g + 1) * half)
        bgt = xc_scr[:, SSD_INNER + g * SSD_N:SSD_INNER + (g + 1) * SSD_N].T.astype(BF16)
        y_off.append(_dot(cgb[g], sb[:, gs]))
        c_state.append(_dot(bgt, w[:, gs]))
    st_scr[...] = jnp.exp(ea[q - 1:q, :]) * s_in + jnp.concatenate(c_state, axis=1)
    y = y_scr[...] + jnp.concatenate(y_off, axis=1) * jnp.exp(ea) + dexp_ref[...] * x

    yz = y * _silu(z_ref[...])
    y_ref[...] = _rms_norm(yz, ng_ref[...]).astype(BF16)


def _ssd_call(xbc, z, dtr, p, chunks):
    nb, L, _ = xbc.shape
    q = SSD_Q
    tl = q * chunks
    vec = lambda b, c: (0, 0)
    row = lambda b, c: (b, c, 0)
    return pl.pallas_call(
        functools.partial(_ssd_kernel, chunks=chunks),
        out_shape=[jax.ShapeDtypeStruct((nb, L, SSD_INNER), BF16),
                   jax.ShapeDtypeStruct((nb, SSD_INNER, SSD_N), F32)],
        grid=(nb, L // tl),
        in_specs=[pl.BlockSpec((None, tl, SSD_CONV_DIM), row), pl.BlockSpec((None, tl, SSD_INNER), row),
                  pl.BlockSpec((None, tl, LANES), row),
                  pl.BlockSpec((1, LANES), vec), pl.BlockSpec((1, LANES), vec),
                  pl.BlockSpec((1, SSD_INNER), vec), pl.BlockSpec((1, SSD_INNER), vec)],
        out_specs=[pl.BlockSpec((None, tl, SSD_INNER), row),
                   pl.BlockSpec((None, SSD_INNER, SSD_N), lambda b, c: (b, 0, 0))],
        scratch_shapes=[pltpu.VMEM((SSD_N, SSD_INNER), F32)] + [pltpu.VMEM((q, SSD_INNER), F32)] * 6,
        compiler_params=_cparams(("parallel", "arbitrary")),
        name="ssd",
    )(xbc, z, dtr, p["dtb"], p["alog"], p["dexp"], p["ng"])


def _srow_kernel(xg_ref, xbc_ref, dte_ref, lconv_ref, h0_ref, sconv_ref,
                 lcw_ref, lcb_ref, wa_ref, wx_ref, ba_ref, bx_ref, lam_ref, lng_ref,
                 scw_ref, scb_ref, dtbe_ref, aloge_ref,
                 yl_ref, h_ref, xs_ref, bm_ref, cm_ref, xdtt_ref, dect_ref, lcn_ref, scn_ref, xc_scr):
    xl = xg_ref[:, 0:LRU_W]
    acc = lcb_ref[...] + xl * lcw_ref[CONV_W - 1:CONV_W, :]
    for k in range(CONV_W - 1):
        acc = acc + lconv_ref[k] * lcw_ref[k:k + 1, :]
    xc_scr[...] = acc
    for k in range(CONV_W - 2):
        lcn_ref[k] = lconv_ref[k + 1]
        scn_ref[k] = sconv_ref[k + 1]
    lcn_ref[CONV_W - 2] = xl
    scn_ref[CONV_W - 2] = xbc_ref[...]
    sp = _softplus(-lam_ref[...])
    for j in range(LRU_W // LRU_PACK_W):
        cs = slice(j * LRU_PACK_W, (j + 1) * LRU_PACK_W)
        xc = xc_scr[:, cs]
        a, u = _lru_gates(xc, xc.astype(BF16), wa_ref[j], wx_ref[j], ba_ref[:, cs], bx_ref[:, cs], sp[:, cs])
        h_ref[:, cs] = a * h0_ref[:, cs] + u
    y = h_ref[...] * _gelu_tanh(xg_ref[:, LRU_W:2 * LRU_W])
    yl_ref[...] = _rms_norm(y, lng_ref[...]).astype(BF16)

    acc = scb_ref[...] + xbc_ref[...] * scw_ref[CONV_W - 1:CONV_W, :]
    for k in range(CONV_W - 1):
        acc = acc + sconv_ref[k] * scw_ref[k:k + 1, :]
    xc = _silu(acc)
    xs = xc[:, 0:SSD_INNER]
    xs_ref[...] = xs
    bm_ref[...] = xc[:, SSD_INNER:SSD_INNER + SSD_GROUPS * SSD_N]
    cm_ref[...] = xc[:, SSD_INNER + SSD_GROUPS * SSD_N:]
    dt = _softplus(dte_ref[...] + dtbe_ref[...])
    dec = jnp.exp(dt * (-jnp.exp(aloge_ref[...])))
    xdtt = (xs * dt).T
    dect = dec.T
    for j in range(xdtt_ref.shape[0]):
        xdtt_ref[j] = xdtt[:, j * SUBLANES:(j + 1) * SUBLANES]
        dect_ref[j] = dect[:, j * SUBLANES:(j + 1) * SUBLANES]


def _srow_call(xg, xbc, dte, lconv, h0, sconv, lp, sp):
    n = xg.shape[0]
    args = (xg, xbc, dte, lconv, h0, sconv, lp["cw"], lp["cb"], lp["wa"], lp["wx"], lp["ba"], lp["bx"],
            lp["lam"], lp["ng"], sp["cw"], sp["cb"], sp["dtbe"], sp["aloge"])
    full = lambda a: pl.BlockSpec(a.shape, lambda i, nd=a.ndim: (0,) * nd)
    outs = [jax.ShapeDtypeStruct((n, LRU_W), BF16), jax.ShapeDtypeStruct((n, LRU_W), F32),
            jax.ShapeDtypeStruct((n, SSD_INNER), F32), jax.ShapeDtypeStruct((n, SSD_GROUPS * SSD_N), F32),
            jax.ShapeDtypeStruct((n, SSD_GROUPS * SSD_N), F32),
            jax.ShapeDtypeStruct((n // SUBLANES, SSD_INNER, SUBLANES), F32),
            jax.ShapeDtypeStruct((n // SUBLANES, SSD_INNER, SUBLANES), F32),
            jax.ShapeDtypeStruct((CONV_W - 1, n, LRU_W), F32), jax.ShapeDtypeStruct((CONV_W - 1, n, SSD_CONV_DIM), F32)]
    return pl.pallas_call(
        _srow_kernel,
        out_shape=outs,
        grid=(1,),
        in_specs=[full(a) for a in args],
        out_specs=[pl.BlockSpec(o.shape, lambda i, nd=len(o.shape): (0,) * nd) for o in outs],
        scratch_shapes=[pltpu.VMEM((n, LRU_W), F32)],
        compiler_params=_cparams(("arbitrary",)),
        name="srow",
    )(*args)


def _sstate_kernel(s0_ref, xq_ref, dq_ref, bm_ref, cm_ref, xs_ref, z_ref, dexp_ref, ng_ref,
                   sn_ref, ys_ref, yraw):
    nb = s0_ref.shape[0]
    half = SSD_INNER // SSD_GROUPS
    for bi in range(nb):
        brow = jnp.concatenate(
            [jnp.broadcast_to(bm_ref[bi:bi + 1, g * SSD_N:(g + 1) * SSD_N], (half, SSD_N)) for g in range(SSD_GROUPS)],
            axis=0)
        dec = jnp.concatenate(
            [jnp.broadcast_to(dq_ref[h * SSD_P:h * SSD_P + 1, bi:bi + 1], (SSD_P, SSD_N)) for h in range(SSD_HEADS)],
            axis=0)
        s = dec * s0_ref[bi] + xq_ref[:, bi:bi + 1] * brow
        sn_ref[bi] = s
        sb = s.astype(BF16)
        for g in range(SSD_GROUPS):
            cg = cm_ref[:, g * SSD_N:(g + 1) * SSD_N].astype(BF16)
            res = _dot_nt(cg, sb[g * half:(g + 1) * half, :])
            yraw[bi:bi + 1, g * half:(g + 1) * half] = res[bi:bi + 1, :]
    y = yraw[...] + dexp_ref[...] * xs_ref[...]
    ys_ref[...] = _rms_norm(y * _silu(z_ref[...]), ng_ref[...]).astype(BF16)


def _sstate_call(s0, xq, dq, bm, cm, xs, z, dexp, ng, nb):
    n = s0.shape[0]
    row = lambda i: (i, 0)
    vec = lambda i: (0, 0)
    gn = SSD_GROUPS * SSD_N
    return pl.pallas_call(
        _sstate_kernel,
        out_shape=[jax.ShapeDtypeStruct(s0.shape, F32), jax.ShapeDtypeStruct((n, SSD_INNER), BF16)],
        grid=(n // nb,),
        in_specs=[pl.BlockSpec((nb, SSD_INNER, SSD_N), lambda i: (i, 0, 0)),
                  pl.BlockSpec((None, SSD_INNER, nb), lambda i: (i, 0, 0)),
                  pl.BlockSpec((None, SSD_INNER, nb), lambda i: (i, 0, 0)),
                  pl.BlockSpec((nb, gn), row), pl.BlockSpec((nb, gn), row),
                  pl.BlockSpec((nb, SSD_INNER), row), pl.BlockSpec((nb, SSD_INNER), row),
                  pl.BlockSpec((1, SSD_INNER), vec), pl.BlockSpec((1, SSD_INNER), vec)],
        out_specs=[pl.BlockSpec((nb, SSD_INNER, SSD_N), lambda i: (i, 0, 0)), pl.BlockSpec((nb, SSD_INNER), row)],
        scratch_shapes=[pltpu.VMEM((nb, SSD_INNER), F32)],
        compiler_params=_cparams(("parallel",)),
        name="sstate",
    )(s0, xq, dq, bm, cm, xs, z, dexp, ng)


def _out_proj(yl_ref, ys_ref, wo_ref):
    return _dot(yl_ref[...], wo_ref[0:LRU_W, :]) + _dot(ys_ref[...], wo_ref[LRU_W:LRU_W + SSD_INNER, :])


def _post_kernel(yl_ref, ys_ref, x_ref, g1_ref, sh2_ref, sc2_ref, wo_ref, l1g_ref, l1b_ref, wrt_ref, brc_ref,
                 x1_ref, v_ref, eid_ref, gw_ref, *, per_batch):
    _post_tail(_out_proj(yl_ref, ys_ref, wo_ref), pl.program_id(0), x_ref, g1_ref, sh2_ref, sc2_ref,
               l1g_ref, l1b_ref, wrt_ref, brc_ref, x1_ref, v_ref, eid_ref, gw_ref, per_batch)


def _post_lag_kernel(yl_ref, ys_ref, x_ref, g1_ref, sh2_ref, sc2_ref, wo_ref, l1g_ref, l1b_ref, wrt_ref, brc_ref,
                     x1_ref, v_ref, eid_ref, gw_ref, oa, ob, *, nl):
    s = pl.program_id(0)
    b_lag = jnp.maximum(s - 1, 0) // nl

    @pl.when(s == 0)
    def _():
        ob[...] = jnp.zeros(ob.shape, F32)

    def step(o_new, o_old):
        o_new[...] = _out_proj(yl_ref, ys_ref, wo_ref)
        _post_tail(o_old[...], b_lag, x_ref, g1_ref, sh2_ref, sc2_ref, l1g_ref, l1b_ref, wrt_ref, brc_ref,
                   x1_ref, v_ref, eid_ref, gw_ref, True)

    pl.when(s % 2 == 0)(functools.partial(step, oa, ob))
    pl.when(s % 2 == 1)(functools.partial(step, ob, oa))


def _post_tail(o, b, x_ref, g1_ref, sh2_ref, sc2_ref, l1g_ref, l1b_ref, wrt_ref, brc_ref,
               x1_ref, v_ref, eid_ref, gw_ref, per_batch):
    x1 = _layer_norm(ALPHA * x_ref[...] + _mod_rows(g1_ref, per_batch, b) * o) * l1g_ref[...] + l1b_ref[...]
    x1_ref[...] = x1
    v = _layer_norm(x1) * (1.0 + _mod_rows(sc2_ref, per_batch, b)) + _mod_rows(sh2_ref, per_batch, b)
    _store_row_tiles(v_ref, v)

    lt = _dot(v.astype(BF16), wrt_ref[...]).T + brc_ref[...]
    tl = lt.shape[1]
    row = lax.broadcasted_iota(I32, (GROUP_SIZE, tl), 0).astype(F32)
    big = float(GROUP_SIZE)
    neg = -jnp.inf
    lg = jnp.where(row < N_GROUPS, lt[0:GROUP_SIZE, :], neg)
    gmax = jnp.max(lg, axis=0, keepdims=True)
    gsel = jnp.min(jnp.where(lg == gmax, row, big), axis=0, keepdims=True)
    pg = 1.0 / jnp.sum(jnp.exp(lg - gmax), axis=0, keepdims=True)
    le = lt[GROUP_SIZE:2 * GROUP_SIZE, :]
    for j in range(1, N_GROUPS):
        le = jnp.where(gsel == j, lt[GROUP_SIZE * (j + 1):GROUP_SIZE * (j + 2), :], le)
    m1 = jnp.max(le, axis=0, keepdims=True)
    i1 = jnp.min(jnp.where(le == m1, row, big), axis=0, keepdims=True)
    rest = jnp.where(row == i1, neg, le)
    m2 = jnp.max(rest, axis=0, keepdims=True)
    i2 = jnp.min(jnp.where(rest == m2, row, big), axis=0, keepdims=True)
    e2 = jnp.exp(m2 - m1)
    den = 1.0 + e2
    eid = jnp.where(row == 0, gsel * GROUP_SIZE + i1, jnp.where(row == 1, gsel * GROUP_SIZE + i2, 0.0))
    eid_ref[...] = eid.astype(I32)
    gw_ref[...] = jnp.where(row == 0, pg * (1.0 / den), jnp.where(row == 1, pg * (e2 / den), 0.0))


def _post_call(yl, ys, x3, mod, p, tl):
    nb, L, _ = x3.shape
    nl = L // tl
    row = lambda b, l: (b, l, 0)
    vec = lambda b, l: (0, 0)
    tok = lambda b, l: (b * nl + l, 0)
    tokt = lambda b, l: (0, b * nl + l)
    bof, tof = (lambda b, l: b), (lambda b, l: l)
    return pl.pallas_call(
        functools.partial(_post_kernel, per_batch=mod.per_batch),
        out_shape=[jax.ShapeDtypeStruct((nb, L, D), F32), jax.ShapeDtypeStruct(_tile_rows(nb * L), F32),
                   jax.ShapeDtypeStruct((SUBLANES, nb * L), I32), jax.ShapeDtypeStruct((SUBLANES, nb * L), F32)],
        grid=(nb, nl),
        in_specs=[pl.BlockSpec((None, tl, LRU_W), row), pl.BlockSpec((None, tl, SSD_INNER), row),
                  pl.BlockSpec((None, tl, D), row),
                  mod.spec(2, tl, bof, tof), mod.spec(3, tl, bof, tof), mod.spec(4, tl, bof, tof),
                  pl.BlockSpec((LRU_W + SSD_INNER, D), vec), pl.BlockSpec((1, D), vec), pl.BlockSpec((1, D), vec),
                  pl.BlockSpec((D, LANES), vec), pl.BlockSpec((LANES, 1), vec)],
        out_specs=[pl.BlockSpec((None, tl, D), row), pl.BlockSpec(_tile_rows(tl), tok),
                   pl.BlockSpec((SUBLANES, tl), tokt), pl.BlockSpec((SUBLANES, tl), tokt)],
        compiler_params=_cparams(("parallel", "arbitrary")),
        name="post",
    )(yl, ys, x3, mod.table, mod.table, mod.table, p["wo"], p["l1g"], p["l1b"], p["wrt"], p["brc"])


def _post_lag_call(yl, ys, x3, mod, p, tl):
    nb, L, _ = x3.shape
    nl = L // tl
    n = nb * nl
    cur = lambda s: (jnp.minimum(s, n - 1) // nl, jnp.minimum(s, n - 1) % nl, 0)
    lag = lambda s: (jnp.maximum(s - 1, 0) // nl, jnp.maximum(s - 1, 0) % nl, 0)
    lag_tok = lambda s: (jnp.maximum(s - 1, 0), 0)
    lag_tokt = lambda s: (0, jnp.maximum(s - 1, 0))
    vec = lambda s: (0, 0)
    one = lambda *g: 0
    return pl.pallas_call(
        functools.partial(_post_lag_kernel, nl=nl),
        out_shape=[jax.ShapeDtypeStruct((nb, L, D), F32), jax.ShapeDtypeStruct(_tile_rows(nb * L), F32),
                   jax.ShapeDtypeStruct((SUBLANES, nb * L), I32), jax.ShapeDtypeStruct((SUBLANES, nb * L), F32)],
        grid=(n + 1,),
        in_specs=[pl.BlockSpec((None, tl, LRU_W), cur), pl.BlockSpec((None, tl, SSD_INNER), cur),
                  pl.BlockSpec((None, tl, D), lag),
                  mod.spec(2, tl, one, one), mod.spec(3, tl, one, one), mod.spec(4, tl, one, one),
                  pl.BlockSpec((LRU_W + SSD_INNER, D), vec), pl.BlockSpec((1, D), vec), pl.BlockSpec((1, D), vec),
                  pl.BlockSpec((D, LANES), vec), pl.BlockSpec((LANES, 1), vec)],
        out_specs=[pl.BlockSpec((None, tl, D), lag), pl.BlockSpec(_tile_rows(tl), lag_tok),
                   pl.BlockSpec((SUBLANES, tl), lag_tokt), pl.BlockSpec((SUBLANES, tl), lag_tokt)],
        scratch_shapes=[pltpu.VMEM((tl, D), F32), pltpu.VMEM((tl, D), F32)],
        compiler_params=_cparams(("arbitrary",)),
        name="postl",
    )(yl, ys, x3, mod.table, mod.table, mod.table, p["wo"], p["l1g"], p["l1b"], p["wrt"], p["brc"])


def _route_kernel(eidp_ref, eids_ref, dest_ref, cnt_ref, run, poff, *, p_tiles):
    ph = pl.program_id(0)
    t = pl.program_id(1)
    n = ROUTE_TILE
    eid = jnp.where(t < p_tiles, eidp_ref[...], eids_ref[...])
    rowi = lax.broadcasted_iota(I32, (LANES, n), 0)
    oh0 = rowi == eid[0:1, :]
    oh1 = rowi == eid[1:2, :]
    oh = oh0.astype(F32) + oh1.astype(F32)
    tile_cnt = jnp.sum(oh, axis=1, keepdims=True)

    @pl.when((ph == 0) & (t == 0))
    def _():
        run[...] = jnp.zeros((LANES, n), F32)

    @pl.when(ph == 0)
    def _():
        run[...] = run[...] + tile_cnt

    @pl.when((ph == 1) & (t == 0))
    def _():
        counts = run[...]
        cnt_ref[...] = counts
        ci = counts.astype(I32)
        q = jnp.floor(counts * (1.0 / MOE_BM)).astype(I32)
        rem = ci - q * MOE_BM
        q = q + jnp.where(rem >= MOE_BM, 1, 0) - jnp.where(rem < 0, 1, 0)
        nblk = q + jnp.where(ci - q * MOE_BM > 0, 1, 0)
        r = lax.broadcasted_iota(I32, (LANES, LANES), 0)
        c = lax.broadcasted_iota(I32, (LANES, LANES), 1)
        lower = (r > c).astype(BF16)
        poff[...] = _dot(lower, nblk.astype(F32).astype(BF16)) * float(MOE_BM)
        run[...] = jnp.zeros((LANES, n), F32)

    @pl.when(ph == 1)
    def _():
        r = lax.broadcasted_iota(I32, (n, n), 0)
        c = lax.broadcasted_iota(I32, (n, n), 1)
        before = (r < c).astype(BF16)
        slot = _dot(oh.astype(BF16), before) + run[...] + poff[...]
        d0 = jnp.sum(jnp.where(oh0, slot, 0.0), axis=0, keepdims=True)
        d1 = jnp.sum(jnp.where(oh1, slot, 0.0), axis=0, keepdims=True)
        row = lax.broadcasted_iota(I32, (SUBLANES, n), 0)
        dest_ref[...] = jnp.where(row == 0, d0, jnp.where(row == 1, d1, 0.0)).astype(I32)
        run[...] = run[...] + tile_cnt


def _route_call(eid_p, eid_s):
    n = ROUTE_TILE
    p_tiles = eid_p.shape[1] // n
    s_tiles = eid_s.shape[1] // n
    t_all = (p_tiles + s_tiles) * n
    return pl.pallas_call(
        functools.partial(_route_kernel, p_tiles=p_tiles),
        out_shape=[jax.ShapeDtypeStruct((SUBLANES, t_all), I32), jax.ShapeDtypeStruct((LANES, n), F32)],
        grid=(2, p_tiles + s_tiles),
        in_specs=[pl.BlockSpec((SUBLANES, n), lambda ph, t: (0, jnp.minimum(t, p_tiles - 1))),
                  pl.BlockSpec((SUBLANES, n), lambda ph, t: (0, jnp.maximum(t - p_tiles, 0)))],
        out_specs=[pl.BlockSpec((SUBLANES, n), lambda ph, t: (0, t * ph)),
                   pl.BlockSpec((LANES, n), lambda ph, t: (0, 0))],
        scratch_shapes=[pltpu.VMEM((LANES, n), F32), pltpu.VMEM((LANES, n), F32)],
        compiler_params=_cparams(("arbitrary", "arbitrary")),
        name="route",
    )(eid_p, eid_s)


def _row_copy(src, src_row, dst, dst_row, sem):
    s0 = pl.multiple_of(src_row * SUBLANES, SUBLANES)
    d0 = pl.multiple_of(dst_row * SUBLANES, SUBLANES)
    return pltpu.make_async_copy(src.at[pl.ds(s0, SUBLANES), :], dst.at[pl.ds(d0, SUBLANES), :], sem)


def _scatter_rows(dest_ref, v_ref, xpad_ref, sem, tl):
    def start(r, carry):
        for k in range(2):
            _row_copy(v_ref, r, xpad_ref, dest_ref[k, r], sem).start(priority=k)
        return carry

    def wait(r, carry):
        for k in range(2):
            _row_copy(v_ref, r, xpad_ref, dest_ref[k, r], sem).wait()
        return carry

    lax.fori_loop(0, tl, start, 0, unroll=DMA_UNROLL)
    lax.fori_loop(0, tl, wait, 0, unroll=DMA_UNROLL)


def _dispatch_kernel(zflag_ref, destp_ref, dests_ref, vp_ref, vs_ref, xpad_ref, zbuf, sem, zsem,
                     *, tl_p, tl_s, p_tiles, n_blocks):
    blk_rows = _tile_rows(MOE_BM)[0]

    @pl.when(pl.program_id(0) == 0)
    def _():
        zbuf[...] = jnp.zeros(zbuf.shape, F32)

        def zero_block(go):
            def body(b, carry):
                @pl.when(zflag_ref[b] != 0)
                def _():
                    r0 = pl.multiple_of(b * blk_rows, blk_rows)
                    cp = pltpu.make_async_copy(zbuf, xpad_ref.at[pl.ds(r0, blk_rows), :], zsem)
                    if go:
                        cp.start()
                    else:
                        cp.wait()
                return carry
            lax.fori_loop(0, n_blocks, body, 0)

        zero_block(True)
        zero_block(False)

    @pl.when(pl.program_id(0) < p_tiles)
    def _():
        _scatter_rows(destp_ref, vp_ref, xpad_ref, sem, tl_p)

    @pl.when(pl.program_id(0) >= p_tiles)
    def _():
        _scatter_rows(dests_ref, vs_ref, xpad_ref, sem, tl_s)


def _dispatch_call(zflag, dest, v_p, v_s, n_blocks, tl_p, tl_s):
    p_tiles = v_p.shape[0] // _tile_rows(tl_p)[0]
    s_tiles = v_s.shape[0] // _tile_rows(tl_s)[0]
    s_off = p_tiles * tl_p // tl_s
    return pl.pallas_call(
        functools.partial(_dispatch_kernel, tl_p=tl_p, tl_s=tl_s, p_tiles=p_tiles, n_blocks=n_blocks),
        out_shape=jax.ShapeDtypeStruct(_tile_rows(n_blocks * MOE_BM), F32),
        grid_spec=pltpu.PrefetchScalarGridSpec(
            num_scalar_prefetch=1,
            grid=(p_tiles + s_tiles,),
            in_specs=[pl.BlockSpec((SUBLANES, tl_p), lambda t, z: (0, jnp.minimum(t, p_tiles - 1)),
                                   memory_space=pltpu.SMEM),
                      pl.BlockSpec((SUBLANES, tl_s), lambda t, z: (0, s_off + jnp.maximum(t - p_tiles, 0)),
                                   memory_space=pltpu.SMEM),
                      pl.BlockSpec(_tile_rows(tl_p), lambda t, z: (jnp.minimum(t, p_tiles - 1), 0)),
                      pl.BlockSpec(_tile_rows(tl_s), lambda t, z: (jnp.maximum(t - p_tiles, 0), 0))],
            out_specs=pl.BlockSpec(memory_space=pl.ANY),
            scratch_shapes=[pltpu.VMEM(_tile_rows(MOE_BM), F32), pltpu.SemaphoreType.DMA, pltpu.SemaphoreType.DMA]),
        compiler_params=_cparams(("arbitrary",)),
        name="dispatch",
    )(zflag, dest, dest, v_p, v_s)


def _expert_kernel(be_ref, nb_ref, nxt_ref, slot_ref, x_ref, wg_hbm, wu_hbm, wd_hbm, o_ref,
                   wgf, wuf, wdf, wgb, wub, wdb, sems):
    i = pl.program_id(0)

    def weights(e, slot, go):
        for m, (hbm, buf) in enumerate(((wg_hbm, wgf), (wu_hbm, wuf), (wd_hbm, wdf))):
            cp = pltpu.make_async_copy(hbm.at[e], buf.at[slot], sems.at[slot, m])
            if go:
                cp.start()
            else:
                cp.wait()

    @pl.when(i < nb_ref[0])
    def _():
        e = be_ref[i]
        slot = slot_ref[e]

        @pl.when(i == 0)
        def _():
            weights(e, slot, True)

        @pl.when((i == 0) | (e != be_ref[jnp.maximum(i - 1, 0)]))
        def _():
            weights(e, slot, False)
            nxt = nxt_ref[e]

            @pl.when(nxt >= 0)
            def _():
                weights(nxt, 1 - slot, True)

            wgb[...] = wgf[slot].astype(BF16)
            wub[...] = wuf[slot].astype(BF16)
            wdb[...] = wdf[slot].astype(BF16)

        x = _load_row_tiles(x_ref).astype(BF16)
        h = _silu(_dot(x, wgb[...])) * _dot(x, wub[...])
        _store_row_tiles(o_ref, _dot(h.astype(BF16), wdb[...]))


def _expert_call(blk_e, nblk, nxt_e, slot_e, xpad, w_gate, w_up, w_down):
    n_rows = xpad.shape[0] // SUBLANES
    blk = lambda i, be, nb, nx, sl: (jnp.minimum(i, nb[0] - 1), 0)
    hbm = pl.BlockSpec(memory_space=pl.ANY)
    return pl.pallas_call(
        _expert_kernel,
        out_shape=jax.ShapeDtypeStruct(_tile_rows(n_rows), F32),
        grid_spec=pltpu.PrefetchScalarGridSpec(
            num_scalar_prefetch=4,
            grid=(n_rows // MOE_BM,),
            in_specs=[pl.BlockSpec(_tile_rows(MOE_BM), blk), hbm, hbm, hbm],
            out_specs=pl.BlockSpec(_tile_rows(MOE_BM), blk),
            scratch_shapes=[pltpu.VMEM((2, D, D_FF), F32), pltpu.VMEM((2, D, D_FF), F32), pltpu.VMEM((2, D_FF, D), F32),
                            pltpu.VMEM((D, D_FF), BF16), pltpu.VMEM((D, D_FF), BF16), pltpu.VMEM((D_FF, D), BF16),
                            pltpu.SemaphoreType.DMA((2, 3))]),
        input_output_aliases={4: 0},
        compiler_params=_cparams(("arbitrary",)),
        name="expert",
    )(blk_e, nblk, nxt_e, slot_e, xpad, w_gate, w_up, w_down)


def _combine_kernel(dest_ref, gw_ref, x1_ref, g2_ref, l2g_ref, l2b_ref, ypad_ref, y_ref, ybuf, sems,
                    *, tok_off, k_stride, tl, nl, per_batch):
    i = pl.program_id(0)

    def gather(tile, slot, go):
        base = tok_off + tile * tl

        def body(r, carry):
            for k in range(2):
                cp = _row_copy(ypad_ref, dest_ref[k * k_stride + base + r], ybuf.at[slot, k], r, sems.at[slot])
                if go:
                    cp.start(priority=k)
                else:
                    cp.wait()
            return carry

        lax.fori_loop(0, tl, body, 0, unroll=DMA_UNROLL)

    @pl.when(i == 0)
    def _():
        gather(0, 0, True)

    @pl.when(i + 1 < pl.num_programs(0))
    def _():
        gather(i + 1, (i + 1) % 2, True)

    slot = i % 2
    gather(i, slot, False)
    gw_t = jnp.concatenate([gw_ref[...], jnp.zeros((LANES - SUBLANES, tl), F32)], axis=0).T
    f = _load_row_tiles(ybuf.at[slot, 0]) * gw_t[:, 0:1] + _load_row_tiles(ybuf.at[slot, 1]) * gw_t[:, 1:2]
    g2 = _mod_rows(g2_ref, per_batch, i // nl)
    y_ref[...] = _layer_norm(ALPHA * x1_ref[...] + g2 * f) * l2g_ref[...] + l2b_ref[...]


def _combine_call(dest, gw, x1, mod, l2g, l2b, ypad, tok_off, tl):
    nb, L, _ = x1.shape
    nl = L // tl
    row = lambda i, d: (i // nl, i % nl, 0)
    vec = lambda i, d: (0, 0)
    g2_spec = mod.spec(5, tl, lambda i, d: i // nl, lambda i, d: i % nl)
    return pl.pallas_call(
        functools.partial(_combine_kernel, tok_off=tok_off, k_stride=dest.shape[1], tl=tl, nl=nl,
                          per_batch=mod.per_batch),
        out_shape=jax.ShapeDtypeStruct((nb, L, D), F32),
        grid_spec=pltpu.PrefetchScalarGridSpec(
            num_scalar_prefetch=1,
            grid=(nb * nl,),
            in_specs=[pl.BlockSpec((SUBLANES, tl), lambda i, d: (0, i)),
                      pl.BlockSpec((None, tl, D), row), g2_spec,
                      pl.BlockSpec((1, D), vec), pl.BlockSpec((1, D), vec),
                      pl.BlockSpec(memory_space=pl.ANY)],
            out_specs=pl.BlockSpec((None, tl, D), row),
            scratch_shapes=[pltpu.VMEM((2, 2) + _tile_rows(tl), F32), pltpu.SemaphoreType.DMA((2,))]),
        compiler_params=_cparams(("arbitrary",)),
        name="combine",
    )(dest[:2].reshape(-1), gw, x1, mod.table, l2g, l2b, ypad)


def _block_diag(w):
    nh, blk, _ = w.shape
    w4 = w.reshape(nh // LRU_PACK, LRU_PACK, blk, blk)
    eye = jnp.eye(LRU_PACK, dtype=w.dtype)
    out = jnp.einsum("gaij,ab->gaibj", w4, eye)
    return out.reshape(nh // LRU_PACK, LRU_PACK * blk, LRU_PACK * blk).astype(BF16)


def _row(v):
    return v.reshape(1, -1).astype(F32)


def _pad_lanes(v):
    return jnp.pad(v.reshape(1, -1).astype(F32), ((0, 0), (0, LANES - v.shape[-1])))


def kernel(x_prompt, x_sample, c_prompt, c_sample, state_lru_conv, state_lru_h, state_ssd_conv, state_ssd, w_ada, b_ada, w_in, lru_conv_w, lru_conv_b, lru_wa, lru_ba, lru_wx, lru_bx, lru_lambda, lru_norm_g, ssd_conv_w, ssd_conv_b, ssd_dt_bias, ssd_a_log, ssd_d, ssd_norm_g, w_out, ln1_g, ln1_b, w_rg, b_rg, w_re, b_re, w_gate, w_up, w_down, ln2_g, ln2_b):
    assert w_ada.shape[0] == DEPTH == 1
    nbp, seq, _ = x_prompt.shape
    nbs = x_sample.shape[0]
    t_p = nbp * seq
    t_all = t_p + nbs
    assert x_sample.shape[1] == 1 and t_p % ROUTE_TILE == 0 and nbs % TOK_TILE == 0 and nbp == SUBLANES

    c_rows = -(-(nbp + nbs) // 16) * 16
    c_all = jnp.pad(jnp.concatenate([c_sample, c_prompt], axis=0), ((0, c_rows - nbp - nbs), (0, 0)))
    w_in0 = w_in[0]
    o_z, o_xbc, o_dt = 2 * LRU_W, 2 * LRU_W + SSD_INNER, 2 * LRU_W + SSD_INNER + SSD_CONV_DIM
    w_in_t = jnp.swapaxes(w_in0, 0, 1)
    *ws_p, w_dt_t = _wcast_call(w_in_t, ((0, o_z), (o_z, o_xbc), (o_xbc, o_dt)), 512)
    w_dt = jnp.swapaxes(w_dt_t, 0, 1)
    ws_s = ws_p + [jnp.repeat(w_dt, SSD_P, axis=1)]
    ws_p = ws_p + [jnp.pad(w_dt, ((0, 0), (0, LANES - SSD_HEADS)))]
    lp = dict(cw=lru_conv_w[0], cb=_row(lru_conv_b[0]), wa=_block_diag(lru_wa[0]), wx=_block_diag(lru_wx[0]),
              ba=_row(lru_ba[0]), bx=_row(lru_bx[0]), lam=_row(lru_lambda[0]), ng=_row(lru_norm_g[0]))
    sp = dict(cw=ssd_conv_w[0], cb=_row(ssd_conv_b[0]), dtb=_pad_lanes(ssd_dt_bias[0]), alog=_pad_lanes(ssd_a_log[0]),
              dexp=_row(jnp.repeat(ssd_d[0], SSD_P)), ng=_row(ssd_norm_g[0]),
              dtbe=_row(jnp.repeat(ssd_dt_bias[0], SSD_P)), aloge=_row(jnp.repeat(ssd_a_log[0], SSD_P)))
    wrt = jnp.zeros((D, LANES), F32).at[:, 0:N_GROUPS].set(w_rg[0]).at[:, GROUP_SIZE:ROUTE_ROWS].set(w_re[0]).astype(BF16)
    brc = jnp.zeros((LANES, 1), F32).at[0:N_GROUPS, 0].set(b_rg[0]).at[GROUP_SIZE:ROUTE_ROWS, 0].set(b_re[0])
    pp = dict(wo=w_out[0].astype(BF16), l1g=_row(ln1_g[0]), l1b=_row(ln1_b[0]), wrt=wrt, brc=brc)

    table = _ada_call(c_all, w_ada[0], _row(b_ada[0]))
    mod_p = _Mod(table, nbs, True)
    mod_s = _Mod(table, 0, False)

    xg, z, xbc, dtr, p_lru_conv, p_ssd_conv = _proj_conv_call(x_prompt, mod_p, ws_p, lp, sp, SEQ_TILE)
    yl, p_lru_h = _lru_call(xg, lp, LRU_TILE)
    ys, p_ssd = _ssd_call(xbc, z, dtr, sp, SSD_CHUNKS_PER_STEP)
    x1_p, v_p, eid_p, gw_p = _post_lag_call(yl, ys, x_prompt, mod_p, pp, SEQ_TILE)

    xs3 = x_sample.reshape(1, nbs, D)
    xg_s, z_s, xbc_s, dte_s = _proj_call(xs3, mod_s, ws_s, nbs)
    xg_s, z_s, xbc_s, dte_s = xg_s[0], z_s[0], xbc_s[0], dte_s[0]
    yl_s, h_s, xs_s, bm_s, cm_s, xdtt, dect, s_lru_conv, s_ssd_conv = _srow_call(
        xg_s, xbc_s, dte_s, jnp.swapaxes(state_lru_conv[0], 0, 1), state_lru_h[0],
        jnp.swapaxes(state_ssd_conv[0], 0, 1), lp, sp)
    s_new, ys_s = _sstate_call(state_ssd[0].reshape(nbs, SSD_INNER, SSD_N), xdtt, dect,
                               bm_s, cm_s, xs_s, z_s, sp["dexp"], sp["ng"], SUBLANES)
    x1_s, v_s, eid_s, gw_s = _post_call(yl_s[None], ys_s[None], xs3, mod_s, pp, nbs)

    s_pad = -nbs % ROUTE_TILE
    dest, counts = _route_call(eid_p, jnp.pad(eid_s, ((0, 0), (0, s_pad)), constant_values=-1))
    counts = counts[:N_EXPERTS, 0].astype(I32)
    n_blocks = -(-(2 * t_all) // MOE_BM) + N_EXPERTS
    pend = jnp.cumsum(((counts + MOE_BM - 1) // MOE_BM) * MOE_BM)
    blk_start = jnp.arange(n_blocks, dtype=I32) * MOE_BM
    blk_e = jnp.minimum(jnp.sum((pend[None, :] <= blk_start[:, None]).astype(I32), axis=1), N_EXPERTS - 1)
    nblk = (pend[-1:] // MOE_BM).astype(I32)
    blk_ids = jnp.arange(n_blocks, dtype=I32)
    last_of_expert = jnp.any(((pend // MOE_BM - 1)[None, :] == blk_ids[:, None]) & (counts > 0)[None, :], axis=1)
    zflag = (last_of_expert | (blk_ids >= nblk[0])).astype(I32)
    used = counts > 0
    eids = jnp.arange(N_EXPERTS, dtype=I32)
    later_used = (eids[None, :] > eids[:, None]) & used[None, :]
    nxt_e = jnp.min(jnp.where(later_used, eids[None, :], N_EXPERTS), axis=1)
    nxt_e = jnp.where(nxt_e == N_EXPERTS, -1, nxt_e).astype(I32)
    slot_e = ((jnp.cumsum(used.astype(I32)) - used.astype(I32)) % 2).astype(I32)
    xpad = _dispatch_call(zflag, dest, v_p, v_s, n_blocks, 4 * TOK_TILE, TOK_TILE)
    ypad = _expert_call(blk_e, nblk, nxt_e, slot_e, xpad, w_gate[0], w_up[0], w_down[0])
    l2g, l2b = _row(ln2_g[0]), _row(ln2_b[0])
    y_p = _combine_call(dest, gw_p, x1_p, mod_p, l2g, l2b, ypad, 0, 4 * TOK_TILE)
    y_s = _combine_call(dest, gw_s, x1_s, mod_s, l2g, l2b, ypad, t_p, TOK_TILE)

    return (y_p, y_s.reshape(nbs, 1, D),
            p_lru_conv[None], p_lru_h.reshape(1, nbp, LRU_W), p_ssd_conv[None],
            p_ssd.reshape(1, nbp, SSD_HEADS, SSD_P, SSD_N),
            jnp.swapaxes(s_lru_conv, 0, 1)[None], h_s[None], jnp.swapaxes(s_ssd_conv, 0, 1)[None],
            s_new.reshape(1, nbs, SSD_HEADS, SSD_P, SSD_N))
```

```python
import functools
import math

import jax
import jax.numpy as jnp
from jax import lax
from jax.experimental import pallas as pl
from jax.experimental.pallas import tpu as pltpu

F32 = jnp.float32
BF16 = jnp.bfloat16
I32 = jnp.int32
HIGHEST = lax.Precision.HIGHEST

D = 1024
DEPTH = 1
CONV_W = 4
LRU_W = D
LRU_HEADS = 16
LRU_C = 8.0
LRU_PACK = 4
LRU_PACK_W = LRU_PACK * (LRU_W // LRU_HEADS)
SSD_INNER = D
SSD_HEADS = 16
SSD_P = SSD_INNER // SSD_HEADS
SSD_GROUPS = 2
SSD_N = 128
SSD_Q = 128
SSD_CONV_DIM = SSD_INNER + 2 * SSD_GROUPS * SSD_N
N_GROUPS = 4
GROUP_SIZE = 8
N_EXPERTS = N_GROUPS * GROUP_SIZE
D_FF = D // 2
N_MOD = 6
LN_EPS = 1e-5
RMS_EPS = 1e-6
ALPHA = (2.0 * DEPTH) ** 0.25

LANES = 128
SUBLANES = 8
VMEM_LIMIT = 56 * 1024 * 1024
SEQ_TILE = 512
TOK_TILE = 128
ROUTE_TILE = 1024
DMA_UNROLL = 16
MOE_BM = 384
LRU_TILE = 1024
SSD_CHUNKS_PER_STEP = 4
ROUTE_ROWS = 40

NT_DIMS = (((1,), (1,)), ((), ()))


def _tile_rows(rows):
    return (rows * SUBLANES, LANES)


def _cparams(sem):
    return pltpu.CompilerParams(dimension_semantics=sem, vmem_limit_bytes=VMEM_LIMIT)


def _sigmoid(x):
    return 0.5 * (jnp.tanh(0.5 * x) + 1.0)


def _silu(x):
    return x * _sigmoid(x)


def _softplus(x):
    return jnp.maximum(x, 0.0) + jnp.log1p(jnp.exp(-jnp.abs(x)))


def _gelu_tanh(x):
    return 0.5 * x * (1.0 + jnp.tanh(math.sqrt(2.0 / math.pi) * (x + 0.044715 * (x * x * x))))


def _layer_norm(x):
    mu = jnp.mean(x, axis=-1, keepdims=True)
    xc = x - mu
    var = jnp.mean(xc * xc, axis=-1, keepdims=True)
    return xc * lax.rsqrt(var + LN_EPS)


def _rms_norm(x, g):
    return x * lax.rsqrt(jnp.mean(x * x, axis=-1, keepdims=True) + RMS_EPS) * g


def _dot(a, b, **kw):
    return jnp.dot(a, b, preferred_element_type=F32, **kw)


def _dot_nt(a, b, **kw):
    return lax.dot_general(a, b, NT_DIMS, preferred_element_type=F32, **kw)


def _store_row_tiles(ref, val):
    rows = val.shape[0]
    for j in range(SUBLANES):
        ref[pl.ds(j, rows, stride=SUBLANES), :] = val[:, j * LANES:(j + 1) * LANES]


def _load_row_tiles(ref):
    rows = ref.shape[0] // SUBLANES
    return jnp.concatenate([ref[pl.ds(j, rows, stride=SUBLANES), :] for j in range(SUBLANES)], axis=1)


def _lru_gates(xc, xb, wa, wx, ba, bx, sp):
    r = _sigmoid(_dot(xb, wa) + ba)
    i = _sigmoid(_dot(xb, wx) + bx)
    log_a = (-LRU_C) * r * sp
    a = jnp.exp(log_a)
    mult = jnp.sqrt(jnp.tanh(-log_a) * (a * a + 1.0))
    return a, mult * (i * xc)


def _ada_kernel(c_ref, w_ref, b_ref, o_ref):
    s = _silu(c_ref[...]).astype(BF16)
    o_ref[...] = _dot(s, w_ref[...].astype(BF16)) + b_ref[...]


def _ada_call(c_all, w_ada, b_ada):
    rows = c_all.shape[0]
    tn = 512
    per_mod = D // tn
    return pl.pallas_call(
        _ada_kernel,
        out_shape=jax.ShapeDtypeStruct((N_MOD, rows, D), F32),
        grid=(N_MOD * per_mod,),
        in_specs=[pl.BlockSpec((rows, D), lambda j: (0, 0)),
                  pl.BlockSpec((D, tn), lambda j: (0, j)),
                  pl.BlockSpec((1, tn), lambda j: (0, j))],
        out_specs=pl.BlockSpec((None, rows, tn), lambda j: (j // per_mod, 0, j % per_mod)),
        compiler_params=_cparams(("arbitrary",)),
        name="ada",
    )(c_all, w_ada, b_ada)


def _wcast_kernel(wt_ref, tail_ref, *o_refs, firsts):
    j = pl.program_id(0)
    *o_refs, tail_out = o_refs
    tail_out[...] = tail_ref[...].astype(BF16)
    wb = wt_ref[...].T.astype(BF16)
    for k, o_ref in enumerate(o_refs):
        last = firsts[k + 1] if k + 1 < len(firsts) else pl.num_programs(0)

        @pl.when((j >= firsts[k]) & (j < last))
        def _():
            o_ref[...] = wb


def _wcast_call(wt, bounds, tc):
    n, k_dim = wt.shape
    assert all(lo % tc == 0 and hi % tc == 0 for lo, hi in bounds)
    assert all(a[1] == b[0] for a, b in zip(bounds, bounds[1:]))
    firsts = tuple((lo - bounds[0][0]) // tc for lo, _ in bounds)
    n_steps = (bounds[-1][1] - bounds[0][0]) // tc
    base = bounds[0][0] // tc
    tail = n - bounds[-1][1]
    assert tail > 0 and bounds[-1][1] % tail == 0

    def out_spec(first, nblk):
        return pl.BlockSpec((k_dim, tc), lambda j: (0, jnp.clip(j - first, 0, nblk - 1)))

    return pl.pallas_call(
        functools.partial(_wcast_kernel, firsts=firsts),
        out_shape=[jax.ShapeDtypeStruct((k_dim, hi - lo), BF16) for lo, hi in bounds]
                  + [jax.ShapeDtypeStruct((tail, k_dim), BF16)],
        grid=(n_steps,),
        in_specs=[pl.BlockSpec((tc, k_dim), lambda j: (base + j, 0)),
                  pl.BlockSpec((tail, k_dim), lambda j: (bounds[-1][1] // tail, 0))],
        out_specs=[out_spec(f, (hi - lo) // tc) for f, (lo, hi) in zip(firsts, bounds)]
                  + [pl.BlockSpec((tail, k_dim), lambda j: (0, 0))],
        compiler_params=_cparams(("arbitrary",)),
        name="wcast",
    )(wt, wt)


class _Mod:
    def __init__(self, table, row0, per_batch):
        self.table, self.row0, self.per_batch = table, row0, per_batch

    def spec(self, k, tl, batch_of, tile_of):
        if self.per_batch:
            blk = self.row0 // SUBLANES
            return pl.BlockSpec((None, SUBLANES, D), lambda *g: (k, blk, 0))
        blk = self.row0 // tl
        return pl.BlockSpec((None, tl, D), lambda *g: (k, blk + tile_of(*g), 0))


def _mod_rows(ref, per_batch, b):
    return ref[pl.ds(b, 1), :] if per_batch else ref[...]


def _proj_kernel(x_ref, sh_ref, sc_ref, w1_ref, w2_ref, w3_ref, w4_ref, o1_ref, o2_ref, o3_ref, o4_ref, *, per_batch):
    b = pl.program_id(0)
    u = _layer_norm(x_ref[...]) * (1.0 + _mod_rows(sc_ref, per_batch, b)) + _mod_rows(sh_ref, per_batch, b)
    ub = u.astype(BF16)
    o1_ref[...] = _dot(ub, w1_ref[...])
    o2_ref[...] = _dot(ub, w2_ref[...])
    o3_ref[...] = _dot(ub, w3_ref[...])
    o4_ref[...] = _dot(ub, w4_ref[...])


def _causal_conv(xpad, x_new, cw_ref, cb_ref, conv_ref, tl):
    xpad[SUBLANES:SUBLANES + tl, :] = x_new
    off = SUBLANES - (CONV_W - 1)
    acc = xpad[off:off + tl, :] * cw_ref[0:1, :]
    for k in range(1, CONV_W):
        acc = acc + xpad[off + k:off + k + tl, :] * cw_ref[k:k + 1, :]
    conv_ref[...] = xpad[SUBLANES + tl - (CONV_W - 1):SUBLANES + tl, :]
    xpad[0:SUBLANES, :] = xpad[tl:tl + SUBLANES, :]
    return acc + cb_ref[...]


def _proj_conv_kernel(x_ref, sh_ref, sc_ref, w1_ref, w2_ref, w3_ref, w4_ref, lcw_ref, lcb_ref, scw_ref, scb_ref,
                      o1_ref, o2_ref, o3_ref, o4_ref, lconv_ref, sconv_ref, lpad, spad, *, tl):
    b = pl.program_id(0)

    @pl.when(pl.program_id(1) == 0)
    def _():
        lpad[0:SUBLANES, :] = jnp.zeros((SUBLANES, LRU_W), F32)
        spad[0:SUBLANES, :] = jnp.zeros((SUBLANES, SSD_CONV_DIM), F32)

    u = _layer_norm(x_ref[...]) * (1.0 + _mod_rows(sc_ref, True, b)) + _mod_rows(sh_ref, True, b)
    ub = u.astype(BF16)
    o1_ref[:, 0:LRU_W] = _causal_conv(lpad, _dot(ub, w1_ref[:, 0:LRU_W]), lcw_ref, lcb_ref, lconv_ref, tl)
    o1_ref[:, LRU_W:2 * LRU_W] = _dot(ub, w1_ref[:, LRU_W:2 * LRU_W])
    o2_ref[...] = _dot(ub, w2_ref[...])
    o3_ref[...] = _silu(_causal_conv(spad, _dot(ub, w3_ref[...]), scw_ref, scb_ref, sconv_ref, tl))
    o4_ref[...] = _dot(ub, w4_ref[...])


def _proj_conv_call(x3, mod, ws, lp, sp, tl):
    nb, L, _ = x3.shape
    widths = [w.shape[1] for w in ws]
    row = lambda b, l: (b, l, 0)
    full = lambda b, l: (0, 0)
    st = lambda b, l: (b, 0, 0)
    bof, tof = (lambda b, l: b), (lambda b, l: l)
    return pl.pallas_call(
        functools.partial(_proj_conv_kernel, tl=tl),
        out_shape=[jax.ShapeDtypeStruct((nb, L, n), F32) for n in widths]
                  + [jax.ShapeDtypeStruct((nb, CONV_W - 1, LRU_W), F32),
                     jax.ShapeDtypeStruct((nb, CONV_W - 1, SSD_CONV_DIM), F32)],
        grid=(nb, L // tl),
        in_specs=[pl.BlockSpec((None, tl, D), row), mod.spec(0, tl, bof, tof), mod.spec(1, tl, bof, tof)]
                 + [pl.BlockSpec((D, n), full) for n in widths]
                 + [pl.BlockSpec((CONV_W, LRU_W), full), pl.BlockSpec((1, LRU_W), full),
                    pl.BlockSpec((CONV_W, SSD_CONV_DIM), full), pl.BlockSpec((1, SSD_CONV_DIM), full)],
        out_specs=[pl.BlockSpec((None, tl, n), row) for n in widths]
                  + [pl.BlockSpec((None, CONV_W - 1, LRU_W), st), pl.BlockSpec((None, CONV_W - 1, SSD_CONV_DIM), st)],
        scratch_shapes=[pltpu.VMEM((tl + SUBLANES, LRU_W), F32), pltpu.VMEM((tl + SUBLANES, SSD_CONV_DIM), F32)],
        compiler_params=_cparams(("parallel", "arbitrary")),
        name="projc",
    )(x3, mod.table, mod.table, *ws, lp["cw"], lp["cb"], sp["cw"], sp["cb"])


def _proj_call(x3, mod, ws, tl):
    nb, L, _ = x3.shape
    widths = [w.shape[1] for w in ws]
    row = lambda b, l: (b, l, 0)
    full = lambda b, l: (0, 0)
    bof, tof = (lambda b, l: b), (lambda b, l: l)
    return pl.pallas_call(
        functools.partial(_proj_kernel, per_batch=mod.per_batch),
        out_shape=[jax.ShapeDtypeStruct((nb, L, n), F32) for n in widths],
        grid=(nb, L // tl),
        in_specs=[pl.BlockSpec((None, tl, D), row), mod.spec(0, tl, bof, tof), mod.spec(1, tl, bof, tof)]
                 + [pl.BlockSpec((D, n), full) for n in widths],
        out_specs=[pl.BlockSpec((None, tl, n), row) for n in widths],
        compiler_params=_cparams(("parallel", "arbitrary")),
        name="proj",
    )(x3, mod.table, mod.table, *ws)


def _scan_segments(a_sk, u_sk, h0, seg):
    stride = seg + 1
    nt = D // LANES
    rows = lambda g: pl.ds(g, SUBLANES, stride=stride)
    shape = (nt, SUBLANES, LANES)

    def local(g, carry):
        h, p = carry
        a = a_sk[:, rows(g), :]
        h = a * h + u_sk[:, rows(g), :]
        u_sk[:, rows(g), :] = h
        return h, a * p

    h_fin, p_fin = lax.fori_loop(0, seg, local, (jnp.zeros(shape, F32), jnp.ones(shape, F32)), unroll=2)
    starts = [jnp.stack([h0[:, c * LANES:(c + 1) * LANES] for c in range(nt)], axis=0)]
    for s in range(SUBLANES):
        starts.append(p_fin[:, s:s + 1, :] * starts[-1] + h_fin[:, s:s + 1, :])
    start = jnp.concatenate(starts[:SUBLANES], axis=1)

    def fixup(g, p):
        p = a_sk[:, rows(g), :] * p
        u_sk[:, rows(g), :] = u_sk[:, rows(g), :] + p * start
        return p

    lax.fori_loop(0, seg, fixup, jnp.ones(shape, F32), unroll=2)
    return jnp.concatenate([starts[SUBLANES][c] for c in range(nt)], axis=1)


def _lru_kernel(x_ref, g_ref, wa_ref, wx_ref, ba_ref, bx_ref, lam_ref, ng_ref,
                y_ref, h_ref, a_sk, u_sk, hc_scr, *, tl):
    @pl.when(pl.program_id(1) == 0)
    def _():
        hc_scr[...] = jnp.zeros((1, D), F32)

    seg = tl // SUBLANES
    sp = _softplus(-lam_ref[...])
    for j in range(LRU_W // LRU_PACK_W):
        cs = slice(j * LRU_PACK_W, (j + 1) * LRU_PACK_W)
        xc = x_ref[:, cs]
        a, u = _lru_gates(xc, xc.astype(BF16), wa_ref[j], wx_ref[j], ba_ref[:, cs], bx_ref[:, cs], sp[:, cs])
        for s in range(SUBLANES):
            for t in range(LRU_PACK_W // LANES):
                c = j * (LRU_PACK_W // LANES) + t
                a_sk[c, s * (seg + 1):s * (seg + 1) + seg, :] = a[s * seg:(s + 1) * seg, t * LANES:(t + 1) * LANES]
                u_sk[c, s * (seg + 1):s * (seg + 1) + seg, :] = u[s * seg:(s + 1) * seg, t * LANES:(t + 1) * LANES]

    h_last = _scan_segments(a_sk, u_sk, hc_scr[...], seg)
    hc_scr[...] = h_last
    h_ref[...] = h_last
    for s in range(SUBLANES):
        h = jnp.concatenate([u_sk[c, s * (seg + 1):s * (seg + 1) + seg, :] for c in range(D // LANES)], axis=1)
        y = h * _gelu_tanh(g_ref[s * seg:(s + 1) * seg, :])
        y_ref[s * seg:(s + 1) * seg, :] = _rms_norm(y, ng_ref[...]).astype(BF16)


def _lru_call(xg, p, tl):
    nb, L, _ = xg.shape
    vec = lambda b, l: (0, 0)
    blk = lambda b, l: (0, 0, 0)
    return pl.pallas_call(
        functools.partial(_lru_kernel, tl=tl),
        out_shape=[jax.ShapeDtypeStruct((nb, L, LRU_W), BF16),
                   jax.ShapeDtypeStruct((nb, 1, LRU_W), F32)],
        grid=(nb, L // tl),
        in_specs=[pl.BlockSpec((None, tl, LRU_W), lambda b, l: (b, l, 0)),
                  pl.BlockSpec((None, tl, LRU_W), lambda b, l: (b, l, 1)),
                  pl.BlockSpec((LRU_W // LRU_PACK_W, LRU_PACK_W, LRU_PACK_W), blk),
                  pl.BlockSpec((LRU_W // LRU_PACK_W, LRU_PACK_W, LRU_PACK_W), blk),
                  pl.BlockSpec((1, LRU_W), vec), pl.BlockSpec((1, LRU_W), vec),
                  pl.BlockSpec((1, LRU_W), vec), pl.BlockSpec((1, LRU_W), vec)],
        out_specs=[pl.BlockSpec((None, tl, LRU_W), lambda b, l: (b, l, 0)),
                   pl.BlockSpec((None, 1, LRU_W), lambda b, l: (b, 0, 0))],
        scratch_shapes=[pltpu.VMEM((LRU_W // LANES, tl + SUBLANES, LANES), F32),
                        pltpu.VMEM((LRU_W // LANES, tl + SUBLANES, LANES), F32),
                        pltpu.VMEM((1, LRU_W), F32)],
        compiler_params=_cparams(("parallel", "arbitrary")),
        name="lru",
    )(xg, xg, p["wa"], p["wx"], p["ba"], p["bx"], p["lam"], p["ng"])


def _ssd_kernel(xc_ref, z_blk, dt_blk, dtb_ref, alog_ref, dexp_ref, ng_ref,
                y_blk, st_ref, st_scr, *tmp, chunks):
    @pl.when(pl.program_id(1) == 0)
    def _():
        st_scr[...] = jnp.zeros((SSD_N, SSD_INNER), F32)

    sets = (tmp[0:3], tmp[3:6])
    for c in range(chunks):
        rows = pl.ds(c * SSD_Q, SSD_Q)
        _ssd_chunk(xc_ref.at[rows, :], z_blk.at[rows, :], dt_blk.at[rows, :], dtb_ref, alog_ref, dexp_ref, ng_ref,
                   y_blk.at[rows, :], st_scr, *sets[c % 2])

    @pl.when(pl.program_id(1) == pl.num_programs(1) - 1)
    def _():
        st_ref[...] = st_scr[...].T


def _ssd_chunk(xc_scr, z_ref, dt_ref, dtb_ref, alog_ref, dexp_ref, ng_ref, y_ref, st_scr, y_scr, ea_scr, ew_scr):
    q = SSD_Q
    dt = _softplus(dt_ref[...] + dtb_ref[...])
    da = dt * (-jnp.exp(alog_ref[...]))
    ri = lax.broadcasted_iota(I32, (q, q), 0)
    ci = lax.broadcasted_iota(I32, (q, q), 1)
    causal = ri >= ci
    acs = _dot(causal.astype(F32), da, precision=HIGHEST)
    acs_t = acs.T
    dt_t = dt.T
    w_end = dt * jnp.exp(acs[q - 1:q, :] - acs)

    half = SSD_INNER // SSD_GROUPS
    hpg = SSD_HEADS // SSD_GROUPS
    cgb, cbs = [], []
    for g in range(SSD_GROUPS):
        bgb = xc_scr[:, SSD_INNER + g * SSD_N:SSD_INNER + (g + 1) * SSD_N].astype(BF16)
        cgb.append(xc_scr[:, SSD_INNER + (SSD_GROUPS + g) * SSD_N:SSD_INNER + (SSD_GROUPS + g + 1) * SSD_N].astype(BF16))
        cbs.append(jnp.where(causal, _dot_nt(cgb[g], bgb), 0.0))

    low = lax.broadcasted_iota(I32, (q, LANES), 1) < SSD_P
    for k in range(SSD_HEADS // 2):
        cs = slice(k * LANES, (k + 1) * LANES)
        cb = cbs[(2 * k) // hpg]
        m, colb, wb = [], [], []
        for h in (2 * k, 2 * k + 1):
            colb.append(jnp.broadcast_to(acs[:, h:h + 1], (q, LANES)))
            wb.append(jnp.broadcast_to(w_end[:, h:h + 1], (q, LANES)))
            seg = colb[-1] - acs_t[h:h + 1, :]
            decay = jnp.exp(jnp.minimum(seg, 0.0))
            m.append((cb * (decay * dt_t[h:h + 1, :])).astype(BF16))
        ea_scr[:, cs] = jnp.where(low, colb[0], colb[1])
        ew_scr[:, cs] = jnp.where(low, wb[0], wb[1])
        xk = xc_scr[:, cs]
        zero = jnp.zeros_like(xk)
        rhs = jnp.concatenate([jnp.where(low, xk, zero), jnp.where(low, zero, xk)], axis=0).astype(BF16)
        y_scr[:, cs] = _dot(jnp.concatenate(m, axis=1), rhs)

    x = xc_scr[:, 0:SSD_INNER]
    ea = ea_scr[...]
    w = (x * ew_scr[...]).astype(BF16)
    s_in = st_scr[...]
    sb = s_in.astype(BF16)
    y_off, c_state = [], []
    for g in range(SSD_GROUPS):
        gs = slice(g * half, (g + 1) * half)
        bgt = xc_scr[:, SSD_INNER + g * SSD_N:SSD_INNER + (g + 1) * SSD_N].T.astype(BF16)
        y_off.append(_dot(cgb[g], sb[:, gs]))
        c_state.append(_dot(bgt, w[:, gs]))
    st_scr[...] = jnp.exp(ea[q - 1:q, :]) * s_in + jnp.concatenate(c_state, axis=1)
    y = y_scr[...] + jnp.concatenate(y_off, axis=1) * jnp.exp(ea) + dexp_ref[...] * x

    yz = y * _silu(z_ref[...])
    y_ref[...] = _rms_norm(yz, ng_ref[...]).astype(BF16)


def _ssd_call(xbc, z, dtr, p, chunks):
    nb, L, _ = xbc.shape
    q = SSD_Q
    tl = q * chunks
    vec = lambda b, c: (0, 0)
    row = lambda b, c: (b, c, 0)
    return pl.pallas_call(
        functools.partial(_ssd_kernel, chunks=chunks),
        out_shape=[jax.ShapeDtypeStruct((nb, L, SSD_INNER), BF16),
                   jax.ShapeDtypeStruct((nb, SSD_INNER, SSD_N), F32)],
        grid=(nb, L // tl),
        in_specs=[pl.BlockSpec((None, tl, SSD_CONV_DIM), row), pl.BlockSpec((None, tl, SSD_INNER), row),
                  pl.BlockSpec((None, tl, LANES), row),
                  pl.BlockSpec((1, LANES), vec), pl.BlockSpec((1, LANES), vec),
                  pl.BlockSpec((1, SSD_INNER), vec), pl.BlockSpec((1, SSD_INNER), vec)],
        out_specs=[pl.BlockSpec((None, tl, SSD_INNER), row),
                   pl.BlockSpec((None, SSD_INNER, SSD_N), lambda b, c: (b, 0, 0))],
        scratch_shapes=[pltpu.VMEM((SSD_N, SSD_INNER), F32)] + [pltpu.VMEM((q, SSD_INNER), F32)] * 6,
        compiler_params=_cparams(("parallel", "arbitrary")),
        name="ssd",
    )(xbc, z, dtr, p["dtb"], p["alog"], p["dexp"], p["ng"])


def _srow_kernel(xg_ref, xbc_ref, dte_ref, lconv_ref, h0_ref, sconv_ref,
                 lcw_ref, lcb_ref, wa_ref, wx_ref, ba_ref, bx_ref, lam_ref, lng_ref,
                 scw_ref, scb_ref, dtbe_ref, aloge_ref,
                 yl_ref, h_ref, xs_ref, bm_ref, cm_ref, xdtt_ref, dect_ref, lcn_ref, scn_ref, xc_scr):
    xl = xg_ref[:, 0:LRU_W]
    acc = lcb_ref[...] + xl * lcw_ref[CONV_W - 1:CONV_W, :]
    for k in range(CONV_W - 1):
        acc = acc + lconv_ref[k] * lcw_ref[k:k + 1, :]
    xc_scr[...] = acc
    for k in range(CONV_W - 2):
        lcn_ref[k] = lconv_ref[k + 1]
        scn_ref[k] = sconv_ref[k + 1]
    lcn_ref[CONV_W - 2] = xl
    scn_ref[CONV_W - 2] = xbc_ref[...]
    sp = _softplus(-lam_ref[...])
    for j in range(LRU_W // LRU_PACK_W):
        cs = slice(j * LRU_PACK_W, (j + 1) * LRU_PACK_W)
        xc = xc_scr[:, cs]
        a, u = _lru_gates(xc, xc.astype(BF16), wa_ref[j], wx_ref[j], ba_ref[:, cs], bx_ref[:, cs], sp[:, cs])
        h_ref[:, cs] = a * h0_ref[:, cs] + u
    y = h_ref[...] * _gelu_tanh(xg_ref[:, LRU_W:2 * LRU_W])
    yl_ref[...] = _rms_norm(y, lng_ref[...]).astype(BF16)

    acc = scb_ref[...] + xbc_ref[...] * scw_ref[CONV_W - 1:CONV_W, :]
    for k in range(CONV_W - 1):
        acc = acc + sconv_ref[k] * scw_ref[k:k + 1, :]
    xc = _silu(acc)
    xs = xc[:, 0:SSD_INNER]
    xs_ref[...] = xs
    bm_ref[...] = xc[:, SSD_INNER:SSD_INNER + SSD_GROUPS * SSD_N]
    cm_ref[...] = xc[:, SSD_INNER + SSD_GROUPS * SSD_N:]
    dt = _softplus(dte_ref[...] + dtbe_ref[...])
    dec = jnp.exp(dt * (-jnp.exp(aloge_ref[...])))
    xdtt = (xs * dt).T
    dect = dec.T
    for j in range(xdtt_ref.shape[0]):
        xdtt_ref[j] = xdtt[:, j * SUBLANES:(j + 1) * SUBLANES]
        dect_ref[j] = dect[:, j * SUBLANES:(j + 1) * SUBLANES]


def _srow_call(xg, xbc, dte, lconv, h0, sconv, lp, sp):
    n = xg.shape[0]
    args = (xg, xbc, dte, lconv, h0, sconv, lp["cw"], lp["cb"], lp["wa"], lp["wx"], lp["ba"], lp["bx"],
            lp["lam"], lp["ng"], sp["cw"], sp["cb"], sp["dtbe"], sp["aloge"])
    full = lambda a: pl.BlockSpec(a.shape, lambda i, nd=a.ndim: (0,) * nd)
    outs = [jax.ShapeDtypeStruct((n, LRU_W), BF16), jax.ShapeDtypeStruct((n, LRU_W), F32),
            jax.ShapeDtypeStruct((n, SSD_INNER), F32), jax.ShapeDtypeStruct((n, SSD_GROUPS * SSD_N), F32),
            jax.ShapeDtypeStruct((n, SSD_GROUPS * SSD_N), F32),
            jax.ShapeDtypeStruct((n // SUBLANES, SSD_INNER, SUBLANES), F32),
            jax.ShapeDtypeStruct((n // SUBLANES, SSD_INNER, SUBLANES), F32),
            jax.ShapeDtypeStruct((CONV_W - 1, n, LRU_W), F32), jax.ShapeDtypeStruct((CONV_W - 1, n, SSD_CONV_DIM), F32)]
    return pl.pallas_call(
        _srow_kernel,
        out_shape=outs,
        grid=(1,),
        in_specs=[full(a) for a in args],
        out_specs=[pl.BlockSpec(o.shape, lambda i, nd=len(o.shape): (0,) * nd) for o in outs],
        scratch_shapes=[pltpu.VMEM((n, LRU_W), F32)],
        compiler_params=_cparams(("arbitrary",)),
        name="srow",
    )(*args)


def _sstate_kernel(s0_ref, xq_ref, dq_ref, bm_ref, cm_ref, xs_ref, z_ref, dexp_ref, ng_ref,
                   sn_ref, ys_ref, yraw):
    nb = s0_ref.shape[0]
    half = SSD_INNER // SSD_GROUPS
    for bi in range(nb):
        brow = jnp.concatenate(
            [jnp.broadcast_to(bm_ref[bi:bi + 1, g * SSD_N:(g + 1) * SSD_N], (half, SSD_N)) for g in range(SSD_GROUPS)],
            axis=0)
        dec = jnp.concatenate(
            [jnp.broadcast_to(dq_ref[h * SSD_P:h * SSD_P + 1, bi:bi + 1], (SSD_P, SSD_N)) for h in range(SSD_HEADS)],
            axis=0)
        s = dec * s0_ref[bi] + xq_ref[:, bi:bi + 1] * brow
        sn_ref[bi] = s
        sb = s.astype(BF16)
        for g in range(SSD_GROUPS):
            cg = cm_ref[:, g * SSD_N:(g + 1) * SSD_N].astype(BF16)
            res = _dot_nt(cg, sb[g * half:(g + 1) * half, :])
            yraw[bi:bi + 1, g * half:(g + 1) * half] = res[bi:bi + 1, :]
    y = yraw[...] + dexp_ref[...] * xs_ref[...]
    ys_ref[...] = _rms_norm(y * _silu(z_ref[...]), ng_ref[...]).astype(BF16)


def _sstate_call(s0, xq, dq, bm, cm, xs, z, dexp, ng, nb):
    n = s0.shape[0]
    row = lambda i: (i, 0)
    vec = lambda i: (0, 0)
    gn = SSD_GROUPS * SSD_N
    return pl.pallas_call(
        _sstate_kernel,
        out_shape=[jax.ShapeDtypeStruct(s0.shape, F32), jax.ShapeDtypeStruct((n, SSD_INNER), BF16)],
        grid=(n // nb,),
        in_specs=[pl.BlockSpec((nb, SSD_INNER, SSD_N), lambda i: (i, 0, 0)),
                  pl.BlockSpec((None, SSD_INNER, nb), lambda i: (i, 0, 0)),
                  pl.BlockSpec((None, SSD_INNER, nb), lambda i: (i, 0, 0)),
                  pl.BlockSpec((nb, gn), row), pl.BlockSpec((nb, gn), row),
                  pl.BlockSpec((nb, SSD_INNER), row), pl.BlockSpec((nb, SSD_INNER), row),
                  pl.BlockSpec((1, SSD_INNER), vec), pl.BlockSpec((1, SSD_INNER), vec)],
        out_specs=[pl.BlockSpec((nb, SSD_INNER, SSD_N), lambda i: (i, 0, 0)), pl.BlockSpec((nb, SSD_INNER), row)],
        scratch_shapes=[pltpu.VMEM((nb, SSD_INNER), F32)],
        compiler_params=_cparams(("parallel",)),
        name="sstate",
    )(s0, xq, dq, bm, cm, xs, z, dexp, ng)


def _out_proj(yl_ref, ys_ref, wo_ref):
    return _dot(yl_ref[...], wo_ref[0:LRU_W, :]) + _dot(ys_ref[...], wo_ref[LRU_W:LRU_W + SSD_INNER, :])


def _post_kernel(yl_ref, ys_ref, x_ref, g1_ref, sh2_ref, sc2_ref, wo_ref, l1g_ref, l1b_ref, wrt_ref, brc_ref,
                 x1_ref, v_ref, eid_ref, gw_ref, *, per_batch):
    _post_tail(_out_proj(yl_ref, ys_ref, wo_ref), pl.program_id(0), x_ref, g1_ref, sh2_ref, sc2_ref,
               l1g_ref, l1b_ref, wrt_ref, brc_ref, x1_ref, v_ref, eid_ref, gw_ref, per_batch)


def _post_lag_kernel(yl_ref, ys_ref, x_ref, g1_ref, sh2_ref, sc2_ref, wo_ref, l1g_ref, l1b_ref, wrt_ref, brc_ref,
                     x1_ref, v_ref, eid_ref, gw_ref, oa, ob, *, nl):
    s = pl.program_id(0)
    b_lag = jnp.maximum(s - 1, 0) // nl

    @pl.when(s == 0)
    def _():
        ob[...] = jnp.zeros(ob.shape, F32)

    def step(o_new, o_old):
        o_new[...] = _out_proj(yl_ref, ys_ref, wo_ref)
        _post_tail(o_old[...], b_lag, x_ref, g1_ref, sh2_ref, sc2_ref, l1g_ref, l1b_ref, wrt_ref, brc_ref,
                   x1_ref, v_ref, eid_ref, gw_ref, True)

    pl.when(s % 2 == 0)(functools.partial(step, oa, ob))
    pl.when(s % 2 == 1)(functools.partial(step, ob, oa))


def _post_tail(o, b, x_ref, g1_ref, sh2_ref, sc2_ref, l1g_ref, l1b_ref, wrt_ref, brc_ref,
               x1_ref, v_ref, eid_ref, gw_ref, per_batch):
    x1 = _layer_norm(ALPHA * x_ref[...] + _mod_rows(g1_ref, per_batch, b) * o) * l1g_ref[...] + l1b_ref[...]
    x1_ref[...] = x1
    v = _layer_norm(x1) * (1.0 + _mod_rows(sc2_ref, per_batch, b)) + _mod_rows(sh2_ref, per_batch, b)
    _store_row_tiles(v_ref, v)

    lt = _dot(v.astype(BF16), wrt_ref[...]).T + brc_ref[...]
    tl = lt.shape[1]
    row = lax.broadcasted_iota(I32, (GROUP_SIZE, tl), 0).astype(F32)
    big = float(GROUP_SIZE)
    neg = -jnp.inf
    lg = jnp.where(row < N_GROUPS, lt[0:GROUP_SIZE, :], neg)
    gmax = jnp.max(lg, axis=0, keepdims=True)
    gsel = jnp.min(jnp.where(lg == gmax, row, big), axis=0, keepdims=True)
    pg = 1.0 / jnp.sum(jnp.exp(lg - gmax), axis=0, keepdims=True)
    le = lt[GROUP_SIZE:2 * GROUP_SIZE, :]
    for j in range(1, N_GROUPS):
        le = jnp.where(gsel == j, lt[GROUP_SIZE * (j + 1):GROUP_SIZE * (j + 2), :], le)
    m1 = jnp.max(le, axis=0, keepdims=True)
    i1 = jnp.min(jnp.where(le == m1, row, big), axis=0, keepdims=True)
    rest = jnp.where(row == i1, neg, le)
    m2 = jnp.max(rest, axis=0, keepdims=True)
    i2 = jnp.min(jnp.where(rest == m2, row, big), axis=0, keepdims=True)
    e2 = jnp.exp(m2 - m1)
    den = 1.0 + e2
    eid = jnp.where(row == 0, gsel * GROUP_SIZE + i1, jnp.where(row == 1, gsel * GROUP_SIZE + i2, 0.0))
    eid_ref[...] = eid.astype(I32)
    gw_ref[...] = jnp.where(row == 0, pg * (1.0 / den), jnp.where(row == 1, pg * (e2 / den), 0.0))


def _post_call(yl, ys, x3, mod, p, tl):
    nb, L, _ = x3.shape
    nl = L // tl
    row = lambda b, l: (b, l, 0)
    vec = lambda b, l: (0, 0)
    tok = lambda b, l: (b * nl + l, 0)
    tokt = lambda b, l: (0, b * nl + l)
    bof, tof = (lambda b, l: b), (lambda b, l: l)
    return pl.pallas_call(
        functools.partial(_post_kernel, per_batch=mod.per_batch),
        out_shape=[jax.ShapeDtypeStruct((nb, L, D), F32), jax.ShapeDtypeStruct(_tile_rows(nb * L), F32),
                   jax.ShapeDtypeStruct((SUBLANES, nb * L), I32), jax.ShapeDtypeStruct((SUBLANES, nb * L), F32)],
        grid=(nb, nl),
        in_specs=[pl.BlockSpec((None, tl, LRU_W), row), pl.BlockSpec((None, tl, SSD_INNER), row),
                  pl.BlockSpec((None, tl, D), row),
                  mod.spec(2, tl, bof, tof), mod.spec(3, tl, bof, tof), mod.spec(4, tl, bof, tof),
                  pl.BlockSpec((LRU_W + SSD_INNER, D), vec), pl.BlockSpec((1, D), vec), pl.BlockSpec((1, D), vec),
                  pl.BlockSpec((D, LANES), vec), pl.BlockSpec((LANES, 1), vec)],
        out_specs=[pl.BlockSpec((None, tl, D), row), pl.BlockSpec(_tile_rows(tl), tok),
                   pl.BlockSpec((SUBLANES, tl), tokt), pl.BlockSpec((SUBLANES, tl), tokt)],
        compiler_params=_cparams(("parallel", "arbitrary")),
        name="post",
    )(yl, ys, x3, mod.table, mod.table, mod.table, p["wo"], p["l1g"], p["l1b"], p["wrt"], p["brc"])


def _post_lag_call(yl, ys, x3, mod, p, tl):
    nb, L, _ = x3.shape
    nl = L // tl
    n = nb * nl
    cur = lambda s: (jnp.minimum(s, n - 1) // nl, jnp.minimum(s, n - 1) % nl, 0)
    lag = lambda s: (jnp.maximum(s - 1, 0) // nl, jnp.maximum(s - 1, 0) % nl, 0)
    lag_tok = lambda s: (jnp.maximum(s - 1, 0), 0)
    lag_tokt = lambda s: (0, jnp.maximum(s - 1, 0))
    vec = lambda s: (0, 0)
    one = lambda *g: 0
    return pl.pallas_call(
        functools.partial(_post_lag_kernel, nl=nl),
        out_shape=[jax.ShapeDtypeStruct((nb, L, D), F32), jax.ShapeDtypeStruct(_tile_rows(nb * L), F32),
                   jax.ShapeDtypeStruct((SUBLANES, nb * L), I32), jax.ShapeDtypeStruct((SUBLANES, nb * L), F32)],
        grid=(n + 1,),
        in_specs=[pl.BlockSpec((None, tl, LRU_W), cur), pl.BlockSpec((None, tl, SSD_INNER), cur),
                  pl.BlockSpec((None, tl, D), lag),
                  mod.spec(2, tl, one, one), mod.spec(3, tl, one, one), mod.spec(4, tl, one, one),
                  pl.BlockSpec((LRU_W + SSD_INNER, D), vec), pl.BlockSpec((1, D), vec), pl.BlockSpec((1, D), vec),
                  pl.BlockSpec((D, LANES), vec), pl.BlockSpec((LANES, 1), vec)],
        out_specs=[pl.BlockSpec((None, tl, D), lag), pl.BlockSpec(_tile_rows(tl), lag_tok),
                   pl.BlockSpec((SUBLANES, tl), lag_tokt), pl.BlockSpec((SUBLANES, tl), lag_tokt)],
        scratch_shapes=[pltpu.VMEM((tl, D), F32), pltpu.VMEM((tl, D), F32)],
        compiler_params=_cparams(("arbitrary",)),
        name="postl",
    )(yl, ys, x3, mod.table, mod.table, mod.table, p["wo"], p["l1g"], p["l1b"], p["wrt"], p["brc"])


def _route_kernel(eidp_ref, eids_ref, dest_ref, cnt_ref, run, poff, *, p_tiles):
    ph = pl.program_id(0)
    t = pl.program_id(1)
    n = ROUTE_TILE
    eid = jnp.where(t < p_tiles, eidp_ref[...], eids_ref[...])
    rowi = lax.broadcasted_iota(I32, (LANES, n), 0)
    oh0 = rowi == eid[0:1, :]
    oh1 = rowi == eid[1:2, :]
    oh = oh0.astype(F32) + oh1.astype(F32)
    tile_cnt = jnp.sum(oh, axis=1, keepdims=True)

    @pl.when((ph == 0) & (t == 0))
    def _():
        run[...] = jnp.zeros((LANES, n), F32)

    @pl.when(ph == 0)
    def _():
        run[...] = run[...] + tile_cnt

    @pl.when((ph == 1) & (t == 0))
    def _():
        counts = run[...]
        cnt_ref[...] = counts
        ci = counts.astype(I32)
        q = jnp.floor(counts * (1.0 / MOE_BM)).astype(I32)
        rem = ci - q * MOE_BM
        q = q + jnp.where(rem >= MOE_BM, 1, 0) - jnp.where(rem < 0, 1, 0)
        nblk = q + jnp.where(ci - q * MOE_BM > 0, 1, 0)
        r = lax.broadcasted_iota(I32, (LANES, LANES), 0)
        c = lax.broadcasted_iota(I32, (LANES, LANES), 1)
        lower = (r > c).astype(BF16)
        poff[...] = _dot(lower, nblk.astype(F32).astype(BF16)) * float(MOE_BM)
        run[...] = jnp.zeros((LANES, n), F32)

    @pl.when(ph == 1)
    def _():
        r = lax.broadcasted_iota(I32, (n, n), 0)
        c = lax.broadcasted_iota(I32, (n, n), 1)
        before = (r < c).astype(BF16)
        slot = _dot(oh.astype(BF16), before) + run[...] + poff[...]
        d0 = jnp.sum(jnp.where(oh0, slot, 0.0), axis=0, keepdims=True)
        d1 = jnp.sum(jnp.where(oh1, slot, 0.0), axis=0, keepdims=True)
        row = lax.broadcasted_iota(I32, (SUBLANES, n), 0)
        dest_ref[...] = jnp.where(row == 0, d0, jnp.where(row == 1, d1, 0.0)).astype(I32)
        run[...] = run[...] + tile_cnt


def _route_call(eid_p, eid_s):
    n = ROUTE_TILE
    p_tiles = eid_p.shape[1] // n
    s_tiles = eid_s.shape[1] // n
    t_all = (p_tiles + s_tiles) * n
    return pl.pallas_call(
        functools.partial(_route_kernel, p_tiles=p_tiles),
        out_shape=[jax.ShapeDtypeStruct((SUBLANES, t_all), I32), jax.ShapeDtypeStruct((LANES, n), F32)],
        grid=(2, p_tiles + s_tiles),
        in_specs=[pl.BlockSpec((SUBLANES, n), lambda ph, t: (0, jnp.minimum(t, p_tiles - 1))),
                  pl.BlockSpec((SUBLANES, n), lambda ph, t: (0, jnp.maximum(t - p_tiles, 0)))],
        out_specs=[pl.BlockSpec((SUBLANES, n), lambda ph, t: (0, t * ph)),
                   pl.BlockSpec((LANES, n), lambda ph, t: (0, 0))],
        scratch_shapes=[pltpu.VMEM((LANES, n), F32), pltpu.VMEM((LANES, n), F32)],
        compiler_params=_cparams(("arbitrary", "arbitrary")),
        name="route",
    )(eid_p, eid_s)


def _row_copy(src, src_row, dst, dst_row, sem):
    s0 = pl.multiple_of(src_row * SUBLANES, SUBLANES)
    d0 = pl.multiple_of(dst_row * SUBLANES, SUBLANES)
    return pltpu.make_async_copy(src.at[pl.ds(s0, SUBLANES), :], dst.at[pl.ds(d0, SUBLANES), :], sem)


def _scatter_rows(dest_ref, v_ref, xpad_ref, sem, tl):
    def start(r, carry):
        for k in range(2):
            _row_copy(v_ref, r, xpad_ref, dest_ref[k, r], sem).start(priority=k)
        return carry

    def wait(r, carry):
        for k in range(2):
            _row_copy(v_ref, r, xpad_ref, dest_ref[k, r], sem).wait()
        return carry

    lax.fori_loop(0, tl, start, 0, unroll=DMA_UNROLL)
    lax.fori_loop(0, tl, wait, 0, unroll=DMA_UNROLL)


def _dispatch_kernel(zflag_ref, destp_ref, dests_ref, vp_ref, vs_ref, xpad_ref, zbuf, sem, zsem,
                     *, tl_p, tl_s, p_tiles, n_blocks):
    blk_rows = _tile_rows(MOE_BM)[0]

    @pl.when(pl.program_id(0) == 0)
    def _():
        zbuf[...] = jnp.zeros(zbuf.shape, F32)

        def zero_block(go):
            def body(b, carry):
                @pl.when(zflag_ref[b] != 0)
                def _():
                    r0 = pl.multiple_of(b * blk_rows, blk_rows)
                    cp = pltpu.make_async_copy(zbuf, xpad_ref.at[pl.ds(r0, blk_rows), :], zsem)
                    if go:
                        cp.start()
                    else:
                        cp.wait()
                return carry
            lax.fori_loop(0, n_blocks, body, 0)

        zero_block(True)
        zero_block(False)

    @pl.when(pl.program_id(0) < p_tiles)
    def _():
        _scatter_rows(destp_ref, vp_ref, xpad_ref, sem, tl_p)

    @pl.when(pl.program_id(0) >= p_tiles)
    def _():
        _scatter_rows(dests_ref, vs_ref, xpad_ref, sem, tl_s)


def _dispatch_call(zflag, dest, v_p, v_s, n_blocks, tl_p, tl_s):
    p_tiles = v_p.shape[0] // _tile_rows(tl_p)[0]
    s_tiles = v_s.shape[0] // _tile_rows(tl_s)[0]
    s_off = p_tiles * tl_p // tl_s
    return pl.pallas_call(
        functools.partial(_dispatch_kernel, tl_p=tl_p, tl_s=tl_s, p_tiles=p_tiles, n_blocks=n_blocks),
        out_shape=jax.ShapeDtypeStruct(_tile_rows(n_blocks * MOE_BM), F32),
        grid_spec=pltpu.PrefetchScalarGridSpec(
            num_scalar_prefetch=1,
            grid=(p_tiles + s_tiles,),
            in_specs=[pl.BlockSpec((SUBLANES, tl_p), lambda t, z: (0, jnp.minimum(t, p_tiles - 1)),
                                   memory_space=pltpu.SMEM),
                      pl.BlockSpec((SUBLANES, tl_s), lambda t, z: (0, s_off + jnp.maximum(t - p_tiles, 0)),
                                   memory_space=pltpu.SMEM),
                      pl.BlockSpec(_tile_rows(tl_p), lambda t, z: (jnp.minimum(t, p_tiles - 1), 0)),
                      pl.BlockSpec(_tile_rows(tl_s), lambda t, z: (jnp.maximum(t - p_tiles, 0), 0))],
            out_specs=pl.BlockSpec(memory_space=pl.ANY),
            scratch_shapes=[pltpu.VMEM(_tile_rows(MOE_BM), F32), pltpu.SemaphoreType.DMA, pltpu.SemaphoreType.DMA]),
        compiler_params=_cparams(("arbitrary",)),
        name="dispatch",
    )(zflag, dest, dest, v_p, v_s)


def _expert_kernel(be_ref, nb_ref, nxt_ref, slot_ref, x_ref, wg_hbm, wu_hbm, wd_hbm, o_ref,
                   wgf, wuf, wdf, wgb, wub, wdb, sems):
    i = pl.program_id(0)

    def weights(e, slot, go):
        for m, (hbm, buf) in enumerate(((wg_hbm, wgf), (wu_hbm, wuf), (wd_hbm, wdf))):
            cp = pltpu.make_async_copy(hbm.at[e], buf.at[slot], sems.at[slot, m])
            if go:
                cp.start()
            else:
                cp.wait()

    @pl.when(i < nb_ref[0])
    def _():
        e = be_ref[i]
        slot = slot_ref[e]

        @pl.when(i == 0)
        def _():
            weights(e, slot, True)

        @pl.when((i == 0) | (e != be_ref[jnp.maximum(i - 1, 0)]))
        def _():
            weights(e, slot, False)
            nxt = nxt_ref[e]

            @pl.when(nxt >= 0)
            def _():
                weights(nxt, 1 - slot, True)

            wgb[...] = wgf[slot].astype(BF16)
            wub[...] = wuf[slot].astype(BF16)
            wdb[...] = wdf[slot].astype(BF16)

        x = _load_row_tiles(x_ref).astype(BF16)
        h = _silu(_dot(x, wgb[...])) * _dot(x, wub[...])
        _store_row_tiles(o_ref, _dot(h.astype(BF16), wdb[...]))


def _expert_call(blk_e, nblk, nxt_e, slot_e, xpad, w_gate, w_up, w_down):
    n_rows = xpad.shape[0] // SUBLANES
    blk = lambda i, be, nb, nx, sl: (jnp.minimum(i, nb[0] - 1), 0)
    hbm = pl.BlockSpec(memory_space=pl.ANY)
    return pl.pallas_call(
        _expert_kernel,
        out_shape=jax.ShapeDtypeStruct(_tile_rows(n_rows), F32),
        grid_spec=pltpu.PrefetchScalarGridSpec(
            num_scalar_prefetch=4,
            grid=(n_rows // MOE_BM,),
            in_specs=[pl.BlockSpec(_tile_rows(MOE_BM), blk), hbm, hbm, hbm],
            out_specs=pl.BlockSpec(_tile_rows(MOE_BM), blk),
            scratch_shapes=[pltpu.VMEM((2, D, D_FF), F32), pltpu.VMEM((2, D, D_FF), F32), pltpu.VMEM((2, D_FF, D), F32),
                            pltpu.VMEM((D, D_FF), BF16), pltpu.VMEM((D, D_FF), BF16), pltpu.VMEM((D_FF, D), BF16),
                            pltpu.SemaphoreType.DMA((2, 3))]),
        input_output_aliases={4: 0},
        compiler_params=_cparams(("arbitrary",)),
        name="expert",
    )(blk_e, nblk, nxt_e, slot_e, xpad, w_gate, w_up, w_down)


def _combine_kernel(dest_ref, gw_ref, x1_ref, g2_ref, l2g_ref, l2b_ref, ypad_ref, y_ref, ybuf, sems,
                    *, tok_off, k_stride, tl, nl, per_batch):
    i = pl.program_id(0)

    def gather(tile, slot, go):
        base = tok_off + tile * tl

        def body(r, carry):
            for k in range(2):
                cp = _row_copy(ypad_ref, dest_ref[k * k_stride + base + r], ybuf.at[slot, k], r, sems.at[slot])
                if go:
                    cp.start(priority=k)
                else:
                    cp.wait()
            return carry

        lax.fori_loop(0, tl, body, 0, unroll=DMA_UNROLL)

    @pl.when(i == 0)
    def _():
        gather(0, 0, True)

    @pl.when(i + 1 < pl.num_programs(0))
    def _():
        gather(i + 1, (i + 1) % 2, True)

    slot = i % 2
    gather(i, slot, False)
    gw_t = jnp.concatenate([gw_ref[...], jnp.zeros((LANES - SUBLANES, tl), F32)], axis=0).T
    f = _load_row_tiles(ybuf.at[slot, 0]) * gw_t[:, 0:1] + _load_row_tiles(ybuf.at[slot, 1]) * gw_t[:, 1:2]
    g2 = _mod_rows(g2_ref, per_batch, i // nl)
    y_ref[...] = _layer_norm(ALPHA * x1_ref[...] + g2 * f) * l2g_ref[...] + l2b_ref[...]


def _combine_call(dest, gw, x1, mod, l2g, l2b, ypad, tok_off, tl):
    nb, L, _ = x1.shape
    nl = L // tl
    row = lambda i, d: (i // nl, i % nl, 0)
    vec = lambda i, d: (0, 0)
    g2_spec = mod.spec(5, tl, lambda i, d: i // nl, lambda i, d: i % nl)
    return pl.pallas_call(
        functools.partial(_combine_kernel, tok_off=tok_off, k_stride=dest.shape[1], tl=tl, nl=nl,
                          per_batch=mod.per_batch),
        out_shape=jax.ShapeDtypeStruct((nb, L, D), F32),
        grid_spec=pltpu.PrefetchScalarGridSpec(
            num_scalar_prefetch=1,
            grid=(nb * nl,),
            in_specs=[pl.BlockSpec((SUBLANES, tl), lambda i, d: (0, i)),
                      pl.BlockSpec((None, tl, D), row), g2_spec,
                      pl.BlockSpec((1, D), vec), pl.BlockSpec((1, D), vec),
                      pl.BlockSpec(memory_space=pl.ANY)],
            out_specs=pl.BlockSpec((None, tl, D), row),
            scratch_shapes=[pltpu.VMEM((2, 2) + _tile_rows(tl), F32), pltpu.SemaphoreType.DMA((2,))]),
        compiler_params=_cparams(("arbitrary",)),
        name="combine",
    )(dest[:2].reshape(-1), gw, x1, mod.table, l2g, l2b, ypad)


def _block_diag(w):
    nh, blk, _ = w.shape
    w4 = w.reshape(nh // LRU_PACK, LRU_PACK, blk, blk)
    eye = jnp.eye(LRU_PACK, dtype=w.dtype)
    out = jnp.einsum("gaij,ab->gaibj", w4, eye)
    return out.reshape(nh // LRU_PACK, LRU_PACK * blk, LRU_PACK * blk).astype(BF16)


def _row(v):
    return v.reshape(1, -1).astype(F32)


def _pad_lanes(v):
    return jnp.pad(v.reshape(1, -1).astype(F32), ((0, 0), (0, LANES - v.shape[-1])))


def kernel(x_prompt, x_sample, c_prompt, c_sample, state_lru_conv, state_lru_h, state_ssd_conv, state_ssd, w_ada, b_ada, w_in, lru_conv_w, lru_conv_b, lru_wa, lru_ba, lru_wx, lru_bx, lru_lambda, lru_norm_g, ssd_conv_w, ssd_conv_b, ssd_dt_bias, ssd_a_log, ssd_d, ssd_norm_g, w_out, ln1_g, ln1_b, w_rg, b_rg, w_re, b_re, w_gate, w_up, w_down, ln2_g, ln2_b):
    assert w_ada.shape[0] == DEPTH == 1
    nbp, seq, _ = x_prompt.shape
    nbs = x_sample.shape[0]
    t_p = nbp * seq
    t_all = t_p + nbs
    assert x_sample.shape[1] == 1 and t_p % ROUTE_TILE == 0 and nbs % TOK_TILE == 0 and nbp == SUBLANES

    c_rows = -(-(nbp + nbs) // 16) * 16
    c_all = jnp.pad(jnp.concatenate([c_sample, c_prompt], axis=0), ((0, c_rows - nbp - nbs), (0, 0)))
    w_in0 = w_in[0]
    o_z, o_xbc, o_dt = 2 * LRU_W, 2 * LRU_W + SSD_INNER, 2 * LRU_W + SSD_INNER + SSD_CONV_DIM
    w_in_t = jnp.swapaxes(w_in0, 0, 1)
    *ws_p, w_dt_t = _wcast_call(w_in_t, ((0, o_z), (o_z, o_xbc), (o_xbc, o_dt)), 512)
    w_dt = jnp.swapaxes(w_dt_t, 0, 1)
    ws_s = ws_p + [jnp.repeat(w_dt, SSD_P, axis=1)]
    ws_p = ws_p + [jnp.pad(w_dt, ((0, 0), (0, LANES - SSD_HEADS)))]
    lp = dict(cw=lru_conv_w[0], cb=_row(lru_conv_b[0]), wa=_block_diag(lru_wa[0]), wx=_block_diag(lru_wx[0]),
              ba=_row(lru_ba[0]), bx=_row(lru_bx[0]), lam=_row(lru_lambda[0]), ng=_row(lru_norm_g[0]))
    sp = dict(cw=ssd_conv_w[0], cb=_row(ssd_conv_b[0]), dtb=_pad_lanes(ssd_dt_bias[0]), alog=_pad_lanes(ssd_a_log[0]),
              dexp=_row(jnp.repeat(ssd_d[0], SSD_P)), ng=_row(ssd_norm_g[0]),
              dtbe=_row(jnp.repeat(ssd_dt_bias[0], SSD_P)), aloge=_row(jnp.repeat(ssd_a_log[0], SSD_P)))
    wrt = jnp.zeros((D, LANES), F32).at[:, 0:N_GROUPS].set(w_rg[0]).at[:, GROUP_SIZE:ROUTE_ROWS].set(w_re[0]).astype(BF16)
    brc = jnp.zeros((LANES, 1), F32).at[0:N_GROUPS, 0].set(b_rg[0]).at[GROUP_SIZE:ROUTE_ROWS, 0].set(b_re[0])
    pp = dict(wo=w_out[0].astype(BF16), l1g=_row(ln1_g[0]), l1b=_row(ln1_b[0]), wrt=wrt, brc=brc)

    table = _ada_call(c_all, w_ada[0], _row(b_ada[0]))
    mod_p = _Mod(table, nbs, True)
    mod_s = _Mod(table, 0, False)

    xg, z, xbc, dtr, p_lru_conv, p_ssd_conv = _proj_conv_call(x_prompt, mod_p, ws_p, lp, sp, SEQ_TILE)
    yl, p_lru_h = _lru_call(xg, lp, LRU_TILE)
    ys, p_ssd = _ssd_call(xbc, z, dtr, sp, SSD_CHUNKS_PER_STEP)
    x1_p, v_p, eid_p, gw_p = _post_lag_call(yl, ys, x_prompt, mod_p, pp, SEQ_TILE)

    xs3 = x_sample.reshape(1, nbs, D)
    xg_s, z_s, xbc_s, dte_s = _proj_call(xs3, mod_s, ws_s, nbs)
    xg_s, z_s, xbc_s, dte_s = xg_s[0], z_s[0], xbc_s[0], dte_s[0]
    yl_s, h_s, xs_s, bm_s, cm_s, xdtt, dect, s_lru_conv, s_ssd_conv = _srow_call(
        xg_s, xbc_s, dte_s, jnp.swapaxes(state_lru_conv[0], 0, 1), state_lru_h[0],
        jnp.swapaxes(state_ssd_conv[0], 0, 1), lp, sp)
    s_new, ys_s = _sstate_call(state_ssd[0].reshape(nbs, SSD_INNER, SSD_N), xdtt, dect,
                               bm_s, cm_s, xs_s, z_s, sp["dexp"], sp["ng"], SUBLANES)
    x1_s, v_s, eid_s, gw_s = _post_call(yl_s[None], ys_s[None], xs3, mod_s, pp, nbs)

    s_pad = -nbs % ROUTE_TILE
    dest, counts = _route_call(eid_p, jnp.pad(eid_s, ((0, 0), (0, s_pad)), constant_values=-1))
    counts = counts[:N_EXPERTS, 0].astype(I32)
    n_blocks = -(-(2 * t_all) // MOE_BM) + N_EXPERTS
    pend = jnp.cumsum(((counts + MOE_BM - 1) // MOE_BM) * MOE_BM)
    blk_start = jnp.arange(n_blocks, dtype=I32) * MOE_BM
    blk_e = jnp.minimum(jnp.sum((pend[None, :] <= blk_start[:, None]).astype(I32), axis=1), N_EXPERTS - 1)
    nblk = (pend[-1:] // MOE_BM).astype(I32)
    blk_ids = jnp.arange(n_blocks, dtype=I32)
    last_of_expert = jnp.any(((pend // MOE_BM - 1)[None, :] == blk_ids[:, None]) & (counts > 0)[None, :], axis=1)
    zflag = (last_of_expert | (blk_ids >= nblk[0])).astype(I32)
    used = counts > 0
    eids = jnp.arange(N_EXPERTS, dtype=I32)
    later_used = (eids[None, :] > eids[:, None]) & used[None, :]
    nxt_e = jnp.min(jnp.where(later_used, eids[None, :], N_EXPERTS), axis=1)
    nxt_e = jnp.where(nxt_e == N_EXPERTS, -1, nxt_e).astype(I32)
    slot_e = ((jnp.cumsum(used.astype(I32)) - used.astype(I32)) % 2).astype(I32)
    xpad = _dispatch_call(zflag, dest, v_p, v_s, n_blocks, 4 * TOK_TILE, TOK_TILE)
    ypad = _expert_call(blk_e, nblk, nxt_e, slot_e, xpad, w_gate[0], w_up[0], w_down[0])
    l2g, l2b = _row(ln2_g[0]), _row(ln2_b[0])
    y_p = _combine_call(dest, gw_p, x1_p, mod_p, l2g, l2b, ypad, 0, 4 * TOK_TILE)
    y_s = _combine_call(dest, gw_s, x1_s, mod_s, l2g, l2b, ypad, t_p, TOK_TILE)

    return (y_p, y_s.reshape(nbs, 1, D),
            p_lru_conv[None], p_lru_h.reshape(1, nbp, LRU_W), p_ssd_conv[None],
            p_ssd.reshape(1, nbp, SSD_HEADS, SSD_P, SSD_N),
            jnp.swapaxes(s_lru_conv, 0, 1)[None], h_s[None], jnp.swapaxes(s_ssd_conv, 0, 1)[None],
            s_new.reshape(1, nbs, SSD_HEADS, SSD_P, SSD_N))
```

```python
import functools
import math

import jax
import jax.numpy as jnp
from jax import lax
from jax.experimental import pallas as pl
from jax.experimental.pallas import tpu as pltpu

F32 = jnp.float32
BF16 = jnp.bfloat16
I32 = jnp.int32
HIGHEST = lax.Precision.HIGHEST

D = 1024
DEPTH = 1
CONV_W = 4
LRU_W = D
LRU_HEADS = 16
LRU_C = 8.0
LRU_PACK = 4
LRU_PACK_W = LRU_PACK * (LRU_W // LRU_HEADS)
SSD_INNER = D
SSD_HEADS = 16
SSD_P = SSD_INNER // SSD_HEADS
SSD_GROUPS = 2
SSD_N = 128
SSD_Q = 128
SSD_CONV_DIM = SSD_INNER + 2 * SSD_GROUPS * SSD_N
N_GROUPS = 4
GROUP_SIZE = 8
N_EXPERTS = N_GROUPS * GROUP_SIZE
D_FF = D // 2
N_MOD = 6
LN_EPS = 1e-5
RMS_EPS = 1e-6
ALPHA = (2.0 * DEPTH) ** 0.25

LANES = 128
SUBLANES = 8
VMEM_LIMIT = 56 * 1024 * 1024
SEQ_TILE = 512
TOK_TILE = 128
ROUTE_TILE = 1024
DMA_UNROLL = 32
MOE_BM = 384
LRU_TILE = 1024
SSD_CHUNKS_PER_STEP = 4
ROUTE_ROWS = 40

NT_DIMS = (((1,), (1,)), ((), ()))


def _tile_rows(rows):
    return (rows * SUBLANES, LANES)


def _cparams(sem):
    return pltpu.CompilerParams(dimension_semantics=sem, vmem_limit_bytes=VMEM_LIMIT)


def _sigmoid(x):
    return 0.5 * (jnp.tanh(0.5 * x) + 1.0)


def _silu(x):
    return x * _sigmoid(x)


def _softplus(x):
    return jnp.maximum(x, 0.0) + jnp.log1p(jnp.exp(-jnp.abs(x)))


def _gelu_tanh(x):
    return 0.5 * x * (1.0 + jnp.tanh(math.sqrt(2.0 / math.pi) * (x + 0.044715 * (x * x * x))))


def _layer_norm(x):
    mu = jnp.mean(x, axis=-1, keepdims=True)
    xc = x - mu
    var = jnp.mean(xc * xc, axis=-1, keepdims=True)
    return xc * lax.rsqrt(var + LN_EPS)


def _rms_norm(x, g):
    return x * lax.rsqrt(jnp.mean(x * x, axis=-1, keepdims=True) + RMS_EPS) * g


def _dot(a, b, **kw):
    return jnp.dot(a, b, preferred_element_type=F32, **kw)


def _dot_nt(a, b, **kw):
    return lax.dot_general(a, b, NT_DIMS, preferred_element_type=F32, **kw)


def _store_row_tiles(ref, val):
    rows = val.shape[0]
    for j in range(SUBLANES):
        ref[pl.ds(j, rows, stride=SUBLANES), :] = val[:, j * LANES:(j + 1) * LANES]


def _load_row_tiles(ref):
    rows = ref.shape[0] // SUBLANES
    return jnp.concatenate([ref[pl.ds(j, rows, stride=SUBLANES), :] for j in range(SUBLANES)], axis=1)


def _lru_gates(xc, xb, wa, wx, ba, bx, sp):
    r = _sigmoid(_dot(xb, wa) + ba)
    i = _sigmoid(_dot(xb, wx) + bx)
    log_a = (-LRU_C) * r * sp
    a = jnp.exp(log_a)
    mult = jnp.sqrt(jnp.tanh(-log_a) * (a * a + 1.0))
    return a, mult * (i * xc)


def _ada_kernel(c_ref, w_ref, b_ref, o_ref):
    s = _silu(c_ref[...]).astype(BF16)
    o_ref[...] = _dot(s, w_ref[...].astype(BF16)) + b_ref[...]


def _ada_call(c_all, w_ada, b_ada):
    rows = c_all.shape[0]
    tn = 512
    per_mod = D // tn
    return pl.pallas_call(
        _ada_kernel,
        out_shape=jax.ShapeDtypeStruct((N_MOD, rows, D), F32),
        grid=(N_MOD * per_mod,),
        in_specs=[pl.BlockSpec((rows, D), lambda j: (0, 0)),
                  pl.BlockSpec((D, tn), lambda j: (0, j)),
                  pl.BlockSpec((1, tn), lambda j: (0, j))],
        out_specs=pl.BlockSpec((None, rows, tn), lambda j: (j // per_mod, 0, j % per_mod)),
        compiler_params=_cparams(("arbitrary",)),
        name="ada",
    )(c_all, w_ada, b_ada)


def _wcast_kernel(wt_ref, tail_ref, *o_refs, firsts):
    j = pl.program_id(0)
    *o_refs, tail_out = o_refs
    tail_out[...] = tail_ref[...].astype(BF16)
    wb = wt_ref[...].T.astype(BF16)
    for k, o_ref in enumerate(o_refs):
        last = firsts[k + 1] if k + 1 < len(firsts) else pl.num_programs(0)

        @pl.when((j >= firsts[k]) & (j < last))
        def _():
            o_ref[...] = wb


def _wcast_call(wt, bounds, tc):
    n, k_dim = wt.shape
    assert all(lo % tc == 0 and hi % tc == 0 for lo, hi in bounds)
    assert all(a[1] == b[0] for a, b in zip(bounds, bounds[1:]))
    firsts = tuple((lo - bounds[0][0]) // tc for lo, _ in bounds)
    n_steps = (bounds[-1][1] - bounds[0][0]) // tc
    base = bounds[0][0] // tc
    tail = n - bounds[-1][1]
    assert tail > 0 and bounds[-1][1] % tail == 0

    def out_spec(first, nblk):
        return pl.BlockSpec((k_dim, tc), lambda j: (0, jnp.clip(j - first, 0, nblk - 1)))

    return pl.pallas_call(
        functools.partial(_wcast_kernel, firsts=firsts),
        out_shape=[jax.ShapeDtypeStruct((k_dim, hi - lo), BF16) for lo, hi in bounds]
                  + [jax.ShapeDtypeStruct((tail, k_dim), BF16)],
        grid=(n_steps,),
        in_specs=[pl.BlockSpec((tc, k_dim), lambda j: (base + j, 0)),
                  pl.BlockSpec((tail, k_dim), lambda j: (bounds[-1][1] // tail, 0))],
        out_specs=[out_spec(f, (hi - lo) // tc) for f, (lo, hi) in zip(firsts, bounds)]
                  + [pl.BlockSpec((tail, k_dim), lambda j: (0, 0))],
        compiler_params=_cparams(("arbitrary",)),
        name="wcast",
    )(wt, wt)


class _Mod:
    def __init__(self, table, row0, per_batch):
        self.table, self.row0, self.per_batch = table, row0, per_batch

    def spec(self, k, tl, batch_of, tile_of):
        if self.per_batch:
            blk = self.row0 // SUBLANES
            return pl.BlockSpec((None, SUBLANES, D), lambda *g: (k, blk, 0))
        blk = self.row0 // tl
        return pl.BlockSpec((None, tl, D), lambda *g: (k, blk + tile_of(*g), 0))


def _mod_rows(ref, per_batch, b):
    return ref[pl.ds(b, 1), :] if per_batch else ref[...]


def _proj_kernel(x_ref, sh_ref, sc_ref, w1_ref, w2_ref, w3_ref, w4_ref, o1_ref, o2_ref, o3_ref, o4_ref, *, per_batch):
    b = pl.program_id(0)
    u = _layer_norm(x_ref[...]) * (1.0 + _mod_rows(sc_ref, per_batch, b)) + _mod_rows(sh_ref, per_batch, b)
    ub = u.astype(BF16)
    o1_ref[...] = _dot(ub, w1_ref[...])
    o2_ref[...] = _dot(ub, w2_ref[...])
    o3_ref[...] = _dot(ub, w3_ref[...])
    o4_ref[...] = _dot(ub, w4_ref[...])


def _causal_conv(xpad, x_new, cw_ref, cb_ref, conv_ref, tl):
    xpad[SUBLANES:SUBLANES + tl, :] = x_new
    off = SUBLANES - (CONV_W - 1)
    acc = xpad[off:off + tl, :] * cw_ref[0:1, :]
    for k in range(1, CONV_W):
        acc = acc + xpad[off + k:off + k + tl, :] * cw_ref[k:k + 1, :]
    conv_ref[...] = xpad[SUBLANES + tl - (CONV_W - 1):SUBLANES + tl, :]
    xpad[0:SUBLANES, :] = xpad[tl:tl + SUBLANES, :]
    return acc + cb_ref[...]


def _proj_conv_kernel(x_ref, sh_ref, sc_ref, w1_ref, w2_ref, w3_ref, w4_ref, lcw_ref, lcb_ref, scw_ref, scb_ref,
                      o1_ref, o2_ref, o3_ref, o4_ref, lconv_ref, sconv_ref, lpad, spad, *, tl):
    b = pl.program_id(0)

    @pl.when(pl.program_id(1) == 0)
    def _():
        lpad[0:SUBLANES, :] = jnp.zeros((SUBLANES, LRU_W), F32)
        spad[0:SUBLANES, :] = jnp.zeros((SUBLANES, SSD_CONV_DIM), F32)

    u = _layer_norm(x_ref[...]) * (1.0 + _mod_rows(sc_ref, True, b)) + _mod_rows(sh_ref, True, b)
    ub = u.astype(BF16)
    o1_ref[:, 0:LRU_W] = _causal_conv(lpad, _dot(ub, w1_ref[:, 0:LRU_W]), lcw_ref, lcb_ref, lconv_ref, tl)
    o1_ref[:, LRU_W:2 * LRU_W] = _dot(ub, w1_ref[:, LRU_W:2 * LRU_W])
    o2_ref[...] = _dot(ub, w2_ref[...])
    o3_ref[...] = _silu(_causal_conv(spad, _dot(ub, w3_ref[...]), scw_ref, scb_ref, sconv_ref, tl))
    o4_ref[...] = _dot(ub, w4_ref[...])


def _proj_conv_call(x3, mod, ws, lp, sp, tl):
    nb, L, _ = x3.shape
    widths = [w.shape[1] for w in ws]
    row = lambda b, l: (b, l, 0)
    full = lambda b, l: (0, 0)
    st = lambda b, l: (b, 0, 0)
    bof, tof = (lambda b, l: b), (lambda b, l: l)
    return pl.pallas_call(
        functools.partial(_proj_conv_kernel, tl=tl),
        out_shape=[jax.ShapeDtypeStruct((nb, L, n), F32) for n in widths]
                  + [jax.ShapeDtypeStruct((nb, CONV_W - 1, LRU_W), F32),
                     jax.ShapeDtypeStruct((nb, CONV_W - 1, SSD_CONV_DIM), F32)],
        grid=(nb, L // tl),
        in_specs=[pl.BlockSpec((None, tl, D), row), mod.spec(0, tl, bof, tof), mod.spec(1, tl, bof, tof)]
                 + [pl.BlockSpec((D, n), full) for n in widths]
                 + [pl.BlockSpec((CONV_W, LRU_W), full), pl.BlockSpec((1, LRU_W), full),
                    pl.BlockSpec((CONV_W, SSD_CONV_DIM), full), pl.BlockSpec((1, SSD_CONV_DIM), full)],
        out_specs=[pl.BlockSpec((None, tl, n), row) for n in widths]
                  + [pl.BlockSpec((None, CONV_W - 1, LRU_W), st), pl.BlockSpec((None, CONV_W - 1, SSD_CONV_DIM), st)],
        scratch_shapes=[pltpu.VMEM((tl + SUBLANES, LRU_W), F32), pltpu.VMEM((tl + SUBLANES, SSD_CONV_DIM), F32)],
        compiler_params=_cparams(("parallel", "arbitrary")),
        name="projc",
    )(x3, mod.table, mod.table, *ws, lp["cw"], lp["cb"], sp["cw"], sp["cb"])


def _proj_call(x3, mod, ws, tl):
    nb, L, _ = x3.shape
    widths = [w.shape[1] for w in ws]
    row = lambda b, l: (b, l, 0)
    full = lambda b, l: (0, 0)
    bof, tof = (lambda b, l: b), (lambda b, l: l)
    return pl.pallas_call(
        functools.partial(_proj_kernel, per_batch=mod.per_batch),
        out_shape=[jax.ShapeDtypeStruct((nb, L, n), F32) for n in widths],
        grid=(nb, L // tl),
        in_specs=[pl.BlockSpec((None, tl, D), row), mod.spec(0, tl, bof, tof), mod.spec(1, tl, bof, tof)]
                 + [pl.BlockSpec((D, n), full) for n in widths],
        out_specs=[pl.BlockSpec((None, tl, n), row) for n in widths],
        compiler_params=_cparams(("parallel", "arbitrary")),
        name="proj",
    )(x3, mod.table, mod.table, *ws)


def _scan_segments(a_sk, u_sk, h0, seg):
    stride = seg + 1
    nt = D // LANES
    rows = lambda g: pl.ds(g, SUBLANES, stride=stride)
    shape = (nt, SUBLANES, LANES)

    def local(g, carry):
        h, p = carry
        a = a_sk[:, rows(g), :]
        h = a * h + u_sk[:, rows(g), :]
        u_sk[:, rows(g), :] = h
        return h, a * p

    h_fin, p_fin = lax.fori_loop(0, seg, local, (jnp.zeros(shape, F32), jnp.ones(shape, F32)), unroll=2)
    starts = [jnp.stack([h0[:, c * LANES:(c + 1) * LANES] for c in range(nt)], axis=0)]
    for s in range(SUBLANES):
        starts.append(p_fin[:, s:s + 1, :] * starts[-1] + h_fin[:, s:s + 1, :])
    start = jnp.concatenate(starts[:SUBLANES], axis=1)

    def fixup(g, p):
        p = a_sk[:, rows(g), :] * p
        u_sk[:, rows(g), :] = u_sk[:, rows(g), :] + p * start
        return p

    lax.fori_loop(0, seg, fixup, jnp.ones(shape, F32), unroll=2)
    return jnp.concatenate([starts[SUBLANES][c] for c in range(nt)], axis=1)


def _lru_kernel(x_ref, g_ref, wa_ref, wx_ref, ba_ref, bx_ref, lam_ref, ng_ref,
                y_ref, h_ref, a_sk, u_sk, hc_scr, *, tl):
    @pl.when(pl.program_id(1) == 0)
    def _():
        hc_scr[...] = jnp.zeros((1, D), F32)

    seg = tl // SUBLANES
    sp = _softplus(-lam_ref[...])
    for j in range(LRU_W // LRU_PACK_W):
        cs = slice(j * LRU_PACK_W, (j + 1) * LRU_PACK_W)
        xc = x_ref[:, cs]
        a, u = _lru_gates(xc, xc.astype(BF16), wa_ref[j], wx_ref[j], ba_ref[:, cs], bx_ref[:, cs], sp[:, cs])
        for s in range(SUBLANES):
            for t in range(LRU_PACK_W // LANES):
                c = j * (LRU_PACK_W // LANES) + t
                a_sk[c, s * (seg + 1):s * (seg + 1) + seg, :] = a[s * seg:(s + 1) * seg, t * LANES:(t + 1) * LANES]
                u_sk[c, s * (seg + 1):s * (seg + 1) + seg, :] = u[s * seg:(s + 1) * seg, t * LANES:(t + 1) * LANES]

    h_last = _scan_segments(a_sk, u_sk, hc_scr[...], seg)
    hc_scr[...] = h_last
    h_ref[...] = h_last
    for s in range(SUBLANES):
        h = jnp.concatenate([u_sk[c, s * (seg + 1):s * (seg + 1) + seg, :] for c in range(D // LANES)], axis=1)
        y = h * _gelu_tanh(g_ref[s * seg:(s + 1) * seg, :])
        y_ref[s * seg:(s + 1) * seg, :] = _rms_norm(y, ng_ref[...]).astype(BF16)


def _lru_call(xg, p, tl):
    nb, L, _ = xg.shape
    vec = lambda b, l: (0, 0)
    blk = lambda b, l: (0, 0, 0)
    return pl.pallas_call(
        functools.partial(_lru_kernel, tl=tl),
        out_shape=[jax.ShapeDtypeStruct((nb, L, LRU_W), BF16),
                   jax.ShapeDtypeStruct((nb, 1, LRU_W), F32)],
        grid=(nb, L // tl),
        in_specs=[pl.BlockSpec((None, tl, LRU_W), lambda b, l: (b, l, 0)),
                  pl.BlockSpec((None, tl, LRU_W), lambda b, l: (b, l, 1)),
                  pl.BlockSpec((LRU_W // LRU_PACK_W, LRU_PACK_W, LRU_PACK_W), blk),
                  pl.BlockSpec((LRU_W // LRU_PACK_W, LRU_PACK_W, LRU_PACK_W), blk),
                  pl.BlockSpec((1, LRU_W), vec), pl.BlockSpec((1, LRU_W), vec),
                  pl.BlockSpec((1, LRU_W), vec), pl.BlockSpec((1, LRU_W), vec)],
        out_specs=[pl.BlockSpec((None, tl, LRU_W), lambda b, l: (b, l, 0)),
                   pl.BlockSpec((None, 1, LRU_W), lambda b, l: (b, 0, 0))],
        scratch_shapes=[pltpu.VMEM((LRU_W // LANES, tl + SUBLANES, LANES), F32),
                        pltpu.VMEM((LRU_W // LANES, tl + SUBLANES, LANES), F32),
                        pltpu.VMEM((1, LRU_W), F32)],
        compiler_params=_cparams(("parallel", "arbitrary")),
        name="lru",
    )(xg, xg, p["wa"], p["wx"], p["ba"], p["bx"], p["lam"], p["ng"])


def _ssd_kernel(xc_ref, z_blk, dt_blk, dtb_ref, alog_ref, dexp_ref, ng_ref,
                y_blk, st_ref, st_scr, *tmp, chunks):
    @pl.when(pl.program_id(1) == 0)
    def _():
        st_scr[...] = jnp.zeros((SSD_N, SSD_INNER), F32)

    sets = (tmp[0:3], tmp[3:6])
    for c in range(chunks):
        rows = pl.ds(c * SSD_Q, SSD_Q)
        _ssd_chunk(xc_ref.at[rows, :], z_blk.at[rows, :], dt_blk.at[rows, :], dtb_ref, alog_ref, dexp_ref, ng_ref,
                   y_blk.at[rows, :], st_scr, *sets[c % 2])

    @pl.when(pl.program_id(1) == pl.num_programs(1) - 1)
    def _():
        st_ref[...] = st_scr[...].T


def _ssd_chunk(xc_scr, z_ref, dt_ref, dtb_ref, alog_ref, dexp_ref, ng_ref, y_ref, st_scr, y_scr, ea_scr, ew_scr):
    q = SSD_Q
    dt = _softplus(dt_ref[...] + dtb_ref[...])
    da = dt * (-jnp.exp(alog_ref[...]))
    ri = lax.broadcasted_iota(I32, (q, q), 0)
    ci = lax.broadcasted_iota(I32, (q, q), 1)
    causal = ri >= ci
    acs = _dot(causal.astype(F32), da, precision=HIGHEST)
    acs_t = acs.T
    dt_t = dt.T
    w_end = dt * jnp.exp(acs[q - 1:q, :] - acs)

    half = SSD_INNER // SSD_GROUPS
    hpg = SSD_HEADS // SSD_GROUPS
    cgb, cbs = [], []
    for g in range(SSD_GROUPS):
        bgb = xc_scr[:, SSD_INNER + g * SSD_N:SSD_INNER + (g + 1) * SSD_N].astype(BF16)
        cgb.append(xc_scr[:, SSD_INNER + (SSD_GROUPS + g) * SSD_N:SSD_INNER + (SSD_GROUPS + g + 1) * SSD_N].astype(BF16))
        cbs.append(jnp.where(causal, _dot_nt(cgb[g], bgb), 0.0))

    low = lax.broadcasted_iota(I32, (q, LANES), 1) < SSD_P
    for k in range(SSD_HEADS // 2):
        cs = slice(k * LANES, (k + 1) * LANES)
        cb = cbs[(2 * k) // hpg]
        m, colb, wb = [], [], []
        for h in (2 * k, 2 * k + 1):
            colb.append(jnp.broadcast_to(acs[:, h:h + 1], (q, LANES)))
            wb.append(jnp.broadcast_to(w_end[:, h:h + 1], (q, LANES)))
            seg = colb[-1] - acs_t[h:h + 1, :]
            decay = jnp.exp(jnp.minimum(seg, 0.0))
            m.append((cb * (decay * dt_t[h:h + 1, :])).astype(BF16))
        ea_scr[:, cs] = jnp.where(low, colb[0], colb[1])
        ew_scr[:, cs] = jnp.where(low, wb[0], wb[1])
        xk = xc_scr[:, cs]
        zero = jnp.zeros_like(xk)
        rhs = jnp.concatenate([jnp.where(low, xk, zero), jnp.where(low, zero, xk)], axis=0).astype(BF16)
        y_scr[:, cs] = _dot(jnp.concatenate(m, axis=1), rhs)

    x = xc_scr[:, 0:SSD_INNER]
    ea = ea_scr[...]
    w = (x * ew_scr[...]).astype(BF16)
    s_in = st_scr[...]
    sb = s_in.astype(BF16)
    y_off, c_state = [], []
    for g in range(SSD_GROUPS):
        gs = slice(g * half, (g + 1) * half)
        bgt = xc_scr[:, SSD_INNER + g * SSD_N:SSD_INNER + (g + 1) * SSD_N].T.astype(BF16)
        y_off.append(_dot(cgb[g], sb[:, gs]))
        c_state.append(_dot(bgt, w[:, gs]))
    st_scr[...] = jnp.exp(ea[q - 1:q, :]) * s_in + jnp.concatenate(c_state, axis=1)
    y = y_scr[...] + jnp.concatenate(y_off, axis=1) * jnp.exp(ea) + dexp_ref[...] * x

    yz = y * _silu(z_ref[...])
    y_ref[...] = _rms_norm(yz, ng_ref[...]).astype(BF16)


def _ssd_call(xbc, z, dtr, p, chunks):
    nb, L, _ = xbc.shape
    q = SSD_Q
    tl = q * chunks
    vec = lambda b, c: (0, 0)
    row = lambda b, c: (b, c, 0)
    return pl.pallas_call(
        functools.partial(_ssd_kernel, chunks=chunks),
        out_shape=[jax.ShapeDtypeStruct((nb, L, SSD_INNER), BF16),
                   jax.ShapeDtypeStruct((nb, SSD_INNER, SSD_N), F32)],
        grid=(nb, L // tl),
        in_specs=[pl.BlockSpec((None, tl, SSD_CONV_DIM), row), pl.BlockSpec((None, tl, SSD_INNER), row),
                  pl.BlockSpec((None, tl, LANES), row),
                  pl.BlockSpec((1, LANES), vec), pl.BlockSpec((1, LANES), vec),
                  pl.BlockSpec((1, SSD_INNER), vec), pl.BlockSpec((1, SSD_INNER), vec)],
        out_specs=[pl.BlockSpec((None, tl, SSD_INNER), row),
                   pl.BlockSpec((None, SSD_INNER, SSD_N), lambda b, c: (b, 0, 0))],
        scratch_shapes=[pltpu.VMEM((SSD_N, SSD_INNER), F32)] + [pltpu.VMEM((q, SSD_INNER), F32)] * 6,
        compiler_params=_cparams(("parallel", "arbitrary")),
        name="ssd",
    )(xbc, z, dtr, p["dtb"], p["alog"], p["dexp"], p["ng"])


def _srow_kernel(xg_ref, xbc_ref, dte_ref, lconv_ref, h0_ref, sconv_ref,
                 lcw_ref, lcb_ref, wa_ref, wx_ref, ba_ref, bx_ref, lam_ref, lng_ref,
                 scw_ref, scb_ref, dtbe_ref, aloge_ref,
                 yl_ref, h_ref, xs_ref, bm_ref, cm_ref, xdtt_ref, dect_ref, lcn_ref, scn_ref, xc_scr):
    xl = xg_ref[:, 0:LRU_W]
    acc = lcb_ref[...] + xl * lcw_ref[CONV_W - 1:CONV_W, :]
    for k in range(CONV_W - 1):
        acc = acc + lconv_ref[k] * lcw_ref[k:k + 1, :]
    xc_scr[...] = acc
    for k in range(CONV_W - 2):
        lcn_ref[k] = lconv_ref[k + 1]
        scn_ref[k] = sconv_ref[k + 1]
    lcn_ref[CONV_W - 2] = xl
    scn_ref[CONV_W - 2] = xbc_ref[...]
    sp = _softplus(-lam_ref[...])
    for j in range(LRU_W // LRU_PACK_W):
        cs = slice(j * LRU_PACK_W, (j + 1) * LRU_PACK_W)
        xc = xc_scr[:, cs]
        a, u = _lru_gates(xc, xc.astype(BF16), wa_ref[j], wx_ref[j], ba_ref[:, cs], bx_ref[:, cs], sp[:, cs])
        h_ref[:, cs] = a * h0_ref[:, cs] + u
    y = h_ref[...] * _gelu_tanh(xg_ref[:, LRU_W:2 * LRU_W])
    yl_ref[...] = _rms_norm(y, lng_ref[...]).astype(BF16)

    acc = scb_ref[...] + xbc_ref[...] * scw_ref[CONV_W - 1:CONV_W, :]
    for k in range(CONV_W - 1):
        acc = acc + sconv_ref[k] * scw_ref[k:k + 1, :]
    xc = _silu(acc)
    xs = xc[:, 0:SSD_INNER]
    xs_ref[...] = xs
    bm_ref[...] = xc[:, SSD_INNER:SSD_INNER + SSD_GROUPS * SSD_N]
    cm_ref[...] = xc[:, SSD_INNER + SSD_GROUPS * SSD_N:]
    dt = _softplus(dte_ref[...] + dtbe_ref[...])
    dec = jnp.exp(dt * (-jnp.exp(aloge_ref[...])))
    xdtt = (xs * dt).T
    dect = dec.T
    for j in range(xdtt_ref.shape[0]):
        xdtt_ref[j] = xdtt[:, j * SUBLANES:(j + 1) * SUBLANES]
        dect_ref[j] = dect[:, j * SUBLANES:(j + 1) * SUBLANES]


def _srow_call(xg, xbc, dte, lconv, h0, sconv, lp, sp):
    n = xg.shape[0]
    args = (xg, xbc, dte, lconv, h0, sconv, lp["cw"], lp["cb"], lp["wa"], lp["wx"], lp["ba"], lp["bx"],
            lp["lam"], lp["ng"], sp["cw"], sp["cb"], sp["dtbe"], sp["aloge"])
    full = lambda a: pl.BlockSpec(a.shape, lambda i, nd=a.ndim: (0,) * nd)
    outs = [jax.ShapeDtypeStruct((n, LRU_W), BF16), jax.ShapeDtypeStruct((n, LRU_W), F32),
            jax.ShapeDtypeStruct((n, SSD_INNER), F32), jax.ShapeDtypeStruct((n, SSD_GROUPS * SSD_N), F32),
            jax.ShapeDtypeStruct((n, SSD_GROUPS * SSD_N), F32),
            jax.ShapeDtypeStruct((n // SUBLANES, SSD_INNER, SUBLANES), F32),
            jax.ShapeDtypeStruct((n // SUBLANES, SSD_INNER, SUBLANES), F32),
            jax.ShapeDtypeStruct((CONV_W - 1, n, LRU_W), F32), jax.ShapeDtypeStruct((CONV_W - 1, n, SSD_CONV_DIM), F32)]
    return pl.pallas_call(
        _srow_kernel,
        out_shape=outs,
        grid=(1,),
        in_specs=[full(a) for a in args],
        out_specs=[pl.BlockSpec(o.shape, lambda i, nd=len(o.shape): (0,) * nd) for o in outs],
        scratch_shapes=[pltpu.VMEM((n, LRU_W), F32)],
        compiler_params=_cparams(("arbitrary",)),
        name="srow",
    )(*args)


def _sstate_kernel(s0_ref, xq_ref, dq_ref, bm_ref, cm_ref, xs_ref, z_ref, dexp_ref, ng_ref,
                   sn_ref, ys_ref, yraw):
    nb = s0_ref.shape[0]
    half = SSD_INNER // SSD_GROUPS
    for bi in range(nb):
        brow = jnp.concatenate(
            [jnp.broadcast_to(bm_ref[bi:bi + 1, g * SSD_N:(g + 1) * SSD_N], (half, SSD_N)) for g in range(SSD_GROUPS)],
            axis=0)
        dec = jnp.concatenate(
            [jnp.broadcast_to(dq_ref[h * SSD_P:h * SSD_P + 1, bi:bi + 1], (SSD_P, SSD_N)) for h in range(SSD_HEADS)],
            axis=0)
        s = dec * s0_ref[bi] + xq_ref[:, bi:bi + 1] * brow
        sn_ref[bi] = s
        sb = s.astype(BF16)
        for g in range(SSD_GROUPS):
            cg = cm_ref[:, g * SSD_N:(g + 1) * SSD_N].astype(BF16)
            res = _dot_nt(cg, sb[g * half:(g + 1) * half, :])
            yraw[bi:bi + 1, g * half:(g + 1) * half] = res[bi:bi + 1, :]
    y = yraw[...] + dexp_ref[...] * xs_ref[...]
    ys_ref[...] = _rms_norm(y * _silu(z_ref[...]), ng_ref[...]).astype(BF16)


def _sstate_call(s0, xq, dq, bm, cm, xs, z, dexp, ng, nb):
    n = s0.shape[0]
    row = lambda i: (i, 0)
    vec = lambda i: (0, 0)
    gn = SSD_GROUPS * SSD_N
    return pl.pallas_call(
        _sstate_kernel,
        out_shape=[jax.ShapeDtypeStruct(s0.shape, F32), jax.ShapeDtypeStruct((n, SSD_INNER), BF16)],
        grid=(n // nb,),
        in_specs=[pl.BlockSpec((nb, SSD_INNER, SSD_N), lambda i: (i, 0, 0)),
                  pl.BlockSpec((None, SSD_INNER, nb), lambda i: (i, 0, 0)),
                  pl.BlockSpec((None, SSD_INNER, nb), lambda i: (i, 0, 0)),
                  pl.BlockSpec((nb, gn), row), pl.BlockSpec((nb, gn), row),
                  pl.BlockSpec((nb, SSD_INNER), row), pl.BlockSpec((nb, SSD_INNER), row),
                  pl.BlockSpec((1, SSD_INNER), vec), pl.BlockSpec((1, SSD_INNER), vec)],
        out_specs=[pl.BlockSpec((nb, SSD_INNER, SSD_N), lambda i: (i, 0, 0)), pl.BlockSpec((nb, SSD_INNER), row)],
        scratch_shapes=[pltpu.VMEM((nb, SSD_INNER), F32)],
        compiler_params=_cparams(("parallel",)),
        name="sstate",
    )(s0, xq, dq, bm, cm, xs, z, dexp, ng)


def _out_proj(yl_ref, ys_ref, wo_ref):
    return _dot(yl_ref[...], wo_ref[0:LRU_W, :]) + _dot(ys_ref[...], wo_ref[LRU_W:LRU_W + SSD_INNER, :])


def _post_kernel(yl_ref, ys_ref, x_ref, g1_ref, sh2_ref, sc2_ref, wo_ref, l1g_ref, l1b_ref, wrt_ref, brc_ref,
                 x1_ref, v_ref, eid_ref, gw_ref, *, per_batch):
    _post_tail(_out_proj(yl_ref, ys_ref, wo_ref), pl.program_id(0), x_ref, g1_ref, sh2_ref, sc2_ref,
               l1g_ref, l1b_ref, wrt_ref, brc_ref, x1_ref, v_ref, eid_ref, gw_ref, per_batch)


def _post_lag_kernel(yl_ref, ys_ref, x_ref, g1_ref, sh2_ref, sc2_ref, wo_ref, l1g_ref, l1b_ref, wrt_ref, brc_ref,
                     x1_ref, v_ref, eid_ref, gw_ref, oa, ob, *, nl):
    s = pl.program_id(0)
    b_lag = jnp.maximum(s - 1, 0) // nl

    @pl.when(s == 0)
    def _():
        ob[...] = jnp.zeros(ob.shape, F32)

    def step(o_new, o_old):
        o_new[...] = _out_proj(yl_ref, ys_ref, wo_ref)
        _post_tail(o_old[...], b_lag, x_ref, g1_ref, sh2_ref, sc2_ref, l1g_ref, l1b_ref, wrt_ref, brc_ref,
                   x1_ref, v_ref, eid_ref, gw_ref, True)

    pl.when(s % 2 == 0)(functools.partial(step, oa, ob))
    pl.when(s % 2 == 1)(functools.partial(step, ob, oa))


def _post_tail(o, b, x_ref, g1_ref, sh2_ref, sc2_ref, l1g_ref, l1b_ref, wrt_ref, brc_ref,
               x1_ref, v_ref, eid_ref, gw_ref, per_batch):
    x1 = _layer_norm(ALPHA * x_ref[...] + _mod_rows(g1_ref, per_batch, b) * o) * l1g_ref[...] + l1b_ref[...]
    x1_ref[...] = x1
    v = _layer_norm(x1) * (1.0 + _mod_rows(sc2_ref, per_batch, b)) + _mod_rows(sh2_ref, per_batch, b)
    _store_row_tiles(v_ref, v)

    lt = _dot(v.astype(BF16), wrt_ref[...]).T + brc_ref[...]
    tl = lt.shape[1]
    row = lax.broadcasted_iota(I32, (GROUP_SIZE, tl), 0).astype(F32)
    big = float(GROUP_SIZE)
    neg = -jnp.inf
    lg = jnp.where(row < N_GROUPS, lt[0:GROUP_SIZE, :], neg)
    gmax = jnp.max(lg, axis=0, keepdims=True)
    gsel = jnp.min(jnp.where(lg == gmax, row, big), axis=0, keepdims=True)
    pg = 1.0 / jnp.sum(jnp.exp(lg - gmax), axis=0, keepdims=True)
    le = lt[GROUP_SIZE:2 * GROUP_SIZE, :]
    for j in range(1, N_GROUPS):
        le = jnp.where(gsel == j, lt[GROUP_SIZE * (j + 1):GROUP_SIZE * (j + 2), :], le)
    m1 = jnp.max(le, axis=0, keepdims=True)
    i1 = jnp.min(jnp.where(le == m1, row, big), axis=0, keepdims=True)
    rest = jnp.where(row == i1, neg, le)
    m2 = jnp.max(rest, axis=0, keepdims=True)
    i2 = jnp.min(jnp.where(rest == m2, row, big), axis=0, keepdims=True)
    e2 = jnp.exp(m2 - m1)
    den = 1.0 + e2
    eid = jnp.where(row == 0, gsel * GROUP_SIZE + i1, jnp.where(row == 1, gsel * GROUP_SIZE + i2, 0.0))
    eid_ref[...] = eid.astype(I32)
    gw_ref[...] = jnp.where(row == 0, pg * (1.0 / den), jnp.where(row == 1, pg * (e2 / den), 0.0))


def _post_call(yl, ys, x3, mod, p, tl):
    nb, L, _ = x3.shape
    nl = L // tl
    row = lambda b, l: (b, l, 0)
    vec = lambda b, l: (0, 0)
    tok = lambda b, l: (b * nl + l, 0)
    tokt = lambda b, l: (0, b * nl + l)
    bof, tof = (lambda b, l: b), (lambda b, l: l)
    return pl.pallas_call(
        functools.partial(_post_kernel, per_batch=mod.per_batch),
        out_shape=[jax.ShapeDtypeStruct((nb, L, D), F32), jax.ShapeDtypeStruct(_tile_rows(nb * L), F32),
                   jax.ShapeDtypeStruct((SUBLANES, nb * L), I32), jax.ShapeDtypeStruct((SUBLANES, nb * L), F32)],
        grid=(nb, nl),
        in_specs=[pl.BlockSpec((None, tl, LRU_W), row), pl.BlockSpec((None, tl, SSD_INNER), row),
                  pl.BlockSpec((None, tl, D), row),
                  mod.spec(2, tl, bof, tof), mod.spec(3, tl, bof, tof), mod.spec(4, tl, bof, tof),
                  pl.BlockSpec((LRU_W + SSD_INNER, D), vec), pl.BlockSpec((1, D), vec), pl.BlockSpec((1, D), vec),
                  pl.BlockSpec((D, LANES), vec), pl.BlockSpec((LANES, 1), vec)],
        out_specs=[pl.BlockSpec((None, tl, D), row), pl.BlockSpec(_tile_rows(tl), tok),
                   pl.BlockSpec((SUBLANES, tl), tokt), pl.BlockSpec((SUBLANES, tl), tokt)],
        compiler_params=_cparams(("parallel", "arbitrary")),
        name="post",
    )(yl, ys, x3, mod.table, mod.table, mod.table, p["wo"], p["l1g"], p["l1b"], p["wrt"], p["brc"])


def _post_lag_call(yl, ys, x3, mod, p, tl):
    nb, L, _ = x3.shape
    nl = L // tl
    n = nb * nl
    cur = lambda s: (jnp.minimum(s, n - 1) // nl, jnp.minimum(s, n - 1) % nl, 0)
    lag = lambda s: (jnp.maximum(s - 1, 0) // nl, jnp.maximum(s - 1, 0) % nl, 0)
    lag_tok = lambda s: (jnp.maximum(s - 1, 0), 0)
    lag_tokt = lambda s: (0, jnp.maximum(s - 1, 0))
    vec = lambda s: (0, 0)
    one = lambda *g: 0
    return pl.pallas_call(
        functools.partial(_post_lag_kernel, nl=nl),
        out_shape=[jax.ShapeDtypeStruct((nb, L, D), F32), jax.ShapeDtypeStruct(_tile_rows(nb * L), F32),
                   jax.ShapeDtypeStruct((SUBLANES, nb * L), I32), jax.ShapeDtypeStruct((SUBLANES, nb * L), F32)],
        grid=(n + 1,),
        in_specs=[pl.BlockSpec((None, tl, LRU_W), cur), pl.BlockSpec((None, tl, SSD_INNER), cur),
                  pl.BlockSpec((None, tl, D), lag),
                  mod.spec(2, tl, one, one), mod.spec(3, tl, one, one), mod.spec(4, tl, one, one),
                  pl.BlockSpec((LRU_W + SSD_INNER, D), vec), pl.BlockSpec((1, D), vec), pl.BlockSpec((1, D), vec),
                  pl.BlockSpec((D, LANES), vec), pl.BlockSpec((LANES, 1), vec)],
        out_specs=[pl.BlockSpec((None, tl, D), lag), pl.BlockSpec(_tile_rows(tl), lag_tok),
                   pl.BlockSpec((SUBLANES, tl), lag_tokt), pl.BlockSpec((SUBLANES, tl), lag_tokt)],
        scratch_shapes=[pltpu.VMEM((tl, D), F32), pltpu.VMEM((tl, D), F32)],
        compiler_params=_cparams(("arbitrary",)),
        name="postl",
    )(yl, ys, x3, mod.table, mod.table, mod.table, p["wo"], p["l1g"], p["l1b"], p["wrt"], p["brc"])


def _route_kernel(eidp_ref, eids_ref, dest_ref, cnt_ref, run, poff, *, p_tiles):
    ph = pl.program_id(0)
    t = pl.program_id(1)
    n = ROUTE_TILE
    eid = jnp.where(t < p_tiles, eidp_ref[...], eids_ref[...])
    rowi = lax.broadcasted_iota(I32, (LANES, n), 0)
    oh0 = rowi == eid[0:1, :]
    oh1 = rowi == eid[1:2, :]
    oh = oh0.astype(F32) + oh1.astype(F32)
    tile_cnt = jnp.sum(oh, axis=1, keepdims=True)

    @pl.when((ph == 0) & (t == 0))
    def _():
        run[...] = jnp.zeros((LANES, n), F32)

    @pl.when(ph == 0)
    def _():
        run[...] = run[...] + tile_cnt

    @pl.when((ph == 1) & (t == 0))
    def _():
        counts = run[...]
        cnt_ref[...] = counts
        ci = counts.astype(I32)
        q = jnp.floor(counts * (1.0 / MOE_BM)).astype(I32)
        rem = ci - q * MOE_BM
        q = q + jnp.where(rem >= MOE_BM, 1, 0) - jnp.where(rem < 0, 1, 0)
        nblk = q + jnp.where(ci - q * MOE_BM > 0, 1, 0)
        r = lax.broadcasted_iota(I32, (LANES, LANES), 0)
        c = lax.broadcasted_iota(I32, (LANES, LANES), 1)
        lower = (r > c).astype(BF16)
        poff[...] = _dot(lower, nblk.astype(F32).astype(BF16)) * float(MOE_BM)
        run[...] = jnp.zeros((LANES, n), F32)

    @pl.when(ph == 1)
    def _():
        r = lax.broadcasted_iota(I32, (n, n), 0)
        c = lax.broadcasted_iota(I32, (n, n), 1)
        before = (r < c).astype(BF16)
        slot = _dot(oh.astype(BF16), before) + run[...] + poff[...]
        d0 = jnp.sum(jnp.where(oh0, slot, 0.0), axis=0, keepdims=True)
        d1 = jnp.sum(jnp.where(oh1, slot, 0.0), axis=0, keepdims=True)
        row = lax.broadcasted_iota(I32, (SUBLANES, n), 0)
        dest_ref[...] = jnp.where(row == 0, d0, jnp.where(row == 1, d1, 0.0)).astype(I32)
        run[...] = run[...] + tile_cnt


def _route_call(eid_p, eid_s):
    n = ROUTE_TILE
    p_tiles = eid_p.shape[1] // n
    s_tiles = eid_s.shape[1] // n
    t_all = (p_tiles + s_tiles) * n
    return pl.pallas_call(
        functools.partial(_route_kernel, p_tiles=p_tiles),
        out_shape=[jax.ShapeDtypeStruct((SUBLANES, t_all), I32), jax.ShapeDtypeStruct((LANES, n), F32)],
        grid=(2, p_tiles + s_tiles),
        in_specs=[pl.BlockSpec((SUBLANES, n), lambda ph, t: (0, jnp.minimum(t, p_tiles - 1))),
                  pl.BlockSpec((SUBLANES, n), lambda ph, t: (0, jnp.maximum(t - p_tiles, 0)))],
        out_specs=[pl.BlockSpec((SUBLANES, n), lambda ph, t: (0, t * ph)),
                   pl.BlockSpec((LANES, n), lambda ph, t: (0, 0))],
        scratch_shapes=[pltpu.VMEM((LANES, n), F32), pltpu.VMEM((LANES, n), F32)],
        compiler_params=_cparams(("arbitrary", "arbitrary")),
        name="route",
    )(eid_p, eid_s)


def _row_copy(src, src_row, dst, dst_row, sem):
    s0 = pl.multiple_of(src_row * SUBLANES, SUBLANES)
    d0 = pl.multiple_of(dst_row * SUBLANES, SUBLANES)
    return pltpu.make_async_copy(src.at[pl.ds(s0, SUBLANES), :], dst.at[pl.ds(d0, SUBLANES), :], sem)


def _scatter_rows(dest_ref, v_ref, xpad_ref, sem, tl):
    def start(r, carry):
        for k in range(2):
            _row_copy(v_ref, r, xpad_ref, dest_ref[k, r], sem).start(priority=k)
        return carry

    def wait(r, carry):
        for k in range(2):
            _row_copy(v_ref, r, xpad_ref, dest_ref[k, r], sem).wait()
        return carry

    lax.fori_loop(0, tl, start, 0, unroll=DMA_UNROLL)
    lax.fori_loop(0, tl, wait, 0, unroll=DMA_UNROLL)


def _dispatch_kernel(zflag_ref, destp_ref, dests_ref, vp_ref, vs_ref, xpad_ref, zbuf, sem, zsem,
                     *, tl_p, tl_s, p_tiles, n_blocks):
    blk_rows = _tile_rows(MOE_BM)[0]

    @pl.when(pl.program_id(0) == 0)
    def _():
        zbuf[...] = jnp.zeros(zbuf.shape, F32)

        def zero_block(go):
            def body(b, carry):
                @pl.when(zflag_ref[b] != 0)
                def _():
                    r0 = pl.multiple_of(b * blk_rows, blk_rows)
                    cp = pltpu.make_async_copy(zbuf, xpad_ref.at[pl.ds(r0, blk_rows), :], zsem)
                    if go:
                        cp.start()
                    else:
                        cp.wait()
                return carry
            lax.fori_loop(0, n_blocks, body, 0)

        zero_block(True)
        zero_block(False)

    @pl.when(pl.program_id(0) < p_tiles)
    def _():
        _scatter_rows(destp_ref, vp_ref, xpad_ref, sem, tl_p)

    @pl.when(pl.program_id(0) >= p_tiles)
    def _():
        _scatter_rows(dests_ref, vs_ref, xpad_ref, sem, tl_s)


def _dispatch_call(zflag, dest, v_p, v_s, n_blocks, tl_p, tl_s):
    p_tiles = v_p.shape[0] // _tile_rows(tl_p)[0]
    s_tiles = v_s.shape[0] // _tile_rows(tl_s)[0]
    s_off = p_tiles * tl_p // tl_s
    return pl.pallas_call(
        functools.partial(_dispatch_kernel, tl_p=tl_p, tl_s=tl_s, p_tiles=p_tiles, n_blocks=n_blocks),
        out_shape=jax.ShapeDtypeStruct(_tile_rows(n_blocks * MOE_BM), F32),
        grid_spec=pltpu.PrefetchScalarGridSpec(
            num_scalar_prefetch=1,
            grid=(p_tiles + s_tiles,),
            in_specs=[pl.BlockSpec((SUBLANES, tl_p), lambda t, z: (0, jnp.minimum(t, p_tiles - 1)),
                                   memory_space=pltpu.SMEM),
                      pl.BlockSpec((SUBLANES, tl_s), lambda t, z: (0, s_off + jnp.maximum(t - p_tiles, 0)),
                                   memory_space=pltpu.SMEM),
                      pl.BlockSpec(_tile_rows(tl_p), lambda t, z: (jnp.minimum(t, p_tiles - 1), 0)),
                      pl.BlockSpec(_tile_rows(tl_s), lambda t, z: (jnp.maximum(t - p_tiles, 0), 0))],
            out_specs=pl.BlockSpec(memory_space=pl.ANY),
            scratch_shapes=[pltpu.VMEM(_tile_rows(MOE_BM), F32), pltpu.SemaphoreType.DMA, pltpu.SemaphoreType.DMA]),
        compiler_params=_cparams(("arbitrary",)),
        name="dispatch",
    )(zflag, dest, dest, v_p, v_s)


def _expert_kernel(be_ref, nb_ref, nxt_ref, slot_ref, x_ref, wg_hbm, wu_hbm, wd_hbm, o_ref,
                   wgf, wuf, wdf, wgb, wub, wdb, sems):
    i = pl.program_id(0)

    def weights(e, slot, go):
        for m, (hbm, buf) in enumerate(((wg_hbm, wgf), (wu_hbm, wuf), (wd_hbm, wdf))):
            cp = pltpu.make_async_copy(hbm.at[e], buf.at[slot], sems.at[slot, m])
            if go:
                cp.start()
            else:
                cp.wait()

    @pl.when(i < nb_ref[0])
    def _():
        e = be_ref[i]
        slot = slot_ref[e]

        @pl.when(i == 0)
        def _():
            weights(e, slot, True)

        @pl.when((i == 0) | (e != be_ref[jnp.maximum(i - 1, 0)]))
        def _():
            weights(e, slot, False)
            nxt = nxt_ref[e]

            @pl.when(nxt >= 0)
            def _():
                weights(nxt, 1 - slot, True)

            wgb[...] = wgf[slot].astype(BF16)
            wub[...] = wuf[slot].astype(BF16)
            wdb[...] = wdf[slot].astype(BF16)

        x = _load_row_tiles(x_ref).astype(BF16)
        h = _silu(_dot(x, wgb[...])) * _dot(x, wub[...])
        _store_row_tiles(o_ref, _dot(h.astype(BF16), wdb[...]))


def _expert_call(blk_e, nblk, nxt_e, slot_e, xpad, w_gate, w_up, w_down):
    n_rows = xpad.shape[0] // SUBLANES
    blk = lambda i, be, nb, nx, sl: (jnp.minimum(i, nb[0] - 1), 0)
    hbm = pl.BlockSpec(memory_space=pl.ANY)
    return pl.pallas_call(
        _expert_kernel,
        out_shape=jax.ShapeDtypeStruct(_tile_rows(n_rows), F32),
        grid_spec=pltpu.PrefetchScalarGridSpec(
            num_scalar_prefetch=4,
            grid=(n_rows // MOE_BM,),
            in_specs=[pl.BlockSpec(_tile_rows(MOE_BM), blk), hbm, hbm, hbm],
            out_specs=pl.BlockSpec(_tile_rows(MOE_BM), blk),
            scratch_shapes=[pltpu.VMEM((2, D, D_FF), F32), pltpu.VMEM((2, D, D_FF), F32), pltpu.VMEM((2, D_FF, D), F32),
                            pltpu.VMEM((D, D_FF), BF16), pltpu.VMEM((D, D_FF), BF16), pltpu.VMEM((D_FF, D), BF16),
                            pltpu.SemaphoreType.DMA((2, 3))]),
        input_output_aliases={4: 0},
        compiler_params=_cparams(("arbitrary",)),
        name="expert",
    )(blk_e, nblk, nxt_e, slot_e, xpad, w_gate, w_up, w_down)


def _combine_kernel(dest_ref, gw_ref, x1_ref, g2_ref, l2g_ref, l2b_ref, ypad_ref, y_ref, ybuf, sems,
                    *, tok_off, k_stride, tl, nl, per_batch):
    i = pl.program_id(0)

    def gather(tile, slot, go):
        base = tok_off + tile * tl

        def body(r, carry):
            for k in range(2):
                cp = _row_copy(ypad_ref, dest_ref[k * k_stride + base + r], ybuf.at[slot, k], r, sems.at[slot])
                if go:
                    cp.start(priority=k)
                else:
                    cp.wait()
            return carry

        lax.fori_loop(0, tl, body, 0, unroll=DMA_UNROLL)

    @pl.when(i == 0)
    def _():
        gather(0, 0, True)

    @pl.when(i + 1 < pl.num_programs(0))
    def _():
        gather(i + 1, (i + 1) % 2, True)

    slot = i % 2
    gather(i, slot, False)
    gw_t = jnp.concatenate([gw_ref[...], jnp.zeros((LANES - SUBLANES, tl), F32)], axis=0).T
    f = _load_row_tiles(ybuf.at[slot, 0]) * gw_t[:, 0:1] + _load_row_tiles(ybuf.at[slot, 1]) * gw_t[:, 1:2]
    g2 = _mod_rows(g2_ref, per_batch, i // nl)
    y_ref[...] = _layer_norm(ALPHA * x1_ref[...] + g2 * f) * l2g_ref[...] + l2b_ref[...]


def _combine_call(dest, gw, x1, mod, l2g, l2b, ypad, tok_off, tl):
    nb, L, _ = x1.shape
    nl = L // tl
    row = lambda i, d: (i // nl, i % nl, 0)
    vec = lambda i, d: (0, 0)
    g2_spec = mod.spec(5, tl, lambda i, d: i // nl, lambda i, d: i % nl)
    return pl.pallas_call(
        functools.partial(_combine_kernel, tok_off=tok_off, k_stride=dest.shape[1], tl=tl, nl=nl,
                          per_batch=mod.per_batch),
        out_shape=jax.ShapeDtypeStruct((nb, L, D), F32),
        grid_spec=pltpu.PrefetchScalarGridSpec(
            num_scalar_prefetch=1,
            grid=(nb * nl,),
            in_specs=[pl.BlockSpec((SUBLANES, tl), lambda i, d: (0, i)),
                      pl.BlockSpec((None, tl, D), row), g2_spec,
                      pl.BlockSpec((1, D), vec), pl.BlockSpec((1, D), vec),
                      pl.BlockSpec(memory_space=pl.ANY)],
            out_specs=pl.BlockSpec((None, tl, D), row),
            scratch_shapes=[pltpu.VMEM((2, 2) + _tile_rows(tl), F32), pltpu.SemaphoreType.DMA((2,))]),
        compiler_params=_cparams(("arbitrary",)),
        name="combine",
    )(dest[:2].reshape(-1), gw, x1, mod.table, l2g, l2b, ypad)


def _block_diag(w):
    nh, blk, _ = w.shape
    w4 = w.reshape(nh // LRU_PACK, LRU_PACK, blk, blk)
    eye = jnp.eye(LRU_PACK, dtype=w.dtype)
    out = jnp.einsum("gaij,ab->gaibj", w4, eye)
    return out.reshape(nh // LRU_PACK, LRU_PACK * blk, LRU_PACK * blk).astype(BF16)


def _row(v):
    return v.reshape(1, -1).astype(F32)


def _pad_lanes(v):
    return jnp.pad(v.reshape(1, -1).astype(F32), ((0, 0), (0, LANES - v.shape[-1])))


def kernel(x_prompt, x_sample, c_prompt, c_sample, state_lru_conv, state_lru_h, state_ssd_conv, state_ssd, w_ada, b_ada, w_in, lru_conv_w, lru_conv_b, lru_wa, lru_ba, lru_wx, lru_bx, lru_lambda, lru_norm_g, ssd_conv_w, ssd_conv_b, ssd_dt_bias, ssd_a_log, ssd_d, ssd_norm_g, w_out, ln1_g, ln1_b, w_rg, b_rg, w_re, b_re, w_gate, w_up, w_down, ln2_g, ln2_b):
    assert w_ada.shape[0] == DEPTH == 1
    nbp, seq, _ = x_prompt.shape
    nbs = x_sample.shape[0]
    t_p = nbp * seq
    t_all = t_p + nbs
    assert x_sample.shape[1] == 1 and t_p % ROUTE_TILE == 0 and nbs % TOK_TILE == 0 and nbp == SUBLANES

    c_rows = -(-(nbp + nbs) // 16) * 16
    c_all = jnp.pad(jnp.concatenate([c_sample, c_prompt], axis=0), ((0, c_rows - nbp - nbs), (0, 0)))
    w_in0 = w_in[0]
    o_z, o_xbc, o_dt = 2 * LRU_W, 2 * LRU_W + SSD_INNER, 2 * LRU_W + SSD_INNER + SSD_CONV_DIM
    w_in_t = jnp.swapaxes(w_in0, 0, 1)
    *ws_p, w_dt_t = _wcast_call(w_in_t, ((0, o_z), (o_z, o_xbc), (o_xbc, o_dt)), 512)
    w_dt = jnp.swapaxes(w_dt_t, 0, 1)
    ws_s = ws_p + [jnp.repeat(w_dt, SSD_P, axis=1)]
    ws_p = ws_p + [jnp.pad(w_dt, ((0, 0), (0, LANES - SSD_HEADS)))]
    lp = dict(cw=lru_conv_w[0], cb=_row(lru_conv_b[0]), wa=_block_diag(lru_wa[0]), wx=_block_diag(lru_wx[0]),
              ba=_row(lru_ba[0]), bx=_row(lru_bx[0]), lam=_row(lru_lambda[0]), ng=_row(lru_norm_g[0]))
    sp = dict(cw=ssd_conv_w[0], cb=_row(ssd_conv_b[0]), dtb=_pad_lanes(ssd_dt_bias[0]), alog=_pad_lanes(ssd_a_log[0]),
              dexp=_row(jnp.repeat(ssd_d[0], SSD_P)), ng=_row(ssd_norm_g[0]),
              dtbe=_row(jnp.repeat(ssd_dt_bias[0], SSD_P)), aloge=_row(jnp.repeat(ssd_a_log[0], SSD_P)))
    wrt = jnp.zeros((D, LANES), F32).at[:, 0:N_GROUPS].set(w_rg[0]).at[:, GROUP_SIZE:ROUTE_ROWS].set(w_re[0]).astype(BF16)
    brc = jnp.zeros((LANES, 1), F32).at[0:N_GROUPS, 0].set(b_rg[0]).at[GROUP_SIZE:ROUTE_ROWS, 0].set(b_re[0])
    pp = dict(wo=w_out[0].astype(BF16), l1g=_row(ln1_g[0]), l1b=_row(ln1_b[0]), wrt=wrt, brc=brc)

    table = _ada_call(c_all, w_ada[0], _row(b_ada[0]))
    mod_p = _Mod(table, nbs, True)
    mod_s = _Mod(table, 0, False)

    xg, z, xbc, dtr, p_lru_conv, p_ssd_conv = _proj_conv_call(x_prompt, mod_p, ws_p, lp, sp, SEQ_TILE)
    yl, p_lru_h = _lru_call(xg, lp, LRU_TILE)
    ys, p_ssd = _ssd_call(xbc, z, dtr, sp, SSD_CHUNKS_PER_STEP)
    x1_p, v_p, eid_p, gw_p = _post_lag_call(yl, ys, x_prompt, mod_p, pp, SEQ_TILE)

    xs3 = x_sample.reshape(1, nbs, D)
    xg_s, z_s, xbc_s, dte_s = _proj_call(xs3, mod_s, ws_s, nbs)
    xg_s, z_s, xbc_s, dte_s = xg_s[0], z_s[0], xbc_s[0], dte_s[0]
    yl_s, h_s, xs_s, bm_s, cm_s, xdtt, dect, s_lru_conv, s_ssd_conv = _srow_call(
        xg_s, xbc_s, dte_s, jnp.swapaxes(state_lru_conv[0], 0, 1), state_lru_h[0],
        jnp.swapaxes(state_ssd_conv[0], 0, 1), lp, sp)
    s_new, ys_s = _sstate_call(state_ssd[0].reshape(nbs, SSD_INNER, SSD_N), xdtt, dect,
                               bm_s, cm_s, xs_s, z_s, sp["dexp"], sp["ng"], SUBLANES)
    x1_s, v_s, eid_s, gw_s = _post_call(yl_s[None], ys_s[None], xs3, mod_s, pp, nbs)

    s_pad = -nbs % ROUTE_TILE
    dest, counts = _route_call(eid_p, jnp.pad(eid_s, ((0, 0), (0, s_pad)), constant_values=-1))
    counts = counts[:N_EXPERTS, 0].astype(I32)
    n_blocks = -(-(2 * t_all) // MOE_BM) + N_EXPERTS
    pend = jnp.cumsum(((counts + MOE_BM - 1) // MOE_BM) * MOE_BM)
    blk_start = jnp.arange(n_blocks, dtype=I32) * MOE_BM
    blk_e = jnp.minimum(jnp.sum((pend[None, :] <= blk_start[:, None]).astype(I32), axis=1), N_EXPERTS - 1)
    nblk = (pend[-1:] // MOE_BM).astype(I32)
    blk_ids = jnp.arange(n_blocks, dtype=I32)
    last_of_expert = jnp.any(((pend // MOE_BM - 1)[None, :] == blk_ids[:, None]) & (counts > 0)[None, :], axis=1)
    zflag = (last_of_expert | (blk_ids >= nblk[0])).astype(I32)
    used = counts > 0
    eids = jnp.arange(N_EXPERTS, dtype=I32)
    later_used = (eids[None, :] > eids[:, None]) & used[None, :]
    nxt_e = jnp.min(jnp.where(later_used, eids[None, :], N_EXPERTS), axis=1)
    nxt_e = jnp.where(nxt_e == N_EXPERTS, -1, nxt_e).astype(I32)
    slot_e = ((jnp.cumsum(used.astype(I32)) - used.astype(I32)) % 2).astype(I32)
    xpad = _dispatch_call(zflag, dest, v_p, v_s, n_blocks, 4 * TOK_TILE, TOK_TILE)
    ypad = _expert_call(blk_e, nblk, nxt_e, slot_e, xpad, w_gate[0], w_up[0], w_down[0])
    l2g, l2b = _row(ln2_g[0]), _row(ln2_b[0])
    y_p = _combine_call(dest, gw_p, x1_p, mod_p, l2g, l2b, ypad, 0, 4 * TOK_TILE)
    y_s = _combine_call(dest, gw_s, x1_s, mod_s, l2g, l2b, ypad, t_p, TOK_TILE)

    return (y_p, y_s.reshape(nbs, 1, D),
            p_lru_conv[None], p_lru_h.reshape(1, nbp, LRU_W), p_ssd_conv[None],
            p_ssd.reshape(1, nbp, SSD_HEADS, SSD_P, SSD_N),
            jnp.swapaxes(s_lru_conv, 0, 1)[None], h_s[None], jnp.swapaxes(s_ssd_conv, 0, 1)[None],
            s_new.reshape(1, nbs, SSD_HEADS, SSD_P, SSD_N))
```
